```python
import jax, jax.numpy as jnp
from jax import lax
import numpy as np

D_MODEL = 1024
BATCH = 32
SEQ = 2048
DEPTH = 2

N_MIXERS = 4
GROUP_W = D_MODEL // N_MIXERS
HEAD_DIM = 64
N_HEADS = GROUP_W // HEAD_DIM
NORM_EPS = 1e-6
RWKV_W_RANK = 32
RWKV_A_RANK = 32
RWKV_G_RANK = 64
RWKV_GN_EPS = 64e-5
SB_BLOCK = 128
ML_CHUNK = 64
ML_CONV = 4
GATE_CAP = 15.0
DSA_BLOCK = 128
IDX_HEADS = 4
IDX_DIM = 32
TOPK_MAX = 256
ROPE_THETA = 10000.0
N_GROUPS = 4
EXP_PER_GROUP = 8
N_EXPERTS = N_GROUPS * EXP_PER_GROUP
EXPERT_FF = D_MODEL // 2
TOP_IN_GROUP = 2
MOE_BLOCK = 128

A_SIZES = (GROUP_W, GROUP_W, GROUP_W, RWKV_W_RANK, RWKV_A_RANK, RWKV_G_RANK)
B_SIZES = (GROUP_W, GROUP_W, GROUP_W)
C_SIZES = (GROUP_W, GROUP_W, GROUP_W, GROUP_W, N_HEADS, N_HEADS)
D_SIZES = (GROUP_W, HEAD_DIM, HEAD_DIM, IDX_HEADS * IDX_DIM, IDX_DIM, IDX_HEADS)
A_COLS = sum(A_SIZES)
B_COLS = sum(B_SIZES)
C_COLS = sum(C_SIZES)
D_COLS = sum(D_SIZES)
P_TOTAL = A_COLS + B_COLS + C_COLS + D_COLS

kernel_name = "hymba_style_rwkv7_stickbreak_mlstm_dsa_hmoe"

F32 = jnp.float32


def split_cols(t, sizes):
    return jnp.split(t, [int(i) for i in np.cumsum(sizes)[:-1]], axis=-1)


def rms_norm(x, g):
    xf = x.astype(F32)
    y = xf * lax.rsqrt(jnp.mean(xf * xf, -1, keepdims=True) + NORM_EPS)
    return (y * g.astype(F32)).astype(x.dtype)


def rope(x, pos):
    half = x.shape[-1] // 2
    inv = ROPE_THETA ** (-jnp.arange(half, dtype=F32) / half)
    ang = pos.astype(F32)[:, None] * inv[None, :]
    cos = jnp.cos(ang)[None, :, None, :]
    sin = jnp.sin(ang)[None, :, None, :]
    xf = x.astype(F32)
    x1, x2 = xf[..., :half], xf[..., half:]
    return jnp.concatenate([x1 * cos - x2 * sin, x2 * cos + x1 * sin], -1).astype(x.dtype)


def token_shift(p, mu):
    prev = jnp.pad(p, ((0, 0), (1, 0), (0, 0)))[:, :-1]
    return p + (prev - p) * mu


def causal_dwconv(x, w, b):
    ch = x.shape[-1]
    y = lax.conv_general_dilated(x, w[:, None, :].astype(x.dtype), window_strides=(1,),
                                 padding=((w.shape[0] - 1, 0),),
                                 dimension_numbers=('NWC', 'WIO', 'NWC'), feature_group_count=ch)
    return y + b


def rwkv7_time_mix(p, mu, w0, w2, a0, a2, g2, k_k, k_a, r_k, ln_g, ln_b):
    B, S, _ = p.shape
    H, d = N_HEADS, HEAD_DIM
    p = token_shift(p, mu)
    r, k, v, wd, ad, gd = split_cols(p, A_SIZES)
    w = -jax.nn.softplus(-(w0 + jnp.tanh(wd) @ w2)) - 0.5
    decay = jnp.exp(-jnp.exp(w.astype(F32)))
    a = jax.nn.sigmoid(a0 + ad @ a2)
    g = jax.nn.sigmoid(gd) @ g2
    kk = (k * k_k).astype(F32).reshape(B, S, H, d)
    kk = kk / jnp.maximum(jnp.sqrt(jnp.sum(kk * kk, -1, keepdims=True)), 1e-12)
    k = k * (1 + (a - 1) * k_a)
    heads = lambda t: t.astype(F32).reshape(B, S, H, d)
    r_h, k_h, v_h, w_h, a_h = heads(r), heads(k), heads(v), heads(decay), heads(a)

    def step(state, inp):
        rt, wt, kt, vt, kkt, at = inp
        sa = jnp.einsum('bhvk,bhk->bhv', state, -kkt)
        state = (state * wt[:, :, None, :] + sa[..., None] * (kkt * at)[:, :, None, :]
                 + vt[..., None] * kt[:, :, None, :])
        return state, jnp.einsum('bhvk,bhk->bhv', state, rt)

    xs = tuple(jnp.moveaxis(t, 1, 0) for t in (r_h, w_h, k_h, v_h, kk, a_h))
    _, y = lax.scan(step, jnp.zeros((B, H, d, d), F32), xs)
    y = jnp.moveaxis(y, 0, 1)
    mean = jnp.mean(y, -1, keepdims=True)
    var = jnp.mean(jnp.square(y - mean), -1, keepdims=True)
    y = (y - mean) * lax.rsqrt(var + RWKV_GN_EPS) * ln_g.astype(F32).reshape(H, d) + ln_b.astype(F32).reshape(H, d)
    bonus = jnp.sum(r_h * k_h * r_k.astype(F32).reshape(H, d), -1, keepdims=True) * v_h
    y = (y + bonus).reshape(B, S, GROUP_W) * g.astype(F32)
    return y.astype(p.dtype)


def stick_breaking_attn(q, k, v):
    B, S, H, d = q.shape
    scale = d ** -0.5
    outs = []
    for i in range(S // SB_BLOCK):
        q0 = i * SB_BLOCK
        kl = q0 + SB_BLOCK
        z = jnp.einsum('bqhd,bkhd->bhqk', q[:, q0:kl], k[:, :kl]).astype(F32) * scale
        t_idx = q0 + jnp.arange(SB_BLOCK)[:, None]
        s_idx = jnp.arange(kl)[None, :]
        mask = s_idx < t_idx
        log1m = jnp.where(mask, jax.nn.log_sigmoid(-z), 0.0)
        cs = jnp.cumsum(log1m, -1)
        log_a = jax.nn.log_sigmoid(z) + cs[..., -1:] - cs
        att = jnp.where(mask, jnp.exp(log_a), 0.0)
        outs.append(jnp.einsum('bhqk,bkhd->bqhd', att.astype(v.dtype), v[:, :kl]))
    return jnp.concatenate(outs, 1)


def mlstm_chunkwise(q, k, v, log_i, log_f):
    B, S, H, d = q.shape
    L = ML_CHUNK
    nc = S // L

    def to_chunks(t):
        t = t.reshape((B, nc, L, H) + t.shape[3:])
        return jnp.moveaxis(jnp.moveaxis(t, 3, 2), 1, 0)

    causal = jnp.tril(jnp.ones((L, L), bool))

    def chunk(carry, inp):
        c_st, n_st, m_st = carry
        qc, kc, vc, li, lf = inp
        b = jnp.cumsum(lf, -1)
        dmat = jnp.where(causal, b[..., :, None] - b[..., None, :] + li[..., None, :], -jnp.inf)
        g_inter = b + m_st[..., None]
        m_t = jnp.maximum(g_inter, jnp.max(dmat, -1))
        s_inter = jnp.exp(g_inter - m_t)
        sqk = jnp.einsum('bhtd,bhsd->bhts', qc, kc) * jnp.exp(dmat - m_t[..., None])
        num = s_inter[..., None] * jnp.einsum('bhvd,bhtd->bhtv', c_st, qc) + jnp.einsum('bhts,bhsv->bhtv', sqk, vc)
        den = s_inter * jnp.einsum('bhd,bhtd->bht', n_st, qc) + jnp.sum(sqk, -1)
        h = num / jnp.maximum(jnp.abs(den), jnp.exp(-m_t))[..., None]
        b_last = b[..., -1]
        dec = b_last[..., None] - b + li
        m_new = jnp.maximum(b_last + m_st, jnp.max(dec, -1))
        wk = jnp.exp(dec - m_new[..., None])
        s_old = jnp.exp(b_last + m_st - m_new)
        c_st = s_old[..., None, None] * c_st + jnp.einsum('bhs,bhsv,bhsd->bhvd', wk, vc, kc)
        n_st = s_old[..., None] * n_st + jnp.einsum('bhs,bhsd->bhd', wk, kc)
        return (c_st, n_st, m_new), h

    init = (jnp.zeros((B, H, d, d), F32), jnp.zeros((B, H, d), F32), jnp.zeros((B, H), F32))
    _, hs = lax.scan(chunk, init, tuple(to_chunks(t) for t in (q, k, v, log_i, log_f)))
    hs = jnp.moveaxis(jnp.moveaxis(hs, 0, 1), 2, 3)
    return hs.reshape(B, S, H, d)


def mlstm_mix(p, conv_w, conv_b, ig_b, fg_b, norm_g):
    B, S, _ = p.shape
    H, d = N_HEADS, HEAD_DIM
    qk = jax.nn.silu(causal_dwconv(p[..., :2 * GROUP_W], conv_w, conv_b))
    q, k = qk[..., :GROUP_W], qk[..., GROUP_W:]
    _, _, v, o, ig, fg = split_cols(p, C_SIZES)
    cap = lambda t: GATE_CAP * jnp.tanh(t / GATE_CAP)
    log_i = cap((ig + ig_b).astype(F32))
    log_f = jax.nn.log_sigmoid(cap((fg + fg_b).astype(F32)))
    heads = lambda t: t.astype(F32).reshape(B, S, H, d)
    h = mlstm_chunkwise(heads(q), heads(k) * d ** -0.5, heads(v), log_i, log_f)
    h = rms_norm(h, norm_g.reshape(H, d)).reshape(B, S, GROUP_W)
    return (jax.nn.sigmoid(o.astype(F32)) * h).astype(p.dtype)


def dsa_attn(q, k, v, qi, ki, wi):
    B, S, H, d = q.shape
    n_sel = min(TOPK_MAX, S // 4)
    gather = jax.vmap(lambda t, ix: t[ix])
    outs = []
    for i in range(S // DSA_BLOCK):
        q0 = i * DSA_BLOCK
        q1 = q0 + DSA_BLOCK
        kl = min(S, max(q1, n_sel))
        adm = jnp.arange(kl)[None, :] <= (q0 + jnp.arange(DSA_BLOCK))[:, None]
        sc = jnp.einsum('bqhe,bke->bqhk', qi[:, q0:q1], ki[:, :kl]).astype(F32)
        idx_score = jnp.einsum('bqh,bqhk->bqk', wi[:, q0:q1].astype(F32), jax.nn.relu(sc))
        idx_score = jnp.where(adm, idx_score, -jnp.inf)
        vals, sel = lax.top_k(idx_score, n_sel)
        valid = jnp.isfinite(vals)
        ks = gather(k[:, :kl], sel)
        vs = gather(v[:, :kl], sel)
        logits = jnp.einsum('bqhd,bqnd->bhqn', q[:, q0:q1], ks).astype(F32) * d ** -0.5
        logits = jnp.where(valid[:, None], logits, -jnp.inf)
        prob = jax.nn.softmax(logits, -1)
        outs.append(jnp.einsum('bhqn,bqnd->bqhd', prob.astype(v.dtype), vs))
    return jnp.concatenate(outs, 1)


def hier_moe(h, wg, bg, we, be, w1, w3, w2):
    B, S, D = h.shape
    N = B * S
    hf = h.reshape(N, D)
    g_prob = jax.nn.softmax((hf @ wg).astype(F32) + bg, -1)
    g_p, g_idx = lax.top_k(g_prob, 1)
    e_logits = ((hf @ we).astype(F32) + be).reshape(N, N_GROUPS, EXP_PER_GROUP)
    e_logits = jnp.take_along_axis(e_logits, g_idx[:, :, None], axis=1)[:, 0]
    e_p, e_idx = lax.top_k(jax.nn.softmax(e_logits, -1), TOP_IN_GROUP)
    gate = g_p * e_p / jnp.sum(e_p, -1, keepdims=True)
    expert = g_idx * EXP_PER_GROUP + e_idx
    n_asg = N * TOP_IN_GROUP
    flat_e = expert.reshape(n_asg)
    flat_tok = jnp.repeat(jnp.arange(N, dtype=jnp.int32), TOP_IN_GROUP)
    flat_w = gate.reshape(n_asg)
    order = jnp.argsort(flat_e)
    se = flat_e[order]
    counts = jnp.bincount(flat_e, length=N_EXPERTS)
    start = jnp.cumsum(counts) - counts
    pad_counts = (counts + MOE_BLOCK - 1) // MOE_BLOCK * MOE_BLOCK
    pad_end = jnp.cumsum(pad_counts)
    pad_start = pad_end - pad_counts
    slot = pad_start[se] + jnp.arange(n_asg) - start[se]
    n_blocks = -(-n_asg // MOE_BLOCK) + N_EXPERTS
    n_slots = n_blocks * MOE_BLOCK
    slot_tok = jnp.full((n_slots,), N, jnp.int32).at[slot].set(flat_tok[order])
    slot_w = jnp.zeros((n_slots,), F32).at[slot].set(flat_w[order])
    blk_start = jnp.arange(n_blocks) * MOE_BLOCK
    blk_e = jnp.minimum(jnp.sum(pad_end[None, :] <= blk_start[:, None], 1), N_EXPERTS - 1)
    h_pad = jnp.concatenate([hf, jnp.zeros((1, D), hf.dtype)], 0)

    def run_block(args):
        tok, e, wt = args
        xb = h_pad[tok]
        y = (jax.nn.silu(xb @ w1[e]) * (xb @ w3[e])) @ w2[e]
        return y * wt[:, None].astype(y.dtype)

    yb = lax.map(run_block, (slot_tok.reshape(n_blocks, MOE_BLOCK), blk_e,
                             slot_w.reshape(n_blocks, MOE_BLOCK)))
    out = jnp.zeros((N + 1, D), yb.dtype).at[slot_tok].add(yb.reshape(n_slots, D))[:N]
    return out.reshape(B, S, D)


def setup_inputs(seed: int = 0) -> dict:
    key = jax.random.key(seed)
    ks = iter(jax.random.split(key, 64))
    nrm = lambda shape, s: jax.random.normal(next(ks), shape, F32) * s
    L = DEPTH
    D = D_MODEL
    return {
        "x": nrm((BATCH, SEQ, D), 1.0),
        "c": nrm((BATCH, D), 1.0),
        "ada_w": nrm((L, D, 6 * D), 0.3 * D ** -0.5),
        "ada_b": nrm((L, 6 * D), 0.02),
        "norm1_g": 1.0 + nrm((L, D), 0.1),
        "norm2_g": 1.0 + nrm((L, D), 0.1),
        "w_in": nrm((L, D, P_TOTAL), D ** -0.5),
        "rk_mu": jax.random.uniform(next(ks), (L, A_COLS), F32),
        "rk_w0": nrm((L, GROUP_W), 0.5),
        "rk_w2": nrm((L, RWKV_W_RANK, GROUP_W), 0.5 * RWKV_W_RANK ** -0.5),
        "rk_a0": nrm((L, GROUP_W), 0.5),
        "rk_a2": nrm((L, RWKV_A_RANK, GROUP_W), 0.5 * RWKV_A_RANK ** -0.5),
        "rk_g2": nrm((L, RWKV_G_RANK, GROUP_W), RWKV_G_RANK ** -0.5),
        "rk_kk": 0.85 + nrm((L, GROUP_W), 0.1),
        "rk_ka": 1.0 + nrm((L, GROUP_W), 0.1),
        "rk_rk": nrm((L, GROUP_W), 0.1),
        "rk_ln_g": 1.0 + nrm((L, GROUP_W), 0.1),
        "rk_ln_b": nrm((L, GROUP_W), 0.02),
        "sb_norm_g": 1.0 + nrm((L, GROUP_W), 0.1),
        "ml_conv_w": nrm((L, ML_CONV, 2 * GROUP_W), ML_CONV ** -0.5),
        "ml_conv_b": nrm((L, 2 * GROUP_W), 0.02),
        "ml_ig_b": -1.0 + nrm((L, N_HEADS), 0.5),
        "ml_fg_b": 3.0 + nrm((L, N_HEADS), 0.5),
        "ml_norm_g": 1.0 + nrm((L, GROUP_W), 0.1),
        "ds_qn_g": 1.0 + nrm((L, HEAD_DIM), 0.1),
        "ds_kn_g": 1.0 + nrm((L, HEAD_DIM), 0.1),
        "ds_out_g": 1.0 + nrm((L, GROUP_W), 0.1),
        "w_out": nrm((L, N_MIXERS * GROUP_W, D), (N_MIXERS * GROUP_W) ** -0.5),
        "moe_wg": nrm((L, D, N_GROUPS), D ** -0.5),
        "moe_bg": nrm((L, N_GROUPS), 0.01),
        "moe_we": nrm((L, D, N_EXPERTS), D ** -0.5),
        "moe_be": nrm((L, N_EXPERTS), 0.01),
        "moe_w1": nrm((L, N_EXPERTS, D, EXPERT_FF), D ** -0.5),
        "moe_w3": nrm((L, N_EXPERTS, D, EXPERT_FF), D ** -0.5),
        "moe_w2": nrm((L, N_EXPERTS, EXPERT_FF, D), EXPERT_FF ** -0.5),
    }


def reference(x, c, ada_w, ada_b, norm1_g, norm2_g, w_in, rk_mu, rk_w0, rk_w2, rk_a0, rk_a2, rk_g2,
              rk_kk, rk_ka, rk_rk, rk_ln_g, rk_ln_b, sb_norm_g, ml_conv_w, ml_conv_b, ml_ig_b, ml_fg_b,
              ml_norm_g, ds_qn_g, ds_kn_g, ds_out_g, w_out, moe_wg, moe_bg, moe_we, moe_be,
              moe_w1, moe_w3, moe_w2):
    B, S, _ = x.shape
    H, d = N_HEADS, HEAD_DIM
    pos = jnp.arange(S)
    c_act = jax.nn.silu(c)
    for l in range(DEPTH):
        mod = (c_act @ ada_w[l] + ada_b[l])[:, None, :]
        sh1, sc1, gt1, sh2, sc2, gt2 = jnp.split(mod, 6, axis=-1)

        h = rms_norm(x, norm1_g[l]) * (1 + sc1) + sh1
        p = h @ w_in[l]
        pA, pB, pC, pD = split_cols(p, (A_COLS, B_COLS, C_COLS, D_COLS))

        yA = rwkv7_time_mix(pA, rk_mu[l], rk_w0[l], rk_w2[l], rk_a0[l], rk_a2[l], rk_g2[l],
                            rk_kk[l], rk_ka[l], rk_rk[l], rk_ln_g[l], rk_ln_b[l])

        qb, kb, vb = (t.reshape(B, S, H, d) for t in split_cols(pB, B_SIZES))
        yB = rms_norm(stick_breaking_attn(qb, kb, vb), sb_norm_g[l].reshape(H, d)).reshape(B, S, GROUP_W)

        yC = mlstm_mix(pC, ml_conv_w[l], ml_conv_b[l], ml_ig_b[l], ml_fg_b[l], ml_norm_g[l])

        qd, kd, vd, qi, ki, wi = split_cols(pD, D_SIZES)
        qd = rope(rms_norm(qd.reshape(B, S, H, d), ds_qn_g[l]), pos)
        kd = rope(rms_norm(kd[:, :, None, :], ds_kn_g[l]), pos)[:, :, 0]
        qi = rope(qi.reshape(B, S, IDX_HEADS, IDX_DIM), pos)
        ki = rope(ki[:, :, None, :], pos)[:, :, 0]
        wi = wi * (IDX_HEADS ** -0.5 * IDX_DIM ** -0.5)
        yD = rms_norm(dsa_attn(qd, kd, vd, qi, ki, wi), ds_out_g[l].reshape(H, d)).reshape(B, S, GROUP_W)

        y = jnp.concatenate([yA, yB, yC, yD], -1) @ w_out[l]
        x = x + gt1 * y

        h = rms_norm(x, norm2_g[l]) * (1 + sc2) + sh2
        x = x + gt2 * hier_moe(h, moe_wg[l], moe_bg[l], moe_we[l], moe_be[l], moe_w1[l], moe_w3[l], moe_w2[l])
    return x
```

```python
import functools

import jax
import jax.numpy as jnp
import numpy as np
from jax import lax
from jax.experimental import pallas as pl
from jax.experimental.pallas import tpu as pltpu

F32 = jnp.float32
BF16 = jnp.bfloat16

D_MODEL = 1024
N_MIXERS = 4
GROUP_W = D_MODEL // N_MIXERS
HEAD_DIM = 64
N_HEADS = GROUP_W // HEAD_DIM
NORM_EPS = 1e-6
RWKV_W_RANK = 32
RWKV_A_RANK = 32
RWKV_G_RANK = 64
RWKV_GN_EPS = 64e-5
SB_BLOCK = 128
ML_CHUNK = 64
ML_CONV = 4
GATE_CAP = 15.0
DSA_BLOCK = 128
IDX_HEADS = 4
IDX_DIM = 32
TOPK_MAX = 256
ROPE_THETA = 10000.0
N_GROUPS = 4
EXP_PER_GROUP = 8
N_EXPERTS = N_GROUPS * EXP_PER_GROUP
EXPERT_FF = D_MODEL // 2
TOP_IN_GROUP = 2

A_SIZES = (GROUP_W, GROUP_W, GROUP_W, RWKV_W_RANK, RWKV_A_RANK, RWKV_G_RANK)
B_SIZES = (GROUP_W, GROUP_W, GROUP_W)
C_SIZES = (GROUP_W, GROUP_W, GROUP_W, GROUP_W, N_HEADS, N_HEADS)
D_SIZES = (GROUP_W, HEAD_DIM, HEAD_DIM, IDX_HEADS * IDX_DIM, IDX_DIM, IDX_HEADS)
A_COLS = sum(A_SIZES)
B_COLS = sum(B_SIZES)
C_COLS = sum(C_SIZES)
D_COLS = sum(D_SIZES)

LANE = 128
A_PAD = 896
B_PAD = 768
C_PAD = 1152
D_PAD = 640
P_PAD = A_PAD + B_PAD + C_PAD + D_PAD
ROUTER_PAD = LANE

IN_ROWS = 256
OUT_ROWS = 512
MOE_ROWS = 256
VMEM_LIMIT = 48 * 1024 * 1024


def _split_cols(t, sizes):
    return jnp.split(t, [int(i) for i in np.cumsum(sizes)[:-1]], axis=-1)


def _in_proj_kernel(x_ref, sc_ref, sh_ref, g_ref, w_ref, oa_ref, ob_ref, oc_ref, od_ref):
    x = x_ref[0]
    y = x * lax.rsqrt(jnp.mean(x * x, -1, keepdims=True) + NORM_EPS) * g_ref[...]
    h = y * (1.0 + sc_ref[0]) + sh_ref[0]
    p = jnp.dot(h.astype(BF16), w_ref[...], preferred_element_type=F32)
    oa_ref[0] = p[:, :A_PAD]
    ob_ref[0] = p[:, A_PAD:A_PAD + B_PAD]
    oc_ref[0] = p[:, A_PAD + B_PAD:A_PAD + B_PAD + C_PAD]
    od_ref[0] = p[:, A_PAD + B_PAD + C_PAD:]


def _in_proj(x, sc, sh, g, w_pad):
    B, S, D = x.shape
    row = lambda w: pl.BlockSpec((1, IN_ROWS, w), lambda b, i: (b, i, 0))
    vec = pl.BlockSpec((1, 1, D), lambda b, i: (b, 0, 0))
    return pl.pallas_call(
        _in_proj_kernel,
        grid=(B, S // IN_ROWS),
        in_specs=[row(D), vec, vec, pl.BlockSpec((1, D), lambda b, i: (0, 0)),
                  pl.BlockSpec((D, P_PAD), lambda b, i: (0, 0))],
        out_specs=[row(A_PAD), row(B_PAD), row(C_PAD), row(D_PAD)],
        out_shape=[jax.ShapeDtypeStruct((B, S, w), F32) for w in (A_PAD, B_PAD, C_PAD, D_PAD)],
        compiler_params=pltpu.CompilerParams(dimension_semantics=("parallel", "parallel"),
                                             vmem_limit_bytes=VMEM_LIMIT),
        name="in_proj",
    )(x, sc, sh, g, w_pad)


def _pad_w_in(w):
    wa, wb, wc, wd = _split_cols(w, (A_COLS, B_COLS, C_COLS, D_COLS))
    padc = lambda t, n: jnp.pad(t, ((0, 0), (0, n - t.shape[1])))
    return jnp.concatenate([padc(wa, A_PAD), padc(wb, B_PAD), padc(wc, C_PAD), padc(wd, D_PAD)], 1).astype(BF16)


def _split_bf16(t):
    hi = t.astype(BF16)
    lo = (t - hi.astype(F32)).astype(BF16)
    return hi, lo


def _out_proj_kernel(ya_ref, yb_ref, yc_ref, yd_ref, x_ref, gt_ref, sc_ref, sh_ref, g_ref, w_ref,
                     rhi_ref, rlo_ref, x1_ref, h2_ref, lg_ref):
    acc = jnp.zeros(x_ref.shape[1:], F32)
    for n, y_ref in enumerate((ya_ref, yb_ref, yc_ref, yd_ref)):
        acc += jnp.dot(y_ref[0].astype(BF16), w_ref[n * GROUP_W:(n + 1) * GROUP_W, :],
                       preferred_element_type=F32)
    x1 = x_ref[0] + gt_ref[0] * acc
    x1_ref[0] = x1
    y = x1 * lax.rsqrt(jnp.mean(x1 * x1, -1, keepdims=True) + NORM_EPS) * g_ref[...]
    h = y * (1.0 + sc_ref[0]) + sh_ref[0]
    hi, lo = _split_bf16(h)
    h2_ref[0] = hi
    lg_ref[0] = (jnp.dot(hi, rhi_ref[...], preferred_element_type=F32)
                 + jnp.dot(lo, rhi_ref[...], preferred_element_type=F32)
                 + jnp.dot(hi, rlo_ref[...], preferred_element_type=F32))


def _out_proj(ys, x, gt, sc, sh, g, w_out, r_hi, r_lo):
    B, S, D = x.shape
    row = lambda w: pl.BlockSpec((1, OUT_ROWS, w), lambda b, i: (b, i, 0))
    vec = pl.BlockSpec((1, 1, D), lambda b, i: (b, 0, 0))
    full = lambda a: pl.BlockSpec(a.shape, lambda b, i: (0,) * a.ndim)
    return pl.pallas_call(
        _out_proj_kernel,
        grid=(B, S // OUT_ROWS),
        in_specs=[row(GROUP_W)] * 4 + [row(D), vec, vec, vec, full(g), full(w_out), full(r_hi), full(r_lo)],
        out_specs=[row(D), row(D), row(ROUTER_PAD)],
        out_shape=[jax.ShapeDtypeStruct((B, S, D), F32), jax.ShapeDtypeStruct((B, S, D), BF16),
                   jax.ShapeDtypeStruct((B, S, ROUTER_PAD), F32)],
        compiler_params=pltpu.CompilerParams(dimension_semantics=("parallel", "parallel"),
                                             vmem_limit_bytes=VMEM_LIMIT),
        name="out_proj",
    )(*ys, x, gt, sc, sh, g, w_out, r_hi, r_lo)


def _moe_ffn_kernel(blk_e_ref, x_ref, wt_ref, w1_ref, w3_ref, w2_ref, o_ref):
    del blk_e_ref
    xb = x_ref[...]
    a = jnp.dot(xb, w1_ref[0], preferred_element_type=F32)
    b = jnp.dot(xb, w3_ref[0], preferred_element_type=F32)
    hmid = (a * jax.nn.sigmoid(a) * b).astype(BF16)
    y = jnp.dot(hmid, w2_ref[0], preferred_element_type=F32)
    o_ref[...] = y * wt_ref[...]


def _moe_ffn(blk_e, xs, wt, w1, w3, w2):
    n_slots, D = xs.shape
    n_blocks = n_slots // MOE_ROWS
    FF = w1.shape[-1]
    return pl.pallas_call(
        _moe_ffn_kernel,
        grid_spec=pltpu.PrefetchScalarGridSpec(
            num_scalar_prefetch=1,
            grid=(n_blocks,),
            in_specs=[pl.BlockSpec((MOE_ROWS, D), lambda i, e: (i, 0)),
                      pl.BlockSpec((MOE_ROWS, 1), lambda i, e: (i, 0)),
                      pl.BlockSpec((1, D, FF), lambda i, e: (e[i], 0, 0)),
                      pl.BlockSpec((1, D, FF), lambda i, e: (e[i], 0, 0)),
                      pl.BlockSpec((1, FF, D), lambda i, e: (e[i], 0, 0))],
            out_specs=pl.BlockSpec((MOE_ROWS, D), lambda i, e: (i, 0)),
        ),
        out_shape=jax.ShapeDtypeStruct((n_slots, D), F32),
        compiler_params=pltpu.CompilerParams(dimension_semantics=("arbitrary",),
                                             vmem_limit_bytes=VMEM_LIMIT),
        name="moe_ffn",
    )(blk_e, xs, wt, w1, w3, w2)


def _hier_moe(h2, logits, bg, be, w1, w3, w2):
    N, D = h2.shape
    g_prob = jax.nn.softmax(logits[:, :N_GROUPS] + bg, -1)
    g_p, g_idx = lax.top_k(g_prob, 1)
    e_logits = (logits[:, N_GROUPS:N_GROUPS + N_EXPERTS] + be).reshape(N, N_GROUPS, EXP_PER_GROUP)
    e_logits = jnp.take_along_axis(e_logits, g_idx[:, :, None], axis=1)[:, 0]
    e_p, e_idx = lax.top_k(jax.nn.softmax(e_logits, -1), TOP_IN_GROUP)
    gate = g_p * e_p / jnp.sum(e_p, -1, keepdims=True)
    expert = g_idx * EXP_PER_GROUP + e_idx
    n_asg = N * TOP_IN_GROUP
    flat_e = expert.reshape(n_asg)
    order = jnp.argsort(flat_e)
    se = flat_e[order]
    counts = jnp.bincount(flat_e, length=N_EXPERTS)
    start = jnp.cumsum(counts) - counts
    pad_counts = (counts + MOE_ROWS - 1) // MOE_ROWS * MOE_ROWS
    pad_end = jnp.cumsum(pad_counts)
    pad_start = pad_end - pad_counts
    slot = (pad_start[se] + jnp.arange(n_asg) - start[se]).astype(jnp.int32)
    n_blocks = n_asg // MOE_ROWS + N_EXPERTS
    n_slots = n_blocks * MOE_ROWS
    slot_tok = jnp.zeros((n_slots,), jnp.int32).at[slot].set((order // TOP_IN_GROUP).astype(jnp.int32))
    slot_w = jnp.zeros((n_slots,), F32).at[slot].set(gate.reshape(n_asg)[order])
    blk_start = jnp.arange(n_blocks) * MOE_ROWS
    blk_e = jnp.minimum(jnp.sum(pad_end[None, :] <= blk_start[:, None], 1), N_EXPERTS - 1).astype(jnp.int32)
    yb = _moe_ffn(blk_e, h2[slot_tok], slot_w[:, None], w1, w3, w2)
    asg_slot = jnp.zeros((n_asg,), jnp.int32).at[order].set(slot).reshape(N, TOP_IN_GROUP)
    return yb[asg_slot[:, 0]] + yb[asg_slot[:, 1]]


def _rms_norm(x, g):
    xf = x.astype(F32)
    y = xf * lax.rsqrt(jnp.mean(xf * xf, -1, keepdims=True) + NORM_EPS)
    return (y * g.astype(F32)).astype(x.dtype)


def _rope(x, pos):
    half = x.shape[-1] // 2
    inv = ROPE_THETA ** (-jnp.arange(half, dtype=F32) / half)
    ang = pos.astype(F32)[:, None] * inv[None, :]
    cos = jnp.cos(ang)[None, :, None, :]
    sin = jnp.sin(ang)[None, :, None, :]
    xf = x.astype(F32)
    x1, x2 = xf[..., :half], xf[..., half:]
    return jnp.concatenate([x1 * cos - x2 * sin, x2 * cos + x1 * sin], -1).astype(x.dtype)


def _token_shift(p, mu):
    prev = jnp.pad(p, ((0, 0), (1, 0), (0, 0)))[:, :-1]
    return p + (prev - p) * mu


def _causal_dwconv(x, w, b):
    ch = x.shape[-1]
    y = lax.conv_general_dilated(x, w[:, None, :].astype(x.dtype), window_strides=(1,),
                                 padding=((w.shape[0] - 1, 0),),
                                 dimension_numbers=('NWC', 'WIO', 'NWC'), feature_group_count=ch)
    return y + b


def _rwkv7_time_mix(p, mu, w0, w2, a0, a2, g2, k_k, k_a, r_k, ln_g, ln_b):
    B, S, _ = p.shape
    H, d = N_HEADS, HEAD_DIM
    p = _token_shift(p, mu)
    r, k, v, wd, ad, gd = _split_cols(p, A_SIZES)
    w = -jax.nn.softplus(-(w0 + jnp.tanh(wd) @ w2)) - 0.5
    decay = jnp.exp(-jnp.exp(w.astype(F32)))
    a = jax.nn.sigmoid(a0 + ad @ a2)
    g = jax.nn.sigmoid(gd) @ g2
    kk = (k * k_k).astype(F32).reshape(B, S, H, d)
    kk = kk / jnp.maximum(jnp.sqrt(jnp.sum(kk * kk, -1, keepdims=True)), 1e-12)
    k = k * (1 + (a - 1) * k_a)
    heads = lambda t: t.astype(F32).reshape(B, S, H, d)
    r_h, k_h, v_h, w_h, a_h = heads(r), heads(k), heads(v), heads(decay), heads(a)

    def step(state, inp):
        rt, wt, kt, vt, kkt, at = inp
        sa = jnp.einsum('bhvk,bhk->bhv', state, -kkt)
        state = (state * wt[:, :, None, :] + sa[..., None] * (kkt * at)[:, :, None, :]
                 + vt[..., None] * kt[:, :, None, :])
        return state, jnp.einsum('bhvk,bhk->bhv', state, rt)

    xs = tuple(jnp.moveaxis(t, 1, 0) for t in (r_h, w_h, k_h, v_h, kk, a_h))
    _, y = lax.scan(step, jnp.zeros((B, H, d, d), F32), xs)
    y = jnp.moveaxis(y, 0, 1)
    mean = jnp.mean(y, -1, keepdims=True)
    var = jnp.mean(jnp.square(y - mean), -1, keepdims=True)
    y = (y - mean) * lax.rsqrt(var + RWKV_GN_EPS) * ln_g.astype(F32).reshape(H, d) + ln_b.astype(F32).reshape(H, d)
    bonus = jnp.sum(r_h * k_h * r_k.astype(F32).reshape(H, d), -1, keepdims=True) * v_h
    y = (y + bonus).reshape(B, S, GROUP_W) * g.astype(F32)
    return y.astype(p.dtype)


def _stick_breaking_attn(q, k, v):
    B, S, H, d = q.shape
    scale = d ** -0.5
    outs = []
    for i in range(S // SB_BLOCK):
        q0 = i * SB_BLOCK
        kl = q0 + SB_BLOCK
        z = jnp.einsum('bqhd,bkhd->bhqk', q[:, q0:kl], k[:, :kl]).astype(F32) * scale
        t_idx = q0 + jnp.arange(SB_BLOCK)[:, None]
        s_idx = jnp.arange(kl)[None, :]
        mask = s_idx < t_idx
        log1m = jnp.where(mask, jax.nn.log_sigmoid(-z), 0.0)
        cs = jnp.cumsum(log1m, -1)
        log_a = jax.nn.log_sigmoid(z) + cs[..., -1:] - cs
        att = jnp.where(mask, jnp.exp(log_a), 0.0)
        outs.append(jnp.einsum('bhqk,bkhd->bqhd', att.astype(v.dtype), v[:, :kl]))
    return jnp.concatenate(outs, 1)


def _mlstm_chunkwise(q, k, v, log_i, log_f):
    B, S, H, d = q.shape
    L = ML_CHUNK
    nc = S // L

    def to_chunks(t):
        t = t.reshape((B, nc, L, H) + t.shape[3:])
        return jnp.moveaxis(jnp.moveaxis(t, 3, 2), 1, 0)

    causal = jnp.tril(jnp.ones((L, L), bool))

    def chunk(carry, inp):
        c_st, n_st, m_st = carry
        qc, kc, vc, li, lf = inp
        b = jnp.cumsum(lf, -1)
        dmat = jnp.where(causal, b[..., :, None] - b[..., None, :] + li[..., None, :], -jnp.inf)
        g_inter = b + m_st[..., None]
        m_t = jnp.maximum(g_inter, jnp.max(dmat, -1))
        s_inter = jnp.exp(g_inter - m_t)
        sqk = jnp.einsum('bhtd,bhsd->bhts', qc, kc) * jnp.exp(dmat - m_t[..., None])
        num = s_inter[..., None] * jnp.einsum('bhvd,bhtd->bhtv', c_st, qc) + jnp.einsum('bhts,bhsv->bhtv', sqk, vc)
        den = s_inter * jnp.einsum('bhd,bhtd->bht', n_st, qc) + jnp.sum(sqk, -1)
        h = num / jnp.maximum(jnp.abs(den), jnp.exp(-m_t))[..., None]
        b_last = b[..., -1]
        dec = b_last[..., None] - b + li
        m_new = jnp.maximum(b_last + m_st, jnp.max(dec, -1))
        wk = jnp.exp(dec - m_new[..., None])
        s_old = jnp.exp(b_last + m_st - m_new)
        c_st = s_old[..., None, None] * c_st + jnp.einsum('bhs,bhsv,bhsd->bhvd', wk, vc, kc)
        n_st = s_old[..., None] * n_st + jnp.einsum('bhs,bhsd->bhd', wk, kc)
        return (c_st, n_st, m_new), h

    init = (jnp.zeros((B, H, d, d), F32), jnp.zeros((B, H, d), F32), jnp.zeros((B, H), F32))
    _, hs = lax.scan(chunk, init, tuple(to_chunks(t) for t in (q, k, v, log_i, log_f)))
    hs = jnp.moveaxis(jnp.moveaxis(hs, 0, 1), 2, 3)
    return hs.reshape(B, S, H, d)


def _mlstm_mix(p, conv_w, conv_b, ig_b, fg_b, norm_g):
    B, S, _ = p.shape
    H, d = N_HEADS, HEAD_DIM
    qk = jax.nn.silu(_causal_dwconv(p[..., :2 * GROUP_W], conv_w, conv_b))
    q, k = qk[..., :GROUP_W], qk[..., GROUP_W:]
    _, _, v, o, ig, fg = _split_cols(p, C_SIZES)
    cap = lambda t: GATE_CAP * jnp.tanh(t / GATE_CAP)
    log_i = cap((ig + ig_b).astype(F32))
    log_f = jax.nn.log_sigmoid(cap((fg + fg_b).astype(F32)))
    heads = lambda t: t.astype(F32).reshape(B, S, H, d)
    h = _mlstm_chunkwise(heads(q), heads(k) * d ** -0.5, heads(v), log_i, log_f)
    h = _rms_norm(h, norm_g.reshape(H, d)).reshape(B, S, GROUP_W)
    return (jax.nn.sigmoid(o.astype(F32)) * h).astype(p.dtype)


def _dsa_attn(q, k, v, qi, ki, wi):
    B, S, H, d = q.shape
    n_sel = min(TOPK_MAX, S // 4)
    gather = jax.vmap(lambda t, ix: t[ix])
    outs = []
    for i in range(S // DSA_BLOCK):
        q0 = i * DSA_BLOCK
        q1 = q0 + DSA_BLOCK
        kl = min(S, max(q1, n_sel))
        adm = jnp.arange(kl)[None, :] <= (q0 + jnp.arange(DSA_BLOCK))[:, None]
        sc = jnp.einsum('bqhe,bke->bqhk', qi[:, q0:q1], ki[:, :kl]).astype(F32)
        idx_score = jnp.einsum('bqh,bqhk->bqk', wi[:, q0:q1].astype(F32), jax.nn.relu(sc))
        idx_score = jnp.where(adm, idx_score, -jnp.inf)
        vals, sel = lax.top_k(idx_score, n_sel)
        valid = jnp.isfinite(vals)
        ks = gather(k[:, :kl], sel)
        vs = gather(v[:, :kl], sel)
        logits = jnp.einsum('bqhd,bqnd->bhqn', q[:, q0:q1], ks).astype(F32) * d ** -0.5
        logits = jnp.where(valid[:, None], logits, -jnp.inf)
        prob = jax.nn.softmax(logits, -1)
        outs.append(jnp.einsum('bhqn,bqnd->bqhd', prob.astype(v.dtype), vs))
    return jnp.concatenate(outs, 1)


def kernel(x, c, ada_w, ada_b, norm1_g, norm2_g, w_in, rk_mu, rk_w0, rk_w2, rk_a0, rk_a2, rk_g2, rk_kk, rk_ka, rk_rk, rk_ln_g, rk_ln_b, sb_norm_g, ml_conv_w, ml_conv_b, ml_ig_b, ml_fg_b, ml_norm_g, ds_qn_g, ds_kn_g, ds_out_g, w_out, moe_wg, moe_bg, moe_we, moe_be, moe_w1, moe_w3, moe_w2):
    B, S, D = x.shape
    H, d = N_HEADS, HEAD_DIM
    depth = ada_w.shape[0]
    pos = jnp.arange(S)
    c_act = jax.nn.silu(c)
    for l in range(depth):
        mod = (c_act @ ada_w[l] + ada_b[l])[:, None, :]
        sh1, sc1, gt1, sh2, sc2, gt2 = jnp.split(mod, 6, axis=-1)

        pA, pB, pC, pD = _in_proj(x, sc1, sh1, norm1_g[l][None], _pad_w_in(w_in[l]))
        pC = pC[..., :C_COLS]
        pD = pD[..., :D_COLS]

        yA = _rwkv7_time_mix(pA, rk_mu[l], rk_w0[l], rk_w2[l], rk_a0[l], rk_a2[l], rk_g2[l],
                             rk_kk[l], rk_ka[l], rk_rk[l], rk_ln_g[l], rk_ln_b[l])

        qb, kb, vb = (t.reshape(B, S, H, d) for t in _split_cols(pB, B_SIZES))
        yB = _rms_norm(_stick_breaking_attn(qb, kb, vb), sb_norm_g[l].reshape(H, d)).reshape(B, S, GROUP_W)

        yC = _mlstm_mix(pC, ml_conv_w[l], ml_conv_b[l], ml_ig_b[l], ml_fg_b[l], ml_norm_g[l])

        qd, kd, vd, qi, ki, wi = _split_cols(pD, D_SIZES)
        qd = _rope(_rms_norm(qd.reshape(B, S, H, d), ds_qn_g[l]), pos)
        kd = _rope(_rms_norm(kd[:, :, None, :], ds_kn_g[l]), pos)[:, :, 0]
        qi = _rope(qi.reshape(B, S, IDX_HEADS, IDX_DIM), pos)
        ki = _rope(ki[:, :, None, :], pos)[:, :, 0]
        wi = wi * (IDX_HEADS ** -0.5 * IDX_DIM ** -0.5)
        yD = _rms_norm(_dsa_attn(qd, kd, vd, qi, ki, wi), ds_out_g[l].reshape(H, d)).reshape(B, S, GROUP_W)

        router = jnp.pad(jnp.concatenate([moe_wg[l], moe_we[l]], 1),
                         ((0, 0), (0, ROUTER_PAD - N_GROUPS - N_EXPERTS)))
        r_hi, r_lo = _split_bf16(router)
        x1, h2, logits = _out_proj((yA, yB, yC, yD), x, gt1, sc2, sh2, norm2_g[l][None],
                                   w_out[l].astype(BF16), r_hi, r_lo)

        moe = _hier_moe(h2.reshape(B * S, D), logits.reshape(B * S, ROUTER_PAD), moe_bg[l], moe_be[l],
                        moe_w1[l].astype(BF16), moe_w3[l].astype(BF16), moe_w2[l].astype(BF16))
        x = x1 + gt2 * moe.reshape(B, S, D)
    return x
```

```python
import functools

import jax
import jax.numpy as jnp
import numpy as np
from jax import lax
from jax.experimental import pallas as pl
from jax.experimental.pallas import tpu as pltpu

F32 = jnp.float32
BF16 = jnp.bfloat16

D_MODEL = 1024
N_MIXERS = 4
GROUP_W = D_MODEL // N_MIXERS
HEAD_DIM = 64
N_HEADS = GROUP_W // HEAD_DIM
NORM_EPS = 1e-6
RWKV_W_RANK = 32
RWKV_A_RANK = 32
RWKV_G_RANK = 64
RWKV_GN_EPS = 64e-5
SB_BLOCK = 128
ML_CHUNK = 64
ML_CONV = 4
GATE_CAP = 15.0
DSA_BLOCK = 128
IDX_HEADS = 4
IDX_DIM = 32
TOPK_MAX = 256
ROPE_THETA = 10000.0
N_GROUPS = 4
EXP_PER_GROUP = 8
N_EXPERTS = N_GROUPS * EXP_PER_GROUP
EXPERT_FF = D_MODEL // 2
TOP_IN_GROUP = 2

A_SIZES = (GROUP_W, GROUP_W, GROUP_W, RWKV_W_RANK, RWKV_A_RANK, RWKV_G_RANK)
B_SIZES = (GROUP_W, GROUP_W, GROUP_W)
C_SIZES = (GROUP_W, GROUP_W, GROUP_W, GROUP_W, N_HEADS, N_HEADS)
D_SIZES = (GROUP_W, HEAD_DIM, HEAD_DIM, IDX_HEADS * IDX_DIM, IDX_DIM, IDX_HEADS)
A_COLS = sum(A_SIZES)
B_COLS = sum(B_SIZES)
C_COLS = sum(C_SIZES)
D_COLS = sum(D_SIZES)

LANE = 128
A_PAD = 896
B_PAD = 768
C_PAD = 1152
D_PAD = 640
P_PAD = A_PAD + B_PAD + C_PAD + D_PAD
ROUTER_PAD = LANE

IN_ROWS = 256
OUT_ROWS = 512
MOE_ROWS = 256
VMEM_LIMIT = 48 * 1024 * 1024


def _split_cols(t, sizes):
    return jnp.split(t, [int(i) for i in np.cumsum(sizes)[:-1]], axis=-1)


def _in_proj_kernel(x_ref, sc_ref, sh_ref, g_ref, w_ref, oa_ref, ob_ref, oc_ref, od_ref):
    x = x_ref[0]
    y = x * lax.rsqrt(jnp.mean(x * x, -1, keepdims=True) + NORM_EPS) * g_ref[...]
    h = y * (1.0 + sc_ref[0]) + sh_ref[0]
    p = jnp.dot(h.astype(BF16), w_ref[...], preferred_element_type=F32)
    oa_ref[0] = p[:, :A_PAD]
    ob_ref[0] = p[:, A_PAD:A_PAD + B_PAD]
    oc_ref[0] = p[:, A_PAD + B_PAD:A_PAD + B_PAD + C_PAD]
    od_ref[0] = p[:, A_PAD + B_PAD + C_PAD:]


def _in_proj(x, sc, sh, g, w_pad):
    B, S, D = x.shape
    row = lambda w: pl.BlockSpec((1, IN_ROWS, w), lambda b, i: (b, i, 0))
    vec = pl.BlockSpec((1, 1, D), lambda b, i: (b, 0, 0))
    return pl.pallas_call(
        _in_proj_kernel,
        grid=(B, S // IN_ROWS),
        in_specs=[row(D), vec, vec, pl.BlockSpec((1, D), lambda b, i: (0, 0)),
                  pl.BlockSpec((D, P_PAD), lambda b, i: (0, 0))],
        out_specs=[row(A_PAD), row(B_PAD), row(C_PAD), row(D_PAD)],
        out_shape=[jax.ShapeDtypeStruct((B, S, w), F32) for w in (A_PAD, B_PAD, C_PAD, D_PAD)],
        compiler_params=pltpu.CompilerParams(dimension_semantics=("parallel", "parallel"),
                                             vmem_limit_bytes=VMEM_LIMIT),
        name="in_proj",
    )(x, sc, sh, g, w_pad)


def _pad_w_in(w):
    wa, wb, wc, wd = _split_cols(w, (A_COLS, B_COLS, C_COLS, D_COLS))
    padc = lambda t, n: jnp.pad(t, ((0, 0), (0, n - t.shape[1])))
    return jnp.concatenate([padc(wa, A_PAD), padc(wb, B_PAD), padc(wc, C_PAD), padc(wd, D_PAD)], 1).astype(BF16)


def _split_bf16(t):
    hi = t.astype(BF16)
    lo = (t - hi.astype(F32)).astype(BF16)
    return hi, lo


def _out_proj_kernel(ya_ref, yb_ref, yc_ref, yd_ref, x_ref, gt_ref, sc_ref, sh_ref, g_ref, w_ref,
                     rhi_ref, rlo_ref, x1_ref, h2_ref, lg_ref):
    acc = jnp.zeros(x_ref.shape[1:], F32)
    for n, y_ref in enumerate((ya_ref, yb_ref, yc_ref, yd_ref)):
        acc += jnp.dot(y_ref[0].astype(BF16), w_ref[n * GROUP_W:(n + 1) * GROUP_W, :],
                       preferred_element_type=F32)
    x1 = x_ref[0] + gt_ref[0] * acc
    x1_ref[0] = x1
    y = x1 * lax.rsqrt(jnp.mean(x1 * x1, -1, keepdims=True) + NORM_EPS) * g_ref[...]
    h = y * (1.0 + sc_ref[0]) + sh_ref[0]
    hi, lo = _split_bf16(h)
    h2_ref[0] = hi
    lg_ref[0] = (jnp.dot(hi, rhi_ref[...], preferred_element_type=F32)
                 + jnp.dot(lo, rhi_ref[...], preferred_element_type=F32)
                 + jnp.dot(hi, rlo_ref[...], preferred_element_type=F32))


def _out_proj(ys, x, gt, sc, sh, g, w_out, r_hi, r_lo):
    B, S, D = x.shape
    row = lambda w: pl.BlockSpec((1, OUT_ROWS, w), lambda b, i: (b, i, 0))
    vec = pl.BlockSpec((1, 1, D), lambda b, i: (b, 0, 0))
    full = lambda a: pl.BlockSpec(a.shape, lambda b, i: (0,) * a.ndim)
    return pl.pallas_call(
        _out_proj_kernel,
        grid=(B, S // OUT_ROWS),
        in_specs=[row(GROUP_W)] * 4 + [row(D), vec, vec, vec, full(g), full(w_out), full(r_hi), full(r_lo)],
        out_specs=[row(D), row(D), row(ROUTER_PAD)],
        out_shape=[jax.ShapeDtypeStruct((B, S, D), F32), jax.ShapeDtypeStruct((B, S, D), BF16),
                   jax.ShapeDtypeStruct((B, S, ROUTER_PAD), F32)],
        compiler_params=pltpu.CompilerParams(dimension_semantics=("parallel", "parallel"),
                                             vmem_limit_bytes=VMEM_LIMIT),
        name="out_proj",
    )(*ys, x, gt, sc, sh, g, w_out, r_hi, r_lo)


def _moe_ffn_kernel(blk_e_ref, x_ref, wt_ref, w1_ref, w3_ref, w2_ref, o_ref):
    del blk_e_ref
    xb = x_ref[...]
    a = jnp.dot(xb, w1_ref[0], preferred_element_type=F32)
    b = jnp.dot(xb, w3_ref[0], preferred_element_type=F32)
    hmid = (a * jax.nn.sigmoid(a) * b).astype(BF16)
    y = jnp.dot(hmid, w2_ref[0], preferred_element_type=F32)
    o_ref[...] = y * wt_ref[...]


def _moe_ffn(blk_e, xs, wt, w1, w3, w2):
    n_slots, D = xs.shape
    n_blocks = n_slots // MOE_ROWS
    FF = w1.shape[-1]
    return pl.pallas_call(
        _moe_ffn_kernel,
        grid_spec=pltpu.PrefetchScalarGridSpec(
            num_scalar_prefetch=1,
            grid=(n_blocks,),
            in_specs=[pl.BlockSpec((MOE_ROWS, D), lambda i, e: (i, 0)),
                      pl.BlockSpec((MOE_ROWS, 1), lambda i, e: (i, 0)),
                      pl.BlockSpec((1, D, FF), lambda i, e: (e[i], 0, 0)),
                      pl.BlockSpec((1, D, FF), lambda i, e: (e[i], 0, 0)),
                      pl.BlockSpec((1, FF, D), lambda i, e: (e[i], 0, 0))],
            out_specs=pl.BlockSpec((MOE_ROWS, D), lambda i, e: (i, 0)),
        ),
        out_shape=jax.ShapeDtypeStruct((n_slots, D), F32),
        compiler_params=pltpu.CompilerParams(dimension_semantics=("arbitrary",),
                                             vmem_limit_bytes=VMEM_LIMIT),
        name="moe_ffn",
    )(blk_e, xs, wt, w1, w3, w2)


def _hier_moe(h2, logits, bg, be, w1, w3, w2):
    N, D = h2.shape
    g_prob = jax.nn.softmax(logits[:, :N_GROUPS] + bg, -1)
    g_p, g_idx = lax.top_k(g_prob, 1)
    e_logits = (logits[:, N_GROUPS:N_GROUPS + N_EXPERTS] + be).reshape(N, N_GROUPS, EXP_PER_GROUP)
    e_logits = jnp.take_along_axis(e_logits, g_idx[:, :, None], axis=1)[:, 0]
    e_p, e_idx = lax.top_k(jax.nn.softmax(e_logits, -1), TOP_IN_GROUP)
    gate = g_p * e_p / jnp.sum(e_p, -1, keepdims=True)
    expert = g_idx * EXP_PER_GROUP + e_idx
    n_asg = N * TOP_IN_GROUP
    flat_e = expert.reshape(n_asg)
    order = jnp.argsort(flat_e)
    se = flat_e[order]
    counts = jnp.bincount(flat_e, length=N_EXPERTS)
    start = jnp.cumsum(counts) - counts
    pad_counts = (counts + MOE_ROWS - 1) // MOE_ROWS * MOE_ROWS
    pad_end = jnp.cumsum(pad_counts)
    pad_start = pad_end - pad_counts
    slot = (pad_start[se] + jnp.arange(n_asg) - start[se]).astype(jnp.int32)
    n_blocks = n_asg // MOE_ROWS + N_EXPERTS
    n_slots = n_blocks * MOE_ROWS
    slot_tok = jnp.zeros((n_slots,), jnp.int32).at[slot].set((order // TOP_IN_GROUP).astype(jnp.int32))
    slot_w = jnp.zeros((n_slots,), F32).at[slot].set(gate.reshape(n_asg)[order])
    blk_start = jnp.arange(n_blocks) * MOE_ROWS
    blk_e = jnp.minimum(jnp.sum(pad_end[None, :] <= blk_start[:, None], 1), N_EXPERTS - 1).astype(jnp.int32)
    yb = _moe_ffn(blk_e, h2[slot_tok], slot_w[:, None], w1, w3, w2)
    asg_slot = jnp.zeros((n_asg,), jnp.int32).at[order].set(slot).reshape(N, TOP_IN_GROUP)
    return yb[asg_slot[:, 0]] + yb[asg_slot[:, 1]]


INT_MIN = -2 ** 31
DSA_KEY_STEP = 512


def _float_order_key(x):
    bits = pltpu.bitcast(x, jnp.int32)
    bits = jnp.where(x == 0.0, 0, bits)
    return bits ^ ((bits >> 31) & 0x7FFFFFFF)


def _row_count(mask):
    return jnp.sum(jnp.where(mask, 1.0, 0.0), axis=1, keepdims=True)


def _dsa_block(qd_ref, kdt_ref, vdw_ref, qi_ref, kit_ref, wi_ref, g_ref, o_ref, *, kl, n_sel):
    q0 = pl.program_id(1) * DSA_BLOCK
    kit = kit_ref[0, :, :kl]
    ki4 = jnp.concatenate([kit] * IDX_HEADS, axis=0)
    k_hi, k_lo = _split_bf16(ki4)
    qi = qi_ref[0]
    wi = wi_ref[0]
    lane_i = lax.broadcasted_iota(jnp.int32, qi.shape, 1)
    score = jnp.zeros((DSA_BLOCK, kl), F32)
    for h in range(IDX_HEADS):
        q_hi, q_lo = _split_bf16(jnp.where(lane_i // IDX_DIM == h, qi, 0.0))
        sc = (jnp.dot(q_hi, k_hi, preferred_element_type=F32) + jnp.dot(q_lo, k_hi, preferred_element_type=F32)
              + jnp.dot(q_hi, k_lo, preferred_element_type=F32))
        score = score + wi[:, h:h + 1] * jnp.maximum(sc, 0.0)
    kidx = lax.broadcasted_iota(jnp.int32, (DSA_BLOCK, kl), 1)
    qpos = q0 + lax.broadcasted_iota(jnp.int32, (DSA_BLOCK, kl), 0)
    adm = kidx <= qpos
    key = _float_order_key(jnp.where(adm, score, -jnp.inf))

    def value_bit(it, tau):
        cand = tau | jnp.left_shift(jnp.int32(1), 31 - it)
        cnt = _row_count(key >= (cand ^ INT_MIN))
        return jnp.where(cnt >= n_sel, cand, tau)

    tau = lax.fori_loop(0, 32, value_bit, jnp.zeros((DSA_BLOCK, 1), jnp.int32)) ^ INT_MIN
    gt = key > tau
    eq = (key == tau) & adm
    need = n_sel - _row_count(gt)
    n_eq = _row_count(eq)

    def index_bits():
        def index_bit(it, bound):
            cand = bound | jnp.left_shift(jnp.int32(1), 11 - it)
            cnt = _row_count(eq & (kidx < cand))
            return jnp.where(cnt <= need, cand, bound)
        return lax.fori_loop(0, 12, index_bit, jnp.zeros((DSA_BLOCK, 1), jnp.int32))

    bound = lax.cond(jnp.max(n_eq - need) > 0.0, index_bits,
                     lambda: jnp.full((DSA_BLOCK, 1), kl, jnp.int32))
    sel = gt | (eq & (kidx < bound))

    kdt = kdt_ref[0, :, :kl]
    kd4 = jnp.concatenate([kdt] * N_HEADS, axis=0).astype(BF16)
    vdw = vdw_ref[0, :kl, :].astype(BF16)
    qd = qd_ref[0]
    lane_h = lax.broadcasted_iota(jnp.int32, qd.shape, 1) // HEAD_DIM
    out = jnp.zeros(qd.shape, F32)
    for h in range(N_HEADS):
        qm = jnp.where(lane_h == h, qd, 0.0).astype(BF16)
        lg = jnp.dot(qm, kd4, preferred_element_type=F32) * HEAD_DIM ** -0.5
        lg = jnp.where(sel, lg, -jnp.inf)
        p = jnp.exp(lg - jnp.max(lg, axis=1, keepdims=True))
        r = jnp.dot(p.astype(BF16), vdw, preferred_element_type=F32) / jnp.sum(p, axis=1, keepdims=True)
        r = r * lax.rsqrt(jnp.mean(r * r, axis=1, keepdims=True) + NORM_EPS)
        out = out + jnp.where(lane_h == h, r, 0.0)
    o_ref[0] = out * g_ref[...]


def _dsa_kernel(qd_ref, kdt_ref, vdw_ref, qi_ref, kit_ref, wi_ref, g_ref, o_ref, *, kls, n_sel):
    blocks_per_step = DSA_KEY_STEP // DSA_BLOCK
    for j, kl in enumerate(kls):
        @pl.when(pl.program_id(1) // blocks_per_step == j)
        def _():
            _dsa_block(qd_ref, kdt_ref, vdw_ref, qi_ref, kit_ref, wi_ref, g_ref, o_ref, kl=kl, n_sel=n_sel)


def _dsa_attn_norm(qd, kd, vd, qi, ki, wi, g):
    B, S, _ = qd.shape
    n_sel = min(TOPK_MAX, S // 4)
    assert S % DSA_KEY_STEP == 0 and n_sel <= DSA_KEY_STEP
    kls = tuple(range(DSA_KEY_STEP, S + 1, DSA_KEY_STEP))
    kdt = jnp.swapaxes(kd, 1, 2)
    kit = jnp.swapaxes(ki, 1, 2)
    vdw = jnp.tile(vd, (1, 1, N_HEADS))
    blk = lambda w: pl.BlockSpec((1, DSA_BLOCK, w), lambda b, i: (b, i, 0))
    per_b = lambda r, c: pl.BlockSpec((1, r, c), lambda b, i: (b, 0, 0))
    return pl.pallas_call(
        functools.partial(_dsa_kernel, kls=kls, n_sel=n_sel),
        grid=(B, S // DSA_BLOCK),
        in_specs=[blk(GROUP_W), per_b(HEAD_DIM, S), per_b(S, GROUP_W), blk(IDX_HEADS * IDX_DIM),
                  per_b(IDX_DIM, S), blk(IDX_HEADS), pl.BlockSpec((1, GROUP_W), lambda b, i: (0, 0))],
        out_specs=blk(GROUP_W),
        out_shape=jax.ShapeDtypeStruct((B, S, GROUP_W), F32),
        compiler_params=pltpu.CompilerParams(dimension_semantics=("parallel", "parallel"),
                                             vmem_limit_bytes=VMEM_LIMIT),
        name="dsa_attn",
    )(qd, kdt, vdw, qi, kit, wi, g[None])


def _rms_norm(x, g):
    xf = x.astype(F32)
    y = xf * lax.rsqrt(jnp.mean(xf * xf, -1, keepdims=True) + NORM_EPS)
    return (y * g.astype(F32)).astype(x.dtype)


def _rope(x, pos):
    half = x.shape[-1] // 2
    inv = ROPE_THETA ** (-jnp.arange(half, dtype=F32) / half)
    ang = pos.astype(F32)[:, None] * inv[None, :]
    cos = jnp.cos(ang)[None, :, None, :]
    sin = jnp.sin(ang)[None, :, None, :]
    xf = x.astype(F32)
    x1, x2 = xf[..., :half], xf[..., half:]
    return jnp.concatenate([x1 * cos - x2 * sin, x2 * cos + x1 * sin], -1).astype(x.dtype)


def _token_shift(p, mu):
    prev = jnp.pad(p, ((0, 0), (1, 0), (0, 0)))[:, :-1]
    return p + (prev - p) * mu


def _causal_dwconv(x, w, b):
    ch = x.shape[-1]
    y = lax.conv_general_dilated(x, w[:, None, :].astype(x.dtype), window_strides=(1,),
                                 padding=((w.shape[0] - 1, 0),),
                                 dimension_numbers=('NWC', 'WIO', 'NWC'), feature_group_count=ch)
    return y + b


def _rwkv7_time_mix(p, mu, w0, w2, a0, a2, g2, k_k, k_a, r_k, ln_g, ln_b):
    B, S, _ = p.shape
    H, d = N_HEADS, HEAD_DIM
    p = _token_shift(p, mu)
    r, k, v, wd, ad, gd = _split_cols(p, A_SIZES)
    w = -jax.nn.softplus(-(w0 + jnp.tanh(wd) @ w2)) - 0.5
    decay = jnp.exp(-jnp.exp(w.astype(F32)))
    a = jax.nn.sigmoid(a0 + ad @ a2)
    g = jax.nn.sigmoid(gd) @ g2
    kk = (k * k_k).astype(F32).reshape(B, S, H, d)
    kk = kk / jnp.maximum(jnp.sqrt(jnp.sum(kk * kk, -1, keepdims=True)), 1e-12)
    k = k * (1 + (a - 1) * k_a)
    heads = lambda t: t.astype(F32).reshape(B, S, H, d)
    r_h, k_h, v_h, w_h, a_h = heads(r), heads(k), heads(v), heads(decay), heads(a)

    def step(state, inp):
        rt, wt, kt, vt, kkt, at = inp
        sa = jnp.einsum('bhvk,bhk->bhv', state, -kkt)
        state = (state * wt[:, :, None, :] + sa[..., None] * (kkt * at)[:, :, None, :]
                 + vt[..., None] * kt[:, :, None, :])
        return state, jnp.einsum('bhvk,bhk->bhv', state, rt)

    xs = tuple(jnp.moveaxis(t, 1, 0) for t in (r_h, w_h, k_h, v_h, kk, a_h))
    _, y = lax.scan(step, jnp.zeros((B, H, d, d), F32), xs)
    y = jnp.moveaxis(y, 0, 1)
    mean = jnp.mean(y, -1, keepdims=True)
    var = jnp.mean(jnp.square(y - mean), -1, keepdims=True)
    y = (y - mean) * lax.rsqrt(var + RWKV_GN_EPS) * ln_g.astype(F32).reshape(H, d) + ln_b.astype(F32).reshape(H, d)
    bonus = jnp.sum(r_h * k_h * r_k.astype(F32).reshape(H, d), -1, keepdims=True) * v_h
    y = (y + bonus).reshape(B, S, GROUP_W) * g.astype(F32)
    return y.astype(p.dtype)


def _stick_breaking_attn(q, k, v):
    B, S, H, d = q.shape
    scale = d ** -0.5
    outs = []
    for i in range(S // SB_BLOCK):
        q0 = i * SB_BLOCK
        kl = q0 + SB_BLOCK
        z = jnp.einsum('bqhd,bkhd->bhqk', q[:, q0:kl], k[:, :kl]).astype(F32) * scale
        t_idx = q0 + jnp.arange(SB_BLOCK)[:, None]
        s_idx = jnp.arange(kl)[None, :]
        mask = s_idx < t_idx
        log1m = jnp.where(mask, jax.nn.log_sigmoid(-z), 0.0)
        cs = jnp.cumsum(log1m, -1)
        log_a = jax.nn.log_sigmoid(z) + cs[..., -1:] - cs
        att = jnp.where(mask, jnp.exp(log_a), 0.0)
        outs.append(jnp.einsum('bhqk,bkhd->bqhd', att.astype(v.dtype), v[:, :kl]))
    return jnp.concatenate(outs, 1)


def _mlstm_chunkwise(q, k, v, log_i, log_f):
    B, S, H, d = q.shape
    L = ML_CHUNK
    nc = S // L

    def to_chunks(t):
        t = t.reshape((B, nc, L, H) + t.shape[3:])
        return jnp.moveaxis(jnp.moveaxis(t, 3, 2), 1, 0)

    causal = jnp.tril(jnp.ones((L, L), bool))

    def chunk(carry, inp):
        c_st, n_st, m_st = carry
        qc, kc, vc, li, lf = inp
        b = jnp.cumsum(lf, -1)
        dmat = jnp.where(causal, b[..., :, None] - b[..., None, :] + li[..., None, :], -jnp.inf)
        g_inter = b + m_st[..., None]
        m_t = jnp.maximum(g_inter, jnp.max(dmat, -1))
        s_inter = jnp.exp(g_inter - m_t)
        sqk = jnp.einsum('bhtd,bhsd->bhts', qc, kc) * jnp.exp(dmat - m_t[..., None])
        num = s_inter[..., None] * jnp.einsum('bhvd,bhtd->bhtv', c_st, qc) + jnp.einsum('bhts,bhsv->bhtv', sqk, vc)
        den = s_inter * jnp.einsum('bhd,bhtd->bht', n_st, qc) + jnp.sum(sqk, -1)
        h = num / jnp.maximum(jnp.abs(den), jnp.exp(-m_t))[..., None]
        b_last = b[..., -1]
        dec = b_last[..., None] - b + li
        m_new = jnp.maximum(b_last + m_st, jnp.max(dec, -1))
        wk = jnp.exp(dec - m_new[..., None])
        s_old = jnp.exp(b_last + m_st - m_new)
        c_st = s_old[..., None, None] * c_st + jnp.einsum('bhs,bhsv,bhsd->bhvd', wk, vc, kc)
        n_st = s_old[..., None] * n_st + jnp.einsum('bhs,bhsd->bhd', wk, kc)
        return (c_st, n_st, m_new), h

    init = (jnp.zeros((B, H, d, d), F32), jnp.zeros((B, H, d), F32), jnp.zeros((B, H), F32))
    _, hs = lax.scan(chunk, init, tuple(to_chunks(t) for t in (q, k, v, log_i, log_f)))
    hs = jnp.moveaxis(jnp.moveaxis(hs, 0, 1), 2, 3)
    return hs.reshape(B, S, H, d)


def _mlstm_mix(p, conv_w, conv_b, ig_b, fg_b, norm_g):
    B, S, _ = p.shape
    H, d = N_HEADS, HEAD_DIM
    qk = jax.nn.silu(_causal_dwconv(p[..., :2 * GROUP_W], conv_w, conv_b))
    q, k = qk[..., :GROUP_W], qk[..., GROUP_W:]
    _, _, v, o, ig, fg = _split_cols(p, C_SIZES)
    cap = lambda t: GATE_CAP * jnp.tanh(t / GATE_CAP)
    log_i = cap((ig + ig_b).astype(F32))
    log_f = jax.nn.log_sigmoid(cap((fg + fg_b).astype(F32)))
    heads = lambda t: t.astype(F32).reshape(B, S, H, d)
    h = _mlstm_chunkwise(heads(q), heads(k) * d ** -0.5, heads(v), log_i, log_f)
    h = _rms_norm(h, norm_g.reshape(H, d)).reshape(B, S, GROUP_W)
    return (jax.nn.sigmoid(o.astype(F32)) * h).astype(p.dtype)


def kernel(x, c, ada_w, ada_b, norm1_g, norm2_g, w_in, rk_mu, rk_w0, rk_w2, rk_a0, rk_a2, rk_g2, rk_kk, rk_ka, rk_rk, rk_ln_g, rk_ln_b, sb_norm_g, ml_conv_w, ml_conv_b, ml_ig_b, ml_fg_b, ml_norm_g, ds_qn_g, ds_kn_g, ds_out_g, w_out, moe_wg, moe_bg, moe_we, moe_be, moe_w1, moe_w3, moe_w2):
    B, S, D = x.shape
    H, d = N_HEADS, HEAD_DIM
    depth = ada_w.shape[0]
    pos = jnp.arange(S)
    c_act = jax.nn.silu(c)
    for l in range(depth):
        mod = (c_act @ ada_w[l] + ada_b[l])[:, None, :]
        sh1, sc1, gt1, sh2, sc2, gt2 = jnp.split(mod, 6, axis=-1)

        pA, pB, pC, pD = _in_proj(x, sc1, sh1, norm1_g[l][None], _pad_w_in(w_in[l]))
        pC = pC[..., :C_COLS]
        pD = pD[..., :D_COLS]

        yA = _rwkv7_time_mix(pA, rk_mu[l], rk_w0[l], rk_w2[l], rk_a0[l], rk_a2[l], rk_g2[l],
                             rk_kk[l], rk_ka[l], rk_rk[l], rk_ln_g[l], rk_ln_b[l])

        qb, kb, vb = (t.reshape(B, S, H, d) for t in _split_cols(pB, B_SIZES))
        yB = _rms_norm(_stick_breaking_attn(qb, kb, vb), sb_norm_g[l].reshape(H, d)).reshape(B, S, GROUP_W)

        yC = _mlstm_mix(pC, ml_conv_w[l], ml_conv_b[l], ml_ig_b[l], ml_fg_b[l], ml_norm_g[l])

        qd, kd, vd, qi, ki, wi = _split_cols(pD, D_SIZES)
        qd = _rope(_rms_norm(qd.reshape(B, S, H, d), ds_qn_g[l]), pos).reshape(B, S, GROUP_W)
        kd = _rope(_rms_norm(kd[:, :, None, :], ds_kn_g[l]), pos)[:, :, 0]
        qi = _rope(qi.reshape(B, S, IDX_HEADS, IDX_DIM), pos).reshape(B, S, IDX_HEADS * IDX_DIM)
        ki = _rope(ki[:, :, None, :], pos)[:, :, 0]
        wi = wi * (IDX_HEADS ** -0.5 * IDX_DIM ** -0.5)
        yD = _dsa_attn_norm(qd, kd, vd, qi, ki, wi, ds_out_g[l])

        router = jnp.pad(jnp.concatenate([moe_wg[l], moe_we[l]], 1),
                         ((0, 0), (0, ROUTER_PAD - N_GROUPS - N_EXPERTS)))
        r_hi, r_lo = _split_bf16(router)
        x1, h2, logits = _out_proj((yA, yB, yC, yD), x, gt1, sc2, sh2, norm2_g[l][None],
                                   w_out[l].astype(BF16), r_hi, r_lo)

        moe = _hier_moe(h2.reshape(B * S, D), logits.reshape(B * S, ROUTER_PAD), moe_bg[l], moe_be[l],
                        moe_w1[l].astype(BF16), moe_w3[l].astype(BF16), moe_w2[l].astype(BF16))
        x = x1 + gt2 * moe.reshape(B, S, D)
    return x
```

```python
import functools

import jax
import jax.numpy as jnp
import numpy as np
from jax import lax
from jax.experimental import pallas as pl
from jax.experimental.pallas import tpu as pltpu

F32 = jnp.float32
BF16 = jnp.bfloat16

D_MODEL = 1024
N_MIXERS = 4
GROUP_W = D_MODEL // N_MIXERS
HEAD_DIM = 64
N_HEADS = GROUP_W // HEAD_DIM
NORM_EPS = 1e-6
RWKV_W_RANK = 32
RWKV_A_RANK = 32
RWKV_G_RANK = 64
RWKV_GN_EPS = 64e-5
SB_BLOCK = 128
ML_CHUNK = 64
ML_CONV = 4
GATE_CAP = 15.0
DSA_BLOCK = 128
IDX_HEADS = 4
IDX_DIM = 32
TOPK_MAX = 256
ROPE_THETA = 10000.0
N_GROUPS = 4
EXP_PER_GROUP = 8
N_EXPERTS = N_GROUPS * EXP_PER_GROUP
EXPERT_FF = D_MODEL // 2
TOP_IN_GROUP = 2

A_SIZES = (GROUP_W, GROUP_W, GROUP_W, RWKV_W_RANK, RWKV_A_RANK, RWKV_G_RANK)
B_SIZES = (GROUP_W, GROUP_W, GROUP_W)
C_SIZES = (GROUP_W, GROUP_W, GROUP_W, GROUP_W, N_HEADS, N_HEADS)
D_SIZES = (GROUP_W, HEAD_DIM, HEAD_DIM, IDX_HEADS * IDX_DIM, IDX_DIM, IDX_HEADS)
A_COLS = sum(A_SIZES)
B_COLS = sum(B_SIZES)
C_COLS = sum(C_SIZES)
D_COLS = sum(D_SIZES)

LANE = 128
A_PAD = 896
B_PAD = 768
C_PAD = 1152
D_PAD = 640
P_PAD = A_PAD + B_PAD + C_PAD + D_PAD
ROUTER_PAD = LANE

IN_ROWS = 256
OUT_ROWS = 512
MOE_ROWS = 256
VMEM_LIMIT = 48 * 1024 * 1024


def _split_cols(t, sizes):
    return jnp.split(t, [int(i) for i in np.cumsum(sizes)[:-1]], axis=-1)


def _in_proj_kernel(x_ref, sc_ref, sh_ref, g_ref, w_ref, oa_ref, ob_ref, oc_ref, od_ref):
    x = x_ref[0]
    y = x * lax.rsqrt(jnp.mean(x * x, -1, keepdims=True) + NORM_EPS) * g_ref[...]
    h = y * (1.0 + sc_ref[0]) + sh_ref[0]
    p = jnp.dot(h.astype(BF16), w_ref[...], preferred_element_type=F32)
    oa_ref[0] = p[:, :A_PAD]
    ob_ref[0] = p[:, A_PAD:A_PAD + B_PAD]
    oc_ref[0] = p[:, A_PAD + B_PAD:A_PAD + B_PAD + C_PAD]
    od_ref[0] = p[:, A_PAD + B_PAD + C_PAD:]


def _in_proj(x, sc, sh, g, w_pad):
    B, S, D = x.shape
    row = lambda w: pl.BlockSpec((1, IN_ROWS, w), lambda b, i: (b, i, 0))
    vec = pl.BlockSpec((1, 1, D), lambda b, i: (b, 0, 0))
    return pl.pallas_call(
        _in_proj_kernel,
        grid=(B, S // IN_ROWS),
        in_specs=[row(D), vec, vec, pl.BlockSpec((1, D), lambda b, i: (0, 0)),
                  pl.BlockSpec((D, P_PAD), lambda b, i: (0, 0))],
        out_specs=[row(A_PAD), row(B_PAD), row(C_PAD), row(D_PAD)],
        out_shape=[jax.ShapeDtypeStruct((B, S, w), F32) for w in (A_PAD, B_PAD, C_PAD, D_PAD)],
        compiler_params=pltpu.CompilerParams(dimension_semantics=("parallel", "parallel"),
                                             vmem_limit_bytes=VMEM_LIMIT),
        name="in_proj",
    )(x, sc, sh, g, w_pad)


def _pad_w_in(w):
    wa, wb, wc, wd = _split_cols(w, (A_COLS, B_COLS, C_COLS, D_COLS))
    padc = lambda t, n: jnp.pad(t, ((0, 0), (0, n - t.shape[1])))
    return jnp.concatenate([padc(wa, A_PAD), padc(wb, B_PAD), padc(wc, C_PAD), padc(wd, D_PAD)], 1).astype(BF16)


def _split_bf16(t):
    hi = t.astype(BF16)
    lo = (t - hi.astype(F32)).astype(BF16)
    return hi, lo


def _out_proj_kernel(ya_ref, yb_ref, yc_ref, yd_ref, x_ref, gt_ref, sc_ref, sh_ref, g_ref, w_ref,
                     rhi_ref, rlo_ref, x1_ref, h2_ref, lg_ref):
    acc = jnp.zeros(x_ref.shape[1:], F32)
    for n, y_ref in enumerate((ya_ref, yb_ref, yc_ref, yd_ref)):
        acc += jnp.dot(y_ref[0].astype(BF16), w_ref[n * GROUP_W:(n + 1) * GROUP_W, :],
                       preferred_element_type=F32)
    x1 = x_ref[0] + gt_ref[0] * acc
    x1_ref[0] = x1
    y = x1 * lax.rsqrt(jnp.mean(x1 * x1, -1, keepdims=True) + NORM_EPS) * g_ref[...]
    h = y * (1.0 + sc_ref[0]) + sh_ref[0]
    hi, lo = _split_bf16(h)
    h2_ref[0] = hi
    lg_ref[0] = (jnp.dot(hi, rhi_ref[...], preferred_element_type=F32)
                 + jnp.dot(lo, rhi_ref[...], preferred_element_type=F32)
                 + jnp.dot(hi, rlo_ref[...], preferred_element_type=F32))


def _out_proj(ys, x, gt, sc, sh, g, w_out, r_hi, r_lo):
    B, S, D = x.shape
    row = lambda w: pl.BlockSpec((1, OUT_ROWS, w), lambda b, i: (b, i, 0))
    vec = pl.BlockSpec((1, 1, D), lambda b, i: (b, 0, 0))
    full = lambda a: pl.BlockSpec(a.shape, lambda b, i: (0,) * a.ndim)
    return pl.pallas_call(
        _out_proj_kernel,
        grid=(B, S // OUT_ROWS),
        in_specs=[row(GROUP_W)] * 4 + [row(D), vec, vec, vec, full(g), full(w_out), full(r_hi), full(r_lo)],
        out_specs=[row(D), row(D), row(ROUTER_PAD)],
        out_shape=[jax.ShapeDtypeStruct((B, S, D), F32), jax.ShapeDtypeStruct((B, S, D), BF16),
                   jax.ShapeDtypeStruct((B, S, ROUTER_PAD), F32)],
        compiler_params=pltpu.CompilerParams(dimension_semantics=("parallel", "parallel"),
                                             vmem_limit_bytes=VMEM_LIMIT),
        name="out_proj",
    )(*ys, x, gt, sc, sh, g, w_out, r_hi, r_lo)


def _moe_ffn_kernel(blk_e_ref, x_ref, wt_ref, w1_ref, w3_ref, w2_ref, o_ref):
    del blk_e_ref
    xb = x_ref[...]
    a = jnp.dot(xb, w1_ref[0], preferred_element_type=F32)
    b = jnp.dot(xb, w3_ref[0], preferred_element_type=F32)
    hmid = (a * jax.nn.sigmoid(a) * b).astype(BF16)
    y = jnp.dot(hmid, w2_ref[0], preferred_element_type=F32)
    o_ref[...] = y * wt_ref[...]


def _moe_ffn(blk_e, xs, wt, w1, w3, w2):
    n_slots, D = xs.shape
    n_blocks = n_slots // MOE_ROWS
    FF = w1.shape[-1]
    return pl.pallas_call(
        _moe_ffn_kernel,
        grid_spec=pltpu.PrefetchScalarGridSpec(
            num_scalar_prefetch=1,
            grid=(n_blocks,),
            in_specs=[pl.BlockSpec((MOE_ROWS, D), lambda i, e: (i, 0)),
                      pl.BlockSpec((MOE_ROWS, 1), lambda i, e: (i, 0)),
                      pl.BlockSpec((1, D, FF), lambda i, e: (e[i], 0, 0)),
                      pl.BlockSpec((1, D, FF), lambda i, e: (e[i], 0, 0)),
                      pl.BlockSpec((1, FF, D), lambda i, e: (e[i], 0, 0))],
            out_specs=pl.BlockSpec((MOE_ROWS, D), lambda i, e: (i, 0)),
        ),
        out_shape=jax.ShapeDtypeStruct((n_slots, D), F32),
        compiler_params=pltpu.CompilerParams(dimension_semantics=("arbitrary",),
                                             vmem_limit_bytes=VMEM_LIMIT),
        name="moe_ffn",
    )(blk_e, xs, wt, w1, w3, w2)


def _hier_moe(h2, logits, bg, be, w1, w3, w2):
    N, D = h2.shape
    g_prob = jax.nn.softmax(logits[:, :N_GROUPS] + bg, -1)
    g_p, g_idx = lax.top_k(g_prob, 1)
    e_logits = (logits[:, N_GROUPS:N_GROUPS + N_EXPERTS] + be).reshape(N, N_GROUPS, EXP_PER_GROUP)
    e_logits = jnp.take_along_axis(e_logits, g_idx[:, :, None], axis=1)[:, 0]
    e_p, e_idx = lax.top_k(jax.nn.softmax(e_logits, -1), TOP_IN_GROUP)
    gate = g_p * e_p / jnp.sum(e_p, -1, keepdims=True)
    expert = g_idx * EXP_PER_GROUP + e_idx
    n_asg = N * TOP_IN_GROUP
    flat_e = expert.reshape(n_asg)
    order = jnp.argsort(flat_e)
    se = flat_e[order]
    counts = jnp.bincount(flat_e, length=N_EXPERTS)
    start = jnp.cumsum(counts) - counts
    pad_counts = (counts + MOE_ROWS - 1) // MOE_ROWS * MOE_ROWS
    pad_end = jnp.cumsum(pad_counts)
    pad_start = pad_end - pad_counts
    slot = (pad_start[se] + jnp.arange(n_asg) - start[se]).astype(jnp.int32)
    n_blocks = n_asg // MOE_ROWS + N_EXPERTS
    n_slots = n_blocks * MOE_ROWS
    slot_tok = jnp.zeros((n_slots,), jnp.int32).at[slot].set((order // TOP_IN_GROUP).astype(jnp.int32))
    slot_w = jnp.zeros((n_slots,), F32).at[slot].set(gate.reshape(n_asg)[order])
    blk_start = jnp.arange(n_blocks) * MOE_ROWS
    blk_e = jnp.minimum(jnp.sum(pad_end[None, :] <= blk_start[:, None], 1), N_EXPERTS - 1).astype(jnp.int32)
    yb = _moe_ffn(blk_e, h2[slot_tok], slot_w[:, None], w1, w3, w2)
    asg_slot = jnp.zeros((n_asg,), jnp.int32).at[order].set(slot).reshape(N, TOP_IN_GROUP)
    return yb[asg_slot[:, 0]] + yb[asg_slot[:, 1]]


INT_MIN = -2 ** 31
DSA_KEY_STEP = 512


def _float_order_key(x):
    bits = pltpu.bitcast(x, jnp.int32)
    bits = jnp.where(x == 0.0, 0, bits)
    return bits ^ ((bits >> 31) & 0x7FFFFFFF)


def _row_count(mask):
    return jnp.sum(jnp.where(mask, 1.0, 0.0), axis=1, keepdims=True)


def _dsa_block(qd_ref, kdt_ref, vdw_ref, qi_ref, kit_ref, wi_ref, g_ref, o_ref, *, kl, n_sel):
    q0 = pl.program_id(1) * DSA_BLOCK
    kit = kit_ref[0, :, :kl]
    ki4 = jnp.concatenate([kit] * IDX_HEADS, axis=0)
    k_hi, k_lo = _split_bf16(ki4)
    qi = qi_ref[0]
    wi = wi_ref[0]
    lane_i = lax.broadcasted_iota(jnp.int32, qi.shape, 1)
    score = jnp.zeros((DSA_BLOCK, kl), F32)
    for h in range(IDX_HEADS):
        q_hi, q_lo = _split_bf16(jnp.where(lane_i // IDX_DIM == h, qi, 0.0))
        sc = (jnp.dot(q_hi, k_hi, preferred_element_type=F32) + jnp.dot(q_lo, k_hi, preferred_element_type=F32)
              + jnp.dot(q_hi, k_lo, preferred_element_type=F32))
        score = score + wi[:, h:h + 1] * jnp.maximum(sc, 0.0)
    kidx = lax.broadcasted_iota(jnp.int32, (DSA_BLOCK, kl), 1)
    qpos = q0 + lax.broadcasted_iota(jnp.int32, (DSA_BLOCK, kl), 0)
    adm = kidx <= qpos
    key = _float_order_key(jnp.where(adm, score, -jnp.inf))

    def value_bit(it, tau):
        cand = tau | jnp.left_shift(jnp.int32(1), 31 - it)
        cnt = _row_count(key >= (cand ^ INT_MIN))
        return jnp.where(cnt >= n_sel, cand, tau)

    tau = lax.fori_loop(0, 32, value_bit, jnp.zeros((DSA_BLOCK, 1), jnp.int32)) ^ INT_MIN
    gt = key > tau
    eq = (key == tau) & adm
    need = n_sel - _row_count(gt)
    n_eq = _row_count(eq)

    def index_bits():
        def index_bit(it, bound):
            cand = bound | jnp.left_shift(jnp.int32(1), 11 - it)
            cnt = _row_count(eq & (kidx < cand))
            return jnp.where(cnt <= need, cand, bound)
        return lax.fori_loop(0, 12, index_bit, jnp.zeros((DSA_BLOCK, 1), jnp.int32))

    bound = lax.cond(jnp.max(n_eq - need) > 0.0, index_bits,
                     lambda: jnp.full((DSA_BLOCK, 1), kl, jnp.int32))
    sel = gt | (eq & (kidx < bound))

    kdt = kdt_ref[0, :, :kl]
    kd4 = jnp.concatenate([kdt] * N_HEADS, axis=0).astype(BF16)
    vdw = vdw_ref[0, :kl, :].astype(BF16)
    qd = qd_ref[0]
    lane_h = lax.broadcasted_iota(jnp.int32, qd.shape, 1) // HEAD_DIM
    out = jnp.zeros(qd.shape, F32)
    for h in range(N_HEADS):
        qm = jnp.where(lane_h == h, qd, 0.0).astype(BF16)
        lg = jnp.dot(qm, kd4, preferred_element_type=F32) * HEAD_DIM ** -0.5
        lg = jnp.where(sel, lg, -jnp.inf)
        p = jnp.exp(lg - jnp.max(lg, axis=1, keepdims=True))
        r = jnp.dot(p.astype(BF16), vdw, preferred_element_type=F32) / jnp.sum(p, axis=1, keepdims=True)
        r = r * lax.rsqrt(jnp.mean(r * r, axis=1, keepdims=True) + NORM_EPS)
        out = out + jnp.where(lane_h == h, r, 0.0)
    o_ref[0] = out * g_ref[...]


def _dsa_kernel(qd_ref, kdt_ref, vdw_ref, qi_ref, kit_ref, wi_ref, g_ref, o_ref, *, kls, n_sel):
    blocks_per_step = DSA_KEY_STEP // DSA_BLOCK
    for j, kl in enumerate(kls):
        @pl.when(pl.program_id(1) // blocks_per_step == j)
        def _():
            _dsa_block(qd_ref, kdt_ref, vdw_ref, qi_ref, kit_ref, wi_ref, g_ref, o_ref, kl=kl, n_sel=n_sel)


def _dsa_attn_norm(qd, kd, vd, qi, ki, wi, g):
    B, S, _ = qd.shape
    n_sel = min(TOPK_MAX, S // 4)
    assert S % DSA_KEY_STEP == 0 and n_sel <= DSA_KEY_STEP
    kls = tuple(range(DSA_KEY_STEP, S + 1, DSA_KEY_STEP))
    kdt = jnp.swapaxes(kd, 1, 2)
    kit = jnp.swapaxes(ki, 1, 2)
    vdw = jnp.tile(vd, (1, 1, N_HEADS))
    blk = lambda w: pl.BlockSpec((1, DSA_BLOCK, w), lambda b, i: (b, i, 0))
    per_b = lambda r, c: pl.BlockSpec((1, r, c), lambda b, i: (b, 0, 0))
    return pl.pallas_call(
        functools.partial(_dsa_kernel, kls=kls, n_sel=n_sel),
        grid=(B, S // DSA_BLOCK),
        in_specs=[blk(GROUP_W), per_b(HEAD_DIM, S), per_b(S, GROUP_W), blk(IDX_HEADS * IDX_DIM),
                  per_b(IDX_DIM, S), blk(IDX_HEADS), pl.BlockSpec((1, GROUP_W), lambda b, i: (0, 0))],
        out_specs=blk(GROUP_W),
        out_shape=jax.ShapeDtypeStruct((B, S, GROUP_W), F32),
        compiler_params=pltpu.CompilerParams(dimension_semantics=("parallel", "parallel"),
                                             vmem_limit_bytes=VMEM_LIMIT),
        name="dsa_attn",
    )(qd, kdt, vdw, qi, kit, wi, g[None])


RWKV_CHUNK = 64
RWKV_LOW = RWKV_W_RANK + RWKV_A_RANK + RWKV_G_RANK
RWKV_STACK = N_HEADS * RWKV_CHUNK


def _dot(a, b):
    return jnp.dot(a, b, preferred_element_type=F32)


def _dot_nt(a, b):
    return lax.dot_general(a, b, (((1,), (1,)), ((), ())), preferred_element_type=F32)


def _dot_tn(a, b):
    return lax.dot_general(a, b, (((0,), (0,)), ((), ())), preferred_element_type=F32)


def _split3_bf16(t):
    p1 = t.astype(BF16)
    r1 = t - p1.astype(F32)
    p2 = r1.astype(BF16)
    p3 = (r1 - p2.astype(F32)).astype(BF16)
    return p1, p2, p3


def _dot_f32_by_exact(a, b_exact):
    return sum(_dot(p, b_exact) for p in _split3_bf16(a))


def _dot_exact_by_f32(a_exact, b):
    return sum(_dot(a_exact, p) for p in _split3_bf16(b))


def _dot3(a, b_hi, b_lo):
    a_hi, a_lo = _split_bf16(a)
    return _dot(a_hi, b_hi) + _dot(a_lo, b_hi) + _dot(a_hi, b_lo)


def _softplus(z):
    return jnp.maximum(z, 0.0) + jnp.log(1.0 + jnp.exp(-jnp.abs(z)))


def _rwkv_kernel(p_ref, mu_ref, vec_ref, lhi_ref, llo_ref, o_ref, state_ref, prev_ref):
    L, GW, ST = RWKV_CHUNK, GROUP_W, RWKV_STACK

    @pl.when(pl.program_id(1) == 0)
    def _():
        state_ref[...] = jnp.zeros_like(state_ref)
        prev_ref[...] = jnp.zeros_like(prev_ref)

    p = p_ref[0]
    row = lax.broadcasted_iota(jnp.int32, p.shape, 0)
    prev = jnp.where(row == 0, prev_ref[...], pltpu.roll(p, 1, axis=0))
    prev_ref[...] = p[L - 1:L, :]
    ps = p + (prev - p) * mu_ref[...]
    r, k, v = ps[:, :GW], ps[:, GW:2 * GW], ps[:, 2 * GW:3 * GW]
    low = ps[:, 3 * GW:]
    lane_low = lax.broadcasted_iota(jnp.int32, low.shape, 1)
    low = jnp.where(lane_low < RWKV_W_RANK, jnp.tanh(low),
                    jnp.where(lane_low < RWKV_W_RANK + RWKV_A_RANK, low, jax.nn.sigmoid(low)))
    up = _dot3(low, lhi_ref[...], llo_ref[...])
    w0, a0, k_k, k_a = vec_ref[0:1, :], vec_ref[1:2, :], vec_ref[2:3, :], vec_ref[3:4, :]
    r_k, ln_g, ln_b = vec_ref[4:5, :], vec_ref[5:6, :], vec_ref[6:7, :]
    logw = -jnp.exp(-_softplus(-(w0 + up[:, :GW])) - 0.5)
    rate = jax.nn.sigmoid(a0 + up[:, GW:2 * GW])
    gate = up[:, 2 * GW:]

    ri = lax.broadcasted_iota(jnp.int32, (ST, ST), 0)
    ci = lax.broadcasted_iota(jnp.int32, (ST, ST), 1)
    same_head = (ri // L) == (ci // L)
    ones_bd = jnp.where(same_head, 1.0, 0.0).astype(BF16)

    kk = k * k_k
    kk = kk / jnp.maximum(jnp.sqrt(_dot_f32_by_exact(kk * kk, ones_bd)), 1e-12)
    k = k * (1.0 + (rate - 1.0) * k_a)

    ti = lax.broadcasted_iota(jnp.int32, (L, L), 0)
    tj = lax.broadcasted_iota(jnp.int32, (L, L), 1)
    lc = _dot_exact_by_f32(jnp.where(tj <= ti, 1.0, 0.0).astype(BF16), logw)
    lc_last = lc[L - 1:L, :]
    dec_in = jnp.exp(lc)
    dec_out = jnp.exp(-lc)
    a_t = -kk * jnp.exp(lc - logw)
    b_t = kk * rate * dec_out
    k_t = k * dec_out
    r_t = r * dec_in
    to_end = jnp.exp(lc_last)

    stack = lambda t: jnp.concatenate([t] * N_HEADS, axis=0)
    bd = lambda t: jnp.where(same_head, stack(t), 0.0).astype(BF16)
    a_bd, r_bd, v_bd = bd(a_t), bd(r_t), bd(v)
    m = _dot_nt(jnp.concatenate([a_bd, r_bd], axis=0),
                jnp.concatenate([stack(b_t), stack(k_t)], axis=0).astype(BF16))
    strict = same_head & ((ci % L) < (ri % L))
    incl = same_head & ((ci % L) <= (ri % L))
    m_ab = jnp.where(strict, m[:ST, :ST], 0.0)
    m_ak = jnp.where(strict, m[:ST, ST:], 0.0).astype(BF16)
    m_rb = jnp.where(incl, m[ST:, :ST], 0.0).astype(BF16)
    m_rk = jnp.where(incl, m[ST:, ST:], 0.0).astype(BF16)

    inv = jnp.where(ri == ci, 1.0, 0.0) + m_ab
    pw = m_ab
    n_doublings = RWKV_CHUNK.bit_length() - 2
    for s in range(n_doublings):
        pw_b = pw.astype(BF16)
        pw = _dot(pw_b, pw_b)
        inv = inv + _dot(inv.astype(BF16), pw.astype(BF16))

    t0 = state_ref[...]
    t0_b = t0.astype(BF16)
    u = _dot(inv.astype(BF16), (_dot(a_bd, t0_b) + _dot(m_ak, v_bd)).astype(BF16)).astype(BF16)
    y_bd = _dot(r_bd, t0_b) + _dot(m_rb, u) + _dot(m_rk, v_bd)
    y = sum(y_bd[h * L:(h + 1) * L, :] for h in range(N_HEADS))

    to_end_col = jnp.sum(jnp.where(ri == ci, jnp.broadcast_to(to_end, (ST, ST)), 0.0), axis=1, keepdims=True)
    state_ref[...] = (to_end_col * t0 + _dot_tn(bd(b_t * to_end), u) + _dot_tn(bd(k_t * to_end), v_bd))

    inv_d = 1.0 / HEAD_DIM
    mean = _dot_f32_by_exact(y, ones_bd) * inv_d
    yc = y - mean
    var = _dot_f32_by_exact(yc * yc, ones_bd) * inv_d
    yn = yc * lax.rsqrt(var + RWKV_GN_EPS) * ln_g + ln_b
    bonus = _dot_f32_by_exact(r * k * r_k, ones_bd) * v
    o_ref[0] = (yn + bonus) * gate


def _rwkv7_time_mix(p, mu, w0, w2, a0, a2, g2, k_k, k_a, r_k, ln_g, ln_b):
    B, S, _ = p.shape
    GW = GROUP_W
    assert S % RWKV_CHUNK == 0 and RWKV_STACK == GW
    low_w = jnp.zeros((RWKV_LOW, 3 * GW), F32)
    low_w = low_w.at[:RWKV_W_RANK, :GW].set(w2)
    low_w = low_w.at[RWKV_W_RANK:RWKV_W_RANK + RWKV_A_RANK, GW:2 * GW].set(a2)
    low_w = low_w.at[RWKV_W_RANK + RWKV_A_RANK:, 2 * GW:].set(g2)
    l_hi, l_lo = _split_bf16(low_w)
    vecs = jnp.stack([w0, a0, k_k, k_a, r_k, ln_g, ln_b, jnp.zeros_like(w0)], 0)
    full = lambda a: pl.BlockSpec(a.shape, lambda b, c: (0,) * a.ndim)
    mu2 = mu[None]
    return pl.pallas_call(
        _rwkv_kernel,
        grid=(B, S // RWKV_CHUNK),
        in_specs=[pl.BlockSpec((1, RWKV_CHUNK, A_PAD), lambda b, c: (b, c, 0)),
                  full(mu2), full(vecs), full(l_hi), full(l_lo)],
        out_specs=pl.BlockSpec((1, RWKV_CHUNK, GW), lambda b, c: (b, c, 0)),
        out_shape=jax.ShapeDtypeStruct((B, S, GW), F32),
        scratch_shapes=[pltpu.VMEM((RWKV_STACK, GW), F32), pltpu.VMEM((1, A_PAD), F32)],
        compiler_params=pltpu.CompilerParams(dimension_semantics=("parallel", "arbitrary"),
                                             vmem_limit_bytes=VMEM_LIMIT),
        name="rwkv7",
    )(p, mu2, vecs, l_hi, l_lo)


def _rms_norm(x, g):
    xf = x.astype(F32)
    y = xf * lax.rsqrt(jnp.mean(xf * xf, -1, keepdims=True) + NORM_EPS)
    return (y * g.astype(F32)).astype(x.dtype)


def _rope(x, pos):
    half = x.shape[-1] // 2
    inv = ROPE_THETA ** (-jnp.arange(half, dtype=F32) / half)
    ang = pos.astype(F32)[:, None] * inv[None, :]
    cos = jnp.cos(ang)[None, :, None, :]
    sin = jnp.sin(ang)[None, :, None, :]
    xf = x.astype(F32)
    x1, x2 = xf[..., :half], xf[..., half:]
    return jnp.concatenate([x1 * cos - x2 * sin, x2 * cos + x1 * sin], -1).astype(x.dtype)


def _token_shift(p, mu):
    prev = jnp.pad(p, ((0, 0), (1, 0), (0, 0)))[:, :-1]
    return p + (prev - p) * mu


def _causal_dwconv(x, w, b):
    ch = x.shape[-1]
    y = lax.conv_general_dilated(x, w[:, None, :].astype(x.dtype), window_strides=(1,),
                                 padding=((w.shape[0] - 1, 0),),
                                 dimension_numbers=('NWC', 'WIO', 'NWC'), feature_group_count=ch)
    return y + b


def _stick_breaking_attn(q, k, v):
    B, S, H, d = q.shape
    scale = d ** -0.5
    outs = []
    for i in range(S // SB_BLOCK):
        q0 = i * SB_BLOCK
        kl = q0 + SB_BLOCK
        z = jnp.einsum('bqhd,bkhd->bhqk', q[:, q0:kl], k[:, :kl]).astype(F32) * scale
        t_idx = q0 + jnp.arange(SB_BLOCK)[:, None]
        s_idx = jnp.arange(kl)[None, :]
        mask = s_idx < t_idx
        log1m = jnp.where(mask, jax.nn.log_sigmoid(-z), 0.0)
        cs = jnp.cumsum(log1m, -1)
        log_a = jax.nn.log_sigmoid(z) + cs[..., -1:] - cs
        att = jnp.where(mask, jnp.exp(log_a), 0.0)
        outs.append(jnp.einsum('bhqk,bkhd->bqhd', att.astype(v.dtype), v[:, :kl]))
    return jnp.concatenate(outs, 1)


def _mlstm_chunkwise(q, k, v, log_i, log_f):
    B, S, H, d = q.shape
    L = ML_CHUNK
    nc = S // L

    def to_chunks(t):
        t = t.reshape((B, nc, L, H) + t.shape[3:])
        return jnp.moveaxis(jnp.moveaxis(t, 3, 2), 1, 0)

    causal = jnp.tril(jnp.ones((L, L), bool))

    def chunk(carry, inp):
        c_st, n_st, m_st = carry
        qc, kc, vc, li, lf = inp
        b = jnp.cumsum(lf, -1)
        dmat = jnp.where(causal, b[..., :, None] - b[..., None, :] + li[..., None, :], -jnp.inf)
        g_inter = b + m_st[..., None]
        m_t = jnp.maximum(g_inter, jnp.max(dmat, -1))
        s_inter = jnp.exp(g_inter - m_t)
        sqk = jnp.einsum('bhtd,bhsd->bhts', qc, kc) * jnp.exp(dmat - m_t[..., None])
        num = s_inter[..., None] * jnp.einsum('bhvd,bhtd->bhtv', c_st, qc) + jnp.einsum('bhts,bhsv->bhtv', sqk, vc)
        den = s_inter * jnp.einsum('bhd,bhtd->bht', n_st, qc) + jnp.sum(sqk, -1)
        h = num / jnp.maximum(jnp.abs(den), jnp.exp(-m_t))[..., None]
        b_last = b[..., -1]
        dec = b_last[..., None] - b + li
        m_new = jnp.maximum(b_last + m_st, jnp.max(dec, -1))
        wk = jnp.exp(dec - m_new[..., None])
        s_old = jnp.exp(b_last + m_st - m_new)
        c_st = s_old[..., None, None] * c_st + jnp.einsum('bhs,bhsv,bhsd->bhvd', wk, vc, kc)
        n_st = s_old[..., None] * n_st + jnp.einsum('bhs,bhsd->bhd', wk, kc)
        return (c_st, n_st, m_new), h

    init = (jnp.zeros((B, H, d, d), F32), jnp.zeros((B, H, d), F32), jnp.zeros((B, H), F32))
    _, hs = lax.scan(chunk, init, tuple(to_chunks(t) for t in (q, k, v, log_i, log_f)))
    hs = jnp.moveaxis(jnp.moveaxis(hs, 0, 1), 2, 3)
    return hs.reshape(B, S, H, d)


def _mlstm_mix(p, conv_w, conv_b, ig_b, fg_b, norm_g):
    B, S, _ = p.shape
    H, d = N_HEADS, HEAD_DIM
    qk = jax.nn.silu(_causal_dwconv(p[..., :2 * GROUP_W], conv_w, conv_b))
    q, k = qk[..., :GROUP_W], qk[..., GROUP_W:]
    _, _, v, o, ig, fg = _split_cols(p, C_SIZES)
    cap = lambda t: GATE_CAP * jnp.tanh(t / GATE_CAP)
    log_i = cap((ig + ig_b).astype(F32))
    log_f = jax.nn.log_sigmoid(cap((fg + fg_b).astype(F32)))
    heads = lambda t: t.astype(F32).reshape(B, S, H, d)
    h = _mlstm_chunkwise(heads(q), heads(k) * d ** -0.5, heads(v), log_i, log_f)
    h = _rms_norm(h, norm_g.reshape(H, d)).reshape(B, S, GROUP_W)
    return (jax.nn.sigmoid(o.astype(F32)) * h).astype(p.dtype)


def kernel(x, c, ada_w, ada_b, norm1_g, norm2_g, w_in, rk_mu, rk_w0, rk_w2, rk_a0, rk_a2, rk_g2, rk_kk, rk_ka, rk_rk, rk_ln_g, rk_ln_b, sb_norm_g, ml_conv_w, ml_conv_b, ml_ig_b, ml_fg_b, ml_norm_g, ds_qn_g, ds_kn_g, ds_out_g, w_out, moe_wg, moe_bg, moe_we, moe_be, moe_w1, moe_w3, moe_w2):
    B, S, D = x.shape
    H, d = N_HEADS, HEAD_DIM
    depth = ada_w.shape[0]
    pos = jnp.arange(S)
    c_act = jax.nn.silu(c)
    for l in range(depth):
        mod = (c_act @ ada_w[l] + ada_b[l])[:, None, :]
        sh1, sc1, gt1, sh2, sc2, gt2 = jnp.split(mod, 6, axis=-1)

        pA, pB, pC, pD = _in_proj(x, sc1, sh1, norm1_g[l][None], _pad_w_in(w_in[l]))
        pC = pC[..., :C_COLS]
        pD = pD[..., :D_COLS]

        yA = _rwkv7_time_mix(pA, rk_mu[l], rk_w0[l], rk_w2[l], rk_a0[l], rk_a2[l], rk_g2[l],
                             rk_kk[l], rk_ka[l], rk_rk[l], rk_ln_g[l], rk_ln_b[l])

        qb, kb, vb = (t.reshape(B, S, H, d) for t in _split_cols(pB, B_SIZES))
        yB = _rms_norm(_stick_breaking_attn(qb, kb, vb), sb_norm_g[l].reshape(H, d)).reshape(B, S, GROUP_W)

        yC = _mlstm_mix(pC, ml_conv_w[l], ml_conv_b[l], ml_ig_b[l], ml_fg_b[l], ml_norm_g[l])

        qd, kd, vd, qi, ki, wi = _split_cols(pD, D_SIZES)
        qd = _rope(_rms_norm(qd.reshape(B, S, H, d), ds_qn_g[l]), pos).reshape(B, S, GROUP_W)
        kd = _rope(_rms_norm(kd[:, :, None, :], ds_kn_g[l]), pos)[:, :, 0]
        qi = _rope(qi.reshape(B, S, IDX_HEADS, IDX_DIM), pos).reshape(B, S, IDX_HEADS * IDX_DIM)
        ki = _rope(ki[:, :, None, :], pos)[:, :, 0]
        wi = wi * (IDX_HEADS ** -0.5 * IDX_DIM ** -0.5)
        yD = _dsa_attn_norm(qd, kd, vd, qi, ki, wi, ds_out_g[l])

        router = jnp.pad(jnp.concatenate([moe_wg[l], moe_we[l]], 1),
                         ((0, 0), (0, ROUTER_PAD - N_GROUPS - N_EXPERTS)))
        r_hi, r_lo = _split_bf16(router)
        x1, h2, logits = _out_proj((yA, yB, yC, yD), x, gt1, sc2, sh2, norm2_g[l][None],
                                   w_out[l].astype(BF16), r_hi, r_lo)

        moe = _hier_moe(h2.reshape(B * S, D), logits.reshape(B * S, ROUTER_PAD), moe_bg[l], moe_be[l],
                        moe_w1[l].astype(BF16), moe_w3[l].astype(BF16), moe_w2[l].astype(BF16))
        x = x1 + gt2 * moe.reshape(B, S, D)
    return x
```

```python
import functools

import jax
import jax.numpy as jnp
import numpy as np
from jax import lax
from jax.experimental import pallas as pl
from jax.experimental.pallas import tpu as pltpu

F32 = jnp.float32
BF16 = jnp.bfloat16

D_MODEL = 1024
N_MIXERS = 4
GROUP_W = D_MODEL // N_MIXERS
HEAD_DIM = 64
N_HEADS = GROUP_W // HEAD_DIM
NORM_EPS = 1e-6
RWKV_W_RANK = 32
RWKV_A_RANK = 32
RWKV_G_RANK = 64
RWKV_GN_EPS = 64e-5
SB_BLOCK = 128
ML_CHUNK = 64
ML_CONV = 4
GATE_CAP = 15.0
DSA_BLOCK = 128
IDX_HEADS = 4
IDX_DIM = 32
TOPK_MAX = 256
ROPE_THETA = 10000.0
N_GROUPS = 4
EXP_PER_GROUP = 8
N_EXPERTS = N_GROUPS * EXP_PER_GROUP
EXPERT_FF = D_MODEL // 2
TOP_IN_GROUP = 2

A_SIZES = (GROUP_W, GROUP_W, GROUP_W, RWKV_W_RANK, RWKV_A_RANK, RWKV_G_RANK)
B_SIZES = (GROUP_W, GROUP_W, GROUP_W)
C_SIZES = (GROUP_W, GROUP_W, GROUP_W, GROUP_W, N_HEADS, N_HEADS)
D_SIZES = (GROUP_W, HEAD_DIM, HEAD_DIM, IDX_HEADS * IDX_DIM, IDX_DIM, IDX_HEADS)
A_COLS = sum(A_SIZES)
B_COLS = sum(B_SIZES)
C_COLS = sum(C_SIZES)
D_COLS = sum(D_SIZES)

LANE = 128
A_PAD = 896
B_PAD = 768
C_PAD = 1152
D_PAD = 640
P_PAD = A_PAD + B_PAD + C_PAD + D_PAD
ROUTER_PAD = LANE

IN_ROWS = 256
OUT_ROWS = 512
MOE_ROWS = 256
VMEM_LIMIT = 48 * 1024 * 1024


def _split_cols(t, sizes):
    return jnp.split(t, [int(i) for i in np.cumsum(sizes)[:-1]], axis=-1)


def _in_proj_kernel(x_ref, sc_ref, sh_ref, g_ref, w_ref, oa_ref, ob_ref, oc_ref, od_ref):
    x = x_ref[0]
    y = x * lax.rsqrt(jnp.mean(x * x, -1, keepdims=True) + NORM_EPS) * g_ref[...]
    h = y * (1.0 + sc_ref[0]) + sh_ref[0]
    p = jnp.dot(h.astype(BF16), w_ref[...], preferred_element_type=F32)
    oa_ref[0] = p[:, :A_PAD]
    ob_ref[0] = p[:, A_PAD:A_PAD + B_PAD]
    oc_ref[0] = p[:, A_PAD + B_PAD:A_PAD + B_PAD + C_PAD]
    od_ref[0] = p[:, A_PAD + B_PAD + C_PAD:]


def _in_proj(x, sc, sh, g, w_pad):
    B, S, D = x.shape
    row = lambda w: pl.BlockSpec((1, IN_ROWS, w), lambda b, i: (b, i, 0))
    vec = pl.BlockSpec((1, 1, D), lambda b, i: (b, 0, 0))
    return pl.pallas_call(
        _in_proj_kernel,
        grid=(B, S // IN_ROWS),
        in_specs=[row(D), vec, vec, pl.BlockSpec((1, D), lambda b, i: (0, 0)),
                  pl.BlockSpec((D, P_PAD), lambda b, i: (0, 0))],
        out_specs=[row(A_PAD), row(B_PAD), row(C_PAD), row(D_PAD)],
        out_shape=[jax.ShapeDtypeStruct((B, S, w), F32) for w in (A_PAD, B_PAD, C_PAD, D_PAD)],
        compiler_params=pltpu.CompilerParams(dimension_semantics=("parallel", "parallel"),
                                             vmem_limit_bytes=VMEM_LIMIT),
        name="in_proj",
    )(x, sc, sh, g, w_pad)


def _pad_w_in(w):
    wa, wb, wc, wd = _split_cols(w, (A_COLS, B_COLS, C_COLS, D_COLS))
    padc = lambda t, n: jnp.pad(t, ((0, 0), (0, n - t.shape[1])))
    return jnp.concatenate([padc(wa, A_PAD), padc(wb, B_PAD), padc(wc, C_PAD), padc(wd, D_PAD)], 1).astype(BF16)


def _split_bf16(t):
    hi = t.astype(BF16)
    lo = (t - hi.astype(F32)).astype(BF16)
    return hi, lo


def _out_proj_kernel(ya_ref, yb_ref, yc_ref, yd_ref, x_ref, gt_ref, sc_ref, sh_ref, g_ref, w_ref,
                     rhi_ref, rlo_ref, x1_ref, h2_ref, lg_ref):
    acc = jnp.zeros(x_ref.shape[1:], F32)
    for n, y_ref in enumerate((ya_ref, yb_ref, yc_ref, yd_ref)):
        acc += jnp.dot(y_ref[0].astype(BF16), w_ref[n * GROUP_W:(n + 1) * GROUP_W, :],
                       preferred_element_type=F32)
    x1 = x_ref[0] + gt_ref[0] * acc
    x1_ref[0] = x1
    y = x1 * lax.rsqrt(jnp.mean(x1 * x1, -1, keepdims=True) + NORM_EPS) * g_ref[...]
    h = y * (1.0 + sc_ref[0]) + sh_ref[0]
    hi, lo = _split_bf16(h)
    h2_ref[0] = hi
    lg_ref[0] = (jnp.dot(hi, rhi_ref[...], preferred_element_type=F32)
                 + jnp.dot(lo, rhi_ref[...], preferred_element_type=F32)
                 + jnp.dot(hi, rlo_ref[...], preferred_element_type=F32))


def _out_proj(ys, x, gt, sc, sh, g, w_out, r_hi, r_lo):
    B, S, D = x.shape
    row = lambda w: pl.BlockSpec((1, OUT_ROWS, w), lambda b, i: (b, i, 0))
    vec = pl.BlockSpec((1, 1, D), lambda b, i: (b, 0, 0))
    full = lambda a: pl.BlockSpec(a.shape, lambda b, i: (0,) * a.ndim)
    return pl.pallas_call(
        _out_proj_kernel,
        grid=(B, S // OUT_ROWS),
        in_specs=[row(GROUP_W)] * 4 + [row(D), vec, vec, vec, full(g), full(w_out), full(r_hi), full(r_lo)],
        out_specs=[row(D), row(D), row(ROUTER_PAD)],
        out_shape=[jax.ShapeDtypeStruct((B, S, D), F32), jax.ShapeDtypeStruct((B, S, D), BF16),
                   jax.ShapeDtypeStruct((B, S, ROUTER_PAD), F32)],
        compiler_params=pltpu.CompilerParams(dimension_semantics=("parallel", "parallel"),
                                             vmem_limit_bytes=VMEM_LIMIT),
        name="out_proj",
    )(*ys, x, gt, sc, sh, g, w_out, r_hi, r_lo)


def _moe_ffn_kernel(blk_e_ref, x_ref, wt_ref, w1_ref, w3_ref, w2_ref, o_ref):
    del blk_e_ref
    xb = x_ref[...]
    a = jnp.dot(xb, w1_ref[0], preferred_element_type=F32)
    b = jnp.dot(xb, w3_ref[0], preferred_element_type=F32)
    hmid = (a * jax.nn.sigmoid(a) * b).astype(BF16)
    y = jnp.dot(hmid, w2_ref[0], preferred_element_type=F32)
    o_ref[...] = y * wt_ref[...]


def _moe_ffn(blk_e, xs, wt, w1, w3, w2):
    n_slots, D = xs.shape
    n_blocks = n_slots // MOE_ROWS
    FF = w1.shape[-1]
    return pl.pallas_call(
        _moe_ffn_kernel,
        grid_spec=pltpu.PrefetchScalarGridSpec(
            num_scalar_prefetch=1,
            grid=(n_blocks,),
            in_specs=[pl.BlockSpec((MOE_ROWS, D), lambda i, e: (i, 0)),
                      pl.BlockSpec((MOE_ROWS, 1), lambda i, e: (i, 0)),
                      pl.BlockSpec((1, D, FF), lambda i, e: (e[i], 0, 0)),
                      pl.BlockSpec((1, D, FF), lambda i, e: (e[i], 0, 0)),
                      pl.BlockSpec((1, FF, D), lambda i, e: (e[i], 0, 0))],
            out_specs=pl.BlockSpec((MOE_ROWS, D), lambda i, e: (i, 0)),
        ),
        out_shape=jax.ShapeDtypeStruct((n_slots, D), F32),
        compiler_params=pltpu.CompilerParams(dimension_semantics=("arbitrary",),
                                             vmem_limit_bytes=VMEM_LIMIT),
        name="moe_ffn",
    )(blk_e, xs, wt, w1, w3, w2)


def _hier_moe(h2, logits, bg, be, w1, w3, w2):
    N, D = h2.shape
    g_prob = jax.nn.softmax(logits[:, :N_GROUPS] + bg, -1)
    g_p, g_idx = lax.top_k(g_prob, 1)
    e_logits = (logits[:, N_GROUPS:N_GROUPS + N_EXPERTS] + be).reshape(N, N_GROUPS, EXP_PER_GROUP)
    e_logits = jnp.take_along_axis(e_logits, g_idx[:, :, None], axis=1)[:, 0]
    e_p, e_idx = lax.top_k(jax.nn.softmax(e_logits, -1), TOP_IN_GROUP)
    gate = g_p * e_p / jnp.sum(e_p, -1, keepdims=True)
    expert = g_idx * EXP_PER_GROUP + e_idx
    n_asg = N * TOP_IN_GROUP
    flat_e = expert.reshape(n_asg)
    order = jnp.argsort(flat_e)
    se = flat_e[order]
    counts = jnp.bincount(flat_e, length=N_EXPERTS)
    start = jnp.cumsum(counts) - counts
    pad_counts = (counts + MOE_ROWS - 1) // MOE_ROWS * MOE_ROWS
    pad_end = jnp.cumsum(pad_counts)
    pad_start = pad_end - pad_counts
    slot = (pad_start[se] + jnp.arange(n_asg) - start[se]).astype(jnp.int32)
    n_blocks = n_asg // MOE_ROWS + N_EXPERTS
    n_slots = n_blocks * MOE_ROWS
    slot_tok = jnp.zeros((n_slots,), jnp.int32).at[slot].set((order // TOP_IN_GROUP).astype(jnp.int32))
    slot_w = jnp.zeros((n_slots,), F32).at[slot].set(gate.reshape(n_asg)[order])
    blk_start = jnp.arange(n_blocks) * MOE_ROWS
    blk_e = jnp.minimum(jnp.sum(pad_end[None, :] <= blk_start[:, None], 1), N_EXPERTS - 1).astype(jnp.int32)
    yb = _moe_ffn(blk_e, h2[slot_tok], slot_w[:, None], w1, w3, w2)
    asg_slot = jnp.zeros((n_asg,), jnp.int32).at[order].set(slot).reshape(N, TOP_IN_GROUP)
    return yb[asg_slot[:, 0]] + yb[asg_slot[:, 1]]


INT_MIN = -2 ** 31
DSA_KEY_STEP = 512


def _float_order_key(x):
    bits = pltpu.bitcast(x, jnp.int32)
    bits = jnp.where(x == 0.0, 0, bits)
    return bits ^ ((bits >> 31) & 0x7FFFFFFF)


def _row_count(mask):
    return jnp.sum(jnp.where(mask, 1.0, 0.0), axis=1, keepdims=True)


def _dsa_block(qd_ref, kdt_ref, vdw_ref, qi_ref, kit_ref, wi_ref, g_ref, o_ref, *, kl, n_sel):
    q0 = pl.program_id(1) * DSA_BLOCK
    kit = kit_ref[0, :, :kl]
    ki4 = jnp.concatenate([kit] * IDX_HEADS, axis=0)
    k_hi, k_lo = _split_bf16(ki4)
    qi = qi_ref[0]
    wi = wi_ref[0]
    lane_i = lax.broadcasted_iota(jnp.int32, qi.shape, 1)
    score = jnp.zeros((DSA_BLOCK, kl), F32)
    for h in range(IDX_HEADS):
        q_hi, q_lo = _split_bf16(jnp.where(lane_i // IDX_DIM == h, qi, 0.0))
        sc = (jnp.dot(q_hi, k_hi, preferred_element_type=F32) + jnp.dot(q_lo, k_hi, preferred_element_type=F32)
              + jnp.dot(q_hi, k_lo, preferred_element_type=F32))
        score = score + wi[:, h:h + 1] * jnp.maximum(sc, 0.0)
    kidx = lax.broadcasted_iota(jnp.int32, (DSA_BLOCK, kl), 1)
    qpos = q0 + lax.broadcasted_iota(jnp.int32, (DSA_BLOCK, kl), 0)
    adm = kidx <= qpos
    key = _float_order_key(jnp.where(adm, score, -jnp.inf))

    def value_bit(it, tau):
        cand = tau | jnp.left_shift(jnp.int32(1), 31 - it)
        cnt = _row_count(key >= (cand ^ INT_MIN))
        return jnp.where(cnt >= n_sel, cand, tau)

    tau = lax.fori_loop(0, 32, value_bit, jnp.zeros((DSA_BLOCK, 1), jnp.int32)) ^ INT_MIN
    gt = key > tau
    eq = (key == tau) & adm
    need = n_sel - _row_count(gt)
    n_eq = _row_count(eq)

    def index_bits():
        def index_bit(it, bound):
            cand = bound | jnp.left_shift(jnp.int32(1), 11 - it)
            cnt = _row_count(eq & (kidx < cand))
            return jnp.where(cnt <= need, cand, bound)
        return lax.fori_loop(0, 12, index_bit, jnp.zeros((DSA_BLOCK, 1), jnp.int32))

    bound = lax.cond(jnp.max(n_eq - need) > 0.0, index_bits,
                     lambda: jnp.full((DSA_BLOCK, 1), kl, jnp.int32))
    sel = gt | (eq & (kidx < bound))

    kdt = kdt_ref[0, :, :kl]
    kd4 = jnp.concatenate([kdt] * N_HEADS, axis=0).astype(BF16)
    vdw = vdw_ref[0, :kl, :].astype(BF16)
    qd = qd_ref[0]
    lane_h = lax.broadcasted_iota(jnp.int32, qd.shape, 1) // HEAD_DIM
    out = jnp.zeros(qd.shape, F32)
    for h in range(N_HEADS):
        qm = jnp.where(lane_h == h, qd, 0.0).astype(BF16)
        lg = jnp.dot(qm, kd4, preferred_element_type=F32) * HEAD_DIM ** -0.5
        lg = jnp.where(sel, lg, -jnp.inf)
        p = jnp.exp(lg - jnp.max(lg, axis=1, keepdims=True))
        r = jnp.dot(p.astype(BF16), vdw, preferred_element_type=F32) / jnp.sum(p, axis=1, keepdims=True)
        r = r * lax.rsqrt(jnp.mean(r * r, axis=1, keepdims=True) + NORM_EPS)
        out = out + jnp.where(lane_h == h, r, 0.0)
    o_ref[0] = out * g_ref[...]


def _dsa_kernel(qd_ref, kdt_ref, vdw_ref, qi_ref, kit_ref, wi_ref, g_ref, o_ref, *, kls, n_sel):
    blocks_per_step = DSA_KEY_STEP // DSA_BLOCK
    for j, kl in enumerate(kls):
        @pl.when(pl.program_id(1) // blocks_per_step == j)
        def _():
            _dsa_block(qd_ref, kdt_ref, vdw_ref, qi_ref, kit_ref, wi_ref, g_ref, o_ref, kl=kl, n_sel=n_sel)


def _dsa_attn_norm(qd, kd, vd, qi, ki, wi, g):
    B, S, _ = qd.shape
    n_sel = min(TOPK_MAX, S // 4)
    assert S % DSA_KEY_STEP == 0 and n_sel <= DSA_KEY_STEP
    kls = tuple(range(DSA_KEY_STEP, S + 1, DSA_KEY_STEP))
    kdt = jnp.swapaxes(kd, 1, 2)
    kit = jnp.swapaxes(ki, 1, 2)
    vdw = jnp.tile(vd, (1, 1, N_HEADS))
    blk = lambda w: pl.BlockSpec((1, DSA_BLOCK, w), lambda b, i: (b, i, 0))
    per_b = lambda r, c: pl.BlockSpec((1, r, c), lambda b, i: (b, 0, 0))
    return pl.pallas_call(
        functools.partial(_dsa_kernel, kls=kls, n_sel=n_sel),
        grid=(B, S // DSA_BLOCK),
        in_specs=[blk(GROUP_W), per_b(HEAD_DIM, S), per_b(S, GROUP_W), blk(IDX_HEADS * IDX_DIM),
                  per_b(IDX_DIM, S), blk(IDX_HEADS), pl.BlockSpec((1, GROUP_W), lambda b, i: (0, 0))],
        out_specs=blk(GROUP_W),
        out_shape=jax.ShapeDtypeStruct((B, S, GROUP_W), F32),
        compiler_params=pltpu.CompilerParams(dimension_semantics=("parallel", "parallel"),
                                             vmem_limit_bytes=VMEM_LIMIT),
        name="dsa_attn",
    )(qd, kdt, vdw, qi, kit, wi, g[None])


RWKV_CHUNK = 64
RWKV_LOW = RWKV_W_RANK + RWKV_A_RANK + RWKV_G_RANK
RWKV_STACK = N_HEADS * RWKV_CHUNK


def _dot(a, b):
    return jnp.dot(a, b, preferred_element_type=F32)


def _dot_nt(a, b):
    return lax.dot_general(a, b, (((1,), (1,)), ((), ())), preferred_element_type=F32)


def _dot_tn(a, b):
    return lax.dot_general(a, b, (((0,), (0,)), ((), ())), preferred_element_type=F32)


def _split3_bf16(t):
    p1 = t.astype(BF16)
    r1 = t - p1.astype(F32)
    p2 = r1.astype(BF16)
    p3 = (r1 - p2.astype(F32)).astype(BF16)
    return p1, p2, p3


def _dot_f32_by_exact(a, b_exact):
    return sum(_dot(p, b_exact) for p in _split3_bf16(a))


def _dot_exact_by_f32(a_exact, b):
    return sum(_dot(a_exact, p) for p in _split3_bf16(b))


def _dot3(a, b_hi, b_lo):
    a_hi, a_lo = _split_bf16(a)
    return _dot(a_hi, b_hi) + _dot(a_lo, b_hi) + _dot(a_hi, b_lo)


def _softplus(z):
    return jnp.maximum(z, 0.0) + jnp.log(1.0 + jnp.exp(-jnp.abs(z)))


def _rwkv_kernel(p_ref, mu_ref, vec_ref, lhi_ref, llo_ref, o_ref, state_ref, prev_ref):
    L, GW, ST = RWKV_CHUNK, GROUP_W, RWKV_STACK

    @pl.when(pl.program_id(1) == 0)
    def _():
        state_ref[...] = jnp.zeros_like(state_ref)
        prev_ref[...] = jnp.zeros_like(prev_ref)

    p = p_ref[0]
    row = lax.broadcasted_iota(jnp.int32, p.shape, 0)
    prev = jnp.where(row == 0, prev_ref[...], pltpu.roll(p, 1, axis=0))
    prev_ref[...] = p[L - 1:L, :]
    ps = p + (prev - p) * mu_ref[...]
    r, k, v = ps[:, :GW], ps[:, GW:2 * GW], ps[:, 2 * GW:3 * GW]
    low = ps[:, 3 * GW:]
    lane_low = lax.broadcasted_iota(jnp.int32, low.shape, 1)
    low = jnp.where(lane_low < RWKV_W_RANK, jnp.tanh(low),
                    jnp.where(lane_low < RWKV_W_RANK + RWKV_A_RANK, low, jax.nn.sigmoid(low)))
    up = _dot3(low, lhi_ref[...], llo_ref[...])
    w0, a0, k_k, k_a = vec_ref[0:1, :], vec_ref[1:2, :], vec_ref[2:3, :], vec_ref[3:4, :]
    r_k, ln_g, ln_b = vec_ref[4:5, :], vec_ref[5:6, :], vec_ref[6:7, :]
    logw = -jnp.exp(-_softplus(-(w0 + up[:, :GW])) - 0.5)
    rate = jax.nn.sigmoid(a0 + up[:, GW:2 * GW])
    gate = up[:, 2 * GW:]

    ri = lax.broadcasted_iota(jnp.int32, (ST, ST), 0)
    ci = lax.broadcasted_iota(jnp.int32, (ST, ST), 1)
    same_head = (ri // L) == (ci // L)
    ones_bd = jnp.where(same_head, 1.0, 0.0).astype(BF16)

    kk = k * k_k
    kk = kk / jnp.maximum(jnp.sqrt(_dot_f32_by_exact(kk * kk, ones_bd)), 1e-12)
    k = k * (1.0 + (rate - 1.0) * k_a)

    ti = lax.broadcasted_iota(jnp.int32, (L, L), 0)
    tj = lax.broadcasted_iota(jnp.int32, (L, L), 1)
    lc = _dot_exact_by_f32(jnp.where(tj <= ti, 1.0, 0.0).astype(BF16), logw)
    lc_last = lc[L - 1:L, :]
    dec_in = jnp.exp(lc)
    dec_out = jnp.exp(-lc)
    a_t = -kk * jnp.exp(lc - logw)
    b_t = kk * rate * dec_out
    k_t = k * dec_out
    r_t = r * dec_in
    to_end = jnp.exp(lc_last)

    stack = lambda t: jnp.concatenate([t] * N_HEADS, axis=0)
    bd = lambda t: jnp.where(same_head, stack(t), 0.0).astype(BF16)
    a_bd, r_bd, v_bd = bd(a_t), bd(r_t), bd(v)
    m = _dot_nt(jnp.concatenate([a_bd, r_bd], axis=0),
                jnp.concatenate([stack(b_t), stack(k_t)], axis=0).astype(BF16))
    strict = same_head & ((ci % L) < (ri % L))
    incl = same_head & ((ci % L) <= (ri % L))
    m_ab = jnp.where(strict, m[:ST, :ST], 0.0)
    m_ak = jnp.where(strict, m[:ST, ST:], 0.0).astype(BF16)
    m_rb = jnp.where(incl, m[ST:, :ST], 0.0).astype(BF16)
    m_rk = jnp.where(incl, m[ST:, ST:], 0.0).astype(BF16)

    inv = jnp.where(ri == ci, 1.0, 0.0) + m_ab
    pw = m_ab
    n_doublings = RWKV_CHUNK.bit_length() - 2
    for s in range(n_doublings):
        pw_b = pw.astype(BF16)
        pw = _dot(pw_b, pw_b)
        inv = inv + _dot(inv.astype(BF16), pw.astype(BF16))

    t0 = state_ref[...]
    t0_b = t0.astype(BF16)
    u = _dot(inv.astype(BF16), (_dot(a_bd, t0_b) + _dot(m_ak, v_bd)).astype(BF16)).astype(BF16)
    y_bd = _dot(r_bd, t0_b) + _dot(m_rb, u) + _dot(m_rk, v_bd)
    y = sum(y_bd[h * L:(h + 1) * L, :] for h in range(N_HEADS))

    to_end_col = jnp.sum(jnp.where(ri == ci, jnp.broadcast_to(to_end, (ST, ST)), 0.0), axis=1, keepdims=True)
    state_ref[...] = (to_end_col * t0 + _dot_tn(bd(b_t * to_end), u) + _dot_tn(bd(k_t * to_end), v_bd))

    inv_d = 1.0 / HEAD_DIM
    mean = _dot_f32_by_exact(y, ones_bd) * inv_d
    yc = y - mean
    var = _dot_f32_by_exact(yc * yc, ones_bd) * inv_d
    yn = yc * lax.rsqrt(var + RWKV_GN_EPS) * ln_g + ln_b
    bonus = _dot_f32_by_exact(r * k * r_k, ones_bd) * v
    o_ref[0] = (yn + bonus) * gate


def _rwkv7_time_mix(p, mu, w0, w2, a0, a2, g2, k_k, k_a, r_k, ln_g, ln_b):
    B, S, _ = p.shape
    GW = GROUP_W
    assert S % RWKV_CHUNK == 0 and RWKV_STACK == GW
    low_w = jnp.zeros((RWKV_LOW, 3 * GW), F32)
    low_w = low_w.at[:RWKV_W_RANK, :GW].set(w2)
    low_w = low_w.at[RWKV_W_RANK:RWKV_W_RANK + RWKV_A_RANK, GW:2 * GW].set(a2)
    low_w = low_w.at[RWKV_W_RANK + RWKV_A_RANK:, 2 * GW:].set(g2)
    l_hi, l_lo = _split_bf16(low_w)
    vecs = jnp.stack([w0, a0, k_k, k_a, r_k, ln_g, ln_b, jnp.zeros_like(w0)], 0)
    full = lambda a: pl.BlockSpec(a.shape, lambda b, c: (0,) * a.ndim)
    mu2 = mu[None]
    return pl.pallas_call(
        _rwkv_kernel,
        grid=(B, S // RWKV_CHUNK),
        in_specs=[pl.BlockSpec((1, RWKV_CHUNK, A_PAD), lambda b, c: (b, c, 0)),
                  full(mu2), full(vecs), full(l_hi), full(l_lo)],
        out_specs=pl.BlockSpec((1, RWKV_CHUNK, GW), lambda b, c: (b, c, 0)),
        out_shape=jax.ShapeDtypeStruct((B, S, GW), F32),
        scratch_shapes=[pltpu.VMEM((RWKV_STACK, GW), F32), pltpu.VMEM((1, A_PAD), F32)],
        compiler_params=pltpu.CompilerParams(dimension_semantics=("parallel", "arbitrary"),
                                             vmem_limit_bytes=VMEM_LIMIT),
        name="rwkv7",
    )(p, mu2, vecs, l_hi, l_lo)


def _sb_kernel(q_ref, k_ref, v_ref, g_ref, o_ref, acc_ref):
    i = pl.program_id(1)
    T, GW, H = SB_BLOCK, GROUP_W, N_HEADS
    q = q_ref[0]
    lane_h = lax.broadcasted_iota(jnp.int32, (T, GW), 1) // HEAD_DIM
    qs = jnp.concatenate([jnp.where(lane_h == h, q, 0.0) for h in range(H)], axis=0).astype(BF16)
    si = lax.broadcasted_iota(jnp.int32, (T, T), 0)
    sj = lax.broadcasted_iota(jnp.int32, (T, T), 1)
    later = jnp.where(si > sj, 1.0, 0.0).astype(BF16)
    qrow = lax.broadcasted_iota(jnp.int32, (H * T, T), 0) % T
    kcol = lax.broadcasted_iota(jnp.int32, (H * T, T), 1)
    causal = kcol < qrow

    def key_block(j, carry, diagonal):
        start = pl.multiple_of(j * T, T)
        kj = k_ref[0, pl.ds(start, T), :].astype(BF16)
        vj = v_ref[0, pl.ds(start, T), :].astype(BF16)
        z = _dot_nt(qs, kj) * HEAD_DIM ** -0.5
        log1m = -_softplus(z)
        log1m_in = jnp.where(causal, log1m, 0.0) if diagonal else log1m
        hi, lo = _split_bf16(log1m_in)
        suffix = _dot(hi, later) + _dot(lo, later)
        att = jnp.exp(z + log1m + suffix + carry)
        if diagonal:
            att = jnp.where(causal, att, 0.0)
        acc_ref[...] += _dot(att.astype(BF16), vj)
        return carry + jnp.sum(log1m_in, axis=1, keepdims=True)

    acc_ref[...] = jnp.zeros_like(acc_ref)
    carry = key_block(i, jnp.zeros((H * T, 1), F32), True)
    lax.fori_loop(0, i, lambda it, c: key_block(i - 1 - it, c, False), carry)

    y = sum(jnp.where(lane_h == h, acc_ref[h * T:(h + 1) * T, :], 0.0) for h in range(H))
    hi_ = lax.broadcasted_iota(jnp.int32, (GW, GW), 0) // HEAD_DIM
    hj_ = lax.broadcasted_iota(jnp.int32, (GW, GW), 1) // HEAD_DIM
    ones_bd = jnp.where(hi_ == hj_, 1.0, 0.0).astype(BF16)
    ms = _dot_f32_by_exact(y * y, ones_bd) * (1.0 / HEAD_DIM)
    o_ref[0] = y * lax.rsqrt(ms + NORM_EPS) * g_ref[...]


def _stick_breaking_norm(q, k, v, g):
    B, S, GW = q.shape
    assert S % SB_BLOCK == 0
    per_b = pl.BlockSpec((1, S, GW), lambda b, i: (b, 0, 0))
    blk = pl.BlockSpec((1, SB_BLOCK, GW), lambda b, i: (b, i, 0))
    return pl.pallas_call(
        _sb_kernel,
        grid=(B, S // SB_BLOCK),
        in_specs=[blk, per_b, per_b, pl.BlockSpec((1, GW), lambda b, i: (0, 0))],
        out_specs=blk,
        out_shape=jax.ShapeDtypeStruct((B, S, GW), F32),
        scratch_shapes=[pltpu.VMEM((N_HEADS * SB_BLOCK, GW), F32)],
        compiler_params=pltpu.CompilerParams(dimension_semantics=("parallel", "parallel"),
                                             vmem_limit_bytes=VMEM_LIMIT),
        name="stick_breaking",
    )(q, k, v, g[None])


def _rms_norm(x, g):
    xf = x.astype(F32)
    y = xf * lax.rsqrt(jnp.mean(xf * xf, -1, keepdims=True) + NORM_EPS)
    return (y * g.astype(F32)).astype(x.dtype)


def _rope(x, pos):
    half = x.shape[-1] // 2
    inv = ROPE_THETA ** (-jnp.arange(half, dtype=F32) / half)
    ang = pos.astype(F32)[:, None] * inv[None, :]
    cos = jnp.cos(ang)[None, :, None, :]
    sin = jnp.sin(ang)[None, :, None, :]
    xf = x.astype(F32)
    x1, x2 = xf[..., :half], xf[..., half:]
    return jnp.concatenate([x1 * cos - x2 * sin, x2 * cos + x1 * sin], -1).astype(x.dtype)


def _token_shift(p, mu):
    prev = jnp.pad(p, ((0, 0), (1, 0), (0, 0)))[:, :-1]
    return p + (prev - p) * mu


def _causal_dwconv(x, w, b):
    ch = x.shape[-1]
    y = lax.conv_general_dilated(x, w[:, None, :].astype(x.dtype), window_strides=(1,),
                                 padding=((w.shape[0] - 1, 0),),
                                 dimension_numbers=('NWC', 'WIO', 'NWC'), feature_group_count=ch)
    return y + b


def _mlstm_chunkwise(q, k, v, log_i, log_f):
    B, S, H, d = q.shape
    L = ML_CHUNK
    nc = S // L

    def to_chunks(t):
        t = t.reshape((B, nc, L, H) + t.shape[3:])
        return jnp.moveaxis(jnp.moveaxis(t, 3, 2), 1, 0)

    causal = jnp.tril(jnp.ones((L, L), bool))

    def chunk(carry, inp):
        c_st, n_st, m_st = carry
        qc, kc, vc, li, lf = inp
        b = jnp.cumsum(lf, -1)
        dmat = jnp.where(causal, b[..., :, None] - b[..., None, :] + li[..., None, :], -jnp.inf)
        g_inter = b + m_st[..., None]
        m_t = jnp.maximum(g_inter, jnp.max(dmat, -1))
        s_inter = jnp.exp(g_inter - m_t)
        sqk = jnp.einsum('bhtd,bhsd->bhts', qc, kc) * jnp.exp(dmat - m_t[..., None])
        num = s_inter[..., None] * jnp.einsum('bhvd,bhtd->bhtv', c_st, qc) + jnp.einsum('bhts,bhsv->bhtv', sqk, vc)
        den = s_inter * jnp.einsum('bhd,bhtd->bht', n_st, qc) + jnp.sum(sqk, -1)
        h = num / jnp.maximum(jnp.abs(den), jnp.exp(-m_t))[..., None]
        b_last = b[..., -1]
        dec = b_last[..., None] - b + li
        m_new = jnp.maximum(b_last + m_st, jnp.max(dec, -1))
        wk = jnp.exp(dec - m_new[..., None])
        s_old = jnp.exp(b_last + m_st - m_new)
        c_st = s_old[..., None, None] * c_st + jnp.einsum('bhs,bhsv,bhsd->bhvd', wk, vc, kc)
        n_st = s_old[..., None] * n_st + jnp.einsum('bhs,bhsd->bhd', wk, kc)
        return (c_st, n_st, m_new), h

    init = (jnp.zeros((B, H, d, d), F32), jnp.zeros((B, H, d), F32), jnp.zeros((B, H), F32))
    _, hs = lax.scan(chunk, init, tuple(to_chunks(t) for t in (q, k, v, log_i, log_f)))
    hs = jnp.moveaxis(jnp.moveaxis(hs, 0, 1), 2, 3)
    return hs.reshape(B, S, H, d)


def _mlstm_mix(p, conv_w, conv_b, ig_b, fg_b, norm_g):
    B, S, _ = p.shape
    H, d = N_HEADS, HEAD_DIM
    qk = jax.nn.silu(_causal_dwconv(p[..., :2 * GROUP_W], conv_w, conv_b))
    q, k = qk[..., :GROUP_W], qk[..., GROUP_W:]
    _, _, v, o, ig, fg = _split_cols(p, C_SIZES)
    cap = lambda t: GATE_CAP * jnp.tanh(t / GATE_CAP)
    log_i = cap((ig + ig_b).astype(F32))
    log_f = jax.nn.log_sigmoid(cap((fg + fg_b).astype(F32)))
    heads = lambda t: t.astype(F32).reshape(B, S, H, d)
    h = _mlstm_chunkwise(heads(q), heads(k) * d ** -0.5, heads(v), log_i, log_f)
    h = _rms_norm(h, norm_g.reshape(H, d)).reshape(B, S, GROUP_W)
    return (jax.nn.sigmoid(o.astype(F32)) * h).astype(p.dtype)


def kernel(x, c, ada_w, ada_b, norm1_g, norm2_g, w_in, rk_mu, rk_w0, rk_w2, rk_a0, rk_a2, rk_g2, rk_kk, rk_ka, rk_rk, rk_ln_g, rk_ln_b, sb_norm_g, ml_conv_w, ml_conv_b, ml_ig_b, ml_fg_b, ml_norm_g, ds_qn_g, ds_kn_g, ds_out_g, w_out, moe_wg, moe_bg, moe_we, moe_be, moe_w1, moe_w3, moe_w2):
    B, S, D = x.shape
    H, d = N_HEADS, HEAD_DIM
    depth = ada_w.shape[0]
    pos = jnp.arange(S)
    c_act = jax.nn.silu(c)
    for l in range(depth):
        mod = (c_act @ ada_w[l] + ada_b[l])[:, None, :]
        sh1, sc1, gt1, sh2, sc2, gt2 = jnp.split(mod, 6, axis=-1)

        pA, pB, pC, pD = _in_proj(x, sc1, sh1, norm1_g[l][None], _pad_w_in(w_in[l]))
        pC = pC[..., :C_COLS]
        pD = pD[..., :D_COLS]

        yA = _rwkv7_time_mix(pA, rk_mu[l], rk_w0[l], rk_w2[l], rk_a0[l], rk_a2[l], rk_g2[l],
                             rk_kk[l], rk_ka[l], rk_rk[l], rk_ln_g[l], rk_ln_b[l])

        qb, kb, vb = _split_cols(pB, B_SIZES)
        yB = _stick_breaking_norm(qb, kb, vb, sb_norm_g[l])

        yC = _mlstm_mix(pC, ml_conv_w[l], ml_conv_b[l], ml_ig_b[l], ml_fg_b[l], ml_norm_g[l])

        qd, kd, vd, qi, ki, wi = _split_cols(pD, D_SIZES)
        qd = _rope(_rms_norm(qd.reshape(B, S, H, d), ds_qn_g[l]), pos).reshape(B, S, GROUP_W)
        kd = _rope(_rms_norm(kd[:, :, None, :], ds_kn_g[l]), pos)[:, :, 0]
        qi = _rope(qi.reshape(B, S, IDX_HEADS, IDX_DIM), pos).reshape(B, S, IDX_HEADS * IDX_DIM)
        ki = _rope(ki[:, :, None, :], pos)[:, :, 0]
        wi = wi * (IDX_HEADS ** -0.5 * IDX_DIM ** -0.5)
        yD = _dsa_attn_norm(qd, kd, vd, qi, ki, wi, ds_out_g[l])

        router = jnp.pad(jnp.concatenate([moe_wg[l], moe_we[l]], 1),
                         ((0, 0), (0, ROUTER_PAD - N_GROUPS - N_EXPERTS)))
        r_hi, r_lo = _split_bf16(router)
        x1, h2, logits = _out_proj((yA, yB, yC, yD), x, gt1, sc2, sh2, norm2_g[l][None],
                                   w_out[l].astype(BF16), r_hi, r_lo)

        moe = _hier_moe(h2.reshape(B * S, D), logits.reshape(B * S, ROUTER_PAD), moe_bg[l], moe_be[l],
                        moe_w1[l].astype(BF16), moe_w3[l].astype(BF16), moe_w2[l].astype(BF16))
        x = x1 + gt2 * moe.reshape(B, S, D)
    return x
```

```python
import functools

import jax
import jax.numpy as jnp
import numpy as np
from jax import lax
from jax.experimental import pallas as pl
from jax.experimental.pallas import tpu as pltpu

F32 = jnp.float32
BF16 = jnp.bfloat16

D_MODEL = 1024
N_MIXERS = 4
GROUP_W = D_MODEL // N_MIXERS
HEAD_DIM = 64
N_HEADS = GROUP_W // HEAD_DIM
NORM_EPS = 1e-6
RWKV_W_RANK = 32
RWKV_A_RANK = 32
RWKV_G_RANK = 64
RWKV_GN_EPS = 64e-5
SB_BLOCK = 128
ML_CHUNK = 64
ML_CONV = 4
GATE_CAP = 15.0
DSA_BLOCK = 128
IDX_HEADS = 4
IDX_DIM = 32
TOPK_MAX = 256
ROPE_THETA = 10000.0
N_GROUPS = 4
EXP_PER_GROUP = 8
N_EXPERTS = N_GROUPS * EXP_PER_GROUP
EXPERT_FF = D_MODEL // 2
TOP_IN_GROUP = 2

A_SIZES = (GROUP_W, GROUP_W, GROUP_W, RWKV_W_RANK, RWKV_A_RANK, RWKV_G_RANK)
B_SIZES = (GROUP_W, GROUP_W, GROUP_W)
C_SIZES = (GROUP_W, GROUP_W, GROUP_W, GROUP_W, N_HEADS, N_HEADS)
D_SIZES = (GROUP_W, HEAD_DIM, HEAD_DIM, IDX_HEADS * IDX_DIM, IDX_DIM, IDX_HEADS)
A_COLS = sum(A_SIZES)
B_COLS = sum(B_SIZES)
C_COLS = sum(C_SIZES)
D_COLS = sum(D_SIZES)

LANE = 128
A_PAD = 896
B_PAD = 768
C_PAD = 1152
D_PAD = 640
P_PAD = A_PAD + B_PAD + C_PAD + D_PAD
ROUTER_PAD = LANE

IN_ROWS = 256
OUT_ROWS = 512
MOE_ROWS = 256
VMEM_LIMIT = 48 * 1024 * 1024


def _split_cols(t, sizes):
    return jnp.split(t, [int(i) for i in np.cumsum(sizes)[:-1]], axis=-1)


def _in_proj_kernel(x_ref, sc_ref, sh_ref, g_ref, w_ref, oa_ref, ob_ref, oc_ref, od_ref):
    x = x_ref[0]
    y = x * lax.rsqrt(jnp.mean(x * x, -1, keepdims=True) + NORM_EPS) * g_ref[...]
    h = y * (1.0 + sc_ref[0]) + sh_ref[0]
    p = jnp.dot(h.astype(BF16), w_ref[...], preferred_element_type=F32)
    oa_ref[0] = p[:, :A_PAD]
    ob_ref[0] = p[:, A_PAD:A_PAD + B_PAD]
    oc_ref[0] = p[:, A_PAD + B_PAD:A_PAD + B_PAD + C_PAD]
    od_ref[0] = p[:, A_PAD + B_PAD + C_PAD:]


def _in_proj(x, sc, sh, g, w_pad):
    B, S, D = x.shape
    row = lambda w: pl.BlockSpec((1, IN_ROWS, w), lambda b, i: (b, i, 0))
    vec = pl.BlockSpec((1, 1, D), lambda b, i: (b, 0, 0))
    return pl.pallas_call(
        _in_proj_kernel,
        grid=(B, S // IN_ROWS),
        in_specs=[row(D), vec, vec, pl.BlockSpec((1, D), lambda b, i: (0, 0)),
                  pl.BlockSpec((D, P_PAD), lambda b, i: (0, 0))],
        out_specs=[row(A_PAD), row(B_PAD), row(C_PAD), row(D_PAD)],
        out_shape=[jax.ShapeDtypeStruct((B, S, w), F32) for w in (A_PAD, B_PAD, C_PAD, D_PAD)],
        compiler_params=pltpu.CompilerParams(dimension_semantics=("parallel", "parallel"),
                                             vmem_limit_bytes=VMEM_LIMIT),
        name="in_proj",
    )(x, sc, sh, g, w_pad)


def _pad_w_in(w):
    wa, wb, wc, wd = _split_cols(w, (A_COLS, B_COLS, C_COLS, D_COLS))
    padc = lambda t, n: jnp.pad(t, ((0, 0), (0, n - t.shape[1])))
    return jnp.concatenate([padc(wa, A_PAD), padc(wb, B_PAD), padc(wc, C_PAD), padc(wd, D_PAD)], 1).astype(BF16)


def _split_bf16(t):
    hi = t.astype(BF16)
    lo = (t - hi.astype(F32)).astype(BF16)
    return hi, lo


def _out_proj_kernel(ya_ref, yb_ref, yc_ref, yd_ref, x_ref, gt_ref, sc_ref, sh_ref, g_ref, w_ref,
                     rhi_ref, rlo_ref, x1_ref, h2_ref, lg_ref):
    acc = jnp.zeros(x_ref.shape[1:], F32)
    for n, y_ref in enumerate((ya_ref, yb_ref, yc_ref, yd_ref)):
        acc += jnp.dot(y_ref[0].astype(BF16), w_ref[n * GROUP_W:(n + 1) * GROUP_W, :],
                       preferred_element_type=F32)
    x1 = x_ref[0] + gt_ref[0] * acc
    x1_ref[0] = x1
    y = x1 * lax.rsqrt(jnp.mean(x1 * x1, -1, keepdims=True) + NORM_EPS) * g_ref[...]
    h = y * (1.0 + sc_ref[0]) + sh_ref[0]
    hi, lo = _split_bf16(h)
    h2_ref[0] = hi
    lg_ref[0] = (jnp.dot(hi, rhi_ref[...], preferred_element_type=F32)
                 + jnp.dot(lo, rhi_ref[...], preferred_element_type=F32)
                 + jnp.dot(hi, rlo_ref[...], preferred_element_type=F32))


def _out_proj(ys, x, gt, sc, sh, g, w_out, r_hi, r_lo):
    B, S, D = x.shape
    row = lambda w: pl.BlockSpec((1, OUT_ROWS, w), lambda b, i: (b, i, 0))
    vec = pl.BlockSpec((1, 1, D), lambda b, i: (b, 0, 0))
    full = lambda a: pl.BlockSpec(a.shape, lambda b, i: (0,) * a.ndim)
    return pl.pallas_call(
        _out_proj_kernel,
        grid=(B, S // OUT_ROWS),
        in_specs=[row(GROUP_W)] * 4 + [row(D), vec, vec, vec, full(g), full(w_out), full(r_hi), full(r_lo)],
        out_specs=[row(D), row(D), row(ROUTER_PAD)],
        out_shape=[jax.ShapeDtypeStruct((B, S, D), F32), jax.ShapeDtypeStruct((B, S, D), BF16),
                   jax.ShapeDtypeStruct((B, S, ROUTER_PAD), F32)],
        compiler_params=pltpu.CompilerParams(dimension_semantics=("parallel", "parallel"),
                                             vmem_limit_bytes=VMEM_LIMIT),
        name="out_proj",
    )(*ys, x, gt, sc, sh, g, w_out, r_hi, r_lo)


def _moe_ffn_kernel(blk_e_ref, x_ref, wt_ref, w1_ref, w3_ref, w2_ref, o_ref):
    del blk_e_ref
    xb = x_ref[...]
    a = jnp.dot(xb, w1_ref[0], preferred_element_type=F32)
    b = jnp.dot(xb, w3_ref[0], preferred_element_type=F32)
    hmid = (a * jax.nn.sigmoid(a) * b).astype(BF16)
    y = jnp.dot(hmid, w2_ref[0], preferred_element_type=F32)
    o_ref[...] = y * wt_ref[...]


def _moe_ffn(blk_e, xs, wt, w1, w3, w2):
    n_slots, D = xs.shape
    n_blocks = n_slots // MOE_ROWS
    FF = w1.shape[-1]
    return pl.pallas_call(
        _moe_ffn_kernel,
        grid_spec=pltpu.PrefetchScalarGridSpec(
            num_scalar_prefetch=1,
            grid=(n_blocks,),
            in_specs=[pl.BlockSpec((MOE_ROWS, D), lambda i, e: (i, 0)),
                      pl.BlockSpec((MOE_ROWS, 1), lambda i, e: (i, 0)),
                      pl.BlockSpec((1, D, FF), lambda i, e: (e[i], 0, 0)),
                      pl.BlockSpec((1, D, FF), lambda i, e: (e[i], 0, 0)),
                      pl.BlockSpec((1, FF, D), lambda i, e: (e[i], 0, 0))],
            out_specs=pl.BlockSpec((MOE_ROWS, D), lambda i, e: (i, 0)),
        ),
        out_shape=jax.ShapeDtypeStruct((n_slots, D), F32),
        compiler_params=pltpu.CompilerParams(dimension_semantics=("arbitrary",),
                                             vmem_limit_bytes=VMEM_LIMIT),
        name="moe_ffn",
    )(blk_e, xs, wt, w1, w3, w2)


def _hier_moe(h2, logits, bg, be, w1, w3, w2):
    N, D = h2.shape
    g_prob = jax.nn.softmax(logits[:, :N_GROUPS] + bg, -1)
    g_p, g_idx = lax.top_k(g_prob, 1)
    e_logits = (logits[:, N_GROUPS:N_GROUPS + N_EXPERTS] + be).reshape(N, N_GROUPS, EXP_PER_GROUP)
    e_logits = jnp.take_along_axis(e_logits, g_idx[:, :, None], axis=1)[:, 0]
    e_p, e_idx = lax.top_k(jax.nn.softmax(e_logits, -1), TOP_IN_GROUP)
    gate = g_p * e_p / jnp.sum(e_p, -1, keepdims=True)
    expert = g_idx * EXP_PER_GROUP + e_idx
    n_asg = N * TOP_IN_GROUP
    flat_e = expert.reshape(n_asg)
    order = jnp.argsort(flat_e)
    se = flat_e[order]
    counts = jnp.bincount(flat_e, length=N_EXPERTS)
    start = jnp.cumsum(counts) - counts
    pad_counts = (counts + MOE_ROWS - 1) // MOE_ROWS * MOE_ROWS
    pad_end = jnp.cumsum(pad_counts)
    pad_start = pad_end - pad_counts
    slot = (pad_start[se] + jnp.arange(n_asg) - start[se]).astype(jnp.int32)
    n_blocks = n_asg // MOE_ROWS + N_EXPERTS
    n_slots = n_blocks * MOE_ROWS
    blk_start = jnp.arange(n_blocks) * MOE_ROWS
    blk_e = jnp.minimum(jnp.sum(pad_end[None, :] <= blk_start[:, None], 1), N_EXPERTS - 1).astype(jnp.int32)
    slot_e = jnp.repeat(blk_e, MOE_ROWS)
    slot_pos = jnp.arange(n_slots) - pad_start[slot_e]
    slot_real = (slot_pos < counts[slot_e]) & (jnp.arange(n_slots) < pad_end[N_EXPERTS - 1])
    slot_src = jnp.where(slot_real, start[slot_e] + slot_pos, 0)
    slot_tok = jnp.where(slot_real, (order // TOP_IN_GROUP)[slot_src], 0).astype(jnp.int32)
    slot_w = jnp.where(slot_real, gate.reshape(n_asg)[order][slot_src], 0.0)
    yb = _moe_ffn(blk_e, h2[slot_tok], slot_w[:, None], w1, w3, w2)
    asg_slot = slot[jnp.argsort(order)].reshape(N, TOP_IN_GROUP)
    return yb[asg_slot[:, 0]] + yb[asg_slot[:, 1]]


INT_MIN = -2 ** 31
DSA_KEY_STEP = 512


def _float_order_key(x):
    bits = pltpu.bitcast(x, jnp.int32)
    bits = jnp.where(x == 0.0, 0, bits)
    return bits ^ ((bits >> 31) & 0x7FFFFFFF)


def _row_count(mask):
    return jnp.sum(jnp.where(mask, 1.0, 0.0), axis=1, keepdims=True)


def _dsa_block(qd_ref, kdt_ref, vdw_ref, qi_ref, kit_ref, wi_ref, g_ref, o_ref, *, kl, n_sel):
    q0 = pl.program_id(1) * DSA_BLOCK
    kit = kit_ref[0, :, :kl]
    ki4 = jnp.concatenate([kit] * IDX_HEADS, axis=0)
    k_hi, k_lo = _split_bf16(ki4)
    qi = qi_ref[0]
    wi = wi_ref[0]
    lane_i = lax.broadcasted_iota(jnp.int32, qi.shape, 1)
    score = jnp.zeros((DSA_BLOCK, kl), F32)
    for h in range(IDX_HEADS):
        q_hi, q_lo = _split_bf16(jnp.where(lane_i // IDX_DIM == h, qi, 0.0))
        sc = (jnp.dot(q_hi, k_hi, preferred_element_type=F32) + jnp.dot(q_lo, k_hi, preferred_element_type=F32)
              + jnp.dot(q_hi, k_lo, preferred_element_type=F32))
        score = score + wi[:, h:h + 1] * jnp.maximum(sc, 0.0)
    kidx = lax.broadcasted_iota(jnp.int32, (DSA_BLOCK, kl), 1)
    qpos = q0 + lax.broadcasted_iota(jnp.int32, (DSA_BLOCK, kl), 0)
    adm = kidx <= qpos
    key = _float_order_key(jnp.where(adm, score, -jnp.inf))

    def value_bit(it, tau):
        cand = tau | jnp.left_shift(jnp.int32(1), 31 - it)
        cnt = _row_count(key >= (cand ^ INT_MIN))
        return jnp.where(cnt >= n_sel, cand, tau)

    tau = lax.fori_loop(0, 32, value_bit, jnp.zeros((DSA_BLOCK, 1), jnp.int32)) ^ INT_MIN
    gt = key > tau
    eq = (key == tau) & adm
    need = n_sel - _row_count(gt)
    n_eq = _row_count(eq)

    def index_bits():
        def index_bit(it, bound):
            cand = bound | jnp.left_shift(jnp.int32(1), 11 - it)
            cnt = _row_count(eq & (kidx < cand))
            return jnp.where(cnt <= need, cand, bound)
        return lax.fori_loop(0, 12, index_bit, jnp.zeros((DSA_BLOCK, 1), jnp.int32))

    bound = lax.cond(jnp.max(n_eq - need) > 0.0, index_bits,
                     lambda: jnp.full((DSA_BLOCK, 1), kl, jnp.int32))
    sel = gt | (eq & (kidx < bound))

    kdt = kdt_ref[0, :, :kl]
    kd4 = jnp.concatenate([kdt] * N_HEADS, axis=0).astype(BF16)
    vdw = vdw_ref[0, :kl, :].astype(BF16)
    qd = qd_ref[0]
    lane_h = lax.broadcasted_iota(jnp.int32, qd.shape, 1) // HEAD_DIM
    out = jnp.zeros(qd.shape, F32)
    for h in range(N_HEADS):
        qm = jnp.where(lane_h == h, qd, 0.0).astype(BF16)
        lg = jnp.dot(qm, kd4, preferred_element_type=F32) * HEAD_DIM ** -0.5
        lg = jnp.where(sel, lg, -jnp.inf)
        p = jnp.exp(lg - jnp.max(lg, axis=1, keepdims=True))
        r = jnp.dot(p.astype(BF16), vdw, preferred_element_type=F32) / jnp.sum(p, axis=1, keepdims=True)
        r = r * lax.rsqrt(jnp.mean(r * r, axis=1, keepdims=True) + NORM_EPS)
        out = out + jnp.where(lane_h == h, r, 0.0)
    o_ref[0] = out * g_ref[...]


def _dsa_kernel(qd_ref, kdt_ref, vdw_ref, qi_ref, kit_ref, wi_ref, g_ref, o_ref, *, kls, n_sel):
    blocks_per_step = DSA_KEY_STEP // DSA_BLOCK
    for j, kl in enumerate(kls):
        @pl.when(pl.program_id(1) // blocks_per_step == j)
        def _():
            _dsa_block(qd_ref, kdt_ref, vdw_ref, qi_ref, kit_ref, wi_ref, g_ref, o_ref, kl=kl, n_sel=n_sel)


def _dsa_attn_norm(qd, kd, vd, qi, ki, wi, g):
    B, S, _ = qd.shape
    n_sel = min(TOPK_MAX, S // 4)
    assert S % DSA_KEY_STEP == 0 and n_sel <= DSA_KEY_STEP
    kls = tuple(range(DSA_KEY_STEP, S + 1, DSA_KEY_STEP))
    kdt = jnp.swapaxes(kd, 1, 2)
    kit = jnp.swapaxes(ki, 1, 2)
    vdw = jnp.tile(vd, (1, 1, N_HEADS))
    blk = lambda w: pl.BlockSpec((1, DSA_BLOCK, w), lambda b, i: (b, i, 0))
    per_b = lambda r, c: pl.BlockSpec((1, r, c), lambda b, i: (b, 0, 0))
    return pl.pallas_call(
        functools.partial(_dsa_kernel, kls=kls, n_sel=n_sel),
        grid=(B, S // DSA_BLOCK),
        in_specs=[blk(GROUP_W), per_b(HEAD_DIM, S), per_b(S, GROUP_W), blk(IDX_HEADS * IDX_DIM),
                  per_b(IDX_DIM, S), blk(IDX_HEADS), pl.BlockSpec((1, GROUP_W), lambda b, i: (0, 0))],
        out_specs=blk(GROUP_W),
        out_shape=jax.ShapeDtypeStruct((B, S, GROUP_W), F32),
        compiler_params=pltpu.CompilerParams(dimension_semantics=("parallel", "parallel"),
                                             vmem_limit_bytes=VMEM_LIMIT),
        name="dsa_attn",
    )(qd, kdt, vdw, qi, kit, wi, g[None])


RWKV_CHUNK = 64
RWKV_LOW = RWKV_W_RANK + RWKV_A_RANK + RWKV_G_RANK
RWKV_STACK = N_HEADS * RWKV_CHUNK


def _dot(a, b):
    return jnp.dot(a, b, preferred_element_type=F32)


def _dot_nt(a, b):
    return lax.dot_general(a, b, (((1,), (1,)), ((), ())), preferred_element_type=F32)


def _dot_tn(a, b):
    return lax.dot_general(a, b, (((0,), (0,)), ((), ())), preferred_element_type=F32)


def _split3_bf16(t):
    p1 = t.astype(BF16)
    r1 = t - p1.astype(F32)
    p2 = r1.astype(BF16)
    p3 = (r1 - p2.astype(F32)).astype(BF16)
    return p1, p2, p3


def _dot_f32_by_exact(a, b_exact):
    return sum(_dot(p, b_exact) for p in _split3_bf16(a))


def _dot_exact_by_f32(a_exact, b):
    return sum(_dot(a_exact, p) for p in _split3_bf16(b))


def _dot3(a, b_hi, b_lo):
    a_hi, a_lo = _split_bf16(a)
    return _dot(a_hi, b_hi) + _dot(a_lo, b_hi) + _dot(a_hi, b_lo)


def _softplus(z):
    return jnp.maximum(z, 0.0) + jnp.log(1.0 + jnp.exp(-jnp.abs(z)))


def _rwkv_kernel(p_ref, mu_ref, vec_ref, lhi_ref, llo_ref, o_ref, state_ref, prev_ref):
    L, GW, ST = RWKV_CHUNK, GROUP_W, RWKV_STACK

    @pl.when(pl.program_id(1) == 0)
    def _():
        state_ref[...] = jnp.zeros_like(state_ref)
        prev_ref[...] = jnp.zeros_like(prev_ref)

    p = p_ref[0]
    row = lax.broadcasted_iota(jnp.int32, p.shape, 0)
    prev = jnp.where(row == 0, prev_ref[...], pltpu.roll(p, 1, axis=0))
    prev_ref[...] = p[L - 1:L, :]
    ps = p + (prev - p) * mu_ref[...]
    r, k, v = ps[:, :GW], ps[:, GW:2 * GW], ps[:, 2 * GW:3 * GW]
    low = ps[:, 3 * GW:]
    lane_low = lax.broadcasted_iota(jnp.int32, low.shape, 1)
    low = jnp.where(lane_low < RWKV_W_RANK, jnp.tanh(low),
                    jnp.where(lane_low < RWKV_W_RANK + RWKV_A_RANK, low, jax.nn.sigmoid(low)))
    up = _dot3(low, lhi_ref[...], llo_ref[...])
    w0, a0, k_k, k_a = vec_ref[0:1, :], vec_ref[1:2, :], vec_ref[2:3, :], vec_ref[3:4, :]
    r_k, ln_g, ln_b = vec_ref[4:5, :], vec_ref[5:6, :], vec_ref[6:7, :]
    logw = -jnp.exp(-_softplus(-(w0 + up[:, :GW])) - 0.5)
    rate = jax.nn.sigmoid(a0 + up[:, GW:2 * GW])
    gate = up[:, 2 * GW:]

    ri = lax.broadcasted_iota(jnp.int32, (ST, ST), 0)
    ci = lax.broadcasted_iota(jnp.int32, (ST, ST), 1)
    same_head = (ri // L) == (ci // L)
    ones_bd = jnp.where(same_head, 1.0, 0.0).astype(BF16)

    kk = k * k_k
    kk = kk / jnp.maximum(jnp.sqrt(_dot_f32_by_exact(kk * kk, ones_bd)), 1e-12)
    k = k * (1.0 + (rate - 1.0) * k_a)

    ti = lax.broadcasted_iota(jnp.int32, (L, L), 0)
    tj = lax.broadcasted_iota(jnp.int32, (L, L), 1)
    lc = _dot_exact_by_f32(jnp.where(tj <= ti, 1.0, 0.0).astype(BF16), logw)
    lc_last = lc[L - 1:L, :]
    dec_in = jnp.exp(lc)
    dec_out = jnp.exp(-lc)
    a_t = -kk * jnp.exp(lc - logw)
    b_t = kk * rate * dec_out
    k_t = k * dec_out
    r_t = r * dec_in
    to_end = jnp.exp(lc_last)

    stack = lambda t: jnp.concatenate([t] * N_HEADS, axis=0)
    bd = lambda t: jnp.where(same_head, stack(t), 0.0).astype(BF16)
    a_bd, r_bd, v_bd = bd(a_t), bd(r_t), bd(v)
    m = _dot_nt(jnp.concatenate([a_bd, r_bd], axis=0),
                jnp.concatenate([stack(b_t), stack(k_t)], axis=0).astype(BF16))
    strict = same_head & ((ci % L) < (ri % L))
    incl = same_head & ((ci % L) <= (ri % L))
    m_ab = jnp.where(strict, m[:ST, :ST], 0.0)
    m_ak = jnp.where(strict, m[:ST, ST:], 0.0).astype(BF16)
    m_rb = jnp.where(incl, m[ST:, :ST], 0.0).astype(BF16)
    m_rk = jnp.where(incl, m[ST:, ST:], 0.0).astype(BF16)

    inv = jnp.where(ri == ci, 1.0, 0.0) + m_ab
    pw = m_ab
    n_doublings = RWKV_CHUNK.bit_length() - 2
    for s in range(n_doublings):
        pw_b = pw.astype(BF16)
        pw = _dot(pw_b, pw_b)
        inv = inv + _dot(inv.astype(BF16), pw.astype(BF16))

    t0 = state_ref[...]
    t0_b = t0.astype(BF16)
    u = _dot(inv.astype(BF16), (_dot(a_bd, t0_b) + _dot(m_ak, v_bd)).astype(BF16)).astype(BF16)
    y_bd = _dot(r_bd, t0_b) + _dot(m_rb, u) + _dot(m_rk, v_bd)
    y = sum(y_bd[h * L:(h + 1) * L, :] for h in range(N_HEADS))

    to_end_col = jnp.sum(jnp.where(ri == ci, jnp.broadcast_to(to_end, (ST, ST)), 0.0), axis=1, keepdims=True)
    state_ref[...] = (to_end_col * t0 + _dot_tn(bd(b_t * to_end), u) + _dot_tn(bd(k_t * to_end), v_bd))

    inv_d = 1.0 / HEAD_DIM
    mean = _dot_f32_by_exact(y, ones_bd) * inv_d
    yc = y - mean
    var = _dot_f32_by_exact(yc * yc, ones_bd) * inv_d
    yn = yc * lax.rsqrt(var + RWKV_GN_EPS) * ln_g + ln_b
    bonus = _dot_f32_by_exact(r * k * r_k, ones_bd) * v
    o_ref[0] = (yn + bonus) * gate


def _rwkv7_time_mix(p, mu, w0, w2, a0, a2, g2, k_k, k_a, r_k, ln_g, ln_b):
    B, S, _ = p.shape
    GW = GROUP_W
    assert S % RWKV_CHUNK == 0 and RWKV_STACK == GW
    low_w = jnp.zeros((RWKV_LOW, 3 * GW), F32)
    low_w = low_w.at[:RWKV_W_RANK, :GW].set(w2)
    low_w = low_w.at[RWKV_W_RANK:RWKV_W_RANK + RWKV_A_RANK, GW:2 * GW].set(a2)
    low_w = low_w.at[RWKV_W_RANK + RWKV_A_RANK:, 2 * GW:].set(g2)
    l_hi, l_lo = _split_bf16(low_w)
    vecs = jnp.stack([w0, a0, k_k, k_a, r_k, ln_g, ln_b, jnp.zeros_like(w0)], 0)
    full = lambda a: pl.BlockSpec(a.shape, lambda b, c: (0,) * a.ndim)
    mu2 = mu[None]
    return pl.pallas_call(
        _rwkv_kernel,
        grid=(B, S // RWKV_CHUNK),
        in_specs=[pl.BlockSpec((1, RWKV_CHUNK, A_PAD), lambda b, c: (b, c, 0)),
                  full(mu2), full(vecs), full(l_hi), full(l_lo)],
        out_specs=pl.BlockSpec((1, RWKV_CHUNK, GW), lambda b, c: (b, c, 0)),
        out_shape=jax.ShapeDtypeStruct((B, S, GW), F32),
        scratch_shapes=[pltpu.VMEM((RWKV_STACK, GW), F32), pltpu.VMEM((1, A_PAD), F32)],
        compiler_params=pltpu.CompilerParams(dimension_semantics=("parallel", "arbitrary"),
                                             vmem_limit_bytes=VMEM_LIMIT),
        name="rwkv7",
    )(p, mu2, vecs, l_hi, l_lo)


def _sb_kernel(q_ref, k_ref, v_ref, g_ref, o_ref, kbd_ref, vbd_ref):
    i = pl.program_id(1)
    T, GW, H = SB_BLOCK, GROUP_W, N_HEADS
    lane_h = lax.broadcasted_iota(jnp.int32, (T, GW), 1) // HEAD_DIM
    k_new, v_new = k_ref[0], v_ref[0]
    for h in range(H):
        kbd_ref[i, h * T:(h + 1) * T, :] = jnp.where(lane_h == h, k_new, 0.0).astype(BF16)
        vbd_ref[i, h * T:(h + 1) * T, :] = jnp.where(lane_h == h, v_new, 0.0).astype(BF16)

    q = q_ref[0].astype(BF16)
    si = lax.broadcasted_iota(jnp.int32, (T, 2 * T), 0)
    sj = lax.broadcasted_iota(jnp.int32, (T, 2 * T), 1)
    later_and_all = jnp.where((si > sj) | (sj >= T), 1.0, 0.0).astype(BF16)
    qrow = lax.broadcasted_iota(jnp.int32, (T, H * T), 0)
    kcol = lax.broadcasted_iota(jnp.int32, (T, H * T), 1) % T
    causal = kcol < qrow

    def key_block(j, state, diagonal):
        carry, acc = state
        z = _dot_nt(q, kbd_ref[j]) * HEAD_DIM ** -0.5
        soft = jnp.log(1.0 + jnp.exp(-jnp.abs(z)))
        log1m = -(jnp.maximum(z, 0.0) + soft)
        log_sig = jnp.minimum(z, 0.0) - soft
        log1m_in = (jnp.where(causal, log1m, 0.0) if diagonal else log1m).astype(BF16)
        sums = [_dot(log1m_in[:, h * T:(h + 1) * T], later_and_all) for h in range(H)]
        suffix = jnp.concatenate([s[:, :T] for s in sums], axis=1)
        total = jnp.concatenate([s[:, T:] for s in sums], axis=1)
        att = jnp.exp(log_sig + suffix + carry)
        if diagonal:
            att = jnp.where(causal, att, 0.0)
        return carry + total, acc + _dot(att.astype(BF16), vbd_ref[j])

    state = key_block(i, (jnp.zeros((T, H * T), F32), jnp.zeros((T, GW), F32)), True)
    _, y = lax.fori_loop(0, i, lambda it, st: key_block(i - 1 - it, st, False), state)

    hi_ = lax.broadcasted_iota(jnp.int32, (GW, GW), 0) // HEAD_DIM
    hj_ = lax.broadcasted_iota(jnp.int32, (GW, GW), 1) // HEAD_DIM
    ones_bd = jnp.where(hi_ == hj_, 1.0, 0.0).astype(BF16)
    ms = _dot_f32_by_exact(y * y, ones_bd) * (1.0 / HEAD_DIM)
    o_ref[0] = y * lax.rsqrt(ms + NORM_EPS) * g_ref[...]


def _stick_breaking_norm(q, k, v, g):
    B, S, GW = q.shape
    assert S % SB_BLOCK == 0
    blk = pl.BlockSpec((1, SB_BLOCK, GW), lambda b, i: (b, i, 0))
    stacked = pltpu.VMEM((S // SB_BLOCK, N_HEADS * SB_BLOCK, GW), BF16)
    return pl.pallas_call(
        _sb_kernel,
        grid=(B, S // SB_BLOCK),
        in_specs=[blk, blk, blk, pl.BlockSpec((1, GW), lambda b, i: (0, 0))],
        out_specs=blk,
        out_shape=jax.ShapeDtypeStruct((B, S, GW), F32),
        scratch_shapes=[stacked, stacked],
        compiler_params=pltpu.CompilerParams(dimension_semantics=("parallel", "arbitrary"),
                                             vmem_limit_bytes=VMEM_LIMIT),
        name="stick_breaking",
    )(q, k, v, g[None])


def _rms_norm(x, g):
    xf = x.astype(F32)
    y = xf * lax.rsqrt(jnp.mean(xf * xf, -1, keepdims=True) + NORM_EPS)
    return (y * g.astype(F32)).astype(x.dtype)


def _rope(x, pos):
    half = x.shape[-1] // 2
    inv = ROPE_THETA ** (-jnp.arange(half, dtype=F32) / half)
    ang = pos.astype(F32)[:, None] * inv[None, :]
    cos = jnp.cos(ang)[None, :, None, :]
    sin = jnp.sin(ang)[None, :, None, :]
    xf = x.astype(F32)
    x1, x2 = xf[..., :half], xf[..., half:]
    return jnp.concatenate([x1 * cos - x2 * sin, x2 * cos + x1 * sin], -1).astype(x.dtype)


def _token_shift(p, mu):
    prev = jnp.pad(p, ((0, 0), (1, 0), (0, 0)))[:, :-1]
    return p + (prev - p) * mu


def _causal_dwconv(x, w, b):
    ch = x.shape[-1]
    y = lax.conv_general_dilated(x, w[:, None, :].astype(x.dtype), window_strides=(1,),
                                 padding=((w.shape[0] - 1, 0),),
                                 dimension_numbers=('NWC', 'WIO', 'NWC'), feature_group_count=ch)
    return y + b


def _mlstm_chunkwise(q, k, v, log_i, log_f):
    B, S, H, d = q.shape
    L = ML_CHUNK
    nc = S // L

    def to_chunks(t):
        t = t.reshape((B, nc, L, H) + t.shape[3:])
        return jnp.moveaxis(jnp.moveaxis(t, 3, 2), 1, 0)

    causal = jnp.tril(jnp.ones((L, L), bool))

    def chunk(carry, inp):
        c_st, n_st, m_st = carry
        qc, kc, vc, li, lf = inp
        b = jnp.cumsum(lf, -1)
        dmat = jnp.where(causal, b[..., :, None] - b[..., None, :] + li[..., None, :], -jnp.inf)
        g_inter = b + m_st[..., None]
        m_t = jnp.maximum(g_inter, jnp.max(dmat, -1))
        s_inter = jnp.exp(g_inter - m_t)
        sqk = jnp.einsum('bhtd,bhsd->bhts', qc, kc) * jnp.exp(dmat - m_t[..., None])
        num = s_inter[..., None] * jnp.einsum('bhvd,bhtd->bhtv', c_st, qc) + jnp.einsum('bhts,bhsv->bhtv', sqk, vc)
        den = s_inter * jnp.einsum('bhd,bhtd->bht', n_st, qc) + jnp.sum(sqk, -1)
        h = num / jnp.maximum(jnp.abs(den), jnp.exp(-m_t))[..., None]
        b_last = b[..., -1]
        dec = b_last[..., None] - b + li
        m_new = jnp.maximum(b_last + m_st, jnp.max(dec, -1))
        wk = jnp.exp(dec - m_new[..., None])
        s_old = jnp.exp(b_last + m_st - m_new)
        c_st = s_old[..., None, None] * c_st + jnp.einsum('bhs,bhsv,bhsd->bhvd', wk, vc, kc)
        n_st = s_old[..., None] * n_st + jnp.einsum('bhs,bhsd->bhd', wk, kc)
        return (c_st, n_st, m_new), h

    init = (jnp.zeros((B, H, d, d), F32), jnp.zeros((B, H, d), F32), jnp.zeros((B, H), F32))
    _, hs = lax.scan(chunk, init, tuple(to_chunks(t) for t in (q, k, v, log_i, log_f)))
    hs = jnp.moveaxis(jnp.moveaxis(hs, 0, 1), 2, 3)
    return hs.reshape(B, S, H, d)


def _mlstm_mix(p, conv_w, conv_b, ig_b, fg_b, norm_g):
    B, S, _ = p.shape
    H, d = N_HEADS, HEAD_DIM
    qk = jax.nn.silu(_causal_dwconv(p[..., :2 * GROUP_W], conv_w, conv_b))
    q, k = qk[..., :GROUP_W], qk[..., GROUP_W:]
    _, _, v, o, ig, fg = _split_cols(p, C_SIZES)
    cap = lambda t: GATE_CAP * jnp.tanh(t / GATE_CAP)
    log_i = cap((ig + ig_b).astype(F32))
    log_f = jax.nn.log_sigmoid(cap((fg + fg_b).astype(F32)))
    heads = lambda t: t.astype(F32).reshape(B, S, H, d)
    h = _mlstm_chunkwise(heads(q), heads(k) * d ** -0.5, heads(v), log_i, log_f)
    h = _rms_norm(h, norm_g.reshape(H, d)).reshape(B, S, GROUP_W)
    return (jax.nn.sigmoid(o.astype(F32)) * h).astype(p.dtype)


def kernel(x, c, ada_w, ada_b, norm1_g, norm2_g, w_in, rk_mu, rk_w0, rk_w2, rk_a0, rk_a2, rk_g2, rk_kk, rk_ka, rk_rk, rk_ln_g, rk_ln_b, sb_norm_g, ml_conv_w, ml_conv_b, ml_ig_b, ml_fg_b, ml_norm_g, ds_qn_g, ds_kn_g, ds_out_g, w_out, moe_wg, moe_bg, moe_we, moe_be, moe_w1, moe_w3, moe_w2):
    B, S, D = x.shape
    H, d = N_HEADS, HEAD_DIM
    depth = ada_w.shape[0]
    pos = jnp.arange(S)
    c_act = jax.nn.silu(c)
    for l in range(depth):
        mod = (c_act @ ada_w[l] + ada_b[l])[:, None, :]
        sh1, sc1, gt1, sh2, sc2, gt2 = jnp.split(mod, 6, axis=-1)

        pA, pB, pC, pD = _in_proj(x, sc1, sh1, norm1_g[l][None], _pad_w_in(w_in[l]))
        pC = pC[..., :C_COLS]
        pD = pD[..., :D_COLS]

        yA = _rwkv7_time_mix(pA, rk_mu[l], rk_w0[l], rk_w2[l], rk_a0[l], rk_a2[l], rk_g2[l],
                             rk_kk[l], rk_ka[l], rk_rk[l], rk_ln_g[l], rk_ln_b[l])

        qb, kb, vb = _split_cols(pB, B_SIZES)
        yB = _stick_breaking_norm(qb, kb, vb, sb_norm_g[l])

        yC = _mlstm_mix(pC, ml_conv_w[l], ml_conv_b[l], ml_ig_b[l], ml_fg_b[l], ml_norm_g[l])

        qd, kd, vd, qi, ki, wi = _split_cols(pD, D_SIZES)
        qd = _rope(_rms_norm(qd.reshape(B, S, H, d), ds_qn_g[l]), pos).reshape(B, S, GROUP_W)
        kd = _rope(_rms_norm(kd[:, :, None, :], ds_kn_g[l]), pos)[:, :, 0]
        qi = _rope(qi.reshape(B, S, IDX_HEADS, IDX_DIM), pos).reshape(B, S, IDX_HEADS * IDX_DIM)
        ki = _rope(ki[:, :, None, :], pos)[:, :, 0]
        wi = wi * (IDX_HEADS ** -0.5 * IDX_DIM ** -0.5)
        yD = _dsa_attn_norm(qd, kd, vd, qi, ki, wi, ds_out_g[l])

        router = jnp.pad(jnp.concatenate([moe_wg[l], moe_we[l]], 1),
                         ((0, 0), (0, ROUTER_PAD - N_GROUPS - N_EXPERTS)))
        r_hi, r_lo = _split_bf16(router)
        x1, h2, logits = _out_proj((yA, yB, yC, yD), x, gt1, sc2, sh2, norm2_g[l][None],
                                   w_out[l].astype(BF16), r_hi, r_lo)

        moe = _hier_moe(h2.reshape(B * S, D), logits.reshape(B * S, ROUTER_PAD), moe_bg[l], moe_be[l],
                        moe_w1[l].astype(BF16), moe_w3[l].astype(BF16), moe_w2[l].astype(BF16))
        x = x1 + gt2 * moe.reshape(B, S, D)
    return x
```

```python
import functools

import jax
import jax.numpy as jnp
import numpy as np
from jax import lax
from jax.experimental import pallas as pl
from jax.experimental.pallas import tpu as pltpu

F32 = jnp.float32
BF16 = jnp.bfloat16

D_MODEL = 1024
N_MIXERS = 4
GROUP_W = D_MODEL // N_MIXERS
HEAD_DIM = 64
N_HEADS = GROUP_W // HEAD_DIM
NORM_EPS = 1e-6
RWKV_W_RANK = 32
RWKV_A_RANK = 32
RWKV_G_RANK = 64
RWKV_GN_EPS = 64e-5
SB_BLOCK = 128
ML_CHUNK = 64
ML_CONV = 4
GATE_CAP = 15.0
DSA_BLOCK = 128
IDX_HEADS = 4
IDX_DIM = 32
TOPK_MAX = 256
ROPE_THETA = 10000.0
N_GROUPS = 4
EXP_PER_GROUP = 8
N_EXPERTS = N_GROUPS * EXP_PER_GROUP
EXPERT_FF = D_MODEL // 2
TOP_IN_GROUP = 2

A_SIZES = (GROUP_W, GROUP_W, GROUP_W, RWKV_W_RANK, RWKV_A_RANK, RWKV_G_RANK)
B_SIZES = (GROUP_W, GROUP_W, GROUP_W)
C_SIZES = (GROUP_W, GROUP_W, GROUP_W, GROUP_W, N_HEADS, N_HEADS)
D_SIZES = (GROUP_W, HEAD_DIM, HEAD_DIM, IDX_HEADS * IDX_DIM, IDX_DIM, IDX_HEADS)
A_COLS = sum(A_SIZES)
B_COLS = sum(B_SIZES)
C_COLS = sum(C_SIZES)
D_COLS = sum(D_SIZES)

LANE = 128
A_PAD = 896
B_PAD = 768
C_PAD = 1152
D_PAD = 640
P_PAD = A_PAD + B_PAD + C_PAD + D_PAD
ROUTER_PAD = LANE

IN_ROWS = 256
OUT_ROWS = 512
MOE_ROWS = 256
VMEM_LIMIT = 48 * 1024 * 1024


def _split_cols(t, sizes):
    return jnp.split(t, [int(i) for i in np.cumsum(sizes)[:-1]], axis=-1)


def _in_proj_kernel(x_ref, sc_ref, sh_ref, g_ref, w_ref, oa_ref, ob_ref, oc_ref, od_ref):
    x = x_ref[0]
    y = x * lax.rsqrt(jnp.mean(x * x, -1, keepdims=True) + NORM_EPS) * g_ref[...]
    h = y * (1.0 + sc_ref[0]) + sh_ref[0]
    p = jnp.dot(h.astype(BF16), w_ref[...], preferred_element_type=F32)
    oa_ref[0] = p[:, :A_PAD]
    ob_ref[0] = p[:, A_PAD:A_PAD + B_PAD]
    oc_ref[0] = p[:, A_PAD + B_PAD:A_PAD + B_PAD + C_PAD]
    od_ref[0] = p[:, A_PAD + B_PAD + C_PAD:]


def _in_proj(x, sc, sh, g, w_pad):
    B, S, D = x.shape
    row = lambda w: pl.BlockSpec((1, IN_ROWS, w), lambda b, i: (b, i, 0))
    vec = pl.BlockSpec((1, 1, D), lambda b, i: (b, 0, 0))
    return pl.pallas_call(
        _in_proj_kernel,
        grid=(B, S // IN_ROWS),
        in_specs=[row(D), vec, vec, pl.BlockSpec((1, D), lambda b, i: (0, 0)),
                  pl.BlockSpec((D, P_PAD), lambda b, i: (0, 0))],
        out_specs=[row(A_PAD), row(B_PAD), row(C_PAD), row(D_PAD)],
        out_shape=[jax.ShapeDtypeStruct((B, S, w), F32) for w in (A_PAD, B_PAD, C_PAD, D_PAD)],
        compiler_params=pltpu.CompilerParams(dimension_semantics=("parallel", "parallel"),
                                             vmem_limit_bytes=VMEM_LIMIT),
        name="in_proj",
    )(x, sc, sh, g, w_pad)


def _pad_w_in(w):
    wa, wb, wc, wd = _split_cols(w, (A_COLS, B_COLS, C_COLS, D_COLS))
    padc = lambda t, n: jnp.pad(t, ((0, 0), (0, n - t.shape[1])))
    return jnp.concatenate([padc(wa, A_PAD), padc(wb, B_PAD), padc(wc, C_PAD), padc(wd, D_PAD)], 1).astype(BF16)


def _split_bf16(t):
    hi = t.astype(BF16)
    lo = (t - hi.astype(F32)).astype(BF16)
    return hi, lo


def _out_proj_kernel(ya_ref, yb_ref, yc_ref, yd_ref, x_ref, gt_ref, sc_ref, sh_ref, g_ref, w_ref,
                     rhi_ref, rlo_ref, x1_ref, h2_ref, lg_ref):
    acc = jnp.zeros(x_ref.shape[1:], F32)
    for n, y_ref in enumerate((ya_ref, yb_ref, yc_ref, yd_ref)):
        acc += jnp.dot(y_ref[0].astype(BF16), w_ref[n * GROUP_W:(n + 1) * GROUP_W, :],
                       preferred_element_type=F32)
    x1 = x_ref[0] + gt_ref[0] * acc
    x1_ref[0] = x1
    y = x1 * lax.rsqrt(jnp.mean(x1 * x1, -1, keepdims=True) + NORM_EPS) * g_ref[...]
    h = y * (1.0 + sc_ref[0]) + sh_ref[0]
    hi, lo = _split_bf16(h)
    h2_ref[0] = hi
    lg_ref[0] = (jnp.dot(hi, rhi_ref[...], preferred_element_type=F32)
                 + jnp.dot(lo, rhi_ref[...], preferred_element_type=F32)
                 + jnp.dot(hi, rlo_ref[...], preferred_element_type=F32))


def _out_proj(ys, x, gt, sc, sh, g, w_out, r_hi, r_lo):
    B, S, D = x.shape
    row = lambda w: pl.BlockSpec((1, OUT_ROWS, w), lambda b, i: (b, i, 0))
    vec = pl.BlockSpec((1, 1, D), lambda b, i: (b, 0, 0))
    full = lambda a: pl.BlockSpec(a.shape, lambda b, i: (0,) * a.ndim)
    return pl.pallas_call(
        _out_proj_kernel,
        grid=(B, S // OUT_ROWS),
        in_specs=[row(GROUP_W)] * 4 + [row(D), vec, vec, vec, full(g), full(w_out), full(r_hi), full(r_lo)],
        out_specs=[row(D), row(D), row(ROUTER_PAD)],
        out_shape=[jax.ShapeDtypeStruct((B, S, D), F32), jax.ShapeDtypeStruct((B, S, D), BF16),
                   jax.ShapeDtypeStruct((B, S, ROUTER_PAD), F32)],
        compiler_params=pltpu.CompilerParams(dimension_semantics=("parallel", "parallel"),
                                             vmem_limit_bytes=VMEM_LIMIT),
        name="out_proj",
    )(*ys, x, gt, sc, sh, g, w_out, r_hi, r_lo)


def _moe_ffn_kernel(blk_e_ref, x_ref, wt_ref, w1_ref, w3_ref, w2_ref, o_ref):
    del blk_e_ref
    xb = x_ref[...]
    a = jnp.dot(xb, w1_ref[0], preferred_element_type=F32)
    b = jnp.dot(xb, w3_ref[0], preferred_element_type=F32)
    hmid = (a * jax.nn.sigmoid(a) * b).astype(BF16)
    y = jnp.dot(hmid, w2_ref[0], preferred_element_type=F32)
    o_ref[...] = y * wt_ref[...]


def _moe_ffn(blk_e, xs, wt, w1, w3, w2):
    n_slots, D = xs.shape
    n_blocks = n_slots // MOE_ROWS
    FF = w1.shape[-1]
    return pl.pallas_call(
        _moe_ffn_kernel,
        grid_spec=pltpu.PrefetchScalarGridSpec(
            num_scalar_prefetch=1,
            grid=(n_blocks,),
            in_specs=[pl.BlockSpec((MOE_ROWS, D), lambda i, e: (i, 0)),
                      pl.BlockSpec((MOE_ROWS, 1), lambda i, e: (i, 0)),
                      pl.BlockSpec((1, D, FF), lambda i, e: (e[i], 0, 0)),
                      pl.BlockSpec((1, D, FF), lambda i, e: (e[i], 0, 0)),
                      pl.BlockSpec((1, FF, D), lambda i, e: (e[i], 0, 0))],
            out_specs=pl.BlockSpec((MOE_ROWS, D), lambda i, e: (i, 0)),
        ),
        out_shape=jax.ShapeDtypeStruct((n_slots, D), F32),
        compiler_params=pltpu.CompilerParams(dimension_semantics=("arbitrary",),
                                             vmem_limit_bytes=VMEM_LIMIT),
        name="moe_ffn",
    )(blk_e, xs, wt, w1, w3, w2)


def _hier_moe(h2, logits, bg, be, w1, w3, w2):
    N, D = h2.shape
    g_prob = jax.nn.softmax(logits[:, :N_GROUPS] + bg, -1)
    g_p, g_idx = lax.top_k(g_prob, 1)
    e_logits = (logits[:, N_GROUPS:N_GROUPS + N_EXPERTS] + be).reshape(N, N_GROUPS, EXP_PER_GROUP)
    e_logits = jnp.take_along_axis(e_logits, g_idx[:, :, None], axis=1)[:, 0]
    e_p, e_idx = lax.top_k(jax.nn.softmax(e_logits, -1), TOP_IN_GROUP)
    gate = g_p * e_p / jnp.sum(e_p, -1, keepdims=True)
    expert = g_idx * EXP_PER_GROUP + e_idx
    n_asg = N * TOP_IN_GROUP
    flat_e = expert.reshape(n_asg)
    order = jnp.argsort(flat_e)
    se = flat_e[order]
    counts = jnp.bincount(flat_e, length=N_EXPERTS)
    start = jnp.cumsum(counts) - counts
    pad_counts = (counts + MOE_ROWS - 1) // MOE_ROWS * MOE_ROWS
    pad_end = jnp.cumsum(pad_counts)
    pad_start = pad_end - pad_counts
    slot = (pad_start[se] + jnp.arange(n_asg) - start[se]).astype(jnp.int32)
    n_blocks = n_asg // MOE_ROWS + N_EXPERTS
    n_slots = n_blocks * MOE_ROWS
    blk_start = jnp.arange(n_blocks) * MOE_ROWS
    blk_e = jnp.minimum(jnp.sum(pad_end[None, :] <= blk_start[:, None], 1), N_EXPERTS - 1).astype(jnp.int32)
    slot_e = jnp.repeat(blk_e, MOE_ROWS)
    slot_pos = jnp.arange(n_slots) - pad_start[slot_e]
    slot_real = (slot_pos < counts[slot_e]) & (jnp.arange(n_slots) < pad_end[N_EXPERTS - 1])
    slot_src = jnp.where(slot_real, start[slot_e] + slot_pos, 0)
    slot_tok = jnp.where(slot_real, (order // TOP_IN_GROUP)[slot_src], 0).astype(jnp.int32)
    slot_w = jnp.where(slot_real, gate.reshape(n_asg)[order][slot_src], 0.0)
    yb = _moe_ffn(blk_e, h2[slot_tok], slot_w[:, None], w1, w3, w2)
    asg_slot = slot[jnp.argsort(order)].reshape(N, TOP_IN_GROUP)
    return yb[asg_slot[:, 0]] + yb[asg_slot[:, 1]]


INT_MIN = -2 ** 31
DSA_KEY_STEP = 512
DSA_CHAINS = 4


def _float_order_key(x):
    bits = pltpu.bitcast(x, jnp.int32)
    bits = jnp.where(x == 0.0, 0, bits)
    return bits ^ ((bits >> 31) & 0x7FFFFFFF)


def _row_count(mask):
    return jnp.sum(jnp.where(mask, 1.0, 0.0), axis=1, keepdims=True)


def _dsa_block(qd_ref, kdt_ref, vdw_ref, qi_ref, kit_ref, wi_ref, g_ref, o_ref, *, kl, n_sel):
    q0 = pl.program_id(1) * DSA_BLOCK
    kit = kit_ref[0, :, :kl]
    ki4 = jnp.concatenate([kit] * IDX_HEADS, axis=0)
    k_hi, k_lo = _split_bf16(ki4)
    qi = qi_ref[0]
    wi = wi_ref[0]
    lane_i = lax.broadcasted_iota(jnp.int32, qi.shape, 1)
    score = jnp.zeros((DSA_BLOCK, kl), F32)
    for h in range(IDX_HEADS):
        q_hi, q_lo = _split_bf16(jnp.where(lane_i // IDX_DIM == h, qi, 0.0))
        sc = (jnp.dot(q_hi, k_hi, preferred_element_type=F32) + jnp.dot(q_lo, k_hi, preferred_element_type=F32)
              + jnp.dot(q_hi, k_lo, preferred_element_type=F32))
        score = score + wi[:, h:h + 1] * jnp.maximum(sc, 0.0)
    kidx = lax.broadcasted_iota(jnp.int32, (DSA_BLOCK, kl), 1)
    qpos = q0 + lax.broadcasted_iota(jnp.int32, (DSA_BLOCK, kl), 0)
    adm = kidx <= qpos
    key = _float_order_key(jnp.where(adm, score, -jnp.inf))

    rows = DSA_BLOCK // DSA_CHAINS
    key_groups = [key[c * rows:(c + 1) * rows] for c in range(DSA_CHAINS)]

    def value_bit(it, taus):
        bit = jnp.left_shift(jnp.int32(1), 31 - it)
        out = []
        for key_c, tau_c in zip(key_groups, taus):
            cand = tau_c | bit
            cnt = _row_count(key_c >= (cand ^ INT_MIN))
            out.append(jnp.where(cnt >= n_sel, cand, tau_c))
        return tuple(out)

    taus = lax.fori_loop(0, 32, value_bit, tuple(jnp.zeros((rows, 1), jnp.int32) for _ in range(DSA_CHAINS)))
    tau = jnp.concatenate(taus, axis=0) ^ INT_MIN
    gt = key > tau
    eq = (key == tau) & adm
    need = n_sel - _row_count(gt)
    n_eq = _row_count(eq)

    def index_bits():
        def index_bit(it, bound):
            cand = bound | jnp.left_shift(jnp.int32(1), 11 - it)
            cnt = _row_count(eq & (kidx < cand))
            return jnp.where(cnt <= need, cand, bound)
        return lax.fori_loop(0, 12, index_bit, jnp.zeros((DSA_BLOCK, 1), jnp.int32))

    bound = lax.cond(jnp.max(n_eq - need) > 0.0, index_bits,
                     lambda: jnp.full((DSA_BLOCK, 1), kl, jnp.int32))
    sel = gt | (eq & (kidx < bound))

    kdt = kdt_ref[0, :, :kl]
    kd4 = jnp.concatenate([kdt] * N_HEADS, axis=0).astype(BF16)
    vdw = vdw_ref[0, :kl, :].astype(BF16)
    qd = qd_ref[0]
    lane_h = lax.broadcasted_iota(jnp.int32, qd.shape, 1) // HEAD_DIM
    out = jnp.zeros(qd.shape, F32)
    for h in range(N_HEADS):
        qm = jnp.where(lane_h == h, qd, 0.0).astype(BF16)
        lg = jnp.dot(qm, kd4, preferred_element_type=F32) * HEAD_DIM ** -0.5
        lg = jnp.where(sel, lg, -jnp.inf)
        p = jnp.exp(lg - jnp.max(lg, axis=1, keepdims=True))
        r = jnp.dot(p.astype(BF16), vdw, preferred_element_type=F32) / jnp.sum(p, axis=1, keepdims=True)
        r = r * lax.rsqrt(jnp.mean(r * r, axis=1, keepdims=True) + NORM_EPS)
        out = out + jnp.where(lane_h == h, r, 0.0)
    o_ref[0] = out * g_ref[...]


def _dsa_kernel(qd_ref, kdt_ref, vdw_ref, qi_ref, kit_ref, wi_ref, g_ref, o_ref, *, kls, n_sel):
    blocks_per_step = DSA_KEY_STEP // DSA_BLOCK
    for j, kl in enumerate(kls):
        @pl.when(pl.program_id(1) // blocks_per_step == j)
        def _():
            _dsa_block(qd_ref, kdt_ref, vdw_ref, qi_ref, kit_ref, wi_ref, g_ref, o_ref, kl=kl, n_sel=n_sel)


def _dsa_attn_norm(qd, kd, vd, qi, ki, wi, g):
    B, S, _ = qd.shape
    n_sel = min(TOPK_MAX, S // 4)
    assert S % DSA_KEY_STEP == 0 and n_sel <= DSA_KEY_STEP
    kls = tuple(range(DSA_KEY_STEP, S + 1, DSA_KEY_STEP))
    kdt = jnp.swapaxes(kd, 1, 2)
    kit = jnp.swapaxes(ki, 1, 2)
    vdw = jnp.tile(vd, (1, 1, N_HEADS))
    blk = lambda w: pl.BlockSpec((1, DSA_BLOCK, w), lambda b, i: (b, i, 0))
    per_b = lambda r, c: pl.BlockSpec((1, r, c), lambda b, i: (b, 0, 0))
    return pl.pallas_call(
        functools.partial(_dsa_kernel, kls=kls, n_sel=n_sel),
        grid=(B, S // DSA_BLOCK),
        in_specs=[blk(GROUP_W), per_b(HEAD_DIM, S), per_b(S, GROUP_W), blk(IDX_HEADS * IDX_DIM),
                  per_b(IDX_DIM, S), blk(IDX_HEADS), pl.BlockSpec((1, GROUP_W), lambda b, i: (0, 0))],
        out_specs=blk(GROUP_W),
        out_shape=jax.ShapeDtypeStruct((B, S, GROUP_W), F32),
        compiler_params=pltpu.CompilerParams(dimension_semantics=("parallel", "parallel"),
                                             vmem_limit_bytes=VMEM_LIMIT),
        name="dsa_attn",
    )(qd, kdt, vdw, qi, kit, wi, g[None])


RWKV_CHUNK = 64
RWKV_LOW = RWKV_W_RANK + RWKV_A_RANK + RWKV_G_RANK
RWKV_STACK = N_HEADS * RWKV_CHUNK
RWKV_BATCH = 2


def _dot(a, b):
    return jnp.dot(a, b, preferred_element_type=F32)


def _dot_nt(a, b):
    return lax.dot_general(a, b, (((1,), (1,)), ((), ())), preferred_element_type=F32)


def _dot_tn(a, b):
    return lax.dot_general(a, b, (((0,), (0,)), ((), ())), preferred_element_type=F32)


def _split3_bf16(t):
    p1 = t.astype(BF16)
    r1 = t - p1.astype(F32)
    p2 = r1.astype(BF16)
    p3 = (r1 - p2.astype(F32)).astype(BF16)
    return p1, p2, p3


def _dot_f32_by_exact(a, b_exact):
    return sum(_dot(p, b_exact) for p in _split3_bf16(a))


def _dot_exact_by_f32(a_exact, b):
    return sum(_dot(a_exact, p) for p in _split3_bf16(b))


def _dot3(a, b_hi, b_lo):
    a_hi, a_lo = _split_bf16(a)
    return _dot(a_hi, b_hi) + _dot(a_lo, b_hi) + _dot(a_hi, b_lo)


def _softplus(z):
    return jnp.maximum(z, 0.0) + jnp.log(1.0 + jnp.exp(-jnp.abs(z)))


def _rwkv_kernel(p_ref, mu_ref, vec_ref, lhi_ref, llo_ref, o_ref, state_ref, prev_ref):
    @pl.when(pl.program_id(1) == 0)
    def _():
        state_ref[...] = jnp.zeros_like(state_ref)
        prev_ref[...] = jnp.zeros_like(prev_ref)

    for n in range(RWKV_BATCH):
        _rwkv_chunk(p_ref.at[n], mu_ref, vec_ref, lhi_ref, llo_ref, o_ref.at[n], state_ref.at[n], prev_ref.at[n])


def _rwkv_chunk(p_ref, mu_ref, vec_ref, lhi_ref, llo_ref, o_ref, state_ref, prev_ref):
    L, GW, ST = RWKV_CHUNK, GROUP_W, RWKV_STACK
    p = p_ref[...]
    row = lax.broadcasted_iota(jnp.int32, p.shape, 0)
    prev = jnp.where(row == 0, prev_ref[...], pltpu.roll(p, 1, axis=0))
    prev_ref[...] = p[L - 1:L, :]
    ps = p + (prev - p) * mu_ref[...]
    r, k, v = ps[:, :GW], ps[:, GW:2 * GW], ps[:, 2 * GW:3 * GW]
    low = ps[:, 3 * GW:]
    lane_low = lax.broadcasted_iota(jnp.int32, low.shape, 1)
    low = jnp.where(lane_low < RWKV_W_RANK, jnp.tanh(low),
                    jnp.where(lane_low < RWKV_W_RANK + RWKV_A_RANK, low, jax.nn.sigmoid(low)))
    up = _dot3(low, lhi_ref[...], llo_ref[...])
    w0, a0, k_k, k_a = vec_ref[0:1, :], vec_ref[1:2, :], vec_ref[2:3, :], vec_ref[3:4, :]
    r_k, ln_g, ln_b = vec_ref[4:5, :], vec_ref[5:6, :], vec_ref[6:7, :]
    logw = -jnp.exp(-_softplus(-(w0 + up[:, :GW])) - 0.5)
    rate = jax.nn.sigmoid(a0 + up[:, GW:2 * GW])
    gate = up[:, 2 * GW:]

    ri = lax.broadcasted_iota(jnp.int32, (ST, ST), 0)
    ci = lax.broadcasted_iota(jnp.int32, (ST, ST), 1)
    same_head = (ri // L) == (ci // L)
    ones_bd = jnp.where(same_head, 1.0, 0.0).astype(BF16)

    kk = k * k_k
    kk = kk / jnp.maximum(jnp.sqrt(_dot_f32_by_exact(kk * kk, ones_bd)), 1e-12)
    k = k * (1.0 + (rate - 1.0) * k_a)

    ti = lax.broadcasted_iota(jnp.int32, (L, L), 0)
    tj = lax.broadcasted_iota(jnp.int32, (L, L), 1)
    lc = _dot_exact_by_f32(jnp.where(tj <= ti, 1.0, 0.0).astype(BF16), logw)
    lc_last = lc[L - 1:L, :]
    dec_in = jnp.exp(lc)
    dec_out = jnp.exp(-lc)
    a_t = -kk * jnp.exp(lc - logw)
    b_t = kk * rate * dec_out
    k_t = k * dec_out
    r_t = r * dec_in
    to_end = jnp.exp(lc_last)

    stack = lambda t: jnp.concatenate([t] * N_HEADS, axis=0)
    bd = lambda t: jnp.where(same_head, stack(t), 0.0).astype(BF16)
    a_bd, r_bd, v_bd = bd(a_t), bd(r_t), bd(v)
    m = _dot_nt(jnp.concatenate([a_bd, r_bd], axis=0),
                jnp.concatenate([stack(b_t), stack(k_t)], axis=0).astype(BF16))
    strict = same_head & ((ci % L) < (ri % L))
    incl = same_head & ((ci % L) <= (ri % L))
    m_ab = jnp.where(strict, m[:ST, :ST], 0.0)
    m_ak = jnp.where(strict, m[:ST, ST:], 0.0).astype(BF16)
    m_rb = jnp.where(incl, m[ST:, :ST], 0.0).astype(BF16)
    m_rk = jnp.where(incl, m[ST:, ST:], 0.0).astype(BF16)

    inv = jnp.where(ri == ci, 1.0, 0.0) + m_ab
    pw = m_ab
    n_doublings = RWKV_CHUNK.bit_length() - 2
    for s in range(n_doublings):
        pw_b = pw.astype(BF16)
        pw = _dot(pw_b, pw_b)
        inv = inv + _dot(inv.astype(BF16), pw.astype(BF16))

    t0 = state_ref[...]
    t0_b = t0.astype(BF16)
    u = _dot(inv.astype(BF16), (_dot(a_bd, t0_b) + _dot(m_ak, v_bd)).astype(BF16)).astype(BF16)
    y_bd = _dot(r_bd, t0_b) + _dot(m_rb, u) + _dot(m_rk, v_bd)
    y = sum(y_bd[h * L:(h + 1) * L, :] for h in range(N_HEADS))

    to_end_col = jnp.sum(jnp.where(ri == ci, jnp.broadcast_to(to_end, (ST, ST)), 0.0), axis=1, keepdims=True)
    state_ref[...] = (to_end_col * t0 + _dot_tn(bd(b_t * to_end), u) + _dot_tn(bd(k_t * to_end), v_bd))

    inv_d = 1.0 / HEAD_DIM
    mean = _dot_f32_by_exact(y, ones_bd) * inv_d
    yc = y - mean
    var = _dot_f32_by_exact(yc * yc, ones_bd) * inv_d
    yn = yc * lax.rsqrt(var + RWKV_GN_EPS) * ln_g + ln_b
    bonus = _dot_f32_by_exact(r * k * r_k, ones_bd) * v
    o_ref[...] = (yn + bonus) * gate


def _rwkv7_time_mix(p, mu, w0, w2, a0, a2, g2, k_k, k_a, r_k, ln_g, ln_b):
    B, S, _ = p.shape
    GW = GROUP_W
    assert S % RWKV_CHUNK == 0 and RWKV_STACK == GW and RWKV_CHUNK == HEAD_DIM and B % RWKV_BATCH == 0
    low_w = jnp.zeros((RWKV_LOW, 3 * GW), F32)
    low_w = low_w.at[:RWKV_W_RANK, :GW].set(w2)
    low_w = low_w.at[RWKV_W_RANK:RWKV_W_RANK + RWKV_A_RANK, GW:2 * GW].set(a2)
    low_w = low_w.at[RWKV_W_RANK + RWKV_A_RANK:, 2 * GW:].set(g2)
    l_hi, l_lo = _split_bf16(low_w)
    vecs = jnp.stack([w0, a0, k_k, k_a, r_k, ln_g, ln_b, jnp.zeros_like(w0)], 0)
    full = lambda a: pl.BlockSpec(a.shape, lambda b, c: (0,) * a.ndim)
    mu2 = mu[None]
    return pl.pallas_call(
        _rwkv_kernel,
        grid=(B // RWKV_BATCH, S // RWKV_CHUNK),
        in_specs=[pl.BlockSpec((RWKV_BATCH, RWKV_CHUNK, A_PAD), lambda b, c: (b, c, 0)),
                  full(mu2), full(vecs), full(l_hi), full(l_lo)],
        out_specs=pl.BlockSpec((RWKV_BATCH, RWKV_CHUNK, GW), lambda b, c: (b, c, 0)),
        out_shape=jax.ShapeDtypeStruct((B, S, GW), F32),
        scratch_shapes=[pltpu.VMEM((RWKV_BATCH, RWKV_STACK, GW), F32), pltpu.VMEM((RWKV_BATCH, 1, A_PAD), F32)],
        compiler_params=pltpu.CompilerParams(dimension_semantics=("parallel", "arbitrary"),
                                             vmem_limit_bytes=VMEM_LIMIT),
        name="rwkv7",
    )(p, mu2, vecs, l_hi, l_lo)


def _sb_kernel(q_ref, k_ref, v_ref, g_ref, o_ref, kbd_ref, vbd_ref):
    i = pl.program_id(1)
    T, GW, H = SB_BLOCK, GROUP_W, N_HEADS
    lane_h = lax.broadcasted_iota(jnp.int32, (T, GW), 1) // HEAD_DIM
    k_new, v_new = k_ref[0], v_ref[0]
    for h in range(H):
        kbd_ref[i, h * T:(h + 1) * T, :] = jnp.where(lane_h == h, k_new, 0.0).astype(BF16)
        vbd_ref[i, h * T:(h + 1) * T, :] = jnp.where(lane_h == h, v_new, 0.0).astype(BF16)

    q = q_ref[0].astype(BF16)
    si = lax.broadcasted_iota(jnp.int32, (T, 2 * T), 0)
    sj = lax.broadcasted_iota(jnp.int32, (T, 2 * T), 1)
    later_and_all = jnp.where((si > sj) | (sj >= T), 1.0, 0.0).astype(BF16)
    qrow = lax.broadcasted_iota(jnp.int32, (T, H * T), 0)
    kcol = lax.broadcasted_iota(jnp.int32, (T, H * T), 1) % T
    causal = kcol < qrow

    def key_block(j, state, diagonal):
        carry, acc = state
        z = _dot_nt(q, kbd_ref[j]) * HEAD_DIM ** -0.5
        soft = jnp.log(1.0 + jnp.exp(-jnp.abs(z)))
        log1m = -(jnp.maximum(z, 0.0) + soft)
        log_sig = jnp.minimum(z, 0.0) - soft
        log1m_in = (jnp.where(causal, log1m, 0.0) if diagonal else log1m).astype(BF16)
        sums = [_dot(log1m_in[:, h * T:(h + 1) * T], later_and_all) for h in range(H)]
        suffix = jnp.concatenate([s[:, :T] for s in sums], axis=1)
        total = jnp.concatenate([s[:, T:] for s in sums], axis=1)
        att = jnp.exp(log_sig + suffix + carry)
        if diagonal:
            att = jnp.where(causal, att, 0.0)
        return carry + total, acc + _dot(att.astype(BF16), vbd_ref[j])

    state = key_block(i, (jnp.zeros((T, H * T), F32), jnp.zeros((T, GW), F32)), True)
    odd = i % 2
    state = lax.fori_loop(0, odd, lambda it, st: key_block(i - 1, st, False), state)
    top = i - 1 - odd

    def two_blocks(it, st):
        return key_block(top - 2 * it - 1, key_block(top - 2 * it, st, False), False)

    _, y = lax.fori_loop(0, i // 2, two_blocks, state)

    hi_ = lax.broadcasted_iota(jnp.int32, (GW, GW), 0) // HEAD_DIM
    hj_ = lax.broadcasted_iota(jnp.int32, (GW, GW), 1) // HEAD_DIM
    ones_bd = jnp.where(hi_ == hj_, 1.0, 0.0).astype(BF16)
    ms = _dot_f32_by_exact(y * y, ones_bd) * (1.0 / HEAD_DIM)
    o_ref[0] = y * lax.rsqrt(ms + NORM_EPS) * g_ref[...]


def _stick_breaking_norm(q, k, v, g):
    B, S, GW = q.shape
    assert S % SB_BLOCK == 0
    blk = pl.BlockSpec((1, SB_BLOCK, GW), lambda b, i: (b, i, 0))
    stacked = pltpu.VMEM((S // SB_BLOCK, N_HEADS * SB_BLOCK, GW), BF16)
    return pl.pallas_call(
        _sb_kernel,
        grid=(B, S // SB_BLOCK),
        in_specs=[blk, blk, blk, pl.BlockSpec((1, GW), lambda b, i: (0, 0))],
        out_specs=blk,
        out_shape=jax.ShapeDtypeStruct((B, S, GW), F32),
        scratch_shapes=[stacked, stacked],
        compiler_params=pltpu.CompilerParams(dimension_semantics=("parallel", "arbitrary"),
                                             vmem_limit_bytes=VMEM_LIMIT),
        name="stick_breaking",
    )(q, k, v, g[None])


def _rms_norm(x, g):
    xf = x.astype(F32)
    y = xf * lax.rsqrt(jnp.mean(xf * xf, -1, keepdims=True) + NORM_EPS)
    return (y * g.astype(F32)).astype(x.dtype)


def _rope(x, pos):
    half = x.shape[-1] // 2
    inv = ROPE_THETA ** (-jnp.arange(half, dtype=F32) / half)
    ang = pos.astype(F32)[:, None] * inv[None, :]
    cos = jnp.cos(ang)[None, :, None, :]
    sin = jnp.sin(ang)[None, :, None, :]
    xf = x.astype(F32)
    x1, x2 = xf[..., :half], xf[..., half:]
    return jnp.concatenate([x1 * cos - x2 * sin, x2 * cos + x1 * sin], -1).astype(x.dtype)


def _token_shift(p, mu):
    prev = jnp.pad(p, ((0, 0), (1, 0), (0, 0)))[:, :-1]
    return p + (prev - p) * mu


def _causal_dwconv(x, w, b):
    ch = x.shape[-1]
    y = lax.conv_general_dilated(x, w[:, None, :].astype(x.dtype), window_strides=(1,),
                                 padding=((w.shape[0] - 1, 0),),
                                 dimension_numbers=('NWC', 'WIO', 'NWC'), feature_group_count=ch)
    return y + b


def _mlstm_chunkwise(q, k, v, log_i, log_f):
    B, S, H, d = q.shape
    L = ML_CHUNK
    nc = S // L

    def to_chunks(t):
        t = t.reshape((B, nc, L, H) + t.shape[3:])
        return jnp.moveaxis(jnp.moveaxis(t, 3, 2), 1, 0)

    causal = jnp.tril(jnp.ones((L, L), bool))

    def chunk(carry, inp):
        c_st, n_st, m_st = carry
        qc, kc, vc, li, lf = inp
        b = jnp.cumsum(lf, -1)
        dmat = jnp.where(causal, b[..., :, None] - b[..., None, :] + li[..., None, :], -jnp.inf)
        g_inter = b + m_st[..., None]
        m_t = jnp.maximum(g_inter, jnp.max(dmat, -1))
        s_inter = jnp.exp(g_inter - m_t)
        sqk = jnp.einsum('bhtd,bhsd->bhts', qc, kc) * jnp.exp(dmat - m_t[..., None])
        num = s_inter[..., None] * jnp.einsum('bhvd,bhtd->bhtv', c_st, qc) + jnp.einsum('bhts,bhsv->bhtv', sqk, vc)
        den = s_inter * jnp.einsum('bhd,bhtd->bht', n_st, qc) + jnp.sum(sqk, -1)
        h = num / jnp.maximum(jnp.abs(den), jnp.exp(-m_t))[..., None]
        b_last = b[..., -1]
        dec = b_last[..., None] - b + li
        m_new = jnp.maximum(b_last + m_st, jnp.max(dec, -1))
        wk = jnp.exp(dec - m_new[..., None])
        s_old = jnp.exp(b_last + m_st - m_new)
        c_st = s_old[..., None, None] * c_st + jnp.einsum('bhs,bhsv,bhsd->bhvd', wk, vc, kc)
        n_st = s_old[..., None] * n_st + jnp.einsum('bhs,bhsd->bhd', wk, kc)
        return (c_st, n_st, m_new), h

    init = (jnp.zeros((B, H, d, d), F32), jnp.zeros((B, H, d), F32), jnp.zeros((B, H), F32))
    _, hs = lax.scan(chunk, init, tuple(to_chunks(t) for t in (q, k, v, log_i, log_f)))
    hs = jnp.moveaxis(jnp.moveaxis(hs, 0, 1), 2, 3)
    return hs.reshape(B, S, H, d)


def _mlstm_mix(p, conv_w, conv_b, ig_b, fg_b, norm_g):
    B, S, _ = p.shape
    H, d = N_HEADS, HEAD_DIM
    qk = jax.nn.silu(_causal_dwconv(p[..., :2 * GROUP_W], conv_w, conv_b))
    q, k = qk[..., :GROUP_W], qk[..., GROUP_W:]
    _, _, v, o, ig, fg = _split_cols(p, C_SIZES)
    cap = lambda t: GATE_CAP * jnp.tanh(t / GATE_CAP)
    log_i = cap((ig + ig_b).astype(F32))
    log_f = jax.nn.log_sigmoid(cap((fg + fg_b).astype(F32)))
    heads = lambda t: t.astype(F32).reshape(B, S, H, d)
    h = _mlstm_chunkwise(heads(q), heads(k) * d ** -0.5, heads(v), log_i, log_f)
    h = _rms_norm(h, norm_g.reshape(H, d)).reshape(B, S, GROUP_W)
    return (jax.nn.sigmoid(o.astype(F32)) * h).astype(p.dtype)


def kernel(x, c, ada_w, ada_b, norm1_g, norm2_g, w_in, rk_mu, rk_w0, rk_w2, rk_a0, rk_a2, rk_g2, rk_kk, rk_ka, rk_rk, rk_ln_g, rk_ln_b, sb_norm_g, ml_conv_w, ml_conv_b, ml_ig_b, ml_fg_b, ml_norm_g, ds_qn_g, ds_kn_g, ds_out_g, w_out, moe_wg, moe_bg, moe_we, moe_be, moe_w1, moe_w3, moe_w2):
    B, S, D = x.shape
    H, d = N_HEADS, HEAD_DIM
    depth = ada_w.shape[0]
    pos = jnp.arange(S)
    c_act = jax.nn.silu(c)
    for l in range(depth):
        mod = (c_act @ ada_w[l] + ada_b[l])[:, None, :]
        sh1, sc1, gt1, sh2, sc2, gt2 = jnp.split(mod, 6, axis=-1)

        pA, pB, pC, pD = _in_proj(x, sc1, sh1, norm1_g[l][None], _pad_w_in(w_in[l]))
        pC = pC[..., :C_COLS]
        pD = pD[..., :D_COLS]

        yA = _rwkv7_time_mix(pA, rk_mu[l], rk_w0[l], rk_w2[l], rk_a0[l], rk_a2[l], rk_g2[l],
                             rk_kk[l], rk_ka[l], rk_rk[l], rk_ln_g[l], rk_ln_b[l])

        qb, kb, vb = _split_cols(pB, B_SIZES)
        yB = _stick_breaking_norm(qb, kb, vb, sb_norm_g[l])

        yC = _mlstm_mix(pC, ml_conv_w[l], ml_conv_b[l], ml_ig_b[l], ml_fg_b[l], ml_norm_g[l])

        qd, kd, vd, qi, ki, wi = _split_cols(pD, D_SIZES)
        qd = _rope(_rms_norm(qd.reshape(B, S, H, d), ds_qn_g[l]), pos).reshape(B, S, GROUP_W)
        kd = _rope(_rms_norm(kd[:, :, None, :], ds_kn_g[l]), pos)[:, :, 0]
        qi = _rope(qi.reshape(B, S, IDX_HEADS, IDX_DIM), pos).reshape(B, S, IDX_HEADS * IDX_DIM)
        ki = _rope(ki[:, :, None, :], pos)[:, :, 0]
        wi = wi * (IDX_HEADS ** -0.5 * IDX_DIM ** -0.5)
        yD = _dsa_attn_norm(qd, kd, vd, qi, ki, wi, ds_out_g[l])

        router = jnp.pad(jnp.concatenate([moe_wg[l], moe_we[l]], 1),
                         ((0, 0), (0, ROUTER_PAD - N_GROUPS - N_EXPERTS)))
        r_hi, r_lo = _split_bf16(router)
        x1, h2, logits = _out_proj((yA, yB, yC, yD), x, gt1, sc2, sh2, norm2_g[l][None],
                                   w_out[l].astype(BF16), r_hi, r_lo)

        moe = _hier_moe(h2.reshape(B * S, D), logits.reshape(B * S, ROUTER_PAD), moe_bg[l], moe_be[l],
                        moe_w1[l].astype(BF16), moe_w3[l].astype(BF16), moe_w2[l].astype(BF16))
        x = x1 + gt2 * moe.reshape(B, S, D)
    return x
```

```python
import functools

import jax
import jax.numpy as jnp
import numpy as np
from jax import lax
from jax.experimental import pallas as pl
from jax.experimental.pallas import tpu as pltpu

F32 = jnp.float32
BF16 = jnp.bfloat16

D_MODEL = 1024
N_MIXERS = 4
GROUP_W = D_MODEL // N_MIXERS
HEAD_DIM = 64
N_HEADS = GROUP_W // HEAD_DIM
NORM_EPS = 1e-6
RWKV_W_RANK = 32
RWKV_A_RANK = 32
RWKV_G_RANK = 64
RWKV_GN_EPS = 64e-5
SB_BLOCK = 128
ML_CHUNK = 64
ML_CONV = 4
GATE_CAP = 15.0
DSA_BLOCK = 128
IDX_HEADS = 4
IDX_DIM = 32
TOPK_MAX = 256
ROPE_THETA = 10000.0
N_GROUPS = 4
EXP_PER_GROUP = 8
N_EXPERTS = N_GROUPS * EXP_PER_GROUP
EXPERT_FF = D_MODEL // 2
TOP_IN_GROUP = 2

A_SIZES = (GROUP_W, GROUP_W, GROUP_W, RWKV_W_RANK, RWKV_A_RANK, RWKV_G_RANK)
B_SIZES = (GROUP_W, GROUP_W, GROUP_W)
C_SIZES = (GROUP_W, GROUP_W, GROUP_W, GROUP_W, N_HEADS, N_HEADS)
D_SIZES = (GROUP_W, HEAD_DIM, HEAD_DIM, IDX_HEADS * IDX_DIM, IDX_DIM, IDX_HEADS)
A_COLS = sum(A_SIZES)
B_COLS = sum(B_SIZES)
C_COLS = sum(C_SIZES)
D_COLS = sum(D_SIZES)

LANE = 128
A_PAD = 896
B_PAD = 768
C_PAD = 1152
D_PAD = 640
P_PAD = A_PAD + B_PAD + C_PAD + D_PAD
ROUTER_PAD = LANE

IN_ROWS = 256
OUT_ROWS = 512
MOE_ROWS = 256
VMEM_LIMIT = 48 * 1024 * 1024


def _split_cols(t, sizes):
    return jnp.split(t, [int(i) for i in np.cumsum(sizes)[:-1]], axis=-1)


def _in_proj_kernel(x_ref, sc_ref, sh_ref, g_ref, w_ref, oa_ref, ob_ref, oc_ref, od_ref):
    x = x_ref[0]
    y = x * lax.rsqrt(jnp.mean(x * x, -1, keepdims=True) + NORM_EPS) * g_ref[...]
    h = y * (1.0 + sc_ref[0]) + sh_ref[0]
    p = jnp.dot(h.astype(BF16), w_ref[...], preferred_element_type=F32)
    oa_ref[0] = p[:, :A_PAD]
    ob_ref[0] = p[:, A_PAD:A_PAD + B_PAD]
    oc_ref[0] = p[:, A_PAD + B_PAD:A_PAD + B_PAD + C_PAD]
    od_ref[0] = p[:, A_PAD + B_PAD + C_PAD:]


def _in_proj(x, sc, sh, g, w_pad):
    B, S, D = x.shape
    row = lambda w: pl.BlockSpec((1, IN_ROWS, w), lambda b, i: (b, i, 0))
    vec = pl.BlockSpec((1, 1, D), lambda b, i: (b, 0, 0))
    return pl.pallas_call(
        _in_proj_kernel,
        grid=(B, S // IN_ROWS),
        in_specs=[row(D), vec, vec, pl.BlockSpec((1, D), lambda b, i: (0, 0)),
                  pl.BlockSpec((D, P_PAD), lambda b, i: (0, 0))],
        out_specs=[row(A_PAD), row(B_PAD), row(C_PAD), row(D_PAD)],
        out_shape=[jax.ShapeDtypeStruct((B, S, w), F32) for w in (A_PAD, B_PAD, C_PAD, D_PAD)],
        compiler_params=pltpu.CompilerParams(dimension_semantics=("parallel", "parallel"),
                                             vmem_limit_bytes=VMEM_LIMIT),
        name="in_proj",
    )(x, sc, sh, g, w_pad)


def _pad_w_in(w):
    wa, wb, wc, wd = _split_cols(w, (A_COLS, B_COLS, C_COLS, D_COLS))
    padc = lambda t, n: jnp.pad(t, ((0, 0), (0, n - t.shape[1])))
    return jnp.concatenate([padc(wa, A_PAD), padc(wb, B_PAD), padc(wc, C_PAD), padc(wd, D_PAD)], 1).astype(BF16)


def _split_bf16(t):
    hi = t.astype(BF16)
    lo = (t - hi.astype(F32)).astype(BF16)
    return hi, lo


def _out_proj_kernel(ya_ref, yb_ref, yc_ref, yd_ref, x_ref, gt_ref, sc_ref, sh_ref, g_ref, w_ref,
                     rhi_ref, rlo_ref, x1_ref, h2_ref, lg_ref):
    acc = jnp.zeros(x_ref.shape[1:], F32)
    for n, y_ref in enumerate((ya_ref, yb_ref, yc_ref, yd_ref)):
        acc += jnp.dot(y_ref[0].astype(BF16), w_ref[n * GROUP_W:(n + 1) * GROUP_W, :],
                       preferred_element_type=F32)
    x1 = x_ref[0] + gt_ref[0] * acc
    x1_ref[0] = x1
    y = x1 * lax.rsqrt(jnp.mean(x1 * x1, -1, keepdims=True) + NORM_EPS) * g_ref[...]
    h = y * (1.0 + sc_ref[0]) + sh_ref[0]
    hi, lo = _split_bf16(h)
    h2_ref[0] = hi
    lg_ref[0] = (jnp.dot(hi, rhi_ref[...], preferred_element_type=F32)
                 + jnp.dot(lo, rhi_ref[...], preferred_element_type=F32)
                 + jnp.dot(hi, rlo_ref[...], preferred_element_type=F32))


def _out_proj(ys, x, gt, sc, sh, g, w_out, r_hi, r_lo):
    B, S, D = x.shape
    row = lambda w: pl.BlockSpec((1, OUT_ROWS, w), lambda b, i: (b, i, 0))
    vec = pl.BlockSpec((1, 1, D), lambda b, i: (b, 0, 0))
    full = lambda a: pl.BlockSpec(a.shape, lambda b, i: (0,) * a.ndim)
    return pl.pallas_call(
        _out_proj_kernel,
        grid=(B, S // OUT_ROWS),
        in_specs=[row(GROUP_W)] * 4 + [row(D), vec, vec, vec, full(g), full(w_out), full(r_hi), full(r_lo)],
        out_specs=[row(D), row(D), row(ROUTER_PAD)],
        out_shape=[jax.ShapeDtypeStruct((B, S, D), F32), jax.ShapeDtypeStruct((B, S, D), BF16),
                   jax.ShapeDtypeStruct((B, S, ROUTER_PAD), F32)],
        compiler_params=pltpu.CompilerParams(dimension_semantics=("parallel", "parallel"),
                                             vmem_limit_bytes=VMEM_LIMIT),
        name="out_proj",
    )(*ys, x, gt, sc, sh, g, w_out, r_hi, r_lo)


def _moe_ffn_kernel(blk_e_ref, x_ref, wt_ref, w1_ref, w3_ref, w2_ref, o_ref):
    del blk_e_ref
    xb = x_ref[...]
    a = jnp.dot(xb, w1_ref[0], preferred_element_type=F32)
    b = jnp.dot(xb, w3_ref[0], preferred_element_type=F32)
    hmid = (a * jax.nn.sigmoid(a) * b).astype(BF16)
    y = jnp.dot(hmid, w2_ref[0], preferred_element_type=F32)
    o_ref[...] = y * wt_ref[...]


def _moe_ffn(blk_e, xs, wt, w1, w3, w2):
    n_slots, D = xs.shape
    n_blocks = n_slots // MOE_ROWS
    FF = w1.shape[-1]
    return pl.pallas_call(
        _moe_ffn_kernel,
        grid_spec=pltpu.PrefetchScalarGridSpec(
            num_scalar_prefetch=1,
            grid=(n_blocks,),
            in_specs=[pl.BlockSpec((MOE_ROWS, D), lambda i, e: (i, 0)),
                      pl.BlockSpec((MOE_ROWS, 1), lambda i, e: (i, 0)),
                      pl.BlockSpec((1, D, FF), lambda i, e: (e[i], 0, 0)),
                      pl.BlockSpec((1, D, FF), lambda i, e: (e[i], 0, 0)),
                      pl.BlockSpec((1, FF, D), lambda i, e: (e[i], 0, 0))],
            out_specs=pl.BlockSpec((MOE_ROWS, D), lambda i, e: (i, 0)),
        ),
        out_shape=jax.ShapeDtypeStruct((n_slots, D), F32),
        compiler_params=pltpu.CompilerParams(dimension_semantics=("arbitrary",),
                                             vmem_limit_bytes=VMEM_LIMIT),
        name="moe_ffn",
    )(blk_e, xs, wt, w1, w3, w2)


def _hier_moe(h2, logits, bg, be, w1, w3, w2):
    N, D = h2.shape
    g_prob = jax.nn.softmax(logits[:, :N_GROUPS] + bg, -1)
    g_p, g_idx = lax.top_k(g_prob, 1)
    e_logits = (logits[:, N_GROUPS:N_GROUPS + N_EXPERTS] + be).reshape(N, N_GROUPS, EXP_PER_GROUP)
    e_logits = jnp.take_along_axis(e_logits, g_idx[:, :, None], axis=1)[:, 0]
    e_p, e_idx = lax.top_k(jax.nn.softmax(e_logits, -1), TOP_IN_GROUP)
    gate = g_p * e_p / jnp.sum(e_p, -1, keepdims=True)
    expert = g_idx * EXP_PER_GROUP + e_idx
    n_asg = N * TOP_IN_GROUP
    flat_e = expert.reshape(n_asg)
    order = jnp.argsort(flat_e)
    se = flat_e[order]
    bounds = jnp.searchsorted(se, jnp.arange(N_EXPERTS + 1, dtype=se.dtype))
    start, counts = bounds[:-1], bounds[1:] - bounds[:-1]
    pad_counts = (counts + MOE_ROWS - 1) // MOE_ROWS * MOE_ROWS
    pad_end = jnp.cumsum(pad_counts)
    pad_start = pad_end - pad_counts
    slot = (pad_start[se] + jnp.arange(n_asg) - start[se]).astype(jnp.int32)
    n_blocks = n_asg // MOE_ROWS + N_EXPERTS
    n_slots = n_blocks * MOE_ROWS
    blk_start = jnp.arange(n_blocks) * MOE_ROWS
    blk_e = jnp.minimum(jnp.sum(pad_end[None, :] <= blk_start[:, None], 1), N_EXPERTS - 1).astype(jnp.int32)
    slot_e = jnp.repeat(blk_e, MOE_ROWS)
    slot_pos = jnp.arange(n_slots) - pad_start[slot_e]
    slot_real = (slot_pos < counts[slot_e]) & (jnp.arange(n_slots) < pad_end[N_EXPERTS - 1])
    slot_src = jnp.where(slot_real, start[slot_e] + slot_pos, 0)
    slot_tok = jnp.where(slot_real, (order // TOP_IN_GROUP)[slot_src], 0).astype(jnp.int32)
    slot_w = jnp.where(slot_real, gate.reshape(n_asg)[order][slot_src], 0.0)
    yb = _moe_ffn(blk_e, h2[slot_tok], slot_w[:, None], w1, w3, w2)
    asg_slot = slot[jnp.argsort(order)].reshape(N, TOP_IN_GROUP)
    return yb[asg_slot[:, 0]] + yb[asg_slot[:, 1]]


INT_MIN = -2 ** 31
DSA_KEY_STEP = 512
DSA_CHAINS = 4


def _float_order_key(x):
    bits = pltpu.bitcast(x, jnp.int32)
    bits = jnp.where(x == 0.0, 0, bits)
    return bits ^ ((bits >> 31) & 0x7FFFFFFF)


def _row_count(mask):
    return jnp.sum(jnp.where(mask, 1.0, 0.0), axis=1, keepdims=True)


def _dsa_block(qd_ref, kdt_ref, vdw_ref, qi_ref, kit_ref, wi_ref, g_ref, o_ref, *, kl, n_sel):
    q0 = pl.program_id(1) * DSA_BLOCK
    kit = kit_ref[0, :, :kl]
    ki4 = jnp.concatenate([kit] * IDX_HEADS, axis=0)
    k_hi, k_lo = _split_bf16(ki4)
    qi = qi_ref[0]
    wi = wi_ref[0]
    lane_i = lax.broadcasted_iota(jnp.int32, qi.shape, 1)
    score = jnp.zeros((DSA_BLOCK, kl), F32)
    for h in range(IDX_HEADS):
        q_hi, q_lo = _split_bf16(jnp.where(lane_i // IDX_DIM == h, qi, 0.0))
        sc = (jnp.dot(q_hi, k_hi, preferred_element_type=F32) + jnp.dot(q_lo, k_hi, preferred_element_type=F32)
              + jnp.dot(q_hi, k_lo, preferred_element_type=F32))
        score = score + wi[:, h:h + 1] * jnp.maximum(sc, 0.0)
    kidx = lax.broadcasted_iota(jnp.int32, (DSA_BLOCK, kl), 1)
    qpos = q0 + lax.broadcasted_iota(jnp.int32, (DSA_BLOCK, kl), 0)
    adm = kidx <= qpos
    key = _float_order_key(jnp.where(adm, score, -jnp.inf))

    rows = DSA_BLOCK // DSA_CHAINS
    key_groups = [key[c * rows:(c + 1) * rows] for c in range(DSA_CHAINS)]

    def value_bit(it, taus):
        bit = jnp.left_shift(jnp.int32(1), 31 - it)
        out = []
        for key_c, tau_c in zip(key_groups, taus):
            cand = tau_c | bit
            cnt = _row_count(key_c >= (cand ^ INT_MIN))
            out.append(jnp.where(cnt >= n_sel, cand, tau_c))
        return tuple(out)

    taus = lax.fori_loop(0, 32, value_bit, tuple(jnp.zeros((rows, 1), jnp.int32) for _ in range(DSA_CHAINS)))
    tau = jnp.concatenate(taus, axis=0) ^ INT_MIN
    gt = key > tau
    eq = (key == tau) & adm
    need = n_sel - _row_count(gt)
    n_eq = _row_count(eq)

    def index_bits():
        def index_bit(it, bound):
            cand = bound | jnp.left_shift(jnp.int32(1), 11 - it)
            cnt = _row_count(eq & (kidx < cand))
            return jnp.where(cnt <= need, cand, bound)
        return lax.fori_loop(0, 12, index_bit, jnp.zeros((DSA_BLOCK, 1), jnp.int32))

    bound = lax.cond(jnp.max(n_eq - need) > 0.0, index_bits,
                     lambda: jnp.full((DSA_BLOCK, 1), kl, jnp.int32))
    sel = gt | (eq & (kidx < bound))

    kdt = kdt_ref[0, :, :kl]
    kd4 = jnp.concatenate([kdt] * N_HEADS, axis=0).astype(BF16)
    vdw = vdw_ref[0, :kl, :].astype(BF16)
    qd = qd_ref[0]
    lane_h = lax.broadcasted_iota(jnp.int32, qd.shape, 1) // HEAD_DIM
    out = jnp.zeros(qd.shape, F32)
    for h in range(N_HEADS):
        qm = jnp.where(lane_h == h, qd, 0.0).astype(BF16)
        lg = jnp.dot(qm, kd4, preferred_element_type=F32) * HEAD_DIM ** -0.5
        lg = jnp.where(sel, lg, -jnp.inf)
        p = jnp.exp(lg - jnp.max(lg, axis=1, keepdims=True))
        r = jnp.dot(p.astype(BF16), vdw, preferred_element_type=F32) / jnp.sum(p, axis=1, keepdims=True)
        r = r * lax.rsqrt(jnp.mean(r * r, axis=1, keepdims=True) + NORM_EPS)
        out = out + jnp.where(lane_h == h, r, 0.0)
    o_ref[0] = out * g_ref[...]


def _dsa_kernel(qd_ref, kdt_ref, vdw_ref, qi_ref, kit_ref, wi_ref, g_ref, o_ref, *, kls, n_sel):
    blocks_per_step = DSA_KEY_STEP // DSA_BLOCK
    for j, kl in enumerate(kls):
        @pl.when(pl.program_id(1) // blocks_per_step == j)
        def _():
            _dsa_block(qd_ref, kdt_ref, vdw_ref, qi_ref, kit_ref, wi_ref, g_ref, o_ref, kl=kl, n_sel=n_sel)


DSA_PREP_ROWS = 256
D_Q, D_KV, D_QI, D_KW = 0, GROUP_W, GROUP_W + LANE, GROUP_W + 2 * LANE


def _swap_halves(x, half):
    n = x.shape[1]
    lane = lax.broadcasted_iota(jnp.int32, x.shape, 1)
    return jnp.where(lane % (2 * half) < half, pltpu.roll(x, n - half, axis=1), pltpu.roll(x, half, axis=1))


def _dsa_prep_kernel(p_ref, cq_ref, sq_ref, ci_ref, si_ref, gq_ref, gk_ref,
                     qd_ref, kdt_ref, vdw_ref, qi_ref, kit_ref, wi_ref):
    GW = GROUP_W
    x = p_ref[0]
    hi_ = lax.broadcasted_iota(jnp.int32, (GW, GW), 0) // HEAD_DIM
    hj_ = lax.broadcasted_iota(jnp.int32, (GW, GW), 1) // HEAD_DIM
    ones_bd = jnp.where(hi_ == hj_, 1.0, 0.0).astype(BF16)
    q = x[:, D_Q:D_Q + GW]
    q = q * lax.rsqrt(_dot_f32_by_exact(q * q, ones_bd) * (1.0 / HEAD_DIM) + NORM_EPS) * gq_ref[...]
    qd_ref[0] = q * cq_ref[...] + _swap_halves(q, HEAD_DIM // 2) * sq_ref[...]
    kv = x[:, D_KV:D_KV + LANE]
    lane = lax.broadcasted_iota(jnp.int32, kv.shape, 1)
    is_k = lane < HEAD_DIM
    ms = jnp.sum(jnp.where(is_k, kv * kv, 0.0), axis=1, keepdims=True) * (1.0 / HEAD_DIM)
    kn = kv * lax.rsqrt(ms + NORM_EPS) * gk_ref[...]
    kr = kn * cq_ref[:, :LANE] + _swap_halves(kn, HEAD_DIM // 2) * sq_ref[:, :LANE]
    kdt_ref[0] = jnp.transpose(jnp.where(is_k, kr, 0.0))[:HEAD_DIM, :]
    v2 = jnp.where(is_k, pltpu.roll(kv, HEAD_DIM, axis=1), kv)
    vdw_ref[0] = jnp.concatenate([v2] * (GW // LANE), axis=1).astype(BF16)
    qi = x[:, D_QI:D_QI + LANE]
    qi_ref[0] = qi * ci_ref[...] + _swap_halves(qi, IDX_DIM // 2) * si_ref[...]
    kw = x[:, D_KW:D_KW + LANE]
    kir = kw * ci_ref[...] + _swap_halves(kw, IDX_DIM // 2) * si_ref[...]
    kit_ref[0] = jnp.transpose(kir)[:IDX_DIM, :]
    wi_ref[0] = pltpu.roll(kw, LANE - IDX_DIM, axis=1) * (IDX_HEADS ** -0.5 * IDX_DIM ** -0.5)


def _rope_tables(S, dim, width):
    half = dim // 2
    inv = ROPE_THETA ** (-jnp.arange(half, dtype=F32) / half)
    ang = jnp.arange(S, dtype=F32)[:, None] * inv[None, :]
    cos = jnp.tile(jnp.cos(ang), (1, width // half))
    sin = jnp.tile(jnp.concatenate([-jnp.sin(ang), jnp.sin(ang)], axis=1), (1, width // dim))
    return cos, sin


def _dsa_prep(p, qn_g, kn_g):
    B, S, _ = p.shape
    GW, R = GROUP_W, DSA_PREP_ROWS
    cq, sq = _rope_tables(S, HEAD_DIM, GW)
    ci, si = _rope_tables(S, IDX_DIM, LANE)
    gq = jnp.tile(qn_g, N_HEADS)[None]
    gk = jnp.pad(kn_g, (0, LANE - HEAD_DIM))[None]
    rows = lambda w: pl.BlockSpec((1, R, w), lambda b, i: (b, i, 0))
    tab = lambda w: pl.BlockSpec((R, w), lambda b, i: (i, 0))
    cols = lambda r: pl.BlockSpec((1, r, R), lambda b, i: (b, 0, i))
    vec = lambda w: pl.BlockSpec((1, w), lambda b, i: (0, 0))
    return pl.pallas_call(
        _dsa_prep_kernel,
        grid=(B, S // R),
        in_specs=[rows(D_PAD), tab(GW), tab(GW), tab(LANE), tab(LANE), vec(GW), vec(LANE)],
        out_specs=[rows(GW), cols(HEAD_DIM), rows(GW), rows(LANE), cols(IDX_DIM), rows(LANE)],
        out_shape=[jax.ShapeDtypeStruct((B, S, GW), F32), jax.ShapeDtypeStruct((B, HEAD_DIM, S), F32),
                   jax.ShapeDtypeStruct((B, S, GW), BF16), jax.ShapeDtypeStruct((B, S, LANE), F32),
                   jax.ShapeDtypeStruct((B, IDX_DIM, S), F32), jax.ShapeDtypeStruct((B, S, LANE), F32)],
        compiler_params=pltpu.CompilerParams(dimension_semantics=("parallel", "parallel"),
                                             vmem_limit_bytes=VMEM_LIMIT),
        name="dsa_prep",
    )(p, cq, sq, ci, si, gq, gk)


def _dsa_attn_norm(qd, kdt, vdw, qi, kit, wi, g):
    B, S, _ = qd.shape
    n_sel = min(TOPK_MAX, S // 4)
    assert S % DSA_KEY_STEP == 0 and n_sel <= DSA_KEY_STEP
    kls = tuple(range(DSA_KEY_STEP, S + 1, DSA_KEY_STEP))
    blk = lambda w: pl.BlockSpec((1, DSA_BLOCK, w), lambda b, i: (b, i, 0))
    per_b = lambda r, c: pl.BlockSpec((1, r, c), lambda b, i: (b, 0, 0))
    return pl.pallas_call(
        functools.partial(_dsa_kernel, kls=kls, n_sel=n_sel),
        grid=(B, S // DSA_BLOCK),
        in_specs=[blk(GROUP_W), per_b(HEAD_DIM, S), per_b(S, GROUP_W), blk(IDX_HEADS * IDX_DIM),
                  per_b(IDX_DIM, S), blk(LANE), pl.BlockSpec((1, GROUP_W), lambda b, i: (0, 0))],
        out_specs=blk(GROUP_W),
        out_shape=jax.ShapeDtypeStruct((B, S, GROUP_W), F32),
        compiler_params=pltpu.CompilerParams(dimension_semantics=("parallel", "parallel"),
                                             vmem_limit_bytes=VMEM_LIMIT),
        name="dsa_attn",
    )(qd, kdt, vdw, qi, kit, wi, g[None])


RWKV_CHUNK = 64
RWKV_LOW = RWKV_W_RANK + RWKV_A_RANK + RWKV_G_RANK
RWKV_STACK = N_HEADS * RWKV_CHUNK
RWKV_BATCH = 2


def _dot(a, b):
    return jnp.dot(a, b, preferred_element_type=F32)


def _dot_nt(a, b):
    return lax.dot_general(a, b, (((1,), (1,)), ((), ())), preferred_element_type=F32)


def _dot_tn(a, b):
    return lax.dot_general(a, b, (((0,), (0,)), ((), ())), preferred_element_type=F32)


def _split3_bf16(t):
    p1 = t.astype(BF16)
    r1 = t - p1.astype(F32)
    p2 = r1.astype(BF16)
    p3 = (r1 - p2.astype(F32)).astype(BF16)
    return p1, p2, p3


def _dot_f32_by_exact(a, b_exact):
    return sum(_dot(p, b_exact) for p in _split3_bf16(a))


def _dot_exact_by_f32(a_exact, b):
    return sum(_dot(a_exact, p) for p in _split3_bf16(b))


def _dot3(a, b_hi, b_lo):
    a_hi, a_lo = _split_bf16(a)
    return _dot(a_hi, b_hi) + _dot(a_lo, b_hi) + _dot(a_hi, b_lo)


def _softplus(z):
    return jnp.maximum(z, 0.0) + jnp.log(1.0 + jnp.exp(-jnp.abs(z)))


def _rwkv_kernel(p_ref, mu_ref, vec_ref, lhi_ref, llo_ref, o_ref, state_ref, prev_ref):
    @pl.when(pl.program_id(1) == 0)
    def _():
        state_ref[...] = jnp.zeros_like(state_ref)
        prev_ref[...] = jnp.zeros_like(prev_ref)

    for n in range(RWKV_BATCH):
        _rwkv_chunk(p_ref.at[n], mu_ref, vec_ref, lhi_ref, llo_ref, o_ref.at[n], state_ref.at[n], prev_ref.at[n])


def _rwkv_chunk(p_ref, mu_ref, vec_ref, lhi_ref, llo_ref, o_ref, state_ref, prev_ref):
    L, GW, ST = RWKV_CHUNK, GROUP_W, RWKV_STACK
    p = p_ref[...]
    row = lax.broadcasted_iota(jnp.int32, p.shape, 0)
    prev = jnp.where(row == 0, prev_ref[...], pltpu.roll(p, 1, axis=0))
    prev_ref[...] = p[L - 1:L, :]
    ps = p + (prev - p) * mu_ref[...]
    r, k, v = ps[:, :GW], ps[:, GW:2 * GW], ps[:, 2 * GW:3 * GW]
    low = ps[:, 3 * GW:]
    lane_low = lax.broadcasted_iota(jnp.int32, low.shape, 1)
    low = jnp.where(lane_low < RWKV_W_RANK, jnp.tanh(low),
                    jnp.where(lane_low < RWKV_W_RANK + RWKV_A_RANK, low, jax.nn.sigmoid(low)))
    up = _dot3(low, lhi_ref[...], llo_ref[...])
    w0, a0, k_k, k_a = vec_ref[0:1, :], vec_ref[1:2, :], vec_ref[2:3, :], vec_ref[3:4, :]
    r_k, ln_g, ln_b = vec_ref[4:5, :], vec_ref[5:6, :], vec_ref[6:7, :]
    logw = -jnp.exp(-_softplus(-(w0 + up[:, :GW])) - 0.5)
    rate = jax.nn.sigmoid(a0 + up[:, GW:2 * GW])
    gate = up[:, 2 * GW:]

    ri = lax.broadcasted_iota(jnp.int32, (ST, ST), 0)
    ci = lax.broadcasted_iota(jnp.int32, (ST, ST), 1)
    same_head = (ri // L) == (ci // L)
    ones_bd = jnp.where(same_head, 1.0, 0.0).astype(BF16)

    kk = k * k_k
    kk = kk / jnp.maximum(jnp.sqrt(_dot_f32_by_exact(kk * kk, ones_bd)), 1e-12)
    k = k * (1.0 + (rate - 1.0) * k_a)

    ti = lax.broadcasted_iota(jnp.int32, (L, L), 0)
    tj = lax.broadcasted_iota(jnp.int32, (L, L), 1)
    lc = _dot_exact_by_f32(jnp.where(tj <= ti, 1.0, 0.0).astype(BF16), logw)
    lc_last = lc[L - 1:L, :]
    dec_in = jnp.exp(lc)
    dec_out = jnp.exp(-lc)
    a_t = -kk * jnp.exp(lc - logw)
    b_t = kk * rate * dec_out
    k_t = k * dec_out
    r_t = r * dec_in
    to_end = jnp.exp(lc_last)

    stack = lambda t: jnp.concatenate([t] * N_HEADS, axis=0)
    bd = lambda t: jnp.where(same_head, stack(t), 0.0).astype(BF16)
    a_bd, r_bd, v_bd = bd(a_t), bd(r_t), bd(v)
    m = _dot_nt(jnp.concatenate([a_bd, r_bd], axis=0),
                jnp.concatenate([stack(b_t), stack(k_t)], axis=0).astype(BF16))
    strict = same_head & ((ci % L) < (ri % L))
    incl = same_head & ((ci % L) <= (ri % L))
    m_ab = jnp.where(strict, m[:ST, :ST], 0.0)
    m_ak = jnp.where(strict, m[:ST, ST:], 0.0).astype(BF16)
    m_rb = jnp.where(incl, m[ST:, :ST], 0.0).astype(BF16)
    m_rk = jnp.where(incl, m[ST:, ST:], 0.0).astype(BF16)

    inv = jnp.where(ri == ci, 1.0, 0.0) + m_ab
    pw = m_ab
    n_doublings = RWKV_CHUNK.bit_length() - 2
    for s in range(n_doublings):
        pw_b = pw.astype(BF16)
        pw = _dot(pw_b, pw_b)
        inv = inv + _dot(inv.astype(BF16), pw.astype(BF16))

    t0 = state_ref[...]
    t0_b = t0.astype(BF16)
    u = _dot(inv.astype(BF16), (_dot(a_bd, t0_b) + _dot(m_ak, v_bd)).astype(BF16)).astype(BF16)
    y_bd = _dot(r_bd, t0_b) + _dot(m_rb, u) + _dot(m_rk, v_bd)
    y = sum(y_bd[h * L:(h + 1) * L, :] for h in range(N_HEADS))

    to_end_col = jnp.sum(jnp.where(ri == ci, jnp.broadcast_to(to_end, (ST, ST)), 0.0), axis=1, keepdims=True)
    state_ref[...] = (to_end_col * t0 + _dot_tn(bd(b_t * to_end), u) + _dot_tn(bd(k_t * to_end), v_bd))

    inv_d = 1.0 / HEAD_DIM
    mean = _dot_f32_by_exact(y, ones_bd) * inv_d
    yc = y - mean
    var = _dot_f32_by_exact(yc * yc, ones_bd) * inv_d
    yn = yc * lax.rsqrt(var + RWKV_GN_EPS) * ln_g + ln_b
    bonus = _dot_f32_by_exact(r * k * r_k, ones_bd) * v
    o_ref[...] = (yn + bonus) * gate


def _rwkv7_time_mix(p, mu, w0, w2, a0, a2, g2, k_k, k_a, r_k, ln_g, ln_b):
    B, S, _ = p.shape
    GW = GROUP_W
    assert S % RWKV_CHUNK == 0 and RWKV_STACK == GW and RWKV_CHUNK == HEAD_DIM and B % RWKV_BATCH == 0
    low_w = jnp.zeros((RWKV_LOW, 3 * GW), F32)
    low_w = low_w.at[:RWKV_W_RANK, :GW].set(w2)
    low_w = low_w.at[RWKV_W_RANK:RWKV_W_RANK + RWKV_A_RANK, GW:2 * GW].set(a2)
    low_w = low_w.at[RWKV_W_RANK + RWKV_A_RANK:, 2 * GW:].set(g2)
    l_hi, l_lo = _split_bf16(low_w)
    vecs = jnp.stack([w0, a0, k_k, k_a, r_k, ln_g, ln_b, jnp.zeros_like(w0)], 0)
    full = lambda a: pl.BlockSpec(a.shape, lambda b, c: (0,) * a.ndim)
    mu2 = mu[None]
    return pl.pallas_call(
        _rwkv_kernel,
        grid=(B // RWKV_BATCH, S // RWKV_CHUNK),
        in_specs=[pl.BlockSpec((RWKV_BATCH, RWKV_CHUNK, A_PAD), lambda b, c: (b, c, 0)),
                  full(mu2), full(vecs), full(l_hi), full(l_lo)],
        out_specs=pl.BlockSpec((RWKV_BATCH, RWKV_CHUNK, GW), lambda b, c: (b, c, 0)),
        out_shape=jax.ShapeDtypeStruct((B, S, GW), F32),
        scratch_shapes=[pltpu.VMEM((RWKV_BATCH, RWKV_STACK, GW), F32), pltpu.VMEM((RWKV_BATCH, 1, A_PAD), F32)],
        compiler_params=pltpu.CompilerParams(dimension_semantics=("parallel", "arbitrary"),
                                             vmem_limit_bytes=VMEM_LIMIT),
        name="rwkv7",
    )(p, mu2, vecs, l_hi, l_lo)


def _sb_kernel(q_ref, k_ref, v_ref, g_ref, o_ref, kbd_ref, vbd_ref):
    i = pl.program_id(1)
    T, GW, H = SB_BLOCK, GROUP_W, N_HEADS
    lane_h = lax.broadcasted_iota(jnp.int32, (T, GW), 1) // HEAD_DIM
    k_new, v_new = k_ref[0], v_ref[0]
    for h in range(H):
        kbd_ref[i, h * T:(h + 1) * T, :] = jnp.where(lane_h == h, k_new, 0.0).astype(BF16)
        vbd_ref[i, h * T:(h + 1) * T, :] = jnp.where(lane_h == h, v_new, 0.0).astype(BF16)

    q = q_ref[0].astype(BF16)
    si = lax.broadcasted_iota(jnp.int32, (T, 2 * T), 0)
    sj = lax.broadcasted_iota(jnp.int32, (T, 2 * T), 1)
    later_and_all = jnp.where((si > sj) | (sj >= T), 1.0, 0.0).astype(BF16)
    qrow = lax.broadcasted_iota(jnp.int32, (T, H * T), 0)
    kcol = lax.broadcasted_iota(jnp.int32, (T, H * T), 1) % T
    causal = kcol < qrow

    def key_block(j, state, diagonal):
        carry, acc = state
        z = _dot_nt(q, kbd_ref[j]) * HEAD_DIM ** -0.5
        soft = jnp.log(1.0 + jnp.exp(-jnp.abs(z)))
        log1m = -(jnp.maximum(z, 0.0) + soft)
        log_sig = jnp.minimum(z, 0.0) - soft
        log1m_in = (jnp.where(causal, log1m, 0.0) if diagonal else log1m).astype(BF16)
        sums = [_dot(log1m_in[:, h * T:(h + 1) * T], later_and_all) for h in range(H)]
        suffix = jnp.concatenate([s[:, :T] for s in sums], axis=1)
        total = jnp.concatenate([s[:, T:] for s in sums], axis=1)
        att = jnp.exp(log_sig + suffix + carry)
        if diagonal:
            att = jnp.where(causal, att, 0.0)
        return carry + total, acc + _dot(att.astype(BF16), vbd_ref[j])

    state = key_block(i, (jnp.zeros((T, H * T), F32), jnp.zeros((T, GW), F32)), True)
    odd = i % 2
    state = lax.fori_loop(0, odd, lambda it, st: key_block(i - 1, st, False), state)
    top = i - 1 - odd

    def two_blocks(it, st):
        return key_block(top - 2 * it - 1, key_block(top - 2 * it, st, False), False)

    _, y = lax.fori_loop(0, i // 2, two_blocks, state)

    hi_ = lax.broadcasted_iota(jnp.int32, (GW, GW), 0) // HEAD_DIM
    hj_ = lax.broadcasted_iota(jnp.int32, (GW, GW), 1) // HEAD_DIM
    ones_bd = jnp.where(hi_ == hj_, 1.0, 0.0).astype(BF16)
    ms = _dot_f32_by_exact(y * y, ones_bd) * (1.0 / HEAD_DIM)
    o_ref[0] = y * lax.rsqrt(ms + NORM_EPS) * g_ref[...]


def _stick_breaking_norm(p, g):
    B, S, _ = p.shape
    GW = GROUP_W
    assert S % SB_BLOCK == 0
    col = lambda n: pl.BlockSpec((1, SB_BLOCK, GW), lambda b, i: (b, i, n))
    blk = col(0)
    q = k = v = p
    stacked = pltpu.VMEM((S // SB_BLOCK, N_HEADS * SB_BLOCK, GW), BF16)
    return pl.pallas_call(
        _sb_kernel,
        grid=(B, S // SB_BLOCK),
        in_specs=[col(0), col(1), col(2), pl.BlockSpec((1, GW), lambda b, i: (0, 0))],
        out_specs=blk,
        out_shape=jax.ShapeDtypeStruct((B, S, GW), F32),
        scratch_shapes=[stacked, stacked],
        compiler_params=pltpu.CompilerParams(dimension_semantics=("parallel", "arbitrary"),
                                             vmem_limit_bytes=VMEM_LIMIT),
        name="stick_breaking",
    )(q, k, v, g[None])


ML_HALO = 8


def _dot_nt_exact_by_f32(a_exact, b):
    return sum(_dot_nt(a_exact, p) for p in _split3_bf16(b))


def _mlstm_kernel(p_ref, cw_ref, cb_ref, gb_ref, g_ref, o_ref, ext_ref, ct_ref, n_ref, m_ref):
    L, GW, H = ML_CHUNK, GROUP_W, N_HEADS
    ST = H * L

    @pl.when(pl.program_id(1) == 0)
    def _():
        ext_ref[...] = jnp.zeros_like(ext_ref)
        ct_ref[...] = jnp.zeros_like(ct_ref)
        n_ref[...] = jnp.zeros_like(n_ref)
        m_ref[...] = jnp.zeros_like(m_ref)

    x = p_ref[0]
    ext_ref[ML_HALO:, :] = x[:, :2 * GW]
    conv = cb_ref[...]
    for j in range(ML_CONV):
        conv = conv + cw_ref[j:j + 1, :] * ext_ref[pl.ds(ML_HALO - (ML_CONV - 1) + j, L), :]
    ext_ref[:ML_HALO, :] = x[L - ML_HALO:, :2 * GW]
    qk = conv * jax.nn.sigmoid(conv)
    q, k = qk[:, :GW], qk[:, GW:] * HEAD_DIM ** -0.5
    v, o = x[:, 2 * GW:3 * GW], x[:, 3 * GW:4 * GW]

    gates = x[:, 4 * GW:]
    gi = lax.broadcasted_iota(jnp.int32, (LANE, 2 * GW), 0)
    gj = lax.broadcasted_iota(jnp.int32, (LANE, 2 * GW), 1)
    expand = jnp.where(gi == (gj % GW) // HEAD_DIM + H * (gj // GW), 1.0, 0.0).astype(BF16)
    graw = _dot_f32_by_exact(gates, expand) + gb_ref[...]
    capped = GATE_CAP * jnp.tanh(graw * (1.0 / GATE_CAP))
    log_i = capped[:, :GW]
    cf = capped[:, GW:]
    log_f = jnp.minimum(cf, 0.0) - jnp.log(1.0 + jnp.exp(-jnp.abs(cf)))

    ti = lax.broadcasted_iota(jnp.int32, (L, L), 0)
    tj = lax.broadcasted_iota(jnp.int32, (L, L), 1)
    bf = _dot_exact_by_f32(jnp.where(tj <= ti, 1.0, 0.0).astype(BF16), log_f)
    b_last = bf[L - 1:L, :]
    m_row, n_row, ct = m_ref[...], n_ref[...], ct_ref[...]

    ri = lax.broadcasted_iota(jnp.int32, (ST, GW), 0)
    ci = lax.broadcasted_iota(jnp.int32, (ST, GW), 1)
    same_head = (ri // L) == (ci // HEAD_DIM)
    first_lane = ci == (ri // L) * HEAD_DIM
    stack = lambda t: jnp.concatenate([t] * H, axis=0)
    pick = lambda t: jnp.sum(jnp.where(first_lane, t, 0.0), axis=1, keepdims=True)

    qs = jnp.where(same_head, stack(q), 0.0)
    qs_b = qs.astype(BF16)
    v_b = v.astype(BF16)
    b_col = pick(stack(bf))
    g_col = pick(stack(bf + m_row))
    row_part = _dot_nt_exact_by_f32(jnp.where(first_lane, 1.0, 0.0).astype(BF16), log_i - bf)
    rt = lax.broadcasted_iota(jnp.int32, (ST, L), 0) % L
    rs = lax.broadcasted_iota(jnp.int32, (ST, L), 1)
    dmat = jnp.where(rs <= rt, b_col + row_part, -jnp.inf)
    m_t = jnp.maximum(g_col, jnp.max(dmat, axis=1, keepdims=True))
    s_inter = jnp.exp(g_col - m_t)
    sqk = _dot_nt(qs_b, k.astype(BF16)) * jnp.exp(dmat - m_t)
    num = s_inter * _dot(qs_b, ct.astype(BF16)) + jnp.where(same_head, _dot(sqk.astype(BF16), v_b), 0.0)
    den = s_inter * jnp.sum(qs * n_row, axis=1, keepdims=True) + jnp.sum(sqk, axis=1, keepdims=True)
    hst = num / jnp.maximum(jnp.abs(den), jnp.exp(-m_t))
    h = sum(hst[n * L:(n + 1) * L, :] for n in range(H))

    dec = b_last - bf + log_i
    m_new = jnp.maximum(b_last + m_row, jnp.max(dec, axis=0, keepdims=True))
    kw = k * jnp.exp(dec - m_new)
    s_old = jnp.exp(b_last + m_row - m_new)
    hi_ = lax.broadcasted_iota(jnp.int32, (GW, GW), 0) // HEAD_DIM
    hj_ = lax.broadcasted_iota(jnp.int32, (GW, GW), 1) // HEAD_DIM
    ct_ref[...] = s_old * ct + jnp.where(hi_ == hj_, _dot_tn(kw.astype(BF16), v_b), 0.0)
    n_ref[...] = s_old * n_row + jnp.sum(kw, axis=0, keepdims=True)
    m_ref[...] = m_new

    ones_bd = jnp.where(hi_ == hj_, 1.0, 0.0).astype(BF16)
    ms = _dot_f32_by_exact(h * h, ones_bd) * (1.0 / HEAD_DIM)
    o_ref[0] = jax.nn.sigmoid(o) * (h * lax.rsqrt(ms + NORM_EPS) * g_ref[...])


def _mlstm_mix(p, conv_w, conv_b, ig_b, fg_b, norm_g):
    B, S, _ = p.shape
    GW = GROUP_W
    assert S % ML_CHUNK == 0 and ML_CONV - 1 <= ML_HALO <= ML_CHUNK
    gate_b = jnp.concatenate([jnp.repeat(ig_b, HEAD_DIM), jnp.repeat(fg_b, HEAD_DIM)])[None]
    full = lambda a: pl.BlockSpec(a.shape, lambda b, c: (0,) * a.ndim)
    cb2, g2 = conv_b[None], norm_g[None]
    return pl.pallas_call(
        _mlstm_kernel,
        grid=(B, S // ML_CHUNK),
        in_specs=[pl.BlockSpec((1, ML_CHUNK, C_PAD), lambda b, c: (b, c, 0)),
                  full(conv_w), full(cb2), full(gate_b), full(g2)],
        out_specs=pl.BlockSpec((1, ML_CHUNK, GW), lambda b, c: (b, c, 0)),
        out_shape=jax.ShapeDtypeStruct((B, S, GW), F32),
        scratch_shapes=[pltpu.VMEM((ML_HALO + ML_CHUNK, 2 * GW), F32), pltpu.VMEM((GW, GW), F32),
                        pltpu.VMEM((1, GW), F32), pltpu.VMEM((1, GW), F32)],
        compiler_params=pltpu.CompilerParams(dimension_semantics=("parallel", "arbitrary"),
                                             vmem_limit_bytes=VMEM_LIMIT),
        name="mlstm",
    )(p, conv_w, cb2, gate_b, g2)


def _rms_norm(x, g):
    xf = x.astype(F32)
    y = xf * lax.rsqrt(jnp.mean(xf * xf, -1, keepdims=True) + NORM_EPS)
    return (y * g.astype(F32)).astype(x.dtype)


def _rope(x, pos):
    half = x.shape[-1] // 2
    inv = ROPE_THETA ** (-jnp.arange(half, dtype=F32) / half)
    ang = pos.astype(F32)[:, None] * inv[None, :]
    cos = jnp.cos(ang)[None, :, None, :]
    sin = jnp.sin(ang)[None, :, None, :]
    xf = x.astype(F32)
    x1, x2 = xf[..., :half], xf[..., half:]
    return jnp.concatenate([x1 * cos - x2 * sin, x2 * cos + x1 * sin], -1).astype(x.dtype)


def kernel(x, c, ada_w, ada_b, norm1_g, norm2_g, w_in, rk_mu, rk_w0, rk_w2, rk_a0, rk_a2, rk_g2, rk_kk, rk_ka, rk_rk, rk_ln_g, rk_ln_b, sb_norm_g, ml_conv_w, ml_conv_b, ml_ig_b, ml_fg_b, ml_norm_g, ds_qn_g, ds_kn_g, ds_out_g, w_out, moe_wg, moe_bg, moe_we, moe_be, moe_w1, moe_w3, moe_w2):
    B, S, D = x.shape
    H, d = N_HEADS, HEAD_DIM
    depth = ada_w.shape[0]
    pos = jnp.arange(S)
    c_act = jax.nn.silu(c)
    for l in range(depth):
        mod = (c_act @ ada_w[l] + ada_b[l])[:, None, :]
        sh1, sc1, gt1, sh2, sc2, gt2 = jnp.split(mod, 6, axis=-1)

        pA, pB, pC, pD = _in_proj(x, sc1, sh1, norm1_g[l][None], _pad_w_in(w_in[l]))

        yA = _rwkv7_time_mix(pA, rk_mu[l], rk_w0[l], rk_w2[l], rk_a0[l], rk_a2[l], rk_g2[l],
                             rk_kk[l], rk_ka[l], rk_rk[l], rk_ln_g[l], rk_ln_b[l])

        yB = _stick_breaking_norm(pB, sb_norm_g[l])

        yC = _mlstm_mix(pC, ml_conv_w[l], ml_conv_b[l], ml_ig_b[l], ml_fg_b[l], ml_norm_g[l])

        yD = _dsa_attn_norm(*_dsa_prep(pD, ds_qn_g[l], ds_kn_g[l]), ds_out_g[l])

        router = jnp.pad(jnp.concatenate([moe_wg[l], moe_we[l]], 1),
                         ((0, 0), (0, ROUTER_PAD - N_GROUPS - N_EXPERTS)))
        r_hi, r_lo = _split_bf16(router)
        x1, h2, logits = _out_proj((yA, yB, yC, yD), x, gt1, sc2, sh2, norm2_g[l][None],
                                   w_out[l].astype(BF16), r_hi, r_lo)

        moe = _hier_moe(h2.reshape(B * S, D), logits.reshape(B * S, ROUTER_PAD), moe_bg[l], moe_be[l],
                        moe_w1[l].astype(BF16), moe_w3[l].astype(BF16), moe_w2[l].astype(BF16))
        x = x1 + gt2 * moe.reshape(B, S, D)
    return x
```

```python
import functools

import jax
import jax.numpy as jnp
import numpy as np
from jax import lax
from jax.experimental import pallas as pl
from jax.experimental.pallas import tpu as pltpu

F32 = jnp.float32
BF16 = jnp.bfloat16

D_MODEL = 1024
N_MIXERS = 4
GROUP_W = D_MODEL // N_MIXERS
HEAD_DIM = 64
N_HEADS = GROUP_W // HEAD_DIM
NORM_EPS = 1e-6
RWKV_W_RANK = 32
RWKV_A_RANK = 32
RWKV_G_RANK = 64
RWKV_GN_EPS = 64e-5
SB_BLOCK = 128
ML_CHUNK = 64
ML_CONV = 4
GATE_CAP = 15.0
DSA_BLOCK = 128
IDX_HEADS = 4
IDX_DIM = 32
TOPK_MAX = 256
ROPE_THETA = 10000.0
N_GROUPS = 4
EXP_PER_GROUP = 8
N_EXPERTS = N_GROUPS * EXP_PER_GROUP
EXPERT_FF = D_MODEL // 2
TOP_IN_GROUP = 2

A_SIZES = (GROUP_W, GROUP_W, GROUP_W, RWKV_W_RANK, RWKV_A_RANK, RWKV_G_RANK)
B_SIZES = (GROUP_W, GROUP_W, GROUP_W)
C_SIZES = (GROUP_W, GROUP_W, GROUP_W, GROUP_W, N_HEADS, N_HEADS)
D_SIZES = (GROUP_W, HEAD_DIM, HEAD_DIM, IDX_HEADS * IDX_DIM, IDX_DIM, IDX_HEADS)
A_COLS = sum(A_SIZES)
B_COLS = sum(B_SIZES)
C_COLS = sum(C_SIZES)
D_COLS = sum(D_SIZES)

LANE = 128
A_PAD = 896
B_PAD = 768
C_PAD = 1152
D_PAD = 640
P_PAD = A_PAD + B_PAD + C_PAD + D_PAD
ROUTER_PAD = LANE

IN_ROWS = 256
OUT_ROWS = 512
MOE_ROWS = 512
VMEM_LIMIT = 48 * 1024 * 1024


def _split_cols(t, sizes):
    return jnp.split(t, [int(i) for i in np.cumsum(sizes)[:-1]], axis=-1)


def _in_proj_kernel(x_ref, sc_ref, sh_ref, g_ref, w_ref, oa_ref, ob_ref, oc_ref, od_ref):
    x = x_ref[0]
    y = x * lax.rsqrt(jnp.mean(x * x, -1, keepdims=True) + NORM_EPS) * g_ref[...]
    h = y * (1.0 + sc_ref[0]) + sh_ref[0]
    p = jnp.dot(h.astype(BF16), w_ref[...], preferred_element_type=F32)
    oa_ref[0] = p[:, :A_PAD]
    ob_ref[0] = p[:, A_PAD:A_PAD + B_PAD]
    oc_ref[0] = p[:, A_PAD + B_PAD:A_PAD + B_PAD + C_PAD]
    od_ref[0] = p[:, A_PAD + B_PAD + C_PAD:]


def _in_proj(x, sc, sh, g, w_pad):
    B, S, D = x.shape
    row = lambda w: pl.BlockSpec((1, IN_ROWS, w), lambda b, i: (b, i, 0))
    vec = pl.BlockSpec((1, 1, D), lambda b, i: (b, 0, 0))
    return pl.pallas_call(
        _in_proj_kernel,
        grid=(B, S // IN_ROWS),
        in_specs=[row(D), vec, vec, pl.BlockSpec((1, D), lambda b, i: (0, 0)),
                  pl.BlockSpec((D, P_PAD), lambda b, i: (0, 0))],
        out_specs=[row(A_PAD), row(B_PAD), row(C_PAD), row(D_PAD)],
        out_shape=[jax.ShapeDtypeStruct((B, S, w), F32) for w in (A_PAD, B_PAD, C_PAD, D_PAD)],
        compiler_params=pltpu.CompilerParams(dimension_semantics=("parallel", "parallel"),
                                             vmem_limit_bytes=VMEM_LIMIT),
        name="in_proj",
    )(x, sc, sh, g, w_pad)


def _pad_w_in(w):
    wa, wb, wc, wd = _split_cols(w, (A_COLS, B_COLS, C_COLS, D_COLS))
    padc = lambda t, n: jnp.pad(t, ((0, 0), (0, n - t.shape[1])))
    return jnp.concatenate([padc(wa, A_PAD), padc(wb, B_PAD), padc(wc, C_PAD), padc(wd, D_PAD)], 1).astype(BF16)


def _split_bf16(t):
    hi = t.astype(BF16)
    lo = (t - hi.astype(F32)).astype(BF16)
    return hi, lo


def _out_proj_kernel(ya_ref, yb_ref, yc_ref, yd_ref, x_ref, gt_ref, sc_ref, sh_ref, g_ref, w_ref,
                     rhi_ref, rlo_ref, rb_ref, x1_ref, h2_ref, route_ref):
    acc = jnp.zeros(x_ref.shape[1:], F32)
    for n, y_ref in enumerate((ya_ref, yb_ref, yc_ref, yd_ref)):
        acc += jnp.dot(y_ref[0].astype(BF16), w_ref[n * GROUP_W:(n + 1) * GROUP_W, :],
                       preferred_element_type=F32)
    x1 = x_ref[0] + gt_ref[0] * acc
    x1_ref[0] = x1
    y = x1 * lax.rsqrt(jnp.mean(x1 * x1, -1, keepdims=True) + NORM_EPS) * g_ref[...]
    h = y * (1.0 + sc_ref[0]) + sh_ref[0]
    hi, lo = _split_bf16(h)
    h2_ref[0] = hi
    lg = (jnp.dot(hi, rhi_ref[...], preferred_element_type=F32)
          + jnp.dot(lo, rhi_ref[...], preferred_element_type=F32)
          + jnp.dot(hi, rlo_ref[...], preferred_element_type=F32)) + rb_ref[...]
    route_ref[0] = _route(lg)


def _route(lg):
    lane = lax.broadcasted_iota(jnp.int32, lg.shape, 1)
    neg = -jnp.inf
    first = lambda hit: jnp.min(jnp.where(hit, lane, ROUTER_PAD), axis=1, keepdims=True)
    is_grp = lane < N_GROUPS
    grp = jnp.where(is_grp, lg, neg)
    g_max = jnp.max(grp, axis=1, keepdims=True)
    g_p = 1.0 / jnp.sum(jnp.where(is_grp, jnp.exp(grp - g_max), 0.0), axis=1, keepdims=True)
    g_idx = first(grp == g_max)
    e_lane = lane - N_GROUPS
    in_group = (e_lane >= 0) & (e_lane < N_EXPERTS) & (e_lane // EXP_PER_GROUP == g_idx)
    e_log = jnp.where(in_group, lg, neg)
    e1_max = jnp.max(e_log, axis=1, keepdims=True)
    e1_lane = first(e_log == e1_max)
    e_log2 = jnp.where(lane == e1_lane, neg, e_log)
    e2_max = jnp.max(e_log2, axis=1, keepdims=True)
    e2_lane = first(e_log2 == e2_max)
    ratio = jnp.exp(e2_max - e1_max)
    gate1 = g_p / (1.0 + ratio)
    gate2 = gate1 * ratio
    out = jnp.where(lane == 0, (e1_lane - N_GROUPS).astype(F32), 0.0)
    out = jnp.where(lane == 1, (e2_lane - N_GROUPS).astype(F32), out)
    out = jnp.where(lane == 2, gate1, out)
    return jnp.where(lane == 3, gate2, out)


def _out_proj(ys, x, gt, sc, sh, g, w_out, r_hi, r_lo, r_b):
    B, S, D = x.shape
    row = lambda w: pl.BlockSpec((1, OUT_ROWS, w), lambda b, i: (b, i, 0))
    vec = pl.BlockSpec((1, 1, D), lambda b, i: (b, 0, 0))
    full = lambda a: pl.BlockSpec(a.shape, lambda b, i: (0,) * a.ndim)
    return pl.pallas_call(
        _out_proj_kernel,
        grid=(B, S // OUT_ROWS),
        in_specs=[row(GROUP_W)] * 4 + [row(D), vec, vec, vec, full(g), full(w_out), full(r_hi), full(r_lo),
                                        full(r_b)],
        out_specs=[row(D), row(D), row(ROUTER_PAD)],
        out_shape=[jax.ShapeDtypeStruct((B, S, D), F32), jax.ShapeDtypeStruct((B, S, D), BF16),
                   jax.ShapeDtypeStruct((B, S, ROUTER_PAD), F32)],
        compiler_params=pltpu.CompilerParams(dimension_semantics=("parallel", "parallel"),
                                             vmem_limit_bytes=VMEM_LIMIT),
        name="out_proj",
    )(*ys, x, gt, sc, sh, g, w_out, r_hi, r_lo, r_b)


def _moe_ffn_kernel(blk_e_ref, x_ref, wt_ref, w1_ref, w3_ref, w2_ref, o_ref, w1b_ref, w3b_ref, w2b_ref):
    i = pl.program_id(0)
    changed = jnp.logical_or(i == 0, blk_e_ref[i] != blk_e_ref[jnp.maximum(i - 1, 0)])

    @pl.when(changed)
    def _():
        w1b_ref[...] = w1_ref[0].astype(BF16)
        w3b_ref[...] = w3_ref[0].astype(BF16)
        w2b_ref[...] = w2_ref[0].astype(BF16)

    xb = x_ref[...]
    a = jnp.dot(xb, w1b_ref[...], preferred_element_type=F32)
    b = jnp.dot(xb, w3b_ref[...], preferred_element_type=F32)
    hmid = (a * jax.nn.sigmoid(a) * b).astype(BF16)
    y = jnp.dot(hmid, w2b_ref[...], preferred_element_type=F32)
    o_ref[...] = y * wt_ref[...]


def _moe_ffn(blk_e, xs, wt, w1, w3, w2):
    n_slots, D = xs.shape
    n_blocks = n_slots // MOE_ROWS
    FF = w1.shape[-1]
    return pl.pallas_call(
        _moe_ffn_kernel,
        grid_spec=pltpu.PrefetchScalarGridSpec(
            num_scalar_prefetch=1,
            grid=(n_blocks,),
            in_specs=[pl.BlockSpec((MOE_ROWS, D), lambda i, e: (i, 0)),
                      pl.BlockSpec((MOE_ROWS, 1), lambda i, e: (i, 0)),
                      pl.BlockSpec((1, D, FF), lambda i, e: (e[i], 0, 0)),
                      pl.BlockSpec((1, D, FF), lambda i, e: (e[i], 0, 0)),
                      pl.BlockSpec((1, FF, D), lambda i, e: (e[i], 0, 0))],
            out_specs=pl.BlockSpec((MOE_ROWS, D), lambda i, e: (i, 0)),
            scratch_shapes=[pltpu.VMEM((D, FF), BF16), pltpu.VMEM((D, FF), BF16), pltpu.VMEM((FF, D), BF16)],
        ),
        out_shape=jax.ShapeDtypeStruct((n_slots, D), F32),
        compiler_params=pltpu.CompilerParams(dimension_semantics=("arbitrary",),
                                             vmem_limit_bytes=VMEM_LIMIT),
        name="moe_ffn",
    )(blk_e, xs, wt, w1, w3, w2)


def _hier_moe(h2, route, w1, w3, w2):
    N, D = h2.shape
    expert = route[:, :TOP_IN_GROUP].astype(jnp.int32)
    gate = route[:, TOP_IN_GROUP:2 * TOP_IN_GROUP]
    n_asg = N * TOP_IN_GROUP
    flat_e = expert.reshape(n_asg)
    order = jnp.argsort(flat_e)
    se = flat_e[order]
    bounds = jnp.searchsorted(se, jnp.arange(N_EXPERTS + 1, dtype=se.dtype))
    start, counts = bounds[:-1], bounds[1:] - bounds[:-1]
    pad_counts = (counts + MOE_ROWS - 1) // MOE_ROWS * MOE_ROWS
    pad_end = jnp.cumsum(pad_counts)
    pad_start = pad_end - pad_counts
    slot = (pad_start[se] + jnp.arange(n_asg) - start[se]).astype(jnp.int32)
    n_blocks = n_asg // MOE_ROWS + N_EXPERTS
    n_slots = n_blocks * MOE_ROWS
    blk_start = jnp.arange(n_blocks) * MOE_ROWS
    blk_e = jnp.minimum(jnp.sum(pad_end[None, :] <= blk_start[:, None], 1), N_EXPERTS - 1).astype(jnp.int32)
    slot_e = jnp.repeat(blk_e, MOE_ROWS)
    slot_pos = jnp.arange(n_slots) - pad_start[slot_e]
    slot_real = (slot_pos < counts[slot_e]) & (jnp.arange(n_slots) < pad_end[N_EXPERTS - 1])
    slot_src = jnp.where(slot_real, start[slot_e] + slot_pos, 0)
    slot_tok = jnp.where(slot_real, (order // TOP_IN_GROUP)[slot_src], 0).astype(jnp.int32)
    slot_w = jnp.where(slot_real, gate.reshape(n_asg)[order][slot_src], 0.0)
    yb = _moe_ffn(blk_e, h2[slot_tok], slot_w[:, None], w1, w3, w2)
    asg_slot = slot[jnp.argsort(order)].reshape(N, TOP_IN_GROUP)
    return yb[asg_slot[:, 0]] + yb[asg_slot[:, 1]]


INT_MIN = -2 ** 31
DSA_KEY_STEP = 512
DSA_CHAINS = 4


def _float_order_key(x):
    bits = pltpu.bitcast(x, jnp.int32)
    bits = jnp.where(x == 0.0, 0, bits)
    return bits ^ ((bits >> 31) & 0x7FFFFFFF)


def _row_count(mask):
    return jnp.sum(jnp.where(mask, 1.0, 0.0), axis=1, keepdims=True)


def _dsa_block(qd_ref, kdt_ref, vdw_ref, qi_ref, kit_ref, wi_ref, g_ref, o_ref, *, kl, n_sel):
    q0 = pl.program_id(1) * DSA_BLOCK
    kit = kit_ref[0, :, :kl]
    ki4 = jnp.concatenate([kit] * IDX_HEADS, axis=0)
    k_hi, k_lo = _split_bf16(ki4)
    qi = qi_ref[0]
    wi = wi_ref[0]
    lane_i = lax.broadcasted_iota(jnp.int32, qi.shape, 1)
    score = jnp.zeros((DSA_BLOCK, kl), F32)
    for h in range(IDX_HEADS):
        q_hi, q_lo = _split_bf16(jnp.where(lane_i // IDX_DIM == h, qi, 0.0))
        sc = (jnp.dot(q_hi, k_hi, preferred_element_type=F32) + jnp.dot(q_lo, k_hi, preferred_element_type=F32)
              + jnp.dot(q_hi, k_lo, preferred_element_type=F32))
        score = score + wi[:, h:h + 1] * jnp.maximum(sc, 0.0)
    kidx = lax.broadcasted_iota(jnp.int32, (DSA_BLOCK, kl), 1)
    qpos = q0 + lax.broadcasted_iota(jnp.int32, (DSA_BLOCK, kl), 0)
    adm = kidx <= qpos
    key = _float_order_key(jnp.where(adm, score, -jnp.inf))

    rows = DSA_BLOCK // DSA_CHAINS
    key_groups = [key[c * rows:(c + 1) * rows] for c in range(DSA_CHAINS)]

    def value_bit(it, taus):
        bit = jnp.left_shift(jnp.int32(1), 31 - it)
        out = []
        for key_c, tau_c in zip(key_groups, taus):
            cand = tau_c | bit
            cnt = _row_count(key_c >= (cand ^ INT_MIN))
            out.append(jnp.where(cnt >= n_sel, cand, tau_c))
        return tuple(out)

    taus = lax.fori_loop(0, 32, value_bit, tuple(jnp.zeros((rows, 1), jnp.int32) for _ in range(DSA_CHAINS)))
    tau = jnp.concatenate(taus, axis=0) ^ INT_MIN
    gt = key > tau
    eq = (key == tau) & adm
    need = n_sel - _row_count(gt)
    n_eq = _row_count(eq)

    def index_bits():
        def index_bit(it, bound):
            cand = bound | jnp.left_shift(jnp.int32(1), 11 - it)
            cnt = _row_count(eq & (kidx < cand))
            return jnp.where(cnt <= need, cand, bound)
        return lax.fori_loop(0, 12, index_bit, jnp.zeros((DSA_BLOCK, 1), jnp.int32))

    bound = lax.cond(jnp.max(n_eq - need) > 0.0, index_bits,
                     lambda: jnp.full((DSA_BLOCK, 1), kl, jnp.int32))
    sel = gt | (eq & (kidx < bound))

    kdt = kdt_ref[0, :, :kl]
    kd4 = jnp.concatenate([kdt] * N_HEADS, axis=0).astype(BF16)
    vdw = vdw_ref[0, :kl, :].astype(BF16)
    qd = qd_ref[0]
    lane_h = lax.broadcasted_iota(jnp.int32, qd.shape, 1) // HEAD_DIM
    out = jnp.zeros(qd.shape, F32)
    for h in range(N_HEADS):
        qm = jnp.where(lane_h == h, qd, 0.0).astype(BF16)
        lg = jnp.dot(qm, kd4, preferred_element_type=F32) * HEAD_DIM ** -0.5
        lg = jnp.where(sel, lg, -jnp.inf)
        p = jnp.exp(lg - jnp.max(lg, axis=1, keepdims=True))
        r = jnp.dot(p.astype(BF16), vdw, preferred_element_type=F32) / jnp.sum(p, axis=1, keepdims=True)
        r = r * lax.rsqrt(jnp.mean(r * r, axis=1, keepdims=True) + NORM_EPS)
        out = out + jnp.where(lane_h == h, r, 0.0)
    o_ref[0] = out * g_ref[...]


def _dsa_kernel(qd_ref, kdt_ref, vdw_ref, qi_ref, kit_ref, wi_ref, g_ref, o_ref, *, kls, n_sel):
    blocks_per_step = DSA_KEY_STEP // DSA_BLOCK
    for j, kl in enumerate(kls):
        @pl.when(pl.program_id(1) // blocks_per_step == j)
        def _():
            _dsa_block(qd_ref, kdt_ref, vdw_ref, qi_ref, kit_ref, wi_ref, g_ref, o_ref, kl=kl, n_sel=n_sel)


DSA_PREP_ROWS = 256
D_Q, D_KV, D_QI, D_KW = 0, GROUP_W, GROUP_W + LANE, GROUP_W + 2 * LANE


def _swap_halves(x, half):
    n = x.shape[1]
    lane = lax.broadcasted_iota(jnp.int32, x.shape, 1)
    return jnp.where(lane % (2 * half) < half, pltpu.roll(x, n - half, axis=1), pltpu.roll(x, half, axis=1))


def _dsa_prep_kernel(p_ref, cq_ref, sq_ref, ci_ref, si_ref, gq_ref, gk_ref,
                     qd_ref, kdt_ref, vdw_ref, qi_ref, kit_ref, wi_ref):
    GW = GROUP_W
    x = p_ref[0]
    hi_ = lax.broadcasted_iota(jnp.int32, (GW, GW), 0) // HEAD_DIM
    hj_ = lax.broadcasted_iota(jnp.int32, (GW, GW), 1) // HEAD_DIM
    ones_bd = jnp.where(hi_ == hj_, 1.0, 0.0).astype(BF16)
    q = x[:, D_Q:D_Q + GW]
    q = q * lax.rsqrt(_dot_f32_by_exact(q * q, ones_bd) * (1.0 / HEAD_DIM) + NORM_EPS) * gq_ref[...]
    qd_ref[0] = q * cq_ref[...] + _swap_halves(q, HEAD_DIM // 2) * sq_ref[...]
    kv = x[:, D_KV:D_KV + LANE]
    lane = lax.broadcasted_iota(jnp.int32, kv.shape, 1)
    is_k = lane < HEAD_DIM
    ms = jnp.sum(jnp.where(is_k, kv * kv, 0.0), axis=1, keepdims=True) * (1.0 / HEAD_DIM)
    kn = kv * lax.rsqrt(ms + NORM_EPS) * gk_ref[...]
    kr = kn * cq_ref[:, :LANE] + _swap_halves(kn, HEAD_DIM // 2) * sq_ref[:, :LANE]
    kdt_ref[0] = jnp.transpose(jnp.where(is_k, kr, 0.0))[:HEAD_DIM, :]
    v2 = jnp.where(is_k, pltpu.roll(kv, HEAD_DIM, axis=1), kv)
    vdw_ref[0] = jnp.concatenate([v2] * (GW // LANE), axis=1).astype(BF16)
    qi = x[:, D_QI:D_QI + LANE]
    qi_ref[0] = qi * ci_ref[...] + _swap_halves(qi, IDX_DIM // 2) * si_ref[...]
    kw = x[:, D_KW:D_KW + LANE]
    kir = kw * ci_ref[...] + _swap_halves(kw, IDX_DIM // 2) * si_ref[...]
    kit_ref[0] = jnp.transpose(kir)[:IDX_DIM, :]
    wi_ref[0] = pltpu.roll(kw, LANE - IDX_DIM, axis=1) * (IDX_HEADS ** -0.5 * IDX_DIM ** -0.5)


def _rope_tables(S, dim, width):
    half = dim // 2
    inv = ROPE_THETA ** (-jnp.arange(half, dtype=F32) / half)
    ang = jnp.arange(S, dtype=F32)[:, None] * inv[None, :]
    cos = jnp.tile(jnp.cos(ang), (1, width // half))
    sin = jnp.tile(jnp.concatenate([-jnp.sin(ang), jnp.sin(ang)], axis=1), (1, width // dim))
    return cos, sin


def _dsa_prep(p, qn_g, kn_g):
    B, S, _ = p.shape
    GW, R = GROUP_W, DSA_PREP_ROWS
    cq, sq = _rope_tables(S, HEAD_DIM, GW)
    ci, si = _rope_tables(S, IDX_DIM, LANE)
    gq = jnp.tile(qn_g, N_HEADS)[None]
    gk = jnp.pad(kn_g, (0, LANE - HEAD_DIM))[None]
    rows = lambda w: pl.BlockSpec((1, R, w), lambda b, i: (b, i, 0))
    tab = lambda w: pl.BlockSpec((R, w), lambda b, i: (i, 0))
    cols = lambda r: pl.BlockSpec((1, r, R), lambda b, i: (b, 0, i))
    vec = lambda w: pl.BlockSpec((1, w), lambda b, i: (0, 0))
    return pl.pallas_call(
        _dsa_prep_kernel,
        grid=(B, S // R),
        in_specs=[rows(D_PAD), tab(GW), tab(GW), tab(LANE), tab(LANE), vec(GW), vec(LANE)],
        out_specs=[rows(GW), cols(HEAD_DIM), rows(GW), rows(LANE), cols(IDX_DIM), rows(LANE)],
        out_shape=[jax.ShapeDtypeStruct((B, S, GW), F32), jax.ShapeDtypeStruct((B, HEAD_DIM, S), F32),
                   jax.ShapeDtypeStruct((B, S, GW), BF16), jax.ShapeDtypeStruct((B, S, LANE), F32),
                   jax.ShapeDtypeStruct((B, IDX_DIM, S), F32), jax.ShapeDtypeStruct((B, S, LANE), F32)],
        compiler_params=pltpu.CompilerParams(dimension_semantics=("parallel", "parallel"),
                                             vmem_limit_bytes=VMEM_LIMIT),
        name="dsa_prep",
    )(p, cq, sq, ci, si, gq, gk)


def _dsa_attn_norm(qd, kdt, vdw, qi, kit, wi, g):
    B, S, _ = qd.shape
    n_sel = min(TOPK_MAX, S // 4)
    assert S % DSA_KEY_STEP == 0 and n_sel <= DSA_KEY_STEP
    kls = tuple(range(DSA_KEY_STEP, S + 1, DSA_KEY_STEP))
    blk = lambda w: pl.BlockSpec((1, DSA_BLOCK, w), lambda b, i: (b, i, 0))
    per_b = lambda r, c: pl.BlockSpec((1, r, c), lambda b, i: (b, 0, 0))
    return pl.pallas_call(
        functools.partial(_dsa_kernel, kls=kls, n_sel=n_sel),
        grid=(B, S // DSA_BLOCK),
        in_specs=[blk(GROUP_W), per_b(HEAD_DIM, S), per_b(S, GROUP_W), blk(IDX_HEADS * IDX_DIM),
                  per_b(IDX_DIM, S), blk(LANE), pl.BlockSpec((1, GROUP_W), lambda b, i: (0, 0))],
        out_specs=blk(GROUP_W),
        out_shape=jax.ShapeDtypeStruct((B, S, GROUP_W), F32),
        compiler_params=pltpu.CompilerParams(dimension_semantics=("parallel", "parallel"),
                                             vmem_limit_bytes=VMEM_LIMIT),
        name="dsa_attn",
    )(qd, kdt, vdw, qi, kit, wi, g[None])


RWKV_CHUNK = 64
RWKV_LOW = RWKV_W_RANK + RWKV_A_RANK + RWKV_G_RANK
RWKV_STACK = N_HEADS * RWKV_CHUNK
RWKV_BATCH = 2


def _dot(a, b):
    return jnp.dot(a, b, preferred_element_type=F32)


def _dot_nt(a, b):
    return lax.dot_general(a, b, (((1,), (1,)), ((), ())), preferred_element_type=F32)


def _dot_tn(a, b):
    return lax.dot_general(a, b, (((0,), (0,)), ((), ())), preferred_element_type=F32)


def _split3_bf16(t):
    p1 = t.astype(BF16)
    r1 = t - p1.astype(F32)
    p2 = r1.astype(BF16)
    p3 = (r1 - p2.astype(F32)).astype(BF16)
    return p1, p2, p3


def _dot_f32_by_exact(a, b_exact):
    return sum(_dot(p, b_exact) for p in _split3_bf16(a))


def _dot_exact_by_f32(a_exact, b):
    return sum(_dot(a_exact, p) for p in _split3_bf16(b))


def _dot3(a, b_hi, b_lo):
    a_hi, a_lo = _split_bf16(a)
    return _dot(a_hi, b_hi) + _dot(a_lo, b_hi) + _dot(a_hi, b_lo)


def _softplus(z):
    return jnp.maximum(z, 0.0) + jnp.log(1.0 + jnp.exp(-jnp.abs(z)))


def _rwkv_kernel(p_ref, mu_ref, vec_ref, lhi_ref, llo_ref, o_ref, state_ref, prev_ref):
    @pl.when(pl.program_id(1) == 0)
    def _():
        state_ref[...] = jnp.zeros_like(state_ref)
        prev_ref[...] = jnp.zeros_like(prev_ref)

    for n in range(RWKV_BATCH):
        _rwkv_chunk(p_ref.at[n], mu_ref, vec_ref, lhi_ref, llo_ref, o_ref.at[n], state_ref.at[n], prev_ref.at[n])


def _rwkv_chunk(p_ref, mu_ref, vec_ref, lhi_ref, llo_ref, o_ref, state_ref, prev_ref):
    L, GW, ST = RWKV_CHUNK, GROUP_W, RWKV_STACK
    p = p_ref[...]
    row = lax.broadcasted_iota(jnp.int32, p.shape, 0)
    prev = jnp.where(row == 0, prev_ref[...], pltpu.roll(p, 1, axis=0))
    prev_ref[...] = p[L - 1:L, :]
    ps = p + (prev - p) * mu_ref[...]
    r, k, v = ps[:, :GW], ps[:, GW:2 * GW], ps[:, 2 * GW:3 * GW]
    low = ps[:, 3 * GW:]
    lane_low = lax.broadcasted_iota(jnp.int32, low.shape, 1)
    low = jnp.where(lane_low < RWKV_W_RANK, jnp.tanh(low),
                    jnp.where(lane_low < RWKV_W_RANK + RWKV_A_RANK, low, jax.nn.sigmoid(low)))
    up = _dot3(low, lhi_ref[...], llo_ref[...])
    w0, a0, k_k, k_a = vec_ref[0:1, :], vec_ref[1:2, :], vec_ref[2:3, :], vec_ref[3:4, :]
    r_k, ln_g, ln_b = vec_ref[4:5, :], vec_ref[5:6, :], vec_ref[6:7, :]
    logw = -jnp.exp(-_softplus(-(w0 + up[:, :GW])) - 0.5)
    rate = jax.nn.sigmoid(a0 + up[:, GW:2 * GW])
    gate = up[:, 2 * GW:]

    ri = lax.broadcasted_iota(jnp.int32, (ST, ST), 0)
    ci = lax.broadcasted_iota(jnp.int32, (ST, ST), 1)
    same_head = (ri // L) == (ci // L)
    ones_bd = jnp.where(same_head, 1.0, 0.0).astype(BF16)

    kk = k * k_k
    kk = kk / jnp.maximum(jnp.sqrt(_dot_f32_by_exact(kk * kk, ones_bd)), 1e-12)
    k = k * (1.0 + (rate - 1.0) * k_a)

    ti = lax.broadcasted_iota(jnp.int32, (L, L), 0)
    tj = lax.broadcasted_iota(jnp.int32, (L, L), 1)
    lc = _dot_exact_by_f32(jnp.where(tj <= ti, 1.0, 0.0).astype(BF16), logw)
    lc_last = lc[L - 1:L, :]
    dec_in = jnp.exp(lc)
    dec_out = jnp.exp(-lc)
    a_t = -kk * jnp.exp(lc - logw)
    b_t = kk * rate * dec_out
    k_t = k * dec_out
    r_t = r * dec_in
    to_end = jnp.exp(lc_last)

    stack = lambda t: jnp.concatenate([t] * N_HEADS, axis=0)
    bd = lambda t: jnp.where(same_head, stack(t), 0.0).astype(BF16)
    a_bd, r_bd, v_bd = bd(a_t), bd(r_t), bd(v)
    m = _dot_nt(jnp.concatenate([a_bd, r_bd], axis=0),
                jnp.concatenate([stack(b_t), stack(k_t)], axis=0).astype(BF16))
    strict = same_head & ((ci % L) < (ri % L))
    incl = same_head & ((ci % L) <= (ri % L))
    m_ab = jnp.where(strict, m[:ST, :ST], 0.0)
    m_ak = jnp.where(strict, m[:ST, ST:], 0.0).astype(BF16)
    m_rb = jnp.where(incl, m[ST:, :ST], 0.0).astype(BF16)
    m_rk = jnp.where(incl, m[ST:, ST:], 0.0).astype(BF16)

    inv = jnp.where(ri == ci, 1.0, 0.0) + m_ab
    pw = m_ab
    n_doublings = RWKV_CHUNK.bit_length() - 2
    for s in range(n_doublings):
        pw_b = pw.astype(BF16)
        pw = _dot(pw_b, pw_b)
        inv = inv + _dot(inv.astype(BF16), pw.astype(BF16))

    t0 = state_ref[...]
    t0_b = t0.astype(BF16)
    u = _dot(inv.astype(BF16), (_dot(a_bd, t0_b) + _dot(m_ak, v_bd)).astype(BF16)).astype(BF16)
    y_bd = _dot(r_bd, t0_b) + _dot(m_rb, u) + _dot(m_rk, v_bd)
    y = sum(y_bd[h * L:(h + 1) * L, :] for h in range(N_HEADS))

    to_end_col = jnp.sum(jnp.where(ri == ci, jnp.broadcast_to(to_end, (ST, ST)), 0.0), axis=1, keepdims=True)
    state_ref[...] = (to_end_col * t0 + _dot_tn(bd(b_t * to_end), u) + _dot_tn(bd(k_t * to_end), v_bd))

    inv_d = 1.0 / HEAD_DIM
    mean = _dot_f32_by_exact(y, ones_bd) * inv_d
    yc = y - mean
    var = _dot_f32_by_exact(yc * yc, ones_bd) * inv_d
    yn = yc * lax.rsqrt(var + RWKV_GN_EPS) * ln_g + ln_b
    bonus = _dot_f32_by_exact(r * k * r_k, ones_bd) * v
    o_ref[...] = (yn + bonus) * gate


def _rwkv7_time_mix(p, mu, w0, w2, a0, a2, g2, k_k, k_a, r_k, ln_g, ln_b):
    B, S, _ = p.shape
    GW = GROUP_W
    assert S % RWKV_CHUNK == 0 and RWKV_STACK == GW and RWKV_CHUNK == HEAD_DIM and B % RWKV_BATCH == 0
    low_w = jnp.zeros((RWKV_LOW, 3 * GW), F32)
    low_w = low_w.at[:RWKV_W_RANK, :GW].set(w2)
    low_w = low_w.at[RWKV_W_RANK:RWKV_W_RANK + RWKV_A_RANK, GW:2 * GW].set(a2)
    low_w = low_w.at[RWKV_W_RANK + RWKV_A_RANK:, 2 * GW:].set(g2)
    l_hi, l_lo = _split_bf16(low_w)
    vecs = jnp.stack([w0, a0, k_k, k_a, r_k, ln_g, ln_b, jnp.zeros_like(w0)], 0)
    full = lambda a: pl.BlockSpec(a.shape, lambda b, c: (0,) * a.ndim)
    mu2 = mu[None]
    return pl.pallas_call(
        _rwkv_kernel,
        grid=(B // RWKV_BATCH, S // RWKV_CHUNK),
        in_specs=[pl.BlockSpec((RWKV_BATCH, RWKV_CHUNK, A_PAD), lambda b, c: (b, c, 0)),
                  full(mu2), full(vecs), full(l_hi), full(l_lo)],
        out_specs=pl.BlockSpec((RWKV_BATCH, RWKV_CHUNK, GW), lambda b, c: (b, c, 0)),
        out_shape=jax.ShapeDtypeStruct((B, S, GW), F32),
        scratch_shapes=[pltpu.VMEM((RWKV_BATCH, RWKV_STACK, GW), F32), pltpu.VMEM((RWKV_BATCH, 1, A_PAD), F32)],
        compiler_params=pltpu.CompilerParams(dimension_semantics=("parallel", "arbitrary"),
                                             vmem_limit_bytes=VMEM_LIMIT),
        name="rwkv7",
    )(p, mu2, vecs, l_hi, l_lo)


def _sb_kernel(q_ref, k_ref, v_ref, g_ref, o_ref, kbd_ref, vbd_ref):
    i = pl.program_id(1)
    T, GW, H = SB_BLOCK, GROUP_W, N_HEADS
    lane_h = lax.broadcasted_iota(jnp.int32, (T, GW), 1) // HEAD_DIM
    k_new, v_new = k_ref[0], v_ref[0]
    for h in range(H):
        kbd_ref[i, h * T:(h + 1) * T, :] = jnp.where(lane_h == h, k_new, 0.0).astype(BF16)
        vbd_ref[i, h * T:(h + 1) * T, :] = jnp.where(lane_h == h, v_new, 0.0).astype(BF16)

    q = q_ref[0].astype(BF16)
    si = lax.broadcasted_iota(jnp.int32, (T, 2 * T), 0)
    sj = lax.broadcasted_iota(jnp.int32, (T, 2 * T), 1)
    later_and_all = jnp.where((si > sj) | (sj >= T), 1.0, 0.0).astype(BF16)
    qrow = lax.broadcasted_iota(jnp.int32, (T, H * T), 0)
    kcol = lax.broadcasted_iota(jnp.int32, (T, H * T), 1) % T
    causal = kcol < qrow

    def key_block(j, state, diagonal):
        carry, acc = state
        z = _dot_nt(q, kbd_ref[j]) * HEAD_DIM ** -0.5
        soft = jnp.log(1.0 + jnp.exp(-jnp.abs(z)))
        log1m = -(jnp.maximum(z, 0.0) + soft)
        log_sig = jnp.minimum(z, 0.0) - soft
        log1m_in = (jnp.where(causal, log1m, 0.0) if diagonal else log1m).astype(BF16)
        sums = [_dot(log1m_in[:, h * T:(h + 1) * T], later_and_all) for h in range(H)]
        suffix = jnp.concatenate([s[:, :T] for s in sums], axis=1)
        total = jnp.concatenate([s[:, T:] for s in sums], axis=1)
        att = jnp.exp(log_sig + suffix + carry)
        if diagonal:
            att = jnp.where(causal, att, 0.0)
        return carry + total, acc + _dot(att.astype(BF16), vbd_ref[j])

    state = key_block(i, (jnp.zeros((T, H * T), F32), jnp.zeros((T, GW), F32)), True)
    odd = i % 2
    state = lax.fori_loop(0, odd, lambda it, st: key_block(i - 1, st, False), state)
    top = i - 1 - odd

    def two_blocks(it, st):
        return key_block(top - 2 * it - 1, key_block(top - 2 * it, st, False), False)

    _, y = lax.fori_loop(0, i // 2, two_blocks, state)

    hi_ = lax.broadcasted_iota(jnp.int32, (GW, GW), 0) // HEAD_DIM
    hj_ = lax.broadcasted_iota(jnp.int32, (GW, GW), 1) // HEAD_DIM
    ones_bd = jnp.where(hi_ == hj_, 1.0, 0.0).astype(BF16)
    ms = _dot_f32_by_exact(y * y, ones_bd) * (1.0 / HEAD_DIM)
    o_ref[0] = y * lax.rsqrt(ms + NORM_EPS) * g_ref[...]


def _stick_breaking_norm(p, g):
    B, S, _ = p.shape
    GW = GROUP_W
    assert S % SB_BLOCK == 0
    col = lambda n: pl.BlockSpec((1, SB_BLOCK, GW), lambda b, i: (b, i, n))
    blk = col(0)
    q = k = v = p
    stacked = pltpu.VMEM((S // SB_BLOCK, N_HEADS * SB_BLOCK, GW), BF16)
    return pl.pallas_call(
        _sb_kernel,
        grid=(B, S // SB_BLOCK),
        in_specs=[col(0), col(1), col(2), pl.BlockSpec((1, GW), lambda b, i: (0, 0))],
        out_specs=blk,
        out_shape=jax.ShapeDtypeStruct((B, S, GW), F32),
        scratch_shapes=[stacked, stacked],
        compiler_params=pltpu.CompilerParams(dimension_semantics=("parallel", "arbitrary"),
                                             vmem_limit_bytes=VMEM_LIMIT),
        name="stick_breaking",
    )(q, k, v, g[None])


ML_HALO = 8


def _dot_nt_exact_by_f32(a_exact, b):
    return sum(_dot_nt(a_exact, p) for p in _split3_bf16(b))


def _mlstm_kernel(p_ref, cw_ref, cb_ref, gb_ref, g_ref, o_ref, ext_ref, ct_ref, n_ref, m_ref):
    L, GW, H = ML_CHUNK, GROUP_W, N_HEADS
    ST = H * L

    @pl.when(pl.program_id(1) == 0)
    def _():
        ext_ref[...] = jnp.zeros_like(ext_ref)
        ct_ref[...] = jnp.zeros_like(ct_ref)
        n_ref[...] = jnp.zeros_like(n_ref)
        m_ref[...] = jnp.zeros_like(m_ref)

    x = p_ref[0]
    ext_ref[ML_HALO:, :] = x[:, :2 * GW]
    conv = cb_ref[...]
    for j in range(ML_CONV):
        conv = conv + cw_ref[j:j + 1, :] * ext_ref[pl.ds(ML_HALO - (ML_CONV - 1) + j, L), :]
    ext_ref[:ML_HALO, :] = x[L - ML_HALO:, :2 * GW]
    qk = conv * jax.nn.sigmoid(conv)
    q, k = qk[:, :GW], qk[:, GW:] * HEAD_DIM ** -0.5
    v, o = x[:, 2 * GW:3 * GW], x[:, 3 * GW:4 * GW]

    gates = x[:, 4 * GW:]
    gi = lax.broadcasted_iota(jnp.int32, (LANE, 2 * GW), 0)
    gj = lax.broadcasted_iota(jnp.int32, (LANE, 2 * GW), 1)
    expand = jnp.where(gi == (gj % GW) // HEAD_DIM + H * (gj // GW), 1.0, 0.0).astype(BF16)
    graw = _dot_f32_by_exact(gates, expand) + gb_ref[...]
    capped = GATE_CAP * jnp.tanh(graw * (1.0 / GATE_CAP))
    log_i = capped[:, :GW]
    cf = capped[:, GW:]
    log_f = jnp.minimum(cf, 0.0) - jnp.log(1.0 + jnp.exp(-jnp.abs(cf)))

    ti = lax.broadcasted_iota(jnp.int32, (L, L), 0)
    tj = lax.broadcasted_iota(jnp.int32, (L, L), 1)
    bf = _dot_exact_by_f32(jnp.where(tj <= ti, 1.0, 0.0).astype(BF16), log_f)
    b_last = bf[L - 1:L, :]
    m_row, n_row, ct = m_ref[...], n_ref[...], ct_ref[...]

    ri = lax.broadcasted_iota(jnp.int32, (ST, GW), 0)
    ci = lax.broadcasted_iota(jnp.int32, (ST, GW), 1)
    same_head = (ri // L) == (ci // HEAD_DIM)
    first_lane = ci == (ri // L) * HEAD_DIM
    stack = lambda t: jnp.concatenate([t] * H, axis=0)
    pick = lambda t: jnp.sum(jnp.where(first_lane, t, 0.0), axis=1, keepdims=True)

    qs = jnp.where(same_head, stack(q), 0.0)
    qs_b = qs.astype(BF16)
    v_b = v.astype(BF16)
    b_col = pick(stack(bf))
    g_col = pick(stack(bf + m_row))
    row_part = _dot_nt_exact_by_f32(jnp.where(first_lane, 1.0, 0.0).astype(BF16), log_i - bf)
    rt = lax.broadcasted_iota(jnp.int32, (ST, L), 0) % L
    rs = lax.broadcasted_iota(jnp.int32, (ST, L), 1)
    dmat = jnp.where(rs <= rt, b_col + row_part, -jnp.inf)
    m_t = jnp.maximum(g_col, jnp.max(dmat, axis=1, keepdims=True))
    s_inter = jnp.exp(g_col - m_t)
    sqk = _dot_nt(qs_b, k.astype(BF16)) * jnp.exp(dmat - m_t)
    num = s_inter * _dot(qs_b, ct.astype(BF16)) + jnp.where(same_head, _dot(sqk.astype(BF16), v_b), 0.0)
    den = s_inter * jnp.sum(qs * n_row, axis=1, keepdims=True) + jnp.sum(sqk, axis=1, keepdims=True)
    hst = num / jnp.maximum(jnp.abs(den), jnp.exp(-m_t))
    h = sum(hst[n * L:(n + 1) * L, :] for n in range(H))

    dec = b_last - bf + log_i
    m_new = jnp.maximum(b_last + m_row, jnp.max(dec, axis=0, keepdims=True))
    kw = k * jnp.exp(dec - m_new)
    s_old = jnp.exp(b_last + m_row - m_new)
    hi_ = lax.broadcasted_iota(jnp.int32, (GW, GW), 0) // HEAD_DIM
    hj_ = lax.broadcasted_iota(jnp.int32, (GW, GW), 1) // HEAD_DIM
    ct_ref[...] = s_old * ct + jnp.where(hi_ == hj_, _dot_tn(kw.astype(BF16), v_b), 0.0)
    n_ref[...] = s_old * n_row + jnp.sum(kw, axis=0, keepdims=True)
    m_ref[...] = m_new

    ones_bd = jnp.where(hi_ == hj_, 1.0, 0.0).astype(BF16)
    ms = _dot_f32_by_exact(h * h, ones_bd) * (1.0 / HEAD_DIM)
    o_ref[0] = jax.nn.sigmoid(o) * (h * lax.rsqrt(ms + NORM_EPS) * g_ref[...])


def _mlstm_mix(p, conv_w, conv_b, ig_b, fg_b, norm_g):
    B, S, _ = p.shape
    GW = GROUP_W
    assert S % ML_CHUNK == 0 and ML_CONV - 1 <= ML_HALO <= ML_CHUNK
    gate_b = jnp.concatenate([jnp.repeat(ig_b, HEAD_DIM), jnp.repeat(fg_b, HEAD_DIM)])[None]
    full = lambda a: pl.BlockSpec(a.shape, lambda b, c: (0,) * a.ndim)
    cb2, g2 = conv_b[None], norm_g[None]
    return pl.pallas_call(
        _mlstm_kernel,
        grid=(B, S // ML_CHUNK),
        in_specs=[pl.BlockSpec((1, ML_CHUNK, C_PAD), lambda b, c: (b, c, 0)),
                  full(conv_w), full(cb2), full(gate_b), full(g2)],
        out_specs=pl.BlockSpec((1, ML_CHUNK, GW), lambda b, c: (b, c, 0)),
        out_shape=jax.ShapeDtypeStruct((B, S, GW), F32),
        scratch_shapes=[pltpu.VMEM((ML_HALO + ML_CHUNK, 2 * GW), F32), pltpu.VMEM((GW, GW), F32),
                        pltpu.VMEM((1, GW), F32), pltpu.VMEM((1, GW), F32)],
        compiler_params=pltpu.CompilerParams(dimension_semantics=("parallel", "arbitrary"),
                                             vmem_limit_bytes=VMEM_LIMIT),
        name="mlstm",
    )(p, conv_w, cb2, gate_b, g2)


def _rms_norm(x, g):
    xf = x.astype(F32)
    y = xf * lax.rsqrt(jnp.mean(xf * xf, -1, keepdims=True) + NORM_EPS)
    return (y * g.astype(F32)).astype(x.dtype)


def _rope(x, pos):
    half = x.shape[-1] // 2
    inv = ROPE_THETA ** (-jnp.arange(half, dtype=F32) / half)
    ang = pos.astype(F32)[:, None] * inv[None, :]
    cos = jnp.cos(ang)[None, :, None, :]
    sin = jnp.sin(ang)[None, :, None, :]
    xf = x.astype(F32)
    x1, x2 = xf[..., :half], xf[..., half:]
    return jnp.concatenate([x1 * cos - x2 * sin, x2 * cos + x1 * sin], -1).astype(x.dtype)


def kernel(x, c, ada_w, ada_b, norm1_g, norm2_g, w_in, rk_mu, rk_w0, rk_w2, rk_a0, rk_a2, rk_g2, rk_kk, rk_ka, rk_rk, rk_ln_g, rk_ln_b, sb_norm_g, ml_conv_w, ml_conv_b, ml_ig_b, ml_fg_b, ml_norm_g, ds_qn_g, ds_kn_g, ds_out_g, w_out, moe_wg, moe_bg, moe_we, moe_be, moe_w1, moe_w3, moe_w2):
    B, S, D = x.shape
    H, d = N_HEADS, HEAD_DIM
    depth = ada_w.shape[0]
    pos = jnp.arange(S)
    c_act = jax.nn.silu(c)
    for l in range(depth):
        mod = (c_act @ ada_w[l] + ada_b[l])[:, None, :]
        sh1, sc1, gt1, sh2, sc2, gt2 = jnp.split(mod, 6, axis=-1)

        pA, pB, pC, pD = _in_proj(x, sc1, sh1, norm1_g[l][None], _pad_w_in(w_in[l]))

        yA = _rwkv7_time_mix(pA, rk_mu[l], rk_w0[l], rk_w2[l], rk_a0[l], rk_a2[l], rk_g2[l],
                             rk_kk[l], rk_ka[l], rk_rk[l], rk_ln_g[l], rk_ln_b[l])

        yB = _stick_breaking_norm(pB, sb_norm_g[l])

        yC = _mlstm_mix(pC, ml_conv_w[l], ml_conv_b[l], ml_ig_b[l], ml_fg_b[l], ml_norm_g[l])

        yD = _dsa_attn_norm(*_dsa_prep(pD, ds_qn_g[l], ds_kn_g[l]), ds_out_g[l])

        router = jnp.pad(jnp.concatenate([moe_wg[l], moe_we[l]], 1),
                         ((0, 0), (0, ROUTER_PAD - N_GROUPS - N_EXPERTS)))
        r_hi, r_lo = _split_bf16(router)
        r_b = jnp.pad(jnp.concatenate([moe_bg[l], moe_be[l]]), (0, ROUTER_PAD - N_GROUPS - N_EXPERTS))[None]
        x1, h2, route = _out_proj((yA, yB, yC, yD), x, gt1, sc2, sh2, norm2_g[l][None],
                                  w_out[l].astype(BF16), r_hi, r_lo, r_b)

        moe = _hier_moe(h2.reshape(B * S, D), route.reshape(B * S, ROUTER_PAD),
                        moe_w1[l], moe_w3[l], moe_w2[l])
        x = x1 + gt2 * moe.reshape(B, S, D)
    return x
```

```python
import functools

import jax
import jax.numpy as jnp
import numpy as np
from jax import lax
from jax.experimental import pallas as pl
from jax.experimental.pallas import tpu as pltpu

F32 = jnp.float32
BF16 = jnp.bfloat16

D_MODEL = 1024
N_MIXERS = 4
GROUP_W = D_MODEL // N_MIXERS
HEAD_DIM = 64
N_HEADS = GROUP_W // HEAD_DIM
NORM_EPS = 1e-6
RWKV_W_RANK = 32
RWKV_A_RANK = 32
RWKV_G_RANK = 64
RWKV_GN_EPS = 64e-5
SB_BLOCK = 128
ML_CHUNK = 64
ML_CONV = 4
GATE_CAP = 15.0
DSA_BLOCK = 128
IDX_HEADS = 4
IDX_DIM = 32
TOPK_MAX = 256
ROPE_THETA = 10000.0
N_GROUPS = 4
EXP_PER_GROUP = 8
N_EXPERTS = N_GROUPS * EXP_PER_GROUP
EXPERT_FF = D_MODEL // 2
TOP_IN_GROUP = 2

A_SIZES = (GROUP_W, GROUP_W, GROUP_W, RWKV_W_RANK, RWKV_A_RANK, RWKV_G_RANK)
B_SIZES = (GROUP_W, GROUP_W, GROUP_W)
C_SIZES = (GROUP_W, GROUP_W, GROUP_W, GROUP_W, N_HEADS, N_HEADS)
D_SIZES = (GROUP_W, HEAD_DIM, HEAD_DIM, IDX_HEADS * IDX_DIM, IDX_DIM, IDX_HEADS)
A_COLS = sum(A_SIZES)
B_COLS = sum(B_SIZES)
C_COLS = sum(C_SIZES)
D_COLS = sum(D_SIZES)

LANE = 128
A_PAD = 896
B_PAD = 768
C_PAD = 1152
D_PAD = 640
P_PAD = A_PAD + B_PAD + C_PAD + D_PAD
ROUTER_PAD = LANE

IN_ROWS = 256
OUT_ROWS = 512
MOE_ROWS = 512
VMEM_LIMIT = 48 * 1024 * 1024


def _split_cols(t, sizes):
    return jnp.split(t, [int(i) for i in np.cumsum(sizes)[:-1]], axis=-1)


def _in_proj_kernel(x_ref, sc_ref, sh_ref, g_ref, w_ref, oa_ref, ob_ref, oc_ref, od_ref):
    x = x_ref[0]
    y = x * lax.rsqrt(jnp.mean(x * x, -1, keepdims=True) + NORM_EPS) * g_ref[...]
    h = y * (1.0 + sc_ref[0]) + sh_ref[0]
    p = jnp.dot(h.astype(BF16), w_ref[...], preferred_element_type=F32)
    oa_ref[0] = p[:, :A_PAD]
    ob_ref[0] = p[:, A_PAD:A_PAD + B_PAD]
    oc_ref[0] = p[:, A_PAD + B_PAD:A_PAD + B_PAD + C_PAD]
    od_ref[0] = p[:, A_PAD + B_PAD + C_PAD:]


def _in_proj(x, sc, sh, g, w_pad):
    B, S, D = x.shape
    row = lambda w: pl.BlockSpec((1, IN_ROWS, w), lambda b, i: (b, i, 0))
    vec = pl.BlockSpec((1, 1, D), lambda b, i: (b, 0, 0))
    return pl.pallas_call(
        _in_proj_kernel,
        grid=(B, S // IN_ROWS),
        in_specs=[row(D), vec, vec, pl.BlockSpec((1, D), lambda b, i: (0, 0)),
                  pl.BlockSpec((D, P_PAD), lambda b, i: (0, 0))],
        out_specs=[row(A_PAD), row(B_PAD), row(C_PAD), row(D_PAD)],
        out_shape=[jax.ShapeDtypeStruct((B, S, w), F32) for w in (A_PAD, B_PAD, C_PAD, D_PAD)],
        compiler_params=pltpu.CompilerParams(dimension_semantics=("parallel", "parallel"),
                                             vmem_limit_bytes=VMEM_LIMIT),
        name="in_proj",
    )(x, sc, sh, g, w_pad)


def _pad_w_in(w):
    wa, wb, wc, wd = _split_cols(w, (A_COLS, B_COLS, C_COLS, D_COLS))
    padc = lambda t, n: jnp.pad(t, ((0, 0), (0, n - t.shape[1])))
    return jnp.concatenate([padc(wa, A_PAD), padc(wb, B_PAD), padc(wc, C_PAD), padc(wd, D_PAD)], 1).astype(BF16)


def _split_bf16(t):
    hi = t.astype(BF16)
    lo = (t - hi.astype(F32)).astype(BF16)
    return hi, lo


def _out_proj_kernel(ya_ref, yb_ref, yc_ref, yd_ref, x_ref, gt_ref, sc_ref, sh_ref, g_ref, w_ref,
                     rhi_ref, rlo_ref, rb_ref, x1_ref, h2_ref, route_ref):
    acc = jnp.zeros(x_ref.shape[1:], F32)
    for n, y_ref in enumerate((ya_ref, yb_ref, yc_ref, yd_ref)):
        acc += jnp.dot(y_ref[0].astype(BF16), w_ref[n * GROUP_W:(n + 1) * GROUP_W, :],
                       preferred_element_type=F32)
    x1 = x_ref[0] + gt_ref[0] * acc
    x1_ref[0] = x1
    y = x1 * lax.rsqrt(jnp.mean(x1 * x1, -1, keepdims=True) + NORM_EPS) * g_ref[...]
    h = y * (1.0 + sc_ref[0]) + sh_ref[0]
    hi, lo = _split_bf16(h)
    h2_ref[0] = hi
    lg = (jnp.dot(hi, rhi_ref[...], preferred_element_type=F32)
          + jnp.dot(lo, rhi_ref[...], preferred_element_type=F32)
          + jnp.dot(hi, rlo_ref[...], preferred_element_type=F32)) + rb_ref[...]
    route_ref[0] = _route(lg)


def _route(lg):
    lane = lax.broadcasted_iota(jnp.int32, lg.shape, 1)
    neg = -jnp.inf
    first = lambda hit: jnp.min(jnp.where(hit, lane, ROUTER_PAD), axis=1, keepdims=True)
    is_grp = lane < N_GROUPS
    grp = jnp.where(is_grp, lg, neg)
    g_max = jnp.max(grp, axis=1, keepdims=True)
    g_p = 1.0 / jnp.sum(jnp.where(is_grp, jnp.exp(grp - g_max), 0.0), axis=1, keepdims=True)
    g_idx = first(grp == g_max)
    e_lane = lane - N_GROUPS
    in_group = (e_lane >= 0) & (e_lane < N_EXPERTS) & (e_lane // EXP_PER_GROUP == g_idx)
    e_log = jnp.where(in_group, lg, neg)
    e1_max = jnp.max(e_log, axis=1, keepdims=True)
    e1_lane = first(e_log == e1_max)
    e_log2 = jnp.where(lane == e1_lane, neg, e_log)
    e2_max = jnp.max(e_log2, axis=1, keepdims=True)
    e2_lane = first(e_log2 == e2_max)
    ratio = jnp.exp(e2_max - e1_max)
    gate1 = g_p / (1.0 + ratio)
    gate2 = gate1 * ratio
    out = jnp.where(lane == 0, (e1_lane - N_GROUPS).astype(F32), 0.0)
    out = jnp.where(lane == 1, (e2_lane - N_GROUPS).astype(F32), out)
    out = jnp.where(lane == 2, gate1, out)
    return jnp.where(lane == 3, gate2, out)


def _out_proj(ys, x, gt, sc, sh, g, w_out, r_hi, r_lo, r_b):
    B, S, D = x.shape
    row = lambda w: pl.BlockSpec((1, OUT_ROWS, w), lambda b, i: (b, i, 0))
    vec = pl.BlockSpec((1, 1, D), lambda b, i: (b, 0, 0))
    full = lambda a: pl.BlockSpec(a.shape, lambda b, i: (0,) * a.ndim)
    return pl.pallas_call(
        _out_proj_kernel,
        grid=(B, S // OUT_ROWS),
        in_specs=[row(GROUP_W)] * 4 + [row(D), vec, vec, vec, full(g), full(w_out), full(r_hi), full(r_lo),
                                        full(r_b)],
        out_specs=[row(D), row(D), row(ROUTER_PAD)],
        out_shape=[jax.ShapeDtypeStruct((B, S, D), F32), jax.ShapeDtypeStruct((B, S, D), BF16),
                   jax.ShapeDtypeStruct((B, S, ROUTER_PAD), F32)],
        compiler_params=pltpu.CompilerParams(dimension_semantics=("parallel", "parallel"),
                                             vmem_limit_bytes=VMEM_LIMIT),
        name="out_proj",
    )(*ys, x, gt, sc, sh, g, w_out, r_hi, r_lo, r_b)


def _moe_ffn_kernel(blk_e_ref, x_ref, wt_ref, w1_ref, w3_ref, w2_ref, o_ref, w1b_ref, w3b_ref, w2b_ref):
    i = pl.program_id(0)
    changed = jnp.logical_or(i == 0, blk_e_ref[i] != blk_e_ref[jnp.maximum(i - 1, 0)])

    @pl.when(changed)
    def _():
        w1b_ref[...] = w1_ref[0].astype(BF16)
        w3b_ref[...] = w3_ref[0].astype(BF16)
        w2b_ref[...] = w2_ref[0].astype(BF16)

    xb = x_ref[...]
    a = jnp.dot(xb, w1b_ref[...], preferred_element_type=F32)
    b = jnp.dot(xb, w3b_ref[...], preferred_element_type=F32)
    hmid = (a * jax.nn.sigmoid(a) * b).astype(BF16)
    y = jnp.dot(hmid, w2b_ref[...], preferred_element_type=F32)
    o_ref[...] = (y * wt_ref[...]).astype(o_ref.dtype)


def _moe_ffn(blk_e, xs, wt, w1, w3, w2):
    n_slots, D = xs.shape
    n_blocks = n_slots // MOE_ROWS
    FF = w1.shape[-1]
    return pl.pallas_call(
        _moe_ffn_kernel,
        grid_spec=pltpu.PrefetchScalarGridSpec(
            num_scalar_prefetch=1,
            grid=(n_blocks,),
            in_specs=[pl.BlockSpec((MOE_ROWS, D), lambda i, e: (i, 0)),
                      pl.BlockSpec((MOE_ROWS, 1), lambda i, e: (i, 0)),
                      pl.BlockSpec((1, D, FF), lambda i, e: (e[i], 0, 0)),
                      pl.BlockSpec((1, D, FF), lambda i, e: (e[i], 0, 0)),
                      pl.BlockSpec((1, FF, D), lambda i, e: (e[i], 0, 0))],
            out_specs=pl.BlockSpec((MOE_ROWS, D), lambda i, e: (i, 0)),
            scratch_shapes=[pltpu.VMEM((D, FF), BF16), pltpu.VMEM((D, FF), BF16), pltpu.VMEM((FF, D), BF16)],
        ),
        out_shape=jax.ShapeDtypeStruct((n_slots, D), BF16),
        compiler_params=pltpu.CompilerParams(dimension_semantics=("arbitrary",),
                                             vmem_limit_bytes=VMEM_LIMIT),
        name="moe_ffn",
    )(blk_e, xs, wt, w1, w3, w2)


def _hier_moe(h2, route, w1, w3, w2):
    N, D = h2.shape
    expert = route[:, :TOP_IN_GROUP].astype(jnp.int32)
    gate = route[:, TOP_IN_GROUP:2 * TOP_IN_GROUP]
    n_asg = N * TOP_IN_GROUP
    flat_e = expert.reshape(n_asg // LANE, LANE)
    order = jnp.argsort(flat_e.reshape(n_asg))
    rank = jnp.argsort(order).astype(jnp.int32)
    counts = jnp.sum(flat_e[None] == jnp.arange(N_EXPERTS)[:, None, None], axis=(1, 2)).astype(jnp.int32)
    start = jnp.cumsum(counts) - counts
    pad_counts = (counts + MOE_ROWS - 1) // MOE_ROWS * MOE_ROWS
    pad_end = jnp.cumsum(pad_counts)
    pad_start = pad_end - pad_counts
    shift = pad_start - start
    asg_shift = jnp.zeros_like(flat_e)
    for e in range(N_EXPERTS):
        asg_shift = jnp.where(flat_e == e, shift[e], asg_shift)
    asg_slot = (rank + asg_shift.reshape(n_asg)).reshape(N, TOP_IN_GROUP)
    n_blocks = n_asg // MOE_ROWS + N_EXPERTS
    blk_start = jnp.arange(n_blocks) * MOE_ROWS
    blk_e = jnp.minimum(jnp.sum(pad_end[None, :] <= blk_start[:, None], 1), N_EXPERTS - 1).astype(jnp.int32)
    blk_pos = blk_start - pad_start[blk_e]
    row = jnp.arange(MOE_ROWS)[None, :]
    slot_real = (blk_pos[:, None] + row < counts[blk_e][:, None]) & (blk_start[:, None] < pad_end[N_EXPERTS - 1])
    slot_src = jnp.where(slot_real, (start[blk_e] + blk_pos)[:, None] + row, 0).reshape(-1)
    slot_asg = order[slot_src]
    slot_real = slot_real.reshape(-1)
    slot_tok = jnp.where(slot_real, slot_asg // TOP_IN_GROUP, 0).astype(jnp.int32)
    slot_w = jnp.where(slot_real, gate.reshape(n_asg)[slot_asg], 0.0)
    yb = _moe_ffn(blk_e, h2[slot_tok], slot_w[:, None], w1, w3, w2)
    return yb[asg_slot[:, 0]].astype(F32) + yb[asg_slot[:, 1]].astype(F32)


INT_MIN = -2 ** 31
DSA_KEY_STEP = 512


def _float_order_key(x):
    bits = pltpu.bitcast(x, jnp.int32)
    bits = jnp.where(x == 0.0, 0, bits)
    return bits ^ ((bits >> 31) & 0x7FFFFFFF)


def _col_count(mask):
    return jnp.sum(jnp.where(mask, 1.0, 0.0), axis=0, keepdims=True)


def _head_block_diag(t, group):
    n_heads = t.shape[0] // group
    row_h = lax.broadcasted_iota(jnp.int32, t.shape, 0) // group
    return jnp.concatenate([jnp.where(row_h == h, t, 0.0) for h in range(n_heads)], axis=1)


def _dsa_block_t(qd_ref, kd4_ref, vdwt_ref, qi_ref, kihi_ref, kilo_ref, wi_ref, g_ref, o_ref, *, kl, n_sel):
    Q = DSA_BLOCK
    q0 = pl.program_id(1) * Q
    w_hi, w_lo = _split_bf16(_head_block_diag(jnp.transpose(qi_ref[0]), IDX_DIM))
    k_hi, k_lo = kihi_ref[0, :kl, :], kilo_ref[0, :kl, :]
    sc = _dot(k_hi, w_hi) + _dot(k_lo, w_hi) + _dot(k_hi, w_lo)
    wit = jnp.transpose(wi_ref[0])
    score = sum(wit[h:h + 1, :] * jnp.maximum(sc[:, h * Q:(h + 1) * Q], 0.0) for h in range(IDX_HEADS))
    kidx = lax.broadcasted_iota(jnp.int32, (kl, Q), 0)
    qpos = q0 + lax.broadcasted_iota(jnp.int32, (kl, Q), 1)
    adm = kidx <= qpos
    key = _float_order_key(jnp.where(adm, score, -jnp.inf))

    def value_bit(it, tau):
        cand = tau | jnp.left_shift(jnp.int32(1), 31 - it)
        return jnp.where(_col_count(key >= (cand ^ INT_MIN)) >= n_sel, cand, tau)

    tau = lax.fori_loop(0, 32, value_bit, jnp.zeros((1, Q), jnp.int32)) ^ INT_MIN
    gt = key > tau
    eq = (key == tau) & adm
    need = n_sel - _col_count(gt)
    n_eq = _col_count(eq)

    def index_bits():
        def index_bit(it, bound):
            cand = bound | jnp.left_shift(jnp.int32(1), 11 - it)
            return jnp.where(_col_count(eq & (kidx < cand)) <= need, cand, bound)
        return lax.fori_loop(0, 12, index_bit, jnp.zeros((1, Q), jnp.int32))

    bound = lax.cond(jnp.max(n_eq - need) > 0.0, index_bits, lambda: jnp.full((1, Q), kl, jnp.int32))
    sel = gt | (eq & (kidx < bound))

    w_att = _head_block_diag(jnp.transpose(qd_ref[0]), HEAD_DIM).astype(BF16)
    lg = _dot(kd4_ref[0, :kl, :], w_att) * HEAD_DIM ** -0.5
    lg = jnp.where(jnp.concatenate([sel] * N_HEADS, axis=1), lg, -jnp.inf)
    p = jnp.exp(lg - jnp.max(lg, axis=0, keepdims=True))
    out_t = _dot(vdwt_ref[0, :, :kl], p.astype(BF16)) / jnp.sum(p, axis=0, keepdims=True)
    row_h = lax.broadcasted_iota(jnp.int32, (GROUP_W, Q), 0) // HEAD_DIM
    nat_t = sum(jnp.where(row_h == h, out_t[:, h * Q:(h + 1) * Q], 0.0) for h in range(N_HEADS))
    r = jnp.transpose(nat_t)
    hi_ = lax.broadcasted_iota(jnp.int32, (GROUP_W, GROUP_W), 0) // HEAD_DIM
    hj_ = lax.broadcasted_iota(jnp.int32, (GROUP_W, GROUP_W), 1) // HEAD_DIM
    ms = _dot_f32_by_exact(r * r, jnp.where(hi_ == hj_, 1.0, 0.0).astype(BF16)) * (1.0 / HEAD_DIM)
    o_ref[0] = r * lax.rsqrt(ms + NORM_EPS) * g_ref[...]


def _dsa_kernel(qd_ref, kd4_ref, vdwt_ref, qi_ref, kihi_ref, kilo_ref, wi_ref, g_ref, o_ref, *, kls, n_sel):
    blocks_per_step = DSA_KEY_STEP // DSA_BLOCK
    for j, kl in enumerate(kls):
        @pl.when(pl.program_id(1) // blocks_per_step == j)
        def _():
            _dsa_block_t(qd_ref, kd4_ref, vdwt_ref, qi_ref, kihi_ref, kilo_ref, wi_ref, g_ref, o_ref,
                         kl=kl, n_sel=n_sel)


DSA_PREP_ROWS = 256
D_Q, D_KV, D_QI, D_KW = 0, GROUP_W, GROUP_W + LANE, GROUP_W + 2 * LANE


def _swap_halves(x, half):
    n = x.shape[1]
    lane = lax.broadcasted_iota(jnp.int32, x.shape, 1)
    return jnp.where(lane % (2 * half) < half, pltpu.roll(x, n - half, axis=1), pltpu.roll(x, half, axis=1))


def _dsa_prep_kernel(p_ref, cq_ref, sq_ref, ci_ref, si_ref, gq_ref, gk_ref,
                     qd_ref, kd4_ref, vdwt_ref, qi_ref, kihi_ref, kilo_ref, wi_ref):
    GW = GROUP_W
    x = p_ref[0]
    hi_ = lax.broadcasted_iota(jnp.int32, (GW, GW), 0) // HEAD_DIM
    hj_ = lax.broadcasted_iota(jnp.int32, (GW, GW), 1) // HEAD_DIM
    ones_bd = jnp.where(hi_ == hj_, 1.0, 0.0).astype(BF16)
    q = x[:, D_Q:D_Q + GW]
    q = q * lax.rsqrt(_dot_f32_by_exact(q * q, ones_bd) * (1.0 / HEAD_DIM) + NORM_EPS) * gq_ref[...]
    qd_ref[0] = q * cq_ref[...] + _swap_halves(q, HEAD_DIM // 2) * sq_ref[...]
    kv = x[:, D_KV:D_KV + LANE]
    lane = lax.broadcasted_iota(jnp.int32, kv.shape, 1)
    is_k = lane < HEAD_DIM
    ms = jnp.sum(jnp.where(is_k, kv * kv, 0.0), axis=1, keepdims=True) * (1.0 / HEAD_DIM)
    kn = kv * lax.rsqrt(ms + NORM_EPS) * gk_ref[...]
    kr = kn * cq_ref[:, :LANE] + _swap_halves(kn, HEAD_DIM // 2) * sq_ref[:, :LANE]
    k2 = jnp.where(is_k, kr, pltpu.roll(kr, HEAD_DIM, axis=1))
    kd4_ref[0] = jnp.concatenate([k2] * (GW // LANE), axis=1).astype(BF16)
    v2 = jnp.where(is_k, pltpu.roll(kv, HEAD_DIM, axis=1), kv)
    v2t = jnp.transpose(v2)
    vdwt_ref[0] = jnp.concatenate([v2t] * (GW // LANE), axis=0).astype(BF16)
    qi = x[:, D_QI:D_QI + LANE]
    qi_ref[0] = qi * ci_ref[...] + _swap_halves(qi, IDX_DIM // 2) * si_ref[...]
    kw = x[:, D_KW:D_KW + LANE]
    kir = kw * ci_ref[...] + _swap_halves(kw, IDX_DIM // 2) * si_ref[...]
    ki1 = jnp.where(lane < IDX_DIM, kir, 0.0)
    ki2 = ki1 + pltpu.roll(ki1, IDX_DIM, axis=1)
    ki4 = ki2 + pltpu.roll(ki2, 2 * IDX_DIM, axis=1)
    kihi_ref[0], kilo_ref[0] = _split_bf16(ki4)
    wi_ref[0] = pltpu.roll(kw, LANE - IDX_DIM, axis=1) * (IDX_HEADS ** -0.5 * IDX_DIM ** -0.5)


def _rope_tables(S, dim, width):
    half = dim // 2
    inv = ROPE_THETA ** (-jnp.arange(half, dtype=F32) / half)
    ang = jnp.arange(S, dtype=F32)[:, None] * inv[None, :]
    cos = jnp.tile(jnp.cos(ang), (1, width // half))
    sin = jnp.tile(jnp.concatenate([-jnp.sin(ang), jnp.sin(ang)], axis=1), (1, width // dim))
    return cos, sin


def _dsa_prep(p, qn_g, kn_g):
    B, S, _ = p.shape
    GW, R = GROUP_W, DSA_PREP_ROWS
    cq, sq = _rope_tables(S, HEAD_DIM, GW)
    ci, si = _rope_tables(S, IDX_DIM, LANE)
    gq = jnp.tile(qn_g, N_HEADS)[None]
    gk = jnp.pad(kn_g, (0, LANE - HEAD_DIM))[None]
    rows = lambda w: pl.BlockSpec((1, R, w), lambda b, i: (b, i, 0))
    tab = lambda w: pl.BlockSpec((R, w), lambda b, i: (i, 0))
    cols = lambda r: pl.BlockSpec((1, r, R), lambda b, i: (b, 0, i))
    vec = lambda w: pl.BlockSpec((1, w), lambda b, i: (0, 0))
    return pl.pallas_call(
        _dsa_prep_kernel,
        grid=(B, S // R),
        in_specs=[rows(D_PAD), tab(GW), tab(GW), tab(LANE), tab(LANE), vec(GW), vec(LANE)],
        out_specs=[rows(GW), rows(GW), cols(GW), rows(LANE), rows(LANE), rows(LANE), rows(LANE)],
        out_shape=[jax.ShapeDtypeStruct((B, S, GW), F32), jax.ShapeDtypeStruct((B, S, GW), BF16),
                   jax.ShapeDtypeStruct((B, GW, S), BF16), jax.ShapeDtypeStruct((B, S, LANE), F32),
                   jax.ShapeDtypeStruct((B, S, LANE), BF16), jax.ShapeDtypeStruct((B, S, LANE), BF16),
                   jax.ShapeDtypeStruct((B, S, LANE), F32)],
        compiler_params=pltpu.CompilerParams(dimension_semantics=("parallel", "parallel"),
                                             vmem_limit_bytes=VMEM_LIMIT),
        name="dsa_prep",
    )(p, cq, sq, ci, si, gq, gk)


def _dsa_attn_norm(qd, kd4, vdwt, qi, ki_hi, ki_lo, wi, g):
    B, S, _ = qd.shape
    n_sel = min(TOPK_MAX, S // 4)
    assert S % DSA_KEY_STEP == 0 and n_sel <= DSA_KEY_STEP
    kls = tuple(range(DSA_KEY_STEP, S + 1, DSA_KEY_STEP))
    blk = lambda w: pl.BlockSpec((1, DSA_BLOCK, w), lambda b, i: (b, i, 0))
    per_b = lambda r, c: pl.BlockSpec((1, r, c), lambda b, i: (b, 0, 0))
    return pl.pallas_call(
        functools.partial(_dsa_kernel, kls=kls, n_sel=n_sel),
        grid=(B, S // DSA_BLOCK),
        in_specs=[blk(GROUP_W), per_b(S, GROUP_W), per_b(GROUP_W, S), blk(IDX_HEADS * IDX_DIM),
                  per_b(S, LANE), per_b(S, LANE), blk(LANE), pl.BlockSpec((1, GROUP_W), lambda b, i: (0, 0))],
        out_specs=blk(GROUP_W),
        out_shape=jax.ShapeDtypeStruct((B, S, GROUP_W), F32),
        compiler_params=pltpu.CompilerParams(dimension_semantics=("parallel", "parallel"),
                                             vmem_limit_bytes=VMEM_LIMIT),
        name="dsa_attn",
    )(qd, kd4, vdwt, qi, ki_hi, ki_lo, wi, g[None])


RWKV_CHUNK = 64
RWKV_LOW = RWKV_W_RANK + RWKV_A_RANK + RWKV_G_RANK
RWKV_STACK = N_HEADS * RWKV_CHUNK
RWKV_BATCH = 2


def _dot(a, b):
    return jnp.dot(a, b, preferred_element_type=F32)


def _dot_nt(a, b):
    return lax.dot_general(a, b, (((1,), (1,)), ((), ())), preferred_element_type=F32)


def _dot_tn(a, b):
    return lax.dot_general(a, b, (((0,), (0,)), ((), ())), preferred_element_type=F32)


def _split3_bf16(t):
    p1 = t.astype(BF16)
    r1 = t - p1.astype(F32)
    p2 = r1.astype(BF16)
    p3 = (r1 - p2.astype(F32)).astype(BF16)
    return p1, p2, p3


def _dot_f32_by_exact(a, b_exact):
    return sum(_dot(p, b_exact) for p in _split3_bf16(a))


def _dot_exact_by_f32(a_exact, b):
    return sum(_dot(a_exact, p) for p in _split3_bf16(b))


def _dot3(a, b_hi, b_lo):
    a_hi, a_lo = _split_bf16(a)
    return _dot(a_hi, b_hi) + _dot(a_lo, b_hi) + _dot(a_hi, b_lo)


def _softplus(z):
    return jnp.maximum(z, 0.0) + jnp.log(1.0 + jnp.exp(-jnp.abs(z)))


def _rwkv_kernel(p_ref, mu_ref, vec_ref, lhi_ref, llo_ref, o_ref, state_ref, prev_ref):
    @pl.when(pl.program_id(1) == 0)
    def _():
        state_ref[...] = jnp.zeros_like(state_ref)
        prev_ref[...] = jnp.zeros_like(prev_ref)

    for n in range(RWKV_BATCH):
        _rwkv_chunk(p_ref.at[n], mu_ref, vec_ref, lhi_ref, llo_ref, o_ref.at[n], state_ref.at[n], prev_ref.at[n])


def _rwkv_chunk(p_ref, mu_ref, vec_ref, lhi_ref, llo_ref, o_ref, state_ref, prev_ref):
    L, GW, ST = RWKV_CHUNK, GROUP_W, RWKV_STACK
    p = p_ref[...]
    row = lax.broadcasted_iota(jnp.int32, p.shape, 0)
    prev = jnp.where(row == 0, prev_ref[...], pltpu.roll(p, 1, axis=0))
    prev_ref[...] = p[L - 1:L, :]
    ps = p + (prev - p) * mu_ref[...]
    r, k, v = ps[:, :GW], ps[:, GW:2 * GW], ps[:, 2 * GW:3 * GW]
    low = ps[:, 3 * GW:]
    lane_low = lax.broadcasted_iota(jnp.int32, low.shape, 1)
    low = jnp.where(lane_low < RWKV_W_RANK, jnp.tanh(low),
                    jnp.where(lane_low < RWKV_W_RANK + RWKV_A_RANK, low, jax.nn.sigmoid(low)))
    up = _dot3(low, lhi_ref[...], llo_ref[...])
    w0, a0, k_k, k_a = vec_ref[0:1, :], vec_ref[1:2, :], vec_ref[2:3, :], vec_ref[3:4, :]
    r_k, ln_g, ln_b = vec_ref[4:5, :], vec_ref[5:6, :], vec_ref[6:7, :]
    logw = -jnp.exp(-_softplus(-(w0 + up[:, :GW])) - 0.5)
    rate = jax.nn.sigmoid(a0 + up[:, GW:2 * GW])
    gate = up[:, 2 * GW:]

    ri = lax.broadcasted_iota(jnp.int32, (ST, ST), 0)
    ci = lax.broadcasted_iota(jnp.int32, (ST, ST), 1)
    same_head = (ri // L) == (ci // L)
    ones_bd = jnp.where(same_head, 1.0, 0.0).astype(BF16)

    kk = k * k_k
    kk = kk / jnp.maximum(jnp.sqrt(_dot_f32_by_exact(kk * kk, ones_bd)), 1e-12)
    k = k * (1.0 + (rate - 1.0) * k_a)

    ti = lax.broadcasted_iota(jnp.int32, (L, L), 0)
    tj = lax.broadcasted_iota(jnp.int32, (L, L), 1)
    lc = _dot_exact_by_f32(jnp.where(tj <= ti, 1.0, 0.0).astype(BF16), logw)
    lc_last = lc[L - 1:L, :]
    dec_in = jnp.exp(lc)
    dec_out = jnp.exp(-lc)
    a_t = -kk * jnp.exp(lc - logw)
    b_t = kk * rate * dec_out
    k_t = k * dec_out
    r_t = r * dec_in
    to_end = jnp.exp(lc_last)

    stack = lambda t: jnp.concatenate([t] * N_HEADS, axis=0)
    bd = lambda t: jnp.where(same_head, stack(t), 0.0).astype(BF16)
    a_bd, r_bd, v_bd = bd(a_t), bd(r_t), bd(v)
    m = _dot_nt(jnp.concatenate([a_bd, r_bd], axis=0),
                jnp.concatenate([stack(b_t), stack(k_t)], axis=0).astype(BF16))
    strict = same_head & ((ci % L) < (ri % L))
    incl = same_head & ((ci % L) <= (ri % L))
    m_ab = jnp.where(strict, m[:ST, :ST], 0.0)
    m_ak = jnp.where(strict, m[:ST, ST:], 0.0).astype(BF16)
    m_rb = jnp.where(incl, m[ST:, :ST], 0.0).astype(BF16)
    m_rk = jnp.where(incl, m[ST:, ST:], 0.0).astype(BF16)

    inv = jnp.where(ri == ci, 1.0, 0.0) + m_ab
    pw = m_ab
    n_doublings = RWKV_CHUNK.bit_length() - 2
    for s in range(n_doublings):
        pw_b = pw.astype(BF16)
        pw = _dot(pw_b, pw_b)
        inv = inv + _dot(inv.astype(BF16), pw.astype(BF16))

    t0 = state_ref[...]
    t0_b = t0.astype(BF16)
    u = _dot(inv.astype(BF16), (_dot(a_bd, t0_b) + _dot(m_ak, v_bd)).astype(BF16)).astype(BF16)
    y_bd = _dot(r_bd, t0_b) + _dot(m_rb, u) + _dot(m_rk, v_bd)
    y = sum(y_bd[h * L:(h + 1) * L, :] for h in range(N_HEADS))

    to_end_col = jnp.sum(jnp.where(ri == ci, jnp.broadcast_to(to_end, (ST, ST)), 0.0), axis=1, keepdims=True)
    state_ref[...] = (to_end_col * t0 + _dot_tn(bd(b_t * to_end), u) + _dot_tn(bd(k_t * to_end), v_bd))

    inv_d = 1.0 / HEAD_DIM
    mean = _dot_f32_by_exact(y, ones_bd) * inv_d
    yc = y - mean
    var = _dot_f32_by_exact(yc * yc, ones_bd) * inv_d
    yn = yc * lax.rsqrt(var + RWKV_GN_EPS) * ln_g + ln_b
    bonus = _dot_f32_by_exact(r * k * r_k, ones_bd) * v
    o_ref[...] = (yn + bonus) * gate


def _rwkv7_time_mix(p, mu, w0, w2, a0, a2, g2, k_k, k_a, r_k, ln_g, ln_b):
    B, S, _ = p.shape
    GW = GROUP_W
    assert S % RWKV_CHUNK == 0 and RWKV_STACK == GW and RWKV_CHUNK == HEAD_DIM and B % RWKV_BATCH == 0
    low_w = jnp.zeros((RWKV_LOW, 3 * GW), F32)
    low_w = low_w.at[:RWKV_W_RANK, :GW].set(w2)
    low_w = low_w.at[RWKV_W_RANK:RWKV_W_RANK + RWKV_A_RANK, GW:2 * GW].set(a2)
    low_w = low_w.at[RWKV_W_RANK + RWKV_A_RANK:, 2 * GW:].set(g2)
    l_hi, l_lo = _split_bf16(low_w)
    vecs = jnp.stack([w0, a0, k_k, k_a, r_k, ln_g, ln_b, jnp.zeros_like(w0)], 0)
    full = lambda a: pl.BlockSpec(a.shape, lambda b, c: (0,) * a.ndim)
    mu2 = mu[None]
    return pl.pallas_call(
        _rwkv_kernel,
        grid=(B // RWKV_BATCH, S // RWKV_CHUNK),
        in_specs=[pl.BlockSpec((RWKV_BATCH, RWKV_CHUNK, A_PAD), lambda b, c: (b, c, 0)),
                  full(mu2), full(vecs), full(l_hi), full(l_lo)],
        out_specs=pl.BlockSpec((RWKV_BATCH, RWKV_CHUNK, GW), lambda b, c: (b, c, 0)),
        out_shape=jax.ShapeDtypeStruct((B, S, GW), F32),
        scratch_shapes=[pltpu.VMEM((RWKV_BATCH, RWKV_STACK, GW), F32), pltpu.VMEM((RWKV_BATCH, 1, A_PAD), F32)],
        compiler_params=pltpu.CompilerParams(dimension_semantics=("parallel", "arbitrary"),
                                             vmem_limit_bytes=VMEM_LIMIT),
        name="rwkv7",
    )(p, mu2, vecs, l_hi, l_lo)


def _sb_kernel(q_ref, k_ref, v_ref, g_ref, o_ref, kbd_ref, vbd_ref):
    i = pl.program_id(1)
    T, GW, H = SB_BLOCK, GROUP_W, N_HEADS
    lane_h = lax.broadcasted_iota(jnp.int32, (T, GW), 1) // HEAD_DIM
    k_new, v_new = k_ref[0], v_ref[0]
    for h in range(H):
        kbd_ref[i, h * T:(h + 1) * T, :] = jnp.where(lane_h == h, k_new, 0.0).astype(BF16)
        vbd_ref[i, h * T:(h + 1) * T, :] = jnp.where(lane_h == h, v_new, 0.0).astype(BF16)

    q = q_ref[0].astype(BF16)
    si = lax.broadcasted_iota(jnp.int32, (T, 2 * T), 0)
    sj = lax.broadcasted_iota(jnp.int32, (T, 2 * T), 1)
    later_and_all = jnp.where((si > sj) | (sj >= T), 1.0, 0.0).astype(BF16)
    qrow = lax.broadcasted_iota(jnp.int32, (T, H * T), 0)
    kcol = lax.broadcasted_iota(jnp.int32, (T, H * T), 1) % T
    causal = kcol < qrow

    def key_block(j, state, diagonal):
        carry, acc = state
        z = _dot_nt(q, kbd_ref[j]) * HEAD_DIM ** -0.5
        soft = jnp.log(1.0 + jnp.exp(-jnp.abs(z)))
        log1m = -(jnp.maximum(z, 0.0) + soft)
        log_sig = jnp.minimum(z, 0.0) - soft
        log1m_in = (jnp.where(causal, log1m, 0.0) if diagonal else log1m).astype(BF16)
        sums = [_dot(log1m_in[:, h * T:(h + 1) * T], later_and_all) for h in range(H)]
        suffix = jnp.concatenate([s[:, :T] for s in sums], axis=1)
        total = jnp.concatenate([s[:, T:] for s in sums], axis=1)
        att = jnp.exp(log_sig + suffix + carry)
        if diagonal:
            att = jnp.where(causal, att, 0.0)
        return carry + total, acc + _dot(att.astype(BF16), vbd_ref[j])

    state = key_block(i, (jnp.zeros((T, H * T), F32), jnp.zeros((T, GW), F32)), True)
    odd = i % 2
    state = lax.fori_loop(0, odd, lambda it, st: key_block(i - 1, st, False), state)
    top = i - 1 - odd

    def two_blocks(it, st):
        return key_block(top - 2 * it - 1, key_block(top - 2 * it, st, False), False)

    _, y = lax.fori_loop(0, i // 2, two_blocks, state)

    hi_ = lax.broadcasted_iota(jnp.int32, (GW, GW), 0) // HEAD_DIM
    hj_ = lax.broadcasted_iota(jnp.int32, (GW, GW), 1) // HEAD_DIM
    ones_bd = jnp.where(hi_ == hj_, 1.0, 0.0).astype(BF16)
    ms = _dot_f32_by_exact(y * y, ones_bd) * (1.0 / HEAD_DIM)
    o_ref[0] = y * lax.rsqrt(ms + NORM_EPS) * g_ref[...]


def _stick_breaking_norm(p, g):
    B, S, _ = p.shape
    GW = GROUP_W
    assert S % SB_BLOCK == 0
    col = lambda n: pl.BlockSpec((1, SB_BLOCK, GW), lambda b, i: (b, i, n))
    blk = col(0)
    q = k = v = p
    stacked = pltpu.VMEM((S // SB_BLOCK, N_HEADS * SB_BLOCK, GW), BF16)
    return pl.pallas_call(
        _sb_kernel,
        grid=(B, S // SB_BLOCK),
        in_specs=[col(0), col(1), col(2), pl.BlockSpec((1, GW), lambda b, i: (0, 0))],
        out_specs=blk,
        out_shape=jax.ShapeDtypeStruct((B, S, GW), F32),
        scratch_shapes=[stacked, stacked],
        compiler_params=pltpu.CompilerParams(dimension_semantics=("parallel", "arbitrary"),
                                             vmem_limit_bytes=VMEM_LIMIT),
        name="stick_breaking",
    )(q, k, v, g[None])


ML_HALO = 8


def _dot_nt_exact_by_f32(a_exact, b):
    return sum(_dot_nt(a_exact, p) for p in _split3_bf16(b))


def _mlstm_kernel(p_ref, cw_ref, cb_ref, gb_ref, g_ref, o_ref, ext_ref, ct_ref, n_ref, m_ref):
    L, GW, H = ML_CHUNK, GROUP_W, N_HEADS
    ST = H * L

    @pl.when(pl.program_id(1) == 0)
    def _():
        ext_ref[...] = jnp.zeros_like(ext_ref)
        ct_ref[...] = jnp.zeros_like(ct_ref)
        n_ref[...] = jnp.zeros_like(n_ref)
        m_ref[...] = jnp.zeros_like(m_ref)

    x = p_ref[0]
    ext_ref[ML_HALO:, :] = x[:, :2 * GW]
    conv = cb_ref[...]
    for j in range(ML_CONV):
        conv = conv + cw_ref[j:j + 1, :] * ext_ref[pl.ds(ML_HALO - (ML_CONV - 1) + j, L), :]
    ext_ref[:ML_HALO, :] = x[L - ML_HALO:, :2 * GW]
    qk = conv * jax.nn.sigmoid(conv)
    q, k = qk[:, :GW], qk[:, GW:] * HEAD_DIM ** -0.5
    v, o = x[:, 2 * GW:3 * GW], x[:, 3 * GW:4 * GW]

    gates = x[:, 4 * GW:]
    gi = lax.broadcasted_iota(jnp.int32, (LANE, 2 * GW), 0)
    gj = lax.broadcasted_iota(jnp.int32, (LANE, 2 * GW), 1)
    expand = jnp.where(gi == (gj % GW) // HEAD_DIM + H * (gj // GW), 1.0, 0.0).astype(BF16)
    graw = _dot_f32_by_exact(gates, expand) + gb_ref[...]
    capped = GATE_CAP * jnp.tanh(graw * (1.0 / GATE_CAP))
    log_i = capped[:, :GW]
    cf = capped[:, GW:]
    log_f = jnp.minimum(cf, 0.0) - jnp.log(1.0 + jnp.exp(-jnp.abs(cf)))

    ti = lax.broadcasted_iota(jnp.int32, (L, L), 0)
    tj = lax.broadcasted_iota(jnp.int32, (L, L), 1)
    bf = _dot_exact_by_f32(jnp.where(tj <= ti, 1.0, 0.0).astype(BF16), log_f)
    b_last = bf[L - 1:L, :]
    m_row, n_row, ct = m_ref[...], n_ref[...], ct_ref[...]

    ri = lax.broadcasted_iota(jnp.int32, (ST, GW), 0)
    ci = lax.broadcasted_iota(jnp.int32, (ST, GW), 1)
    same_head = (ri // L) == (ci // HEAD_DIM)
    first_lane = ci == (ri // L) * HEAD_DIM
    stack = lambda t: jnp.concatenate([t] * H, axis=0)
    pick = lambda t: jnp.sum(jnp.where(first_lane, t, 0.0), axis=1, keepdims=True)

    qs = jnp.where(same_head, stack(q), 0.0)
    qs_b = qs.astype(BF16)
    v_b = v.astype(BF16)
    b_col = pick(stack(bf))
    g_col = pick(stack(bf + m_row))
    row_part = _dot_nt_exact_by_f32(jnp.where(first_lane, 1.0, 0.0).astype(BF16), log_i - bf)
    rt = lax.broadcasted_iota(jnp.int32, (ST, L), 0) % L
    rs = lax.broadcasted_iota(jnp.int32, (ST, L), 1)
    dmat = jnp.where(rs <= rt, b_col + row_part, -jnp.inf)
    m_t = jnp.maximum(g_col, jnp.max(dmat, axis=1, keepdims=True))
    s_inter = jnp.exp(g_col - m_t)
    sqk = _dot_nt(qs_b, k.astype(BF16)) * jnp.exp(dmat - m_t)
    num = s_inter * _dot(qs_b, ct.astype(BF16)) + jnp.where(same_head, _dot(sqk.astype(BF16), v_b), 0.0)
    den = s_inter * jnp.sum(qs * n_row, axis=1, keepdims=True) + jnp.sum(sqk, axis=1, keepdims=True)
    hst = num / jnp.maximum(jnp.abs(den), jnp.exp(-m_t))
    h = sum(hst[n * L:(n + 1) * L, :] for n in range(H))

    dec = b_last - bf + log_i
    m_new = jnp.maximum(b_last + m_row, jnp.max(dec, axis=0, keepdims=True))
    kw = k * jnp.exp(dec - m_new)
    s_old = jnp.exp(b_last + m_row - m_new)
    hi_ = lax.broadcasted_iota(jnp.int32, (GW, GW), 0) // HEAD_DIM
    hj_ = lax.broadcasted_iota(jnp.int32, (GW, GW), 1) // HEAD_DIM
    ct_ref[...] = s_old * ct + jnp.where(hi_ == hj_, _dot_tn(kw.astype(BF16), v_b), 0.0)
    n_ref[...] = s_old * n_row + jnp.sum(kw, axis=0, keepdims=True)
    m_ref[...] = m_new

    ones_bd = jnp.where(hi_ == hj_, 1.0, 0.0).astype(BF16)
    ms = _dot_f32_by_exact(h * h, ones_bd) * (1.0 / HEAD_DIM)
    o_ref[0] = jax.nn.sigmoid(o) * (h * lax.rsqrt(ms + NORM_EPS) * g_ref[...])


def _mlstm_mix(p, conv_w, conv_b, ig_b, fg_b, norm_g):
    B, S, _ = p.shape
    GW = GROUP_W
    assert S % ML_CHUNK == 0 and ML_CONV - 1 <= ML_HALO <= ML_CHUNK
    gate_b = jnp.concatenate([jnp.repeat(ig_b, HEAD_DIM), jnp.repeat(fg_b, HEAD_DIM)])[None]
    full = lambda a: pl.BlockSpec(a.shape, lambda b, c: (0,) * a.ndim)
    cb2, g2 = conv_b[None], norm_g[None]
    return pl.pallas_call(
        _mlstm_kernel,
        grid=(B, S // ML_CHUNK),
        in_specs=[pl.BlockSpec((1, ML_CHUNK, C_PAD), lambda b, c: (b, c, 0)),
                  full(conv_w), full(cb2), full(gate_b), full(g2)],
        out_specs=pl.BlockSpec((1, ML_CHUNK, GW), lambda b, c: (b, c, 0)),
        out_shape=jax.ShapeDtypeStruct((B, S, GW), F32),
        scratch_shapes=[pltpu.VMEM((ML_HALO + ML_CHUNK, 2 * GW), F32), pltpu.VMEM((GW, GW), F32),
                        pltpu.VMEM((1, GW), F32), pltpu.VMEM((1, GW), F32)],
        compiler_params=pltpu.CompilerParams(dimension_semantics=("parallel", "arbitrary"),
                                             vmem_limit_bytes=VMEM_LIMIT),
        name="mlstm",
    )(p, conv_w, cb2, gate_b, g2)


def _rms_norm(x, g):
    xf = x.astype(F32)
    y = xf * lax.rsqrt(jnp.mean(xf * xf, -1, keepdims=True) + NORM_EPS)
    return (y * g.astype(F32)).astype(x.dtype)


def _rope(x, pos):
    half = x.shape[-1] // 2
    inv = ROPE_THETA ** (-jnp.arange(half, dtype=F32) / half)
    ang = pos.astype(F32)[:, None] * inv[None, :]
    cos = jnp.cos(ang)[None, :, None, :]
    sin = jnp.sin(ang)[None, :, None, :]
    xf = x.astype(F32)
    x1, x2 = xf[..., :half], xf[..., half:]
    return jnp.concatenate([x1 * cos - x2 * sin, x2 * cos + x1 * sin], -1).astype(x.dtype)


def kernel(x, c, ada_w, ada_b, norm1_g, norm2_g, w_in, rk_mu, rk_w0, rk_w2, rk_a0, rk_a2, rk_g2, rk_kk, rk_ka, rk_rk, rk_ln_g, rk_ln_b, sb_norm_g, ml_conv_w, ml_conv_b, ml_ig_b, ml_fg_b, ml_norm_g, ds_qn_g, ds_kn_g, ds_out_g, w_out, moe_wg, moe_bg, moe_we, moe_be, moe_w1, moe_w3, moe_w2):
    B, S, D = x.shape
    H, d = N_HEADS, HEAD_DIM
    depth = ada_w.shape[0]
    pos = jnp.arange(S)
    c_act = jax.nn.silu(c)
    for l in range(depth):
        mod = (c_act @ ada_w[l] + ada_b[l])[:, None, :]
        sh1, sc1, gt1, sh2, sc2, gt2 = jnp.split(mod, 6, axis=-1)

        pA, pB, pC, pD = _in_proj(x, sc1, sh1, norm1_g[l][None], _pad_w_in(w_in[l]))

        yA = _rwkv7_time_mix(pA, rk_mu[l], rk_w0[l], rk_w2[l], rk_a0[l], rk_a2[l], rk_g2[l],
                             rk_kk[l], rk_ka[l], rk_rk[l], rk_ln_g[l], rk_ln_b[l])

        yB = _stick_breaking_norm(pB, sb_norm_g[l])

        yC = _mlstm_mix(pC, ml_conv_w[l], ml_conv_b[l], ml_ig_b[l], ml_fg_b[l], ml_norm_g[l])

        yD = _dsa_attn_norm(*_dsa_prep(pD, ds_qn_g[l], ds_kn_g[l]), ds_out_g[l])

        router = jnp.pad(jnp.concatenate([moe_wg[l], moe_we[l]], 1),
                         ((0, 0), (0, ROUTER_PAD - N_GROUPS - N_EXPERTS)))
        r_hi, r_lo = _split_bf16(router)
        r_b = jnp.pad(jnp.concatenate([moe_bg[l], moe_be[l]]), (0, ROUTER_PAD - N_GROUPS - N_EXPERTS))[None]
        x1, h2, route = _out_proj((yA, yB, yC, yD), x, gt1, sc2, sh2, norm2_g[l][None],
                                  w_out[l].astype(BF16), r_hi, r_lo, r_b)

        moe = _hier_moe(h2.reshape(B * S, D), route.reshape(B * S, ROUTER_PAD),
                        moe_w1[l], moe_w3[l], moe_w2[l])
        x = x1 + gt2 * moe.reshape(B, S, D)
    return x
```

```python
import functools

import jax
import jax.numpy as jnp
import numpy as np
from jax import lax
from jax.experimental import pallas as pl
from jax.experimental.pallas import tpu as pltpu

F32 = jnp.float32
BF16 = jnp.bfloat16

D_MODEL = 1024
N_MIXERS = 4
GROUP_W = D_MODEL // N_MIXERS
HEAD_DIM = 64
N_HEADS = GROUP_W // HEAD_DIM
NORM_EPS = 1e-6
RWKV_W_RANK = 32
RWKV_A_RANK = 32
RWKV_G_RANK = 64
RWKV_GN_EPS = 64e-5
SB_BLOCK = 128
ML_CHUNK = 64
ML_CONV = 4
GATE_CAP = 15.0
DSA_BLOCK = 128
IDX_HEADS = 4
IDX_DIM = 32
TOPK_MAX = 256
ROPE_THETA = 10000.0
N_GROUPS = 4
EXP_PER_GROUP = 8
N_EXPERTS = N_GROUPS * EXP_PER_GROUP
EXPERT_FF = D_MODEL // 2
TOP_IN_GROUP = 2

A_SIZES = (GROUP_W, GROUP_W, GROUP_W, RWKV_W_RANK, RWKV_A_RANK, RWKV_G_RANK)
B_SIZES = (GROUP_W, GROUP_W, GROUP_W)
C_SIZES = (GROUP_W, GROUP_W, GROUP_W, GROUP_W, N_HEADS, N_HEADS)
D_SIZES = (GROUP_W, HEAD_DIM, HEAD_DIM, IDX_HEADS * IDX_DIM, IDX_DIM, IDX_HEADS)
A_COLS = sum(A_SIZES)
B_COLS = sum(B_SIZES)
C_COLS = sum(C_SIZES)
D_COLS = sum(D_SIZES)

LANE = 128
A_PAD = 896
B_PAD = 768
C_PAD = 1152
D_PAD = 640
P_PAD = A_PAD + B_PAD + C_PAD + D_PAD
ROUTER_PAD = LANE

IN_ROWS = 256
OUT_ROWS = 512
MOE_ROWS = 512
VMEM_LIMIT = 48 * 1024 * 1024


def _split_cols(t, sizes):
    return jnp.split(t, [int(i) for i in np.cumsum(sizes)[:-1]], axis=-1)


def _in_proj_kernel(x_ref, sc_ref, sh_ref, g_ref, w_ref, oa_ref, ob_ref, oc_ref, od_ref):
    x = x_ref[0]
    y = x * lax.rsqrt(jnp.mean(x * x, -1, keepdims=True) + NORM_EPS) * g_ref[...]
    h = y * (1.0 + sc_ref[0]) + sh_ref[0]
    p = jnp.dot(h.astype(BF16), w_ref[...], preferred_element_type=F32)
    oa_ref[0] = p[:, :A_PAD]
    ob_ref[0] = p[:, A_PAD:A_PAD + B_PAD]
    oc_ref[0] = p[:, A_PAD + B_PAD:A_PAD + B_PAD + C_PAD]
    od_ref[0] = p[:, A_PAD + B_PAD + C_PAD:]


def _in_proj(x, sc, sh, g, w_pad):
    B, S, D = x.shape
    row = lambda w: pl.BlockSpec((1, IN_ROWS, w), lambda b, i: (b, i, 0))
    vec = pl.BlockSpec((1, 1, D), lambda b, i: (b, 0, 0))
    return pl.pallas_call(
        _in_proj_kernel,
        grid=(B, S // IN_ROWS),
        in_specs=[row(D), vec, vec, pl.BlockSpec((1, D), lambda b, i: (0, 0)),
                  pl.BlockSpec((D, P_PAD), lambda b, i: (0, 0))],
        out_specs=[row(A_PAD), row(B_PAD), row(C_PAD), row(D_PAD)],
        out_shape=[jax.ShapeDtypeStruct((B, S, w), F32) for w in (A_PAD, B_PAD, C_PAD, D_PAD)],
        compiler_params=pltpu.CompilerParams(dimension_semantics=("parallel", "parallel"),
                                             vmem_limit_bytes=VMEM_LIMIT),
        name="in_proj",
    )(x, sc, sh, g, w_pad)


def _pad_w_in(w):
    wa, wb, wc, wd = _split_cols(w, (A_COLS, B_COLS, C_COLS, D_COLS))
    padc = lambda t, n: jnp.pad(t, ((0, 0), (0, n - t.shape[1])))
    return jnp.concatenate([padc(wa, A_PAD), padc(wb, B_PAD), padc(wc, C_PAD), padc(wd, D_PAD)], 1).astype(BF16)


def _split_bf16(t):
    hi = t.astype(BF16)
    lo = (t - hi.astype(F32)).astype(BF16)
    return hi, lo


def _out_proj_kernel(ya_ref, yb_ref, yc_ref, yd_ref, x_ref, gt_ref, sc_ref, sh_ref, g_ref, w_ref,
                     rhi_ref, rlo_ref, rb_ref, x1_ref, h2_ref, route_ref):
    acc = jnp.zeros(x_ref.shape[1:], F32)
    for n, y_ref in enumerate((ya_ref, yb_ref, yc_ref, yd_ref)):
        acc += jnp.dot(y_ref[0].astype(BF16), w_ref[n * GROUP_W:(n + 1) * GROUP_W, :],
                       preferred_element_type=F32)
    x1 = x_ref[0] + gt_ref[0] * acc
    x1_ref[0] = x1
    y = x1 * lax.rsqrt(jnp.mean(x1 * x1, -1, keepdims=True) + NORM_EPS) * g_ref[...]
    h = y * (1.0 + sc_ref[0]) + sh_ref[0]
    hi, lo = _split_bf16(h)
    h2_ref[0] = hi
    lg = (jnp.dot(hi, rhi_ref[...], preferred_element_type=F32)
          + jnp.dot(lo, rhi_ref[...], preferred_element_type=F32)
          + jnp.dot(hi, rlo_ref[...], preferred_element_type=F32)) + rb_ref[...]
    route_ref[0] = _route(lg)


def _route(lg):
    lane = lax.broadcasted_iota(jnp.int32, lg.shape, 1)
    neg = -jnp.inf
    first = lambda hit: jnp.min(jnp.where(hit, lane, ROUTER_PAD), axis=1, keepdims=True)
    is_grp = lane < N_GROUPS
    grp = jnp.where(is_grp, lg, neg)
    g_max = jnp.max(grp, axis=1, keepdims=True)
    g_p = 1.0 / jnp.sum(jnp.where(is_grp, jnp.exp(grp - g_max), 0.0), axis=1, keepdims=True)
    g_idx = first(grp == g_max)
    e_lane = lane - N_GROUPS
    in_group = (e_lane >= 0) & (e_lane < N_EXPERTS) & (e_lane // EXP_PER_GROUP == g_idx)
    e_log = jnp.where(in_group, lg, neg)
    e1_max = jnp.max(e_log, axis=1, keepdims=True)
    e1_lane = first(e_log == e1_max)
    e_log2 = jnp.where(lane == e1_lane, neg, e_log)
    e2_max = jnp.max(e_log2, axis=1, keepdims=True)
    e2_lane = first(e_log2 == e2_max)
    ratio = jnp.exp(e2_max - e1_max)
    gate1 = g_p / (1.0 + ratio)
    gate2 = gate1 * ratio
    out = jnp.where(lane == 0, (e1_lane - N_GROUPS).astype(F32), 0.0)
    out = jnp.where(lane == 1, (e2_lane - N_GROUPS).astype(F32), out)
    out = jnp.where(lane == 2, gate1, out)
    return jnp.where(lane == 3, gate2, out)


def _out_proj(ys, x, gt, sc, sh, g, w_out, r_hi, r_lo, r_b):
    B, S, D = x.shape
    row = lambda w: pl.BlockSpec((1, OUT_ROWS, w), lambda b, i: (b, i, 0))
    vec = pl.BlockSpec((1, 1, D), lambda b, i: (b, 0, 0))
    full = lambda a: pl.BlockSpec(a.shape, lambda b, i: (0,) * a.ndim)
    return pl.pallas_call(
        _out_proj_kernel,
        grid=(B, S // OUT_ROWS),
        in_specs=[row(GROUP_W)] * 4 + [row(D), vec, vec, vec, full(g), full(w_out), full(r_hi), full(r_lo),
                                        full(r_b)],
        out_specs=[row(D), row(D), row(ROUTER_PAD)],
        out_shape=[jax.ShapeDtypeStruct((B, S, D), F32), jax.ShapeDtypeStruct((B, S, D), BF16),
                   jax.ShapeDtypeStruct((B, S, ROUTER_PAD), F32)],
        compiler_params=pltpu.CompilerParams(dimension_semantics=("parallel", "parallel"),
                                             vmem_limit_bytes=VMEM_LIMIT),
        name="out_proj",
    )(*ys, x, gt, sc, sh, g, w_out, r_hi, r_lo, r_b)


def _moe_ffn_kernel(blk_e_ref, x_ref, wt_ref, w1_ref, w3_ref, w2_ref, o_ref, w1b_ref, w3b_ref, w2b_ref):
    i = pl.program_id(0)
    changed = jnp.logical_or(i == 0, blk_e_ref[i] != blk_e_ref[jnp.maximum(i - 1, 0)])

    @pl.when(changed)
    def _():
        w1b_ref[...] = w1_ref[0].astype(BF16)
        w3b_ref[...] = w3_ref[0].astype(BF16)
        w2b_ref[...] = w2_ref[0].astype(BF16)

    xb = x_ref[...]
    a = jnp.dot(xb, w1b_ref[...], preferred_element_type=F32)
    b = jnp.dot(xb, w3b_ref[...], preferred_element_type=F32)
    hmid = (a * jax.nn.sigmoid(a) * b).astype(BF16)
    y = jnp.dot(hmid, w2b_ref[...], preferred_element_type=F32)
    o_ref[...] = (y * wt_ref[...]).astype(o_ref.dtype)


def _moe_ffn(blk_e, xs, wt, w1, w3, w2):
    n_slots, D = xs.shape
    n_blocks = n_slots // MOE_ROWS
    FF = w1.shape[-1]
    return pl.pallas_call(
        _moe_ffn_kernel,
        grid_spec=pltpu.PrefetchScalarGridSpec(
            num_scalar_prefetch=1,
            grid=(n_blocks,),
            in_specs=[pl.BlockSpec((MOE_ROWS, D), lambda i, e: (i, 0)),
                      pl.BlockSpec((MOE_ROWS, 1), lambda i, e: (i, 0)),
                      pl.BlockSpec((1, D, FF), lambda i, e: (e[i], 0, 0)),
                      pl.BlockSpec((1, D, FF), lambda i, e: (e[i], 0, 0)),
                      pl.BlockSpec((1, FF, D), lambda i, e: (e[i], 0, 0))],
            out_specs=pl.BlockSpec((MOE_ROWS, D), lambda i, e: (i, 0)),
            scratch_shapes=[pltpu.VMEM((D, FF), BF16), pltpu.VMEM((D, FF), BF16), pltpu.VMEM((FF, D), BF16)],
        ),
        out_shape=jax.ShapeDtypeStruct((n_slots, D), BF16),
        compiler_params=pltpu.CompilerParams(dimension_semantics=("arbitrary",),
                                             vmem_limit_bytes=VMEM_LIMIT),
        name="moe_ffn",
    )(blk_e, xs, wt, w1, w3, w2)


def _hier_moe(h2, route, w1, w3, w2):
    N, D = h2.shape
    expert = route[:, :TOP_IN_GROUP].astype(jnp.int32)
    gate = route[:, TOP_IN_GROUP:2 * TOP_IN_GROUP]
    n_asg = N * TOP_IN_GROUP
    flat_e = expert.reshape(n_asg // LANE, LANE)
    order = jnp.argsort(flat_e.reshape(n_asg))
    rank = jnp.argsort(order).astype(jnp.int32)
    counts = jnp.sum(flat_e[None] == jnp.arange(N_EXPERTS)[:, None, None], axis=(1, 2)).astype(jnp.int32)
    start = jnp.cumsum(counts) - counts
    pad_counts = (counts + MOE_ROWS - 1) // MOE_ROWS * MOE_ROWS
    pad_end = jnp.cumsum(pad_counts)
    pad_start = pad_end - pad_counts
    shift = pad_start - start
    asg_shift = jnp.zeros_like(flat_e)
    for e in range(N_EXPERTS):
        asg_shift = jnp.where(flat_e == e, shift[e], asg_shift)
    asg_slot = (rank + asg_shift.reshape(n_asg)).reshape(N, TOP_IN_GROUP)
    n_blocks = n_asg // MOE_ROWS + N_EXPERTS
    blk_start = jnp.arange(n_blocks) * MOE_ROWS
    blk_e = jnp.minimum(jnp.sum(pad_end[None, :] <= blk_start[:, None], 1), N_EXPERTS - 1).astype(jnp.int32)
    blk_pos = blk_start - pad_start[blk_e]
    row = jnp.arange(MOE_ROWS)[None, :]
    slot_real = (blk_pos[:, None] + row < counts[blk_e][:, None]) & (blk_start[:, None] < pad_end[N_EXPERTS - 1])
    slot_src = jnp.where(slot_real, (start[blk_e] + blk_pos)[:, None] + row, 0).reshape(-1)
    slot_asg = order[slot_src]
    slot_real = slot_real.reshape(-1)
    slot_tok = jnp.where(slot_real, slot_asg // TOP_IN_GROUP, 0).astype(jnp.int32)
    slot_w = jnp.where(slot_real, gate.reshape(n_asg)[slot_asg], 0.0)
    yb = _moe_ffn(blk_e, h2[slot_tok], slot_w[:, None], w1, w3, w2)
    return yb[asg_slot[:, 0]].astype(F32) + yb[asg_slot[:, 1]].astype(F32)


INT_MIN = -2 ** 31
DSA_KEY_STEP = 512


def _float_order_key(x):
    bits = pltpu.bitcast(x, jnp.int32)
    bits = jnp.where(x == 0.0, 0, bits)
    return bits ^ ((bits >> 31) & 0x7FFFFFFF)


COL_PART = 64


def _col_reduce(x, reduce):
    part = reduce(x.reshape(x.shape[0] // COL_PART, COL_PART, x.shape[1]), axis=0)
    return reduce(part, axis=0, keepdims=True)


def _col_count(mask):
    return _col_reduce(jnp.where(mask, 1.0, 0.0), jnp.sum)


def _head_block_diag(t, group):
    n_heads = t.shape[0] // group
    row_h = lax.broadcasted_iota(jnp.int32, t.shape, 0) // group
    return jnp.concatenate([jnp.where(row_h == h, t, 0.0) for h in range(n_heads)], axis=1)


def _dsa_block_t(qd_ref, kd4_ref, vdwt_ref, qi_ref, kihi_ref, kilo_ref, wi_ref, g_ref, o_ref, *, kl, n_sel):
    Q = DSA_BLOCK
    q0 = pl.program_id(1) * Q
    w_hi, w_lo = _split_bf16(_head_block_diag(jnp.transpose(qi_ref[0]), IDX_DIM))
    k_hi, k_lo = kihi_ref[0, :kl, :], kilo_ref[0, :kl, :]
    sc = _dot(k_hi, w_hi) + _dot(k_lo, w_hi) + _dot(k_hi, w_lo)
    wit = jnp.transpose(wi_ref[0])
    score = sum(wit[h:h + 1, :] * jnp.maximum(sc[:, h * Q:(h + 1) * Q], 0.0) for h in range(IDX_HEADS))
    kidx = lax.broadcasted_iota(jnp.int32, (kl, Q), 0)
    qpos = q0 + lax.broadcasted_iota(jnp.int32, (kl, Q), 1)
    adm = kidx <= qpos
    key = _float_order_key(jnp.where(adm, score, -jnp.inf))

    def value_bit(it, tau):
        cand = tau | jnp.left_shift(jnp.int32(1), 31 - it)
        return jnp.where(_col_count(key >= (cand ^ INT_MIN)) >= n_sel, cand, tau)

    tau = lax.fori_loop(0, 32, value_bit, jnp.zeros((1, Q), jnp.int32)) ^ INT_MIN
    gt = key > tau
    eq = (key == tau) & adm
    need = n_sel - _col_count(gt)
    n_eq = _col_count(eq)

    def index_bits():
        def index_bit(it, bound):
            cand = bound | jnp.left_shift(jnp.int32(1), 11 - it)
            return jnp.where(_col_count(eq & (kidx < cand)) <= need, cand, bound)
        return lax.fori_loop(0, 12, index_bit, jnp.zeros((1, Q), jnp.int32))

    bound = lax.cond(jnp.max(n_eq - need) > 0.0, index_bits, lambda: jnp.full((1, Q), kl, jnp.int32))
    sel = gt | (eq & (kidx < bound))

    w_att = _head_block_diag(jnp.transpose(qd_ref[0]), HEAD_DIM).astype(BF16)
    lg = _dot(kd4_ref[0, :kl, :], w_att) * HEAD_DIM ** -0.5
    lg = jnp.where(jnp.concatenate([sel] * N_HEADS, axis=1), lg, -jnp.inf)
    p = jnp.exp(lg - _col_reduce(lg, jnp.max))
    out_t = _dot(vdwt_ref[0, :, :kl], p.astype(BF16)) / _col_reduce(p, jnp.sum)
    row_h = lax.broadcasted_iota(jnp.int32, (GROUP_W, Q), 0) // HEAD_DIM
    nat_t = sum(jnp.where(row_h == h, out_t[:, h * Q:(h + 1) * Q], 0.0) for h in range(N_HEADS))
    r = jnp.transpose(nat_t)
    hi_ = lax.broadcasted_iota(jnp.int32, (GROUP_W, GROUP_W), 0) // HEAD_DIM
    hj_ = lax.broadcasted_iota(jnp.int32, (GROUP_W, GROUP_W), 1) // HEAD_DIM
    ms = _dot_f32_by_exact(r * r, jnp.where(hi_ == hj_, 1.0, 0.0).astype(BF16)) * (1.0 / HEAD_DIM)
    o_ref[0] = r * lax.rsqrt(ms + NORM_EPS) * g_ref[...]


def _dsa_kernel(qd_ref, kd4_ref, vdwt_ref, qi_ref, kihi_ref, kilo_ref, wi_ref, g_ref, o_ref, *, kls, n_sel):
    blocks_per_step = DSA_KEY_STEP // DSA_BLOCK
    for j, kl in enumerate(kls):
        @pl.when(pl.program_id(1) // blocks_per_step == j)
        def _():
            _dsa_block_t(qd_ref, kd4_ref, vdwt_ref, qi_ref, kihi_ref, kilo_ref, wi_ref, g_ref, o_ref,
                         kl=kl, n_sel=n_sel)


DSA_PREP_ROWS = 256
D_Q, D_KV, D_QI, D_KW = 0, GROUP_W, GROUP_W + LANE, GROUP_W + 2 * LANE


def _swap_halves(x, half):
    n = x.shape[1]
    lane = lax.broadcasted_iota(jnp.int32, x.shape, 1)
    return jnp.where(lane % (2 * half) < half, pltpu.roll(x, n - half, axis=1), pltpu.roll(x, half, axis=1))


def _dsa_prep_kernel(p_ref, cq_ref, sq_ref, ci_ref, si_ref, gq_ref, gk_ref,
                     qd_ref, kd4_ref, vdwt_ref, qi_ref, kihi_ref, kilo_ref, wi_ref):
    GW = GROUP_W
    x = p_ref[0]
    hi_ = lax.broadcasted_iota(jnp.int32, (GW, GW), 0) // HEAD_DIM
    hj_ = lax.broadcasted_iota(jnp.int32, (GW, GW), 1) // HEAD_DIM
    ones_bd = jnp.where(hi_ == hj_, 1.0, 0.0).astype(BF16)
    q = x[:, D_Q:D_Q + GW]
    q = q * lax.rsqrt(_dot_f32_by_exact(q * q, ones_bd) * (1.0 / HEAD_DIM) + NORM_EPS) * gq_ref[...]
    qd_ref[0] = q * cq_ref[...] + _swap_halves(q, HEAD_DIM // 2) * sq_ref[...]
    kv = x[:, D_KV:D_KV + LANE]
    lane = lax.broadcasted_iota(jnp.int32, kv.shape, 1)
    is_k = lane < HEAD_DIM
    ms = jnp.sum(jnp.where(is_k, kv * kv, 0.0), axis=1, keepdims=True) * (1.0 / HEAD_DIM)
    kn = kv * lax.rsqrt(ms + NORM_EPS) * gk_ref[...]
    kr = kn * cq_ref[:, :LANE] + _swap_halves(kn, HEAD_DIM // 2) * sq_ref[:, :LANE]
    k2 = jnp.where(is_k, kr, pltpu.roll(kr, HEAD_DIM, axis=1))
    kd4_ref[0] = jnp.concatenate([k2] * (GW // LANE), axis=1).astype(BF16)
    v2 = jnp.where(is_k, pltpu.roll(kv, HEAD_DIM, axis=1), kv)
    v2t = jnp.transpose(v2)
    vdwt_ref[0] = jnp.concatenate([v2t] * (GW // LANE), axis=0).astype(BF16)
    qi = x[:, D_QI:D_QI + LANE]
    qi_ref[0] = qi * ci_ref[...] + _swap_halves(qi, IDX_DIM // 2) * si_ref[...]
    kw = x[:, D_KW:D_KW + LANE]
    kir = kw * ci_ref[...] + _swap_halves(kw, IDX_DIM // 2) * si_ref[...]
    ki1 = jnp.where(lane < IDX_DIM, kir, 0.0)
    ki2 = ki1 + pltpu.roll(ki1, IDX_DIM, axis=1)
    ki4 = ki2 + pltpu.roll(ki2, 2 * IDX_DIM, axis=1)
    kihi_ref[0], kilo_ref[0] = _split_bf16(ki4)
    wi_ref[0] = pltpu.roll(kw, LANE - IDX_DIM, axis=1) * (IDX_HEADS ** -0.5 * IDX_DIM ** -0.5)


def _rope_tables(S, dim, width):
    half = dim // 2
    inv = ROPE_THETA ** (-jnp.arange(half, dtype=F32) / half)
    ang = jnp.arange(S, dtype=F32)[:, None] * inv[None, :]
    cos = jnp.tile(jnp.cos(ang), (1, width // half))
    sin = jnp.tile(jnp.concatenate([-jnp.sin(ang), jnp.sin(ang)], axis=1), (1, width // dim))
    return cos, sin


def _dsa_prep(p, qn_g, kn_g):
    B, S, _ = p.shape
    GW, R = GROUP_W, DSA_PREP_ROWS
    cq, sq = _rope_tables(S, HEAD_DIM, GW)
    ci, si = _rope_tables(S, IDX_DIM, LANE)
    gq = jnp.tile(qn_g, N_HEADS)[None]
    gk = jnp.pad(kn_g, (0, LANE - HEAD_DIM))[None]
    rows = lambda w: pl.BlockSpec((1, R, w), lambda b, i: (b, i, 0))
    tab = lambda w: pl.BlockSpec((R, w), lambda b, i: (i, 0))
    cols = lambda r: pl.BlockSpec((1, r, R), lambda b, i: (b, 0, i))
    vec = lambda w: pl.BlockSpec((1, w), lambda b, i: (0, 0))
    return pl.pallas_call(
        _dsa_prep_kernel,
        grid=(B, S // R),
        in_specs=[rows(D_PAD), tab(GW), tab(GW), tab(LANE), tab(LANE), vec(GW), vec(LANE)],
        out_specs=[rows(GW), rows(GW), cols(GW), rows(LANE), rows(LANE), rows(LANE), rows(LANE)],
        out_shape=[jax.ShapeDtypeStruct((B, S, GW), F32), jax.ShapeDtypeStruct((B, S, GW), BF16),
                   jax.ShapeDtypeStruct((B, GW, S), BF16), jax.ShapeDtypeStruct((B, S, LANE), F32),
                   jax.ShapeDtypeStruct((B, S, LANE), BF16), jax.ShapeDtypeStruct((B, S, LANE), BF16),
                   jax.ShapeDtypeStruct((B, S, LANE), F32)],
        compiler_params=pltpu.CompilerParams(dimension_semantics=("parallel", "parallel"),
                                             vmem_limit_bytes=VMEM_LIMIT),
        name="dsa_prep",
    )(p, cq, sq, ci, si, gq, gk)


def _dsa_attn_norm(qd, kd4, vdwt, qi, ki_hi, ki_lo, wi, g):
    B, S, _ = qd.shape
    n_sel = min(TOPK_MAX, S // 4)
    assert S % DSA_KEY_STEP == 0 and n_sel <= DSA_KEY_STEP
    kls = tuple(range(DSA_KEY_STEP, S + 1, DSA_KEY_STEP))
    blk = lambda w: pl.BlockSpec((1, DSA_BLOCK, w), lambda b, i: (b, i, 0))
    per_b = lambda r, c: pl.BlockSpec((1, r, c), lambda b, i: (b, 0, 0))
    return pl.pallas_call(
        functools.partial(_dsa_kernel, kls=kls, n_sel=n_sel),
        grid=(B, S // DSA_BLOCK),
        in_specs=[blk(GROUP_W), per_b(S, GROUP_W), per_b(GROUP_W, S), blk(IDX_HEADS * IDX_DIM),
                  per_b(S, LANE), per_b(S, LANE), blk(LANE), pl.BlockSpec((1, GROUP_W), lambda b, i: (0, 0))],
        out_specs=blk(GROUP_W),
        out_shape=jax.ShapeDtypeStruct((B, S, GROUP_W), F32),
        compiler_params=pltpu.CompilerParams(dimension_semantics=("parallel", "parallel"),
                                             vmem_limit_bytes=VMEM_LIMIT),
        name="dsa_attn",
    )(qd, kd4, vdwt, qi, ki_hi, ki_lo, wi, g[None])


RWKV_CHUNK = 64
RWKV_LOW = RWKV_W_RANK + RWKV_A_RANK + RWKV_G_RANK
RWKV_STACK = N_HEADS * RWKV_CHUNK
RWKV_BATCH = 2


def _dot(a, b):
    return jnp.dot(a, b, preferred_element_type=F32)


def _dot_nt(a, b):
    return lax.dot_general(a, b, (((1,), (1,)), ((), ())), preferred_element_type=F32)


def _dot_tn(a, b):
    return lax.dot_general(a, b, (((0,), (0,)), ((), ())), preferred_element_type=F32)


def _split3_bf16(t):
    p1 = t.astype(BF16)
    r1 = t - p1.astype(F32)
    p2 = r1.astype(BF16)
    p3 = (r1 - p2.astype(F32)).astype(BF16)
    return p1, p2, p3


def _dot_f32_by_exact(a, b_exact):
    return sum(_dot(p, b_exact) for p in _split3_bf16(a))


def _dot_exact_by_f32(a_exact, b):
    return sum(_dot(a_exact, p) for p in _split3_bf16(b))


def _dot3(a, b_hi, b_lo):
    a_hi, a_lo = _split_bf16(a)
    return _dot(a_hi, b_hi) + _dot(a_lo, b_hi) + _dot(a_hi, b_lo)


def _softplus(z):
    return jnp.maximum(z, 0.0) + jnp.log(1.0 + jnp.exp(-jnp.abs(z)))


def _rwkv_kernel(p_ref, mu_ref, vec_ref, lhi_ref, llo_ref, o_ref, state_ref, prev_ref):
    @pl.when(pl.program_id(1) == 0)
    def _():
        state_ref[...] = jnp.zeros_like(state_ref)
        prev_ref[...] = jnp.zeros_like(prev_ref)

    for n in range(RWKV_BATCH):
        _rwkv_chunk(p_ref.at[n], mu_ref, vec_ref, lhi_ref, llo_ref, o_ref.at[n], state_ref.at[n], prev_ref.at[n])


def _rwkv_chunk(p_ref, mu_ref, vec_ref, lhi_ref, llo_ref, o_ref, state_ref, prev_ref):
    L, GW, ST = RWKV_CHUNK, GROUP_W, RWKV_STACK
    p = p_ref[...]
    row = lax.broadcasted_iota(jnp.int32, p.shape, 0)
    prev = jnp.where(row == 0, prev_ref[...], pltpu.roll(p, 1, axis=0))
    prev_ref[...] = p[L - 1:L, :]
    ps = p + (prev - p) * mu_ref[...]
    r, k, v = ps[:, :GW], ps[:, GW:2 * GW], ps[:, 2 * GW:3 * GW]
    low = ps[:, 3 * GW:]
    lane_low = lax.broadcasted_iota(jnp.int32, low.shape, 1)
    low = jnp.where(lane_low < RWKV_W_RANK, jnp.tanh(low),
                    jnp.where(lane_low < RWKV_W_RANK + RWKV_A_RANK, low, jax.nn.sigmoid(low)))
    up = _dot3(low, lhi_ref[...], llo_ref[...])
    w0, a0, k_k, k_a = vec_ref[0:1, :], vec_ref[1:2, :], vec_ref[2:3, :], vec_ref[3:4, :]
    r_k, ln_g, ln_b = vec_ref[4:5, :], vec_ref[5:6, :], vec_ref[6:7, :]
    logw = -jnp.exp(-_softplus(-(w0 + up[:, :GW])) - 0.5)
    rate = jax.nn.sigmoid(a0 + up[:, GW:2 * GW])
    gate = up[:, 2 * GW:]

    ri = lax.broadcasted_iota(jnp.int32, (ST, ST), 0)
    ci = lax.broadcasted_iota(jnp.int32, (ST, ST), 1)
    same_head = (ri // L) == (ci // L)
    ones_bd = jnp.where(same_head, 1.0, 0.0).astype(BF16)

    kk = k * k_k
    kk = kk / jnp.maximum(jnp.sqrt(_dot_f32_by_exact(kk * kk, ones_bd)), 1e-12)
    k = k * (1.0 + (rate - 1.0) * k_a)

    ti = lax.broadcasted_iota(jnp.int32, (L, L), 0)
    tj = lax.broadcasted_iota(jnp.int32, (L, L), 1)
    lc = _dot_exact_by_f32(jnp.where(tj <= ti, 1.0, 0.0).astype(BF16), logw)
    lc_last = lc[L - 1:L, :]
    dec_in = jnp.exp(lc)
    dec_out = jnp.exp(-lc)
    a_t = -kk * jnp.exp(lc - logw)
    b_t = kk * rate * dec_out
    k_t = k * dec_out
    r_t = r * dec_in
    to_end = jnp.exp(lc_last)

    stack = lambda t: jnp.concatenate([t] * N_HEADS, axis=0)
    bd = lambda t: jnp.where(same_head, stack(t), 0.0).astype(BF16)
    a_bd, r_bd, v_bd = bd(a_t), bd(r_t), bd(v)
    m = _dot_nt(jnp.concatenate([a_bd, r_bd], axis=0),
                jnp.concatenate([stack(b_t), stack(k_t)], axis=0).astype(BF16))
    strict = same_head & ((ci % L) < (ri % L))
    incl = same_head & ((ci % L) <= (ri % L))
    m_ab = jnp.where(strict, m[:ST, :ST], 0.0)
    m_ak = jnp.where(strict, m[:ST, ST:], 0.0).astype(BF16)
    m_rb = jnp.where(incl, m[ST:, :ST], 0.0).astype(BF16)
    m_rk = jnp.where(incl, m[ST:, ST:], 0.0).astype(BF16)

    inv = jnp.where(ri == ci, 1.0, 0.0) + m_ab
    pw = m_ab
    n_doublings = RWKV_CHUNK.bit_length() - 2
    for s in range(n_doublings):
        pw_b = pw.astype(BF16)
        pw = _dot(pw_b, pw_b)
        inv = inv + _dot(inv.astype(BF16), pw.astype(BF16))

    t0 = state_ref[...]
    t0_b = t0.astype(BF16)
    u = _dot(inv.astype(BF16), (_dot(a_bd, t0_b) + _dot(m_ak, v_bd)).astype(BF16)).astype(BF16)
    y_bd = _dot(r_bd, t0_b) + _dot(m_rb, u) + _dot(m_rk, v_bd)
    y = sum(y_bd[h * L:(h + 1) * L, :] for h in range(N_HEADS))

    to_end_col = jnp.sum(jnp.where(ri == ci, jnp.broadcast_to(to_end, (ST, ST)), 0.0), axis=1, keepdims=True)
    state_ref[...] = (to_end_col * t0 + _dot_tn(bd(b_t * to_end), u) + _dot_tn(bd(k_t * to_end), v_bd))

    inv_d = 1.0 / HEAD_DIM
    mean = _dot_f32_by_exact(y, ones_bd) * inv_d
    yc = y - mean
    var = _dot_f32_by_exact(yc * yc, ones_bd) * inv_d
    yn = yc * lax.rsqrt(var + RWKV_GN_EPS) * ln_g + ln_b
    bonus = _dot_f32_by_exact(r * k * r_k, ones_bd) * v
    o_ref[...] = (yn + bonus) * gate


def _rwkv7_time_mix(p, mu, w0, w2, a0, a2, g2, k_k, k_a, r_k, ln_g, ln_b):
    B, S, _ = p.shape
    GW = GROUP_W
    assert S % RWKV_CHUNK == 0 and RWKV_STACK == GW and RWKV_CHUNK == HEAD_DIM and B % RWKV_BATCH == 0
    low_w = jnp.zeros((RWKV_LOW, 3 * GW), F32)
    low_w = low_w.at[:RWKV_W_RANK, :GW].set(w2)
    low_w = low_w.at[RWKV_W_RANK:RWKV_W_RANK + RWKV_A_RANK, GW:2 * GW].set(a2)
    low_w = low_w.at[RWKV_W_RANK + RWKV_A_RANK:, 2 * GW:].set(g2)
    l_hi, l_lo = _split_bf16(low_w)
    vecs = jnp.stack([w0, a0, k_k, k_a, r_k, ln_g, ln_b, jnp.zeros_like(w0)], 0)
    full = lambda a: pl.BlockSpec(a.shape, lambda b, c: (0,) * a.ndim)
    mu2 = mu[None]
    return pl.pallas_call(
        _rwkv_kernel,
        grid=(B // RWKV_BATCH, S // RWKV_CHUNK),
        in_specs=[pl.BlockSpec((RWKV_BATCH, RWKV_CHUNK, A_PAD), lambda b, c: (b, c, 0)),
                  full(mu2), full(vecs), full(l_hi), full(l_lo)],
        out_specs=pl.BlockSpec((RWKV_BATCH, RWKV_CHUNK, GW), lambda b, c: (b, c, 0)),
        out_shape=jax.ShapeDtypeStruct((B, S, GW), F32),
        scratch_shapes=[pltpu.VMEM((RWKV_BATCH, RWKV_STACK, GW), F32), pltpu.VMEM((RWKV_BATCH, 1, A_PAD), F32)],
        compiler_params=pltpu.CompilerParams(dimension_semantics=("parallel", "arbitrary"),
                                             vmem_limit_bytes=VMEM_LIMIT),
        name="rwkv7",
    )(p, mu2, vecs, l_hi, l_lo)


def _sb_kernel(q_ref, k_ref, v_ref, g_ref, o_ref, kbd_ref, vbd_ref):
    i = pl.program_id(1)
    T, GW, H = SB_BLOCK, GROUP_W, N_HEADS
    lane_h = lax.broadcasted_iota(jnp.int32, (T, GW), 1) // HEAD_DIM
    k_new, v_new = k_ref[0], v_ref[0]
    for h in range(H):
        kbd_ref[i, h * T:(h + 1) * T, :] = jnp.where(lane_h == h, k_new, 0.0).astype(BF16)
        vbd_ref[i, h * T:(h + 1) * T, :] = jnp.where(lane_h == h, v_new, 0.0).astype(BF16)

    q = q_ref[0].astype(BF16)
    si = lax.broadcasted_iota(jnp.int32, (T, 2 * T), 0)
    sj = lax.broadcasted_iota(jnp.int32, (T, 2 * T), 1)
    later_and_all = jnp.where((si > sj) | (sj >= T), 1.0, 0.0).astype(BF16)
    qrow = lax.broadcasted_iota(jnp.int32, (T, H * T), 0)
    kcol = lax.broadcasted_iota(jnp.int32, (T, H * T), 1) % T
    causal = kcol < qrow

    def key_block(j, state, diagonal):
        carry, acc = state
        z = _dot_nt(q, kbd_ref[j]) * HEAD_DIM ** -0.5
        soft = jnp.log(1.0 + jnp.exp(-jnp.abs(z)))
        log1m = -(jnp.maximum(z, 0.0) + soft)
        log_sig = jnp.minimum(z, 0.0) - soft
        log1m_in = (jnp.where(causal, log1m, 0.0) if diagonal else log1m).astype(BF16)
        sums = [_dot(log1m_in[:, h * T:(h + 1) * T], later_and_all) for h in range(H)]
        suffix = jnp.concatenate([s[:, :T] for s in sums], axis=1)
        total = jnp.concatenate([s[:, T:] for s in sums], axis=1)
        att = jnp.exp(log_sig + suffix + carry)
        if diagonal:
            att = jnp.where(causal, att, 0.0)
        return carry + total, acc + _dot(att.astype(BF16), vbd_ref[j])

    state = key_block(i, (jnp.zeros((T, H * T), F32), jnp.zeros((T, GW), F32)), True)
    odd = i % 2
    state = lax.fori_loop(0, odd, lambda it, st: key_block(i - 1, st, False), state)
    top = i - 1 - odd

    def two_blocks(it, st):
        return key_block(top - 2 * it - 1, key_block(top - 2 * it, st, False), False)

    _, y = lax.fori_loop(0, i // 2, two_blocks, state)

    hi_ = lax.broadcasted_iota(jnp.int32, (GW, GW), 0) // HEAD_DIM
    hj_ = lax.broadcasted_iota(jnp.int32, (GW, GW), 1) // HEAD_DIM
    ones_bd = jnp.where(hi_ == hj_, 1.0, 0.0).astype(BF16)
    ms = _dot_f32_by_exact(y * y, ones_bd) * (1.0 / HEAD_DIM)
    o_ref[0] = y * lax.rsqrt(ms + NORM_EPS) * g_ref[...]


def _stick_breaking_norm(p, g):
    B, S, _ = p.shape
    GW = GROUP_W
    assert S % SB_BLOCK == 0
    col = lambda n: pl.BlockSpec((1, SB_BLOCK, GW), lambda b, i: (b, i, n))
    blk = col(0)
    q = k = v = p
    stacked = pltpu.VMEM((S // SB_BLOCK, N_HEADS * SB_BLOCK, GW), BF16)
    return pl.pallas_call(
        _sb_kernel,
        grid=(B, S // SB_BLOCK),
        in_specs=[col(0), col(1), col(2), pl.BlockSpec((1, GW), lambda b, i: (0, 0))],
        out_specs=blk,
        out_shape=jax.ShapeDtypeStruct((B, S, GW), F32),
        scratch_shapes=[stacked, stacked],
        compiler_params=pltpu.CompilerParams(dimension_semantics=("parallel", "arbitrary"),
                                             vmem_limit_bytes=VMEM_LIMIT),
        name="stick_breaking",
    )(q, k, v, g[None])


ML_HALO = 8


def _dot_nt_exact_by_f32(a_exact, b):
    return sum(_dot_nt(a_exact, p) for p in _split3_bf16(b))


def _mlstm_kernel(p_ref, cw_ref, cb_ref, gb_ref, g_ref, o_ref, ext_ref, ct_ref, n_ref, m_ref):
    L, GW, H = ML_CHUNK, GROUP_W, N_HEADS
    ST = H * L

    @pl.when(pl.program_id(1) == 0)
    def _():
        ext_ref[...] = jnp.zeros_like(ext_ref)
        ct_ref[...] = jnp.zeros_like(ct_ref)
        n_ref[...] = jnp.zeros_like(n_ref)
        m_ref[...] = jnp.zeros_like(m_ref)

    x = p_ref[0]
    ext_ref[ML_HALO:, :] = x[:, :2 * GW]
    conv = cb_ref[...]
    for j in range(ML_CONV):
        conv = conv + cw_ref[j:j + 1, :] * ext_ref[pl.ds(ML_HALO - (ML_CONV - 1) + j, L), :]
    ext_ref[:ML_HALO, :] = x[L - ML_HALO:, :2 * GW]
    qk = conv * jax.nn.sigmoid(conv)
    q, k = qk[:, :GW], qk[:, GW:] * HEAD_DIM ** -0.5
    v, o = x[:, 2 * GW:3 * GW], x[:, 3 * GW:4 * GW]

    gates = x[:, 4 * GW:]
    gi = lax.broadcasted_iota(jnp.int32, (LANE, 2 * GW), 0)
    gj = lax.broadcasted_iota(jnp.int32, (LANE, 2 * GW), 1)
    expand = jnp.where(gi == (gj % GW) // HEAD_DIM + H * (gj // GW), 1.0, 0.0).astype(BF16)
    graw = _dot_f32_by_exact(gates, expand) + gb_ref[...]
    capped = GATE_CAP * jnp.tanh(graw * (1.0 / GATE_CAP))
    log_i = capped[:, :GW]
    cf = capped[:, GW:]
    log_f = jnp.minimum(cf, 0.0) - jnp.log(1.0 + jnp.exp(-jnp.abs(cf)))

    ti = lax.broadcasted_iota(jnp.int32, (L, L), 0)
    tj = lax.broadcasted_iota(jnp.int32, (L, L), 1)
    bf = _dot_exact_by_f32(jnp.where(tj <= ti, 1.0, 0.0).astype(BF16), log_f)
    b_last = bf[L - 1:L, :]
    m_row, n_row, ct = m_ref[...], n_ref[...], ct_ref[...]

    ri = lax.broadcasted_iota(jnp.int32, (ST, GW), 0)
    ci = lax.broadcasted_iota(jnp.int32, (ST, GW), 1)
    same_head = (ri // L) == (ci // HEAD_DIM)
    first_lane = ci == (ri // L) * HEAD_DIM
    stack = lambda t: jnp.concatenate([t] * H, axis=0)
    pick = lambda t: jnp.sum(jnp.where(first_lane, t, 0.0), axis=1, keepdims=True)

    qs = jnp.where(same_head, stack(q), 0.0)
    qs_b = qs.astype(BF16)
    v_b = v.astype(BF16)
    b_col = pick(stack(bf))
    g_col = pick(stack(bf + m_row))
    row_part = _dot_nt_exact_by_f32(jnp.where(first_lane, 1.0, 0.0).astype(BF16), log_i - bf)
    rt = lax.broadcasted_iota(jnp.int32, (ST, L), 0) % L
    rs = lax.broadcasted_iota(jnp.int32, (ST, L), 1)
    dmat = jnp.where(rs <= rt, b_col + row_part, -jnp.inf)
    m_t = jnp.maximum(g_col, jnp.max(dmat, axis=1, keepdims=True))
    s_inter = jnp.exp(g_col - m_t)
    sqk = _dot_nt(qs_b, k.astype(BF16)) * jnp.exp(dmat - m_t)
    num = s_inter * _dot(qs_b, ct.astype(BF16)) + jnp.where(same_head, _dot(sqk.astype(BF16), v_b), 0.0)
    den = s_inter * jnp.sum(qs * n_row, axis=1, keepdims=True) + jnp.sum(sqk, axis=1, keepdims=True)
    hst = num / jnp.maximum(jnp.abs(den), jnp.exp(-m_t))
    h = sum(hst[n * L:(n + 1) * L, :] for n in range(H))

    dec = b_last - bf + log_i
    m_new = jnp.maximum(b_last + m_row, jnp.max(dec, axis=0, keepdims=True))
    kw = k * jnp.exp(dec - m_new)
    s_old = jnp.exp(b_last + m_row - m_new)
    hi_ = lax.broadcasted_iota(jnp.int32, (GW, GW), 0) // HEAD_DIM
    hj_ = lax.broadcasted_iota(jnp.int32, (GW, GW), 1) // HEAD_DIM
    ct_ref[...] = s_old * ct + jnp.where(hi_ == hj_, _dot_tn(kw.astype(BF16), v_b), 0.0)
    n_ref[...] = s_old * n_row + jnp.sum(kw, axis=0, keepdims=True)
    m_ref[...] = m_new

    ones_bd = jnp.where(hi_ == hj_, 1.0, 0.0).astype(BF16)
    ms = _dot_f32_by_exact(h * h, ones_bd) * (1.0 / HEAD_DIM)
    o_ref[0] = jax.nn.sigmoid(o) * (h * lax.rsqrt(ms + NORM_EPS) * g_ref[...])


def _mlstm_mix(p, conv_w, conv_b, ig_b, fg_b, norm_g):
    B, S, _ = p.shape
    GW = GROUP_W
    assert S % ML_CHUNK == 0 and ML_CONV - 1 <= ML_HALO <= ML_CHUNK
    gate_b = jnp.concatenate([jnp.repeat(ig_b, HEAD_DIM), jnp.repeat(fg_b, HEAD_DIM)])[None]
    full = lambda a: pl.BlockSpec(a.shape, lambda b, c: (0,) * a.ndim)
    cb2, g2 = conv_b[None], norm_g[None]
    return pl.pallas_call(
        _mlstm_kernel,
        grid=(B, S // ML_CHUNK),
        in_specs=[pl.BlockSpec((1, ML_CHUNK, C_PAD), lambda b, c: (b, c, 0)),
                  full(conv_w), full(cb2), full(gate_b), full(g2)],
        out_specs=pl.BlockSpec((1, ML_CHUNK, GW), lambda b, c: (b, c, 0)),
        out_shape=jax.ShapeDtypeStruct((B, S, GW), F32),
        scratch_shapes=[pltpu.VMEM((ML_HALO + ML_CHUNK, 2 * GW), F32), pltpu.VMEM((GW, GW), F32),
                        pltpu.VMEM((1, GW), F32), pltpu.VMEM((1, GW), F32)],
        compiler_params=pltpu.CompilerParams(dimension_semantics=("parallel", "arbitrary"),
                                             vmem_limit_bytes=VMEM_LIMIT),
        name="mlstm",
    )(p, conv_w, cb2, gate_b, g2)


def _rms_norm(x, g):
    xf = x.astype(F32)
    y = xf * lax.rsqrt(jnp.mean(xf * xf, -1, keepdims=True) + NORM_EPS)
    return (y * g.astype(F32)).astype(x.dtype)


def _rope(x, pos):
    half = x.shape[-1] // 2
    inv = ROPE_THETA ** (-jnp.arange(half, dtype=F32) / half)
    ang = pos.astype(F32)[:, None] * inv[None, :]
    cos = jnp.cos(ang)[None, :, None, :]
    sin = jnp.sin(ang)[None, :, None, :]
    xf = x.astype(F32)
    x1, x2 = xf[..., :half], xf[..., half:]
    return jnp.concatenate([x1 * cos - x2 * sin, x2 * cos + x1 * sin], -1).astype(x.dtype)


def kernel(x, c, ada_w, ada_b, norm1_g, norm2_g, w_in, rk_mu, rk_w0, rk_w2, rk_a0, rk_a2, rk_g2, rk_kk, rk_ka, rk_rk, rk_ln_g, rk_ln_b, sb_norm_g, ml_conv_w, ml_conv_b, ml_ig_b, ml_fg_b, ml_norm_g, ds_qn_g, ds_kn_g, ds_out_g, w_out, moe_wg, moe_bg, moe_we, moe_be, moe_w1, moe_w3, moe_w2):
    B, S, D = x.shape
    H, d = N_HEADS, HEAD_DIM
    depth = ada_w.shape[0]
    pos = jnp.arange(S)
    c_act = jax.nn.silu(c)
    for l in range(depth):
        mod = (c_act @ ada_w[l] + ada_b[l])[:, None, :]
        sh1, sc1, gt1, sh2, sc2, gt2 = jnp.split(mod, 6, axis=-1)

        pA, pB, pC, pD = _in_proj(x, sc1, sh1, norm1_g[l][None], _pad_w_in(w_in[l]))

        yA = _rwkv7_time_mix(pA, rk_mu[l], rk_w0[l], rk_w2[l], rk_a0[l], rk_a2[l], rk_g2[l],
                             rk_kk[l], rk_ka[l], rk_rk[l], rk_ln_g[l], rk_ln_b[l])

        yB = _stick_breaking_norm(pB, sb_norm_g[l])

        yC = _mlstm_mix(pC, ml_conv_w[l], ml_conv_b[l], ml_ig_b[l], ml_fg_b[l], ml_norm_g[l])

        yD = _dsa_attn_norm(*_dsa_prep(pD, ds_qn_g[l], ds_kn_g[l]), ds_out_g[l])

        router = jnp.pad(jnp.concatenate([moe_wg[l], moe_we[l]], 1),
                         ((0, 0), (0, ROUTER_PAD - N_GROUPS - N_EXPERTS)))
        r_hi, r_lo = _split_bf16(router)
        r_b = jnp.pad(jnp.concatenate([moe_bg[l], moe_be[l]]), (0, ROUTER_PAD - N_GROUPS - N_EXPERTS))[None]
        x1, h2, route = _out_proj((yA, yB, yC, yD), x, gt1, sc2, sh2, norm2_g[l][None],
                                  w_out[l].astype(BF16), r_hi, r_lo, r_b)

        moe = _hier_moe(h2.reshape(B * S, D), route.reshape(B * S, ROUTER_PAD),
                        moe_w1[l], moe_w3[l], moe_w2[l])
        x = x1 + gt2 * moe.reshape(B, S, D)
    return x
```

```python
import functools

import jax
import jax.numpy as jnp
import numpy as np
from jax import lax
from jax.experimental import pallas as pl
from jax.experimental.pallas import tpu as pltpu

F32 = jnp.float32
BF16 = jnp.bfloat16

D_MODEL = 1024
N_MIXERS = 4
GROUP_W = D_MODEL // N_MIXERS
HEAD_DIM = 64
N_HEADS = GROUP_W // HEAD_DIM
NORM_EPS = 1e-6
RWKV_W_RANK = 32
RWKV_A_RANK = 32
RWKV_G_RANK = 64
RWKV_GN_EPS = 64e-5
SB_BLOCK = 128
ML_CHUNK = 64
ML_CONV = 4
GATE_CAP = 15.0
DSA_BLOCK = 128
IDX_HEADS = 4
IDX_DIM = 32
TOPK_MAX = 256
ROPE_THETA = 10000.0
N_GROUPS = 4
EXP_PER_GROUP = 8
N_EXPERTS = N_GROUPS * EXP_PER_GROUP
EXPERT_FF = D_MODEL // 2
TOP_IN_GROUP = 2

A_SIZES = (GROUP_W, GROUP_W, GROUP_W, RWKV_W_RANK, RWKV_A_RANK, RWKV_G_RANK)
B_SIZES = (GROUP_W, GROUP_W, GROUP_W)
C_SIZES = (GROUP_W, GROUP_W, GROUP_W, GROUP_W, N_HEADS, N_HEADS)
D_SIZES = (GROUP_W, HEAD_DIM, HEAD_DIM, IDX_HEADS * IDX_DIM, IDX_DIM, IDX_HEADS)
A_COLS = sum(A_SIZES)
B_COLS = sum(B_SIZES)
C_COLS = sum(C_SIZES)
D_COLS = sum(D_SIZES)

LANE = 128
A_PAD = 896
B_PAD = 768
C_PAD = 1152
D_PAD = 640
P_PAD = A_PAD + B_PAD + C_PAD + D_PAD
ROUTER_PAD = LANE

IN_ROWS = 256
OUT_ROWS = 512
MOE_ROWS = 512
VMEM_LIMIT = 48 * 1024 * 1024


def _split_cols(t, sizes):
    return jnp.split(t, [int(i) for i in np.cumsum(sizes)[:-1]], axis=-1)


def _in_proj_kernel(x_ref, sc_ref, sh_ref, g_ref, w_ref, oa_ref, ob_ref, oc_ref, od_ref):
    x = x_ref[0]
    y = x * lax.rsqrt(jnp.mean(x * x, -1, keepdims=True) + NORM_EPS) * g_ref[...]
    h = y * (1.0 + sc_ref[0]) + sh_ref[0]
    p = jnp.dot(h.astype(BF16), w_ref[...], preferred_element_type=F32)
    oa_ref[0] = p[:, :A_PAD]
    ob_ref[0] = p[:, A_PAD:A_PAD + B_PAD]
    oc_ref[0] = p[:, A_PAD + B_PAD:A_PAD + B_PAD + C_PAD]
    od_ref[0] = p[:, A_PAD + B_PAD + C_PAD:]


def _in_proj(x, sc, sh, g, w_pad):
    B, S, D = x.shape
    row = lambda w: pl.BlockSpec((1, IN_ROWS, w), lambda b, i: (b, i, 0))
    vec = pl.BlockSpec((1, 1, D), lambda b, i: (b, 0, 0))
    return pl.pallas_call(
        _in_proj_kernel,
        grid=(B, S // IN_ROWS),
        in_specs=[row(D), vec, vec, pl.BlockSpec((1, D), lambda b, i: (0, 0)),
                  pl.BlockSpec((D, P_PAD), lambda b, i: (0, 0))],
        out_specs=[row(A_PAD), row(B_PAD), row(C_PAD), row(D_PAD)],
        out_shape=[jax.ShapeDtypeStruct((B, S, w), F32) for w in (A_PAD, B_PAD, C_PAD, D_PAD)],
        compiler_params=pltpu.CompilerParams(dimension_semantics=("parallel", "parallel"),
                                             vmem_limit_bytes=VMEM_LIMIT),
        name="in_proj",
    )(x, sc, sh, g, w_pad)


def _pad_w_in(w):
    wa, wb, wc, wd = _split_cols(w, (A_COLS, B_COLS, C_COLS, D_COLS))
    padc = lambda t, n: jnp.pad(t, ((0, 0), (0, n - t.shape[1])))
    return jnp.concatenate([padc(wa, A_PAD), padc(wb, B_PAD), padc(wc, C_PAD), padc(wd, D_PAD)], 1).astype(BF16)


def _split_bf16(t):
    hi = t.astype(BF16)
    lo = (t - hi.astype(F32)).astype(BF16)
    return hi, lo


def _out_proj_kernel(ya_ref, yb_ref, yc_ref, yd_ref, x_ref, gt_ref, sc_ref, sh_ref, g_ref, w_ref,
                     rhi_ref, rlo_ref, rb_ref, x1_ref, h2_ref, route_ref):
    acc = jnp.zeros(x_ref.shape[1:], F32)
    for n, y_ref in enumerate((ya_ref, yb_ref, yc_ref, yd_ref)):
        acc += jnp.dot(y_ref[0].astype(BF16), w_ref[n * GROUP_W:(n + 1) * GROUP_W, :],
                       preferred_element_type=F32)
    x1 = x_ref[0] + gt_ref[0] * acc
    x1_ref[0] = x1
    y = x1 * lax.rsqrt(jnp.mean(x1 * x1, -1, keepdims=True) + NORM_EPS) * g_ref[...]
    h = y * (1.0 + sc_ref[0]) + sh_ref[0]
    hi, lo = _split_bf16(h)
    h2_ref[0] = hi
    lg = (jnp.dot(hi, rhi_ref[...], preferred_element_type=F32)
          + jnp.dot(lo, rhi_ref[...], preferred_element_type=F32)
          + jnp.dot(hi, rlo_ref[...], preferred_element_type=F32)) + rb_ref[...]
    route_ref[0] = _route(lg)


def _route(lg):
    lane = lax.broadcasted_iota(jnp.int32, lg.shape, 1)
    neg = -jnp.inf
    first = lambda hit: jnp.min(jnp.where(hit, lane, ROUTER_PAD), axis=1, keepdims=True)
    is_grp = lane < N_GROUPS
    grp = jnp.where(is_grp, lg, neg)
    g_max = jnp.max(grp, axis=1, keepdims=True)
    g_p = 1.0 / jnp.sum(jnp.where(is_grp, jnp.exp(grp - g_max), 0.0), axis=1, keepdims=True)
    g_idx = first(grp == g_max)
    e_lane = lane - N_GROUPS
    in_group = (e_lane >= 0) & (e_lane < N_EXPERTS) & (e_lane // EXP_PER_GROUP == g_idx)
    e_log = jnp.where(in_group, lg, neg)
    e1_max = jnp.max(e_log, axis=1, keepdims=True)
    e1_lane = first(e_log == e1_max)
    e_log2 = jnp.where(lane == e1_lane, neg, e_log)
    e2_max = jnp.max(e_log2, axis=1, keepdims=True)
    e2_lane = first(e_log2 == e2_max)
    ratio = jnp.exp(e2_max - e1_max)
    gate1 = g_p / (1.0 + ratio)
    gate2 = gate1 * ratio
    out = jnp.where(lane == 0, (e1_lane - N_GROUPS).astype(F32), 0.0)
    out = jnp.where(lane == 1, (e2_lane - N_GROUPS).astype(F32), out)
    out = jnp.where(lane == 2, gate1, out)
    return jnp.where(lane == 3, gate2, out)


def _out_proj(ys, x, gt, sc, sh, g, w_out, r_hi, r_lo, r_b):
    B, S, D = x.shape
    row = lambda w: pl.BlockSpec((1, OUT_ROWS, w), lambda b, i: (b, i, 0))
    vec = pl.BlockSpec((1, 1, D), lambda b, i: (b, 0, 0))
    full = lambda a: pl.BlockSpec(a.shape, lambda b, i: (0,) * a.ndim)
    return pl.pallas_call(
        _out_proj_kernel,
        grid=(B, S // OUT_ROWS),
        in_specs=[row(GROUP_W)] * 4 + [row(D), vec, vec, vec, full(g), full(w_out), full(r_hi), full(r_lo),
                                        full(r_b)],
        out_specs=[row(D), row(D), row(ROUTER_PAD)],
        out_shape=[jax.ShapeDtypeStruct((B, S, D), F32), jax.ShapeDtypeStruct((B, S, D), BF16),
                   jax.ShapeDtypeStruct((B, S, ROUTER_PAD), F32)],
        compiler_params=pltpu.CompilerParams(dimension_semantics=("parallel", "parallel"),
                                             vmem_limit_bytes=VMEM_LIMIT),
        name="out_proj",
    )(*ys, x, gt, sc, sh, g, w_out, r_hi, r_lo, r_b)


def _moe_ffn_kernel(blk_e_ref, x_ref, wt_ref, w1_ref, w3_ref, w2_ref, o_ref, w1b_ref, w3b_ref, w2b_ref):
    i = pl.program_id(0)
    changed = jnp.logical_or(i == 0, blk_e_ref[i] != blk_e_ref[jnp.maximum(i - 1, 0)])

    @pl.when(changed)
    def _():
        w1b_ref[...] = w1_ref[0].astype(BF16)
        w3b_ref[...] = w3_ref[0].astype(BF16)
        w2b_ref[...] = w2_ref[0].astype(BF16)

    xb = x_ref[...]
    a = jnp.dot(xb, w1b_ref[...], preferred_element_type=F32)
    b = jnp.dot(xb, w3b_ref[...], preferred_element_type=F32)
    hmid = (a * jax.nn.sigmoid(a) * b).astype(BF16)
    y = jnp.dot(hmid, w2b_ref[...], preferred_element_type=F32)
    o_ref[...] = (y * wt_ref[...]).astype(o_ref.dtype)


def _moe_ffn(blk_e, xs, wt, w1, w3, w2):
    n_slots, D = xs.shape
    n_blocks = n_slots // MOE_ROWS
    FF = w1.shape[-1]
    return pl.pallas_call(
        _moe_ffn_kernel,
        grid_spec=pltpu.PrefetchScalarGridSpec(
            num_scalar_prefetch=1,
            grid=(n_blocks,),
            in_specs=[pl.BlockSpec((MOE_ROWS, D), lambda i, e: (i, 0)),
                      pl.BlockSpec((MOE_ROWS, 1), lambda i, e: (i, 0)),
                      pl.BlockSpec((1, D, FF), lambda i, e: (e[i], 0, 0)),
                      pl.BlockSpec((1, D, FF), lambda i, e: (e[i], 0, 0)),
                      pl.BlockSpec((1, FF, D), lambda i, e: (e[i], 0, 0))],
            out_specs=pl.BlockSpec((MOE_ROWS, D), lambda i, e: (i, 0)),
            scratch_shapes=[pltpu.VMEM((D, FF), BF16), pltpu.VMEM((D, FF), BF16), pltpu.VMEM((FF, D), BF16)],
        ),
        out_shape=jax.ShapeDtypeStruct((n_slots, D), BF16),
        compiler_params=pltpu.CompilerParams(dimension_semantics=("arbitrary",),
                                             vmem_limit_bytes=VMEM_LIMIT),
        name="moe_ffn",
    )(blk_e, xs, wt, w1, w3, w2)


def _hier_moe(h2, route, w1, w3, w2):
    N, D = h2.shape
    expert = route[:, :TOP_IN_GROUP].astype(jnp.int32)
    gate = route[:, TOP_IN_GROUP:2 * TOP_IN_GROUP]
    n_asg = N * TOP_IN_GROUP
    flat_e = expert.reshape(n_asg // LANE, LANE)
    order = jnp.argsort(flat_e.reshape(n_asg))
    rank = jnp.argsort(order).astype(jnp.int32)
    counts = jnp.sum(flat_e[None] == jnp.arange(N_EXPERTS)[:, None, None], axis=(1, 2)).astype(jnp.int32)
    start = jnp.cumsum(counts) - counts
    pad_counts = (counts + MOE_ROWS - 1) // MOE_ROWS * MOE_ROWS
    pad_end = jnp.cumsum(pad_counts)
    pad_start = pad_end - pad_counts
    shift = pad_start - start
    asg_shift = jnp.zeros_like(flat_e)
    for e in range(N_EXPERTS):
        asg_shift = jnp.where(flat_e == e, shift[e], asg_shift)
    asg_slot = (rank + asg_shift.reshape(n_asg)).reshape(N, TOP_IN_GROUP)
    n_blocks = n_asg // MOE_ROWS + N_EXPERTS
    blk_start = jnp.arange(n_blocks) * MOE_ROWS
    blk_e = jnp.minimum(jnp.sum(pad_end[None, :] <= blk_start[:, None], 1), N_EXPERTS - 1).astype(jnp.int32)
    blk_pos = blk_start - pad_start[blk_e]
    row = jnp.arange(MOE_ROWS)[None, :]
    slot_real = (blk_pos[:, None] + row < counts[blk_e][:, None]) & (blk_start[:, None] < pad_end[N_EXPERTS - 1])
    slot_src = jnp.where(slot_real, (start[blk_e] + blk_pos)[:, None] + row, 0).reshape(-1)
    slot_asg = order[slot_src]
    slot_real = slot_real.reshape(-1)
    slot_tok = jnp.where(slot_real, slot_asg // TOP_IN_GROUP, 0).astype(jnp.int32)
    slot_w = jnp.where(slot_real, gate.reshape(n_asg)[slot_asg], 0.0)
    yb = _moe_ffn(blk_e, h2[slot_tok], slot_w[:, None], w1, w3, w2)
    return yb[asg_slot[:, 0]].astype(F32) + yb[asg_slot[:, 1]].astype(F32)


INT_MIN = -2 ** 31
DSA_KEY_STEP = 512


def _float_order_key(x):
    bits = pltpu.bitcast(x, jnp.int32)
    bits = jnp.where(x == 0.0, 0, bits)
    return bits ^ ((bits >> 31) & 0x7FFFFFFF)


COL_PART = 64


def _col_reduce(x, reduce):
    part = reduce(x.reshape(x.shape[0] // COL_PART, COL_PART, x.shape[1]), axis=0)
    return reduce(part, axis=0, keepdims=True)


def _col_count(mask):
    return _col_reduce(jnp.where(mask, 1.0, 0.0), jnp.sum)


def _head_block_diag(t, group):
    n_heads = t.shape[0] // group
    row_h = lax.broadcasted_iota(jnp.int32, t.shape, 0) // group
    return jnp.concatenate([jnp.where(row_h == h, t, 0.0) for h in range(n_heads)], axis=1)


def _dsa_block_t(qd_ref, kd4_ref, vdwt_ref, qi_ref, kihi_ref, kilo_ref, wi_ref, g_ref, o_ref, *, kl, n_sel):
    Q = DSA_BLOCK
    q0 = pl.program_id(1) * Q
    w_hi, w_lo = _split_bf16(_head_block_diag(jnp.transpose(qi_ref[0]), IDX_DIM))
    k_hi, k_lo = kihi_ref[0, :kl, :], kilo_ref[0, :kl, :]
    sc = _dot(k_hi, w_hi) + _dot(k_lo, w_hi) + _dot(k_hi, w_lo)
    wit = jnp.transpose(wi_ref[0])
    score = sum(wit[h:h + 1, :] * jnp.maximum(sc[:, h * Q:(h + 1) * Q], 0.0) for h in range(IDX_HEADS))
    kidx = lax.broadcasted_iota(jnp.int32, (kl, Q), 0)
    qpos = q0 + lax.broadcasted_iota(jnp.int32, (kl, Q), 1)
    adm = kidx <= qpos
    key = _float_order_key(jnp.where(adm, score, -jnp.inf))

    def value_bit(it, tau):
        cand = tau | jnp.left_shift(jnp.int32(1), 31 - it)
        return jnp.where(_col_count(key >= (cand ^ INT_MIN)) >= n_sel, cand, tau)

    tau = lax.fori_loop(0, 32, value_bit, jnp.zeros((1, Q), jnp.int32)) ^ INT_MIN
    gt = key > tau
    eq = (key == tau) & adm
    need = n_sel - _col_count(gt)
    n_eq = _col_count(eq)

    def index_bits():
        def index_bit(it, bound):
            cand = bound | jnp.left_shift(jnp.int32(1), 11 - it)
            return jnp.where(_col_count(eq & (kidx < cand)) <= need, cand, bound)
        return lax.fori_loop(0, 12, index_bit, jnp.zeros((1, Q), jnp.int32))

    bound = lax.cond(jnp.max(n_eq - need) > 0.0, index_bits, lambda: jnp.full((1, Q), kl, jnp.int32))
    sel = gt | (eq & (kidx < bound))

    w_att = _head_block_diag(jnp.transpose(qd_ref[0]), HEAD_DIM).astype(BF16)
    lg = _dot(kd4_ref[0, :kl, :], w_att) * HEAD_DIM ** -0.5
    lg = jnp.where(jnp.concatenate([sel] * N_HEADS, axis=1), lg, -jnp.inf)
    p = jnp.exp(lg - _col_reduce(lg, jnp.max))
    out_t = _dot(vdwt_ref[0, :, :kl], p.astype(BF16)) / _col_reduce(p, jnp.sum)
    row_h = lax.broadcasted_iota(jnp.int32, (GROUP_W, Q), 0) // HEAD_DIM
    nat_t = sum(jnp.where(row_h == h, out_t[:, h * Q:(h + 1) * Q], 0.0) for h in range(N_HEADS))
    r = jnp.transpose(nat_t)
    hi_ = lax.broadcasted_iota(jnp.int32, (GROUP_W, GROUP_W), 0) // HEAD_DIM
    hj_ = lax.broadcasted_iota(jnp.int32, (GROUP_W, GROUP_W), 1) // HEAD_DIM
    ms = _dot_f32_by_exact(r * r, jnp.where(hi_ == hj_, 1.0, 0.0).astype(BF16)) * (1.0 / HEAD_DIM)
    o_ref[0] = r * lax.rsqrt(ms + NORM_EPS) * g_ref[...]


def _dsa_kernel(qd_ref, kd4_ref, vdwt_ref, qi_ref, kihi_ref, kilo_ref, wi_ref, g_ref, o_ref, *, kls, n_sel):
    blocks_per_step = DSA_KEY_STEP // DSA_BLOCK
    for j, kl in enumerate(kls):
        @pl.when(pl.program_id(1) // blocks_per_step == j)
        def _():
            _dsa_block_t(qd_ref, kd4_ref, vdwt_ref, qi_ref, kihi_ref, kilo_ref, wi_ref, g_ref, o_ref,
                         kl=kl, n_sel=n_sel)


DSA_PREP_ROWS = 256
D_Q, D_KV, D_QI, D_KW = 0, GROUP_W, GROUP_W + LANE, GROUP_W + 2 * LANE


def _swap_halves(x, half):
    n = x.shape[1]
    lane = lax.broadcasted_iota(jnp.int32, x.shape, 1)
    return jnp.where(lane % (2 * half) < half, pltpu.roll(x, n - half, axis=1), pltpu.roll(x, half, axis=1))


def _dsa_prep_kernel(p_ref, cq_ref, sq_ref, ci_ref, si_ref, gq_ref, gk_ref,
                     qd_ref, kd4_ref, vdwt_ref, qi_ref, kihi_ref, kilo_ref, wi_ref):
    GW = GROUP_W
    x = p_ref[0]
    hi_ = lax.broadcasted_iota(jnp.int32, (GW, GW), 0) // HEAD_DIM
    hj_ = lax.broadcasted_iota(jnp.int32, (GW, GW), 1) // HEAD_DIM
    ones_bd = jnp.where(hi_ == hj_, 1.0, 0.0).astype(BF16)
    q = x[:, D_Q:D_Q + GW]
    q = q * lax.rsqrt(_dot_f32_by_exact(q * q, ones_bd) * (1.0 / HEAD_DIM) + NORM_EPS) * gq_ref[...]
    qd_ref[0] = q * cq_ref[...] + _swap_halves(q, HEAD_DIM // 2) * sq_ref[...]
    kv = x[:, D_KV:D_KV + LANE]
    lane = lax.broadcasted_iota(jnp.int32, kv.shape, 1)
    is_k = lane < HEAD_DIM
    ms = jnp.sum(jnp.where(is_k, kv * kv, 0.0), axis=1, keepdims=True) * (1.0 / HEAD_DIM)
    kn = kv * lax.rsqrt(ms + NORM_EPS) * gk_ref[...]
    kr = kn * cq_ref[:, :LANE] + _swap_halves(kn, HEAD_DIM // 2) * sq_ref[:, :LANE]
    k2 = jnp.where(is_k, kr, pltpu.roll(kr, HEAD_DIM, axis=1))
    kd4_ref[0] = jnp.concatenate([k2] * (GW // LANE), axis=1).astype(BF16)
    v2 = jnp.where(is_k, pltpu.roll(kv, HEAD_DIM, axis=1), kv)
    v2t = jnp.transpose(v2)
    vdwt_ref[0] = jnp.concatenate([v2t] * (GW // LANE), axis=0).astype(BF16)
    qi = x[:, D_QI:D_QI + LANE]
    qi_ref[0] = qi * ci_ref[...] + _swap_halves(qi, IDX_DIM // 2) * si_ref[...]
    kw = x[:, D_KW:D_KW + LANE]
    kir = kw * ci_ref[...] + _swap_halves(kw, IDX_DIM // 2) * si_ref[...]
    ki1 = jnp.where(lane < IDX_DIM, kir, 0.0)
    ki2 = ki1 + pltpu.roll(ki1, IDX_DIM, axis=1)
    ki4 = ki2 + pltpu.roll(ki2, 2 * IDX_DIM, axis=1)
    kihi_ref[0], kilo_ref[0] = _split_bf16(ki4)
    wi_ref[0] = pltpu.roll(kw, LANE - IDX_DIM, axis=1) * (IDX_HEADS ** -0.5 * IDX_DIM ** -0.5)


def _rope_tables(S, dim, width):
    half = dim // 2
    inv = ROPE_THETA ** (-jnp.arange(half, dtype=F32) / half)
    ang = jnp.arange(S, dtype=F32)[:, None] * inv[None, :]
    cos = jnp.tile(jnp.cos(ang), (1, width // half))
    sin = jnp.tile(jnp.concatenate([-jnp.sin(ang), jnp.sin(ang)], axis=1), (1, width // dim))
    return cos, sin


def _dsa_prep(p, qn_g, kn_g):
    B, S, _ = p.shape
    GW, R = GROUP_W, DSA_PREP_ROWS
    cq, sq = _rope_tables(S, HEAD_DIM, GW)
    ci, si = _rope_tables(S, IDX_DIM, LANE)
    gq = jnp.tile(qn_g, N_HEADS)[None]
    gk = jnp.pad(kn_g, (0, LANE - HEAD_DIM))[None]
    rows = lambda w: pl.BlockSpec((1, R, w), lambda b, i: (b, i, 0))
    tab = lambda w: pl.BlockSpec((R, w), lambda b, i: (i, 0))
    cols = lambda r: pl.BlockSpec((1, r, R), lambda b, i: (b, 0, i))
    vec = lambda w: pl.BlockSpec((1, w), lambda b, i: (0, 0))
    return pl.pallas_call(
        _dsa_prep_kernel,
        grid=(B, S // R),
        in_specs=[rows(D_PAD), tab(GW), tab(GW), tab(LANE), tab(LANE), vec(GW), vec(LANE)],
        out_specs=[rows(GW), rows(GW), cols(GW), rows(LANE), rows(LANE), rows(LANE), rows(LANE)],
        out_shape=[jax.ShapeDtypeStruct((B, S, GW), F32), jax.ShapeDtypeStruct((B, S, GW), BF16),
                   jax.ShapeDtypeStruct((B, GW, S), BF16), jax.ShapeDtypeStruct((B, S, LANE), F32),
                   jax.ShapeDtypeStruct((B, S, LANE), BF16), jax.ShapeDtypeStruct((B, S, LANE), BF16),
                   jax.ShapeDtypeStruct((B, S, LANE), F32)],
        compiler_params=pltpu.CompilerParams(dimension_semantics=("parallel", "parallel"),
                                             vmem_limit_bytes=VMEM_LIMIT),
        name="dsa_prep",
    )(p, cq, sq, ci, si, gq, gk)


def _dsa_attn_norm(qd, kd4, vdwt, qi, ki_hi, ki_lo, wi, g):
    B, S, _ = qd.shape
    n_sel = min(TOPK_MAX, S // 4)
    assert S % DSA_KEY_STEP == 0 and n_sel <= DSA_KEY_STEP
    kls = tuple(range(DSA_KEY_STEP, S + 1, DSA_KEY_STEP))
    blk = lambda w: pl.BlockSpec((1, DSA_BLOCK, w), lambda b, i: (b, i, 0))
    per_b = lambda r, c: pl.BlockSpec((1, r, c), lambda b, i: (b, 0, 0))
    return pl.pallas_call(
        functools.partial(_dsa_kernel, kls=kls, n_sel=n_sel),
        grid=(B, S // DSA_BLOCK),
        in_specs=[blk(GROUP_W), per_b(S, GROUP_W), per_b(GROUP_W, S), blk(IDX_HEADS * IDX_DIM),
                  per_b(S, LANE), per_b(S, LANE), blk(LANE), pl.BlockSpec((1, GROUP_W), lambda b, i: (0, 0))],
        out_specs=blk(GROUP_W),
        out_shape=jax.ShapeDtypeStruct((B, S, GROUP_W), F32),
        compiler_params=pltpu.CompilerParams(dimension_semantics=("parallel", "parallel"),
                                             vmem_limit_bytes=VMEM_LIMIT),
        name="dsa_attn",
    )(qd, kd4, vdwt, qi, ki_hi, ki_lo, wi, g[None])


RWKV_CHUNK = 64
RWKV_LOW = RWKV_W_RANK + RWKV_A_RANK + RWKV_G_RANK
RWKV_GROUP = LANE // HEAD_DIM
RWKV_BATCH = 2


def _dot(a, b):
    return jnp.dot(a, b, preferred_element_type=F32)


def _dot_nt(a, b):
    return lax.dot_general(a, b, (((1,), (1,)), ((), ())), preferred_element_type=F32)


def _dot_tn(a, b):
    return lax.dot_general(a, b, (((0,), (0,)), ((), ())), preferred_element_type=F32)


def _split3_bf16(t):
    p1 = t.astype(BF16)
    r1 = t - p1.astype(F32)
    p2 = r1.astype(BF16)
    p3 = (r1 - p2.astype(F32)).astype(BF16)
    return p1, p2, p3


def _dot_f32_by_exact(a, b_exact):
    return sum(_dot(p, b_exact) for p in _split3_bf16(a))


def _dot_exact_by_f32(a_exact, b):
    return sum(_dot(a_exact, p) for p in _split3_bf16(b))


def _dot3(a, b_hi, b_lo):
    a_hi, a_lo = _split_bf16(a)
    return _dot(a_hi, b_hi) + _dot(a_lo, b_hi) + _dot(a_hi, b_lo)


def _softplus(z):
    return jnp.maximum(z, 0.0) + jnp.log(1.0 + jnp.exp(-jnp.abs(z)))


def _rwkv_kernel(p_ref, mu_ref, vec_ref, lhi_ref, llo_ref, o_ref, state_ref, prev_ref):
    @pl.when(pl.program_id(1) == 0)
    def _():
        state_ref[...] = jnp.zeros_like(state_ref)
        prev_ref[...] = jnp.zeros_like(prev_ref)

    for n in range(RWKV_BATCH):
        _rwkv_chunk(p_ref.at[n], mu_ref, vec_ref, lhi_ref, llo_ref, o_ref.at[n], state_ref.at[n], prev_ref.at[n])


def _rwkv_chunk(p_ref, mu_ref, vec_ref, lhi_ref, llo_ref, o_ref, state_ref, prev_ref):
    L, GW = RWKV_CHUNK, GROUP_W
    p = p_ref[...]
    row = lax.broadcasted_iota(jnp.int32, p.shape, 0)
    prev = jnp.where(row == 0, prev_ref[...], pltpu.roll(p, 1, axis=0))
    prev_ref[...] = p[L - 1:L, :]
    ps = p + (prev - p) * mu_ref[...]
    r, k, v = ps[:, :GW], ps[:, GW:2 * GW], ps[:, 2 * GW:3 * GW]
    low = ps[:, 3 * GW:]
    lane_low = lax.broadcasted_iota(jnp.int32, low.shape, 1)
    low = jnp.where(lane_low < RWKV_W_RANK, jnp.tanh(low),
                    jnp.where(lane_low < RWKV_W_RANK + RWKV_A_RANK, low, jax.nn.sigmoid(low)))
    up = _dot3(low, lhi_ref[...], llo_ref[...])
    w0, a0, k_k, k_a = vec_ref[0:1, :], vec_ref[1:2, :], vec_ref[2:3, :], vec_ref[3:4, :]
    r_k, ln_g, ln_b = vec_ref[4:5, :], vec_ref[5:6, :], vec_ref[6:7, :]
    logw = -jnp.exp(-_softplus(-(w0 + up[:, :GW])) - 0.5)
    rate = jax.nn.sigmoid(a0 + up[:, GW:2 * GW])
    gate = up[:, 2 * GW:]

    ri = lax.broadcasted_iota(jnp.int32, (GW, GW), 0)
    ci = lax.broadcasted_iota(jnp.int32, (GW, GW), 1)
    ones_bd = jnp.where((ri // HEAD_DIM) == (ci // HEAD_DIM), 1.0, 0.0).astype(BF16)

    kk = k * k_k
    kk = kk / jnp.maximum(jnp.sqrt(_dot_f32_by_exact(kk * kk, ones_bd)), 1e-12)
    k = k * (1.0 + (rate - 1.0) * k_a)

    ti = lax.broadcasted_iota(jnp.int32, (L, L), 0)
    tj = lax.broadcasted_iota(jnp.int32, (L, L), 1)
    lc = _dot_exact_by_f32(jnp.where(tj <= ti, 1.0, 0.0).astype(BF16), logw)
    lc_last = lc[L - 1:L, :]
    dec_in = jnp.exp(lc)
    dec_out = jnp.exp(-lc)
    a_t = -kk * jnp.exp(lc - logw)
    b_t = kk * rate * dec_out
    k_t = k * dec_out
    r_t = r * dec_in
    to_end = jnp.exp(lc_last)

    SG = RWKV_GROUP * L
    gi = lax.broadcasted_iota(jnp.int32, (SG, SG), 0)
    gj = lax.broadcasted_iota(jnp.int32, (SG, SG), 1)
    g_same = (gi // L) == (gj // L)
    strict = g_same & ((gj % L) < (gi % L))
    incl = g_same & ((gj % L) <= (gi % L))
    eye = jnp.where(gi == gj, 1.0, 0.0)
    stack = lambda t: jnp.concatenate([t] * RWKV_GROUP, axis=0)
    bd = lambda t: jnp.where(g_same, stack(t), 0.0).astype(BF16)
    n_doublings = RWKV_CHUNK.bit_length() - 2
    ys = []
    for g in range(N_HEADS // RWKV_GROUP):
        cols = slice(g * SG, (g + 1) * SG)
        a_bd, r_bd, v_bd = bd(a_t[:, cols]), bd(r_t[:, cols]), bd(v[:, cols])
        m = _dot_nt(jnp.concatenate([a_bd, r_bd], axis=0),
                    jnp.concatenate([stack(b_t[:, cols]), stack(k_t[:, cols])], axis=0).astype(BF16))
        m_ab = jnp.where(strict, m[:SG, :SG], 0.0)
        m_ak = jnp.where(strict, m[:SG, SG:], 0.0).astype(BF16)
        m_rb = jnp.where(incl, m[SG:, :SG], 0.0).astype(BF16)
        m_rk = jnp.where(incl, m[SG:, SG:], 0.0).astype(BF16)

        inv = eye + m_ab
        pw = m_ab
        for s in range(n_doublings):
            pw_b = pw.astype(BF16)
            pw = _dot(pw_b, pw_b)
            inv = inv + _dot(inv.astype(BF16), pw.astype(BF16))

        t0 = state_ref[g]
        t0_b = t0.astype(BF16)
        u = _dot(inv.astype(BF16), (_dot(a_bd, t0_b) + _dot(m_ak, v_bd)).astype(BF16)).astype(BF16)
        y_bd = _dot(r_bd, t0_b) + _dot(m_rb, u) + _dot(m_rk, v_bd)
        ys.append(sum(y_bd[h * L:(h + 1) * L, :] for h in range(RWKV_GROUP)))

        end_g = to_end[:, cols]
        to_end_col = jnp.sum(jnp.where(gi == gj, jnp.broadcast_to(end_g, (SG, SG)), 0.0), axis=1, keepdims=True)
        state_ref[g] = (to_end_col * t0 + _dot_tn(bd(b_t[:, cols] * end_g), u)
                        + _dot_tn(bd(k_t[:, cols] * end_g), v_bd))
    y = jnp.concatenate(ys, axis=1)

    inv_d = 1.0 / HEAD_DIM
    mean = _dot_f32_by_exact(y, ones_bd) * inv_d
    yc = y - mean
    var = _dot_f32_by_exact(yc * yc, ones_bd) * inv_d
    yn = yc * lax.rsqrt(var + RWKV_GN_EPS) * ln_g + ln_b
    bonus = _dot_f32_by_exact(r * k * r_k, ones_bd) * v
    o_ref[...] = (yn + bonus) * gate


def _rwkv7_time_mix(p, mu, w0, w2, a0, a2, g2, k_k, k_a, r_k, ln_g, ln_b):
    B, S, _ = p.shape
    GW = GROUP_W
    assert S % RWKV_CHUNK == 0 and RWKV_CHUNK == HEAD_DIM and B % RWKV_BATCH == 0
    low_w = jnp.zeros((RWKV_LOW, 3 * GW), F32)
    low_w = low_w.at[:RWKV_W_RANK, :GW].set(w2)
    low_w = low_w.at[RWKV_W_RANK:RWKV_W_RANK + RWKV_A_RANK, GW:2 * GW].set(a2)
    low_w = low_w.at[RWKV_W_RANK + RWKV_A_RANK:, 2 * GW:].set(g2)
    l_hi, l_lo = _split_bf16(low_w)
    vecs = jnp.stack([w0, a0, k_k, k_a, r_k, ln_g, ln_b, jnp.zeros_like(w0)], 0)
    full = lambda a: pl.BlockSpec(a.shape, lambda b, c: (0,) * a.ndim)
    mu2 = mu[None]
    return pl.pallas_call(
        _rwkv_kernel,
        grid=(B // RWKV_BATCH, S // RWKV_CHUNK),
        in_specs=[pl.BlockSpec((RWKV_BATCH, RWKV_CHUNK, A_PAD), lambda b, c: (b, c, 0)),
                  full(mu2), full(vecs), full(l_hi), full(l_lo)],
        out_specs=pl.BlockSpec((RWKV_BATCH, RWKV_CHUNK, GW), lambda b, c: (b, c, 0)),
        out_shape=jax.ShapeDtypeStruct((B, S, GW), F32),
        scratch_shapes=[pltpu.VMEM((RWKV_BATCH, N_HEADS // RWKV_GROUP, LANE, LANE), F32),
                        pltpu.VMEM((RWKV_BATCH, 1, A_PAD), F32)],
        compiler_params=pltpu.CompilerParams(dimension_semantics=("parallel", "arbitrary"),
                                             vmem_limit_bytes=VMEM_LIMIT),
        name="rwkv7",
    )(p, mu2, vecs, l_hi, l_lo)


def _sb_kernel(q_ref, k_ref, v_ref, g_ref, o_ref, kbd_ref, vbd_ref):
    i = pl.program_id(1)
    T, GW, H = SB_BLOCK, GROUP_W, N_HEADS
    lane_h = lax.broadcasted_iota(jnp.int32, (T, GW), 1) // HEAD_DIM
    k_new, v_new = k_ref[0], v_ref[0]
    for h in range(H):
        kbd_ref[i, h * T:(h + 1) * T, :] = jnp.where(lane_h == h, k_new, 0.0).astype(BF16)
        vbd_ref[i, h * T:(h + 1) * T, :] = jnp.where(lane_h == h, v_new, 0.0).astype(BF16)

    q = q_ref[0].astype(BF16)
    si = lax.broadcasted_iota(jnp.int32, (T, 2 * T), 0)
    sj = lax.broadcasted_iota(jnp.int32, (T, 2 * T), 1)
    later_and_all = jnp.where((si > sj) | (sj >= T), 1.0, 0.0).astype(BF16)
    qrow = lax.broadcasted_iota(jnp.int32, (T, H * T), 0)
    kcol = lax.broadcasted_iota(jnp.int32, (T, H * T), 1) % T
    causal = kcol < qrow

    def key_block(j, state, diagonal):
        carry, acc = state
        z = _dot_nt(q, kbd_ref[j]) * HEAD_DIM ** -0.5
        soft = jnp.log(1.0 + jnp.exp(-jnp.abs(z)))
        log1m = -(jnp.maximum(z, 0.0) + soft)
        log_sig = jnp.minimum(z, 0.0) - soft
        log1m_in = (jnp.where(causal, log1m, 0.0) if diagonal else log1m).astype(BF16)
        sums = [_dot(log1m_in[:, h * T:(h + 1) * T], later_and_all) for h in range(H)]
        suffix = jnp.concatenate([s[:, :T] for s in sums], axis=1)
        total = jnp.concatenate([s[:, T:] for s in sums], axis=1)
        att = jnp.exp(log_sig + suffix + carry)
        if diagonal:
            att = jnp.where(causal, att, 0.0)
        return carry + total, acc + _dot(att.astype(BF16), vbd_ref[j])

    state = key_block(i, (jnp.zeros((T, H * T), F32), jnp.zeros((T, GW), F32)), True)
    odd = i % 2
    state = lax.fori_loop(0, odd, lambda it, st: key_block(i - 1, st, False), state)
    top = i - 1 - odd

    def two_blocks(it, st):
        return key_block(top - 2 * it - 1, key_block(top - 2 * it, st, False), False)

    _, y = lax.fori_loop(0, i // 2, two_blocks, state)

    hi_ = lax.broadcasted_iota(jnp.int32, (GW, GW), 0) // HEAD_DIM
    hj_ = lax.broadcasted_iota(jnp.int32, (GW, GW), 1) // HEAD_DIM
    ones_bd = jnp.where(hi_ == hj_, 1.0, 0.0).astype(BF16)
    ms = _dot_f32_by_exact(y * y, ones_bd) * (1.0 / HEAD_DIM)
    o_ref[0] = y * lax.rsqrt(ms + NORM_EPS) * g_ref[...]


def _stick_breaking_norm(p, g):
    B, S, _ = p.shape
    GW = GROUP_W
    assert S % SB_BLOCK == 0
    col = lambda n: pl.BlockSpec((1, SB_BLOCK, GW), lambda b, i: (b, i, n))
    blk = col(0)
    q = k = v = p
    stacked = pltpu.VMEM((S // SB_BLOCK, N_HEADS * SB_BLOCK, GW), BF16)
    return pl.pallas_call(
        _sb_kernel,
        grid=(B, S // SB_BLOCK),
        in_specs=[col(0), col(1), col(2), pl.BlockSpec((1, GW), lambda b, i: (0, 0))],
        out_specs=blk,
        out_shape=jax.ShapeDtypeStruct((B, S, GW), F32),
        scratch_shapes=[stacked, stacked],
        compiler_params=pltpu.CompilerParams(dimension_semantics=("parallel", "arbitrary"),
                                             vmem_limit_bytes=VMEM_LIMIT),
        name="stick_breaking",
    )(q, k, v, g[None])


ML_HALO = 8
ML_GROUP = LANE // HEAD_DIM


def _dot_nt_exact_by_f32(a_exact, b):
    return sum(_dot_nt(a_exact, p) for p in _split3_bf16(b))


def _mlstm_kernel(p_ref, cw_ref, cb_ref, gb_ref, g_ref, o_ref, ext_ref, ct_ref, n_ref, m_ref):
    L, GW, H = ML_CHUNK, GROUP_W, N_HEADS

    @pl.when(pl.program_id(1) == 0)
    def _():
        ext_ref[...] = jnp.zeros_like(ext_ref)
        ct_ref[...] = jnp.zeros_like(ct_ref)
        n_ref[...] = jnp.zeros_like(n_ref)
        m_ref[...] = jnp.zeros_like(m_ref)

    x = p_ref[0]
    ext_ref[ML_HALO:, :] = x[:, :2 * GW]
    conv = cb_ref[...]
    for j in range(ML_CONV):
        conv = conv + cw_ref[j:j + 1, :] * ext_ref[pl.ds(ML_HALO - (ML_CONV - 1) + j, L), :]
    ext_ref[:ML_HALO, :] = x[L - ML_HALO:, :2 * GW]
    qk = conv * jax.nn.sigmoid(conv)
    q, k = qk[:, :GW], qk[:, GW:] * HEAD_DIM ** -0.5
    v, o = x[:, 2 * GW:3 * GW], x[:, 3 * GW:4 * GW]

    gates = x[:, 4 * GW:]
    gi = lax.broadcasted_iota(jnp.int32, (LANE, 2 * GW), 0)
    gj = lax.broadcasted_iota(jnp.int32, (LANE, 2 * GW), 1)
    expand = jnp.where(gi == (gj % GW) // HEAD_DIM + H * (gj // GW), 1.0, 0.0).astype(BF16)
    graw = _dot_f32_by_exact(gates, expand) + gb_ref[...]
    capped = GATE_CAP * jnp.tanh(graw * (1.0 / GATE_CAP))
    log_i = capped[:, :GW]
    cf = capped[:, GW:]
    log_f = jnp.minimum(cf, 0.0) - jnp.log(1.0 + jnp.exp(-jnp.abs(cf)))

    ti = lax.broadcasted_iota(jnp.int32, (L, L), 0)
    tj = lax.broadcasted_iota(jnp.int32, (L, L), 1)
    bf = _dot_exact_by_f32(jnp.where(tj <= ti, 1.0, 0.0).astype(BF16), log_f)
    b_last = bf[L - 1:L, :]
    m_row, n_row = m_ref[...], n_ref[...]
    dec = b_last - bf + log_i
    m_new = jnp.maximum(b_last + m_row, jnp.max(dec, axis=0, keepdims=True))
    kw = k * jnp.exp(dec - m_new)
    s_old = jnp.exp(b_last + m_row - m_new)
    n_ref[...] = s_old * n_row + jnp.sum(kw, axis=0, keepdims=True)
    m_ref[...] = m_new
    g_in = bf + m_row
    li_b = log_i - bf

    SG = ML_GROUP * L
    ri = lax.broadcasted_iota(jnp.int32, (SG, SG), 0)
    ci = lax.broadcasted_iota(jnp.int32, (SG, SG), 1)
    same_head = (ri // L) == (ci // HEAD_DIM)
    first_lane = ci == (ri // L) * HEAD_DIM
    sel_first = jnp.where(first_lane, 1.0, 0.0).astype(BF16)
    stack = lambda t: jnp.concatenate([t] * ML_GROUP, axis=0)
    pick = lambda t: jnp.sum(jnp.where(first_lane, t, 0.0), axis=1, keepdims=True)
    rt = lax.broadcasted_iota(jnp.int32, (SG, L), 0) % L
    rs = lax.broadcasted_iota(jnp.int32, (SG, L), 1)
    hs = []
    for g in range(H // ML_GROUP):
        cols = slice(g * SG, (g + 1) * SG)
        ct = ct_ref[g]
        qs = jnp.where(same_head, stack(q[:, cols]), 0.0)
        qs_b = qs.astype(BF16)
        v_b = v[:, cols].astype(BF16)
        b_col = pick(stack(bf[:, cols]))
        g_col = pick(stack(g_in[:, cols]))
        row_part = _dot_nt_exact_by_f32(sel_first, li_b[:, cols])
        dmat = jnp.where(rs <= rt, b_col + row_part, -jnp.inf)
        m_t = jnp.maximum(g_col, jnp.max(dmat, axis=1, keepdims=True))
        s_inter = jnp.exp(g_col - m_t)
        sqk = _dot_nt(qs_b, k[:, cols].astype(BF16)) * jnp.exp(dmat - m_t)
        num = s_inter * _dot(qs_b, ct.astype(BF16)) + jnp.where(same_head, _dot(sqk.astype(BF16), v_b), 0.0)
        den = (s_inter * jnp.sum(qs * n_row[:, cols], axis=1, keepdims=True)
               + jnp.sum(sqk, axis=1, keepdims=True))
        hst = num / jnp.maximum(jnp.abs(den), jnp.exp(-m_t))
        hs.append(sum(hst[n * L:(n + 1) * L, :] for n in range(ML_GROUP)))
        ct_ref[g] = s_old[:, cols] * ct + jnp.where(same_head, _dot_tn(kw[:, cols].astype(BF16), v_b), 0.0)
    h = jnp.concatenate(hs, axis=1)

    hi_ = lax.broadcasted_iota(jnp.int32, (GW, GW), 0) // HEAD_DIM
    hj_ = lax.broadcasted_iota(jnp.int32, (GW, GW), 1) // HEAD_DIM
    ones_bd = jnp.where(hi_ == hj_, 1.0, 0.0).astype(BF16)
    ms = _dot_f32_by_exact(h * h, ones_bd) * (1.0 / HEAD_DIM)
    o_ref[0] = jax.nn.sigmoid(o) * (h * lax.rsqrt(ms + NORM_EPS) * g_ref[...])


def _mlstm_mix(p, conv_w, conv_b, ig_b, fg_b, norm_g):
    B, S, _ = p.shape
    GW = GROUP_W
    assert S % ML_CHUNK == 0 and ML_CONV - 1 <= ML_HALO <= ML_CHUNK
    gate_b = jnp.concatenate([jnp.repeat(ig_b, HEAD_DIM), jnp.repeat(fg_b, HEAD_DIM)])[None]
    full = lambda a: pl.BlockSpec(a.shape, lambda b, c: (0,) * a.ndim)
    cb2, g2 = conv_b[None], norm_g[None]
    return pl.pallas_call(
        _mlstm_kernel,
        grid=(B, S // ML_CHUNK),
        in_specs=[pl.BlockSpec((1, ML_CHUNK, C_PAD), lambda b, c: (b, c, 0)),
                  full(conv_w), full(cb2), full(gate_b), full(g2)],
        out_specs=pl.BlockSpec((1, ML_CHUNK, GW), lambda b, c: (b, c, 0)),
        out_shape=jax.ShapeDtypeStruct((B, S, GW), F32),
        scratch_shapes=[pltpu.VMEM((ML_HALO + ML_CHUNK, 2 * GW), F32),
                        pltpu.VMEM((N_HEADS // ML_GROUP, LANE, LANE), F32),
                        pltpu.VMEM((1, GW), F32), pltpu.VMEM((1, GW), F32)],
        compiler_params=pltpu.CompilerParams(dimension_semantics=("parallel", "arbitrary"),
                                             vmem_limit_bytes=VMEM_LIMIT),
        name="mlstm",
    )(p, conv_w, cb2, gate_b, g2)


def _rms_norm(x, g):
    xf = x.astype(F32)
    y = xf * lax.rsqrt(jnp.mean(xf * xf, -1, keepdims=True) + NORM_EPS)
    return (y * g.astype(F32)).astype(x.dtype)


def _rope(x, pos):
    half = x.shape[-1] // 2
    inv = ROPE_THETA ** (-jnp.arange(half, dtype=F32) / half)
    ang = pos.astype(F32)[:, None] * inv[None, :]
    cos = jnp.cos(ang)[None, :, None, :]
    sin = jnp.sin(ang)[None, :, None, :]
    xf = x.astype(F32)
    x1, x2 = xf[..., :half], xf[..., half:]
    return jnp.concatenate([x1 * cos - x2 * sin, x2 * cos + x1 * sin], -1).astype(x.dtype)


def kernel(x, c, ada_w, ada_b, norm1_g, norm2_g, w_in, rk_mu, rk_w0, rk_w2, rk_a0, rk_a2, rk_g2, rk_kk, rk_ka, rk_rk, rk_ln_g, rk_ln_b, sb_norm_g, ml_conv_w, ml_conv_b, ml_ig_b, ml_fg_b, ml_norm_g, ds_qn_g, ds_kn_g, ds_out_g, w_out, moe_wg, moe_bg, moe_we, moe_be, moe_w1, moe_w3, moe_w2):
    B, S, D = x.shape
    H, d = N_HEADS, HEAD_DIM
    depth = ada_w.shape[0]
    pos = jnp.arange(S)
    c_act = jax.nn.silu(c)
    for l in range(depth):
        mod = (c_act @ ada_w[l] + ada_b[l])[:, None, :]
        sh1, sc1, gt1, sh2, sc2, gt2 = jnp.split(mod, 6, axis=-1)

        pA, pB, pC, pD = _in_proj(x, sc1, sh1, norm1_g[l][None], _pad_w_in(w_in[l]))

        yA = _rwkv7_time_mix(pA, rk_mu[l], rk_w0[l], rk_w2[l], rk_a0[l], rk_a2[l], rk_g2[l],
                             rk_kk[l], rk_ka[l], rk_rk[l], rk_ln_g[l], rk_ln_b[l])

        yB = _stick_breaking_norm(pB, sb_norm_g[l])

        yC = _mlstm_mix(pC, ml_conv_w[l], ml_conv_b[l], ml_ig_b[l], ml_fg_b[l], ml_norm_g[l])

        yD = _dsa_attn_norm(*_dsa_prep(pD, ds_qn_g[l], ds_kn_g[l]), ds_out_g[l])

        router = jnp.pad(jnp.concatenate([moe_wg[l], moe_we[l]], 1),
                         ((0, 0), (0, ROUTER_PAD - N_GROUPS - N_EXPERTS)))
        r_hi, r_lo = _split_bf16(router)
        r_b = jnp.pad(jnp.concatenate([moe_bg[l], moe_be[l]]), (0, ROUTER_PAD - N_GROUPS - N_EXPERTS))[None]
        x1, h2, route = _out_proj((yA, yB, yC, yD), x, gt1, sc2, sh2, norm2_g[l][None],
                                  w_out[l].astype(BF16), r_hi, r_lo, r_b)

        moe = _hier_moe(h2.reshape(B * S, D), route.reshape(B * S, ROUTER_PAD),
                        moe_w1[l], moe_w3[l], moe_w2[l])
        x = x1 + gt2 * moe.reshape(B, S, D)
    return x
```

```python
import functools

import jax
import jax.numpy as jnp
import numpy as np
from jax import lax
from jax.experimental import pallas as pl
from jax.experimental.pallas import tpu as pltpu

F32 = jnp.float32
BF16 = jnp.bfloat16

D_MODEL = 1024
N_MIXERS = 4
GROUP_W = D_MODEL // N_MIXERS
HEAD_DIM = 64
N_HEADS = GROUP_W // HEAD_DIM
NORM_EPS = 1e-6
RWKV_W_RANK = 32
RWKV_A_RANK = 32
RWKV_G_RANK = 64
RWKV_GN_EPS = 64e-5
SB_BLOCK = 128
ML_CHUNK = 64
ML_CONV = 4
GATE_CAP = 15.0
DSA_BLOCK = 128
IDX_HEADS = 4
IDX_DIM = 32
TOPK_MAX = 256
ROPE_THETA = 10000.0
N_GROUPS = 4
EXP_PER_GROUP = 8
N_EXPERTS = N_GROUPS * EXP_PER_GROUP
EXPERT_FF = D_MODEL // 2
TOP_IN_GROUP = 2

A_SIZES = (GROUP_W, GROUP_W, GROUP_W, RWKV_W_RANK, RWKV_A_RANK, RWKV_G_RANK)
B_SIZES = (GROUP_W, GROUP_W, GROUP_W)
C_SIZES = (GROUP_W, GROUP_W, GROUP_W, GROUP_W, N_HEADS, N_HEADS)
D_SIZES = (GROUP_W, HEAD_DIM, HEAD_DIM, IDX_HEADS * IDX_DIM, IDX_DIM, IDX_HEADS)
A_COLS = sum(A_SIZES)
B_COLS = sum(B_SIZES)
C_COLS = sum(C_SIZES)
D_COLS = sum(D_SIZES)

LANE = 128
A_PAD = 896
B_PAD = 768
C_PAD = 1152
D_PAD = 640
P_PAD = A_PAD + B_PAD + C_PAD + D_PAD
ROUTER_PAD = LANE

IN_ROWS = 256
OUT_ROWS = 512
MOE_ROWS = 512
VMEM_LIMIT = 48 * 1024 * 1024


def _split_cols(t, sizes):
    return jnp.split(t, [int(i) for i in np.cumsum(sizes)[:-1]], axis=-1)


def _in_proj_kernel(x_ref, sc_ref, sh_ref, g_ref, w_ref, oa_ref, ob_ref, oc_ref, od_ref):
    x = x_ref[0]
    y = x * lax.rsqrt(jnp.mean(x * x, -1, keepdims=True) + NORM_EPS) * g_ref[...]
    h = y * (1.0 + sc_ref[0]) + sh_ref[0]
    p = jnp.dot(h.astype(BF16), w_ref[...], preferred_element_type=F32)
    oa_ref[0] = p[:, :A_PAD]
    ob_ref[0] = p[:, A_PAD:A_PAD + B_PAD]
    oc_ref[0] = p[:, A_PAD + B_PAD:A_PAD + B_PAD + C_PAD]
    od_ref[0] = p[:, A_PAD + B_PAD + C_PAD:]


def _in_proj(x, sc, sh, g, w_pad):
    B, S, D = x.shape
    row = lambda w: pl.BlockSpec((1, IN_ROWS, w), lambda b, i: (b, i, 0))
    vec = pl.BlockSpec((1, 1, D), lambda b, i: (b, 0, 0))
    return pl.pallas_call(
        _in_proj_kernel,
        grid=(B, S // IN_ROWS),
        in_specs=[row(D), vec, vec, pl.BlockSpec((1, D), lambda b, i: (0, 0)),
                  pl.BlockSpec((D, P_PAD), lambda b, i: (0, 0))],
        out_specs=[row(A_PAD), row(B_PAD), row(C_PAD), row(D_PAD)],
        out_shape=[jax.ShapeDtypeStruct((B, S, w), F32) for w in (A_PAD, B_PAD, C_PAD, D_PAD)],
        compiler_params=pltpu.CompilerParams(dimension_semantics=("parallel", "parallel"),
                                             vmem_limit_bytes=VMEM_LIMIT),
        name="in_proj",
    )(x, sc, sh, g, w_pad)


def _pad_w_in(w):
    wa, wb, wc, wd = _split_cols(w, (A_COLS, B_COLS, C_COLS, D_COLS))
    padc = lambda t, n: jnp.pad(t, ((0, 0), (0, n - t.shape[1])))
    return jnp.concatenate([padc(wa, A_PAD), padc(wb, B_PAD), padc(wc, C_PAD), padc(wd, D_PAD)], 1).astype(BF16)


def _split_bf16(t):
    hi = t.astype(BF16)
    lo = (t - hi.astype(F32)).astype(BF16)
    return hi, lo


def _out_proj_kernel(ya_ref, yb_ref, yc_ref, yd_ref, x_ref, gt_ref, sc_ref, sh_ref, g_ref, w_ref,
                     rhi_ref, rlo_ref, rb_ref, x1_ref, h2_ref, route_ref):
    y_cat = jnp.concatenate([y_ref[0].astype(BF16) for y_ref in (ya_ref, yb_ref, yc_ref, yd_ref)], axis=1)
    x1 = x_ref[0] + gt_ref[0] * jnp.dot(y_cat, w_ref[...], preferred_element_type=F32)
    x1_ref[0] = x1
    y = x1 * lax.rsqrt(jnp.mean(x1 * x1, -1, keepdims=True) + NORM_EPS) * g_ref[...]
    h = y * (1.0 + sc_ref[0]) + sh_ref[0]
    hi, lo = _split_bf16(h)
    h2_ref[0] = hi
    lg = jnp.dot(jnp.concatenate([hi, lo, hi], axis=1),
                 jnp.concatenate([rhi_ref[...], rhi_ref[...], rlo_ref[...]], axis=0),
                 preferred_element_type=F32) + rb_ref[...]
    route_ref[0] = _route(lg)


def _route(lg):
    lane = lax.broadcasted_iota(jnp.int32, lg.shape, 1)
    neg = -jnp.inf
    first = lambda hit: jnp.min(jnp.where(hit, lane, ROUTER_PAD), axis=1, keepdims=True)
    is_grp = lane < N_GROUPS
    grp = jnp.where(is_grp, lg, neg)
    g_max = jnp.max(grp, axis=1, keepdims=True)
    g_p = 1.0 / jnp.sum(jnp.where(is_grp, jnp.exp(grp - g_max), 0.0), axis=1, keepdims=True)
    g_idx = first(grp == g_max)
    e_lane = lane - N_GROUPS
    in_group = (e_lane >= 0) & (e_lane < N_EXPERTS) & (e_lane // EXP_PER_GROUP == g_idx)
    e_log = jnp.where(in_group, lg, neg)
    e1_max = jnp.max(e_log, axis=1, keepdims=True)
    e1_lane = first(e_log == e1_max)
    e_log2 = jnp.where(lane == e1_lane, neg, e_log)
    e2_max = jnp.max(e_log2, axis=1, keepdims=True)
    e2_lane = first(e_log2 == e2_max)
    ratio = jnp.exp(e2_max - e1_max)
    gate1 = g_p / (1.0 + ratio)
    gate2 = gate1 * ratio
    out = jnp.where(lane == 0, (e1_lane - N_GROUPS).astype(F32), 0.0)
    out = jnp.where(lane == 1, (e2_lane - N_GROUPS).astype(F32), out)
    out = jnp.where(lane == 2, gate1, out)
    return jnp.where(lane == 3, gate2, out)


def _out_proj(ys, x, gt, sc, sh, g, w_out, r_hi, r_lo, r_b):
    B, S, D = x.shape
    row = lambda w: pl.BlockSpec((1, OUT_ROWS, w), lambda b, i: (b, i, 0))
    vec = pl.BlockSpec((1, 1, D), lambda b, i: (b, 0, 0))
    full = lambda a: pl.BlockSpec(a.shape, lambda b, i: (0,) * a.ndim)
    return pl.pallas_call(
        _out_proj_kernel,
        grid=(B, S // OUT_ROWS),
        in_specs=[row(GROUP_W)] * 4 + [row(D), vec, vec, vec, full(g), full(w_out), full(r_hi), full(r_lo),
                                        full(r_b)],
        out_specs=[row(D), row(D), row(ROUTER_PAD)],
        out_shape=[jax.ShapeDtypeStruct((B, S, D), F32), jax.ShapeDtypeStruct((B, S, D), BF16),
                   jax.ShapeDtypeStruct((B, S, ROUTER_PAD), F32)],
        compiler_params=pltpu.CompilerParams(dimension_semantics=("parallel", "parallel"),
                                             vmem_limit_bytes=VMEM_LIMIT),
        name="out_proj",
    )(*ys, x, gt, sc, sh, g, w_out, r_hi, r_lo, r_b)


def _moe_ffn_kernel(blk_e_ref, x_ref, wt_ref, w1_ref, w3_ref, w2_ref, o_ref, w1b_ref, w3b_ref, w2b_ref):
    i = pl.program_id(0)
    changed = jnp.logical_or(i == 0, blk_e_ref[i] != blk_e_ref[jnp.maximum(i - 1, 0)])

    @pl.when(changed)
    def _():
        w1b_ref[...] = w1_ref[0].astype(BF16)
        w3b_ref[...] = w3_ref[0].astype(BF16)
        w2b_ref[...] = w2_ref[0].astype(BF16)

    xb = x_ref[...]
    a = jnp.dot(xb, w1b_ref[...], preferred_element_type=F32)
    b = jnp.dot(xb, w3b_ref[...], preferred_element_type=F32)
    hmid = (a * jax.nn.sigmoid(a) * b).astype(BF16)
    y = jnp.dot(hmid, w2b_ref[...], preferred_element_type=F32)
    o_ref[...] = (y * wt_ref[...]).astype(o_ref.dtype)


def _moe_ffn(blk_e, xs, wt, w1, w3, w2):
    n_slots, D = xs.shape
    n_blocks = n_slots // MOE_ROWS
    FF = w1.shape[-1]
    return pl.pallas_call(
        _moe_ffn_kernel,
        grid_spec=pltpu.PrefetchScalarGridSpec(
            num_scalar_prefetch=1,
            grid=(n_blocks,),
            in_specs=[pl.BlockSpec((MOE_ROWS, D), lambda i, e: (i, 0)),
                      pl.BlockSpec((MOE_ROWS, 1), lambda i, e: (i, 0)),
                      pl.BlockSpec((1, D, FF), lambda i, e: (e[i], 0, 0)),
                      pl.BlockSpec((1, D, FF), lambda i, e: (e[i], 0, 0)),
                      pl.BlockSpec((1, FF, D), lambda i, e: (e[i], 0, 0))],
            out_specs=pl.BlockSpec((MOE_ROWS, D), lambda i, e: (i, 0)),
            scratch_shapes=[pltpu.VMEM((D, FF), BF16), pltpu.VMEM((D, FF), BF16), pltpu.VMEM((FF, D), BF16)],
        ),
        out_shape=jax.ShapeDtypeStruct((n_slots, D), BF16),
        compiler_params=pltpu.CompilerParams(dimension_semantics=("arbitrary",),
                                             vmem_limit_bytes=VMEM_LIMIT),
        name="moe_ffn",
    )(blk_e, xs, wt, w1, w3, w2)


def _hier_moe(h2, route, w1, w3, w2):
    N, D = h2.shape
    expert = route[:, :TOP_IN_GROUP].astype(jnp.int32)
    gate = route[:, TOP_IN_GROUP:2 * TOP_IN_GROUP]
    n_asg = N * TOP_IN_GROUP
    flat_e = expert.reshape(n_asg // LANE, LANE)
    order = jnp.argsort(flat_e.reshape(n_asg))
    rank = jnp.argsort(order).astype(jnp.int32)
    counts = jnp.sum(flat_e[None] == jnp.arange(N_EXPERTS)[:, None, None], axis=(1, 2)).astype(jnp.int32)
    start = jnp.cumsum(counts) - counts
    pad_counts = (counts + MOE_ROWS - 1) // MOE_ROWS * MOE_ROWS
    pad_end = jnp.cumsum(pad_counts)
    pad_start = pad_end - pad_counts
    shift = pad_start - start
    asg_shift = jnp.zeros_like(flat_e)
    for e in range(N_EXPERTS):
        asg_shift = jnp.where(flat_e == e, shift[e], asg_shift)
    asg_slot = (rank + asg_shift.reshape(n_asg)).reshape(N, TOP_IN_GROUP)
    n_blocks = n_asg // MOE_ROWS + N_EXPERTS
    blk_start = jnp.arange(n_blocks) * MOE_ROWS
    blk_e = jnp.minimum(jnp.sum(pad_end[None, :] <= blk_start[:, None], 1), N_EXPERTS - 1).astype(jnp.int32)
    blk_pos = blk_start - pad_start[blk_e]
    row = jnp.arange(MOE_ROWS)[None, :]
    slot_real = (blk_pos[:, None] + row < counts[blk_e][:, None]) & (blk_start[:, None] < pad_end[N_EXPERTS - 1])
    slot_src = jnp.where(slot_real, (start[blk_e] + blk_pos)[:, None] + row, 0).reshape(-1)
    slot_asg = order[slot_src]
    slot_real = slot_real.reshape(-1)
    slot_tok = jnp.where(slot_real, slot_asg // TOP_IN_GROUP, 0).astype(jnp.int32)
    slot_w = jnp.where(slot_real, gate.reshape(n_asg)[slot_asg], 0.0)
    yb = _moe_ffn(blk_e, h2[slot_tok], slot_w[:, None], w1, w3, w2)
    return yb[asg_slot[:, 0]].astype(F32) + yb[asg_slot[:, 1]].astype(F32)


INT_MIN = -2 ** 31
DSA_KEY_STEP = 512


def _float_order_key(x):
    bits = pltpu.bitcast(x, jnp.int32)
    bits = jnp.where(x == 0.0, 0, bits)
    return bits ^ ((bits >> 31) & 0x7FFFFFFF)


COL_PART = 64


def _col_reduce(x, reduce):
    part = reduce(x.reshape(x.shape[0] // COL_PART, COL_PART, x.shape[1]), axis=0)
    return reduce(part, axis=0, keepdims=True)


def _col_count(mask):
    return _col_reduce(jnp.where(mask, 1.0, 0.0), jnp.sum)


def _head_block_diag(t, group):
    n_heads = t.shape[0] // group
    row_h = lax.broadcasted_iota(jnp.int32, t.shape, 0) // group
    return jnp.concatenate([jnp.where(row_h == h, t, 0.0) for h in range(n_heads)], axis=1)


def _dsa_block_t(qd_ref, kd4_ref, vdwt_ref, qi_ref, kihi_ref, kilo_ref, wi_ref, g_ref, o_ref, *, kl, n_sel):
    Q = DSA_BLOCK
    q0 = pl.program_id(1) * Q
    w_hi, w_lo = _split_bf16(_head_block_diag(jnp.transpose(qi_ref[0]), IDX_DIM))
    k_hi, k_lo = kihi_ref[0, :kl, :], kilo_ref[0, :kl, :]
    sc = _dot(jnp.concatenate([k_hi, k_lo, k_hi], axis=1),
              jnp.concatenate([w_hi, w_hi, w_lo], axis=0))
    wit = jnp.transpose(wi_ref[0])
    score = sum(wit[h:h + 1, :] * jnp.maximum(sc[:, h * Q:(h + 1) * Q], 0.0) for h in range(IDX_HEADS))
    kidx = lax.broadcasted_iota(jnp.int32, (kl, Q), 0)
    qpos = q0 + lax.broadcasted_iota(jnp.int32, (kl, Q), 1)
    adm = kidx <= qpos
    key = _float_order_key(jnp.where(adm, score, -jnp.inf))

    def value_bit(it, tau):
        cand = tau | jnp.left_shift(jnp.int32(1), 31 - it)
        return jnp.where(_col_count(key >= (cand ^ INT_MIN)) >= n_sel, cand, tau)

    tau = lax.fori_loop(0, 32, value_bit, jnp.zeros((1, Q), jnp.int32)) ^ INT_MIN
    gt = key > tau
    eq = (key == tau) & adm
    need = n_sel - _col_count(gt)
    n_eq = _col_count(eq)

    def index_bits():
        def index_bit(it, bound):
            cand = bound | jnp.left_shift(jnp.int32(1), 11 - it)
            return jnp.where(_col_count(eq & (kidx < cand)) <= need, cand, bound)
        return lax.fori_loop(0, 12, index_bit, jnp.zeros((1, Q), jnp.int32))

    bound = lax.cond(jnp.max(n_eq - need) > 0.0, index_bits, lambda: jnp.full((1, Q), kl, jnp.int32))
    sel = gt | (eq & (kidx < bound))

    w_att = _head_block_diag(jnp.transpose(qd_ref[0]), HEAD_DIM).astype(BF16)
    lg = _dot(kd4_ref[0, :kl, :], w_att) * HEAD_DIM ** -0.5
    lg = jnp.where(jnp.concatenate([sel] * N_HEADS, axis=1), lg, -jnp.inf)
    p = jnp.exp(lg - _col_reduce(lg, jnp.max))
    out_t = _dot(vdwt_ref[0, :, :kl], p.astype(BF16)) / _col_reduce(p, jnp.sum)
    row_h = lax.broadcasted_iota(jnp.int32, (GROUP_W, Q), 0) // HEAD_DIM
    nat_t = sum(jnp.where(row_h == h, out_t[:, h * Q:(h + 1) * Q], 0.0) for h in range(N_HEADS))
    r = jnp.transpose(nat_t)
    hi_ = lax.broadcasted_iota(jnp.int32, (GROUP_W, GROUP_W), 0) // HEAD_DIM
    hj_ = lax.broadcasted_iota(jnp.int32, (GROUP_W, GROUP_W), 1) // HEAD_DIM
    ms = _dot_f32_by_exact(r * r, jnp.where(hi_ == hj_, 1.0, 0.0).astype(BF16)) * (1.0 / HEAD_DIM)
    o_ref[0] = r * lax.rsqrt(ms + NORM_EPS) * g_ref[...]


def _dsa_kernel(qd_ref, kd4_ref, vdwt_ref, qi_ref, kihi_ref, kilo_ref, wi_ref, g_ref, o_ref, *, kls, n_sel):
    blocks_per_step = DSA_KEY_STEP // DSA_BLOCK
    for j, kl in enumerate(kls):
        @pl.when(pl.program_id(1) // blocks_per_step == j)
        def _():
            _dsa_block_t(qd_ref, kd4_ref, vdwt_ref, qi_ref, kihi_ref, kilo_ref, wi_ref, g_ref, o_ref,
                         kl=kl, n_sel=n_sel)


DSA_PREP_ROWS = 256
D_Q, D_KV, D_QI, D_KW = 0, GROUP_W, GROUP_W + LANE, GROUP_W + 2 * LANE


def _swap_halves(x, half):
    n = x.shape[1]
    lane = lax.broadcasted_iota(jnp.int32, x.shape, 1)
    return jnp.where(lane % (2 * half) < half, pltpu.roll(x, n - half, axis=1), pltpu.roll(x, half, axis=1))


def _dsa_prep_kernel(p_ref, cq_ref, sq_ref, ci_ref, si_ref, gq_ref, gk_ref,
                     qd_ref, kd4_ref, vdwt_ref, qi_ref, kihi_ref, kilo_ref, wi_ref):
    GW = GROUP_W
    x = p_ref[0]
    hi_ = lax.broadcasted_iota(jnp.int32, (GW, GW), 0) // HEAD_DIM
    hj_ = lax.broadcasted_iota(jnp.int32, (GW, GW), 1) // HEAD_DIM
    ones_bd = jnp.where(hi_ == hj_, 1.0, 0.0).astype(BF16)
    q = x[:, D_Q:D_Q + GW]
    q = q * lax.rsqrt(_dot_f32_by_exact(q * q, ones_bd) * (1.0 / HEAD_DIM) + NORM_EPS) * gq_ref[...]
    qd_ref[0] = q * cq_ref[...] + _swap_halves(q, HEAD_DIM // 2) * sq_ref[...]
    kv = x[:, D_KV:D_KV + LANE]
    lane = lax.broadcasted_iota(jnp.int32, kv.shape, 1)
    is_k = lane < HEAD_DIM
    ms = jnp.sum(jnp.where(is_k, kv * kv, 0.0), axis=1, keepdims=True) * (1.0 / HEAD_DIM)
    kn = kv * lax.rsqrt(ms + NORM_EPS) * gk_ref[...]
    kr = kn * cq_ref[:, :LANE] + _swap_halves(kn, HEAD_DIM // 2) * sq_ref[:, :LANE]
    k2 = jnp.where(is_k, kr, pltpu.roll(kr, HEAD_DIM, axis=1))
    kd4_ref[0] = jnp.concatenate([k2] * (GW // LANE), axis=1).astype(BF16)
    v2 = jnp.where(is_k, pltpu.roll(kv, HEAD_DIM, axis=1), kv)
    v2t = jnp.transpose(v2)
    vdwt_ref[0] = jnp.concatenate([v2t] * (GW // LANE), axis=0).astype(BF16)
    qi = x[:, D_QI:D_QI + LANE]
    qi_ref[0] = qi * ci_ref[...] + _swap_halves(qi, IDX_DIM // 2) * si_ref[...]
    kw = x[:, D_KW:D_KW + LANE]
    kir = kw * ci_ref[...] + _swap_halves(kw, IDX_DIM // 2) * si_ref[...]
    ki1 = jnp.where(lane < IDX_DIM, kir, 0.0)
    ki2 = ki1 + pltpu.roll(ki1, IDX_DIM, axis=1)
    ki4 = ki2 + pltpu.roll(ki2, 2 * IDX_DIM, axis=1)
    kihi_ref[0], kilo_ref[0] = _split_bf16(ki4)
    wi_ref[0] = pltpu.roll(kw, LANE - IDX_DIM, axis=1) * (IDX_HEADS ** -0.5 * IDX_DIM ** -0.5)


def _rope_tables(S, dim, width):
    half = dim // 2
    inv = ROPE_THETA ** (-jnp.arange(half, dtype=F32) / half)
    ang = jnp.arange(S, dtype=F32)[:, None] * inv[None, :]
    cos = jnp.tile(jnp.cos(ang), (1, width // half))
    sin = jnp.tile(jnp.concatenate([-jnp.sin(ang), jnp.sin(ang)], axis=1), (1, width // dim))
    return cos, sin


def _dsa_prep(p, qn_g, kn_g):
    B, S, _ = p.shape
    GW, R = GROUP_W, DSA_PREP_ROWS
    cq, sq = _rope_tables(S, HEAD_DIM, GW)
    ci, si = _rope_tables(S, IDX_DIM, LANE)
    gq = jnp.tile(qn_g, N_HEADS)[None]
    gk = jnp.pad(kn_g, (0, LANE - HEAD_DIM))[None]
    rows = lambda w: pl.BlockSpec((1, R, w), lambda b, i: (b, i, 0))
    tab = lambda w: pl.BlockSpec((R, w), lambda b, i: (i, 0))
    cols = lambda r: pl.BlockSpec((1, r, R), lambda b, i: (b, 0, i))
    vec = lambda w: pl.BlockSpec((1, w), lambda b, i: (0, 0))
    return pl.pallas_call(
        _dsa_prep_kernel,
        grid=(B, S // R),
        in_specs=[rows(D_PAD), tab(GW), tab(GW), tab(LANE), tab(LANE), vec(GW), vec(LANE)],
        out_specs=[rows(GW), rows(GW), cols(GW), rows(LANE), rows(LANE), rows(LANE), rows(LANE)],
        out_shape=[jax.ShapeDtypeStruct((B, S, GW), F32), jax.ShapeDtypeStruct((B, S, GW), BF16),
                   jax.ShapeDtypeStruct((B, GW, S), BF16), jax.ShapeDtypeStruct((B, S, LANE), F32),
                   jax.ShapeDtypeStruct((B, S, LANE), BF16), jax.ShapeDtypeStruct((B, S, LANE), BF16),
                   jax.ShapeDtypeStruct((B, S, LANE), F32)],
        compiler_params=pltpu.CompilerParams(dimension_semantics=("parallel", "parallel"),
                                             vmem_limit_bytes=VMEM_LIMIT),
        name="dsa_prep",
    )(p, cq, sq, ci, si, gq, gk)


def _dsa_attn_norm(qd, kd4, vdwt, qi, ki_hi, ki_lo, wi, g):
    B, S, _ = qd.shape
    n_sel = min(TOPK_MAX, S // 4)
    assert S % DSA_KEY_STEP == 0 and n_sel <= DSA_KEY_STEP
    kls = tuple(range(DSA_KEY_STEP, S + 1, DSA_KEY_STEP))
    blk = lambda w: pl.BlockSpec((1, DSA_BLOCK, w), lambda b, i: (b, i, 0))
    per_b = lambda r, c: pl.BlockSpec((1, r, c), lambda b, i: (b, 0, 0))
    return pl.pallas_call(
        functools.partial(_dsa_kernel, kls=kls, n_sel=n_sel),
        grid=(B, S // DSA_BLOCK),
        in_specs=[blk(GROUP_W), per_b(S, GROUP_W), per_b(GROUP_W, S), blk(IDX_HEADS * IDX_DIM),
                  per_b(S, LANE), per_b(S, LANE), blk(LANE), pl.BlockSpec((1, GROUP_W), lambda b, i: (0, 0))],
        out_specs=blk(GROUP_W),
        out_shape=jax.ShapeDtypeStruct((B, S, GROUP_W), F32),
        compiler_params=pltpu.CompilerParams(dimension_semantics=("parallel", "parallel"),
                                             vmem_limit_bytes=VMEM_LIMIT),
        name="dsa_attn",
    )(qd, kd4, vdwt, qi, ki_hi, ki_lo, wi, g[None])


RWKV_CHUNK = 64
RWKV_LOW = RWKV_W_RANK + RWKV_A_RANK + RWKV_G_RANK
RWKV_GROUP = N_HEADS
RWKV_BATCH = 2


def _dot(a, b):
    return jnp.dot(a, b, preferred_element_type=F32)


def _dot_nt(a, b):
    return lax.dot_general(a, b, (((1,), (1,)), ((), ())), preferred_element_type=F32)


def _dot_tn(a, b):
    return lax.dot_general(a, b, (((0,), (0,)), ((), ())), preferred_element_type=F32)


def _split3_bf16(t):
    p1 = t.astype(BF16)
    r1 = t - p1.astype(F32)
    p2 = r1.astype(BF16)
    p3 = (r1 - p2.astype(F32)).astype(BF16)
    return p1, p2, p3


def _dot_f32_by_exact(a, b_exact):
    m = a.shape[0]
    r = _dot(jnp.concatenate(_split3_bf16(a), axis=0), b_exact)
    return r[:m] + r[m:2 * m] + r[2 * m:]


def _dot_exact_by_f32(a_exact, b):
    n = b.shape[1]
    r = _dot(a_exact, jnp.concatenate(_split3_bf16(b), axis=1))
    return r[:, :n] + r[:, n:2 * n] + r[:, 2 * n:]


def _dot3(a, b_hi, b_lo):
    a_hi, a_lo = _split_bf16(a)
    return _dot(jnp.concatenate([a_hi, a_lo, a_hi], axis=1), jnp.concatenate([b_hi, b_hi, b_lo], axis=0))


def _softplus(z):
    return jnp.maximum(z, 0.0) + jnp.log(1.0 + jnp.exp(-jnp.abs(z)))


def _rwkv_kernel(p_ref, mu_ref, vec_ref, lhi_ref, llo_ref, o_ref, state_ref, prev_ref):
    @pl.when(pl.program_id(1) == 0)
    def _():
        state_ref[...] = jnp.zeros_like(state_ref)
        prev_ref[...] = jnp.zeros_like(prev_ref)

    for n in range(RWKV_BATCH):
        _rwkv_chunk(p_ref.at[n], mu_ref, vec_ref, lhi_ref, llo_ref, o_ref.at[n], state_ref.at[n], prev_ref.at[n])


def _rwkv_chunk(p_ref, mu_ref, vec_ref, lhi_ref, llo_ref, o_ref, state_ref, prev_ref):
    L, GW = RWKV_CHUNK, GROUP_W
    p = p_ref[...]
    row = lax.broadcasted_iota(jnp.int32, p.shape, 0)
    prev = jnp.where(row == 0, prev_ref[...], pltpu.roll(p, 1, axis=0))
    prev_ref[...] = p[L - 1:L, :]
    ps = p + (prev - p) * mu_ref[...]
    r, k, v = ps[:, :GW], ps[:, GW:2 * GW], ps[:, 2 * GW:3 * GW]
    low = ps[:, 3 * GW:]
    lane_low = lax.broadcasted_iota(jnp.int32, low.shape, 1)
    low = jnp.where(lane_low < RWKV_W_RANK, jnp.tanh(low),
                    jnp.where(lane_low < RWKV_W_RANK + RWKV_A_RANK, low, jax.nn.sigmoid(low)))
    up = _dot3(low, lhi_ref[...], llo_ref[...])
    w0, a0, k_k, k_a = vec_ref[0:1, :], vec_ref[1:2, :], vec_ref[2:3, :], vec_ref[3:4, :]
    r_k, ln_g, ln_b = vec_ref[4:5, :], vec_ref[5:6, :], vec_ref[6:7, :]
    logw = -jnp.exp(-_softplus(-(w0 + up[:, :GW])) - 0.5)
    rate = jax.nn.sigmoid(a0 + up[:, GW:2 * GW])
    gate = up[:, 2 * GW:]

    ri = lax.broadcasted_iota(jnp.int32, (GW, GW), 0)
    ci = lax.broadcasted_iota(jnp.int32, (GW, GW), 1)
    ones_bd = jnp.where((ri // HEAD_DIM) == (ci // HEAD_DIM), 1.0, 0.0).astype(BF16)

    kk = k * k_k
    k = k * (1.0 + (rate - 1.0) * k_a)
    seg = _dot_f32_by_exact(jnp.concatenate([kk * kk, r * k * r_k], axis=0), ones_bd)
    kk = kk / jnp.maximum(jnp.sqrt(seg[:L]), 1e-12)

    ti = lax.broadcasted_iota(jnp.int32, (L, L), 0)
    tj = lax.broadcasted_iota(jnp.int32, (L, L), 1)
    lc = _dot_exact_by_f32(jnp.where(tj <= ti, 1.0, 0.0).astype(BF16), logw)
    lc_last = lc[L - 1:L, :]
    dec_in = jnp.exp(lc)
    dec_out = jnp.exp(-lc)
    a_t = -kk * jnp.exp(lc - logw)
    b_t = kk * rate * dec_out
    k_t = k * dec_out
    r_t = r * dec_in
    to_end = jnp.exp(lc_last)

    SG = RWKV_GROUP * L
    gi = lax.broadcasted_iota(jnp.int32, (SG, SG), 0)
    gj = lax.broadcasted_iota(jnp.int32, (SG, SG), 1)
    g_same = (gi // L) == (gj // L)
    strict = g_same & ((gj % L) < (gi % L))
    incl = g_same & ((gj % L) <= (gi % L))
    eye = jnp.where(gi == gj, 1.0, 0.0)
    stack = lambda t: jnp.concatenate([t] * RWKV_GROUP, axis=0)
    bd = lambda t: jnp.where(g_same, stack(t), 0.0).astype(BF16)
    n_doublings = RWKV_CHUNK.bit_length() - 2
    ys = []
    for g in range(N_HEADS // RWKV_GROUP):
        cols = slice(g * SG, (g + 1) * SG)
        a_bd, r_bd, v_bd = bd(a_t[:, cols]), bd(r_t[:, cols]), bd(v[:, cols])
        m = _dot_nt(jnp.concatenate([a_bd, r_bd], axis=0),
                    jnp.concatenate([stack(b_t[:, cols]), stack(k_t[:, cols])], axis=0).astype(BF16))
        m_ab = jnp.where(strict, m[:SG, :SG], 0.0)
        m_ak = jnp.where(strict, m[:SG, SG:], 0.0).astype(BF16)
        m_rb = jnp.where(incl, m[SG:, :SG], 0.0).astype(BF16)
        m_rk = jnp.where(incl, m[SG:, SG:], 0.0).astype(BF16)

        inv = eye + m_ab
        pw_b = m_ab.astype(BF16)
        pw_b = _dot(pw_b, pw_b).astype(BF16)
        for s in range(n_doublings - 1):
            both = _dot(jnp.concatenate([inv.astype(BF16), pw_b], axis=0), pw_b)
            inv = inv + both[:SG]
            pw_b = both[SG:].astype(BF16)
        inv_b = (inv + _dot(inv.astype(BF16), pw_b)).astype(BF16)

        t0 = state_ref[g]
        t0_b = t0.astype(BF16)
        rhs = _dot(jnp.concatenate([a_bd, m_ak], axis=1), jnp.concatenate([t0_b, v_bd], axis=0))
        u = _dot(inv_b, rhs.astype(BF16)).astype(BF16)
        y_bd = _dot(jnp.concatenate([r_bd, m_rb, m_rk], axis=1),
                    jnp.concatenate([t0_b, u, v_bd], axis=0))
        ys.append(sum(y_bd[h * L:(h + 1) * L, :] for h in range(RWKV_GROUP)))

        end_g = to_end[:, cols]
        to_end_col = jnp.sum(jnp.where(gi == gj, jnp.broadcast_to(end_g, (SG, SG)), 0.0), axis=1, keepdims=True)
        state_ref[g] = to_end_col * t0 + _dot_tn(
            jnp.concatenate([bd(b_t[:, cols] * end_g), bd(k_t[:, cols] * end_g)], axis=0),
            jnp.concatenate([u, v_bd], axis=0))
    y = jnp.concatenate(ys, axis=1)

    inv_d = 1.0 / HEAD_DIM
    mean = _dot_f32_by_exact(y, ones_bd) * inv_d
    yc = y - mean
    var = _dot_f32_by_exact(yc * yc, ones_bd) * inv_d
    yn = yc * lax.rsqrt(var + RWKV_GN_EPS) * ln_g + ln_b
    o_ref[...] = (yn + seg[L:] * v) * gate


def _rwkv7_time_mix(p, mu, w0, w2, a0, a2, g2, k_k, k_a, r_k, ln_g, ln_b):
    B, S, _ = p.shape
    GW = GROUP_W
    assert S % RWKV_CHUNK == 0 and RWKV_CHUNK == HEAD_DIM and B % RWKV_BATCH == 0
    low_w = jnp.zeros((RWKV_LOW, 3 * GW), F32)
    low_w = low_w.at[:RWKV_W_RANK, :GW].set(w2)
    low_w = low_w.at[RWKV_W_RANK:RWKV_W_RANK + RWKV_A_RANK, GW:2 * GW].set(a2)
    low_w = low_w.at[RWKV_W_RANK + RWKV_A_RANK:, 2 * GW:].set(g2)
    l_hi, l_lo = _split_bf16(low_w)
    vecs = jnp.stack([w0, a0, k_k, k_a, r_k, ln_g, ln_b, jnp.zeros_like(w0)], 0)
    full = lambda a: pl.BlockSpec(a.shape, lambda b, c: (0,) * a.ndim)
    mu2 = mu[None]
    return pl.pallas_call(
        _rwkv_kernel,
        grid=(B // RWKV_BATCH, S // RWKV_CHUNK),
        in_specs=[pl.BlockSpec((RWKV_BATCH, RWKV_CHUNK, A_PAD), lambda b, c: (b, c, 0)),
                  full(mu2), full(vecs), full(l_hi), full(l_lo)],
        out_specs=pl.BlockSpec((RWKV_BATCH, RWKV_CHUNK, GW), lambda b, c: (b, c, 0)),
        out_shape=jax.ShapeDtypeStruct((B, S, GW), F32),
        scratch_shapes=[pltpu.VMEM((RWKV_BATCH, N_HEADS // RWKV_GROUP, RWKV_GROUP * RWKV_CHUNK,
                                    RWKV_GROUP * HEAD_DIM), F32),
                        pltpu.VMEM((RWKV_BATCH, 1, A_PAD), F32)],
        compiler_params=pltpu.CompilerParams(dimension_semantics=("parallel", "arbitrary"),
                                             vmem_limit_bytes=VMEM_LIMIT),
        name="rwkv7",
    )(p, mu2, vecs, l_hi, l_lo)


def _sb_kernel(q_ref, k_ref, v_ref, g_ref, o_ref, kbd_ref, vbd_ref):
    i = pl.program_id(1)
    T, GW, H = SB_BLOCK, GROUP_W, N_HEADS
    lane_h = lax.broadcasted_iota(jnp.int32, (T, GW), 1) // HEAD_DIM
    k_new, v_new = k_ref[0], v_ref[0]
    for h in range(H):
        kbd_ref[i, h * T:(h + 1) * T, :] = jnp.where(lane_h == h, k_new, 0.0).astype(BF16)
        vbd_ref[i, h * T:(h + 1) * T, :] = jnp.where(lane_h == h, v_new, 0.0).astype(BF16)

    q = q_ref[0].astype(BF16)
    si = lax.broadcasted_iota(jnp.int32, (T, 2 * T), 0)
    sj = lax.broadcasted_iota(jnp.int32, (T, 2 * T), 1)
    later_and_all = jnp.where((si > sj) | (sj >= T), 1.0, 0.0).astype(BF16)
    qrow = lax.broadcasted_iota(jnp.int32, (T, H * T), 0)
    kcol = lax.broadcasted_iota(jnp.int32, (T, H * T), 1) % T
    causal = kcol < qrow

    def key_block(j, state, diagonal):
        carry, acc = state
        z = _dot_nt(q, kbd_ref[j]) * HEAD_DIM ** -0.5
        soft = jnp.log(1.0 + jnp.exp(-jnp.abs(z)))
        log1m = -(jnp.maximum(z, 0.0) + soft)
        log_sig = jnp.minimum(z, 0.0) - soft
        log1m_in = (jnp.where(causal, log1m, 0.0) if diagonal else log1m).astype(BF16)
        sums = _dot(jnp.concatenate([log1m_in[:, h * T:(h + 1) * T] for h in range(H)], axis=0), later_and_all)
        suffix = jnp.concatenate([sums[h * T:(h + 1) * T, :T] for h in range(H)], axis=1)
        total = jnp.concatenate([sums[h * T:(h + 1) * T, T:] for h in range(H)], axis=1)
        att = jnp.exp(log_sig + suffix + carry)
        if diagonal:
            att = jnp.where(causal, att, 0.0)
        return carry + total, acc + _dot(att.astype(BF16), vbd_ref[j])

    state = key_block(i, (jnp.zeros((T, H * T), F32), jnp.zeros((T, GW), F32)), True)
    odd = i % 2
    state = lax.fori_loop(0, odd, lambda it, st: key_block(i - 1, st, False), state)
    top = i - 1 - odd

    def two_blocks(it, st):
        return key_block(top - 2 * it - 1, key_block(top - 2 * it, st, False), False)

    _, y = lax.fori_loop(0, i // 2, two_blocks, state)

    hi_ = lax.broadcasted_iota(jnp.int32, (GW, GW), 0) // HEAD_DIM
    hj_ = lax.broadcasted_iota(jnp.int32, (GW, GW), 1) // HEAD_DIM
    ones_bd = jnp.where(hi_ == hj_, 1.0, 0.0).astype(BF16)
    ms = _dot_f32_by_exact(y * y, ones_bd) * (1.0 / HEAD_DIM)
    o_ref[0] = y * lax.rsqrt(ms + NORM_EPS) * g_ref[...]


def _stick_breaking_norm(p, g):
    B, S, _ = p.shape
    GW = GROUP_W
    assert S % SB_BLOCK == 0
    col = lambda n: pl.BlockSpec((1, SB_BLOCK, GW), lambda b, i: (b, i, n))
    blk = col(0)
    q = k = v = p
    stacked = pltpu.VMEM((S // SB_BLOCK, N_HEADS * SB_BLOCK, GW), BF16)
    return pl.pallas_call(
        _sb_kernel,
        grid=(B, S // SB_BLOCK),
        in_specs=[col(0), col(1), col(2), pl.BlockSpec((1, GW), lambda b, i: (0, 0))],
        out_specs=blk,
        out_shape=jax.ShapeDtypeStruct((B, S, GW), F32),
        scratch_shapes=[stacked, stacked],
        compiler_params=pltpu.CompilerParams(dimension_semantics=("parallel", "arbitrary"),
                                             vmem_limit_bytes=VMEM_LIMIT),
        name="stick_breaking",
    )(q, k, v, g[None])


ML_HALO = 8
ML_GROUP = N_HEADS


def _dot_nt_exact_by_f32(a_exact, b):
    return _dot_nt(jnp.concatenate([a_exact] * 3, axis=1), jnp.concatenate(_split3_bf16(b), axis=1))


def _mlstm_kernel(p_ref, cw_ref, cb_ref, gb_ref, g_ref, o_ref, ext_ref, ct_ref, n_ref, m_ref):
    L, GW, H = ML_CHUNK, GROUP_W, N_HEADS

    @pl.when(pl.program_id(1) == 0)
    def _():
        ext_ref[...] = jnp.zeros_like(ext_ref)
        ct_ref[...] = jnp.zeros_like(ct_ref)
        n_ref[...] = jnp.zeros_like(n_ref)
        m_ref[...] = jnp.zeros_like(m_ref)

    x = p_ref[0]
    ext_ref[ML_HALO:, :] = x[:, :2 * GW]
    conv = cb_ref[...]
    for j in range(ML_CONV):
        conv = conv + cw_ref[j:j + 1, :] * ext_ref[pl.ds(ML_HALO - (ML_CONV - 1) + j, L), :]
    ext_ref[:ML_HALO, :] = x[L - ML_HALO:, :2 * GW]
    qk = conv * jax.nn.sigmoid(conv)
    q, k = qk[:, :GW], qk[:, GW:] * HEAD_DIM ** -0.5
    v, o = x[:, 2 * GW:3 * GW], x[:, 3 * GW:4 * GW]

    gates = x[:, 4 * GW:]
    gi = lax.broadcasted_iota(jnp.int32, (LANE, 2 * GW), 0)
    gj = lax.broadcasted_iota(jnp.int32, (LANE, 2 * GW), 1)
    expand = jnp.where(gi == (gj % GW) // HEAD_DIM + H * (gj // GW), 1.0, 0.0).astype(BF16)
    graw = _dot_f32_by_exact(gates, expand) + gb_ref[...]
    capped = GATE_CAP * jnp.tanh(graw * (1.0 / GATE_CAP))
    log_i = capped[:, :GW]
    cf = capped[:, GW:]
    log_f = jnp.minimum(cf, 0.0) - jnp.log(1.0 + jnp.exp(-jnp.abs(cf)))

    ti = lax.broadcasted_iota(jnp.int32, (L, L), 0)
    tj = lax.broadcasted_iota(jnp.int32, (L, L), 1)
    bf = _dot_exact_by_f32(jnp.where(tj <= ti, 1.0, 0.0).astype(BF16), log_f)
    b_last = bf[L - 1:L, :]
    m_row, n_row = m_ref[...], n_ref[...]
    dec = b_last - bf + log_i
    m_new = jnp.maximum(b_last + m_row, jnp.max(dec, axis=0, keepdims=True))
    kw = k * jnp.exp(dec - m_new)
    s_old = jnp.exp(b_last + m_row - m_new)
    n_ref[...] = s_old * n_row + jnp.sum(kw, axis=0, keepdims=True)
    m_ref[...] = m_new
    g_in = bf + m_row
    li_b = log_i - bf

    SG = ML_GROUP * L
    ri = lax.broadcasted_iota(jnp.int32, (SG, SG), 0)
    ci = lax.broadcasted_iota(jnp.int32, (SG, SG), 1)
    same_head = (ri // L) == (ci // HEAD_DIM)
    first_lane = ci == (ri // L) * HEAD_DIM
    sel_first = jnp.where(first_lane, 1.0, 0.0).astype(BF16)
    stack = lambda t: jnp.concatenate([t] * ML_GROUP, axis=0)
    pick = lambda t: jnp.sum(jnp.where(first_lane, t, 0.0), axis=1, keepdims=True)
    rt = lax.broadcasted_iota(jnp.int32, (SG, L), 0) % L
    rs = lax.broadcasted_iota(jnp.int32, (SG, L), 1)
    hs = []
    for g in range(H // ML_GROUP):
        cols = slice(g * SG, (g + 1) * SG)
        ct = ct_ref[g]
        qs = jnp.where(same_head, stack(q[:, cols]), 0.0)
        qs_b = qs.astype(BF16)
        v_b = v[:, cols].astype(BF16)
        b_col = pick(stack(bf[:, cols]))
        g_col = pick(stack(g_in[:, cols]))
        row_part = _dot_nt_exact_by_f32(sel_first, li_b[:, cols])
        dmat = jnp.where(rs <= rt, b_col + row_part, -jnp.inf)
        m_t = jnp.maximum(g_col, jnp.max(dmat, axis=1, keepdims=True))
        s_inter = jnp.exp(g_col - m_t)
        sqk = _dot_nt(qs_b, k[:, cols].astype(BF16)) * jnp.exp(dmat - m_t)
        num = s_inter * _dot(qs_b, ct.astype(BF16)) + jnp.where(same_head, _dot(sqk.astype(BF16), v_b), 0.0)
        den = (s_inter * jnp.sum(qs * n_row[:, cols], axis=1, keepdims=True)
               + jnp.sum(sqk, axis=1, keepdims=True))
        hst = num / jnp.maximum(jnp.abs(den), jnp.exp(-m_t))
        hs.append(sum(hst[n * L:(n + 1) * L, :] for n in range(ML_GROUP)))
        ct_ref[g] = s_old[:, cols] * ct + jnp.where(same_head, _dot_tn(kw[:, cols].astype(BF16), v_b), 0.0)
    h = jnp.concatenate(hs, axis=1)

    hi_ = lax.broadcasted_iota(jnp.int32, (GW, GW), 0) // HEAD_DIM
    hj_ = lax.broadcasted_iota(jnp.int32, (GW, GW), 1) // HEAD_DIM
    ones_bd = jnp.where(hi_ == hj_, 1.0, 0.0).astype(BF16)
    ms = _dot_f32_by_exact(h * h, ones_bd) * (1.0 / HEAD_DIM)
    o_ref[0] = jax.nn.sigmoid(o) * (h * lax.rsqrt(ms + NORM_EPS) * g_ref[...])


def _mlstm_mix(p, conv_w, conv_b, ig_b, fg_b, norm_g):
    B, S, _ = p.shape
    GW = GROUP_W
    assert S % ML_CHUNK == 0 and ML_CONV - 1 <= ML_HALO <= ML_CHUNK
    gate_b = jnp.concatenate([jnp.repeat(ig_b, HEAD_DIM), jnp.repeat(fg_b, HEAD_DIM)])[None]
    full = lambda a: pl.BlockSpec(a.shape, lambda b, c: (0,) * a.ndim)
    cb2, g2 = conv_b[None], norm_g[None]
    return pl.pallas_call(
        _mlstm_kernel,
        grid=(B, S // ML_CHUNK),
        in_specs=[pl.BlockSpec((1, ML_CHUNK, C_PAD), lambda b, c: (b, c, 0)),
                  full(conv_w), full(cb2), full(gate_b), full(g2)],
        out_specs=pl.BlockSpec((1, ML_CHUNK, GW), lambda b, c: (b, c, 0)),
        out_shape=jax.ShapeDtypeStruct((B, S, GW), F32),
        scratch_shapes=[pltpu.VMEM((ML_HALO + ML_CHUNK, 2 * GW), F32),
                        pltpu.VMEM((N_HEADS // ML_GROUP, ML_GROUP * ML_CHUNK, ML_GROUP * HEAD_DIM), F32),
                        pltpu.VMEM((1, GW), F32), pltpu.VMEM((1, GW), F32)],
        compiler_params=pltpu.CompilerParams(dimension_semantics=("parallel", "arbitrary"),
                                             vmem_limit_bytes=VMEM_LIMIT),
        name="mlstm",
    )(p, conv_w, cb2, gate_b, g2)


def _rms_norm(x, g):
    xf = x.astype(F32)
    y = xf * lax.rsqrt(jnp.mean(xf * xf, -1, keepdims=True) + NORM_EPS)
    return (y * g.astype(F32)).astype(x.dtype)


def _rope(x, pos):
    half = x.shape[-1] // 2
    inv = ROPE_THETA ** (-jnp.arange(half, dtype=F32) / half)
    ang = pos.astype(F32)[:, None] * inv[None, :]
    cos = jnp.cos(ang)[None, :, None, :]
    sin = jnp.sin(ang)[None, :, None, :]
    xf = x.astype(F32)
    x1, x2 = xf[..., :half], xf[..., half:]
    return jnp.concatenate([x1 * cos - x2 * sin, x2 * cos + x1 * sin], -1).astype(x.dtype)


def kernel(x, c, ada_w, ada_b, norm1_g, norm2_g, w_in, rk_mu, rk_w0, rk_w2, rk_a0, rk_a2, rk_g2, rk_kk, rk_ka, rk_rk, rk_ln_g, rk_ln_b, sb_norm_g, ml_conv_w, ml_conv_b, ml_ig_b, ml_fg_b, ml_norm_g, ds_qn_g, ds_kn_g, ds_out_g, w_out, moe_wg, moe_bg, moe_we, moe_be, moe_w1, moe_w3, moe_w2):
    B, S, D = x.shape
    H, d = N_HEADS, HEAD_DIM
    depth = ada_w.shape[0]
    pos = jnp.arange(S)
    c_act = jax.nn.silu(c)
    for l in range(depth):
        mod = (c_act @ ada_w[l] + ada_b[l])[:, None, :]
        sh1, sc1, gt1, sh2, sc2, gt2 = jnp.split(mod, 6, axis=-1)

        pA, pB, pC, pD = _in_proj(x, sc1, sh1, norm1_g[l][None], _pad_w_in(w_in[l]))

        yA = _rwkv7_time_mix(pA, rk_mu[l], rk_w0[l], rk_w2[l], rk_a0[l], rk_a2[l], rk_g2[l],
                             rk_kk[l], rk_ka[l], rk_rk[l], rk_ln_g[l], rk_ln_b[l])

        yB = _stick_breaking_norm(pB, sb_norm_g[l])

        yC = _mlstm_mix(pC, ml_conv_w[l], ml_conv_b[l], ml_ig_b[l], ml_fg_b[l], ml_norm_g[l])

        yD = _dsa_attn_norm(*_dsa_prep(pD, ds_qn_g[l], ds_kn_g[l]), ds_out_g[l])

        router = jnp.pad(jnp.concatenate([moe_wg[l], moe_we[l]], 1),
                         ((0, 0), (0, ROUTER_PAD - N_GROUPS - N_EXPERTS)))
        r_hi, r_lo = _split_bf16(router)
        r_b = jnp.pad(jnp.concatenate([moe_bg[l], moe_be[l]]), (0, ROUTER_PAD - N_GROUPS - N_EXPERTS))[None]
        x1, h2, route = _out_proj((yA, yB, yC, yD), x, gt1, sc2, sh2, norm2_g[l][None],
                                  w_out[l].astype(BF16), r_hi, r_lo, r_b)

        moe = _hier_moe(h2.reshape(B * S, D), route.reshape(B * S, ROUTER_PAD),
                        moe_w1[l], moe_w3[l], moe_w2[l])
        x = x1 + gt2 * moe.reshape(B, S, D)
    return x
```

```python
import functools
import itertools

import jax
import jax.numpy as jnp
import numpy as np
from jax import lax
from jax.experimental import pallas as pl
from jax.experimental.pallas import tpu as pltpu

F32 = jnp.float32
BF16 = jnp.bfloat16

D_MODEL = 1024
N_MIXERS = 4
GROUP_W = D_MODEL // N_MIXERS
HEAD_DIM = 64
N_HEADS = GROUP_W // HEAD_DIM
NORM_EPS = 1e-6
RWKV_W_RANK = 32
RWKV_A_RANK = 32
RWKV_G_RANK = 64
RWKV_GN_EPS = 64e-5
SB_BLOCK = 128
ML_CHUNK = 64
ML_CONV = 4
GATE_CAP = 15.0
DSA_BLOCK = 128
IDX_HEADS = 4
IDX_DIM = 32
TOPK_MAX = 256
ROPE_THETA = 10000.0
N_GROUPS = 4
EXP_PER_GROUP = 8
N_EXPERTS = N_GROUPS * EXP_PER_GROUP
EXPERT_FF = D_MODEL // 2
TOP_IN_GROUP = 2

A_SIZES = (GROUP_W, GROUP_W, GROUP_W, RWKV_W_RANK, RWKV_A_RANK, RWKV_G_RANK)
B_SIZES = (GROUP_W, GROUP_W, GROUP_W)
C_SIZES = (GROUP_W, GROUP_W, GROUP_W, GROUP_W, N_HEADS, N_HEADS)
D_SIZES = (GROUP_W, HEAD_DIM, HEAD_DIM, IDX_HEADS * IDX_DIM, IDX_DIM, IDX_HEADS)
A_COLS = sum(A_SIZES)
B_COLS = sum(B_SIZES)
C_COLS = sum(C_SIZES)
D_COLS = sum(D_SIZES)

LANE = 128
A_PAD = 896
B_PAD = 768
C_PAD = 1152
D_PAD = 640
P_PAD = A_PAD + B_PAD + C_PAD + D_PAD
ROUTER_PAD = LANE

IN_ROWS = 256
OUT_ROWS = 512
MOE_ROWS = 512
VMEM_LIMIT = 48 * 1024 * 1024


def _split_cols(t, sizes):
    return jnp.split(t, [int(i) for i in np.cumsum(sizes)[:-1]], axis=-1)


def _in_proj_kernel(x_ref, sc_ref, sh_ref, g_ref, w_ref, oa_ref, ob_ref, oc_ref, od_ref):
    x = x_ref[0]
    y = x * lax.rsqrt(jnp.mean(x * x, -1, keepdims=True) + NORM_EPS) * g_ref[...]
    h = y * (1.0 + sc_ref[0]) + sh_ref[0]
    p = jnp.dot(h.astype(BF16), w_ref[...], preferred_element_type=F32)
    oa_ref[0] = p[:, :A_PAD]
    ob_ref[0] = p[:, A_PAD:A_PAD + B_PAD]
    oc_ref[0] = p[:, A_PAD + B_PAD:A_PAD + B_PAD + C_PAD]
    od_ref[0] = p[:, A_PAD + B_PAD + C_PAD:]


def _in_proj(x, sc, sh, g, w_pad):
    B, S, D = x.shape
    row = lambda w: pl.BlockSpec((1, IN_ROWS, w), lambda b, i: (b, i, 0))
    vec = pl.BlockSpec((1, 1, D), lambda b, i: (b, 0, 0))
    return pl.pallas_call(
        _in_proj_kernel,
        grid=(B, S // IN_ROWS),
        in_specs=[row(D), vec, vec, pl.BlockSpec((1, D), lambda b, i: (0, 0)),
                  pl.BlockSpec((D, P_PAD), lambda b, i: (0, 0))],
        out_specs=[row(A_PAD), row(B_PAD), row(C_PAD), row(D_PAD)],
        out_shape=[jax.ShapeDtypeStruct((B, S, w), F32) for w in (A_PAD, B_PAD, C_PAD, D_PAD)],
        compiler_params=pltpu.CompilerParams(dimension_semantics=("parallel", "parallel"),
                                             vmem_limit_bytes=VMEM_LIMIT),
        name="in_proj",
    )(x, sc, sh, g, w_pad)


def _pad_w_in(w):
    wa, wb, wc, wd = _split_cols(w, (A_COLS, B_COLS, C_COLS, D_COLS))
    padc = lambda t, n: jnp.pad(t, ((0, 0), (0, n - t.shape[1])))
    return jnp.concatenate([padc(wa, A_PAD), padc(wb, B_PAD), padc(wc, C_PAD), padc(wd, D_PAD)], 1).astype(BF16)


def _split_bf16(t):
    hi = t.astype(BF16)
    lo = (t - hi.astype(F32)).astype(BF16)
    return hi, lo


def _out_proj_kernel(ya_ref, yb_ref, yc_ref, yd_ref, x_ref, gt_ref, sc_ref, sh_ref, g_ref, w_ref,
                     rhi_ref, rlo_ref, rb_ref, x1_ref, h2_ref, route_ref):
    acc = jnp.zeros(x_ref.shape[1:], F32)
    for n, y_ref in enumerate((ya_ref, yb_ref, yc_ref, yd_ref)):
        acc += jnp.dot(y_ref[0].astype(BF16), w_ref[n * GROUP_W:(n + 1) * GROUP_W, :],
                       preferred_element_type=F32)
    x1 = x_ref[0] + gt_ref[0] * acc
    x1_ref[0] = x1
    y = x1 * lax.rsqrt(jnp.mean(x1 * x1, -1, keepdims=True) + NORM_EPS) * g_ref[...]
    h = y * (1.0 + sc_ref[0]) + sh_ref[0]
    hi, lo = _split_bf16(h)
    h2_ref[0] = hi
    lg = (jnp.dot(hi, rhi_ref[...], preferred_element_type=F32)
          + jnp.dot(lo, rhi_ref[...], preferred_element_type=F32)
          + jnp.dot(hi, rlo_ref[...], preferred_element_type=F32)) + rb_ref[...]
    route_ref[0] = _route(lg)


def _route(lg):
    lane = lax.broadcasted_iota(jnp.int32, lg.shape, 1)
    neg = -jnp.inf
    first = lambda hit: jnp.min(jnp.where(hit, lane, ROUTER_PAD), axis=1, keepdims=True)
    is_grp = lane < N_GROUPS
    grp = jnp.where(is_grp, lg, neg)
    g_max = jnp.max(grp, axis=1, keepdims=True)
    g_p = 1.0 / jnp.sum(jnp.where(is_grp, jnp.exp(grp - g_max), 0.0), axis=1, keepdims=True)
    g_idx = first(grp == g_max)
    e_lane = lane - N_GROUPS
    in_group = (e_lane >= 0) & (e_lane < N_EXPERTS) & (e_lane // EXP_PER_GROUP == g_idx)
    e_log = jnp.where(in_group, lg, neg)
    e1_max = jnp.max(e_log, axis=1, keepdims=True)
    e1_lane = first(e_log == e1_max)
    e_log2 = jnp.where(lane == e1_lane, neg, e_log)
    e2_max = jnp.max(e_log2, axis=1, keepdims=True)
    e2_lane = first(e_log2 == e2_max)
    ratio = jnp.exp(e2_max - e1_max)
    gate1 = g_p / (1.0 + ratio)
    gate2 = gate1 * ratio
    out = jnp.where(lane == 0, (e1_lane - N_GROUPS).astype(F32), 0.0)
    out = jnp.where(lane == 1, (e2_lane - N_GROUPS).astype(F32), out)
    out = jnp.where(lane == 2, gate1, out)
    return jnp.where(lane == 3, gate2, out)


def _out_proj(ys, x, gt, sc, sh, g, w_out, r_hi, r_lo, r_b):
    B, S, D = x.shape
    row = lambda w: pl.BlockSpec((1, OUT_ROWS, w), lambda b, i: (b, i, 0))
    vec = pl.BlockSpec((1, 1, D), lambda b, i: (b, 0, 0))
    full = lambda a: pl.BlockSpec(a.shape, lambda b, i: (0,) * a.ndim)
    return pl.pallas_call(
        _out_proj_kernel,
        grid=(B, S // OUT_ROWS),
        in_specs=[row(GROUP_W)] * 4 + [row(D), vec, vec, vec, full(g), full(w_out), full(r_hi), full(r_lo),
                                        full(r_b)],
        out_specs=[row(D), row(D), row(ROUTER_PAD)],
        out_shape=[jax.ShapeDtypeStruct((B, S, D), F32), jax.ShapeDtypeStruct((B, S, D), BF16),
                   jax.ShapeDtypeStruct((B, S, ROUTER_PAD), F32)],
        compiler_params=pltpu.CompilerParams(dimension_semantics=("parallel", "parallel"),
                                             vmem_limit_bytes=VMEM_LIMIT),
        name="out_proj",
    )(*ys, x, gt, sc, sh, g, w_out, r_hi, r_lo, r_b)


def _moe_ffn_kernel(blk_e_ref, x_ref, wt_ref, w1_ref, w3_ref, w2_ref, o_ref, w1b_ref, w3b_ref, w2b_ref):
    i = pl.program_id(0)
    changed = jnp.logical_or(i == 0, blk_e_ref[i] != blk_e_ref[jnp.maximum(i - 1, 0)])

    @pl.when(changed)
    def _():
        w1b_ref[...] = w1_ref[0].astype(BF16)
        w3b_ref[...] = w3_ref[0].astype(BF16)
        w2b_ref[...] = w2_ref[0].astype(BF16)

    xb = x_ref[...]
    a = jnp.dot(xb, w1b_ref[...], preferred_element_type=F32)
    b = jnp.dot(xb, w3b_ref[...], preferred_element_type=F32)
    hmid = (a * jax.nn.sigmoid(a) * b).astype(BF16)
    y = jnp.dot(hmid, w2b_ref[...], preferred_element_type=F32)
    o_ref[...] = (y * wt_ref[...]).astype(o_ref.dtype)


def _moe_ffn(blk_e, xs, wt, w1, w3, w2):
    n_slots, D = xs.shape
    n_blocks = n_slots // MOE_ROWS
    FF = w1.shape[-1]
    return pl.pallas_call(
        _moe_ffn_kernel,
        grid_spec=pltpu.PrefetchScalarGridSpec(
            num_scalar_prefetch=1,
            grid=(n_blocks,),
            in_specs=[pl.BlockSpec((MOE_ROWS, D), lambda i, e: (i, 0)),
                      pl.BlockSpec((MOE_ROWS, 1), lambda i, e: (i, 0)),
                      pl.BlockSpec((1, D, FF), lambda i, e: (e[i], 0, 0)),
                      pl.BlockSpec((1, D, FF), lambda i, e: (e[i], 0, 0)),
                      pl.BlockSpec((1, FF, D), lambda i, e: (e[i], 0, 0))],
            out_specs=pl.BlockSpec((MOE_ROWS, D), lambda i, e: (i, 0)),
            scratch_shapes=[pltpu.VMEM((D, FF), BF16), pltpu.VMEM((D, FF), BF16), pltpu.VMEM((FF, D), BF16)],
        ),
        out_shape=jax.ShapeDtypeStruct((n_slots, D), BF16),
        compiler_params=pltpu.CompilerParams(dimension_semantics=("arbitrary",),
                                             vmem_limit_bytes=VMEM_LIMIT),
        name="moe_ffn",
    )(blk_e, xs, wt, w1, w3, w2)


def _hier_moe(h2, route, w1, w3, w2):
    N, D = h2.shape
    expert = route[:, :TOP_IN_GROUP].astype(jnp.int32)
    gate = route[:, TOP_IN_GROUP:2 * TOP_IN_GROUP]
    n_asg = N * TOP_IN_GROUP
    flat_e = expert.reshape(n_asg // LANE, LANE)
    order = jnp.argsort(flat_e.reshape(n_asg))
    rank = jnp.argsort(order).astype(jnp.int32)
    counts = jnp.sum(flat_e[None] == jnp.arange(N_EXPERTS)[:, None, None], axis=(1, 2)).astype(jnp.int32)
    start = jnp.cumsum(counts) - counts
    pad_counts = (counts + MOE_ROWS - 1) // MOE_ROWS * MOE_ROWS
    pad_end = jnp.cumsum(pad_counts)
    pad_start = pad_end - pad_counts
    shift = pad_start - start
    asg_shift = jnp.zeros_like(flat_e)
    for e in range(N_EXPERTS):
        asg_shift = jnp.where(flat_e == e, shift[e], asg_shift)
    asg_slot = (rank + asg_shift.reshape(n_asg)).reshape(N, TOP_IN_GROUP)
    n_blocks = n_asg // MOE_ROWS + N_EXPERTS
    blk_start = jnp.arange(n_blocks) * MOE_ROWS
    blk_e = jnp.minimum(jnp.sum(pad_end[None, :] <= blk_start[:, None], 1), N_EXPERTS - 1).astype(jnp.int32)
    blk_pos = blk_start - pad_start[blk_e]
    row = jnp.arange(MOE_ROWS)[None, :]
    slot_real = (blk_pos[:, None] + row < counts[blk_e][:, None]) & (blk_start[:, None] < pad_end[N_EXPERTS - 1])
    slot_src = jnp.where(slot_real, (start[blk_e] + blk_pos)[:, None] + row, 0).reshape(-1)
    slot_asg = order[slot_src]
    slot_real = slot_real.reshape(-1)
    slot_tok = jnp.where(slot_real, slot_asg // TOP_IN_GROUP, 0).astype(jnp.int32)
    slot_w = jnp.where(slot_real, gate.reshape(n_asg)[slot_asg], 0.0)
    yb = _moe_ffn(blk_e, h2[slot_tok], slot_w[:, None], w1, w3, w2)
    return yb[asg_slot[:, 0]].astype(F32) + yb[asg_slot[:, 1]].astype(F32)


INT_MIN = -2 ** 31
DSA_KEY_STEP = 512


def _float_order_key(x):
    bits = pltpu.bitcast(x, jnp.int32)
    bits = jnp.where(x == 0.0, 0, bits)
    return bits ^ ((bits >> 31) & 0x7FFFFFFF)


COL_PART = 64


def _col_reduce(x, reduce):
    part = reduce(x.reshape(x.shape[0] // COL_PART, COL_PART, x.shape[1]), axis=0)
    return reduce(part, axis=0, keepdims=True)


def _col_count(mask):
    return _col_reduce(jnp.where(mask, 1.0, 0.0), jnp.sum)


def _head_block_diag(t, group):
    n_heads = t.shape[0] // group
    row_h = lax.broadcasted_iota(jnp.int32, t.shape, 0) // group
    return jnp.concatenate([jnp.where(row_h == h, t, 0.0) for h in range(n_heads)], axis=1)


def _dsa_block_t(qd_ref, kd4_ref, vdwt_ref, qi_ref, kihi_ref, kilo_ref, wi_ref, g_ref, o_ref, *, kl, n_sel):
    Q = DSA_BLOCK
    q0 = pl.program_id(1) * Q
    w_hi, w_lo = _split_bf16(_head_block_diag(jnp.transpose(qi_ref[0]), IDX_DIM))
    k_hi, k_lo = kihi_ref[0, :kl, :], kilo_ref[0, :kl, :]
    sc = _dot(jnp.concatenate([k_hi, k_lo, k_hi], axis=1),
              jnp.concatenate([w_hi, w_hi, w_lo], axis=0))
    wit = jnp.transpose(wi_ref[0])
    score = sum(wit[h:h + 1, :] * jnp.maximum(sc[:, h * Q:(h + 1) * Q], 0.0) for h in range(IDX_HEADS))
    kidx = lax.broadcasted_iota(jnp.int32, (kl, Q), 0)
    qpos = q0 + lax.broadcasted_iota(jnp.int32, (kl, Q), 1)
    adm = kidx <= qpos
    key = _float_order_key(jnp.where(adm, score, -jnp.inf))

    def value_bit(it, tau):
        cand = tau | jnp.left_shift(jnp.int32(1), 31 - it)
        return jnp.where(_col_count(key >= (cand ^ INT_MIN)) >= n_sel, cand, tau)

    tau = lax.fori_loop(0, 32, value_bit, jnp.zeros((1, Q), jnp.int32)) ^ INT_MIN
    gt = key > tau
    eq = (key == tau) & adm
    need = n_sel - _col_count(gt)
    n_eq = _col_count(eq)

    def index_bits():
        def index_bit(it, bound):
            cand = bound | jnp.left_shift(jnp.int32(1), 11 - it)
            return jnp.where(_col_count(eq & (kidx < cand)) <= need, cand, bound)
        return lax.fori_loop(0, 12, index_bit, jnp.zeros((1, Q), jnp.int32))

    bound = lax.cond(jnp.max(n_eq - need) > 0.0, index_bits, lambda: jnp.full((1, Q), kl, jnp.int32))
    sel = gt | (eq & (kidx < bound))

    w_att = _head_block_diag(jnp.transpose(qd_ref[0]), HEAD_DIM).astype(BF16)
    lg = _dot(kd4_ref[0, :kl, :], w_att) * HEAD_DIM ** -0.5
    lg = jnp.where(jnp.concatenate([sel] * N_HEADS, axis=1), lg, -jnp.inf)
    p = jnp.exp(lg - _col_reduce(lg, jnp.max))
    out_t = _dot(vdwt_ref[0, :, :kl], p.astype(BF16)) / _col_reduce(p, jnp.sum)
    row_h = lax.broadcasted_iota(jnp.int32, (GROUP_W, Q), 0) // HEAD_DIM
    nat_t = sum(jnp.where(row_h == h, out_t[:, h * Q:(h + 1) * Q], 0.0) for h in range(N_HEADS))
    r = jnp.transpose(nat_t)
    hi_ = lax.broadcasted_iota(jnp.int32, (GROUP_W, GROUP_W), 0) // HEAD_DIM
    hj_ = lax.broadcasted_iota(jnp.int32, (GROUP_W, GROUP_W), 1) // HEAD_DIM
    ms = _dot_f32_by_exact(r * r, jnp.where(hi_ == hj_, 1.0, 0.0).astype(BF16)) * (1.0 / HEAD_DIM)
    o_ref[0] = r * lax.rsqrt(ms + NORM_EPS) * g_ref[...]


def _dsa_kernel(qd_ref, kd4_ref, vdwt_ref, qi_ref, kihi_ref, kilo_ref, wi_ref, g_ref, o_ref, *, kls, n_sel):
    blocks_per_step = DSA_KEY_STEP // DSA_BLOCK
    for j, kl in enumerate(kls):
        @pl.when(pl.program_id(1) // blocks_per_step == j)
        def _():
            _dsa_block_t(qd_ref, kd4_ref, vdwt_ref, qi_ref, kihi_ref, kilo_ref, wi_ref, g_ref, o_ref,
                         kl=kl, n_sel=n_sel)


DSA_PREP_ROWS = 256
D_Q, D_KV, D_QI, D_KW = 0, GROUP_W, GROUP_W + LANE, GROUP_W + 2 * LANE


def _swap_halves(x, half):
    n = x.shape[1]
    lane = lax.broadcasted_iota(jnp.int32, x.shape, 1)
    return jnp.where(lane % (2 * half) < half, pltpu.roll(x, n - half, axis=1), pltpu.roll(x, half, axis=1))


def _dsa_prep_kernel(p_ref, cq_ref, sq_ref, ci_ref, si_ref, gq_ref, gk_ref,
                     qd_ref, kd4_ref, vdwt_ref, qi_ref, kihi_ref, kilo_ref, wi_ref):
    GW = GROUP_W
    x = p_ref[0]
    hi_ = lax.broadcasted_iota(jnp.int32, (GW, GW), 0) // HEAD_DIM
    hj_ = lax.broadcasted_iota(jnp.int32, (GW, GW), 1) // HEAD_DIM
    ones_bd = jnp.where(hi_ == hj_, 1.0, 0.0).astype(BF16)
    q = x[:, D_Q:D_Q + GW]
    q = q * lax.rsqrt(_dot_f32_by_exact(q * q, ones_bd) * (1.0 / HEAD_DIM) + NORM_EPS) * gq_ref[...]
    qd_ref[0] = q * cq_ref[...] + _swap_halves(q, HEAD_DIM // 2) * sq_ref[...]
    kv = x[:, D_KV:D_KV + LANE]
    lane = lax.broadcasted_iota(jnp.int32, kv.shape, 1)
    is_k = lane < HEAD_DIM
    ms = jnp.sum(jnp.where(is_k, kv * kv, 0.0), axis=1, keepdims=True) * (1.0 / HEAD_DIM)
    kn = kv * lax.rsqrt(ms + NORM_EPS) * gk_ref[...]
    kr = kn * cq_ref[:, :LANE] + _swap_halves(kn, HEAD_DIM // 2) * sq_ref[:, :LANE]
    k2 = jnp.where(is_k, kr, pltpu.roll(kr, HEAD_DIM, axis=1))
    kd4_ref[0] = jnp.concatenate([k2] * (GW // LANE), axis=1).astype(BF16)
    v2 = jnp.where(is_k, pltpu.roll(kv, HEAD_DIM, axis=1), kv)
    v2t = jnp.transpose(v2)
    vdwt_ref[0] = jnp.concatenate([v2t] * (GW // LANE), axis=0).astype(BF16)
    qi = x[:, D_QI:D_QI + LANE]
    qi_ref[0] = qi * ci_ref[...] + _swap_halves(qi, IDX_DIM // 2) * si_ref[...]
    kw = x[:, D_KW:D_KW + LANE]
    kir = kw * ci_ref[...] + _swap_halves(kw, IDX_DIM // 2) * si_ref[...]
    ki1 = jnp.where(lane < IDX_DIM, kir, 0.0)
    ki2 = ki1 + pltpu.roll(ki1, IDX_DIM, axis=1)
    ki4 = ki2 + pltpu.roll(ki2, 2 * IDX_DIM, axis=1)
    kihi_ref[0], kilo_ref[0] = _split_bf16(ki4)
    wi_ref[0] = pltpu.roll(kw, LANE - IDX_DIM, axis=1) * (IDX_HEADS ** -0.5 * IDX_DIM ** -0.5)


def _rope_tables(S, dim, width):
    half = dim // 2
    inv = ROPE_THETA ** (-jnp.arange(half, dtype=F32) / half)
    ang = jnp.arange(S, dtype=F32)[:, None] * inv[None, :]
    cos = jnp.tile(jnp.cos(ang), (1, width // half))
    sin = jnp.tile(jnp.concatenate([-jnp.sin(ang), jnp.sin(ang)], axis=1), (1, width // dim))
    return cos, sin


def _dsa_prep(p, qn_g, kn_g):
    B, S, _ = p.shape
    GW, R = GROUP_W, DSA_PREP_ROWS
    cq, sq = _rope_tables(S, HEAD_DIM, GW)
    ci, si = _rope_tables(S, IDX_DIM, LANE)
    gq = jnp.tile(qn_g, N_HEADS)[None]
    gk = jnp.pad(kn_g, (0, LANE - HEAD_DIM))[None]
    rows = lambda w: pl.BlockSpec((1, R, w), lambda b, i: (b, i, 0))
    tab = lambda w: pl.BlockSpec((R, w), lambda b, i: (i, 0))
    cols = lambda r: pl.BlockSpec((1, r, R), lambda b, i: (b, 0, i))
    vec = lambda w: pl.BlockSpec((1, w), lambda b, i: (0, 0))
    return pl.pallas_call(
        _dsa_prep_kernel,
        grid=(B, S // R),
        in_specs=[rows(D_PAD), tab(GW), tab(GW), tab(LANE), tab(LANE), vec(GW), vec(LANE)],
        out_specs=[rows(GW), rows(GW), cols(GW), rows(LANE), rows(LANE), rows(LANE), rows(LANE)],
        out_shape=[jax.ShapeDtypeStruct((B, S, GW), F32), jax.ShapeDtypeStruct((B, S, GW), BF16),
                   jax.ShapeDtypeStruct((B, GW, S), BF16), jax.ShapeDtypeStruct((B, S, LANE), F32),
                   jax.ShapeDtypeStruct((B, S, LANE), BF16), jax.ShapeDtypeStruct((B, S, LANE), BF16),
                   jax.ShapeDtypeStruct((B, S, LANE), F32)],
        compiler_params=pltpu.CompilerParams(dimension_semantics=("parallel", "parallel"),
                                             vmem_limit_bytes=VMEM_LIMIT),
        name="dsa_prep",
    )(p, cq, sq, ci, si, gq, gk)


def _dsa_attn_norm(qd, kd4, vdwt, qi, ki_hi, ki_lo, wi, g):
    B, S, _ = qd.shape
    n_sel = min(TOPK_MAX, S // 4)
    assert S % DSA_KEY_STEP == 0 and n_sel <= DSA_KEY_STEP
    kls = tuple(range(DSA_KEY_STEP, S + 1, DSA_KEY_STEP))
    blk = lambda w: pl.BlockSpec((1, DSA_BLOCK, w), lambda b, i: (b, i, 0))
    per_b = lambda r, c: pl.BlockSpec((1, r, c), lambda b, i: (b, 0, 0))
    return pl.pallas_call(
        functools.partial(_dsa_kernel, kls=kls, n_sel=n_sel),
        grid=(B, S // DSA_BLOCK),
        in_specs=[blk(GROUP_W), per_b(S, GROUP_W), per_b(GROUP_W, S), blk(IDX_HEADS * IDX_DIM),
                  per_b(S, LANE), per_b(S, LANE), blk(LANE), pl.BlockSpec((1, GROUP_W), lambda b, i: (0, 0))],
        out_specs=blk(GROUP_W),
        out_shape=jax.ShapeDtypeStruct((B, S, GROUP_W), F32),
        compiler_params=pltpu.CompilerParams(dimension_semantics=("parallel", "parallel"),
                                             vmem_limit_bytes=VMEM_LIMIT),
        name="dsa_attn",
    )(qd, kd4, vdwt, qi, ki_hi, ki_lo, wi, g[None])


RWKV_CHUNK = 64
RWKV_LOW = RWKV_W_RANK + RWKV_A_RANK + RWKV_G_RANK
RWKV_GROUP = N_HEADS
RWKV_BATCH = 2


def _dot(a, b):
    return jnp.dot(a, b, preferred_element_type=F32)


def _dot_nt(a, b):
    return lax.dot_general(a, b, (((1,), (1,)), ((), ())), preferred_element_type=F32)


def _dot_tn(a, b):
    return lax.dot_general(a, b, (((0,), (0,)), ((), ())), preferred_element_type=F32)


def _split3_bf16(t):
    p1 = t.astype(BF16)
    r1 = t - p1.astype(F32)
    p2 = r1.astype(BF16)
    p3 = (r1 - p2.astype(F32)).astype(BF16)
    return p1, p2, p3


def _dot_f32_by_exact(a, b_exact):
    m = a.shape[0]
    r = _dot(jnp.concatenate(_split3_bf16(a), axis=0), b_exact)
    return r[:m] + r[m:2 * m] + r[2 * m:]


def _dot_exact_by_f32(a_exact, b):
    n = b.shape[1]
    r = _dot(a_exact, jnp.concatenate(_split3_bf16(b), axis=1))
    return r[:, :n] + r[:, n:2 * n] + r[:, 2 * n:]


def _dot3(a, b_hi, b_lo):
    a_hi, a_lo = _split_bf16(a)
    return _dot(jnp.concatenate([a_hi, a_lo, a_hi], axis=1), jnp.concatenate([b_hi, b_hi, b_lo], axis=0))


def _softplus(z):
    return jnp.maximum(z, 0.0) + jnp.log(1.0 + jnp.exp(-jnp.abs(z)))


def _rwkv_kernel(p_ref, mu_ref, vec_ref, lhi_ref, llo_ref, o_ref, state_ref, prev_ref):
    @pl.when(pl.program_id(1) == 0)
    def _():
        state_ref[...] = jnp.zeros_like(state_ref)
        prev_ref[...] = jnp.zeros_like(prev_ref)

    chains = [_rwkv_chunk(p_ref.at[n], mu_ref, vec_ref, lhi_ref, llo_ref, o_ref.at[n], state_ref.at[n],
                          prev_ref.at[n]) for n in range(RWKV_BATCH)]
    for _ in itertools.zip_longest(*chains):
        pass


def _rwkv_chunk(p_ref, mu_ref, vec_ref, lhi_ref, llo_ref, o_ref, state_ref, prev_ref):
    L, GW = RWKV_CHUNK, GROUP_W
    p = p_ref[...]
    row = lax.broadcasted_iota(jnp.int32, p.shape, 0)
    prev = jnp.where(row == 0, prev_ref[...], pltpu.roll(p, 1, axis=0))
    prev_ref[...] = p[L - 1:L, :]
    ps = p + (prev - p) * mu_ref[...]
    r, k, v = ps[:, :GW], ps[:, GW:2 * GW], ps[:, 2 * GW:3 * GW]
    low = ps[:, 3 * GW:]
    lane_low = lax.broadcasted_iota(jnp.int32, low.shape, 1)
    low = jnp.where(lane_low < RWKV_W_RANK, jnp.tanh(low),
                    jnp.where(lane_low < RWKV_W_RANK + RWKV_A_RANK, low, jax.nn.sigmoid(low)))
    up = _dot3(low, lhi_ref[...], llo_ref[...])
    w0, a0, k_k, k_a = vec_ref[0:1, :], vec_ref[1:2, :], vec_ref[2:3, :], vec_ref[3:4, :]
    r_k, ln_g, ln_b = vec_ref[4:5, :], vec_ref[5:6, :], vec_ref[6:7, :]
    logw = -jnp.exp(-_softplus(-(w0 + up[:, :GW])) - 0.5)
    rate = jax.nn.sigmoid(a0 + up[:, GW:2 * GW])
    gate = up[:, 2 * GW:]

    ri = lax.broadcasted_iota(jnp.int32, (GW, GW), 0)
    ci = lax.broadcasted_iota(jnp.int32, (GW, GW), 1)
    ones_bd = jnp.where((ri // HEAD_DIM) == (ci // HEAD_DIM), 1.0, 0.0).astype(BF16)

    kk = k * k_k
    k = k * (1.0 + (rate - 1.0) * k_a)
    seg = _dot_f32_by_exact(jnp.concatenate([kk * kk, r * k * r_k], axis=0), ones_bd)
    kk = kk / jnp.maximum(jnp.sqrt(seg[:L]), 1e-12)

    ti = lax.broadcasted_iota(jnp.int32, (L, L), 0)
    tj = lax.broadcasted_iota(jnp.int32, (L, L), 1)
    lc = _dot_exact_by_f32(jnp.where(tj <= ti, 1.0, 0.0).astype(BF16), logw)
    lc_last = lc[L - 1:L, :]
    dec_in = jnp.exp(lc)
    dec_out = jnp.exp(-lc)
    a_t = -kk * jnp.exp(lc - logw)
    b_t = kk * rate * dec_out
    k_t = k * dec_out
    r_t = r * dec_in
    to_end = jnp.exp(lc_last)

    SG = RWKV_GROUP * L
    gi = lax.broadcasted_iota(jnp.int32, (SG, SG), 0)
    gj = lax.broadcasted_iota(jnp.int32, (SG, SG), 1)
    g_same = (gi // L) == (gj // L)
    strict = g_same & ((gj % L) < (gi % L))
    incl = g_same & ((gj % L) <= (gi % L))
    eye = jnp.where(gi == gj, 1.0, 0.0)
    stack = lambda t: jnp.concatenate([t] * RWKV_GROUP, axis=0)
    bd = lambda t: jnp.where(g_same, stack(t), 0.0).astype(BF16)
    n_doublings = RWKV_CHUNK.bit_length() - 2
    ys = []
    for g in range(N_HEADS // RWKV_GROUP):
        cols = slice(g * SG, (g + 1) * SG)
        a_bd, r_bd, v_bd = bd(a_t[:, cols]), bd(r_t[:, cols]), bd(v[:, cols])
        m = _dot_nt(jnp.concatenate([a_bd, r_bd], axis=0),
                    jnp.concatenate([stack(b_t[:, cols]), stack(k_t[:, cols])], axis=0).astype(BF16))
        yield
        m_ab = jnp.where(strict, m[:SG, :SG], 0.0)
        m_ak = jnp.where(strict, m[:SG, SG:], 0.0).astype(BF16)
        m_rb = jnp.where(incl, m[SG:, :SG], 0.0).astype(BF16)
        m_rk = jnp.where(incl, m[SG:, SG:], 0.0).astype(BF16)

        inv = eye + m_ab
        pw_b = m_ab.astype(BF16)
        sq = _dot(pw_b, pw_b)
        yield
        pw_b = sq.astype(BF16)
        for s in range(n_doublings - 1):
            both = _dot(jnp.concatenate([inv.astype(BF16), pw_b], axis=0), pw_b)
            yield
            inv = inv + both[:SG]
            pw_b = both[SG:].astype(BF16)
        last = _dot(inv.astype(BF16), pw_b)
        t0 = state_ref[g]
        t0_b = t0.astype(BF16)
        rhs = _dot(jnp.concatenate([a_bd, m_ak], axis=1), jnp.concatenate([t0_b, v_bd], axis=0))
        yield
        inv_b = (inv + last).astype(BF16)
        u = _dot(inv_b, rhs.astype(BF16))
        yield
        u = u.astype(BF16)
        y_bd = _dot(jnp.concatenate([r_bd, m_rb, m_rk], axis=1),
                    jnp.concatenate([t0_b, u, v_bd], axis=0))
        end_g = to_end[:, cols]
        to_end_col = jnp.sum(jnp.where(gi == gj, jnp.broadcast_to(end_g, (SG, SG)), 0.0), axis=1, keepdims=True)
        carried = _dot_tn(jnp.concatenate([bd(b_t[:, cols] * end_g), bd(k_t[:, cols] * end_g)], axis=0),
                          jnp.concatenate([u, v_bd], axis=0))
        yield
        ys.append(sum(y_bd[h * L:(h + 1) * L, :] for h in range(RWKV_GROUP)))
        state_ref[g] = to_end_col * t0 + carried
    y = jnp.concatenate(ys, axis=1)

    inv_d = 1.0 / HEAD_DIM
    mean = _dot_f32_by_exact(y, ones_bd) * inv_d
    yield
    yc = y - mean
    var = _dot_f32_by_exact(yc * yc, ones_bd) * inv_d
    yield
    yn = yc * lax.rsqrt(var + RWKV_GN_EPS) * ln_g + ln_b
    o_ref[...] = (yn + seg[L:] * v) * gate


def _rwkv7_time_mix(p, mu, w0, w2, a0, a2, g2, k_k, k_a, r_k, ln_g, ln_b):
    B, S, _ = p.shape
    GW = GROUP_W
    assert S % RWKV_CHUNK == 0 and RWKV_CHUNK == HEAD_DIM and B % RWKV_BATCH == 0
    low_w = jnp.zeros((RWKV_LOW, 3 * GW), F32)
    low_w = low_w.at[:RWKV_W_RANK, :GW].set(w2)
    low_w = low_w.at[RWKV_W_RANK:RWKV_W_RANK + RWKV_A_RANK, GW:2 * GW].set(a2)
    low_w = low_w.at[RWKV_W_RANK + RWKV_A_RANK:, 2 * GW:].set(g2)
    l_hi, l_lo = _split_bf16(low_w)
    vecs = jnp.stack([w0, a0, k_k, k_a, r_k, ln_g, ln_b, jnp.zeros_like(w0)], 0)
    full = lambda a: pl.BlockSpec(a.shape, lambda b, c: (0,) * a.ndim)
    mu2 = mu[None]
    return pl.pallas_call(
        _rwkv_kernel,
        grid=(B // RWKV_BATCH, S // RWKV_CHUNK),
        in_specs=[pl.BlockSpec((RWKV_BATCH, RWKV_CHUNK, A_PAD), lambda b, c: (b, c, 0)),
                  full(mu2), full(vecs), full(l_hi), full(l_lo)],
        out_specs=pl.BlockSpec((RWKV_BATCH, RWKV_CHUNK, GW), lambda b, c: (b, c, 0)),
        out_shape=jax.ShapeDtypeStruct((B, S, GW), F32),
        scratch_shapes=[pltpu.VMEM((RWKV_BATCH, N_HEADS // RWKV_GROUP, RWKV_GROUP * RWKV_CHUNK,
                                    RWKV_GROUP * HEAD_DIM), F32),
                        pltpu.VMEM((RWKV_BATCH, 1, A_PAD), F32)],
        compiler_params=pltpu.CompilerParams(dimension_semantics=("parallel", "arbitrary"),
                                             vmem_limit_bytes=VMEM_LIMIT),
        name="rwkv7",
    )(p, mu2, vecs, l_hi, l_lo)


SB_UNROLL = 2


def _sb_kernel(q_ref, k_ref, v_ref, g_ref, o_ref, kbd_ref, vbd_ref):
    i = pl.program_id(1)
    T, GW, H = SB_BLOCK, GROUP_W, N_HEADS
    lane_h = lax.broadcasted_iota(jnp.int32, (T, GW), 1) // HEAD_DIM
    k_new, v_new = k_ref[0], v_ref[0]
    for h in range(H):
        kbd_ref[i, h * T:(h + 1) * T, :] = jnp.where(lane_h == h, k_new, 0.0).astype(BF16)
        vbd_ref[i, h * T:(h + 1) * T, :] = jnp.where(lane_h == h, v_new, 0.0).astype(BF16)

    q = q_ref[0].astype(BF16)
    si = lax.broadcasted_iota(jnp.int32, (T, 2 * T), 0)
    sj = lax.broadcasted_iota(jnp.int32, (T, 2 * T), 1)
    later_and_all = jnp.where((si > sj) | (sj >= T), 1.0, 0.0).astype(BF16)
    qrow = lax.broadcasted_iota(jnp.int32, (T, H * T), 0)
    kcol = lax.broadcasted_iota(jnp.int32, (T, H * T), 1) % T
    causal = kcol < qrow

    def key_block_stages(j, box, diagonal):
        z = _dot_nt(q, kbd_ref[j]) * HEAD_DIM ** -0.5
        yield
        soft = jnp.log(1.0 + jnp.exp(-jnp.abs(z)))
        log1m = -(jnp.maximum(z, 0.0) + soft)
        log_sig = jnp.minimum(z, 0.0) - soft
        log1m_in = (jnp.where(causal, log1m, 0.0) if diagonal else log1m).astype(BF16)
        sums = _dot(jnp.concatenate([log1m_in[:, h * T:(h + 1) * T] for h in range(H)], axis=0), later_and_all)
        yield
        suffix = jnp.concatenate([sums[h * T:(h + 1) * T, :T] for h in range(H)], axis=1)
        total = jnp.concatenate([sums[h * T:(h + 1) * T, T:] for h in range(H)], axis=1)
        att = jnp.exp(log_sig + suffix + box["carry"])
        if diagonal:
            att = jnp.where(causal, att, 0.0)
        box["carry"] = box["carry"] + total
        pv = _dot(att.astype(BF16), vbd_ref[j])
        yield
        box["acc"] = box["acc"] + pv

    def key_blocks(js, state, diagonal=False):
        box = {"carry": state[0], "acc": state[1]}
        for _ in itertools.zip_longest(*[key_block_stages(j, box, diagonal) for j in js]):
            pass
        return box["carry"], box["acc"]

    state = key_blocks([i], (jnp.zeros((T, H * T), F32), jnp.zeros((T, GW), F32)), True)
    rem = i % SB_UNROLL
    state = lax.fori_loop(0, rem, lambda it, st: key_blocks([i - 1 - it], st), state)
    top = i - 1 - rem
    _, y = lax.fori_loop(0, i // SB_UNROLL,
                         lambda it, st: key_blocks([top - SB_UNROLL * it - n for n in range(SB_UNROLL)], st), state)

    hi_ = lax.broadcasted_iota(jnp.int32, (GW, GW), 0) // HEAD_DIM
    hj_ = lax.broadcasted_iota(jnp.int32, (GW, GW), 1) // HEAD_DIM
    ones_bd = jnp.where(hi_ == hj_, 1.0, 0.0).astype(BF16)
    ms = _dot_f32_by_exact(y * y, ones_bd) * (1.0 / HEAD_DIM)
    o_ref[0] = y * lax.rsqrt(ms + NORM_EPS) * g_ref[...]


def _stick_breaking_norm(p, g):
    B, S, _ = p.shape
    GW = GROUP_W
    assert S % SB_BLOCK == 0
    col = lambda n: pl.BlockSpec((1, SB_BLOCK, GW), lambda b, i: (b, i, n))
    blk = col(0)
    q = k = v = p
    stacked = pltpu.VMEM((S // SB_BLOCK, N_HEADS * SB_BLOCK, GW), BF16)
    return pl.pallas_call(
        _sb_kernel,
        grid=(B, S // SB_BLOCK),
        in_specs=[col(0), col(1), col(2), pl.BlockSpec((1, GW), lambda b, i: (0, 0))],
        out_specs=blk,
        out_shape=jax.ShapeDtypeStruct((B, S, GW), F32),
        scratch_shapes=[stacked, stacked],
        compiler_params=pltpu.CompilerParams(dimension_semantics=("parallel", "arbitrary"),
                                             vmem_limit_bytes=VMEM_LIMIT),
        name="stick_breaking",
    )(q, k, v, g[None])


ML_HALO = 8
ML_GROUP = N_HEADS


def _dot_nt_exact_by_f32(a_exact, b):
    return _dot_nt(jnp.concatenate([a_exact] * 3, axis=1), jnp.concatenate(_split3_bf16(b), axis=1))


def _mlstm_kernel(p_ref, cw_ref, cb_ref, gb_ref, g_ref, o_ref, ext_ref, ct_ref, n_ref, m_ref):
    L, GW, H = ML_CHUNK, GROUP_W, N_HEADS

    @pl.when(pl.program_id(1) == 0)
    def _():
        ext_ref[...] = jnp.zeros_like(ext_ref)
        ct_ref[...] = jnp.zeros_like(ct_ref)
        n_ref[...] = jnp.zeros_like(n_ref)
        m_ref[...] = jnp.zeros_like(m_ref)

    x = p_ref[0]
    ext_ref[ML_HALO:, :] = x[:, :2 * GW]
    conv = cb_ref[...]
    for j in range(ML_CONV):
        conv = conv + cw_ref[j:j + 1, :] * ext_ref[pl.ds(ML_HALO - (ML_CONV - 1) + j, L), :]
    ext_ref[:ML_HALO, :] = x[L - ML_HALO:, :2 * GW]
    qk = conv * jax.nn.sigmoid(conv)
    q, k = qk[:, :GW], qk[:, GW:] * HEAD_DIM ** -0.5
    v, o = x[:, 2 * GW:3 * GW], x[:, 3 * GW:4 * GW]

    gates = x[:, 4 * GW:]
    gi = lax.broadcasted_iota(jnp.int32, (LANE, 2 * GW), 0)
    gj = lax.broadcasted_iota(jnp.int32, (LANE, 2 * GW), 1)
    expand = jnp.where(gi == (gj % GW) // HEAD_DIM + H * (gj // GW), 1.0, 0.0).astype(BF16)
    graw = _dot_f32_by_exact(gates, expand) + gb_ref[...]
    capped = GATE_CAP * jnp.tanh(graw * (1.0 / GATE_CAP))
    log_i = capped[:, :GW]
    cf = capped[:, GW:]
    log_f = jnp.minimum(cf, 0.0) - jnp.log(1.0 + jnp.exp(-jnp.abs(cf)))

    ti = lax.broadcasted_iota(jnp.int32, (L, L), 0)
    tj = lax.broadcasted_iota(jnp.int32, (L, L), 1)
    bf = _dot_exact_by_f32(jnp.where(tj <= ti, 1.0, 0.0).astype(BF16), log_f)
    b_last = bf[L - 1:L, :]
    m_row, n_row = m_ref[...], n_ref[...]
    dec = b_last - bf + log_i
    m_new = jnp.maximum(b_last + m_row, jnp.max(dec, axis=0, keepdims=True))
    kw = k * jnp.exp(dec - m_new)
    s_old = jnp.exp(b_last + m_row - m_new)
    n_ref[...] = s_old * n_row + jnp.sum(kw, axis=0, keepdims=True)
    m_ref[...] = m_new
    g_in = bf + m_row
    li_b = log_i - bf

    SG = ML_GROUP * L
    ri = lax.broadcasted_iota(jnp.int32, (SG, SG), 0)
    ci = lax.broadcasted_iota(jnp.int32, (SG, SG), 1)
    same_head = (ri // L) == (ci // HEAD_DIM)
    first_lane = ci == (ri // L) * HEAD_DIM
    sel_first = jnp.where(first_lane, 1.0, 0.0).astype(BF16)
    stack = lambda t: jnp.concatenate([t] * ML_GROUP, axis=0)
    pick = lambda t: jnp.sum(jnp.where(first_lane, t, 0.0), axis=1, keepdims=True)
    rt = lax.broadcasted_iota(jnp.int32, (SG, L), 0) % L
    rs = lax.broadcasted_iota(jnp.int32, (SG, L), 1)
    hs = []
    for g in range(H // ML_GROUP):
        cols = slice(g * SG, (g + 1) * SG)
        ct = ct_ref[g]
        qs = jnp.where(same_head, stack(q[:, cols]), 0.0)
        qs_b = qs.astype(BF16)
        v_b = v[:, cols].astype(BF16)
        b_col = pick(stack(bf[:, cols]))
        g_col = pick(stack(g_in[:, cols]))
        row_part = _dot_nt_exact_by_f32(sel_first, li_b[:, cols])
        dmat = jnp.where(rs <= rt, b_col + row_part, -jnp.inf)
        m_t = jnp.maximum(g_col, jnp.max(dmat, axis=1, keepdims=True))
        s_inter = jnp.exp(g_col - m_t)
        sqk = _dot_nt(qs_b, k[:, cols].astype(BF16)) * jnp.exp(dmat - m_t)
        num = s_inter * _dot(qs_b, ct.astype(BF16)) + jnp.where(same_head, _dot(sqk.astype(BF16), v_b), 0.0)
        den = (s_inter * jnp.sum(qs * n_row[:, cols], axis=1, keepdims=True)
               + jnp.sum(sqk, axis=1, keepdims=True))
        hst = num / jnp.maximum(jnp.abs(den), jnp.exp(-m_t))
        hs.append(sum(hst[n * L:(n + 1) * L, :] for n in range(ML_GROUP)))
        ct_ref[g] = s_old[:, cols] * ct + jnp.where(same_head, _dot_tn(kw[:, cols].astype(BF16), v_b), 0.0)
    h = jnp.concatenate(hs, axis=1)

    hi_ = lax.broadcasted_iota(jnp.int32, (GW, GW), 0) // HEAD_DIM
    hj_ = lax.broadcasted_iota(jnp.int32, (GW, GW), 1) // HEAD_DIM
    ones_bd = jnp.where(hi_ == hj_, 1.0, 0.0).astype(BF16)
    ms = _dot_f32_by_exact(h * h, ones_bd) * (1.0 / HEAD_DIM)
    o_ref[0] = jax.nn.sigmoid(o) * (h * lax.rsqrt(ms + NORM_EPS) * g_ref[...])


def _mlstm_mix(p, conv_w, conv_b, ig_b, fg_b, norm_g):
    B, S, _ = p.shape
    GW = GROUP_W
    assert S % ML_CHUNK == 0 and ML_CONV - 1 <= ML_HALO <= ML_CHUNK
    gate_b = jnp.concatenate([jnp.repeat(ig_b, HEAD_DIM), jnp.repeat(fg_b, HEAD_DIM)])[None]
    full = lambda a: pl.BlockSpec(a.shape, lambda b, c: (0,) * a.ndim)
    cb2, g2 = conv_b[None], norm_g[None]
    return pl.pallas_call(
        _mlstm_kernel,
        grid=(B, S // ML_CHUNK),
        in_specs=[pl.BlockSpec((1, ML_CHUNK, C_PAD), lambda b, c: (b, c, 0)),
                  full(conv_w), full(cb2), full(gate_b), full(g2)],
        out_specs=pl.BlockSpec((1, ML_CHUNK, GW), lambda b, c: (b, c, 0)),
        out_shape=jax.ShapeDtypeStruct((B, S, GW), F32),
        scratch_shapes=[pltpu.VMEM((ML_HALO + ML_CHUNK, 2 * GW), F32),
                        pltpu.VMEM((N_HEADS // ML_GROUP, ML_GROUP * ML_CHUNK, ML_GROUP * HEAD_DIM), F32),
                        pltpu.VMEM((1, GW), F32), pltpu.VMEM((1, GW), F32)],
        compiler_params=pltpu.CompilerParams(dimension_semantics=("parallel", "arbitrary"),
                                             vmem_limit_bytes=VMEM_LIMIT),
        name="mlstm",
    )(p, conv_w, cb2, gate_b, g2)


def _rms_norm(x, g):
    xf = x.astype(F32)
    y = xf * lax.rsqrt(jnp.mean(xf * xf, -1, keepdims=True) + NORM_EPS)
    return (y * g.astype(F32)).astype(x.dtype)


def _rope(x, pos):
    half = x.shape[-1] // 2
    inv = ROPE_THETA ** (-jnp.arange(half, dtype=F32) / half)
    ang = pos.astype(F32)[:, None] * inv[None, :]
    cos = jnp.cos(ang)[None, :, None, :]
    sin = jnp.sin(ang)[None, :, None, :]
    xf = x.astype(F32)
    x1, x2 = xf[..., :half], xf[..., half:]
    return jnp.concatenate([x1 * cos - x2 * sin, x2 * cos + x1 * sin], -1).astype(x.dtype)


def kernel(x, c, ada_w, ada_b, norm1_g, norm2_g, w_in, rk_mu, rk_w0, rk_w2, rk_a0, rk_a2, rk_g2, rk_kk, rk_ka, rk_rk, rk_ln_g, rk_ln_b, sb_norm_g, ml_conv_w, ml_conv_b, ml_ig_b, ml_fg_b, ml_norm_g, ds_qn_g, ds_kn_g, ds_out_g, w_out, moe_wg, moe_bg, moe_we, moe_be, moe_w1, moe_w3, moe_w2):
    B, S, D = x.shape
    H, d = N_HEADS, HEAD_DIM
    depth = ada_w.shape[0]
    pos = jnp.arange(S)
    c_act = jax.nn.silu(c)
    for l in range(depth):
        mod = (c_act @ ada_w[l] + ada_b[l])[:, None, :]
        sh1, sc1, gt1, sh2, sc2, gt2 = jnp.split(mod, 6, axis=-1)

        pA, pB, pC, pD = _in_proj(x, sc1, sh1, norm1_g[l][None], _pad_w_in(w_in[l]))

        yA = _rwkv7_time_mix(pA, rk_mu[l], rk_w0[l], rk_w2[l], rk_a0[l], rk_a2[l], rk_g2[l],
                             rk_kk[l], rk_ka[l], rk_rk[l], rk_ln_g[l], rk_ln_b[l])

        yB = _stick_breaking_norm(pB, sb_norm_g[l])

        yC = _mlstm_mix(pC, ml_conv_w[l], ml_conv_b[l], ml_ig_b[l], ml_fg_b[l], ml_norm_g[l])

        yD = _dsa_attn_norm(*_dsa_prep(pD, ds_qn_g[l], ds_kn_g[l]), ds_out_g[l])

        router = jnp.pad(jnp.concatenate([moe_wg[l], moe_we[l]], 1),
                         ((0, 0), (0, ROUTER_PAD - N_GROUPS - N_EXPERTS)))
        r_hi, r_lo = _split_bf16(router)
        r_b = jnp.pad(jnp.concatenate([moe_bg[l], moe_be[l]]), (0, ROUTER_PAD - N_GROUPS - N_EXPERTS))[None]
        x1, h2, route = _out_proj((yA, yB, yC, yD), x, gt1, sc2, sh2, norm2_g[l][None],
                                  w_out[l].astype(BF16), r_hi, r_lo, r_b)

        moe = _hier_moe(h2.reshape(B * S, D), route.reshape(B * S, ROUTER_PAD),
                        moe_w1[l], moe_w3[l], moe_w2[l])
        x = x1 + gt2 * moe.reshape(B, S, D)
    return x
```

```python
import functools
import itertools

import jax
import jax.numpy as jnp
import numpy as np
from jax import lax
from jax.experimental import pallas as pl
from jax.experimental.pallas import tpu as pltpu

F32 = jnp.float32
BF16 = jnp.bfloat16

D_MODEL = 1024
N_MIXERS = 4
GROUP_W = D_MODEL // N_MIXERS
HEAD_DIM = 64
N_HEADS = GROUP_W // HEAD_DIM
NORM_EPS = 1e-6
RWKV_W_RANK = 32
RWKV_A_RANK = 32
RWKV_G_RANK = 64
RWKV_GN_EPS = 64e-5
SB_BLOCK = 128
ML_CHUNK = 64
ML_CONV = 4
GATE_CAP = 15.0
DSA_BLOCK = 128
IDX_HEADS = 4
IDX_DIM = 32
TOPK_MAX = 256
ROPE_THETA = 10000.0
N_GROUPS = 4
EXP_PER_GROUP = 8
N_EXPERTS = N_GROUPS * EXP_PER_GROUP
EXPERT_FF = D_MODEL // 2
TOP_IN_GROUP = 2

A_SIZES = (GROUP_W, GROUP_W, GROUP_W, RWKV_W_RANK, RWKV_A_RANK, RWKV_G_RANK)
B_SIZES = (GROUP_W, GROUP_W, GROUP_W)
C_SIZES = (GROUP_W, GROUP_W, GROUP_W, GROUP_W, N_HEADS, N_HEADS)
D_SIZES = (GROUP_W, HEAD_DIM, HEAD_DIM, IDX_HEADS * IDX_DIM, IDX_DIM, IDX_HEADS)
A_COLS = sum(A_SIZES)
B_COLS = sum(B_SIZES)
C_COLS = sum(C_SIZES)
D_COLS = sum(D_SIZES)

LANE = 128
A_PAD = 896
B_PAD = 768
C_PAD = 1152
D_PAD = 640
P_PAD = A_PAD + B_PAD + C_PAD + D_PAD
ROUTER_PAD = LANE

IN_ROWS = 256
OUT_ROWS = 512
MOE_ROWS = 512
VMEM_LIMIT = 48 * 1024 * 1024


def _split_cols(t, sizes):
    return jnp.split(t, [int(i) for i in np.cumsum(sizes)[:-1]], axis=-1)


def _in_proj_kernel(x_ref, sc_ref, sh_ref, g_ref, w_ref, oa_ref, ob_ref, oc_ref, od_ref):
    x = x_ref[0]
    y = x * lax.rsqrt(jnp.mean(x * x, -1, keepdims=True) + NORM_EPS) * g_ref[...]
    h = y * (1.0 + sc_ref[0]) + sh_ref[0]
    p = jnp.dot(h.astype(BF16), w_ref[...], preferred_element_type=F32)
    oa_ref[0] = p[:, :A_PAD]
    ob_ref[0] = p[:, A_PAD:A_PAD + B_PAD]
    oc_ref[0] = p[:, A_PAD + B_PAD:A_PAD + B_PAD + C_PAD]
    od_ref[0] = p[:, A_PAD + B_PAD + C_PAD:]


def _in_proj(x, sc, sh, g, w_pad):
    B, S, D = x.shape
    row = lambda w: pl.BlockSpec((1, IN_ROWS, w), lambda b, i: (b, i, 0))
    vec = pl.BlockSpec((1, 1, D), lambda b, i: (b, 0, 0))
    return pl.pallas_call(
        _in_proj_kernel,
        grid=(B, S // IN_ROWS),
        in_specs=[row(D), vec, vec, pl.BlockSpec((1, D), lambda b, i: (0, 0)),
                  pl.BlockSpec((D, P_PAD), lambda b, i: (0, 0))],
        out_specs=[row(A_PAD), row(B_PAD), row(C_PAD), row(D_PAD)],
        out_shape=[jax.ShapeDtypeStruct((B, S, w), F32) for w in (A_PAD, B_PAD, C_PAD, D_PAD)],
        compiler_params=pltpu.CompilerParams(dimension_semantics=("parallel", "parallel"),
                                             vmem_limit_bytes=VMEM_LIMIT),
        name="in_proj",
    )(x, sc, sh, g, w_pad)


def _pad_w_in(w):
    wa, wb, wc, wd = _split_cols(w, (A_COLS, B_COLS, C_COLS, D_COLS))
    padc = lambda t, n: jnp.pad(t, ((0, 0), (0, n - t.shape[1])))
    return jnp.concatenate([padc(wa, A_PAD), padc(wb, B_PAD), padc(wc, C_PAD), padc(wd, D_PAD)], 1).astype(BF16)


def _split_bf16(t):
    hi = t.astype(BF16)
    lo = (t - hi.astype(F32)).astype(BF16)
    return hi, lo


def _out_proj_kernel(ya_ref, yb_ref, yc_ref, yd_ref, x_ref, gt_ref, sc_ref, sh_ref, g_ref, w_ref,
                     rhi_ref, rlo_ref, rb_ref, x1_ref, h2_ref, route_ref):
    acc = jnp.zeros(x_ref.shape[1:], F32)
    for n, y_ref in enumerate((ya_ref, yb_ref, yc_ref, yd_ref)):
        acc += jnp.dot(y_ref[0].astype(BF16), w_ref[n * GROUP_W:(n + 1) * GROUP_W, :],
                       preferred_element_type=F32)
    x1 = x_ref[0] + gt_ref[0] * acc
    x1_ref[0] = x1
    y = x1 * lax.rsqrt(jnp.mean(x1 * x1, -1, keepdims=True) + NORM_EPS) * g_ref[...]
    h = y * (1.0 + sc_ref[0]) + sh_ref[0]
    hi, lo = _split_bf16(h)
    h2_ref[0] = hi
    lg = (jnp.dot(hi, rhi_ref[...], preferred_element_type=F32)
          + jnp.dot(lo, rhi_ref[...], preferred_element_type=F32)
          + jnp.dot(hi, rlo_ref[...], preferred_element_type=F32)) + rb_ref[...]
    route_ref[0] = _route(lg)


def _route(lg):
    lane = lax.broadcasted_iota(jnp.int32, lg.shape, 1)
    neg = -jnp.inf
    first = lambda hit: jnp.min(jnp.where(hit, lane, ROUTER_PAD), axis=1, keepdims=True)
    is_grp = lane < N_GROUPS
    grp = jnp.where(is_grp, lg, neg)
    g_max = jnp.max(grp, axis=1, keepdims=True)
    g_p = 1.0 / jnp.sum(jnp.where(is_grp, jnp.exp(grp - g_max), 0.0), axis=1, keepdims=True)
    g_idx = first(grp == g_max)
    e_lane = lane - N_GROUPS
    in_group = (e_lane >= 0) & (e_lane < N_EXPERTS) & (e_lane // EXP_PER_GROUP == g_idx)
    e_log = jnp.where(in_group, lg, neg)
    e1_max = jnp.max(e_log, axis=1, keepdims=True)
    e1_lane = first(e_log == e1_max)
    e_log2 = jnp.where(lane == e1_lane, neg, e_log)
    e2_max = jnp.max(e_log2, axis=1, keepdims=True)
    e2_lane = first(e_log2 == e2_max)
    ratio = jnp.exp(e2_max - e1_max)
    gate1 = g_p / (1.0 + ratio)
    gate2 = gate1 * ratio
    out = jnp.where(lane == 0, (e1_lane - N_GROUPS).astype(F32), 0.0)
    out = jnp.where(lane == 1, (e2_lane - N_GROUPS).astype(F32), out)
    out = jnp.where(lane == 2, gate1, out)
    return jnp.where(lane == 3, gate2, out)


def _out_proj(ys, x, gt, sc, sh, g, w_out, r_hi, r_lo, r_b):
    B, S, D = x.shape
    row = lambda w: pl.BlockSpec((1, OUT_ROWS, w), lambda b, i: (b, i, 0))
    vec = pl.BlockSpec((1, 1, D), lambda b, i: (b, 0, 0))
    full = lambda a: pl.BlockSpec(a.shape, lambda b, i: (0,) * a.ndim)
    return pl.pallas_call(
        _out_proj_kernel,
        grid=(B, S // OUT_ROWS),
        in_specs=[row(GROUP_W)] * 4 + [row(D), vec, vec, vec, full(g), full(w_out), full(r_hi), full(r_lo),
                                        full(r_b)],
        out_specs=[row(D), row(D), row(ROUTER_PAD)],
        out_shape=[jax.ShapeDtypeStruct((B, S, D), F32), jax.ShapeDtypeStruct((B, S, D), BF16),
                   jax.ShapeDtypeStruct((B, S, ROUTER_PAD), F32)],
        compiler_params=pltpu.CompilerParams(dimension_semantics=("parallel", "parallel"),
                                             vmem_limit_bytes=VMEM_LIMIT),
        name="out_proj",
    )(*ys, x, gt, sc, sh, g, w_out, r_hi, r_lo, r_b)


def _moe_ffn_kernel(blk_e_ref, x_ref, wt_ref, w1_ref, w3_ref, w2_ref, o_ref, w1b_ref, w3b_ref, w2b_ref):
    i = pl.program_id(0)
    changed = jnp.logical_or(i == 0, blk_e_ref[i] != blk_e_ref[jnp.maximum(i - 1, 0)])

    @pl.when(changed)
    def _():
        w1b_ref[...] = w1_ref[0].astype(BF16)
        w3b_ref[...] = w3_ref[0].astype(BF16)
        w2b_ref[...] = w2_ref[0].astype(BF16)

    xb = x_ref[...]
    a = jnp.dot(xb, w1b_ref[...], preferred_element_type=F32)
    b = jnp.dot(xb, w3b_ref[...], preferred_element_type=F32)
    hmid = (a * jax.nn.sigmoid(a) * b).astype(BF16)
    y = jnp.dot(hmid, w2b_ref[...], preferred_element_type=F32)
    o_ref[...] = (y * wt_ref[...]).astype(o_ref.dtype)


def _moe_ffn(blk_e, xs, wt, w1, w3, w2):
    n_slots, D = xs.shape
    n_blocks = n_slots // MOE_ROWS
    FF = w1.shape[-1]
    return pl.pallas_call(
        _moe_ffn_kernel,
        grid_spec=pltpu.PrefetchScalarGridSpec(
            num_scalar_prefetch=1,
            grid=(n_blocks,),
            in_specs=[pl.BlockSpec((MOE_ROWS, D), lambda i, e: (i, 0)),
                      pl.BlockSpec((MOE_ROWS, 1), lambda i, e: (i, 0)),
                      pl.BlockSpec((1, D, FF), lambda i, e: (e[i], 0, 0)),
                      pl.BlockSpec((1, D, FF), lambda i, e: (e[i], 0, 0)),
                      pl.BlockSpec((1, FF, D), lambda i, e: (e[i], 0, 0))],
            out_specs=pl.BlockSpec((MOE_ROWS, D), lambda i, e: (i, 0)),
            scratch_shapes=[pltpu.VMEM((D, FF), BF16), pltpu.VMEM((D, FF), BF16), pltpu.VMEM((FF, D), BF16)],
        ),
        out_shape=jax.ShapeDtypeStruct((n_slots, D), BF16),
        compiler_params=pltpu.CompilerParams(dimension_semantics=("arbitrary",),
                                             vmem_limit_bytes=VMEM_LIMIT),
        name="moe_ffn",
    )(blk_e, xs, wt, w1, w3, w2)


def _hier_moe(h2, route, w1, w3, w2):
    N, D = h2.shape
    expert = route[:, :TOP_IN_GROUP].astype(jnp.int32)
    gate = route[:, TOP_IN_GROUP:2 * TOP_IN_GROUP]
    n_asg = N * TOP_IN_GROUP
    flat_e = expert.reshape(n_asg // LANE, LANE)
    order = jnp.argsort(flat_e.reshape(n_asg))
    rank = jnp.argsort(order).astype(jnp.int32)
    counts = jnp.sum(flat_e[None] == jnp.arange(N_EXPERTS)[:, None, None], axis=(1, 2)).astype(jnp.int32)
    start = jnp.cumsum(counts) - counts
    pad_counts = (counts + MOE_ROWS - 1) // MOE_ROWS * MOE_ROWS
    pad_end = jnp.cumsum(pad_counts)
    pad_start = pad_end - pad_counts
    shift = pad_start - start
    asg_shift = jnp.zeros_like(flat_e)
    for e in range(N_EXPERTS):
        asg_shift = jnp.where(flat_e == e, shift[e], asg_shift)
    asg_slot = (rank + asg_shift.reshape(n_asg)).reshape(N, TOP_IN_GROUP)
    n_blocks = n_asg // MOE_ROWS + N_EXPERTS
    blk_start = jnp.arange(n_blocks) * MOE_ROWS
    blk_e = jnp.minimum(jnp.sum(pad_end[None, :] <= blk_start[:, None], 1), N_EXPERTS - 1).astype(jnp.int32)
    blk_pos = blk_start - pad_start[blk_e]
    row = jnp.arange(MOE_ROWS)[None, :]
    slot_real = (blk_pos[:, None] + row < counts[blk_e][:, None]) & (blk_start[:, None] < pad_end[N_EXPERTS - 1])
    slot_src = jnp.where(slot_real, (start[blk_e] + blk_pos)[:, None] + row, 0).reshape(-1)
    slot_asg = order[slot_src]
    slot_real = slot_real.reshape(-1)
    slot_tok = jnp.where(slot_real, slot_asg // TOP_IN_GROUP, 0).astype(jnp.int32)
    slot_w = jnp.where(slot_real, gate.reshape(n_asg)[slot_asg], 0.0)
    yb = _moe_ffn(blk_e, h2[slot_tok], slot_w[:, None], w1, w3, w2)
    return yb[asg_slot[:, 0]].astype(F32) + yb[asg_slot[:, 1]].astype(F32)


INT_MIN = -2 ** 31
DSA_KEY_STEP = 512


def _float_order_key(x):
    bits = pltpu.bitcast(x, jnp.int32)
    bits = jnp.where(x == 0.0, 0, bits)
    return bits ^ ((bits >> 31) & 0x7FFFFFFF)


COL_PART = 64


def _col_reduce(x, reduce):
    part = reduce(x.reshape(x.shape[0] // COL_PART, COL_PART, x.shape[1]), axis=0)
    return reduce(part, axis=0, keepdims=True)


def _col_count(mask):
    return _col_reduce(jnp.where(mask, 1.0, 0.0), jnp.sum)


def _head_block_diag(t, group):
    n_heads = t.shape[0] // group
    row_h = lax.broadcasted_iota(jnp.int32, t.shape, 0) // group
    return jnp.concatenate([jnp.where(row_h == h, t, 0.0) for h in range(n_heads)], axis=1)


def _dsa_block_t(qd_ref, kd4_ref, vdwt_ref, qi_ref, kihi_ref, kilo_ref, wi_ref, g_ref, o_ref, *, kl, n_sel):
    Q = DSA_BLOCK
    q0 = pl.program_id(1) * Q
    w_hi, w_lo = _split_bf16(_head_block_diag(jnp.transpose(qi_ref[0]), IDX_DIM))
    k_hi, k_lo = kihi_ref[0, :kl, :], kilo_ref[0, :kl, :]
    sc = _dot(jnp.concatenate([k_hi, k_lo, k_hi], axis=1),
              jnp.concatenate([w_hi, w_hi, w_lo], axis=0))
    wit = jnp.transpose(wi_ref[0])
    score = sum(wit[h:h + 1, :] * jnp.maximum(sc[:, h * Q:(h + 1) * Q], 0.0) for h in range(IDX_HEADS))
    kidx = lax.broadcasted_iota(jnp.int32, (kl, Q), 0)
    qpos = q0 + lax.broadcasted_iota(jnp.int32, (kl, Q), 1)
    adm = kidx <= qpos
    key = _float_order_key(jnp.where(adm, score, -jnp.inf))

    def value_bit(it, tau):
        cand = tau | jnp.left_shift(jnp.int32(1), 31 - it)
        return jnp.where(_col_count(key >= (cand ^ INT_MIN)) >= n_sel, cand, tau)

    tau = lax.fori_loop(0, 32, value_bit, jnp.zeros((1, Q), jnp.int32)) ^ INT_MIN
    gt = key > tau
    eq = (key == tau) & adm
    need = n_sel - _col_count(gt)
    n_eq = _col_count(eq)

    def index_bits():
        def index_bit(it, bound):
            cand = bound | jnp.left_shift(jnp.int32(1), 11 - it)
            return jnp.where(_col_count(eq & (kidx < cand)) <= need, cand, bound)
        return lax.fori_loop(0, 12, index_bit, jnp.zeros((1, Q), jnp.int32))

    bound = lax.cond(jnp.max(n_eq - need) > 0.0, index_bits, lambda: jnp.full((1, Q), kl, jnp.int32))
    sel = gt | (eq & (kidx < bound))

    w_att = _head_block_diag(jnp.transpose(qd_ref[0]), HEAD_DIM).astype(BF16)
    lg = _dot(kd4_ref[0, :kl, :], w_att) * HEAD_DIM ** -0.5
    lg = jnp.where(jnp.concatenate([sel] * N_HEADS, axis=1), lg, -jnp.inf)
    p = jnp.exp(lg - _col_reduce(lg, jnp.max))
    out_t = _dot(vdwt_ref[0, :, :kl], p.astype(BF16)) / _col_reduce(p, jnp.sum)
    row_h = lax.broadcasted_iota(jnp.int32, (GROUP_W, Q), 0) // HEAD_DIM
    nat_t = sum(jnp.where(row_h == h, out_t[:, h * Q:(h + 1) * Q], 0.0) for h in range(N_HEADS))
    r = jnp.transpose(nat_t)
    hi_ = lax.broadcasted_iota(jnp.int32, (GROUP_W, GROUP_W), 0) // HEAD_DIM
    hj_ = lax.broadcasted_iota(jnp.int32, (GROUP_W, GROUP_W), 1) // HEAD_DIM
    ms = _dot_f32_by_exact(r * r, jnp.where(hi_ == hj_, 1.0, 0.0).astype(BF16)) * (1.0 / HEAD_DIM)
    o_ref[0] = r * lax.rsqrt(ms + NORM_EPS) * g_ref[...]


def _dsa_kernel(qd_ref, kd4_ref, vdwt_ref, qi_ref, kihi_ref, kilo_ref, wi_ref, g_ref, o_ref, *, kls, n_sel):
    blocks_per_step = DSA_KEY_STEP // DSA_BLOCK
    for j, kl in enumerate(kls):
        @pl.when(pl.program_id(1) // blocks_per_step == j)
        def _():
            _dsa_block_t(qd_ref, kd4_ref, vdwt_ref, qi_ref, kihi_ref, kilo_ref, wi_ref, g_ref, o_ref,
                         kl=kl, n_sel=n_sel)


DSA_PREP_ROWS = 256
D_Q, D_KV, D_QI, D_KW = 0, GROUP_W, GROUP_W + LANE, GROUP_W + 2 * LANE


def _swap_halves(x, half):
    n = x.shape[1]
    lane = lax.broadcasted_iota(jnp.int32, x.shape, 1)
    return jnp.where(lane % (2 * half) < half, pltpu.roll(x, n - half, axis=1), pltpu.roll(x, half, axis=1))


def _dsa_prep_kernel(p_ref, cq_ref, sq_ref, ci_ref, si_ref, gq_ref, gk_ref,
                     qd_ref, kd4_ref, vdwt_ref, qi_ref, kihi_ref, kilo_ref, wi_ref):
    GW = GROUP_W
    x = p_ref[0]
    hi_ = lax.broadcasted_iota(jnp.int32, (GW, GW), 0) // HEAD_DIM
    hj_ = lax.broadcasted_iota(jnp.int32, (GW, GW), 1) // HEAD_DIM
    ones_bd = jnp.where(hi_ == hj_, 1.0, 0.0).astype(BF16)
    q = x[:, D_Q:D_Q + GW]
    q = q * lax.rsqrt(_dot_f32_by_exact(q * q, ones_bd) * (1.0 / HEAD_DIM) + NORM_EPS) * gq_ref[...]
    qd_ref[0] = q * cq_ref[...] + _swap_halves(q, HEAD_DIM // 2) * sq_ref[...]
    kv = x[:, D_KV:D_KV + LANE]
    lane = lax.broadcasted_iota(jnp.int32, kv.shape, 1)
    is_k = lane < HEAD_DIM
    ms = jnp.sum(jnp.where(is_k, kv * kv, 0.0), axis=1, keepdims=True) * (1.0 / HEAD_DIM)
    kn = kv * lax.rsqrt(ms + NORM_EPS) * gk_ref[...]
    kr = kn * cq_ref[:, :LANE] + _swap_halves(kn, HEAD_DIM // 2) * sq_ref[:, :LANE]
    k2 = jnp.where(is_k, kr, pltpu.roll(kr, HEAD_DIM, axis=1))
    kd4_ref[0] = jnp.concatenate([k2] * (GW // LANE), axis=1).astype(BF16)
    v2 = jnp.where(is_k, pltpu.roll(kv, HEAD_DIM, axis=1), kv)
    v2t = jnp.transpose(v2)
    vdwt_ref[0] = jnp.concatenate([v2t] * (GW // LANE), axis=0).astype(BF16)
    qi = x[:, D_QI:D_QI + LANE]
    qi_ref[0] = qi * ci_ref[...] + _swap_halves(qi, IDX_DIM // 2) * si_ref[...]
    kw = x[:, D_KW:D_KW + LANE]
    kir = kw * ci_ref[...] + _swap_halves(kw, IDX_DIM // 2) * si_ref[...]
    ki1 = jnp.where(lane < IDX_DIM, kir, 0.0)
    ki2 = ki1 + pltpu.roll(ki1, IDX_DIM, axis=1)
    ki4 = ki2 + pltpu.roll(ki2, 2 * IDX_DIM, axis=1)
    kihi_ref[0], kilo_ref[0] = _split_bf16(ki4)
    wi_ref[0] = pltpu.roll(kw, LANE - IDX_DIM, axis=1) * (IDX_HEADS ** -0.5 * IDX_DIM ** -0.5)


def _rope_tables(S, dim, width):
    half = dim // 2
    inv = ROPE_THETA ** (-jnp.arange(half, dtype=F32) / half)
    ang = jnp.arange(S, dtype=F32)[:, None] * inv[None, :]
    cos = jnp.tile(jnp.cos(ang), (1, width // half))
    sin = jnp.tile(jnp.concatenate([-jnp.sin(ang), jnp.sin(ang)], axis=1), (1, width // dim))
    return cos, sin


def _dsa_prep(p, qn_g, kn_g):
    B, S, _ = p.shape
    GW, R = GROUP_W, DSA_PREP_ROWS
    cq, sq = _rope_tables(S, HEAD_DIM, GW)
    ci, si = _rope_tables(S, IDX_DIM, LANE)
    gq = jnp.tile(qn_g, N_HEADS)[None]
    gk = jnp.pad(kn_g, (0, LANE - HEAD_DIM))[None]
    rows = lambda w: pl.BlockSpec((1, R, w), lambda b, i: (b, i, 0))
    tab = lambda w: pl.BlockSpec((R, w), lambda b, i: (i, 0))
    cols = lambda r: pl.BlockSpec((1, r, R), lambda b, i: (b, 0, i))
    vec = lambda w: pl.BlockSpec((1, w), lambda b, i: (0, 0))
    return pl.pallas_call(
        _dsa_prep_kernel,
        grid=(B, S // R),
        in_specs=[rows(D_PAD), tab(GW), tab(GW), tab(LANE), tab(LANE), vec(GW), vec(LANE)],
        out_specs=[rows(GW), rows(GW), cols(GW), rows(LANE), rows(LANE), rows(LANE), rows(LANE)],
        out_shape=[jax.ShapeDtypeStruct((B, S, GW), F32), jax.ShapeDtypeStruct((B, S, GW), BF16),
                   jax.ShapeDtypeStruct((B, GW, S), BF16), jax.ShapeDtypeStruct((B, S, LANE), F32),
                   jax.ShapeDtypeStruct((B, S, LANE), BF16), jax.ShapeDtypeStruct((B, S, LANE), BF16),
                   jax.ShapeDtypeStruct((B, S, LANE), F32)],
        compiler_params=pltpu.CompilerParams(dimension_semantics=("parallel", "parallel"),
                                             vmem_limit_bytes=VMEM_LIMIT),
        name="dsa_prep",
    )(p, cq, sq, ci, si, gq, gk)


def _dsa_attn_norm(qd, kd4, vdwt, qi, ki_hi, ki_lo, wi, g):
    B, S, _ = qd.shape
    n_sel = min(TOPK_MAX, S // 4)
    assert S % DSA_KEY_STEP == 0 and n_sel <= DSA_KEY_STEP
    kls = tuple(range(DSA_KEY_STEP, S + 1, DSA_KEY_STEP))
    blk = lambda w: pl.BlockSpec((1, DSA_BLOCK, w), lambda b, i: (b, i, 0))
    per_b = lambda r, c: pl.BlockSpec((1, r, c), lambda b, i: (b, 0, 0))
    return pl.pallas_call(
        functools.partial(_dsa_kernel, kls=kls, n_sel=n_sel),
        grid=(B, S // DSA_BLOCK),
        in_specs=[blk(GROUP_W), per_b(S, GROUP_W), per_b(GROUP_W, S), blk(IDX_HEADS * IDX_DIM),
                  per_b(S, LANE), per_b(S, LANE), blk(LANE), pl.BlockSpec((1, GROUP_W), lambda b, i: (0, 0))],
        out_specs=blk(GROUP_W),
        out_shape=jax.ShapeDtypeStruct((B, S, GROUP_W), F32),
        compiler_params=pltpu.CompilerParams(dimension_semantics=("parallel", "parallel"),
                                             vmem_limit_bytes=VMEM_LIMIT),
        name="dsa_attn",
    )(qd, kd4, vdwt, qi, ki_hi, ki_lo, wi, g[None])


RWKV_CHUNK = 64
RWKV_LOW = RWKV_W_RANK + RWKV_A_RANK + RWKV_G_RANK
RWKV_GROUP = N_HEADS
RWKV_BATCH = 4


def _dot(a, b):
    return jnp.dot(a, b, preferred_element_type=F32)


def _dot_nt(a, b):
    return lax.dot_general(a, b, (((1,), (1,)), ((), ())), preferred_element_type=F32)


def _dot_tn(a, b):
    return lax.dot_general(a, b, (((0,), (0,)), ((), ())), preferred_element_type=F32)


def _split3_bf16(t):
    p1 = t.astype(BF16)
    r1 = t - p1.astype(F32)
    p2 = r1.astype(BF16)
    p3 = (r1 - p2.astype(F32)).astype(BF16)
    return p1, p2, p3


def _dot_f32_by_exact(a, b_exact):
    m = a.shape[0]
    r = _dot(jnp.concatenate(_split3_bf16(a), axis=0), b_exact)
    return r[:m] + r[m:2 * m] + r[2 * m:]


def _dot_exact_by_f32(a_exact, b):
    n = b.shape[1]
    r = _dot(a_exact, jnp.concatenate(_split3_bf16(b), axis=1))
    return r[:, :n] + r[:, n:2 * n] + r[:, 2 * n:]


def _dot3(a, b_hi, b_lo):
    a_hi, a_lo = _split_bf16(a)
    return _dot(jnp.concatenate([a_hi, a_lo, a_hi], axis=1), jnp.concatenate([b_hi, b_hi, b_lo], axis=0))


def _softplus(z):
    return jnp.maximum(z, 0.0) + jnp.log(1.0 + jnp.exp(-jnp.abs(z)))


def _rwkv_kernel(p_ref, mu_ref, vec_ref, lhi_ref, llo_ref, o_ref, state_ref, prev_ref):
    @pl.when(pl.program_id(1) == 0)
    def _():
        state_ref[...] = jnp.zeros_like(state_ref)
        prev_ref[...] = jnp.zeros_like(prev_ref)

    chains = [_rwkv_chunk(p_ref.at[n], mu_ref, vec_ref, lhi_ref, llo_ref, o_ref.at[n], state_ref.at[n],
                          prev_ref.at[n]) for n in range(RWKV_BATCH)]
    for _ in itertools.zip_longest(*chains):
        pass


def _rwkv_chunk(p_ref, mu_ref, vec_ref, lhi_ref, llo_ref, o_ref, state_ref, prev_ref):
    L, GW = RWKV_CHUNK, GROUP_W
    p = p_ref[...]
    row = lax.broadcasted_iota(jnp.int32, p.shape, 0)
    prev = jnp.where(row == 0, prev_ref[...], pltpu.roll(p, 1, axis=0))
    prev_ref[...] = p[L - 1:L, :]
    ps = p + (prev - p) * mu_ref[...]
    r, k, v = ps[:, :GW], ps[:, GW:2 * GW], ps[:, 2 * GW:3 * GW]
    low = ps[:, 3 * GW:]
    lane_low = lax.broadcasted_iota(jnp.int32, low.shape, 1)
    low = jnp.where(lane_low < RWKV_W_RANK, jnp.tanh(low),
                    jnp.where(lane_low < RWKV_W_RANK + RWKV_A_RANK, low, jax.nn.sigmoid(low)))
    up = _dot3(low, lhi_ref[...], llo_ref[...])
    w0, a0, k_k, k_a = vec_ref[0:1, :], vec_ref[1:2, :], vec_ref[2:3, :], vec_ref[3:4, :]
    r_k, ln_g, ln_b = vec_ref[4:5, :], vec_ref[5:6, :], vec_ref[6:7, :]
    logw = -jnp.exp(-_softplus(-(w0 + up[:, :GW])) - 0.5)
    rate = jax.nn.sigmoid(a0 + up[:, GW:2 * GW])
    gate = up[:, 2 * GW:]

    ri = lax.broadcasted_iota(jnp.int32, (GW, GW), 0)
    ci = lax.broadcasted_iota(jnp.int32, (GW, GW), 1)
    ones_bd = jnp.where((ri // HEAD_DIM) == (ci // HEAD_DIM), 1.0, 0.0).astype(BF16)

    kk = k * k_k
    k = k * (1.0 + (rate - 1.0) * k_a)
    seg = _dot_f32_by_exact(jnp.concatenate([kk * kk, r * k * r_k], axis=0), ones_bd)
    kk = kk / jnp.maximum(jnp.sqrt(seg[:L]), 1e-12)

    ti = lax.broadcasted_iota(jnp.int32, (L, L), 0)
    tj = lax.broadcasted_iota(jnp.int32, (L, L), 1)
    lc = _dot_exact_by_f32(jnp.where(tj <= ti, 1.0, 0.0).astype(BF16), logw)
    lc_last = lc[L - 1:L, :]
    dec_in = jnp.exp(lc)
    dec_out = jnp.exp(-lc)
    a_t = -kk * jnp.exp(lc - logw)
    b_t = kk * rate * dec_out
    k_t = k * dec_out
    r_t = r * dec_in
    to_end = jnp.exp(lc_last)

    SG = RWKV_GROUP * L
    gi = lax.broadcasted_iota(jnp.int32, (SG, SG), 0)
    gj = lax.broadcasted_iota(jnp.int32, (SG, SG), 1)
    g_same = (gi // L) == (gj // L)
    strict = g_same & ((gj % L) < (gi % L))
    incl = g_same & ((gj % L) <= (gi % L))
    eye = jnp.where(gi == gj, 1.0, 0.0)
    stack = lambda t: jnp.concatenate([t] * RWKV_GROUP, axis=0)
    bd = lambda t: jnp.where(g_same, stack(t), 0.0).astype(BF16)
    n_doublings = RWKV_CHUNK.bit_length() - 2
    ys = []
    for g in range(N_HEADS // RWKV_GROUP):
        cols = slice(g * SG, (g + 1) * SG)
        a_bd, r_bd, v_bd = bd(a_t[:, cols]), bd(r_t[:, cols]), bd(v[:, cols])
        m = _dot_nt(jnp.concatenate([a_bd, r_bd], axis=0),
                    jnp.concatenate([stack(b_t[:, cols]), stack(k_t[:, cols])], axis=0).astype(BF16))
        yield
        m_ab = jnp.where(strict, m[:SG, :SG], 0.0)
        m_ak = jnp.where(strict, m[:SG, SG:], 0.0).astype(BF16)
        m_rb = jnp.where(incl, m[SG:, :SG], 0.0).astype(BF16)
        m_rk = jnp.where(incl, m[SG:, SG:], 0.0).astype(BF16)

        inv = eye + m_ab
        pw_b = m_ab.astype(BF16)
        sq = _dot(pw_b, pw_b)
        yield
        pw_b = sq.astype(BF16)
        for s in range(n_doublings - 1):
            both = _dot(jnp.concatenate([inv.astype(BF16), pw_b], axis=0), pw_b)
            yield
            inv = inv + both[:SG]
            pw_b = both[SG:].astype(BF16)
        last = _dot(inv.astype(BF16), pw_b)
        t0 = state_ref[g]
        t0_b = t0.astype(BF16)
        rhs = _dot(jnp.concatenate([a_bd, m_ak], axis=1), jnp.concatenate([t0_b, v_bd], axis=0))
        yield
        inv_b = (inv + last).astype(BF16)
        u = _dot(inv_b, rhs.astype(BF16))
        yield
        u = u.astype(BF16)
        y_bd = _dot(jnp.concatenate([r_bd, m_rb, m_rk], axis=1),
                    jnp.concatenate([t0_b, u, v_bd], axis=0))
        end_g = to_end[:, cols]
        to_end_col = jnp.sum(jnp.where(gi == gj, jnp.broadcast_to(end_g, (SG, SG)), 0.0), axis=1, keepdims=True)
        carried = _dot_tn(jnp.concatenate([bd(b_t[:, cols] * end_g), bd(k_t[:, cols] * end_g)], axis=0),
                          jnp.concatenate([u, v_bd], axis=0))
        yield
        ys.append(sum(y_bd[h * L:(h + 1) * L, :] for h in range(RWKV_GROUP)))
        state_ref[g] = to_end_col * t0 + carried
    y = jnp.concatenate(ys, axis=1)

    inv_d = 1.0 / HEAD_DIM
    mean = _dot_f32_by_exact(y, ones_bd) * inv_d
    yield
    yc = y - mean
    var = _dot_f32_by_exact(yc * yc, ones_bd) * inv_d
    yield
    yn = yc * lax.rsqrt(var + RWKV_GN_EPS) * ln_g + ln_b
    o_ref[...] = (yn + seg[L:] * v) * gate


def _rwkv7_time_mix(p, mu, w0, w2, a0, a2, g2, k_k, k_a, r_k, ln_g, ln_b):
    B, S, _ = p.shape
    GW = GROUP_W
    assert S % RWKV_CHUNK == 0 and RWKV_CHUNK == HEAD_DIM and B % RWKV_BATCH == 0
    low_w = jnp.zeros((RWKV_LOW, 3 * GW), F32)
    low_w = low_w.at[:RWKV_W_RANK, :GW].set(w2)
    low_w = low_w.at[RWKV_W_RANK:RWKV_W_RANK + RWKV_A_RANK, GW:2 * GW].set(a2)
    low_w = low_w.at[RWKV_W_RANK + RWKV_A_RANK:, 2 * GW:].set(g2)
    l_hi, l_lo = _split_bf16(low_w)
    vecs = jnp.stack([w0, a0, k_k, k_a, r_k, ln_g, ln_b, jnp.zeros_like(w0)], 0)
    full = lambda a: pl.BlockSpec(a.shape, lambda b, c: (0,) * a.ndim)
    mu2 = mu[None]
    return pl.pallas_call(
        _rwkv_kernel,
        grid=(B // RWKV_BATCH, S // RWKV_CHUNK),
        in_specs=[pl.BlockSpec((RWKV_BATCH, RWKV_CHUNK, A_PAD), lambda b, c: (b, c, 0)),
                  full(mu2), full(vecs), full(l_hi), full(l_lo)],
        out_specs=pl.BlockSpec((RWKV_BATCH, RWKV_CHUNK, GW), lambda b, c: (b, c, 0)),
        out_shape=jax.ShapeDtypeStruct((B, S, GW), F32),
        scratch_shapes=[pltpu.VMEM((RWKV_BATCH, N_HEADS // RWKV_GROUP, RWKV_GROUP * RWKV_CHUNK,
                                    RWKV_GROUP * HEAD_DIM), F32),
                        pltpu.VMEM((RWKV_BATCH, 1, A_PAD), F32)],
        compiler_params=pltpu.CompilerParams(dimension_semantics=("parallel", "arbitrary"),
                                             vmem_limit_bytes=VMEM_LIMIT),
        name="rwkv7",
    )(p, mu2, vecs, l_hi, l_lo)


SB_UNROLL = 4


def _sb_kernel(q_ref, k_ref, v_ref, g_ref, o_ref, kbd_ref, vbd_ref):
    i = pl.program_id(1)
    T, GW, H = SB_BLOCK, GROUP_W, N_HEADS
    lane_h = lax.broadcasted_iota(jnp.int32, (T, GW), 1) // HEAD_DIM
    k_new, v_new = k_ref[0], v_ref[0]
    for h in range(H):
        kbd_ref[i, h * T:(h + 1) * T, :] = jnp.where(lane_h == h, k_new, 0.0).astype(BF16)
        vbd_ref[i, h * T:(h + 1) * T, :] = jnp.where(lane_h == h, v_new, 0.0).astype(BF16)

    q = q_ref[0].astype(BF16)
    si = lax.broadcasted_iota(jnp.int32, (T, 2 * T), 0)
    sj = lax.broadcasted_iota(jnp.int32, (T, 2 * T), 1)
    later_and_all = jnp.where((si > sj) | (sj >= T), 1.0, 0.0).astype(BF16)
    qrow = lax.broadcasted_iota(jnp.int32, (T, H * T), 0)
    kcol = lax.broadcasted_iota(jnp.int32, (T, H * T), 1) % T
    causal = kcol < qrow

    def key_block_stages(j, box, diagonal):
        z = _dot_nt(q, kbd_ref[j]) * HEAD_DIM ** -0.5
        yield
        soft = jnp.log(1.0 + jnp.exp(-jnp.abs(z)))
        log1m = -(jnp.maximum(z, 0.0) + soft)
        log_sig = jnp.minimum(z, 0.0) - soft
        log1m_in = (jnp.where(causal, log1m, 0.0) if diagonal else log1m).astype(BF16)
        sums = _dot(jnp.concatenate([log1m_in[:, h * T:(h + 1) * T] for h in range(H)], axis=0), later_and_all)
        yield
        suffix = jnp.concatenate([sums[h * T:(h + 1) * T, :T] for h in range(H)], axis=1)
        total = jnp.concatenate([sums[h * T:(h + 1) * T, T:] for h in range(H)], axis=1)
        att = jnp.exp(log_sig + suffix + box["carry"])
        if diagonal:
            att = jnp.where(causal, att, 0.0)
        box["carry"] = box["carry"] + total
        pv = _dot(att.astype(BF16), vbd_ref[j])
        yield
        box["acc"] = box["acc"] + pv

    def key_blocks(js, state, diagonal=False):
        box = {"carry": state[0], "acc": state[1]}
        for _ in itertools.zip_longest(*[key_block_stages(j, box, diagonal) for j in js]):
            pass
        return box["carry"], box["acc"]

    state = key_blocks([i], (jnp.zeros((T, H * T), F32), jnp.zeros((T, GW), F32)), True)
    rem = i % SB_UNROLL
    state = lax.fori_loop(0, rem, lambda it, st: key_blocks([i - 1 - it], st), state)
    top = i - 1 - rem
    _, y = lax.fori_loop(0, i // SB_UNROLL,
                         lambda it, st: key_blocks([top - SB_UNROLL * it - n for n in range(SB_UNROLL)], st), state)

    hi_ = lax.broadcasted_iota(jnp.int32, (GW, GW), 0) // HEAD_DIM
    hj_ = lax.broadcasted_iota(jnp.int32, (GW, GW), 1) // HEAD_DIM
    ones_bd = jnp.where(hi_ == hj_, 1.0, 0.0).astype(BF16)
    ms = _dot_f32_by_exact(y * y, ones_bd) * (1.0 / HEAD_DIM)
    o_ref[0] = y * lax.rsqrt(ms + NORM_EPS) * g_ref[...]


def _stick_breaking_norm(p, g):
    B, S, _ = p.shape
    GW = GROUP_W
    assert S % SB_BLOCK == 0
    col = lambda n: pl.BlockSpec((1, SB_BLOCK, GW), lambda b, i: (b, i, n))
    blk = col(0)
    q = k = v = p
    stacked = pltpu.VMEM((S // SB_BLOCK, N_HEADS * SB_BLOCK, GW), BF16)
    return pl.pallas_call(
        _sb_kernel,
        grid=(B, S // SB_BLOCK),
        in_specs=[col(0), col(1), col(2), pl.BlockSpec((1, GW), lambda b, i: (0, 0))],
        out_specs=blk,
        out_shape=jax.ShapeDtypeStruct((B, S, GW), F32),
        scratch_shapes=[stacked, stacked],
        compiler_params=pltpu.CompilerParams(dimension_semantics=("parallel", "arbitrary"),
                                             vmem_limit_bytes=VMEM_LIMIT),
        name="stick_breaking",
    )(q, k, v, g[None])


ML_HALO = 8
ML_GROUP = N_HEADS
ML_BATCH = 2


def _dot_nt_exact_by_f32(a_exact, b):
    return _dot_nt(jnp.concatenate([a_exact] * 3, axis=1), jnp.concatenate(_split3_bf16(b), axis=1))


def _mlstm_kernel(p_ref, cw_ref, cb_ref, gb_ref, g_ref, o_ref, ext_ref, ct_ref, n_ref, m_ref):
    @pl.when(pl.program_id(1) == 0)
    def _():
        ext_ref[...] = jnp.zeros_like(ext_ref)
        ct_ref[...] = jnp.zeros_like(ct_ref)
        n_ref[...] = jnp.zeros_like(n_ref)
        m_ref[...] = jnp.zeros_like(m_ref)

    chains = [_mlstm_chunk(p_ref.at[n], cw_ref, cb_ref, gb_ref, g_ref, o_ref.at[n], ext_ref.at[n], ct_ref.at[n],
                           n_ref.at[n], m_ref.at[n]) for n in range(ML_BATCH)]
    for _ in itertools.zip_longest(*chains):
        pass


def _mlstm_chunk(p_ref, cw_ref, cb_ref, gb_ref, g_ref, o_ref, ext_ref, ct_ref, n_ref, m_ref):
    L, GW, H = ML_CHUNK, GROUP_W, N_HEADS
    x = p_ref[...]
    ext_ref[ML_HALO:, :] = x[:, :2 * GW]
    conv = cb_ref[...]
    for j in range(ML_CONV):
        conv = conv + cw_ref[j:j + 1, :] * ext_ref[pl.ds(ML_HALO - (ML_CONV - 1) + j, L), :]
    ext_ref[:ML_HALO, :] = x[L - ML_HALO:, :2 * GW]
    qk = conv * jax.nn.sigmoid(conv)
    q, k = qk[:, :GW], qk[:, GW:] * HEAD_DIM ** -0.5
    v, o = x[:, 2 * GW:3 * GW], x[:, 3 * GW:4 * GW]

    gates = x[:, 4 * GW:]
    gi = lax.broadcasted_iota(jnp.int32, (LANE, 2 * GW), 0)
    gj = lax.broadcasted_iota(jnp.int32, (LANE, 2 * GW), 1)
    expand = jnp.where(gi == (gj % GW) // HEAD_DIM + H * (gj // GW), 1.0, 0.0).astype(BF16)
    graw = _dot_f32_by_exact(gates, expand) + gb_ref[...]
    yield
    capped = GATE_CAP * jnp.tanh(graw * (1.0 / GATE_CAP))
    log_i = capped[:, :GW]
    cf = capped[:, GW:]
    log_f = jnp.minimum(cf, 0.0) - jnp.log(1.0 + jnp.exp(-jnp.abs(cf)))

    ti = lax.broadcasted_iota(jnp.int32, (L, L), 0)
    tj = lax.broadcasted_iota(jnp.int32, (L, L), 1)
    bf = _dot_exact_by_f32(jnp.where(tj <= ti, 1.0, 0.0).astype(BF16), log_f)
    yield
    b_last = bf[L - 1:L, :]
    m_row, n_row = m_ref[...], n_ref[...]
    dec = b_last - bf + log_i
    m_new = jnp.maximum(b_last + m_row, jnp.max(dec, axis=0, keepdims=True))
    kw = k * jnp.exp(dec - m_new)
    s_old = jnp.exp(b_last + m_row - m_new)
    n_ref[...] = s_old * n_row + jnp.sum(kw, axis=0, keepdims=True)
    m_ref[...] = m_new
    g_in = bf + m_row
    li_b = log_i - bf

    SG = ML_GROUP * L
    ri = lax.broadcasted_iota(jnp.int32, (SG, SG), 0)
    ci = lax.broadcasted_iota(jnp.int32, (SG, SG), 1)
    same_head = (ri // L) == (ci // HEAD_DIM)
    first_lane = ci == (ri // L) * HEAD_DIM
    sel_first = jnp.where(first_lane, 1.0, 0.0).astype(BF16)
    stack = lambda t: jnp.concatenate([t] * ML_GROUP, axis=0)
    pick = lambda t: jnp.sum(jnp.where(first_lane, t, 0.0), axis=1, keepdims=True)
    rt = lax.broadcasted_iota(jnp.int32, (SG, L), 0) % L
    rs = lax.broadcasted_iota(jnp.int32, (SG, L), 1)
    hs = []
    for g in range(H // ML_GROUP):
        cols = slice(g * SG, (g + 1) * SG)
        ct = ct_ref[g]
        qs = jnp.where(same_head, stack(q[:, cols]), 0.0)
        qs_b = qs.astype(BF16)
        v_b = v[:, cols].astype(BF16)
        b_col = pick(stack(bf[:, cols]))
        g_col = pick(stack(g_in[:, cols]))
        row_part = _dot_nt_exact_by_f32(sel_first, li_b[:, cols])
        qk = _dot_nt(qs_b, k[:, cols].astype(BF16))
        inter = _dot(qs_b, ct.astype(BF16))
        carried = _dot_tn(kw[:, cols].astype(BF16), v_b)
        yield
        dmat = jnp.where(rs <= rt, b_col + row_part, -jnp.inf)
        m_t = jnp.maximum(g_col, jnp.max(dmat, axis=1, keepdims=True))
        s_inter = jnp.exp(g_col - m_t)
        sqk = qk * jnp.exp(dmat - m_t)
        intra = _dot(sqk.astype(BF16), v_b)
        yield
        num = s_inter * inter + jnp.where(same_head, intra, 0.0)
        den = (s_inter * jnp.sum(qs * n_row[:, cols], axis=1, keepdims=True)
               + jnp.sum(sqk, axis=1, keepdims=True))
        hst = num / jnp.maximum(jnp.abs(den), jnp.exp(-m_t))
        hs.append(sum(hst[n * L:(n + 1) * L, :] for n in range(ML_GROUP)))
        ct_ref[g] = s_old[:, cols] * ct + jnp.where(same_head, carried, 0.0)
    h = jnp.concatenate(hs, axis=1)

    hi_ = lax.broadcasted_iota(jnp.int32, (GW, GW), 0) // HEAD_DIM
    hj_ = lax.broadcasted_iota(jnp.int32, (GW, GW), 1) // HEAD_DIM
    ones_bd = jnp.where(hi_ == hj_, 1.0, 0.0).astype(BF16)
    ms = _dot_f32_by_exact(h * h, ones_bd) * (1.0 / HEAD_DIM)
    yield
    o_ref[...] = jax.nn.sigmoid(o) * (h * lax.rsqrt(ms + NORM_EPS) * g_ref[...])


def _mlstm_mix(p, conv_w, conv_b, ig_b, fg_b, norm_g):
    B, S, _ = p.shape
    GW = GROUP_W
    assert S % ML_CHUNK == 0 and ML_CONV - 1 <= ML_HALO <= ML_CHUNK and B % ML_BATCH == 0
    gate_b = jnp.concatenate([jnp.repeat(ig_b, HEAD_DIM), jnp.repeat(fg_b, HEAD_DIM)])[None]
    full = lambda a: pl.BlockSpec(a.shape, lambda b, c: (0,) * a.ndim)
    cb2, g2 = conv_b[None], norm_g[None]
    return pl.pallas_call(
        _mlstm_kernel,
        grid=(B // ML_BATCH, S // ML_CHUNK),
        in_specs=[pl.BlockSpec((ML_BATCH, ML_CHUNK, C_PAD), lambda b, c: (b, c, 0)),
                  full(conv_w), full(cb2), full(gate_b), full(g2)],
        out_specs=pl.BlockSpec((ML_BATCH, ML_CHUNK, GW), lambda b, c: (b, c, 0)),
        out_shape=jax.ShapeDtypeStruct((B, S, GW), F32),
        scratch_shapes=[pltpu.VMEM((ML_BATCH, ML_HALO + ML_CHUNK, 2 * GW), F32),
                        pltpu.VMEM((ML_BATCH, N_HEADS // ML_GROUP, ML_GROUP * ML_CHUNK, ML_GROUP * HEAD_DIM), F32),
                        pltpu.VMEM((ML_BATCH, 1, GW), F32), pltpu.VMEM((ML_BATCH, 1, GW), F32)],
        compiler_params=pltpu.CompilerParams(dimension_semantics=("parallel", "arbitrary"),
                                             vmem_limit_bytes=VMEM_LIMIT),
        name="mlstm",
    )(p, conv_w, cb2, gate_b, g2)


def _rms_norm(x, g):
    xf = x.astype(F32)
    y = xf * lax.rsqrt(jnp.mean(xf * xf, -1, keepdims=True) + NORM_EPS)
    return (y * g.astype(F32)).astype(x.dtype)


def _rope(x, pos):
    half = x.shape[-1] // 2
    inv = ROPE_THETA ** (-jnp.arange(half, dtype=F32) / half)
    ang = pos.astype(F32)[:, None] * inv[None, :]
    cos = jnp.cos(ang)[None, :, None, :]
    sin = jnp.sin(ang)[None, :, None, :]
    xf = x.astype(F32)
    x1, x2 = xf[..., :half], xf[..., half:]
    return jnp.concatenate([x1 * cos - x2 * sin, x2 * cos + x1 * sin], -1).astype(x.dtype)


def kernel(x, c, ada_w, ada_b, norm1_g, norm2_g, w_in, rk_mu, rk_w0, rk_w2, rk_a0, rk_a2, rk_g2, rk_kk, rk_ka, rk_rk, rk_ln_g, rk_ln_b, sb_norm_g, ml_conv_w, ml_conv_b, ml_ig_b, ml_fg_b, ml_norm_g, ds_qn_g, ds_kn_g, ds_out_g, w_out, moe_wg, moe_bg, moe_we, moe_be, moe_w1, moe_w3, moe_w2):
    B, S, D = x.shape
    H, d = N_HEADS, HEAD_DIM
    depth = ada_w.shape[0]
    pos = jnp.arange(S)
    c_act = jax.nn.silu(c)
    for l in range(depth):
        mod = (c_act @ ada_w[l] + ada_b[l])[:, None, :]
        sh1, sc1, gt1, sh2, sc2, gt2 = jnp.split(mod, 6, axis=-1)

        pA, pB, pC, pD = _in_proj(x, sc1, sh1, norm1_g[l][None], _pad_w_in(w_in[l]))

        yA = _rwkv7_time_mix(pA, rk_mu[l], rk_w0[l], rk_w2[l], rk_a0[l], rk_a2[l], rk_g2[l],
                             rk_kk[l], rk_ka[l], rk_rk[l], rk_ln_g[l], rk_ln_b[l])

        yB = _stick_breaking_norm(pB, sb_norm_g[l])

        yC = _mlstm_mix(pC, ml_conv_w[l], ml_conv_b[l], ml_ig_b[l], ml_fg_b[l], ml_norm_g[l])

        yD = _dsa_attn_norm(*_dsa_prep(pD, ds_qn_g[l], ds_kn_g[l]), ds_out_g[l])

        router = jnp.pad(jnp.concatenate([moe_wg[l], moe_we[l]], 1),
                         ((0, 0), (0, ROUTER_PAD - N_GROUPS - N_EXPERTS)))
        r_hi, r_lo = _split_bf16(router)
        r_b = jnp.pad(jnp.concatenate([moe_bg[l], moe_be[l]]), (0, ROUTER_PAD - N_GROUPS - N_EXPERTS))[None]
        x1, h2, route = _out_proj((yA, yB, yC, yD), x, gt1, sc2, sh2, norm2_g[l][None],
                                  w_out[l].astype(BF16), r_hi, r_lo, r_b)

        moe = _hier_moe(h2.reshape(B * S, D), route.reshape(B * S, ROUTER_PAD),
                        moe_w1[l], moe_w3[l], moe_w2[l])
        x = x1 + gt2 * moe.reshape(B, S, D)
    return x
```

```python
import functools
import itertools

import jax
import jax.numpy as jnp
import numpy as np
from jax import lax
from jax.experimental import pallas as pl
from jax.experimental.pallas import tpu as pltpu

F32 = jnp.float32
BF16 = jnp.bfloat16

D_MODEL = 1024
N_MIXERS = 4
GROUP_W = D_MODEL // N_MIXERS
HEAD_DIM = 64
N_HEADS = GROUP_W // HEAD_DIM
NORM_EPS = 1e-6
RWKV_W_RANK = 32
RWKV_A_RANK = 32
RWKV_G_RANK = 64
RWKV_GN_EPS = 64e-5
SB_BLOCK = 128
ML_CHUNK = 64
ML_CONV = 4
GATE_CAP = 15.0
DSA_BLOCK = 128
IDX_HEADS = 4
IDX_DIM = 32
TOPK_MAX = 256
ROPE_THETA = 10000.0
N_GROUPS = 4
EXP_PER_GROUP = 8
N_EXPERTS = N_GROUPS * EXP_PER_GROUP
EXPERT_FF = D_MODEL // 2
TOP_IN_GROUP = 2

A_SIZES = (GROUP_W, GROUP_W, GROUP_W, RWKV_W_RANK, RWKV_A_RANK, RWKV_G_RANK)
B_SIZES = (GROUP_W, GROUP_W, GROUP_W)
C_SIZES = (GROUP_W, GROUP_W, GROUP_W, GROUP_W, N_HEADS, N_HEADS)
D_SIZES = (GROUP_W, HEAD_DIM, HEAD_DIM, IDX_HEADS * IDX_DIM, IDX_DIM, IDX_HEADS)
A_COLS = sum(A_SIZES)
B_COLS = sum(B_SIZES)
C_COLS = sum(C_SIZES)
D_COLS = sum(D_SIZES)

LANE = 128
A_PAD = 896
B_PAD = 768
C_PAD = 1152
D_PAD = 640
P_PAD = A_PAD + B_PAD + C_PAD + D_PAD
ROUTER_PAD = LANE

IN_ROWS = 256
OUT_ROWS = 512
MOE_ROWS = 512
VMEM_LIMIT = 48 * 1024 * 1024


def _split_cols(t, sizes):
    return jnp.split(t, [int(i) for i in np.cumsum(sizes)[:-1]], axis=-1)


def _in_proj_kernel(x_ref, sc_ref, sh_ref, g_ref, w_ref, oa_ref, ob_ref, oc_ref, od_ref):
    x = x_ref[0]
    y = x * lax.rsqrt(jnp.mean(x * x, -1, keepdims=True) + NORM_EPS) * g_ref[...]
    h = y * (1.0 + sc_ref[0]) + sh_ref[0]
    p = jnp.dot(h.astype(BF16), w_ref[...], preferred_element_type=F32)
    oa_ref[0] = p[:, :A_PAD]
    ob_ref[0] = p[:, A_PAD:A_PAD + B_PAD]
    oc_ref[0] = p[:, A_PAD + B_PAD:A_PAD + B_PAD + C_PAD]
    od_ref[0] = p[:, A_PAD + B_PAD + C_PAD:]


def _in_proj(x, sc, sh, g, w_pad):
    B, S, D = x.shape
    row = lambda w: pl.BlockSpec((1, IN_ROWS, w), lambda b, i: (b, i, 0))
    vec = pl.BlockSpec((1, 1, D), lambda b, i: (b, 0, 0))
    return pl.pallas_call(
        _in_proj_kernel,
        grid=(B, S // IN_ROWS),
        in_specs=[row(D), vec, vec, pl.BlockSpec((1, D), lambda b, i: (0, 0)),
                  pl.BlockSpec((D, P_PAD), lambda b, i: (0, 0))],
        out_specs=[row(A_PAD), row(B_PAD), row(C_PAD), row(D_PAD)],
        out_shape=[jax.ShapeDtypeStruct((B, S, w), F32) for w in (A_PAD, B_PAD, C_PAD, D_PAD)],
        compiler_params=pltpu.CompilerParams(dimension_semantics=("parallel", "parallel"),
                                             vmem_limit_bytes=VMEM_LIMIT),
        name="in_proj",
    )(x, sc, sh, g, w_pad)


def _pad_w_in(w):
    wa, wb, wc, wd = _split_cols(w, (A_COLS, B_COLS, C_COLS, D_COLS))
    padc = lambda t, n: jnp.pad(t, ((0, 0), (0, n - t.shape[1])))
    return jnp.concatenate([padc(wa, A_PAD), padc(wb, B_PAD), padc(wc, C_PAD), padc(wd, D_PAD)], 1).astype(BF16)


def _split_bf16(t):
    hi = t.astype(BF16)
    lo = (t - hi.astype(F32)).astype(BF16)
    return hi, lo


def _out_proj_kernel(ya_ref, yb_ref, yc_ref, yd_ref, x_ref, gt_ref, sc_ref, sh_ref, g_ref, w_ref,
                     rhi_ref, rlo_ref, rb_ref, x1_ref, h2_ref, route_ref):
    acc = jnp.zeros(x_ref.shape[1:], F32)
    for n, y_ref in enumerate((ya_ref, yb_ref, yc_ref, yd_ref)):
        acc += jnp.dot(y_ref[0].astype(BF16), w_ref[n * GROUP_W:(n + 1) * GROUP_W, :],
                       preferred_element_type=F32)
    x1 = x_ref[0] + gt_ref[0] * acc
    x1_ref[0] = x1
    y = x1 * lax.rsqrt(jnp.mean(x1 * x1, -1, keepdims=True) + NORM_EPS) * g_ref[...]
    h = y * (1.0 + sc_ref[0]) + sh_ref[0]
    hi, lo = _split_bf16(h)
    h2_ref[0] = hi
    lg = (jnp.dot(hi, rhi_ref[...], preferred_element_type=F32)
          + jnp.dot(lo, rhi_ref[...], preferred_element_type=F32)
          + jnp.dot(hi, rlo_ref[...], preferred_element_type=F32)) + rb_ref[...]
    route_ref[0] = _route(lg)


def _route(lg):
    lane = lax.broadcasted_iota(jnp.int32, lg.shape, 1)
    neg = -jnp.inf
    first = lambda hit: jnp.min(jnp.where(hit, lane, ROUTER_PAD), axis=1, keepdims=True)
    is_grp = lane < N_GROUPS
    grp = jnp.where(is_grp, lg, neg)
    g_max = jnp.max(grp, axis=1, keepdims=True)
    g_p = 1.0 / jnp.sum(jnp.where(is_grp, jnp.exp(grp - g_max), 0.0), axis=1, keepdims=True)
    g_idx = first(grp == g_max)
    e_lane = lane - N_GROUPS
    in_group = (e_lane >= 0) & (e_lane < N_EXPERTS) & (e_lane // EXP_PER_GROUP == g_idx)
    e_log = jnp.where(in_group, lg, neg)
    e1_max = jnp.max(e_log, axis=1, keepdims=True)
    e1_lane = first(e_log == e1_max)
    e_log2 = jnp.where(lane == e1_lane, neg, e_log)
    e2_max = jnp.max(e_log2, axis=1, keepdims=True)
    e2_lane = first(e_log2 == e2_max)
    ratio = jnp.exp(e2_max - e1_max)
    gate1 = g_p / (1.0 + ratio)
    gate2 = gate1 * ratio
    out = jnp.where(lane == 0, (e1_lane - N_GROUPS).astype(F32), 0.0)
    out = jnp.where(lane == 1, (e2_lane - N_GROUPS).astype(F32), out)
    out = jnp.where(lane == 2, gate1, out)
    return jnp.where(lane == 3, gate2, out)


def _out_proj(ys, x, gt, sc, sh, g, w_out, r_hi, r_lo, r_b):
    B, S, D = x.shape
    row = lambda w: pl.BlockSpec((1, OUT_ROWS, w), lambda b, i: (b, i, 0))
    vec = pl.BlockSpec((1, 1, D), lambda b, i: (b, 0, 0))
    full = lambda a: pl.BlockSpec(a.shape, lambda b, i: (0,) * a.ndim)
    return pl.pallas_call(
        _out_proj_kernel,
        grid=(B, S // OUT_ROWS),
        in_specs=[row(GROUP_W)] * 4 + [row(D), vec, vec, vec, full(g), full(w_out), full(r_hi), full(r_lo),
                                        full(r_b)],
        out_specs=[row(D), row(D), row(ROUTER_PAD)],
        out_shape=[jax.ShapeDtypeStruct((B, S, D), F32), jax.ShapeDtypeStruct((B, S, D), BF16),
                   jax.ShapeDtypeStruct((B, S, ROUTER_PAD), F32)],
        compiler_params=pltpu.CompilerParams(dimension_semantics=("parallel", "parallel"),
                                             vmem_limit_bytes=VMEM_LIMIT),
        name="out_proj",
    )(*ys, x, gt, sc, sh, g, w_out, r_hi, r_lo, r_b)


def _moe_ffn_kernel(blk_e_ref, x_ref, wt_ref, w1_ref, w3_ref, w2_ref, o_ref, w1b_ref, w3b_ref, w2b_ref):
    i = pl.program_id(0)
    changed = jnp.logical_or(i == 0, blk_e_ref[i] != blk_e_ref[jnp.maximum(i - 1, 0)])

    @pl.when(changed)
    def _():
        w1b_ref[...] = w1_ref[0, 0].astype(BF16)
        w3b_ref[...] = w3_ref[0, 0].astype(BF16)
        w2b_ref[...] = w2_ref[0, 0].astype(BF16)

    xb = x_ref[...]
    a = jnp.dot(xb, w1b_ref[...], preferred_element_type=F32)
    b = jnp.dot(xb, w3b_ref[...], preferred_element_type=F32)
    hmid = (a * jax.nn.sigmoid(a) * b).astype(BF16)
    y = jnp.dot(hmid, w2b_ref[...], preferred_element_type=F32)
    o_ref[...] = (y * wt_ref[...]).astype(o_ref.dtype)


def _moe_ffn(blk_e, xs, wt, w1, w3, w2, layer):
    n_slots, D = xs.shape
    n_blocks = n_slots // MOE_ROWS
    FF = w1.shape[-1]
    return pl.pallas_call(
        _moe_ffn_kernel,
        grid_spec=pltpu.PrefetchScalarGridSpec(
            num_scalar_prefetch=1,
            grid=(n_blocks,),
            in_specs=[pl.BlockSpec((MOE_ROWS, D), lambda i, e: (i, 0)),
                      pl.BlockSpec((MOE_ROWS, 1), lambda i, e: (i, 0)),
                      pl.BlockSpec((1, 1, D, FF), lambda i, e: (layer, e[i], 0, 0)),
                      pl.BlockSpec((1, 1, D, FF), lambda i, e: (layer, e[i], 0, 0)),
                      pl.BlockSpec((1, 1, FF, D), lambda i, e: (layer, e[i], 0, 0))],
            out_specs=pl.BlockSpec((MOE_ROWS, D), lambda i, e: (i, 0)),
            scratch_shapes=[pltpu.VMEM((D, FF), BF16), pltpu.VMEM((D, FF), BF16), pltpu.VMEM((FF, D), BF16)],
        ),
        out_shape=jax.ShapeDtypeStruct((n_slots, D), BF16),
        compiler_params=pltpu.CompilerParams(dimension_semantics=("arbitrary",),
                                             vmem_limit_bytes=VMEM_LIMIT),
        name="moe_ffn",
    )(blk_e, xs, wt, w1, w3, w2)


def _hier_moe(h2, route, w1, w3, w2, layer):
    N, D = h2.shape
    expert = route[:, :TOP_IN_GROUP].astype(jnp.int32)
    gate = route[:, TOP_IN_GROUP:2 * TOP_IN_GROUP]
    n_asg = N * TOP_IN_GROUP
    flat_e = expert.reshape(n_asg // LANE, LANE)
    order = jnp.argsort(flat_e.reshape(n_asg))
    rank = jnp.argsort(order).astype(jnp.int32)
    counts = jnp.sum(flat_e[None] == jnp.arange(N_EXPERTS)[:, None, None], axis=(1, 2)).astype(jnp.int32)
    start = jnp.cumsum(counts) - counts
    pad_counts = (counts + MOE_ROWS - 1) // MOE_ROWS * MOE_ROWS
    pad_end = jnp.cumsum(pad_counts)
    pad_start = pad_end - pad_counts
    shift = pad_start - start
    asg_shift = jnp.zeros_like(flat_e)
    for e in range(N_EXPERTS):
        asg_shift = jnp.where(flat_e == e, shift[e], asg_shift)
    asg_slot = (rank + asg_shift.reshape(n_asg)).reshape(N, TOP_IN_GROUP)
    n_blocks = n_asg // MOE_ROWS + N_EXPERTS
    blk_start = jnp.arange(n_blocks) * MOE_ROWS
    blk_e = jnp.minimum(jnp.sum(pad_end[None, :] <= blk_start[:, None], 1), N_EXPERTS - 1).astype(jnp.int32)
    blk_pos = blk_start - pad_start[blk_e]
    row = jnp.arange(MOE_ROWS)[None, :]
    slot_real = (blk_pos[:, None] + row < counts[blk_e][:, None]) & (blk_start[:, None] < pad_end[N_EXPERTS - 1])
    slot_src = jnp.where(slot_real, (start[blk_e] + blk_pos)[:, None] + row, 0).reshape(-1)
    slot_asg = order[slot_src]
    slot_real = slot_real.reshape(-1)
    slot_tok = jnp.where(slot_real, slot_asg // TOP_IN_GROUP, 0).astype(jnp.int32)
    slot_w = jnp.where(slot_real, gate.reshape(n_asg)[slot_asg], 0.0)
    yb = _moe_ffn(blk_e, h2[slot_tok], slot_w[:, None], w1, w3, w2, layer)
    return yb[asg_slot[:, 0]].astype(F32) + yb[asg_slot[:, 1]].astype(F32)


INT_MIN = -2 ** 31
DSA_KEY_STEP = 512


def _float_order_key(x):
    bits = pltpu.bitcast(x, jnp.int32)
    bits = jnp.where(x == 0.0, 0, bits)
    return bits ^ ((bits >> 31) & 0x7FFFFFFF)


COL_PART = 64


def _col_reduce(x, reduce):
    part = reduce(x.reshape(x.shape[0] // COL_PART, COL_PART, x.shape[1]), axis=0)
    return reduce(part, axis=0, keepdims=True)


def _col_count(mask):
    return _col_reduce(jnp.where(mask, 1.0, 0.0), jnp.sum)


def _head_block_diag(t, group):
    n_heads = t.shape[0] // group
    row_h = lax.broadcasted_iota(jnp.int32, t.shape, 0) // group
    return jnp.concatenate([jnp.where(row_h == h, t, 0.0) for h in range(n_heads)], axis=1)


def _dsa_block_t(qd_ref, kd4_ref, vdwt_ref, qi_ref, kihi_ref, kilo_ref, wi_ref, g_ref, o_ref, *, kl, n_sel):
    Q = DSA_BLOCK
    q0 = pl.program_id(1) * Q
    w_hi, w_lo = _split_bf16(_head_block_diag(jnp.transpose(qi_ref[0]), IDX_DIM))
    k_hi, k_lo = kihi_ref[0, :kl, :], kilo_ref[0, :kl, :]
    sc = _dot(jnp.concatenate([k_hi, k_lo, k_hi], axis=1),
              jnp.concatenate([w_hi, w_hi, w_lo], axis=0))
    wit = jnp.transpose(wi_ref[0])
    score = sum(wit[h:h + 1, :] * jnp.maximum(sc[:, h * Q:(h + 1) * Q], 0.0) for h in range(IDX_HEADS))
    kidx = lax.broadcasted_iota(jnp.int32, (kl, Q), 0)
    qpos = q0 + lax.broadcasted_iota(jnp.int32, (kl, Q), 1)
    adm = kidx <= qpos
    key = _float_order_key(jnp.where(adm, score, -jnp.inf))

    def value_bit(it, tau):
        cand = tau | jnp.left_shift(jnp.int32(1), 31 - it)
        return jnp.where(_col_count(key >= (cand ^ INT_MIN)) >= n_sel, cand, tau)

    tau = lax.fori_loop(0, 32, value_bit, jnp.zeros((1, Q), jnp.int32)) ^ INT_MIN
    gt = key > tau
    eq = (key == tau) & adm
    need = n_sel - _col_count(gt)
    n_eq = _col_count(eq)

    def index_bits():
        def index_bit(it, bound):
            cand = bound | jnp.left_shift(jnp.int32(1), 11 - it)
            return jnp.where(_col_count(eq & (kidx < cand)) <= need, cand, bound)
        return lax.fori_loop(0, 12, index_bit, jnp.zeros((1, Q), jnp.int32))

    bound = lax.cond(jnp.max(n_eq - need) > 0.0, index_bits, lambda: jnp.full((1, Q), kl, jnp.int32))
    sel = gt | (eq & (kidx < bound))

    w_att = _head_block_diag(jnp.transpose(qd_ref[0]), HEAD_DIM).astype(BF16)
    lg = _dot(kd4_ref[0, :kl, :], w_att) * HEAD_DIM ** -0.5
    lg = jnp.where(jnp.concatenate([sel] * N_HEADS, axis=1), lg, -jnp.inf)
    p = jnp.exp(lg - _col_reduce(lg, jnp.max))
    out_t = _dot(vdwt_ref[0, :, :kl], p.astype(BF16)) / _col_reduce(p, jnp.sum)
    row_h = lax.broadcasted_iota(jnp.int32, (GROUP_W, Q), 0) // HEAD_DIM
    nat_t = sum(jnp.where(row_h == h, out_t[:, h * Q:(h + 1) * Q], 0.0) for h in range(N_HEADS))
    r = jnp.transpose(nat_t)
    hi_ = lax.broadcasted_iota(jnp.int32, (GROUP_W, GROUP_W), 0) // HEAD_DIM
    hj_ = lax.broadcasted_iota(jnp.int32, (GROUP_W, GROUP_W), 1) // HEAD_DIM
    ms = _dot_f32_by_exact(r * r, jnp.where(hi_ == hj_, 1.0, 0.0).astype(BF16)) * (1.0 / HEAD_DIM)
    o_ref[0] = r * lax.rsqrt(ms + NORM_EPS) * g_ref[...]


def _dsa_kernel(qd_ref, kd4_ref, vdwt_ref, qi_ref, kihi_ref, kilo_ref, wi_ref, g_ref, o_ref, *, kls, n_sel):
    blocks_per_step = DSA_KEY_STEP // DSA_BLOCK
    for j, kl in enumerate(kls):
        @pl.when(pl.program_id(1) // blocks_per_step == j)
        def _():
            _dsa_block_t(qd_ref, kd4_ref, vdwt_ref, qi_ref, kihi_ref, kilo_ref, wi_ref, g_ref, o_ref,
                         kl=kl, n_sel=n_sel)


DSA_PREP_ROWS = 256
D_Q, D_KV, D_QI, D_KW = 0, GROUP_W, GROUP_W + LANE, GROUP_W + 2 * LANE


def _swap_halves(x, half):
    n = x.shape[1]
    lane = lax.broadcasted_iota(jnp.int32, x.shape, 1)
    return jnp.where(lane % (2 * half) < half, pltpu.roll(x, n - half, axis=1), pltpu.roll(x, half, axis=1))


def _dsa_prep_kernel(p_ref, cq_ref, sq_ref, ci_ref, si_ref, gq_ref, gk_ref,
                     qd_ref, kd4_ref, vdwt_ref, qi_ref, kihi_ref, kilo_ref, wi_ref):
    GW = GROUP_W
    x = p_ref[0]
    hi_ = lax.broadcasted_iota(jnp.int32, (GW, GW), 0) // HEAD_DIM
    hj_ = lax.broadcasted_iota(jnp.int32, (GW, GW), 1) // HEAD_DIM
    ones_bd = jnp.where(hi_ == hj_, 1.0, 0.0).astype(BF16)
    q = x[:, D_Q:D_Q + GW]
    q = q * lax.rsqrt(_dot_f32_by_exact(q * q, ones_bd) * (1.0 / HEAD_DIM) + NORM_EPS) * gq_ref[...]
    qd_ref[0] = q * cq_ref[...] + _swap_halves(q, HEAD_DIM // 2) * sq_ref[...]
    kv = x[:, D_KV:D_KV + LANE]
    lane = lax.broadcasted_iota(jnp.int32, kv.shape, 1)
    is_k = lane < HEAD_DIM
    ms = jnp.sum(jnp.where(is_k, kv * kv, 0.0), axis=1, keepdims=True) * (1.0 / HEAD_DIM)
    kn = kv * lax.rsqrt(ms + NORM_EPS) * gk_ref[...]
    kr = kn * cq_ref[:, :LANE] + _swap_halves(kn, HEAD_DIM // 2) * sq_ref[:, :LANE]
    k2 = jnp.where(is_k, kr, pltpu.roll(kr, HEAD_DIM, axis=1))
    kd4_ref[0] = jnp.concatenate([k2] * (GW // LANE), axis=1).astype(BF16)
    v2 = jnp.where(is_k, pltpu.roll(kv, HEAD_DIM, axis=1), kv)
    v2t = jnp.transpose(v2)
    vdwt_ref[0] = jnp.concatenate([v2t] * (GW // LANE), axis=0).astype(BF16)
    qi = x[:, D_QI:D_QI + LANE]
    qi_ref[0] = qi * ci_ref[...] + _swap_halves(qi, IDX_DIM // 2) * si_ref[...]
    kw = x[:, D_KW:D_KW + LANE]
    kir = kw * ci_ref[...] + _swap_halves(kw, IDX_DIM // 2) * si_ref[...]
    ki1 = jnp.where(lane < IDX_DIM, kir, 0.0)
    ki2 = ki1 + pltpu.roll(ki1, IDX_DIM, axis=1)
    ki4 = ki2 + pltpu.roll(ki2, 2 * IDX_DIM, axis=1)
    kihi_ref[0], kilo_ref[0] = _split_bf16(ki4)
    wi_ref[0] = pltpu.roll(kw, LANE - IDX_DIM, axis=1) * (IDX_HEADS ** -0.5 * IDX_DIM ** -0.5)


def _rope_tables(S, dim, width):
    half = dim // 2
    inv = ROPE_THETA ** (-jnp.arange(half, dtype=F32) / half)
    ang = jnp.arange(S, dtype=F32)[:, None] * inv[None, :]
    cos = jnp.tile(jnp.cos(ang), (1, width // half))
    sin = jnp.tile(jnp.concatenate([-jnp.sin(ang), jnp.sin(ang)], axis=1), (1, width // dim))
    return cos, sin


def _dsa_prep(p, qn_g, kn_g):
    B, S, _ = p.shape
    GW, R = GROUP_W, DSA_PREP_ROWS
    cq, sq = _rope_tables(S, HEAD_DIM, GW)
    ci, si = _rope_tables(S, IDX_DIM, LANE)
    gq = jnp.tile(qn_g, N_HEADS)[None]
    gk = jnp.pad(kn_g, (0, LANE - HEAD_DIM))[None]
    rows = lambda w: pl.BlockSpec((1, R, w), lambda b, i: (b, i, 0))
    tab = lambda w: pl.BlockSpec((R, w), lambda b, i: (i, 0))
    cols = lambda r: pl.BlockSpec((1, r, R), lambda b, i: (b, 0, i))
    vec = lambda w: pl.BlockSpec((1, w), lambda b, i: (0, 0))
    return pl.pallas_call(
        _dsa_prep_kernel,
        grid=(B, S // R),
        in_specs=[rows(D_PAD), tab(GW), tab(GW), tab(LANE), tab(LANE), vec(GW), vec(LANE)],
        out_specs=[rows(GW), rows(GW), cols(GW), rows(LANE), rows(LANE), rows(LANE), rows(LANE)],
        out_shape=[jax.ShapeDtypeStruct((B, S, GW), F32), jax.ShapeDtypeStruct((B, S, GW), BF16),
                   jax.ShapeDtypeStruct((B, GW, S), BF16), jax.ShapeDtypeStruct((B, S, LANE), F32),
                   jax.ShapeDtypeStruct((B, S, LANE), BF16), jax.ShapeDtypeStruct((B, S, LANE), BF16),
                   jax.ShapeDtypeStruct((B, S, LANE), F32)],
        compiler_params=pltpu.CompilerParams(dimension_semantics=("parallel", "parallel"),
                                             vmem_limit_bytes=VMEM_LIMIT),
        name="dsa_prep",
    )(p, cq, sq, ci, si, gq, gk)


def _dsa_attn_norm(qd, kd4, vdwt, qi, ki_hi, ki_lo, wi, g):
    B, S, _ = qd.shape
    n_sel = min(TOPK_MAX, S // 4)
    assert S % DSA_KEY_STEP == 0 and n_sel <= DSA_KEY_STEP
    kls = tuple(range(DSA_KEY_STEP, S + 1, DSA_KEY_STEP))
    blk = lambda w: pl.BlockSpec((1, DSA_BLOCK, w), lambda b, i: (b, i, 0))
    per_b = lambda r, c: pl.BlockSpec((1, r, c), lambda b, i: (b, 0, 0))
    return pl.pallas_call(
        functools.partial(_dsa_kernel, kls=kls, n_sel=n_sel),
        grid=(B, S // DSA_BLOCK),
        in_specs=[blk(GROUP_W), per_b(S, GROUP_W), per_b(GROUP_W, S), blk(IDX_HEADS * IDX_DIM),
                  per_b(S, LANE), per_b(S, LANE), blk(LANE), pl.BlockSpec((1, GROUP_W), lambda b, i: (0, 0))],
        out_specs=blk(GROUP_W),
        out_shape=jax.ShapeDtypeStruct((B, S, GROUP_W), F32),
        compiler_params=pltpu.CompilerParams(dimension_semantics=("parallel", "parallel"),
                                             vmem_limit_bytes=VMEM_LIMIT),
        name="dsa_attn",
    )(qd, kd4, vdwt, qi, ki_hi, ki_lo, wi, g[None])


RWKV_CHUNK = 64
RWKV_LOW = RWKV_W_RANK + RWKV_A_RANK + RWKV_G_RANK
RWKV_GROUP = N_HEADS
RWKV_BATCH = 8


def _dot(a, b):
    return jnp.dot(a, b, preferred_element_type=F32)


def _dot_nt(a, b):
    return lax.dot_general(a, b, (((1,), (1,)), ((), ())), preferred_element_type=F32)


def _dot_tn(a, b):
    return lax.dot_general(a, b, (((0,), (0,)), ((), ())), preferred_element_type=F32)


def _split3_bf16(t):
    p1 = t.astype(BF16)
    r1 = t - p1.astype(F32)
    p2 = r1.astype(BF16)
    p3 = (r1 - p2.astype(F32)).astype(BF16)
    return p1, p2, p3


def _dot_f32_by_exact(a, b_exact):
    m = a.shape[0]
    r = _dot(jnp.concatenate(_split3_bf16(a), axis=0), b_exact)
    return r[:m] + r[m:2 * m] + r[2 * m:]


def _dot_exact_by_f32(a_exact, b):
    n = b.shape[1]
    r = _dot(a_exact, jnp.concatenate(_split3_bf16(b), axis=1))
    return r[:, :n] + r[:, n:2 * n] + r[:, 2 * n:]


def _dot3(a, b_hi, b_lo):
    a_hi, a_lo = _split_bf16(a)
    return _dot(jnp.concatenate([a_hi, a_lo, a_hi], axis=1), jnp.concatenate([b_hi, b_hi, b_lo], axis=0))


def _softplus(z):
    return jnp.maximum(z, 0.0) + jnp.log(1.0 + jnp.exp(-jnp.abs(z)))


def _rwkv_kernel(p_ref, mu_ref, vec_ref, lhi_ref, llo_ref, o_ref, state_ref, prev_ref):
    @pl.when(pl.program_id(1) == 0)
    def _():
        state_ref[...] = jnp.zeros_like(state_ref)
        prev_ref[...] = jnp.zeros_like(prev_ref)

    chains = [_rwkv_chunk(p_ref.at[n], mu_ref, vec_ref, lhi_ref, llo_ref, o_ref.at[n], state_ref.at[n],
                          prev_ref.at[n]) for n in range(RWKV_BATCH)]
    for _ in itertools.zip_longest(*chains):
        pass


def _rwkv_chunk(p_ref, mu_ref, vec_ref, lhi_ref, llo_ref, o_ref, state_ref, prev_ref):
    L, GW = RWKV_CHUNK, GROUP_W
    p = p_ref[...]
    row = lax.broadcasted_iota(jnp.int32, p.shape, 0)
    prev = jnp.where(row == 0, prev_ref[...], pltpu.roll(p, 1, axis=0))
    prev_ref[...] = p[L - 1:L, :]
    ps = p + (prev - p) * mu_ref[...]
    r, k, v = ps[:, :GW], ps[:, GW:2 * GW], ps[:, 2 * GW:3 * GW]
    low = ps[:, 3 * GW:]
    lane_low = lax.broadcasted_iota(jnp.int32, low.shape, 1)
    low = jnp.where(lane_low < RWKV_W_RANK, jnp.tanh(low),
                    jnp.where(lane_low < RWKV_W_RANK + RWKV_A_RANK, low, jax.nn.sigmoid(low)))
    up = _dot3(low, lhi_ref[...], llo_ref[...])
    w0, a0, k_k, k_a = vec_ref[0:1, :], vec_ref[1:2, :], vec_ref[2:3, :], vec_ref[3:4, :]
    r_k, ln_g, ln_b = vec_ref[4:5, :], vec_ref[5:6, :], vec_ref[6:7, :]
    logw = -jnp.exp(-_softplus(-(w0 + up[:, :GW])) - 0.5)
    rate = jax.nn.sigmoid(a0 + up[:, GW:2 * GW])
    gate = up[:, 2 * GW:]

    ri = lax.broadcasted_iota(jnp.int32, (GW, GW), 0)
    ci = lax.broadcasted_iota(jnp.int32, (GW, GW), 1)
    ones_bd = jnp.where((ri // HEAD_DIM) == (ci // HEAD_DIM), 1.0, 0.0).astype(BF16)

    kk = k * k_k
    k = k * (1.0 + (rate - 1.0) * k_a)
    seg = _dot_f32_by_exact(jnp.concatenate([kk * kk, r * k * r_k], axis=0), ones_bd)
    kk = kk / jnp.maximum(jnp.sqrt(seg[:L]), 1e-12)

    ti = lax.broadcasted_iota(jnp.int32, (L, L), 0)
    tj = lax.broadcasted_iota(jnp.int32, (L, L), 1)
    lc = _dot_exact_by_f32(jnp.where(tj <= ti, 1.0, 0.0).astype(BF16), logw)
    lc_last = lc[L - 1:L, :]
    dec_in = jnp.exp(lc)
    dec_out = jnp.exp(-lc)
    a_t = -kk * jnp.exp(lc - logw)
    b_t = kk * rate * dec_out
    k_t = k * dec_out
    r_t = r * dec_in
    to_end = jnp.exp(lc_last)

    SG = RWKV_GROUP * L
    gi = lax.broadcasted_iota(jnp.int32, (SG, SG), 0)
    gj = lax.broadcasted_iota(jnp.int32, (SG, SG), 1)
    g_same = (gi // L) == (gj // L)
    strict = g_same & ((gj % L) < (gi % L))
    incl = g_same & ((gj % L) <= (gi % L))
    eye = jnp.where(gi == gj, 1.0, 0.0)
    stack = lambda t: jnp.concatenate([t] * RWKV_GROUP, axis=0)
    bd = lambda t: jnp.where(g_same, stack(t), 0.0).astype(BF16)
    n_doublings = RWKV_CHUNK.bit_length() - 2
    ys = []
    for g in range(N_HEADS // RWKV_GROUP):
        cols = slice(g * SG, (g + 1) * SG)
        a_bd, r_bd, v_bd = bd(a_t[:, cols]), bd(r_t[:, cols]), bd(v[:, cols])
        m = _dot_nt(jnp.concatenate([a_bd, r_bd], axis=0),
                    jnp.concatenate([stack(b_t[:, cols]), stack(k_t[:, cols])], axis=0).astype(BF16))
        yield
        m_ab = jnp.where(strict, m[:SG, :SG], 0.0)
        m_ak = jnp.where(strict, m[:SG, SG:], 0.0).astype(BF16)
        m_rb = jnp.where(incl, m[SG:, :SG], 0.0).astype(BF16)
        m_rk = jnp.where(incl, m[SG:, SG:], 0.0).astype(BF16)

        inv = eye + m_ab
        pw_b = m_ab.astype(BF16)
        sq = _dot(pw_b, pw_b)
        yield
        pw_b = sq.astype(BF16)
        for s in range(n_doublings - 1):
            both = _dot(jnp.concatenate([inv.astype(BF16), pw_b], axis=0), pw_b)
            yield
            inv = inv + both[:SG]
            pw_b = both[SG:].astype(BF16)
        last = _dot(inv.astype(BF16), pw_b)
        t0 = state_ref[g]
        t0_b = t0.astype(BF16)
        rhs = _dot(jnp.concatenate([a_bd, m_ak], axis=1), jnp.concatenate([t0_b, v_bd], axis=0))
        yield
        inv_b = (inv + last).astype(BF16)
        u = _dot(inv_b, rhs.astype(BF16))
        yield
        u = u.astype(BF16)
        y_bd = _dot(jnp.concatenate([r_bd, m_rb, m_rk], axis=1),
                    jnp.concatenate([t0_b, u, v_bd], axis=0))
        end_g = to_end[:, cols]
        to_end_col = jnp.sum(jnp.where(gi == gj, jnp.broadcast_to(end_g, (SG, SG)), 0.0), axis=1, keepdims=True)
        carried = _dot_tn(jnp.concatenate([bd(b_t[:, cols] * end_g), bd(k_t[:, cols] * end_g)], axis=0),
                          jnp.concatenate([u, v_bd], axis=0))
        yield
        ys.append(sum(y_bd[h * L:(h + 1) * L, :] for h in range(RWKV_GROUP)))
        state_ref[g] = to_end_col * t0 + carried
    y = jnp.concatenate(ys, axis=1)

    inv_d = 1.0 / HEAD_DIM
    mean = _dot_f32_by_exact(y, ones_bd) * inv_d
    yield
    yc = y - mean
    var = _dot_f32_by_exact(yc * yc, ones_bd) * inv_d
    yield
    yn = yc * lax.rsqrt(var + RWKV_GN_EPS) * ln_g + ln_b
    o_ref[...] = (yn + seg[L:] * v) * gate


def _rwkv7_time_mix(p, mu, w0, w2, a0, a2, g2, k_k, k_a, r_k, ln_g, ln_b):
    B, S, _ = p.shape
    GW = GROUP_W
    assert S % RWKV_CHUNK == 0 and RWKV_CHUNK == HEAD_DIM and B % RWKV_BATCH == 0
    low_w = jnp.zeros((RWKV_LOW, 3 * GW), F32)
    low_w = low_w.at[:RWKV_W_RANK, :GW].set(w2)
    low_w = low_w.at[RWKV_W_RANK:RWKV_W_RANK + RWKV_A_RANK, GW:2 * GW].set(a2)
    low_w = low_w.at[RWKV_W_RANK + RWKV_A_RANK:, 2 * GW:].set(g2)
    l_hi, l_lo = _split_bf16(low_w)
    vecs = jnp.stack([w0, a0, k_k, k_a, r_k, ln_g, ln_b, jnp.zeros_like(w0)], 0)
    full = lambda a: pl.BlockSpec(a.shape, lambda b, c: (0,) * a.ndim)
    mu2 = mu[None]
    return pl.pallas_call(
        _rwkv_kernel,
        grid=(B // RWKV_BATCH, S // RWKV_CHUNK),
        in_specs=[pl.BlockSpec((RWKV_BATCH, RWKV_CHUNK, A_PAD), lambda b, c: (b, c, 0)),
                  full(mu2), full(vecs), full(l_hi), full(l_lo)],
        out_specs=pl.BlockSpec((RWKV_BATCH, RWKV_CHUNK, GW), lambda b, c: (b, c, 0)),
        out_shape=jax.ShapeDtypeStruct((B, S, GW), F32),
        scratch_shapes=[pltpu.VMEM((RWKV_BATCH, N_HEADS // RWKV_GROUP, RWKV_GROUP * RWKV_CHUNK,
                                    RWKV_GROUP * HEAD_DIM), F32),
                        pltpu.VMEM((RWKV_BATCH, 1, A_PAD), F32)],
        compiler_params=pltpu.CompilerParams(dimension_semantics=("parallel", "arbitrary"),
                                             vmem_limit_bytes=VMEM_LIMIT),
        name="rwkv7",
    )(p, mu2, vecs, l_hi, l_lo)


SB_UNROLL = 4


def _sb_kernel(q_ref, k_ref, v_ref, g_ref, o_ref, kbd_ref, vbd_ref):
    i = pl.program_id(1)
    T, GW, H = SB_BLOCK, GROUP_W, N_HEADS
    lane_h = lax.broadcasted_iota(jnp.int32, (T, GW), 1) // HEAD_DIM
    k_new, v_new = k_ref[0], v_ref[0]
    for h in range(H):
        kbd_ref[i, h * T:(h + 1) * T, :] = jnp.where(lane_h == h, k_new, 0.0).astype(BF16)
        vbd_ref[i, h * T:(h + 1) * T, :] = jnp.where(lane_h == h, v_new, 0.0).astype(BF16)

    q = q_ref[0].astype(BF16)
    si = lax.broadcasted_iota(jnp.int32, (T, 2 * T), 0)
    sj = lax.broadcasted_iota(jnp.int32, (T, 2 * T), 1)
    later_and_all = jnp.where((si > sj) | (sj >= T), 1.0, 0.0).astype(BF16)
    qrow = lax.broadcasted_iota(jnp.int32, (T, H * T), 0)
    kcol = lax.broadcasted_iota(jnp.int32, (T, H * T), 1) % T
    causal = kcol < qrow

    def key_block_stages(j, box, diagonal):
        z = _dot_nt(q, kbd_ref[j]) * HEAD_DIM ** -0.5
        yield
        soft = jnp.log(1.0 + jnp.exp(-jnp.abs(z)))
        log1m = -(jnp.maximum(z, 0.0) + soft)
        log_sig = jnp.minimum(z, 0.0) - soft
        log1m_in = (jnp.where(causal, log1m, 0.0) if diagonal else log1m).astype(BF16)
        sums = _dot(jnp.concatenate([log1m_in[:, h * T:(h + 1) * T] for h in range(H)], axis=0), later_and_all)
        yield
        suffix = jnp.concatenate([sums[h * T:(h + 1) * T, :T] for h in range(H)], axis=1)
        total = jnp.concatenate([sums[h * T:(h + 1) * T, T:] for h in range(H)], axis=1)
        att = jnp.exp(log_sig + suffix + box["carry"])
        if diagonal:
            att = jnp.where(causal, att, 0.0)
        box["carry"] = box["carry"] + total
        pv = _dot(att.astype(BF16), vbd_ref[j])
        yield
        box["acc"] = box["acc"] + pv

    def key_blocks(js, state, diagonal=False):
        box = {"carry": state[0], "acc": state[1]}
        for _ in itertools.zip_longest(*[key_block_stages(j, box, diagonal) for j in js]):
            pass
        return box["carry"], box["acc"]

    state = key_blocks([i], (jnp.zeros((T, H * T), F32), jnp.zeros((T, GW), F32)), True)
    rem = i % SB_UNROLL
    state = lax.fori_loop(0, rem, lambda it, st: key_blocks([i - 1 - it], st), state)
    top = i - 1 - rem
    _, y = lax.fori_loop(0, i // SB_UNROLL,
                         lambda it, st: key_blocks([top - SB_UNROLL * it - n for n in range(SB_UNROLL)], st), state)

    hi_ = lax.broadcasted_iota(jnp.int32, (GW, GW), 0) // HEAD_DIM
    hj_ = lax.broadcasted_iota(jnp.int32, (GW, GW), 1) // HEAD_DIM
    ones_bd = jnp.where(hi_ == hj_, 1.0, 0.0).astype(BF16)
    ms = _dot_f32_by_exact(y * y, ones_bd) * (1.0 / HEAD_DIM)
    o_ref[0] = y * lax.rsqrt(ms + NORM_EPS) * g_ref[...]


def _stick_breaking_norm(p, g):
    B, S, _ = p.shape
    GW = GROUP_W
    assert S % SB_BLOCK == 0
    col = lambda n: pl.BlockSpec((1, SB_BLOCK, GW), lambda b, i: (b, i, n))
    blk = col(0)
    q = k = v = p
    stacked = pltpu.VMEM((S // SB_BLOCK, N_HEADS * SB_BLOCK, GW), BF16)
    return pl.pallas_call(
        _sb_kernel,
        grid=(B, S // SB_BLOCK),
        in_specs=[col(0), col(1), col(2), pl.BlockSpec((1, GW), lambda b, i: (0, 0))],
        out_specs=blk,
        out_shape=jax.ShapeDtypeStruct((B, S, GW), F32),
        scratch_shapes=[stacked, stacked],
        compiler_params=pltpu.CompilerParams(dimension_semantics=("parallel", "arbitrary"),
                                             vmem_limit_bytes=VMEM_LIMIT),
        name="stick_breaking",
    )(q, k, v, g[None])


ML_HALO = 8
ML_GROUP = N_HEADS
ML_BATCH = 4


def _dot_nt_exact_by_f32(a_exact, b):
    return _dot_nt(jnp.concatenate([a_exact] * 3, axis=1), jnp.concatenate(_split3_bf16(b), axis=1))


def _mlstm_kernel(p_ref, cw_ref, cb_ref, gb_ref, g_ref, o_ref, ext_ref, ct_ref, n_ref, m_ref):
    @pl.when(pl.program_id(1) == 0)
    def _():
        ext_ref[...] = jnp.zeros_like(ext_ref)
        ct_ref[...] = jnp.zeros_like(ct_ref)
        n_ref[...] = jnp.zeros_like(n_ref)
        m_ref[...] = jnp.zeros_like(m_ref)

    chains = [_mlstm_chunk(p_ref.at[n], cw_ref, cb_ref, gb_ref, g_ref, o_ref.at[n], ext_ref.at[n], ct_ref.at[n],
                           n_ref.at[n], m_ref.at[n]) for n in range(ML_BATCH)]
    for _ in itertools.zip_longest(*chains):
        pass


def _mlstm_chunk(p_ref, cw_ref, cb_ref, gb_ref, g_ref, o_ref, ext_ref, ct_ref, n_ref, m_ref):
    L, GW, H = ML_CHUNK, GROUP_W, N_HEADS
    x = p_ref[...]
    ext_ref[ML_HALO:, :] = x[:, :2 * GW]
    conv = cb_ref[...]
    for j in range(ML_CONV):
        conv = conv + cw_ref[j:j + 1, :] * ext_ref[pl.ds(ML_HALO - (ML_CONV - 1) + j, L), :]
    ext_ref[:ML_HALO, :] = x[L - ML_HALO:, :2 * GW]
    qk = conv * jax.nn.sigmoid(conv)
    q, k = qk[:, :GW], qk[:, GW:] * HEAD_DIM ** -0.5
    v, o = x[:, 2 * GW:3 * GW], x[:, 3 * GW:4 * GW]

    gates = x[:, 4 * GW:]
    gi = lax.broadcasted_iota(jnp.int32, (LANE, 2 * GW), 0)
    gj = lax.broadcasted_iota(jnp.int32, (LANE, 2 * GW), 1)
    expand = jnp.where(gi == (gj % GW) // HEAD_DIM + H * (gj // GW), 1.0, 0.0).astype(BF16)
    graw = _dot_f32_by_exact(gates, expand) + gb_ref[...]
    yield
    capped = GATE_CAP * jnp.tanh(graw * (1.0 / GATE_CAP))
    log_i = capped[:, :GW]
    cf = capped[:, GW:]
    log_f = jnp.minimum(cf, 0.0) - jnp.log(1.0 + jnp.exp(-jnp.abs(cf)))

    ti = lax.broadcasted_iota(jnp.int32, (L, L), 0)
    tj = lax.broadcasted_iota(jnp.int32, (L, L), 1)
    bf = _dot_exact_by_f32(jnp.where(tj <= ti, 1.0, 0.0).astype(BF16), log_f)
    yield
    b_last = bf[L - 1:L, :]
    m_row, n_row = m_ref[...], n_ref[...]
    dec = b_last - bf + log_i
    m_new = jnp.maximum(b_last + m_row, jnp.max(dec, axis=0, keepdims=True))
    kw = k * jnp.exp(dec - m_new)
    s_old = jnp.exp(b_last + m_row - m_new)
    n_ref[...] = s_old * n_row + jnp.sum(kw, axis=0, keepdims=True)
    m_ref[...] = m_new
    g_in = bf + m_row
    li_b = log_i - bf

    SG = ML_GROUP * L
    ri = lax.broadcasted_iota(jnp.int32, (SG, SG), 0)
    ci = lax.broadcasted_iota(jnp.int32, (SG, SG), 1)
    same_head = (ri // L) == (ci // HEAD_DIM)
    first_lane = ci == (ri // L) * HEAD_DIM
    sel_first = jnp.where(first_lane, 1.0, 0.0).astype(BF16)
    stack = lambda t: jnp.concatenate([t] * ML_GROUP, axis=0)
    pick = lambda t: jnp.sum(jnp.where(first_lane, t, 0.0), axis=1, keepdims=True)
    rt = lax.broadcasted_iota(jnp.int32, (SG, L), 0) % L
    rs = lax.broadcasted_iota(jnp.int32, (SG, L), 1)
    hs = []
    for g in range(H // ML_GROUP):
        cols = slice(g * SG, (g + 1) * SG)
        ct = ct_ref[g]
        qs = jnp.where(same_head, stack(q[:, cols]), 0.0)
        qs_b = qs.astype(BF16)
        v_b = v[:, cols].astype(BF16)
        b_col = pick(stack(bf[:, cols]))
        g_col = pick(stack(g_in[:, cols]))
        row_part = _dot_nt_exact_by_f32(sel_first, li_b[:, cols])
        qk = _dot_nt(qs_b, k[:, cols].astype(BF16))
        inter = _dot(qs_b, ct.astype(BF16))
        carried = _dot_tn(kw[:, cols].astype(BF16), v_b)
        yield
        dmat = jnp.where(rs <= rt, b_col + row_part, -jnp.inf)
        m_t = jnp.maximum(g_col, jnp.max(dmat, axis=1, keepdims=True))
        s_inter = jnp.exp(g_col - m_t)
        sqk = qk * jnp.exp(dmat - m_t)
        intra = _dot(sqk.astype(BF16), v_b)
        yield
        num = s_inter * inter + jnp.where(same_head, intra, 0.0)
        den = (s_inter * jnp.sum(qs * n_row[:, cols], axis=1, keepdims=True)
               + jnp.sum(sqk, axis=1, keepdims=True))
        hst = num / jnp.maximum(jnp.abs(den), jnp.exp(-m_t))
        hs.append(sum(hst[n * L:(n + 1) * L, :] for n in range(ML_GROUP)))
        ct_ref[g] = s_old[:, cols] * ct + jnp.where(same_head, carried, 0.0)
    h = jnp.concatenate(hs, axis=1)

    hi_ = lax.broadcasted_iota(jnp.int32, (GW, GW), 0) // HEAD_DIM
    hj_ = lax.broadcasted_iota(jnp.int32, (GW, GW), 1) // HEAD_DIM
    ones_bd = jnp.where(hi_ == hj_, 1.0, 0.0).astype(BF16)
    ms = _dot_f32_by_exact(h * h, ones_bd) * (1.0 / HEAD_DIM)
    yield
    o_ref[...] = jax.nn.sigmoid(o) * (h * lax.rsqrt(ms + NORM_EPS) * g_ref[...])


def _mlstm_mix(p, conv_w, conv_b, ig_b, fg_b, norm_g):
    B, S, _ = p.shape
    GW = GROUP_W
    assert S % ML_CHUNK == 0 and ML_CONV - 1 <= ML_HALO <= ML_CHUNK and B % ML_BATCH == 0
    gate_b = jnp.concatenate([jnp.repeat(ig_b, HEAD_DIM), jnp.repeat(fg_b, HEAD_DIM)])[None]
    full = lambda a: pl.BlockSpec(a.shape, lambda b, c: (0,) * a.ndim)
    cb2, g2 = conv_b[None], norm_g[None]
    return pl.pallas_call(
        _mlstm_kernel,
        grid=(B // ML_BATCH, S // ML_CHUNK),
        in_specs=[pl.BlockSpec((ML_BATCH, ML_CHUNK, C_PAD), lambda b, c: (b, c, 0)),
                  full(conv_w), full(cb2), full(gate_b), full(g2)],
        out_specs=pl.BlockSpec((ML_BATCH, ML_CHUNK, GW), lambda b, c: (b, c, 0)),
        out_shape=jax.ShapeDtypeStruct((B, S, GW), F32),
        scratch_shapes=[pltpu.VMEM((ML_BATCH, ML_HALO + ML_CHUNK, 2 * GW), F32),
                        pltpu.VMEM((ML_BATCH, N_HEADS // ML_GROUP, ML_GROUP * ML_CHUNK, ML_GROUP * HEAD_DIM), F32),
                        pltpu.VMEM((ML_BATCH, 1, GW), F32), pltpu.VMEM((ML_BATCH, 1, GW), F32)],
        compiler_params=pltpu.CompilerParams(dimension_semantics=("parallel", "arbitrary"),
                                             vmem_limit_bytes=VMEM_LIMIT),
        name="mlstm",
    )(p, conv_w, cb2, gate_b, g2)


def _rms_norm(x, g):
    xf = x.astype(F32)
    y = xf * lax.rsqrt(jnp.mean(xf * xf, -1, keepdims=True) + NORM_EPS)
    return (y * g.astype(F32)).astype(x.dtype)


def _rope(x, pos):
    half = x.shape[-1] // 2
    inv = ROPE_THETA ** (-jnp.arange(half, dtype=F32) / half)
    ang = pos.astype(F32)[:, None] * inv[None, :]
    cos = jnp.cos(ang)[None, :, None, :]
    sin = jnp.sin(ang)[None, :, None, :]
    xf = x.astype(F32)
    x1, x2 = xf[..., :half], xf[..., half:]
    return jnp.concatenate([x1 * cos - x2 * sin, x2 * cos + x1 * sin], -1).astype(x.dtype)


def kernel(x, c, ada_w, ada_b, norm1_g, norm2_g, w_in, rk_mu, rk_w0, rk_w2, rk_a0, rk_a2, rk_g2, rk_kk, rk_ka, rk_rk, rk_ln_g, rk_ln_b, sb_norm_g, ml_conv_w, ml_conv_b, ml_ig_b, ml_fg_b, ml_norm_g, ds_qn_g, ds_kn_g, ds_out_g, w_out, moe_wg, moe_bg, moe_we, moe_be, moe_w1, moe_w3, moe_w2):
    B, S, D = x.shape
    H, d = N_HEADS, HEAD_DIM
    depth = ada_w.shape[0]
    pos = jnp.arange(S)
    c_act = jax.nn.silu(c)
    for l in range(depth):
        mod = (c_act @ ada_w[l] + ada_b[l])[:, None, :]
        sh1, sc1, gt1, sh2, sc2, gt2 = jnp.split(mod, 6, axis=-1)

        pA, pB, pC, pD = _in_proj(x, sc1, sh1, norm1_g[l][None], _pad_w_in(w_in[l]))

        yA = _rwkv7_time_mix(pA, rk_mu[l], rk_w0[l], rk_w2[l], rk_a0[l], rk_a2[l], rk_g2[l],
                             rk_kk[l], rk_ka[l], rk_rk[l], rk_ln_g[l], rk_ln_b[l])

        yB = _stick_breaking_norm(pB, sb_norm_g[l])

        yC = _mlstm_mix(pC, ml_conv_w[l], ml_conv_b[l], ml_ig_b[l], ml_fg_b[l], ml_norm_g[l])

        yD = _dsa_attn_norm(*_dsa_prep(pD, ds_qn_g[l], ds_kn_g[l]), ds_out_g[l])

        router = jnp.pad(jnp.concatenate([moe_wg[l], moe_we[l]], 1),
                         ((0, 0), (0, ROUTER_PAD - N_GROUPS - N_EXPERTS)))
        r_hi, r_lo = _split_bf16(router)
        r_b = jnp.pad(jnp.concatenate([moe_bg[l], moe_be[l]]), (0, ROUTER_PAD - N_GROUPS - N_EXPERTS))[None]
        x1, h2, route = _out_proj((yA, yB, yC, yD), x, gt1, sc2, sh2, norm2_g[l][None],
                                  w_out[l].astype(BF16), r_hi, r_lo, r_b)

        moe = _hier_moe(h2.reshape(B * S, D), route.reshape(B * S, ROUTER_PAD),
                        moe_w1, moe_w3, moe_w2, l)
        x = x1 + gt2 * moe.reshape(B, S, D)
    return x
```

```python
import functools
import itertools

import jax
import jax.numpy as jnp
import numpy as np
from jax import lax
from jax.experimental import pallas as pl
from jax.experimental.pallas import tpu as pltpu

F32 = jnp.float32
BF16 = jnp.bfloat16

D_MODEL = 1024
N_MIXERS = 4
GROUP_W = D_MODEL // N_MIXERS
HEAD_DIM = 64
N_HEADS = GROUP_W // HEAD_DIM
NORM_EPS = 1e-6
RWKV_W_RANK = 32
RWKV_A_RANK = 32
RWKV_G_RANK = 64
RWKV_GN_EPS = 64e-5
SB_BLOCK = 128
ML_CHUNK = 64
ML_CONV = 4
GATE_CAP = 15.0
DSA_BLOCK = 128
IDX_HEADS = 4
IDX_DIM = 32
TOPK_MAX = 256
ROPE_THETA = 10000.0
N_GROUPS = 4
EXP_PER_GROUP = 8
N_EXPERTS = N_GROUPS * EXP_PER_GROUP
EXPERT_FF = D_MODEL // 2
TOP_IN_GROUP = 2

A_SIZES = (GROUP_W, GROUP_W, GROUP_W, RWKV_W_RANK, RWKV_A_RANK, RWKV_G_RANK)
B_SIZES = (GROUP_W, GROUP_W, GROUP_W)
C_SIZES = (GROUP_W, GROUP_W, GROUP_W, GROUP_W, N_HEADS, N_HEADS)
D_SIZES = (GROUP_W, HEAD_DIM, HEAD_DIM, IDX_HEADS * IDX_DIM, IDX_DIM, IDX_HEADS)
A_COLS = sum(A_SIZES)
B_COLS = sum(B_SIZES)
C_COLS = sum(C_SIZES)
D_COLS = sum(D_SIZES)

LANE = 128
A_PAD = 896
B_PAD = 768
C_PAD = 1152
D_PAD = 640
P_PAD = A_PAD + B_PAD + C_PAD + D_PAD
ROUTER_PAD = LANE

IN_ROWS = 256
OUT_ROWS = 512
MOE_ROWS = 512
VMEM_LIMIT = 48 * 1024 * 1024


def _split_cols(t, sizes):
    return jnp.split(t, [int(i) for i in np.cumsum(sizes)[:-1]], axis=-1)


def _in_proj_kernel(x_ref, sc_ref, sh_ref, g_ref, w_ref, oa_ref, ob_ref, oc_ref, od_ref):
    x = x_ref[0]
    y = x * lax.rsqrt(jnp.mean(x * x, -1, keepdims=True) + NORM_EPS) * g_ref[...]
    h = y * (1.0 + sc_ref[0]) + sh_ref[0]
    p = jnp.dot(h.astype(BF16), w_ref[...], preferred_element_type=F32)
    oa_ref[0] = p[:, :A_PAD]
    ob_ref[0] = p[:, A_PAD:A_PAD + B_PAD]
    oc_ref[0] = p[:, A_PAD + B_PAD:A_PAD + B_PAD + C_PAD]
    od_ref[0] = p[:, A_PAD + B_PAD + C_PAD:]


def _in_proj(x, sc, sh, g, w_pad):
    B, S, D = x.shape
    row = lambda w: pl.BlockSpec((1, IN_ROWS, w), lambda b, i: (b, i, 0))
    vec = pl.BlockSpec((1, 1, D), lambda b, i: (b, 0, 0))
    return pl.pallas_call(
        _in_proj_kernel,
        grid=(B, S // IN_ROWS),
        in_specs=[row(D), vec, vec, pl.BlockSpec((1, D), lambda b, i: (0, 0)),
                  pl.BlockSpec((D, P_PAD), lambda b, i: (0, 0))],
        out_specs=[row(A_PAD), row(B_PAD), row(C_PAD), row(D_PAD)],
        out_shape=[jax.ShapeDtypeStruct((B, S, w), F32) for w in (A_PAD, B_PAD, C_PAD, D_PAD)],
        compiler_params=pltpu.CompilerParams(dimension_semantics=("parallel", "parallel"),
                                             vmem_limit_bytes=VMEM_LIMIT),
        name="in_proj",
    )(x, sc, sh, g, w_pad)


def _pad_w_in(w):
    wa, wb, wc, wd = _split_cols(w, (A_COLS, B_COLS, C_COLS, D_COLS))
    padc = lambda t, n: jnp.pad(t, ((0, 0), (0, n - t.shape[1])))
    return jnp.concatenate([padc(wa, A_PAD), padc(wb, B_PAD), padc(wc, C_PAD), padc(wd, D_PAD)], 1).astype(BF16)


def _split_bf16(t):
    hi = t.astype(BF16)
    lo = (t - hi.astype(F32)).astype(BF16)
    return hi, lo


def _out_proj_kernel(ya_ref, yb_ref, yc_ref, yd_ref, x_ref, gt_ref, sc_ref, sh_ref, g_ref, w_ref,
                     rhi_ref, rlo_ref, rb_ref, x1_ref, h2_ref, route_ref):
    acc = jnp.zeros(x_ref.shape[1:], F32)
    for n, y_ref in enumerate((ya_ref, yb_ref, yc_ref, yd_ref)):
        acc += jnp.dot(y_ref[0].astype(BF16), w_ref[n * GROUP_W:(n + 1) * GROUP_W, :],
                       preferred_element_type=F32)
    x1 = x_ref[0] + gt_ref[0] * acc
    x1_ref[0] = x1
    y = x1 * lax.rsqrt(jnp.mean(x1 * x1, -1, keepdims=True) + NORM_EPS) * g_ref[...]
    h = y * (1.0 + sc_ref[0]) + sh_ref[0]
    hi, lo = _split_bf16(h)
    h2_ref[0] = hi
    lg = (jnp.dot(hi, rhi_ref[...], preferred_element_type=F32)
          + jnp.dot(lo, rhi_ref[...], preferred_element_type=F32)
          + jnp.dot(hi, rlo_ref[...], preferred_element_type=F32)) + rb_ref[...]
    route_ref[0] = _route(lg)


def _route(lg):
    lane = lax.broadcasted_iota(jnp.int32, lg.shape, 1)
    neg = -jnp.inf
    first = lambda hit: jnp.min(jnp.where(hit, lane, ROUTER_PAD), axis=1, keepdims=True)
    is_grp = lane < N_GROUPS
    grp = jnp.where(is_grp, lg, neg)
    g_max = jnp.max(grp, axis=1, keepdims=True)
    g_p = 1.0 / jnp.sum(jnp.where(is_grp, jnp.exp(grp - g_max), 0.0), axis=1, keepdims=True)
    g_idx = first(grp == g_max)
    e_lane = lane - N_GROUPS
    in_group = (e_lane >= 0) & (e_lane < N_EXPERTS) & (e_lane // EXP_PER_GROUP == g_idx)
    e_log = jnp.where(in_group, lg, neg)
    e1_max = jnp.max(e_log, axis=1, keepdims=True)
    e1_lane = first(e_log == e1_max)
    e_log2 = jnp.where(lane == e1_lane, neg, e_log)
    e2_max = jnp.max(e_log2, axis=1, keepdims=True)
    e2_lane = first(e_log2 == e2_max)
    ratio = jnp.exp(e2_max - e1_max)
    gate1 = g_p / (1.0 + ratio)
    gate2 = gate1 * ratio
    out = jnp.where(lane == 0, (e1_lane - N_GROUPS).astype(F32), 0.0)
    out = jnp.where(lane == 1, (e2_lane - N_GROUPS).astype(F32), out)
    out = jnp.where(lane == 2, gate1, out)
    return jnp.where(lane == 3, gate2, out)


def _out_proj(ys, x, gt, sc, sh, g, w_out, r_hi, r_lo, r_b):
    B, S, D = x.shape
    row = lambda w: pl.BlockSpec((1, OUT_ROWS, w), lambda b, i: (b, i, 0))
    vec = pl.BlockSpec((1, 1, D), lambda b, i: (b, 0, 0))
    full = lambda a: pl.BlockSpec(a.shape, lambda b, i: (0,) * a.ndim)
    return pl.pallas_call(
        _out_proj_kernel,
        grid=(B, S // OUT_ROWS),
        in_specs=[row(GROUP_W)] * 4 + [row(D), vec, vec, vec, full(g), full(w_out), full(r_hi), full(r_lo),
                                        full(r_b)],
        out_specs=[row(D), row(D), row(ROUTER_PAD)],
        out_shape=[jax.ShapeDtypeStruct((B, S, D), F32), jax.ShapeDtypeStruct((B, S, D), BF16),
                   jax.ShapeDtypeStruct((B, S, ROUTER_PAD), F32)],
        compiler_params=pltpu.CompilerParams(dimension_semantics=("parallel", "parallel"),
                                             vmem_limit_bytes=VMEM_LIMIT),
        name="out_proj",
    )(*ys, x, gt, sc, sh, g, w_out, r_hi, r_lo, r_b)


def _moe_ffn_kernel(blk_e_ref, x_ref, wt_ref, w1_ref, w3_ref, w2_ref, o_ref, w1b_ref, w3b_ref, w2b_ref):
    i = pl.program_id(0)
    changed = jnp.logical_or(i == 0, blk_e_ref[i] != blk_e_ref[jnp.maximum(i - 1, 0)])

    @pl.when(changed)
    def _():
        w1b_ref[...] = w1_ref[0, 0].astype(BF16)
        w3b_ref[...] = w3_ref[0, 0].astype(BF16)
        w2b_ref[...] = w2_ref[0, 0].astype(BF16)

    xb = x_ref[...]
    a = jnp.dot(xb, w1b_ref[...], preferred_element_type=F32)
    b = jnp.dot(xb, w3b_ref[...], preferred_element_type=F32)
    hmid = (a * jax.nn.sigmoid(a) * b).astype(BF16)
    y = jnp.dot(hmid, w2b_ref[...], preferred_element_type=F32)
    o_ref[...] = (y * wt_ref[...]).astype(o_ref.dtype)


def _moe_ffn(blk_e, xs, wt, w1, w3, w2, layer):
    n_slots, D = xs.shape
    n_blocks = n_slots // MOE_ROWS
    FF = w1.shape[-1]
    return pl.pallas_call(
        _moe_ffn_kernel,
        grid_spec=pltpu.PrefetchScalarGridSpec(
            num_scalar_prefetch=1,
            grid=(n_blocks,),
            in_specs=[pl.BlockSpec((MOE_ROWS, D), lambda i, e: (i, 0)),
                      pl.BlockSpec((MOE_ROWS, 1), lambda i, e: (i, 0)),
                      pl.BlockSpec((1, 1, D, FF), lambda i, e: (layer, e[i], 0, 0)),
                      pl.BlockSpec((1, 1, D, FF), lambda i, e: (layer, e[i], 0, 0)),
                      pl.BlockSpec((1, 1, FF, D), lambda i, e: (layer, e[i], 0, 0))],
            out_specs=pl.BlockSpec((MOE_ROWS, D), lambda i, e: (i, 0)),
            scratch_shapes=[pltpu.VMEM((D, FF), BF16), pltpu.VMEM((D, FF), BF16), pltpu.VMEM((FF, D), BF16)],
        ),
        out_shape=jax.ShapeDtypeStruct((n_slots, D), BF16),
        compiler_params=pltpu.CompilerParams(dimension_semantics=("arbitrary",),
                                             vmem_limit_bytes=VMEM_LIMIT),
        name="moe_ffn",
    )(blk_e, xs, wt, w1, w3, w2)


def _hier_moe(h2, route, w1, w3, w2, layer):
    N, D = h2.shape
    expert = route[:, :TOP_IN_GROUP].astype(jnp.int32)
    gate = route[:, TOP_IN_GROUP:2 * TOP_IN_GROUP]
    n_asg = N * TOP_IN_GROUP
    n_blocks = n_asg // MOE_ROWS + N_EXPERTS
    n_slots = n_blocks * MOE_ROWS
    n_fill = n_slots - n_asg
    flat_e = expert.reshape(n_asg // LANE, LANE)
    counts = jnp.sum(flat_e[None] == jnp.arange(N_EXPERTS)[:, None, None], axis=(1, 2)).astype(jnp.int32)
    pad_counts = (counts + MOE_ROWS - 1) // MOE_ROWS * MOE_ROWS
    pad_end = jnp.cumsum(pad_counts)
    blk_start = jnp.arange(n_blocks) * MOE_ROWS
    blk_e = jnp.minimum(jnp.sum(pad_end[None, :] <= blk_start[:, None], 1), N_EXPERTS - 1).astype(jnp.int32)
    fill_end = jnp.cumsum(pad_counts - counts)
    fill_id = jnp.arange(n_fill).reshape(n_fill // LANE, LANE)
    fill_e = jnp.zeros_like(fill_id)
    for e in range(N_EXPERTS):
        fill_e = fill_e + (fill_id >= fill_end[e])
    keys = jnp.concatenate([flat_e.reshape(n_asg) * 2, fill_e.reshape(n_fill) * 2 + 1])
    toks = jnp.concatenate([jnp.arange(n_asg, dtype=jnp.int32) // TOP_IN_GROUP, jnp.zeros((n_fill,), jnp.int32)])
    wts = jnp.concatenate([gate.reshape(n_asg), jnp.zeros((n_fill,), F32)])
    _, slot_tok, slot_w, slot_src = lax.sort((keys, toks, wts, jnp.arange(n_slots, dtype=jnp.int32)), num_keys=1)
    _, entry_slot = lax.sort((slot_src, jnp.arange(n_slots, dtype=jnp.int32)), num_keys=1)
    asg_slot = entry_slot[:n_asg].reshape(N, TOP_IN_GROUP)
    yb = _moe_ffn(blk_e, h2[slot_tok], slot_w[:, None], w1, w3, w2, layer)
    return yb[asg_slot[:, 0]].astype(F32) + yb[asg_slot[:, 1]].astype(F32)


INT_MIN = -2 ** 31
DSA_KEY_STEP = 512


def _float_order_key(x):
    bits = pltpu.bitcast(x, jnp.int32)
    bits = jnp.where(x == 0.0, 0, bits)
    return bits ^ ((bits >> 31) & 0x7FFFFFFF)


COL_PART = 64


def _col_reduce(x, reduce):
    part = reduce(x.reshape(x.shape[0] // COL_PART, COL_PART, x.shape[1]), axis=0)
    return reduce(part, axis=0, keepdims=True)


def _col_count(mask):
    return _col_reduce(jnp.where(mask, 1.0, 0.0), jnp.sum)


def _head_block_diag(t, group):
    n_heads = t.shape[0] // group
    row_h = lax.broadcasted_iota(jnp.int32, t.shape, 0) // group
    return jnp.concatenate([jnp.where(row_h == h, t, 0.0) for h in range(n_heads)], axis=1)


def _dsa_block_t(qd_ref, kd4_ref, vdwt_ref, qi_ref, kihi_ref, kilo_ref, wi_ref, g_ref, o_ref, *, kl, n_sel):
    Q = DSA_BLOCK
    q0 = pl.program_id(1) * Q
    w_hi, w_lo = _split_bf16(_head_block_diag(jnp.transpose(qi_ref[0]), IDX_DIM))
    k_hi, k_lo = kihi_ref[0, :kl, :], kilo_ref[0, :kl, :]
    sc = _dot(jnp.concatenate([k_hi, k_lo, k_hi], axis=1),
              jnp.concatenate([w_hi, w_hi, w_lo], axis=0))
    wit = jnp.transpose(wi_ref[0])
    score = sum(wit[h:h + 1, :] * jnp.maximum(sc[:, h * Q:(h + 1) * Q], 0.0) for h in range(IDX_HEADS))
    kidx = lax.broadcasted_iota(jnp.int32, (kl, Q), 0)
    qpos = q0 + lax.broadcasted_iota(jnp.int32, (kl, Q), 1)
    adm = kidx <= qpos
    key = _float_order_key(jnp.where(adm, score, -jnp.inf))

    def value_bit(it, tau):
        cand = tau | jnp.left_shift(jnp.int32(1), 31 - it)
        return jnp.where(_col_count(key >= (cand ^ INT_MIN)) >= n_sel, cand, tau)

    tau = lax.fori_loop(0, 32, value_bit, jnp.zeros((1, Q), jnp.int32)) ^ INT_MIN
    gt = key > tau
    eq = (key == tau) & adm
    need = n_sel - _col_count(gt)
    n_eq = _col_count(eq)

    def index_bits():
        def index_bit(it, bound):
            cand = bound | jnp.left_shift(jnp.int32(1), 11 - it)
            return jnp.where(_col_count(eq & (kidx < cand)) <= need, cand, bound)
        return lax.fori_loop(0, 12, index_bit, jnp.zeros((1, Q), jnp.int32))

    bound = lax.cond(jnp.max(n_eq - need) > 0.0, index_bits, lambda: jnp.full((1, Q), kl, jnp.int32))
    sel = gt | (eq & (kidx < bound))

    w_att = _head_block_diag(jnp.transpose(qd_ref[0]), HEAD_DIM).astype(BF16)
    lg = _dot(kd4_ref[0, :kl, :], w_att) * HEAD_DIM ** -0.5
    lg = jnp.where(jnp.concatenate([sel] * N_HEADS, axis=1), lg, -jnp.inf)
    p = jnp.exp(lg - _col_reduce(lg, jnp.max))
    out_t = _dot(vdwt_ref[0, :, :kl], p.astype(BF16)) / _col_reduce(p, jnp.sum)
    row_h = lax.broadcasted_iota(jnp.int32, (GROUP_W, Q), 0) // HEAD_DIM
    nat_t = sum(jnp.where(row_h == h, out_t[:, h * Q:(h + 1) * Q], 0.0) for h in range(N_HEADS))
    r = jnp.transpose(nat_t)
    hi_ = lax.broadcasted_iota(jnp.int32, (GROUP_W, GROUP_W), 0) // HEAD_DIM
    hj_ = lax.broadcasted_iota(jnp.int32, (GROUP_W, GROUP_W), 1) // HEAD_DIM
    ms = _dot_f32_by_exact(r * r, jnp.where(hi_ == hj_, 1.0, 0.0).astype(BF16)) * (1.0 / HEAD_DIM)
    o_ref[0] = r * lax.rsqrt(ms + NORM_EPS) * g_ref[...]


def _dsa_kernel(qd_ref, kd4_ref, vdwt_ref, qi_ref, kihi_ref, kilo_ref, wi_ref, g_ref, o_ref, *, kls, n_sel):
    blocks_per_step = DSA_KEY_STEP // DSA_BLOCK
    for j, kl in enumerate(kls):
        @pl.when(pl.program_id(1) // blocks_per_step == j)
        def _():
            _dsa_block_t(qd_ref, kd4_ref, vdwt_ref, qi_ref, kihi_ref, kilo_ref, wi_ref, g_ref, o_ref,
                         kl=kl, n_sel=n_sel)


DSA_PREP_ROWS = 256
D_Q, D_KV, D_QI, D_KW = 0, GROUP_W, GROUP_W + LANE, GROUP_W + 2 * LANE


def _swap_halves(x, half):
    n = x.shape[1]
    lane = lax.broadcasted_iota(jnp.int32, x.shape, 1)
    return jnp.where(lane % (2 * half) < half, pltpu.roll(x, n - half, axis=1), pltpu.roll(x, half, axis=1))


def _dsa_prep_kernel(p_ref, cq_ref, sq_ref, ci_ref, si_ref, gq_ref, gk_ref,
                     qd_ref, kd4_ref, vdwt_ref, qi_ref, kihi_ref, kilo_ref, wi_ref):
    GW = GROUP_W
    x = p_ref[0]
    hi_ = lax.broadcasted_iota(jnp.int32, (GW, GW), 0) // HEAD_DIM
    hj_ = lax.broadcasted_iota(jnp.int32, (GW, GW), 1) // HEAD_DIM
    ones_bd = jnp.where(hi_ == hj_, 1.0, 0.0).astype(BF16)
    q = x[:, D_Q:D_Q + GW]
    q = q * lax.rsqrt(_dot_f32_by_exact(q * q, ones_bd) * (1.0 / HEAD_DIM) + NORM_EPS) * gq_ref[...]
    qd_ref[0] = q * cq_ref[...] + _swap_halves(q, HEAD_DIM // 2) * sq_ref[...]
    kv = x[:, D_KV:D_KV + LANE]
    lane = lax.broadcasted_iota(jnp.int32, kv.shape, 1)
    is_k = lane < HEAD_DIM
    ms = jnp.sum(jnp.where(is_k, kv * kv, 0.0), axis=1, keepdims=True) * (1.0 / HEAD_DIM)
    kn = kv * lax.rsqrt(ms + NORM_EPS) * gk_ref[...]
    kr = kn * cq_ref[:, :LANE] + _swap_halves(kn, HEAD_DIM // 2) * sq_ref[:, :LANE]
    k2 = jnp.where(is_k, kr, pltpu.roll(kr, HEAD_DIM, axis=1))
    kd4_ref[0] = jnp.concatenate([k2] * (GW // LANE), axis=1).astype(BF16)
    v2 = jnp.where(is_k, pltpu.roll(kv, HEAD_DIM, axis=1), kv)
    v2t = jnp.transpose(v2)
    vdwt_ref[0] = jnp.concatenate([v2t] * (GW // LANE), axis=0).astype(BF16)
    qi = x[:, D_QI:D_QI + LANE]
    qi_ref[0] = qi * ci_ref[...] + _swap_halves(qi, IDX_DIM // 2) * si_ref[...]
    kw = x[:, D_KW:D_KW + LANE]
    kir = kw * ci_ref[...] + _swap_halves(kw, IDX_DIM // 2) * si_ref[...]
    ki1 = jnp.where(lane < IDX_DIM, kir, 0.0)
    ki2 = ki1 + pltpu.roll(ki1, IDX_DIM, axis=1)
    ki4 = ki2 + pltpu.roll(ki2, 2 * IDX_DIM, axis=1)
    kihi_ref[0], kilo_ref[0] = _split_bf16(ki4)
    wi_ref[0] = pltpu.roll(kw, LANE - IDX_DIM, axis=1) * (IDX_HEADS ** -0.5 * IDX_DIM ** -0.5)


def _rope_tables(S, dim, width):
    half = dim // 2
    inv = ROPE_THETA ** (-jnp.arange(half, dtype=F32) / half)
    ang = jnp.arange(S, dtype=F32)[:, None] * inv[None, :]
    cos = jnp.tile(jnp.cos(ang), (1, width // half))
    sin = jnp.tile(jnp.concatenate([-jnp.sin(ang), jnp.sin(ang)], axis=1), (1, width // dim))
    return cos, sin


def _dsa_prep(p, qn_g, kn_g):
    B, S, _ = p.shape
    GW, R = GROUP_W, DSA_PREP_ROWS
    cq, sq = _rope_tables(S, HEAD_DIM, GW)
    ci, si = _rope_tables(S, IDX_DIM, LANE)
    gq = jnp.tile(qn_g, N_HEADS)[None]
    gk = jnp.pad(kn_g, (0, LANE - HEAD_DIM))[None]
    rows = lambda w: pl.BlockSpec((1, R, w), lambda b, i: (b, i, 0))
    tab = lambda w: pl.BlockSpec((R, w), lambda b, i: (i, 0))
    cols = lambda r: pl.BlockSpec((1, r, R), lambda b, i: (b, 0, i))
    vec = lambda w: pl.BlockSpec((1, w), lambda b, i: (0, 0))
    return pl.pallas_call(
        _dsa_prep_kernel,
        grid=(B, S // R),
        in_specs=[rows(D_PAD), tab(GW), tab(GW), tab(LANE), tab(LANE), vec(GW), vec(LANE)],
        out_specs=[rows(GW), rows(GW), cols(GW), rows(LANE), rows(LANE), rows(LANE), rows(LANE)],
        out_shape=[jax.ShapeDtypeStruct((B, S, GW), F32), jax.ShapeDtypeStruct((B, S, GW), BF16),
                   jax.ShapeDtypeStruct((B, GW, S), BF16), jax.ShapeDtypeStruct((B, S, LANE), F32),
                   jax.ShapeDtypeStruct((B, S, LANE), BF16), jax.ShapeDtypeStruct((B, S, LANE), BF16),
                   jax.ShapeDtypeStruct((B, S, LANE), F32)],
        compiler_params=pltpu.CompilerParams(dimension_semantics=("parallel", "parallel"),
                                             vmem_limit_bytes=VMEM_LIMIT),
        name="dsa_prep",
    )(p, cq, sq, ci, si, gq, gk)


def _dsa_attn_norm(qd, kd4, vdwt, qi, ki_hi, ki_lo, wi, g):
    B, S, _ = qd.shape
    n_sel = min(TOPK_MAX, S // 4)
    assert S % DSA_KEY_STEP == 0 and n_sel <= DSA_KEY_STEP
    kls = tuple(range(DSA_KEY_STEP, S + 1, DSA_KEY_STEP))
    blk = lambda w: pl.BlockSpec((1, DSA_BLOCK, w), lambda b, i: (b, i, 0))
    per_b = lambda r, c: pl.BlockSpec((1, r, c), lambda b, i: (b, 0, 0))
    return pl.pallas_call(
        functools.partial(_dsa_kernel, kls=kls, n_sel=n_sel),
        grid=(B, S // DSA_BLOCK),
        in_specs=[blk(GROUP_W), per_b(S, GROUP_W), per_b(GROUP_W, S), blk(IDX_HEADS * IDX_DIM),
                  per_b(S, LANE), per_b(S, LANE), blk(LANE), pl.BlockSpec((1, GROUP_W), lambda b, i: (0, 0))],
        out_specs=blk(GROUP_W),
        out_shape=jax.ShapeDtypeStruct((B, S, GROUP_W), F32),
        compiler_params=pltpu.CompilerParams(dimension_semantics=("parallel", "parallel"),
                                             vmem_limit_bytes=VMEM_LIMIT),
        name="dsa_attn",
    )(qd, kd4, vdwt, qi, ki_hi, ki_lo, wi, g[None])


RWKV_CHUNK = 64
RWKV_LOW = RWKV_W_RANK + RWKV_A_RANK + RWKV_G_RANK
RWKV_GROUP = N_HEADS
RWKV_BATCH = 8


def _dot(a, b):
    return jnp.dot(a, b, preferred_element_type=F32)


def _dot_nt(a, b):
    return lax.dot_general(a, b, (((1,), (1,)), ((), ())), preferred_element_type=F32)


def _dot_tn(a, b):
    return lax.dot_general(a, b, (((0,), (0,)), ((), ())), preferred_element_type=F32)


def _split3_bf16(t):
    p1 = t.astype(BF16)
    r1 = t - p1.astype(F32)
    p2 = r1.astype(BF16)
    p3 = (r1 - p2.astype(F32)).astype(BF16)
    return p1, p2, p3


def _dot_f32_by_exact(a, b_exact):
    m = a.shape[0]
    r = _dot(jnp.concatenate(_split3_bf16(a), axis=0), b_exact)
    return r[:m] + r[m:2 * m] + r[2 * m:]


def _dot_exact_by_f32(a_exact, b):
    n = b.shape[1]
    r = _dot(a_exact, jnp.concatenate(_split3_bf16(b), axis=1))
    return r[:, :n] + r[:, n:2 * n] + r[:, 2 * n:]


def _dot3(a, b_hi, b_lo):
    a_hi, a_lo = _split_bf16(a)
    return _dot(jnp.concatenate([a_hi, a_lo, a_hi], axis=1), jnp.concatenate([b_hi, b_hi, b_lo], axis=0))


def _softplus(z):
    return jnp.maximum(z, 0.0) + jnp.log(1.0 + jnp.exp(-jnp.abs(z)))


def _rwkv_kernel(p_ref, mu_ref, vec_ref, lhi_ref, llo_ref, o_ref, state_ref, prev_ref):
    @pl.when(pl.program_id(1) == 0)
    def _():
        state_ref[...] = jnp.zeros_like(state_ref)
        prev_ref[...] = jnp.zeros_like(prev_ref)

    chains = [_rwkv_chunk(p_ref.at[n], mu_ref, vec_ref, lhi_ref, llo_ref, o_ref.at[n], state_ref.at[n],
                          prev_ref.at[n]) for n in range(RWKV_BATCH)]
    for _ in itertools.zip_longest(*chains):
        pass


def _rwkv_chunk(p_ref, mu_ref, vec_ref, lhi_ref, llo_ref, o_ref, state_ref, prev_ref):
    L, GW = RWKV_CHUNK, GROUP_W
    p = p_ref[...]
    row = lax.broadcasted_iota(jnp.int32, p.shape, 0)
    prev = jnp.where(row == 0, prev_ref[...], pltpu.roll(p, 1, axis=0))
    prev_ref[...] = p[L - 1:L, :]
    ps = p + (prev - p) * mu_ref[...]
    r, k, v = ps[:, :GW], ps[:, GW:2 * GW], ps[:, 2 * GW:3 * GW]
    low = ps[:, 3 * GW:]
    lane_low = lax.broadcasted_iota(jnp.int32, low.shape, 1)
    low = jnp.where(lane_low < RWKV_W_RANK, jnp.tanh(low),
                    jnp.where(lane_low < RWKV_W_RANK + RWKV_A_RANK, low, jax.nn.sigmoid(low)))
    up = _dot3(low, lhi_ref[...], llo_ref[...])
    w0, a0, k_k, k_a = vec_ref[0:1, :], vec_ref[1:2, :], vec_ref[2:3, :], vec_ref[3:4, :]
    r_k, ln_g, ln_b = vec_ref[4:5, :], vec_ref[5:6, :], vec_ref[6:7, :]
    logw = -jnp.exp(-_softplus(-(w0 + up[:, :GW])) - 0.5)
    rate = jax.nn.sigmoid(a0 + up[:, GW:2 * GW])
    gate = up[:, 2 * GW:]

    ri = lax.broadcasted_iota(jnp.int32, (GW, GW), 0)
    ci = lax.broadcasted_iota(jnp.int32, (GW, GW), 1)
    ones_bd = jnp.where((ri // HEAD_DIM) == (ci // HEAD_DIM), 1.0, 0.0).astype(BF16)

    kk = k * k_k
    k = k * (1.0 + (rate - 1.0) * k_a)
    seg = _dot_f32_by_exact(jnp.concatenate([kk * kk, r * k * r_k], axis=0), ones_bd)
    kk = kk / jnp.maximum(jnp.sqrt(seg[:L]), 1e-12)

    ti = lax.broadcasted_iota(jnp.int32, (L, L), 0)
    tj = lax.broadcasted_iota(jnp.int32, (L, L), 1)
    lc = _dot_exact_by_f32(jnp.where(tj <= ti, 1.0, 0.0).astype(BF16), logw)
    lc_last = lc[L - 1:L, :]
    dec_in = jnp.exp(lc)
    dec_out = jnp.exp(-lc)
    a_t = -kk * jnp.exp(lc - logw)
    b_t = kk * rate * dec_out
    k_t = k * dec_out
    r_t = r * dec_in
    to_end = jnp.exp(lc_last)

    SG = RWKV_GROUP * L
    gi = lax.broadcasted_iota(jnp.int32, (SG, SG), 0)
    gj = lax.broadcasted_iota(jnp.int32, (SG, SG), 1)
    g_same = (gi // L) == (gj // L)
    strict = g_same & ((gj % L) < (gi % L))
    incl = g_same & ((gj % L) <= (gi % L))
    eye = jnp.where(gi == gj, 1.0, 0.0)
    stack = lambda t: jnp.concatenate([t] * RWKV_GROUP, axis=0)
    bd = lambda t: jnp.where(g_same, stack(t), 0.0).astype(BF16)
    n_doublings = RWKV_CHUNK.bit_length() - 2
    ys = []
    for g in range(N_HEADS // RWKV_GROUP):
        cols = slice(g * SG, (g + 1) * SG)
        a_bd, r_bd, v_bd = bd(a_t[:, cols]), bd(r_t[:, cols]), bd(v[:, cols])
        m = _dot_nt(jnp.concatenate([a_bd, r_bd], axis=0),
                    jnp.concatenate([stack(b_t[:, cols]), stack(k_t[:, cols])], axis=0).astype(BF16))
        yield
        m_ab = jnp.where(strict, m[:SG, :SG], 0.0)
        m_ak = jnp.where(strict, m[:SG, SG:], 0.0).astype(BF16)
        m_rb = jnp.where(incl, m[SG:, :SG], 0.0).astype(BF16)
        m_rk = jnp.where(incl, m[SG:, SG:], 0.0).astype(BF16)

        inv = eye + m_ab
        pw_b = m_ab.astype(BF16)
        sq = _dot(pw_b, pw_b)
        yield
        pw_b = sq.astype(BF16)
        for s in range(n_doublings - 1):
            both = _dot(jnp.concatenate([inv.astype(BF16), pw_b], axis=0), pw_b)
            yield
            inv = inv + both[:SG]
            pw_b = both[SG:].astype(BF16)
        last = _dot(inv.astype(BF16), pw_b)
        t0 = state_ref[g]
        t0_b = t0.astype(BF16)
        rhs = _dot(jnp.concatenate([a_bd, m_ak], axis=1), jnp.concatenate([t0_b, v_bd], axis=0))
        yield
        inv_b = (inv + last).astype(BF16)
        u = _dot(inv_b, rhs.astype(BF16))
        yield
        u = u.astype(BF16)
        y_bd = _dot(jnp.concatenate([r_bd, m_rb, m_rk], axis=1),
                    jnp.concatenate([t0_b, u, v_bd], axis=0))
        end_g = to_end[:, cols]
        to_end_col = jnp.sum(jnp.where(gi == gj, jnp.broadcast_to(end_g, (SG, SG)), 0.0), axis=1, keepdims=True)
        carried = _dot_tn(jnp.concatenate([bd(b_t[:, cols] * end_g), bd(k_t[:, cols] * end_g)], axis=0),
                          jnp.concatenate([u, v_bd], axis=0))
        yield
        ys.append(sum(y_bd[h * L:(h + 1) * L, :] for h in range(RWKV_GROUP)))
        state_ref[g] = to_end_col * t0 + carried
    y = jnp.concatenate(ys, axis=1)

    inv_d = 1.0 / HEAD_DIM
    mean = _dot_f32_by_exact(y, ones_bd) * inv_d
    yield
    yc = y - mean
    var = _dot_f32_by_exact(yc * yc, ones_bd) * inv_d
    yield
    yn = yc * lax.rsqrt(var + RWKV_GN_EPS) * ln_g + ln_b
    o_ref[...] = (yn + seg[L:] * v) * gate


def _rwkv7_time_mix(p, mu, w0, w2, a0, a2, g2, k_k, k_a, r_k, ln_g, ln_b):
    B, S, _ = p.shape
    GW = GROUP_W
    assert S % RWKV_CHUNK == 0 and RWKV_CHUNK == HEAD_DIM and B % RWKV_BATCH == 0
    low_w = jnp.zeros((RWKV_LOW, 3 * GW), F32)
    low_w = low_w.at[:RWKV_W_RANK, :GW].set(w2)
    low_w = low_w.at[RWKV_W_RANK:RWKV_W_RANK + RWKV_A_RANK, GW:2 * GW].set(a2)
    low_w = low_w.at[RWKV_W_RANK + RWKV_A_RANK:, 2 * GW:].set(g2)
    l_hi, l_lo = _split_bf16(low_w)
    vecs = jnp.stack([w0, a0, k_k, k_a, r_k, ln_g, ln_b, jnp.zeros_like(w0)], 0)
    full = lambda a: pl.BlockSpec(a.shape, lambda b, c: (0,) * a.ndim)
    mu2 = mu[None]
    return pl.pallas_call(
        _rwkv_kernel,
        grid=(B // RWKV_BATCH, S // RWKV_CHUNK),
        in_specs=[pl.BlockSpec((RWKV_BATCH, RWKV_CHUNK, A_PAD), lambda b, c: (b, c, 0)),
                  full(mu2), full(vecs), full(l_hi), full(l_lo)],
        out_specs=pl.BlockSpec((RWKV_BATCH, RWKV_CHUNK, GW), lambda b, c: (b, c, 0)),
        out_shape=jax.ShapeDtypeStruct((B, S, GW), F32),
        scratch_shapes=[pltpu.VMEM((RWKV_BATCH, N_HEADS // RWKV_GROUP, RWKV_GROUP * RWKV_CHUNK,
                                    RWKV_GROUP * HEAD_DIM), F32),
                        pltpu.VMEM((RWKV_BATCH, 1, A_PAD), F32)],
        compiler_params=pltpu.CompilerParams(dimension_semantics=("parallel", "arbitrary"),
                                             vmem_limit_bytes=VMEM_LIMIT),
        name="rwkv7",
    )(p, mu2, vecs, l_hi, l_lo)


SB_UNROLL = 4


def _sb_kernel(q_ref, k_ref, v_ref, g_ref, o_ref, kbd_ref, vbd_ref):
    i = pl.program_id(1)
    T, GW, H = SB_BLOCK, GROUP_W, N_HEADS
    lane_h = lax.broadcasted_iota(jnp.int32, (T, GW), 1) // HEAD_DIM
    k_new, v_new = k_ref[0], v_ref[0]
    for h in range(H):
        kbd_ref[i, h * T:(h + 1) * T, :] = jnp.where(lane_h == h, k_new, 0.0).astype(BF16)
        vbd_ref[i, h * T:(h + 1) * T, :] = jnp.where(lane_h == h, v_new, 0.0).astype(BF16)

    q = q_ref[0].astype(BF16)
    si = lax.broadcasted_iota(jnp.int32, (T, 2 * T), 0)
    sj = lax.broadcasted_iota(jnp.int32, (T, 2 * T), 1)
    later_and_all = jnp.where((si > sj) | (sj >= T), 1.0, 0.0).astype(BF16)
    qrow = lax.broadcasted_iota(jnp.int32, (T, H * T), 0)
    kcol = lax.broadcasted_iota(jnp.int32, (T, H * T), 1) % T
    causal = kcol < qrow

    def key_block_stages(j, box, diagonal):
        z = _dot_nt(q, kbd_ref[j]) * HEAD_DIM ** -0.5
        yield
        soft = jnp.log(1.0 + jnp.exp(-jnp.abs(z)))
        log1m = -(jnp.maximum(z, 0.0) + soft)
        log_sig = jnp.minimum(z, 0.0) - soft
        log1m_in = (jnp.where(causal, log1m, 0.0) if diagonal else log1m).astype(BF16)
        sums = _dot(jnp.concatenate([log1m_in[:, h * T:(h + 1) * T] for h in range(H)], axis=0), later_and_all)
        yield
        suffix = jnp.concatenate([sums[h * T:(h + 1) * T, :T] for h in range(H)], axis=1)
        total = jnp.concatenate([sums[h * T:(h + 1) * T, T:] for h in range(H)], axis=1)
        att = jnp.exp(log_sig + suffix + box["carry"])
        if diagonal:
            att = jnp.where(causal, att, 0.0)
        box["carry"] = box["carry"] + total
        pv = _dot(att.astype(BF16), vbd_ref[j])
        yield
        box["acc"] = box["acc"] + pv

    def key_blocks(js, state, diagonal=False):
        box = {"carry": state[0], "acc": state[1]}
        for _ in itertools.zip_longest(*[key_block_stages(j, box, diagonal) for j in js]):
            pass
        return box["carry"], box["acc"]

    state = key_blocks([i], (jnp.zeros((T, H * T), F32), jnp.zeros((T, GW), F32)), True)
    rem = i % SB_UNROLL
    state = lax.fori_loop(0, rem, lambda it, st: key_blocks([i - 1 - it], st), state)
    top = i - 1 - rem
    _, y = lax.fori_loop(0, i // SB_UNROLL,
                         lambda it, st: key_blocks([top - SB_UNROLL * it - n for n in range(SB_UNROLL)], st), state)

    hi_ = lax.broadcasted_iota(jnp.int32, (GW, GW), 0) // HEAD_DIM
    hj_ = lax.broadcasted_iota(jnp.int32, (GW, GW), 1) // HEAD_DIM
    ones_bd = jnp.where(hi_ == hj_, 1.0, 0.0).astype(BF16)
    ms = _dot_f32_by_exact(y * y, ones_bd) * (1.0 / HEAD_DIM)
    o_ref[0] = y * lax.rsqrt(ms + NORM_EPS) * g_ref[...]


def _stick_breaking_norm(p, g):
    B, S, _ = p.shape
    GW = GROUP_W
    assert S % SB_BLOCK == 0
    col = lambda n: pl.BlockSpec((1, SB_BLOCK, GW), lambda b, i: (b, i, n))
    blk = col(0)
    q = k = v = p
    stacked = pltpu.VMEM((S // SB_BLOCK, N_HEADS * SB_BLOCK, GW), BF16)
    return pl.pallas_call(
        _sb_kernel,
        grid=(B, S // SB_BLOCK),
        in_specs=[col(0), col(1), col(2), pl.BlockSpec((1, GW), lambda b, i: (0, 0))],
        out_specs=blk,
        out_shape=jax.ShapeDtypeStruct((B, S, GW), F32),
        scratch_shapes=[stacked, stacked],
        compiler_params=pltpu.CompilerParams(dimension_semantics=("parallel", "arbitrary"),
                                             vmem_limit_bytes=VMEM_LIMIT),
        name="stick_breaking",
    )(q, k, v, g[None])


ML_HALO = 8
ML_GROUP = N_HEADS
ML_BATCH = 4


def _dot_nt_exact_by_f32(a_exact, b):
    return _dot_nt(jnp.concatenate([a_exact] * 3, axis=1), jnp.concatenate(_split3_bf16(b), axis=1))


def _mlstm_kernel(p_ref, cw_ref, cb_ref, gb_ref, g_ref, o_ref, ext_ref, ct_ref, n_ref, m_ref):
    @pl.when(pl.program_id(1) == 0)
    def _():
        ext_ref[...] = jnp.zeros_like(ext_ref)
        ct_ref[...] = jnp.zeros_like(ct_ref)
        n_ref[...] = jnp.zeros_like(n_ref)
        m_ref[...] = jnp.zeros_like(m_ref)

    chains = [_mlstm_chunk(p_ref.at[n], cw_ref, cb_ref, gb_ref, g_ref, o_ref.at[n], ext_ref.at[n], ct_ref.at[n],
                           n_ref.at[n], m_ref.at[n]) for n in range(ML_BATCH)]
    for _ in itertools.zip_longest(*chains):
        pass


def _mlstm_chunk(p_ref, cw_ref, cb_ref, gb_ref, g_ref, o_ref, ext_ref, ct_ref, n_ref, m_ref):
    L, GW, H = ML_CHUNK, GROUP_W, N_HEADS
    x = p_ref[...]
    ext_ref[ML_HALO:, :] = x[:, :2 * GW]
    conv = cb_ref[...]
    for j in range(ML_CONV):
        conv = conv + cw_ref[j:j + 1, :] * ext_ref[pl.ds(ML_HALO - (ML_CONV - 1) + j, L), :]
    ext_ref[:ML_HALO, :] = x[L - ML_HALO:, :2 * GW]
    qk = conv * jax.nn.sigmoid(conv)
    q, k = qk[:, :GW], qk[:, GW:] * HEAD_DIM ** -0.5
    v, o = x[:, 2 * GW:3 * GW], x[:, 3 * GW:4 * GW]

    gates = x[:, 4 * GW:]
    gi = lax.broadcasted_iota(jnp.int32, (LANE, 2 * GW), 0)
    gj = lax.broadcasted_iota(jnp.int32, (LANE, 2 * GW), 1)
    expand = jnp.where(gi == (gj % GW) // HEAD_DIM + H * (gj // GW), 1.0, 0.0).astype(BF16)
    graw = _dot_f32_by_exact(gates, expand) + gb_ref[...]
    yield
    capped = GATE_CAP * jnp.tanh(graw * (1.0 / GATE_CAP))
    log_i = capped[:, :GW]
    cf = capped[:, GW:]
    log_f = jnp.minimum(cf, 0.0) - jnp.log(1.0 + jnp.exp(-jnp.abs(cf)))

    ti = lax.broadcasted_iota(jnp.int32, (L, L), 0)
    tj = lax.broadcasted_iota(jnp.int32, (L, L), 1)
    bf = _dot_exact_by_f32(jnp.where(tj <= ti, 1.0, 0.0).astype(BF16), log_f)
    yield
    b_last = bf[L - 1:L, :]
    m_row, n_row = m_ref[...], n_ref[...]
    dec = b_last - bf + log_i
    m_new = jnp.maximum(b_last + m_row, jnp.max(dec, axis=0, keepdims=True))
    kw = k * jnp.exp(dec - m_new)
    s_old = jnp.exp(b_last + m_row - m_new)
    n_ref[...] = s_old * n_row + jnp.sum(kw, axis=0, keepdims=True)
    m_ref[...] = m_new
    g_in = bf + m_row
    li_b = log_i - bf

    SG = ML_GROUP * L
    ri = lax.broadcasted_iota(jnp.int32, (SG, SG), 0)
    ci = lax.broadcasted_iota(jnp.int32, (SG, SG), 1)
    same_head = (ri // L) == (ci // HEAD_DIM)
    first_lane = ci == (ri // L) * HEAD_DIM
    sel_first = jnp.where(first_lane, 1.0, 0.0).astype(BF16)
    stack = lambda t: jnp.concatenate([t] * ML_GROUP, axis=0)
    pick = lambda t: jnp.sum(jnp.where(first_lane, t, 0.0), axis=1, keepdims=True)
    rt = lax.broadcasted_iota(jnp.int32, (SG, L), 0) % L
    rs = lax.broadcasted_iota(jnp.int32, (SG, L), 1)
    hs = []
    for g in range(H // ML_GROUP):
        cols = slice(g * SG, (g + 1) * SG)
        ct = ct_ref[g]
        qs = jnp.where(same_head, stack(q[:, cols]), 0.0)
        qs_b = qs.astype(BF16)
        v_b = v[:, cols].astype(BF16)
        b_col = pick(stack(bf[:, cols]))
        g_col = pick(stack(g_in[:, cols]))
        row_part = _dot_nt_exact_by_f32(sel_first, li_b[:, cols])
        qk = _dot_nt(qs_b, k[:, cols].astype(BF16))
        inter = _dot(qs_b, ct.astype(BF16))
        carried = _dot_tn(kw[:, cols].astype(BF16), v_b)
        yield
        dmat = jnp.where(rs <= rt, b_col + row_part, -jnp.inf)
        m_t = jnp.maximum(g_col, jnp.max(dmat, axis=1, keepdims=True))
        s_inter = jnp.exp(g_col - m_t)
        sqk = qk * jnp.exp(dmat - m_t)
        intra = _dot(sqk.astype(BF16), v_b)
        yield
        num = s_inter * inter + jnp.where(same_head, intra, 0.0)
        den = (s_inter * jnp.sum(qs * n_row[:, cols], axis=1, keepdims=True)
               + jnp.sum(sqk, axis=1, keepdims=True))
        hst = num / jnp.maximum(jnp.abs(den), jnp.exp(-m_t))
        hs.append(sum(hst[n * L:(n + 1) * L, :] for n in range(ML_GROUP)))
        ct_ref[g] = s_old[:, cols] * ct + jnp.where(same_head, carried, 0.0)
    h = jnp.concatenate(hs, axis=1)

    hi_ = lax.broadcasted_iota(jnp.int32, (GW, GW), 0) // HEAD_DIM
    hj_ = lax.broadcasted_iota(jnp.int32, (GW, GW), 1) // HEAD_DIM
    ones_bd = jnp.where(hi_ == hj_, 1.0, 0.0).astype(BF16)
    ms = _dot_f32_by_exact(h * h, ones_bd) * (1.0 / HEAD_DIM)
    yield
    o_ref[...] = jax.nn.sigmoid(o) * (h * lax.rsqrt(ms + NORM_EPS) * g_ref[...])


def _mlstm_mix(p, conv_w, conv_b, ig_b, fg_b, norm_g):
    B, S, _ = p.shape
    GW = GROUP_W
    assert S % ML_CHUNK == 0 and ML_CONV - 1 <= ML_HALO <= ML_CHUNK and B % ML_BATCH == 0
    gate_b = jnp.concatenate([jnp.repeat(ig_b, HEAD_DIM), jnp.repeat(fg_b, HEAD_DIM)])[None]
    full = lambda a: pl.BlockSpec(a.shape, lambda b, c: (0,) * a.ndim)
    cb2, g2 = conv_b[None], norm_g[None]
    return pl.pallas_call(
        _mlstm_kernel,
        grid=(B // ML_BATCH, S // ML_CHUNK),
        in_specs=[pl.BlockSpec((ML_BATCH, ML_CHUNK, C_PAD), lambda b, c: (b, c, 0)),
                  full(conv_w), full(cb2), full(gate_b), full(g2)],
        out_specs=pl.BlockSpec((ML_BATCH, ML_CHUNK, GW), lambda b, c: (b, c, 0)),
        out_shape=jax.ShapeDtypeStruct((B, S, GW), F32),
        scratch_shapes=[pltpu.VMEM((ML_BATCH, ML_HALO + ML_CHUNK, 2 * GW), F32),
                        pltpu.VMEM((ML_BATCH, N_HEADS // ML_GROUP, ML_GROUP * ML_CHUNK, ML_GROUP * HEAD_DIM), F32),
                        pltpu.VMEM((ML_BATCH, 1, GW), F32), pltpu.VMEM((ML_BATCH, 1, GW), F32)],
        compiler_params=pltpu.CompilerParams(dimension_semantics=("parallel", "arbitrary"),
                                             vmem_limit_bytes=VMEM_LIMIT),
        name="mlstm",
    )(p, conv_w, cb2, gate_b, g2)


def _rms_norm(x, g):
    xf = x.astype(F32)
    y = xf * lax.rsqrt(jnp.mean(xf * xf, -1, keepdims=True) + NORM_EPS)
    return (y * g.astype(F32)).astype(x.dtype)


def _rope(x, pos):
    half = x.shape[-1] // 2
    inv = ROPE_THETA ** (-jnp.arange(half, dtype=F32) / half)
    ang = pos.astype(F32)[:, None] * inv[None, :]
    cos = jnp.cos(ang)[None, :, None, :]
    sin = jnp.sin(ang)[None, :, None, :]
    xf = x.astype(F32)
    x1, x2 = xf[..., :half], xf[..., half:]
    return jnp.concatenate([x1 * cos - x2 * sin, x2 * cos + x1 * sin], -1).astype(x.dtype)


def kernel(x, c, ada_w, ada_b, norm1_g, norm2_g, w_in, rk_mu, rk_w0, rk_w2, rk_a0, rk_a2, rk_g2, rk_kk, rk_ka, rk_rk, rk_ln_g, rk_ln_b, sb_norm_g, ml_conv_w, ml_conv_b, ml_ig_b, ml_fg_b, ml_norm_g, ds_qn_g, ds_kn_g, ds_out_g, w_out, moe_wg, moe_bg, moe_we, moe_be, moe_w1, moe_w3, moe_w2):
    B, S, D = x.shape
    H, d = N_HEADS, HEAD_DIM
    depth = ada_w.shape[0]
    pos = jnp.arange(S)
    c_act = jax.nn.silu(c)
    for l in range(depth):
        mod = (c_act @ ada_w[l] + ada_b[l])[:, None, :]
        sh1, sc1, gt1, sh2, sc2, gt2 = jnp.split(mod, 6, axis=-1)

        pA, pB, pC, pD = _in_proj(x, sc1, sh1, norm1_g[l][None], _pad_w_in(w_in[l]))

        yA = _rwkv7_time_mix(pA, rk_mu[l], rk_w0[l], rk_w2[l], rk_a0[l], rk_a2[l], rk_g2[l],
                             rk_kk[l], rk_ka[l], rk_rk[l], rk_ln_g[l], rk_ln_b[l])

        yB = _stick_breaking_norm(pB, sb_norm_g[l])

        yC = _mlstm_mix(pC, ml_conv_w[l], ml_conv_b[l], ml_ig_b[l], ml_fg_b[l], ml_norm_g[l])

        yD = _dsa_attn_norm(*_dsa_prep(pD, ds_qn_g[l], ds_kn_g[l]), ds_out_g[l])

        router = jnp.pad(jnp.concatenate([moe_wg[l], moe_we[l]], 1),
                         ((0, 0), (0, ROUTER_PAD - N_GROUPS - N_EXPERTS)))
        r_hi, r_lo = _split_bf16(router)
        r_b = jnp.pad(jnp.concatenate([moe_bg[l], moe_be[l]]), (0, ROUTER_PAD - N_GROUPS - N_EXPERTS))[None]
        x1, h2, route = _out_proj((yA, yB, yC, yD), x, gt1, sc2, sh2, norm2_g[l][None],
                                  w_out[l].astype(BF16), r_hi, r_lo, r_b)

        moe = _hier_moe(h2.reshape(B * S, D), route.reshape(B * S, ROUTER_PAD),
                        moe_w1, moe_w3, moe_w2, l)
        x = x1 + gt2 * moe.reshape(B, S, D)
    return x
```

```python
import functools
import itertools

import jax
import jax.numpy as jnp
import numpy as np
from jax import lax
from jax.experimental import pallas as pl
from jax.experimental.pallas import tpu as pltpu

F32 = jnp.float32
BF16 = jnp.bfloat16

D_MODEL = 1024
N_MIXERS = 4
GROUP_W = D_MODEL // N_MIXERS
HEAD_DIM = 64
N_HEADS = GROUP_W // HEAD_DIM
NORM_EPS = 1e-6
RWKV_W_RANK = 32
RWKV_A_RANK = 32
RWKV_G_RANK = 64
RWKV_GN_EPS = 64e-5
SB_BLOCK = 128
ML_CHUNK = 64
ML_CONV = 4
GATE_CAP = 15.0
DSA_BLOCK = 128
IDX_HEADS = 4
IDX_DIM = 32
TOPK_MAX = 256
ROPE_THETA = 10000.0
N_GROUPS = 4
EXP_PER_GROUP = 8
N_EXPERTS = N_GROUPS * EXP_PER_GROUP
EXPERT_FF = D_MODEL // 2
TOP_IN_GROUP = 2

A_SIZES = (GROUP_W, GROUP_W, GROUP_W, RWKV_W_RANK, RWKV_A_RANK, RWKV_G_RANK)
B_SIZES = (GROUP_W, GROUP_W, GROUP_W)
C_SIZES = (GROUP_W, GROUP_W, GROUP_W, GROUP_W, N_HEADS, N_HEADS)
D_SIZES = (GROUP_W, HEAD_DIM, HEAD_DIM, IDX_HEADS * IDX_DIM, IDX_DIM, IDX_HEADS)
A_COLS = sum(A_SIZES)
B_COLS = sum(B_SIZES)
C_COLS = sum(C_SIZES)
D_COLS = sum(D_SIZES)

LANE = 128
A_PAD = 896
B_PAD = 768
C_PAD = 1152
D_PAD = 640
P_PAD = A_PAD + B_PAD + C_PAD + D_PAD
ROUTER_PAD = LANE

IN_ROWS = 256
OUT_ROWS = 512
MOE_ROWS = 512
VMEM_LIMIT = 48 * 1024 * 1024


def _split_cols(t, sizes):
    return jnp.split(t, [int(i) for i in np.cumsum(sizes)[:-1]], axis=-1)


def _in_proj_kernel(x_ref, sc_ref, sh_ref, g_ref, w_ref, oa_ref, ob_ref, oc_ref, od_ref):
    x = x_ref[0]
    y = x * lax.rsqrt(jnp.mean(x * x, -1, keepdims=True) + NORM_EPS) * g_ref[...]
    h = y * (1.0 + sc_ref[0]) + sh_ref[0]
    p = jnp.dot(h.astype(BF16), w_ref[...], preferred_element_type=F32)
    oa_ref[0] = p[:, :A_PAD]
    ob_ref[0] = p[:, A_PAD:A_PAD + B_PAD]
    oc_ref[0] = p[:, A_PAD + B_PAD:A_PAD + B_PAD + C_PAD]
    od_ref[0] = p[:, A_PAD + B_PAD + C_PAD:]


def _in_proj(x, sc, sh, g, w_pad):
    B, S, D = x.shape
    row = lambda w: pl.BlockSpec((1, IN_ROWS, w), lambda b, i: (b, i, 0))
    vec = pl.BlockSpec((1, 1, D), lambda b, i: (b, 0, 0))
    return pl.pallas_call(
        _in_proj_kernel,
        grid=(B, S // IN_ROWS),
        in_specs=[row(D), vec, vec, pl.BlockSpec((1, D), lambda b, i: (0, 0)),
                  pl.BlockSpec((D, P_PAD), lambda b, i: (0, 0))],
        out_specs=[row(A_PAD), row(B_PAD), row(C_PAD), row(D_PAD)],
        out_shape=[jax.ShapeDtypeStruct((B, S, w), F32) for w in (A_PAD, B_PAD, C_PAD, D_PAD)],
        compiler_params=pltpu.CompilerParams(dimension_semantics=("parallel", "parallel"),
                                             vmem_limit_bytes=VMEM_LIMIT),
        name="in_proj",
    )(x, sc, sh, g, w_pad)


def _pad_w_in(w):
    wa, wb, wc, wd = _split_cols(w, (A_COLS, B_COLS, C_COLS, D_COLS))
    padc = lambda t, n: jnp.pad(t, ((0, 0), (0, n - t.shape[1])))
    return jnp.concatenate([padc(wa, A_PAD), padc(wb, B_PAD), padc(wc, C_PAD), padc(wd, D_PAD)], 1).astype(BF16)


def _split_bf16(t):
    hi = t.astype(BF16)
    lo = (t - hi.astype(F32)).astype(BF16)
    return hi, lo


def _out_proj_kernel(ya_ref, yb_ref, yc_ref, yd_ref, x_ref, gt_ref, sc_ref, sh_ref, g_ref, w_ref,
                     rhi_ref, rlo_ref, rb_ref, x1_ref, h2_ref, route_ref):
    acc = jnp.zeros(x_ref.shape[1:], F32)
    for n, y_ref in enumerate((ya_ref, yb_ref, yc_ref, yd_ref)):
        acc += jnp.dot(y_ref[0].astype(BF16), w_ref[n * GROUP_W:(n + 1) * GROUP_W, :],
                       preferred_element_type=F32)
    x1 = x_ref[0] + gt_ref[0] * acc
    x1_ref[0] = x1
    y = x1 * lax.rsqrt(jnp.mean(x1 * x1, -1, keepdims=True) + NORM_EPS) * g_ref[...]
    h = y * (1.0 + sc_ref[0]) + sh_ref[0]
    hi, lo = _split_bf16(h)
    h2_ref[0] = hi
    lg = (jnp.dot(hi, rhi_ref[...], preferred_element_type=F32)
          + jnp.dot(lo, rhi_ref[...], preferred_element_type=F32)
          + jnp.dot(hi, rlo_ref[...], preferred_element_type=F32)) + rb_ref[...]
    route_ref[0] = _route(lg)


def _route(lg):
    lane = lax.broadcasted_iota(jnp.int32, lg.shape, 1)
    neg = -jnp.inf
    first = lambda hit: jnp.min(jnp.where(hit, lane, ROUTER_PAD), axis=1, keepdims=True)
    is_grp = lane < N_GROUPS
    grp = jnp.where(is_grp, lg, neg)
    g_max = jnp.max(grp, axis=1, keepdims=True)
    g_p = 1.0 / jnp.sum(jnp.where(is_grp, jnp.exp(grp - g_max), 0.0), axis=1, keepdims=True)
    g_idx = first(grp == g_max)
    e_lane = lane - N_GROUPS
    in_group = (e_lane >= 0) & (e_lane < N_EXPERTS) & (e_lane // EXP_PER_GROUP == g_idx)
    e_log = jnp.where(in_group, lg, neg)
    e1_max = jnp.max(e_log, axis=1, keepdims=True)
    e1_lane = first(e_log == e1_max)
    e_log2 = jnp.where(lane == e1_lane, neg, e_log)
    e2_max = jnp.max(e_log2, axis=1, keepdims=True)
    e2_lane = first(e_log2 == e2_max)
    ratio = jnp.exp(e2_max - e1_max)
    gate1 = g_p / (1.0 + ratio)
    gate2 = gate1 * ratio
    out = jnp.where(lane == 0, (e1_lane - N_GROUPS).astype(F32), 0.0)
    out = jnp.where(lane == 1, (e2_lane - N_GROUPS).astype(F32), out)
    out = jnp.where(lane == 2, gate1, out)
    return jnp.where(lane == 3, gate2, out)


def _out_proj(ys, x, gt, sc, sh, g, w_out, r_hi, r_lo, r_b):
    B, S, D = x.shape
    row = lambda w: pl.BlockSpec((1, OUT_ROWS, w), lambda b, i: (b, i, 0))
    vec = pl.BlockSpec((1, 1, D), lambda b, i: (b, 0, 0))
    full = lambda a: pl.BlockSpec(a.shape, lambda b, i: (0,) * a.ndim)
    return pl.pallas_call(
        _out_proj_kernel,
        grid=(B, S // OUT_ROWS),
        in_specs=[row(GROUP_W)] * 4 + [row(D), vec, vec, vec, full(g), full(w_out), full(r_hi), full(r_lo),
                                        full(r_b)],
        out_specs=[row(D), row(D), row(ROUTER_PAD)],
        out_shape=[jax.ShapeDtypeStruct((B, S, D), F32), jax.ShapeDtypeStruct((B, S, D), BF16),
                   jax.ShapeDtypeStruct((B, S, ROUTER_PAD), F32)],
        compiler_params=pltpu.CompilerParams(dimension_semantics=("parallel", "parallel"),
                                             vmem_limit_bytes=VMEM_LIMIT),
        name="out_proj",
    )(*ys, x, gt, sc, sh, g, w_out, r_hi, r_lo, r_b)


def _moe_ffn_kernel(blk_e_ref, x_ref, wt_ref, w1_ref, w3_ref, w2_ref, o_ref, w1b_ref, w3b_ref, w2b_ref):
    i = pl.program_id(0)
    changed = jnp.logical_or(i == 0, blk_e_ref[i] != blk_e_ref[jnp.maximum(i - 1, 0)])

    @pl.when(changed)
    def _():
        w1b_ref[...] = w1_ref[0, 0].astype(BF16)
        w3b_ref[...] = w3_ref[0, 0].astype(BF16)
        w2b_ref[...] = w2_ref[0, 0].astype(BF16)

    xb = x_ref[...]
    a = jnp.dot(xb, w1b_ref[...], preferred_element_type=F32)
    b = jnp.dot(xb, w3b_ref[...], preferred_element_type=F32)
    hmid = (a * jax.nn.sigmoid(a) * b).astype(BF16)
    y = jnp.dot(hmid, w2b_ref[...], preferred_element_type=F32)
    o_ref[...] = (y * wt_ref[...]).astype(o_ref.dtype)


def _moe_ffn(blk_e, xs, wt, w1, w3, w2, layer):
    n_slots, D = xs.shape
    n_blocks = n_slots // MOE_ROWS
    FF = w1.shape[-1]
    return pl.pallas_call(
        _moe_ffn_kernel,
        grid_spec=pltpu.PrefetchScalarGridSpec(
            num_scalar_prefetch=1,
            grid=(n_blocks,),
            in_specs=[pl.BlockSpec((MOE_ROWS, D), lambda i, e: (i, 0)),
                      pl.BlockSpec((MOE_ROWS, 1), lambda i, e: (i, 0)),
                      pl.BlockSpec((1, 1, D, FF), lambda i, e: (layer, e[i], 0, 0)),
                      pl.BlockSpec((1, 1, D, FF), lambda i, e: (layer, e[i], 0, 0)),
                      pl.BlockSpec((1, 1, FF, D), lambda i, e: (layer, e[i], 0, 0))],
            out_specs=pl.BlockSpec((MOE_ROWS, D), lambda i, e: (i, 0)),
            scratch_shapes=[pltpu.VMEM((D, FF), BF16), pltpu.VMEM((D, FF), BF16), pltpu.VMEM((FF, D), BF16)],
        ),
        out_shape=jax.ShapeDtypeStruct((n_slots, D), BF16),
        compiler_params=pltpu.CompilerParams(dimension_semantics=("arbitrary",),
                                             vmem_limit_bytes=VMEM_LIMIT),
        name="moe_ffn",
    )(blk_e, xs, wt, w1, w3, w2)


def _hier_moe(h2, route, w1, w3, w2, layer):
    N, D = h2.shape
    expert = route[:, :TOP_IN_GROUP].astype(jnp.int32)
    gate = route[:, TOP_IN_GROUP:2 * TOP_IN_GROUP]
    n_asg = N * TOP_IN_GROUP
    n_blocks = n_asg // MOE_ROWS + N_EXPERTS
    n_slots = n_blocks * MOE_ROWS
    n_fill = n_slots - n_asg
    flat_e = expert.reshape(n_asg // LANE, LANE)
    counts = jnp.sum(flat_e[None] == jnp.arange(N_EXPERTS)[:, None, None], axis=(1, 2)).astype(jnp.int32)
    pad_counts = (counts + MOE_ROWS - 1) // MOE_ROWS * MOE_ROWS
    pad_end = jnp.cumsum(pad_counts)
    blk_start = jnp.arange(n_blocks) * MOE_ROWS
    blk_e = jnp.minimum(jnp.sum(pad_end[None, :] <= blk_start[:, None], 1), N_EXPERTS - 1).astype(jnp.int32)
    fill_end = jnp.cumsum(pad_counts - counts)
    fill_id = jnp.arange(n_fill).reshape(n_fill // LANE, LANE)
    fill_e = jnp.zeros_like(fill_id)
    for e in range(N_EXPERTS):
        fill_e = fill_e + (fill_id >= fill_end[e])
    keys = jnp.concatenate([flat_e.reshape(n_asg) * 2, fill_e.reshape(n_fill) * 2 + 1])
    toks = jnp.concatenate([jnp.arange(n_asg, dtype=jnp.int32) // TOP_IN_GROUP, jnp.zeros((n_fill,), jnp.int32)])
    wts = jnp.concatenate([gate.reshape(n_asg), jnp.zeros((n_fill,), F32)])
    _, slot_tok, slot_w, slot_src = lax.sort((keys, toks, wts, jnp.arange(n_slots, dtype=jnp.int32)), num_keys=1)
    _, entry_slot = lax.sort((slot_src, jnp.arange(n_slots, dtype=jnp.int32)), num_keys=1)
    asg_slot = entry_slot[:n_asg].reshape(N, TOP_IN_GROUP)
    yb = _moe_ffn(blk_e, _gather_rows(h2, slot_tok), slot_w[:, None], w1, w3, w2, layer)
    return _gather_rows(yb, asg_slot[:, 0]).astype(F32) + _gather_rows(yb, asg_slot[:, 1]).astype(F32)


def _gather_rows(t, idx):
    rows, width = t.shape
    words = lax.bitcast_convert_type(t.reshape(rows, width // 2, 2), jnp.uint32)
    return lax.bitcast_convert_type(words[idx], BF16).reshape(idx.shape[0], width)


INT_MIN = -2 ** 31
DSA_KEY_STEP = 256


def _float_order_key(x):
    bits = pltpu.bitcast(x, jnp.int32)
    bits = jnp.where(x == 0.0, 0, bits)
    return bits ^ ((bits >> 31) & 0x7FFFFFFF)


COL_PART = 64


def _col_reduce(x, reduce):
    part = reduce(x.reshape(x.shape[0] // COL_PART, COL_PART, x.shape[1]), axis=0)
    return reduce(part, axis=0, keepdims=True)


def _col_count(mask):
    return _col_reduce(jnp.where(mask, 1.0, 0.0), jnp.sum)


def _head_block_diag(t, group):
    n_heads = t.shape[0] // group
    row_h = lax.broadcasted_iota(jnp.int32, t.shape, 0) // group
    return jnp.concatenate([jnp.where(row_h == h, t, 0.0) for h in range(n_heads)], axis=1)


def _dsa_block_t(qd_ref, kd4_ref, vdwt_ref, qi_ref, kihi_ref, kilo_ref, wi_ref, g_ref, o_ref, *, kl, n_sel):
    Q = DSA_BLOCK
    q0 = pl.program_id(1) * Q
    w_hi, w_lo = _split_bf16(_head_block_diag(jnp.transpose(qi_ref[0]), IDX_DIM))
    k_hi, k_lo = kihi_ref[0, :kl, :], kilo_ref[0, :kl, :]
    sc = _dot(jnp.concatenate([k_hi, k_lo, k_hi], axis=1),
              jnp.concatenate([w_hi, w_hi, w_lo], axis=0))
    wit = jnp.transpose(wi_ref[0])
    score = sum(wit[h:h + 1, :] * jnp.maximum(sc[:, h * Q:(h + 1) * Q], 0.0) for h in range(IDX_HEADS))
    kidx = lax.broadcasted_iota(jnp.int32, (kl, Q), 0)
    qpos = q0 + lax.broadcasted_iota(jnp.int32, (kl, Q), 1)
    adm = kidx <= qpos
    key = _float_order_key(jnp.where(adm, score, -jnp.inf))

    def value_bit(it, tau):
        cand = tau | jnp.left_shift(jnp.int32(1), 31 - it)
        return jnp.where(_col_count(key >= (cand ^ INT_MIN)) >= n_sel, cand, tau)

    tau = lax.fori_loop(0, 32, value_bit, jnp.zeros((1, Q), jnp.int32)) ^ INT_MIN
    gt = key > tau
    eq = (key == tau) & adm
    need = n_sel - _col_count(gt)
    n_eq = _col_count(eq)

    def index_bits():
        def index_bit(it, bound):
            cand = bound | jnp.left_shift(jnp.int32(1), 11 - it)
            return jnp.where(_col_count(eq & (kidx < cand)) <= need, cand, bound)
        return lax.fori_loop(0, 12, index_bit, jnp.zeros((1, Q), jnp.int32))

    bound = lax.cond(jnp.max(n_eq - need) > 0.0, index_bits, lambda: jnp.full((1, Q), kl, jnp.int32))
    sel = gt | (eq & (kidx < bound))

    w_att = _head_block_diag(jnp.transpose(qd_ref[0]), HEAD_DIM).astype(BF16)
    lg = _dot(kd4_ref[0, :kl, :], w_att) * HEAD_DIM ** -0.5
    lg = jnp.where(jnp.concatenate([sel] * N_HEADS, axis=1), lg, -jnp.inf)
    p = jnp.exp(lg - _col_reduce(lg, jnp.max))
    out_t = _dot(vdwt_ref[0, :, :kl], p.astype(BF16)) / _col_reduce(p, jnp.sum)
    row_h = lax.broadcasted_iota(jnp.int32, (GROUP_W, Q), 0) // HEAD_DIM
    nat_t = sum(jnp.where(row_h == h, out_t[:, h * Q:(h + 1) * Q], 0.0) for h in range(N_HEADS))
    r = jnp.transpose(nat_t)
    hi_ = lax.broadcasted_iota(jnp.int32, (GROUP_W, GROUP_W), 0) // HEAD_DIM
    hj_ = lax.broadcasted_iota(jnp.int32, (GROUP_W, GROUP_W), 1) // HEAD_DIM
    ms = _dot_f32_by_exact(r * r, jnp.where(hi_ == hj_, 1.0, 0.0).astype(BF16)) * (1.0 / HEAD_DIM)
    o_ref[0] = r * lax.rsqrt(ms + NORM_EPS) * g_ref[...]


def _dsa_kernel(qd_ref, kd4_ref, vdwt_ref, qi_ref, kihi_ref, kilo_ref, wi_ref, g_ref, o_ref, *, kls, n_sel):
    blocks_per_step = DSA_KEY_STEP // DSA_BLOCK
    for j, kl in enumerate(kls):
        @pl.when(pl.program_id(1) // blocks_per_step == j)
        def _():
            _dsa_block_t(qd_ref, kd4_ref, vdwt_ref, qi_ref, kihi_ref, kilo_ref, wi_ref, g_ref, o_ref,
                         kl=kl, n_sel=n_sel)


DSA_PREP_ROWS = 256
D_Q, D_KV, D_QI, D_KW = 0, GROUP_W, GROUP_W + LANE, GROUP_W + 2 * LANE


def _swap_halves(x, half):
    n = x.shape[1]
    lane = lax.broadcasted_iota(jnp.int32, x.shape, 1)
    return jnp.where(lane % (2 * half) < half, pltpu.roll(x, n - half, axis=1), pltpu.roll(x, half, axis=1))


def _dsa_prep_kernel(p_ref, cq_ref, sq_ref, ci_ref, si_ref, gq_ref, gk_ref,
                     qd_ref, kd4_ref, vdwt_ref, qi_ref, kihi_ref, kilo_ref, wi_ref):
    GW = GROUP_W
    x = p_ref[0]
    hi_ = lax.broadcasted_iota(jnp.int32, (GW, GW), 0) // HEAD_DIM
    hj_ = lax.broadcasted_iota(jnp.int32, (GW, GW), 1) // HEAD_DIM
    ones_bd = jnp.where(hi_ == hj_, 1.0, 0.0).astype(BF16)
    q = x[:, D_Q:D_Q + GW]
    q = q * lax.rsqrt(_dot_f32_by_exact(q * q, ones_bd) * (1.0 / HEAD_DIM) + NORM_EPS) * gq_ref[...]
    qd_ref[0] = q * cq_ref[...] + _swap_halves(q, HEAD_DIM // 2) * sq_ref[...]
    kv = x[:, D_KV:D_KV + LANE]
    lane = lax.broadcasted_iota(jnp.int32, kv.shape, 1)
    is_k = lane < HEAD_DIM
    ms = jnp.sum(jnp.where(is_k, kv * kv, 0.0), axis=1, keepdims=True) * (1.0 / HEAD_DIM)
    kn = kv * lax.rsqrt(ms + NORM_EPS) * gk_ref[...]
    kr = kn * cq_ref[:, :LANE] + _swap_halves(kn, HEAD_DIM // 2) * sq_ref[:, :LANE]
    k2 = jnp.where(is_k, kr, pltpu.roll(kr, HEAD_DIM, axis=1))
    kd4_ref[0] = jnp.concatenate([k2] * (GW // LANE), axis=1).astype(BF16)
    v2 = jnp.where(is_k, pltpu.roll(kv, HEAD_DIM, axis=1), kv)
    v2t = jnp.transpose(v2)
    vdwt_ref[0] = jnp.concatenate([v2t] * (GW // LANE), axis=0).astype(BF16)
    qi = x[:, D_QI:D_QI + LANE]
    qi_ref[0] = qi * ci_ref[...] + _swap_halves(qi, IDX_DIM // 2) * si_ref[...]
    kw = x[:, D_KW:D_KW + LANE]
    kir = kw * ci_ref[...] + _swap_halves(kw, IDX_DIM // 2) * si_ref[...]
    ki1 = jnp.where(lane < IDX_DIM, kir, 0.0)
    ki2 = ki1 + pltpu.roll(ki1, IDX_DIM, axis=1)
    ki4 = ki2 + pltpu.roll(ki2, 2 * IDX_DIM, axis=1)
    kihi_ref[0], kilo_ref[0] = _split_bf16(ki4)
    wi_ref[0] = pltpu.roll(kw, LANE - IDX_DIM, axis=1) * (IDX_HEADS ** -0.5 * IDX_DIM ** -0.5)


def _rope_tables(S, dim, width):
    half = dim // 2
    inv = ROPE_THETA ** (-jnp.arange(half, dtype=F32) / half)
    ang = jnp.arange(S, dtype=F32)[:, None] * inv[None, :]
    cos = jnp.tile(jnp.cos(ang), (1, width // half))
    sin = jnp.tile(jnp.concatenate([-jnp.sin(ang), jnp.sin(ang)], axis=1), (1, width // dim))
    return cos, sin


def _dsa_prep(p, qn_g, kn_g):
    B, S, _ = p.shape
    GW, R = GROUP_W, DSA_PREP_ROWS
    cq, sq = _rope_tables(S, HEAD_DIM, GW)
    ci, si = _rope_tables(S, IDX_DIM, LANE)
    gq = jnp.tile(qn_g, N_HEADS)[None]
    gk = jnp.pad(kn_g, (0, LANE - HEAD_DIM))[None]
    rows = lambda w: pl.BlockSpec((1, R, w), lambda b, i: (b, i, 0))
    tab = lambda w: pl.BlockSpec((R, w), lambda b, i: (i, 0))
    cols = lambda r: pl.BlockSpec((1, r, R), lambda b, i: (b, 0, i))
    vec = lambda w: pl.BlockSpec((1, w), lambda b, i: (0, 0))
    return pl.pallas_call(
        _dsa_prep_kernel,
        grid=(B, S // R),
        in_specs=[rows(D_PAD), tab(GW), tab(GW), tab(LANE), tab(LANE), vec(GW), vec(LANE)],
        out_specs=[rows(GW), rows(GW), cols(GW), rows(LANE), rows(LANE), rows(LANE), rows(LANE)],
        out_shape=[jax.ShapeDtypeStruct((B, S, GW), F32), jax.ShapeDtypeStruct((B, S, GW), BF16),
                   jax.ShapeDtypeStruct((B, GW, S), BF16), jax.ShapeDtypeStruct((B, S, LANE), F32),
                   jax.ShapeDtypeStruct((B, S, LANE), BF16), jax.ShapeDtypeStruct((B, S, LANE), BF16),
                   jax.ShapeDtypeStruct((B, S, LANE), F32)],
        compiler_params=pltpu.CompilerParams(dimension_semantics=("parallel", "parallel"),
                                             vmem_limit_bytes=VMEM_LIMIT),
        name="dsa_prep",
    )(p, cq, sq, ci, si, gq, gk)


def _dsa_attn_norm(qd, kd4, vdwt, qi, ki_hi, ki_lo, wi, g):
    B, S, _ = qd.shape
    n_sel = min(TOPK_MAX, S // 4)
    assert S % DSA_KEY_STEP == 0 and n_sel <= DSA_KEY_STEP
    kls = tuple(range(DSA_KEY_STEP, S + 1, DSA_KEY_STEP))
    blk = lambda w: pl.BlockSpec((1, DSA_BLOCK, w), lambda b, i: (b, i, 0))
    per_b = lambda r, c: pl.BlockSpec((1, r, c), lambda b, i: (b, 0, 0))
    return pl.pallas_call(
        functools.partial(_dsa_kernel, kls=kls, n_sel=n_sel),
        grid=(B, S // DSA_BLOCK),
        in_specs=[blk(GROUP_W), per_b(S, GROUP_W), per_b(GROUP_W, S), blk(IDX_HEADS * IDX_DIM),
                  per_b(S, LANE), per_b(S, LANE), blk(LANE), pl.BlockSpec((1, GROUP_W), lambda b, i: (0, 0))],
        out_specs=blk(GROUP_W),
        out_shape=jax.ShapeDtypeStruct((B, S, GROUP_W), F32),
        compiler_params=pltpu.CompilerParams(dimension_semantics=("parallel", "parallel"),
                                             vmem_limit_bytes=VMEM_LIMIT),
        name="dsa_attn",
    )(qd, kd4, vdwt, qi, ki_hi, ki_lo, wi, g[None])


RWKV_CHUNK = 64
RWKV_LOW = RWKV_W_RANK + RWKV_A_RANK + RWKV_G_RANK
RWKV_GROUP = N_HEADS
RWKV_BATCH = 8


def _dot(a, b):
    return jnp.dot(a, b, preferred_element_type=F32)


def _dot_nt(a, b):
    return lax.dot_general(a, b, (((1,), (1,)), ((), ())), preferred_element_type=F32)


def _dot_tn(a, b):
    return lax.dot_general(a, b, (((0,), (0,)), ((), ())), preferred_element_type=F32)


def _split3_bf16(t):
    p1 = t.astype(BF16)
    r1 = t - p1.astype(F32)
    p2 = r1.astype(BF16)
    p3 = (r1 - p2.astype(F32)).astype(BF16)
    return p1, p2, p3


def _dot_f32_by_exact(a, b_exact):
    m = a.shape[0]
    r = _dot(jnp.concatenate(_split3_bf16(a), axis=0), b_exact)
    return r[:m] + r[m:2 * m] + r[2 * m:]


def _dot_exact_by_f32(a_exact, b):
    n = b.shape[1]
    r = _dot(a_exact, jnp.concatenate(_split3_bf16(b), axis=1))
    return r[:, :n] + r[:, n:2 * n] + r[:, 2 * n:]


def _dot3(a, b_hi, b_lo):
    a_hi, a_lo = _split_bf16(a)
    return _dot(jnp.concatenate([a_hi, a_lo, a_hi], axis=1), jnp.concatenate([b_hi, b_hi, b_lo], axis=0))


def _softplus(z):
    return jnp.maximum(z, 0.0) + jnp.log(1.0 + jnp.exp(-jnp.abs(z)))


def _rwkv_kernel(p_ref, mu_ref, vec_ref, lhi_ref, llo_ref, o_ref, state_ref, prev_ref):
    @pl.when(pl.program_id(1) == 0)
    def _():
        state_ref[...] = jnp.zeros_like(state_ref)
        prev_ref[...] = jnp.zeros_like(prev_ref)

    chains = [_rwkv_chunk(p_ref.at[n], mu_ref, vec_ref, lhi_ref, llo_ref, o_ref.at[n], state_ref.at[n],
                          prev_ref.at[n]) for n in range(RWKV_BATCH)]
    for _ in itertools.zip_longest(*chains):
        pass


def _rwkv_chunk(p_ref, mu_ref, vec_ref, lhi_ref, llo_ref, o_ref, state_ref, prev_ref):
    L, GW = RWKV_CHUNK, GROUP_W
    p = p_ref[...]
    row = lax.broadcasted_iota(jnp.int32, p.shape, 0)
    prev = jnp.where(row == 0, prev_ref[...], pltpu.roll(p, 1, axis=0))
    prev_ref[...] = p[L - 1:L, :]
    ps = p + (prev - p) * mu_ref[...]
    r, k, v = ps[:, :GW], ps[:, GW:2 * GW], ps[:, 2 * GW:3 * GW]
    low = ps[:, 3 * GW:]
    lane_low = lax.broadcasted_iota(jnp.int32, low.shape, 1)
    low = jnp.where(lane_low < RWKV_W_RANK, jnp.tanh(low),
                    jnp.where(lane_low < RWKV_W_RANK + RWKV_A_RANK, low, jax.nn.sigmoid(low)))
    up = _dot3(low, lhi_ref[...], llo_ref[...])
    w0, a0, k_k, k_a = vec_ref[0:1, :], vec_ref[1:2, :], vec_ref[2:3, :], vec_ref[3:4, :]
    r_k, ln_g, ln_b = vec_ref[4:5, :], vec_ref[5:6, :], vec_ref[6:7, :]
    logw = -jnp.exp(-_softplus(-(w0 + up[:, :GW])) - 0.5)
    rate = jax.nn.sigmoid(a0 + up[:, GW:2 * GW])
    gate = up[:, 2 * GW:]

    ri = lax.broadcasted_iota(jnp.int32, (GW, GW), 0)
    ci = lax.broadcasted_iota(jnp.int32, (GW, GW), 1)
    ones_bd = jnp.where((ri // HEAD_DIM) == (ci // HEAD_DIM), 1.0, 0.0).astype(BF16)

    kk = k * k_k
    k = k * (1.0 + (rate - 1.0) * k_a)
    seg = _dot_f32_by_exact(jnp.concatenate([kk * kk, r * k * r_k], axis=0), ones_bd)
    kk = kk / jnp.maximum(jnp.sqrt(seg[:L]), 1e-12)

    ti = lax.broadcasted_iota(jnp.int32, (L, L), 0)
    tj = lax.broadcasted_iota(jnp.int32, (L, L), 1)
    lc = _dot_exact_by_f32(jnp.where(tj <= ti, 1.0, 0.0).astype(BF16), logw)
    lc_last = lc[L - 1:L, :]
    dec_in = jnp.exp(lc)
    dec_out = jnp.exp(-lc)
    a_t = -kk * jnp.exp(lc - logw)
    b_t = kk * rate * dec_out
    k_t = k * dec_out
    r_t = r * dec_in
    to_end = jnp.exp(lc_last)

    SG = RWKV_GROUP * L
    gi = lax.broadcasted_iota(jnp.int32, (SG, SG), 0)
    gj = lax.broadcasted_iota(jnp.int32, (SG, SG), 1)
    g_same = (gi // L) == (gj // L)
    strict = g_same & ((gj % L) < (gi % L))
    incl = g_same & ((gj % L) <= (gi % L))
    eye = jnp.where(gi == gj, 1.0, 0.0)
    stack = lambda t: jnp.concatenate([t] * RWKV_GROUP, axis=0)
    bd = lambda t: jnp.where(g_same, stack(t), 0.0).astype(BF16)
    n_doublings = RWKV_CHUNK.bit_length() - 2
    ys = []
    for g in range(N_HEADS // RWKV_GROUP):
        cols = slice(g * SG, (g + 1) * SG)
        a_bd, r_bd, v_bd = bd(a_t[:, cols]), bd(r_t[:, cols]), bd(v[:, cols])
        m = _dot_nt(jnp.concatenate([a_bd, r_bd], axis=0),
                    jnp.concatenate([stack(b_t[:, cols]), stack(k_t[:, cols])], axis=0).astype(BF16))
        yield
        m_ab = jnp.where(strict, m[:SG, :SG], 0.0)
        m_ak = jnp.where(strict, m[:SG, SG:], 0.0).astype(BF16)
        m_rb = jnp.where(incl, m[SG:, :SG], 0.0).astype(BF16)
        m_rk = jnp.where(incl, m[SG:, SG:], 0.0).astype(BF16)

        inv = eye + m_ab
        pw_b = m_ab.astype(BF16)
        sq = _dot(pw_b, pw_b)
        yield
        pw_b = sq.astype(BF16)
        for s in range(n_doublings - 1):
            both = _dot(jnp.concatenate([inv.astype(BF16), pw_b], axis=0), pw_b)
            yield
            inv = inv + both[:SG]
            pw_b = both[SG:].astype(BF16)
        last = _dot(inv.astype(BF16), pw_b)
        t0 = state_ref[g]
        t0_b = t0.astype(BF16)
        rhs = _dot(jnp.concatenate([a_bd, m_ak], axis=1), jnp.concatenate([t0_b, v_bd], axis=0))
        yield
        inv_b = (inv + last).astype(BF16)
        u = _dot(inv_b, rhs.astype(BF16))
        yield
        u = u.astype(BF16)
        y_bd = _dot(jnp.concatenate([r_bd, m_rb, m_rk], axis=1),
                    jnp.concatenate([t0_b, u, v_bd], axis=0))
        end_g = to_end[:, cols]
        to_end_col = jnp.sum(jnp.where(gi == gj, jnp.broadcast_to(end_g, (SG, SG)), 0.0), axis=1, keepdims=True)
        carried = _dot_tn(jnp.concatenate([bd(b_t[:, cols] * end_g), bd(k_t[:, cols] * end_g)], axis=0),
                          jnp.concatenate([u, v_bd], axis=0))
        yield
        ys.append(sum(y_bd[h * L:(h + 1) * L, :] for h in range(RWKV_GROUP)))
        state_ref[g] = to_end_col * t0 + carried
    y = jnp.concatenate(ys, axis=1)

    inv_d = 1.0 / HEAD_DIM
    mean = _dot_f32_by_exact(y, ones_bd) * inv_d
    yield
    yc = y - mean
    var = _dot_f32_by_exact(yc * yc, ones_bd) * inv_d
    yield
    yn = yc * lax.rsqrt(var + RWKV_GN_EPS) * ln_g + ln_b
    o_ref[...] = (yn + seg[L:] * v) * gate


def _rwkv7_time_mix(p, mu, w0, w2, a0, a2, g2, k_k, k_a, r_k, ln_g, ln_b):
    B, S, _ = p.shape
    GW = GROUP_W
    assert S % RWKV_CHUNK == 0 and RWKV_CHUNK == HEAD_DIM and B % RWKV_BATCH == 0
    low_w = jnp.zeros((RWKV_LOW, 3 * GW), F32)
    low_w = low_w.at[:RWKV_W_RANK, :GW].set(w2)
    low_w = low_w.at[RWKV_W_RANK:RWKV_W_RANK + RWKV_A_RANK, GW:2 * GW].set(a2)
    low_w = low_w.at[RWKV_W_RANK + RWKV_A_RANK:, 2 * GW:].set(g2)
    l_hi, l_lo = _split_bf16(low_w)
    vecs = jnp.stack([w0, a0, k_k, k_a, r_k, ln_g, ln_b, jnp.zeros_like(w0)], 0)
    full = lambda a: pl.BlockSpec(a.shape, lambda b, c: (0,) * a.ndim)
    mu2 = mu[None]
    return pl.pallas_call(
        _rwkv_kernel,
        grid=(B // RWKV_BATCH, S // RWKV_CHUNK),
        in_specs=[pl.BlockSpec((RWKV_BATCH, RWKV_CHUNK, A_PAD), lambda b, c: (b, c, 0)),
                  full(mu2), full(vecs), full(l_hi), full(l_lo)],
        out_specs=pl.BlockSpec((RWKV_BATCH, RWKV_CHUNK, GW), lambda b, c: (b, c, 0)),
        out_shape=jax.ShapeDtypeStruct((B, S, GW), F32),
        scratch_shapes=[pltpu.VMEM((RWKV_BATCH, N_HEADS // RWKV_GROUP, RWKV_GROUP * RWKV_CHUNK,
                                    RWKV_GROUP * HEAD_DIM), F32),
                        pltpu.VMEM((RWKV_BATCH, 1, A_PAD), F32)],
        compiler_params=pltpu.CompilerParams(dimension_semantics=("parallel", "arbitrary"),
                                             vmem_limit_bytes=VMEM_LIMIT),
        name="rwkv7",
    )(p, mu2, vecs, l_hi, l_lo)


SB_UNROLL = 4


def _sb_kernel(q_ref, k_ref, v_ref, g_ref, o_ref, kbd_ref, vbd_ref):
    i = pl.program_id(1)
    T, GW, H = SB_BLOCK, GROUP_W, N_HEADS
    lane_h = lax.broadcasted_iota(jnp.int32, (T, GW), 1) // HEAD_DIM
    k_new, v_new = k_ref[0], v_ref[0]
    for h in range(H):
        kbd_ref[i, h * T:(h + 1) * T, :] = jnp.where(lane_h == h, k_new, 0.0).astype(BF16)
        vbd_ref[i, h * T:(h + 1) * T, :] = jnp.where(lane_h == h, v_new, 0.0).astype(BF16)

    q = q_ref[0].astype(BF16)
    si = lax.broadcasted_iota(jnp.int32, (T, 2 * T), 0)
    sj = lax.broadcasted_iota(jnp.int32, (T, 2 * T), 1)
    later_and_all = jnp.where((si > sj) | (sj >= T), 1.0, 0.0).astype(BF16)
    qrow = lax.broadcasted_iota(jnp.int32, (T, H * T), 0)
    kcol = lax.broadcasted_iota(jnp.int32, (T, H * T), 1) % T
    causal = kcol < qrow

    def key_block_stages(j, box, diagonal):
        z = _dot_nt(q, kbd_ref[j]) * HEAD_DIM ** -0.5
        yield
        soft = jnp.log(1.0 + jnp.exp(-jnp.abs(z)))
        log1m = -(jnp.maximum(z, 0.0) + soft)
        log_sig = jnp.minimum(z, 0.0) - soft
        log1m_in = (jnp.where(causal, log1m, 0.0) if diagonal else log1m).astype(BF16)
        sums = _dot(jnp.concatenate([log1m_in[:, h * T:(h + 1) * T] for h in range(H)], axis=0), later_and_all)
        yield
        suffix = jnp.concatenate([sums[h * T:(h + 1) * T, :T] for h in range(H)], axis=1)
        total = jnp.concatenate([sums[h * T:(h + 1) * T, T:] for h in range(H)], axis=1)
        att = jnp.exp(log_sig + suffix + box["carry"])
        if diagonal:
            att = jnp.where(causal, att, 0.0)
        box["carry"] = box["carry"] + total
        pv = _dot(att.astype(BF16), vbd_ref[j])
        yield
        box["acc"] = box["acc"] + pv

    def key_blocks(js, state, diagonal=False):
        box = {"carry": state[0], "acc": state[1]}
        for _ in itertools.zip_longest(*[key_block_stages(j, box, diagonal) for j in js]):
            pass
        return box["carry"], box["acc"]

    state = key_blocks([i], (jnp.zeros((T, H * T), F32), jnp.zeros((T, GW), F32)), True)
    rem = i % SB_UNROLL
    state = lax.fori_loop(0, rem, lambda it, st: key_blocks([i - 1 - it], st), state)
    top = i - 1 - rem
    _, y = lax.fori_loop(0, i // SB_UNROLL,
                         lambda it, st: key_blocks([top - SB_UNROLL * it - n for n in range(SB_UNROLL)], st), state)

    hi_ = lax.broadcasted_iota(jnp.int32, (GW, GW), 0) // HEAD_DIM
    hj_ = lax.broadcasted_iota(jnp.int32, (GW, GW), 1) // HEAD_DIM
    ones_bd = jnp.where(hi_ == hj_, 1.0, 0.0).astype(BF16)
    ms = _dot_f32_by_exact(y * y, ones_bd) * (1.0 / HEAD_DIM)
    o_ref[0] = y * lax.rsqrt(ms + NORM_EPS) * g_ref[...]


def _stick_breaking_norm(p, g):
    B, S, _ = p.shape
    GW = GROUP_W
    assert S % SB_BLOCK == 0
    col = lambda n: pl.BlockSpec((1, SB_BLOCK, GW), lambda b, i: (b, i, n))
    blk = col(0)
    q = k = v = p
    stacked = pltpu.VMEM((S // SB_BLOCK, N_HEADS * SB_BLOCK, GW), BF16)
    return pl.pallas_call(
        _sb_kernel,
        grid=(B, S // SB_BLOCK),
        in_specs=[col(0), col(1), col(2), pl.BlockSpec((1, GW), lambda b, i: (0, 0))],
        out_specs=blk,
        out_shape=jax.ShapeDtypeStruct((B, S, GW), F32),
        scratch_shapes=[stacked, stacked],
        compiler_params=pltpu.CompilerParams(dimension_semantics=("parallel", "arbitrary"),
                                             vmem_limit_bytes=VMEM_LIMIT),
        name="stick_breaking",
    )(q, k, v, g[None])


ML_HALO = 8
ML_GROUP = N_HEADS
ML_BATCH = 4


def _dot_nt_exact_by_f32(a_exact, b):
    return _dot_nt(jnp.concatenate([a_exact] * 3, axis=1), jnp.concatenate(_split3_bf16(b), axis=1))


def _mlstm_kernel(p_ref, cw_ref, cb_ref, gb_ref, g_ref, o_ref, ext_ref, ct_ref, n_ref, m_ref):
    @pl.when(pl.program_id(1) == 0)
    def _():
        ext_ref[...] = jnp.zeros_like(ext_ref)
        ct_ref[...] = jnp.zeros_like(ct_ref)
        n_ref[...] = jnp.zeros_like(n_ref)
        m_ref[...] = jnp.zeros_like(m_ref)

    chains = [_mlstm_chunk(p_ref.at[n], cw_ref, cb_ref, gb_ref, g_ref, o_ref.at[n], ext_ref.at[n], ct_ref.at[n],
                           n_ref.at[n], m_ref.at[n]) for n in range(ML_BATCH)]
    for _ in itertools.zip_longest(*chains):
        pass


def _mlstm_chunk(p_ref, cw_ref, cb_ref, gb_ref, g_ref, o_ref, ext_ref, ct_ref, n_ref, m_ref):
    L, GW, H = ML_CHUNK, GROUP_W, N_HEADS
    x = p_ref[...]
    ext_ref[ML_HALO:, :] = x[:, :2 * GW]
    conv = cb_ref[...]
    for j in range(ML_CONV):
        conv = conv + cw_ref[j:j + 1, :] * ext_ref[pl.ds(ML_HALO - (ML_CONV - 1) + j, L), :]
    ext_ref[:ML_HALO, :] = x[L - ML_HALO:, :2 * GW]
    qk = conv * jax.nn.sigmoid(conv)
    q, k = qk[:, :GW], qk[:, GW:] * HEAD_DIM ** -0.5
    v, o = x[:, 2 * GW:3 * GW], x[:, 3 * GW:4 * GW]

    gates = x[:, 4 * GW:]
    gi = lax.broadcasted_iota(jnp.int32, (LANE, 2 * GW), 0)
    gj = lax.broadcasted_iota(jnp.int32, (LANE, 2 * GW), 1)
    expand = jnp.where(gi == (gj % GW) // HEAD_DIM + H * (gj // GW), 1.0, 0.0).astype(BF16)
    graw = _dot_f32_by_exact(gates, expand) + gb_ref[...]
    yield
    capped = GATE_CAP * jnp.tanh(graw * (1.0 / GATE_CAP))
    log_i = capped[:, :GW]
    cf = capped[:, GW:]
    log_f = jnp.minimum(cf, 0.0) - jnp.log(1.0 + jnp.exp(-jnp.abs(cf)))

    ti = lax.broadcasted_iota(jnp.int32, (L, L), 0)
    tj = lax.broadcasted_iota(jnp.int32, (L, L), 1)
    bf = _dot_exact_by_f32(jnp.where(tj <= ti, 1.0, 0.0).astype(BF16), log_f)
    yield
    b_last = bf[L - 1:L, :]
    m_row, n_row = m_ref[...], n_ref[...]
    dec = b_last - bf + log_i
    m_new = jnp.maximum(b_last + m_row, jnp.max(dec, axis=0, keepdims=True))
    kw = k * jnp.exp(dec - m_new)
    s_old = jnp.exp(b_last + m_row - m_new)
    n_ref[...] = s_old * n_row + jnp.sum(kw, axis=0, keepdims=True)
    m_ref[...] = m_new
    g_in = bf + m_row
    li_b = log_i - bf

    SG = ML_GROUP * L
    ri = lax.broadcasted_iota(jnp.int32, (SG, SG), 0)
    ci = lax.broadcasted_iota(jnp.int32, (SG, SG), 1)
    same_head = (ri // L) == (ci // HEAD_DIM)
    first_lane = ci == (ri // L) * HEAD_DIM
    sel_first = jnp.where(first_lane, 1.0, 0.0).astype(BF16)
    stack = lambda t: jnp.concatenate([t] * ML_GROUP, axis=0)
    pick = lambda t: jnp.sum(jnp.where(first_lane, t, 0.0), axis=1, keepdims=True)
    rt = lax.broadcasted_iota(jnp.int32, (SG, L), 0) % L
    rs = lax.broadcasted_iota(jnp.int32, (SG, L), 1)
    hs = []
    for g in range(H // ML_GROUP):
        cols = slice(g * SG, (g + 1) * SG)
        ct = ct_ref[g]
        qs = jnp.where(same_head, stack(q[:, cols]), 0.0)
        qs_b = qs.astype(BF16)
        v_b = v[:, cols].astype(BF16)
        b_col = pick(stack(bf[:, cols]))
        g_col = pick(stack(g_in[:, cols]))
        row_part = _dot_nt_exact_by_f32(sel_first, li_b[:, cols])
        qk = _dot_nt(qs_b, k[:, cols].astype(BF16))
        inter = _dot(qs_b, ct.astype(BF16))
        carried = _dot_tn(kw[:, cols].astype(BF16), v_b)
        yield
        dmat = jnp.where(rs <= rt, b_col + row_part, -jnp.inf)
        m_t = jnp.maximum(g_col, jnp.max(dmat, axis=1, keepdims=True))
        s_inter = jnp.exp(g_col - m_t)
        sqk = qk * jnp.exp(dmat - m_t)
        intra = _dot(sqk.astype(BF16), v_b)
        yield
        num = s_inter * inter + jnp.where(same_head, intra, 0.0)
        den = (s_inter * jnp.sum(qs * n_row[:, cols], axis=1, keepdims=True)
               + jnp.sum(sqk, axis=1, keepdims=True))
        hst = num / jnp.maximum(jnp.abs(den), jnp.exp(-m_t))
        hs.append(sum(hst[n * L:(n + 1) * L, :] for n in range(ML_GROUP)))
        ct_ref[g] = s_old[:, cols] * ct + jnp.where(same_head, carried, 0.0)
    h = jnp.concatenate(hs, axis=1)

    hi_ = lax.broadcasted_iota(jnp.int32, (GW, GW), 0) // HEAD_DIM
    hj_ = lax.broadcasted_iota(jnp.int32, (GW, GW), 1) // HEAD_DIM
    ones_bd = jnp.where(hi_ == hj_, 1.0, 0.0).astype(BF16)
    ms = _dot_f32_by_exact(h * h, ones_bd) * (1.0 / HEAD_DIM)
    yield
    o_ref[...] = jax.nn.sigmoid(o) * (h * lax.rsqrt(ms + NORM_EPS) * g_ref[...])


def _mlstm_mix(p, conv_w, conv_b, ig_b, fg_b, norm_g):
    B, S, _ = p.shape
    GW = GROUP_W
    assert S % ML_CHUNK == 0 and ML_CONV - 1 <= ML_HALO <= ML_CHUNK and B % ML_BATCH == 0
    gate_b = jnp.concatenate([jnp.repeat(ig_b, HEAD_DIM), jnp.repeat(fg_b, HEAD_DIM)])[None]
    full = lambda a: pl.BlockSpec(a.shape, lambda b, c: (0,) * a.ndim)
    cb2, g2 = conv_b[None], norm_g[None]
    return pl.pallas_call(
        _mlstm_kernel,
        grid=(B // ML_BATCH, S // ML_CHUNK),
        in_specs=[pl.BlockSpec((ML_BATCH, ML_CHUNK, C_PAD), lambda b, c: (b, c, 0)),
                  full(conv_w), full(cb2), full(gate_b), full(g2)],
        out_specs=pl.BlockSpec((ML_BATCH, ML_CHUNK, GW), lambda b, c: (b, c, 0)),
        out_shape=jax.ShapeDtypeStruct((B, S, GW), F32),
        scratch_shapes=[pltpu.VMEM((ML_BATCH, ML_HALO + ML_CHUNK, 2 * GW), F32),
                        pltpu.VMEM((ML_BATCH, N_HEADS // ML_GROUP, ML_GROUP * ML_CHUNK, ML_GROUP * HEAD_DIM), F32),
                        pltpu.VMEM((ML_BATCH, 1, GW), F32), pltpu.VMEM((ML_BATCH, 1, GW), F32)],
        compiler_params=pltpu.CompilerParams(dimension_semantics=("parallel", "arbitrary"),
                                             vmem_limit_bytes=VMEM_LIMIT),
        name="mlstm",
    )(p, conv_w, cb2, gate_b, g2)


def _rms_norm(x, g):
    xf = x.astype(F32)
    y = xf * lax.rsqrt(jnp.mean(xf * xf, -1, keepdims=True) + NORM_EPS)
    return (y * g.astype(F32)).astype(x.dtype)


def _rope(x, pos):
    half = x.shape[-1] // 2
    inv = ROPE_THETA ** (-jnp.arange(half, dtype=F32) / half)
    ang = pos.astype(F32)[:, None] * inv[None, :]
    cos = jnp.cos(ang)[None, :, None, :]
    sin = jnp.sin(ang)[None, :, None, :]
    xf = x.astype(F32)
    x1, x2 = xf[..., :half], xf[..., half:]
    return jnp.concatenate([x1 * cos - x2 * sin, x2 * cos + x1 * sin], -1).astype(x.dtype)


def kernel(x, c, ada_w, ada_b, norm1_g, norm2_g, w_in, rk_mu, rk_w0, rk_w2, rk_a0, rk_a2, rk_g2, rk_kk, rk_ka, rk_rk, rk_ln_g, rk_ln_b, sb_norm_g, ml_conv_w, ml_conv_b, ml_ig_b, ml_fg_b, ml_norm_g, ds_qn_g, ds_kn_g, ds_out_g, w_out, moe_wg, moe_bg, moe_we, moe_be, moe_w1, moe_w3, moe_w2):
    B, S, D = x.shape
    H, d = N_HEADS, HEAD_DIM
    depth = ada_w.shape[0]
    pos = jnp.arange(S)
    c_act = jax.nn.silu(c)
    for l in range(depth):
        mod = (c_act @ ada_w[l] + ada_b[l])[:, None, :]
        sh1, sc1, gt1, sh2, sc2, gt2 = jnp.split(mod, 6, axis=-1)

        pA, pB, pC, pD = _in_proj(x, sc1, sh1, norm1_g[l][None], _pad_w_in(w_in[l]))

        yA = _rwkv7_time_mix(pA, rk_mu[l], rk_w0[l], rk_w2[l], rk_a0[l], rk_a2[l], rk_g2[l],
                             rk_kk[l], rk_ka[l], rk_rk[l], rk_ln_g[l], rk_ln_b[l])

        yB = _stick_breaking_norm(pB, sb_norm_g[l])

        yC = _mlstm_mix(pC, ml_conv_w[l], ml_conv_b[l], ml_ig_b[l], ml_fg_b[l], ml_norm_g[l])

        yD = _dsa_attn_norm(*_dsa_prep(pD, ds_qn_g[l], ds_kn_g[l]), ds_out_g[l])

        router = jnp.pad(jnp.concatenate([moe_wg[l], moe_we[l]], 1),
                         ((0, 0), (0, ROUTER_PAD - N_GROUPS - N_EXPERTS)))
        r_hi, r_lo = _split_bf16(router)
        r_b = jnp.pad(jnp.concatenate([moe_bg[l], moe_be[l]]), (0, ROUTER_PAD - N_GROUPS - N_EXPERTS))[None]
        x1, h2, route = _out_proj((yA, yB, yC, yD), x, gt1, sc2, sh2, norm2_g[l][None],
                                  w_out[l].astype(BF16), r_hi, r_lo, r_b)

        moe = _hier_moe(h2.reshape(B * S, D), route.reshape(B * S, ROUTER_PAD),
                        moe_w1, moe_w3, moe_w2, l)
        x = x1 + gt2 * moe.reshape(B, S, D)
    return x
```

```python
import functools
import itertools

import jax
import jax.numpy as jnp
import numpy as np
from jax import lax
from jax.experimental import pallas as pl
from jax.experimental.pallas import tpu as pltpu

F32 = jnp.float32
BF16 = jnp.bfloat16

D_MODEL = 1024
N_MIXERS = 4
GROUP_W = D_MODEL // N_MIXERS
HEAD_DIM = 64
N_HEADS = GROUP_W // HEAD_DIM
NORM_EPS = 1e-6
RWKV_W_RANK = 32
RWKV_A_RANK = 32
RWKV_G_RANK = 64
RWKV_GN_EPS = 64e-5
SB_BLOCK = 128
ML_CHUNK = 64
ML_CONV = 4
GATE_CAP = 15.0
DSA_BLOCK = 128
IDX_HEADS = 4
IDX_DIM = 32
TOPK_MAX = 256
ROPE_THETA = 10000.0
N_GROUPS = 4
EXP_PER_GROUP = 8
N_EXPERTS = N_GROUPS * EXP_PER_GROUP
EXPERT_FF = D_MODEL // 2
TOP_IN_GROUP = 2

A_SIZES = (GROUP_W, GROUP_W, GROUP_W, RWKV_W_RANK, RWKV_A_RANK, RWKV_G_RANK)
B_SIZES = (GROUP_W, GROUP_W, GROUP_W)
C_SIZES = (GROUP_W, GROUP_W, GROUP_W, GROUP_W, N_HEADS, N_HEADS)
D_SIZES = (GROUP_W, HEAD_DIM, HEAD_DIM, IDX_HEADS * IDX_DIM, IDX_DIM, IDX_HEADS)
A_COLS = sum(A_SIZES)
B_COLS = sum(B_SIZES)
C_COLS = sum(C_SIZES)
D_COLS = sum(D_SIZES)

LANE = 128
A_PAD = 896
B_PAD = 768
C_PAD = 1152
D_PAD = 640
P_PAD = A_PAD + B_PAD + C_PAD + D_PAD
ROUTER_PAD = LANE

IN_ROWS = 256
OUT_ROWS = 512
MOE_ROWS = 512
VMEM_LIMIT = 48 * 1024 * 1024


def _split_cols(t, sizes):
    return jnp.split(t, [int(i) for i in np.cumsum(sizes)[:-1]], axis=-1)


def _in_proj_kernel(x_ref, sc_ref, sh_ref, g_ref, w_ref, oa_ref, ob_ref, oc_ref, od_ref):
    x = x_ref[0]
    y = x * lax.rsqrt(jnp.mean(x * x, -1, keepdims=True) + NORM_EPS) * g_ref[...]
    h = y * (1.0 + sc_ref[0]) + sh_ref[0]
    p = jnp.dot(h.astype(BF16), w_ref[...], preferred_element_type=F32)
    oa_ref[0] = p[:, :A_PAD]
    ob_ref[0] = p[:, A_PAD:A_PAD + B_PAD]
    oc_ref[0] = p[:, A_PAD + B_PAD:A_PAD + B_PAD + C_PAD]
    od_ref[0] = p[:, A_PAD + B_PAD + C_PAD:]


def _in_proj(x, sc, sh, g, w_pad):
    B, S, D = x.shape
    row = lambda w: pl.BlockSpec((1, IN_ROWS, w), lambda b, i: (b, i, 0))
    vec = pl.BlockSpec((1, 1, D), lambda b, i: (b, 0, 0))
    return pl.pallas_call(
        _in_proj_kernel,
        grid=(B, S // IN_ROWS),
        in_specs=[row(D), vec, vec, pl.BlockSpec((1, D), lambda b, i: (0, 0)),
                  pl.BlockSpec((D, P_PAD), lambda b, i: (0, 0))],
        out_specs=[row(A_PAD), row(B_PAD), row(C_PAD), row(D_PAD)],
        out_shape=[jax.ShapeDtypeStruct((B, S, w), F32) for w in (A_PAD, B_PAD, C_PAD, D_PAD)],
        compiler_params=pltpu.CompilerParams(dimension_semantics=("parallel", "parallel"),
                                             vmem_limit_bytes=VMEM_LIMIT),
        name="in_proj",
    )(x, sc, sh, g, w_pad)


def _pad_w_in(w):
    wa, wb, wc, wd = _split_cols(w, (A_COLS, B_COLS, C_COLS, D_COLS))
    padc = lambda t, n: jnp.pad(t, ((0, 0), (0, n - t.shape[1])))
    return jnp.concatenate([padc(wa, A_PAD), padc(wb, B_PAD), padc(wc, C_PAD), padc(wd, D_PAD)], 1).astype(BF16)


def _split_bf16(t):
    hi = t.astype(BF16)
    lo = (t - hi.astype(F32)).astype(BF16)
    return hi, lo


def _out_proj_kernel(ya_ref, yb_ref, yc_ref, yd_ref, x_ref, gt_ref, sc_ref, sh_ref, g_ref, w_ref,
                     rhi_ref, rlo_ref, rb_ref, x1_ref, h2_ref, route_ref):
    acc = jnp.zeros(x_ref.shape[1:], F32)
    for n, y_ref in enumerate((ya_ref, yb_ref, yc_ref, yd_ref)):
        acc += jnp.dot(y_ref[0].astype(BF16), w_ref[n * GROUP_W:(n + 1) * GROUP_W, :],
                       preferred_element_type=F32)
    x1 = x_ref[0] + gt_ref[0] * acc
    x1_ref[0] = x1
    y = x1 * lax.rsqrt(jnp.mean(x1 * x1, -1, keepdims=True) + NORM_EPS) * g_ref[...]
    h = y * (1.0 + sc_ref[0]) + sh_ref[0]
    hi, lo = _split_bf16(h)
    h2_ref[0] = hi
    lg = (jnp.dot(hi, rhi_ref[...], preferred_element_type=F32)
          + jnp.dot(lo, rhi_ref[...], preferred_element_type=F32)
          + jnp.dot(hi, rlo_ref[...], preferred_element_type=F32)) + rb_ref[...]
    route_ref[0] = _route(lg)


def _route(lg):
    lane = lax.broadcasted_iota(jnp.int32, lg.shape, 1)
    neg = -jnp.inf
    first = lambda hit: jnp.min(jnp.where(hit, lane, ROUTER_PAD), axis=1, keepdims=True)
    is_grp = lane < N_GROUPS
    grp = jnp.where(is_grp, lg, neg)
    g_max = jnp.max(grp, axis=1, keepdims=True)
    g_p = 1.0 / jnp.sum(jnp.where(is_grp, jnp.exp(grp - g_max), 0.0), axis=1, keepdims=True)
    g_idx = first(grp == g_max)
    e_lane = lane - N_GROUPS
    in_group = (e_lane >= 0) & (e_lane < N_EXPERTS) & (e_lane // EXP_PER_GROUP == g_idx)
    e_log = jnp.where(in_group, lg, neg)
    e1_max = jnp.max(e_log, axis=1, keepdims=True)
    e1_lane = first(e_log == e1_max)
    e_log2 = jnp.where(lane == e1_lane, neg, e_log)
    e2_max = jnp.max(e_log2, axis=1, keepdims=True)
    e2_lane = first(e_log2 == e2_max)
    ratio = jnp.exp(e2_max - e1_max)
    gate1 = g_p / (1.0 + ratio)
    gate2 = gate1 * ratio
    out = jnp.where(lane == 0, (e1_lane - N_GROUPS).astype(F32), 0.0)
    out = jnp.where(lane == 1, (e2_lane - N_GROUPS).astype(F32), out)
    out = jnp.where(lane == 2, gate1, out)
    return jnp.where(lane == 3, gate2, out)


def _out_proj(ys, x, gt, sc, sh, g, w_out, r_hi, r_lo, r_b):
    B, S, D = x.shape
    row = lambda w: pl.BlockSpec((1, OUT_ROWS, w), lambda b, i: (b, i, 0))
    vec = pl.BlockSpec((1, 1, D), lambda b, i: (b, 0, 0))
    full = lambda a: pl.BlockSpec(a.shape, lambda b, i: (0,) * a.ndim)
    return pl.pallas_call(
        _out_proj_kernel,
        grid=(B, S // OUT_ROWS),
        in_specs=[row(GROUP_W)] * 4 + [row(D), vec, vec, vec, full(g), full(w_out), full(r_hi), full(r_lo),
                                        full(r_b)],
        out_specs=[row(D), row(D), row(ROUTER_PAD)],
        out_shape=[jax.ShapeDtypeStruct((B, S, D), F32), jax.ShapeDtypeStruct((B, S, D), BF16),
                   jax.ShapeDtypeStruct((B, S, ROUTER_PAD), F32)],
        compiler_params=pltpu.CompilerParams(dimension_semantics=("parallel", "parallel"),
                                             vmem_limit_bytes=VMEM_LIMIT),
        name="out_proj",
    )(*ys, x, gt, sc, sh, g, w_out, r_hi, r_lo, r_b)


def _moe_ffn_kernel(blk_e_ref, x_ref, wt_ref, w1_ref, w3_ref, w2_ref, o_ref, w1b_ref, w3b_ref, w2b_ref):
    i = pl.program_id(0)
    changed = jnp.logical_or(i == 0, blk_e_ref[i] != blk_e_ref[jnp.maximum(i - 1, 0)])

    @pl.when(changed)
    def _():
        w1b_ref[...] = w1_ref[0, 0].astype(BF16)
        w3b_ref[...] = w3_ref[0, 0].astype(BF16)
        w2b_ref[...] = w2_ref[0, 0].astype(BF16)

    xb = x_ref[...]
    a = jnp.dot(xb, w1b_ref[...], preferred_element_type=F32)
    b = jnp.dot(xb, w3b_ref[...], preferred_element_type=F32)
    hmid = (a * jax.nn.sigmoid(a) * b).astype(BF16)
    y = jnp.dot(hmid, w2b_ref[...], preferred_element_type=F32)
    o_ref[...] = (y * wt_ref[...]).astype(o_ref.dtype)


def _moe_ffn(blk_e, xs, wt, w1, w3, w2, layer):
    n_slots, D = xs.shape
    n_blocks = n_slots // MOE_ROWS
    FF = w1.shape[-1]
    return pl.pallas_call(
        _moe_ffn_kernel,
        grid_spec=pltpu.PrefetchScalarGridSpec(
            num_scalar_prefetch=1,
            grid=(n_blocks,),
            in_specs=[pl.BlockSpec((MOE_ROWS, D), lambda i, e: (i, 0)),
                      pl.BlockSpec((MOE_ROWS, 1), lambda i, e: (i, 0)),
                      pl.BlockSpec((1, 1, D, FF), lambda i, e: (layer, e[i], 0, 0)),
                      pl.BlockSpec((1, 1, D, FF), lambda i, e: (layer, e[i], 0, 0)),
                      pl.BlockSpec((1, 1, FF, D), lambda i, e: (layer, e[i], 0, 0))],
            out_specs=pl.BlockSpec((MOE_ROWS, D), lambda i, e: (i, 0)),
            scratch_shapes=[pltpu.VMEM((D, FF), BF16), pltpu.VMEM((D, FF), BF16), pltpu.VMEM((FF, D), BF16)],
        ),
        out_shape=jax.ShapeDtypeStruct((n_slots, D), BF16),
        compiler_params=pltpu.CompilerParams(dimension_semantics=("arbitrary",),
                                             vmem_limit_bytes=VMEM_LIMIT),
        name="moe_ffn",
    )(blk_e, xs, wt, w1, w3, w2)


def _hier_moe(h2, route, w1, w3, w2, layer):
    N, D = h2.shape
    expert = route[:, :TOP_IN_GROUP].astype(jnp.int32)
    gate = route[:, TOP_IN_GROUP:2 * TOP_IN_GROUP]
    n_asg = N * TOP_IN_GROUP
    n_blocks = n_asg // MOE_ROWS + N_EXPERTS
    n_slots = n_blocks * MOE_ROWS
    n_fill = n_slots - n_asg
    flat_e = expert.reshape(n_asg // LANE, LANE)
    counts = jnp.sum(flat_e[None] == jnp.arange(N_EXPERTS)[:, None, None], axis=(1, 2)).astype(jnp.int32)
    pad_counts = (counts + MOE_ROWS - 1) // MOE_ROWS * MOE_ROWS
    pad_end = jnp.cumsum(pad_counts)
    blk_start = jnp.arange(n_blocks) * MOE_ROWS
    blk_e = jnp.minimum(jnp.sum(pad_end[None, :] <= blk_start[:, None], 1), N_EXPERTS - 1).astype(jnp.int32)
    fill_end = jnp.cumsum(pad_counts - counts)
    fill_id = jnp.arange(n_fill).reshape(n_fill // LANE, LANE)
    fill_e = jnp.zeros_like(fill_id)
    for e in range(N_EXPERTS):
        fill_e = fill_e + (fill_id >= fill_end[e])
    keys = jnp.concatenate([flat_e.reshape(n_asg) * 2, fill_e.reshape(n_fill) * 2 + 1])
    toks = jnp.concatenate([jnp.arange(n_asg, dtype=jnp.int32) // TOP_IN_GROUP, jnp.zeros((n_fill,), jnp.int32)])
    wts = jnp.concatenate([gate.reshape(n_asg), jnp.zeros((n_fill,), F32)])
    _, slot_tok, slot_w, slot_src = lax.sort((keys, toks, wts, jnp.arange(n_slots, dtype=jnp.int32)), num_keys=1)
    _, entry_slot = lax.sort((slot_src, jnp.arange(n_slots, dtype=jnp.int32)), num_keys=1)
    asg_slot = entry_slot[:n_asg].reshape(N, TOP_IN_GROUP)
    yb = _moe_ffn(blk_e, h2[slot_tok], slot_w[:, None], w1, w3, w2, layer)
    return yb[asg_slot[:, 0]].astype(F32) + yb[asg_slot[:, 1]].astype(F32)


INT_MIN = -2 ** 31
DSA_KEY_STEP = 256


def _float_order_key(x):
    bits = pltpu.bitcast(x, jnp.int32)
    bits = jnp.where(x == 0.0, 0, bits)
    return bits ^ ((bits >> 31) & 0x7FFFFFFF)


COL_PART = 64


def _col_reduce(x, reduce):
    part = reduce(x.reshape(x.shape[0] // COL_PART, COL_PART, x.shape[1]), axis=0)
    return reduce(part, axis=0, keepdims=True)


def _col_count(mask):
    return _col_reduce(jnp.where(mask, 1.0, 0.0), jnp.sum)


def _head_block_diag(t, group):
    n_heads = t.shape[0] // group
    row_h = lax.broadcasted_iota(jnp.int32, t.shape, 0) // group
    return jnp.concatenate([jnp.where(row_h == h, t, 0.0) for h in range(n_heads)], axis=1)


def _dsa_block_t(qd_ref, kd4_ref, vdwt_ref, qi_ref, kihi_ref, kilo_ref, wi_ref, g_ref, o_ref, *, kl, n_sel):
    Q = DSA_BLOCK
    q0 = pl.program_id(1) * Q
    w_hi, w_lo = _split_bf16(_head_block_diag(jnp.transpose(qi_ref[0]), IDX_DIM))
    k_hi, k_lo = kihi_ref[0, :kl, :], kilo_ref[0, :kl, :]
    sc = _dot(jnp.concatenate([k_hi, k_lo, k_hi], axis=1),
              jnp.concatenate([w_hi, w_hi, w_lo], axis=0))
    wit = jnp.transpose(wi_ref[0])
    score = sum(wit[h:h + 1, :] * jnp.maximum(sc[:, h * Q:(h + 1) * Q], 0.0) for h in range(IDX_HEADS))
    kidx = lax.broadcasted_iota(jnp.int32, (kl, Q), 0)
    qpos = q0 + lax.broadcasted_iota(jnp.int32, (kl, Q), 1)
    adm = kidx <= qpos
    key = _float_order_key(jnp.where(adm, score, -jnp.inf))

    def value_bit(it, tau):
        cand = tau | jnp.left_shift(jnp.int32(1), 31 - it)
        return jnp.where(_col_count(key >= (cand ^ INT_MIN)) >= n_sel, cand, tau)

    tau = lax.fori_loop(0, 32, value_bit, jnp.zeros((1, Q), jnp.int32)) ^ INT_MIN
    gt = key > tau
    eq = (key == tau) & adm
    need = n_sel - _col_count(gt)
    n_eq = _col_count(eq)

    def index_bits():
        def index_bit(it, bound):
            cand = bound | jnp.left_shift(jnp.int32(1), 11 - it)
            return jnp.where(_col_count(eq & (kidx < cand)) <= need, cand, bound)
        return lax.fori_loop(0, 12, index_bit, jnp.zeros((1, Q), jnp.int32))

    bound = lax.cond(jnp.max(n_eq - need) > 0.0, index_bits, lambda: jnp.full((1, Q), kl, jnp.int32))
    sel = gt | (eq & (kidx < bound))

    w_att = _head_block_diag(jnp.transpose(qd_ref[0]), HEAD_DIM).astype(BF16)
    lg = _dot(kd4_ref[0, :kl, :], w_att) * HEAD_DIM ** -0.5
    lg = jnp.where(jnp.concatenate([sel] * N_HEADS, axis=1), lg, -jnp.inf)
    p = jnp.exp(lg - _col_reduce(lg, jnp.max))
    out_t = _dot(vdwt_ref[0, :, :kl], p.astype(BF16)) / _col_reduce(p, jnp.sum)
    row_h = lax.broadcasted_iota(jnp.int32, (GROUP_W, Q), 0) // HEAD_DIM
    nat_t = sum(jnp.where(row_h == h, out_t[:, h * Q:(h + 1) * Q], 0.0) for h in range(N_HEADS))
    r = jnp.transpose(nat_t)
    hi_ = lax.broadcasted_iota(jnp.int32, (GROUP_W, GROUP_W), 0) // HEAD_DIM
    hj_ = lax.broadcasted_iota(jnp.int32, (GROUP_W, GROUP_W), 1) // HEAD_DIM
    ms = _dot_f32_by_exact(r * r, jnp.where(hi_ == hj_, 1.0, 0.0).astype(BF16)) * (1.0 / HEAD_DIM)
    o_ref[0] = r * lax.rsqrt(ms + NORM_EPS) * g_ref[...]


def _dsa_kernel(qd_ref, kd4_ref, vdwt_ref, qi_ref, kihi_ref, kilo_ref, wi_ref, g_ref, o_ref, *, kls, n_sel):
    blocks_per_step = DSA_KEY_STEP // DSA_BLOCK
    for j, kl in enumerate(kls):
        @pl.when(pl.program_id(1) // blocks_per_step == j)
        def _():
            _dsa_block_t(qd_ref, kd4_ref, vdwt_ref, qi_ref, kihi_ref, kilo_ref, wi_ref, g_ref, o_ref,
                         kl=kl, n_sel=n_sel)


DSA_PREP_ROWS = 256
D_Q, D_KV, D_QI, D_KW = 0, GROUP_W, GROUP_W + LANE, GROUP_W + 2 * LANE


def _swap_halves(x, half):
    n = x.shape[1]
    lane = lax.broadcasted_iota(jnp.int32, x.shape, 1)
    return jnp.where(lane % (2 * half) < half, pltpu.roll(x, n - half, axis=1), pltpu.roll(x, half, axis=1))


def _dsa_prep_kernel(p_ref, cq_ref, sq_ref, ci_ref, si_ref, gq_ref, gk_ref,
                     qd_ref, kd4_ref, vdwt_ref, qi_ref, kihi_ref, kilo_ref, wi_ref):
    GW = GROUP_W
    x = p_ref[0]
    hi_ = lax.broadcasted_iota(jnp.int32, (GW, GW), 0) // HEAD_DIM
    hj_ = lax.broadcasted_iota(jnp.int32, (GW, GW), 1) // HEAD_DIM
    ones_bd = jnp.where(hi_ == hj_, 1.0, 0.0).astype(BF16)
    q = x[:, D_Q:D_Q + GW]
    q = q * lax.rsqrt(_dot_f32_by_exact(q * q, ones_bd) * (1.0 / HEAD_DIM) + NORM_EPS) * gq_ref[...]
    qd_ref[0] = q * cq_ref[...] + _swap_halves(q, HEAD_DIM // 2) * sq_ref[...]
    kv = x[:, D_KV:D_KV + LANE]
    lane = lax.broadcasted_iota(jnp.int32, kv.shape, 1)
    is_k = lane < HEAD_DIM
    ms = jnp.sum(jnp.where(is_k, kv * kv, 0.0), axis=1, keepdims=True) * (1.0 / HEAD_DIM)
    kn = kv * lax.rsqrt(ms + NORM_EPS) * gk_ref[...]
    kr = kn * cq_ref[:, :LANE] + _swap_halves(kn, HEAD_DIM // 2) * sq_ref[:, :LANE]
    k2 = jnp.where(is_k, kr, pltpu.roll(kr, HEAD_DIM, axis=1))
    kd4_ref[0] = jnp.concatenate([k2] * (GW // LANE), axis=1).astype(BF16)
    v2 = jnp.where(is_k, pltpu.roll(kv, HEAD_DIM, axis=1), kv)
    v2t = jnp.transpose(v2)
    vdwt_ref[0] = jnp.concatenate([v2t] * (GW // LANE), axis=0).astype(BF16)
    qi = x[:, D_QI:D_QI + LANE]
    qi_ref[0] = qi * ci_ref[...] + _swap_halves(qi, IDX_DIM // 2) * si_ref[...]
    kw = x[:, D_KW:D_KW + LANE]
    kir = kw * ci_ref[...] + _swap_halves(kw, IDX_DIM // 2) * si_ref[...]
    ki1 = jnp.where(lane < IDX_DIM, kir, 0.0)
    ki2 = ki1 + pltpu.roll(ki1, IDX_DIM, axis=1)
    ki4 = ki2 + pltpu.roll(ki2, 2 * IDX_DIM, axis=1)
    kihi_ref[0], kilo_ref[0] = _split_bf16(ki4)
    wi_ref[0] = pltpu.roll(kw, LANE - IDX_DIM, axis=1) * (IDX_HEADS ** -0.5 * IDX_DIM ** -0.5)


def _rope_tables(S, dim, width):
    half = dim // 2
    inv = ROPE_THETA ** (-jnp.arange(half, dtype=F32) / half)
    ang = jnp.arange(S, dtype=F32)[:, None] * inv[None, :]
    cos = jnp.tile(jnp.cos(ang), (1, width // half))
    sin = jnp.tile(jnp.concatenate([-jnp.sin(ang), jnp.sin(ang)], axis=1), (1, width // dim))
    return cos, sin


def _dsa_prep(p, qn_g, kn_g):
    B, S, _ = p.shape
    GW, R = GROUP_W, DSA_PREP_ROWS
    cq, sq = _rope_tables(S, HEAD_DIM, GW)
    ci, si = _rope_tables(S, IDX_DIM, LANE)
    gq = jnp.tile(qn_g, N_HEADS)[None]
    gk = jnp.pad(kn_g, (0, LANE - HEAD_DIM))[None]
    rows = lambda w: pl.BlockSpec((1, R, w), lambda b, i: (b, i, 0))
    tab = lambda w: pl.BlockSpec((R, w), lambda b, i: (i, 0))
    cols = lambda r: pl.BlockSpec((1, r, R), lambda b, i: (b, 0, i))
    vec = lambda w: pl.BlockSpec((1, w), lambda b, i: (0, 0))
    return pl.pallas_call(
        _dsa_prep_kernel,
        grid=(B, S // R),
        in_specs=[rows(D_PAD), tab(GW), tab(GW), tab(LANE), tab(LANE), vec(GW), vec(LANE)],
        out_specs=[rows(GW), rows(GW), cols(GW), rows(LANE), rows(LANE), rows(LANE), rows(LANE)],
        out_shape=[jax.ShapeDtypeStruct((B, S, GW), F32), jax.ShapeDtypeStruct((B, S, GW), BF16),
                   jax.ShapeDtypeStruct((B, GW, S), BF16), jax.ShapeDtypeStruct((B, S, LANE), F32),
                   jax.ShapeDtypeStruct((B, S, LANE), BF16), jax.ShapeDtypeStruct((B, S, LANE), BF16),
                   jax.ShapeDtypeStruct((B, S, LANE), F32)],
        compiler_params=pltpu.CompilerParams(dimension_semantics=("parallel", "parallel"),
                                             vmem_limit_bytes=VMEM_LIMIT),
        name="dsa_prep",
    )(p, cq, sq, ci, si, gq, gk)


def _dsa_attn_norm(qd, kd4, vdwt, qi, ki_hi, ki_lo, wi, g):
    B, S, _ = qd.shape
    n_sel = min(TOPK_MAX, S // 4)
    assert S % DSA_KEY_STEP == 0 and n_sel <= DSA_KEY_STEP
    kls = tuple(range(DSA_KEY_STEP, S + 1, DSA_KEY_STEP))
    blk = lambda w: pl.BlockSpec((1, DSA_BLOCK, w), lambda b, i: (b, i, 0))
    per_b = lambda r, c: pl.BlockSpec((1, r, c), lambda b, i: (b, 0, 0))
    return pl.pallas_call(
        functools.partial(_dsa_kernel, kls=kls, n_sel=n_sel),
        grid=(B, S // DSA_BLOCK),
        in_specs=[blk(GROUP_W), per_b(S, GROUP_W), per_b(GROUP_W, S), blk(IDX_HEADS * IDX_DIM),
                  per_b(S, LANE), per_b(S, LANE), blk(LANE), pl.BlockSpec((1, GROUP_W), lambda b, i: (0, 0))],
        out_specs=blk(GROUP_W),
        out_shape=jax.ShapeDtypeStruct((B, S, GROUP_W), F32),
        compiler_params=pltpu.CompilerParams(dimension_semantics=("parallel", "parallel"),
                                             vmem_limit_bytes=VMEM_LIMIT),
        name="dsa_attn",
    )(qd, kd4, vdwt, qi, ki_hi, ki_lo, wi, g[None])


RWKV_CHUNK = 64
RWKV_LOW = RWKV_W_RANK + RWKV_A_RANK + RWKV_G_RANK
RWKV_GROUP = N_HEADS
RWKV_BATCH = 8


def _dot(a, b):
    return jnp.dot(a, b, preferred_element_type=F32)


def _dot_nt(a, b):
    return lax.dot_general(a, b, (((1,), (1,)), ((), ())), preferred_element_type=F32)


def _dot_tn(a, b):
    return lax.dot_general(a, b, (((0,), (0,)), ((), ())), preferred_element_type=F32)


def _split3_bf16(t):
    p1 = t.astype(BF16)
    r1 = t - p1.astype(F32)
    p2 = r1.astype(BF16)
    p3 = (r1 - p2.astype(F32)).astype(BF16)
    return p1, p2, p3


def _dot_f32_by_exact(a, b_exact):
    m = a.shape[0]
    r = _dot(jnp.concatenate(_split3_bf16(a), axis=0), b_exact)
    return r[:m] + r[m:2 * m] + r[2 * m:]


def _dot_exact_by_f32(a_exact, b):
    n = b.shape[1]
    r = _dot(a_exact, jnp.concatenate(_split3_bf16(b), axis=1))
    return r[:, :n] + r[:, n:2 * n] + r[:, 2 * n:]


def _dot3(a, b_hi, b_lo):
    a_hi, a_lo = _split_bf16(a)
    return _dot(jnp.concatenate([a_hi, a_lo, a_hi], axis=1), jnp.concatenate([b_hi, b_hi, b_lo], axis=0))


def _softplus(z):
    return jnp.maximum(z, 0.0) + jnp.log(1.0 + jnp.exp(-jnp.abs(z)))


def _rwkv_kernel(p_ref, mu_ref, vec_ref, lhi_ref, llo_ref, o_ref, state_ref, prev_ref):
    @pl.when(pl.program_id(1) == 0)
    def _():
        state_ref[...] = jnp.zeros_like(state_ref)
        prev_ref[...] = jnp.zeros_like(prev_ref)

    masks = _rwkv_masks()
    chains = [_rwkv_chunk(masks, p_ref.at[n], mu_ref, vec_ref, lhi_ref, llo_ref, o_ref.at[n], state_ref.at[n],
                          prev_ref.at[n]) for n in range(RWKV_BATCH)]
    for _ in itertools.zip_longest(*chains):
        pass


def _rwkv_masks():
    L, GW, SG = RWKV_CHUNK, GROUP_W, RWKV_GROUP * RWKV_CHUNK
    ri = lax.broadcasted_iota(jnp.int32, (GW, GW), 0)
    ci = lax.broadcasted_iota(jnp.int32, (GW, GW), 1)
    ones_bd = jnp.where((ri // HEAD_DIM) == (ci // HEAD_DIM), 1.0, 0.0).astype(BF16)
    ti = lax.broadcasted_iota(jnp.int32, (L, L), 0)
    tj = lax.broadcasted_iota(jnp.int32, (L, L), 1)
    tri = jnp.where(tj <= ti, 1.0, 0.0).astype(BF16)
    gi = lax.broadcasted_iota(jnp.int32, (SG, SG), 0)
    gj = lax.broadcasted_iota(jnp.int32, (SG, SG), 1)
    g_same = (gi // L) == (gj // L)
    strict = g_same & ((gj % L) < (gi % L))
    incl = g_same & ((gj % L) <= (gi % L))
    diag = gi == gj
    return ones_bd, tri, g_same, strict, incl, diag, jnp.where(diag, 1.0, 0.0)


def _rwkv_chunk(masks, p_ref, mu_ref, vec_ref, lhi_ref, llo_ref, o_ref, state_ref, prev_ref):
    L, GW = RWKV_CHUNK, GROUP_W
    p = p_ref[...]
    row = lax.broadcasted_iota(jnp.int32, p.shape, 0)
    prev = jnp.where(row == 0, prev_ref[...], pltpu.roll(p, 1, axis=0))
    prev_ref[...] = p[L - 1:L, :]
    ps = p + (prev - p) * mu_ref[...]
    r, k, v = ps[:, :GW], ps[:, GW:2 * GW], ps[:, 2 * GW:3 * GW]
    low = ps[:, 3 * GW:]
    lane_low = lax.broadcasted_iota(jnp.int32, low.shape, 1)
    low = jnp.where(lane_low < RWKV_W_RANK, jnp.tanh(low),
                    jnp.where(lane_low < RWKV_W_RANK + RWKV_A_RANK, low, jax.nn.sigmoid(low)))
    up = _dot3(low, lhi_ref[...], llo_ref[...])
    w0, a0, k_k, k_a = vec_ref[0:1, :], vec_ref[1:2, :], vec_ref[2:3, :], vec_ref[3:4, :]
    r_k, ln_g, ln_b = vec_ref[4:5, :], vec_ref[5:6, :], vec_ref[6:7, :]
    logw = -jnp.exp(-_softplus(-(w0 + up[:, :GW])) - 0.5)
    rate = jax.nn.sigmoid(a0 + up[:, GW:2 * GW])
    gate = up[:, 2 * GW:]

    ones_bd, tri, g_same, strict, incl, diag, eye = masks
    kk = k * k_k
    k = k * (1.0 + (rate - 1.0) * k_a)
    seg = _dot_f32_by_exact(jnp.concatenate([kk * kk, r * k * r_k], axis=0), ones_bd)
    kk = kk / jnp.maximum(jnp.sqrt(seg[:L]), 1e-12)

    lc = _dot_exact_by_f32(tri, logw)
    lc_last = lc[L - 1:L, :]
    dec_in = jnp.exp(lc)
    dec_out = jnp.exp(-lc)
    a_t = -kk * jnp.exp(lc - logw)
    b_t = kk * rate * dec_out
    k_t = k * dec_out
    r_t = r * dec_in
    to_end = jnp.exp(lc_last)

    SG = RWKV_GROUP * L
    stack = lambda t: jnp.concatenate([t] * RWKV_GROUP, axis=0)
    bd = lambda t: jnp.where(g_same, stack(t), 0.0).astype(BF16)
    n_doublings = RWKV_CHUNK.bit_length() - 2
    ys = []
    for g in range(N_HEADS // RWKV_GROUP):
        cols = slice(g * SG, (g + 1) * SG)
        a_bd, r_bd, v_bd = bd(a_t[:, cols]), bd(r_t[:, cols]), bd(v[:, cols])
        m = _dot_nt(jnp.concatenate([a_bd, r_bd], axis=0),
                    jnp.concatenate([stack(b_t[:, cols]), stack(k_t[:, cols])], axis=0).astype(BF16))
        yield
        m_ab = jnp.where(strict, m[:SG, :SG], 0.0)
        m_ak = jnp.where(strict, m[:SG, SG:], 0.0).astype(BF16)
        m_rb = jnp.where(incl, m[SG:, :SG], 0.0).astype(BF16)
        m_rk = jnp.where(incl, m[SG:, SG:], 0.0).astype(BF16)

        inv = eye + m_ab
        pw_b = m_ab.astype(BF16)
        sq = _dot(pw_b, pw_b)
        yield
        pw_b = sq.astype(BF16)
        for s in range(n_doublings - 1):
            both = _dot(jnp.concatenate([inv.astype(BF16), pw_b], axis=0), pw_b)
            yield
            inv = inv + both[:SG]
            pw_b = both[SG:].astype(BF16)
        last = _dot(inv.astype(BF16), pw_b)
        t0 = state_ref[g]
        t0_b = t0.astype(BF16)
        rhs = _dot(jnp.concatenate([a_bd, m_ak], axis=1), jnp.concatenate([t0_b, v_bd], axis=0))
        yield
        inv_b = (inv + last).astype(BF16)
        u = _dot(inv_b, rhs.astype(BF16))
        yield
        u = u.astype(BF16)
        y_bd = _dot(jnp.concatenate([r_bd, m_rb, m_rk], axis=1),
                    jnp.concatenate([t0_b, u, v_bd], axis=0))
        end_g = to_end[:, cols]
        to_end_col = jnp.sum(jnp.where(diag, jnp.broadcast_to(end_g, (SG, SG)), 0.0), axis=1, keepdims=True)
        carried = _dot_tn(jnp.concatenate([bd(b_t[:, cols] * end_g), bd(k_t[:, cols] * end_g)], axis=0),
                          jnp.concatenate([u, v_bd], axis=0))
        yield
        ys.append(sum(y_bd[h * L:(h + 1) * L, :] for h in range(RWKV_GROUP)))
        state_ref[g] = to_end_col * t0 + carried
    y = jnp.concatenate(ys, axis=1)

    inv_d = 1.0 / HEAD_DIM
    mean = _dot_f32_by_exact(y, ones_bd) * inv_d
    yield
    yc = y - mean
    var = _dot_f32_by_exact(yc * yc, ones_bd) * inv_d
    yield
    yn = yc * lax.rsqrt(var + RWKV_GN_EPS) * ln_g + ln_b
    o_ref[...] = (yn + seg[L:] * v) * gate


def _rwkv7_time_mix(p, mu, w0, w2, a0, a2, g2, k_k, k_a, r_k, ln_g, ln_b):
    B, S, _ = p.shape
    GW = GROUP_W
    assert S % RWKV_CHUNK == 0 and RWKV_CHUNK == HEAD_DIM and B % RWKV_BATCH == 0
    low_w = jnp.zeros((RWKV_LOW, 3 * GW), F32)
    low_w = low_w.at[:RWKV_W_RANK, :GW].set(w2)
    low_w = low_w.at[RWKV_W_RANK:RWKV_W_RANK + RWKV_A_RANK, GW:2 * GW].set(a2)
    low_w = low_w.at[RWKV_W_RANK + RWKV_A_RANK:, 2 * GW:].set(g2)
    l_hi, l_lo = _split_bf16(low_w)
    vecs = jnp.stack([w0, a0, k_k, k_a, r_k, ln_g, ln_b, jnp.zeros_like(w0)], 0)
    full = lambda a: pl.BlockSpec(a.shape, lambda b, c: (0,) * a.ndim)
    mu2 = mu[None]
    return pl.pallas_call(
        _rwkv_kernel,
        grid=(B // RWKV_BATCH, S // RWKV_CHUNK),
        in_specs=[pl.BlockSpec((RWKV_BATCH, RWKV_CHUNK, A_PAD), lambda b, c: (b, c, 0)),
                  full(mu2), full(vecs), full(l_hi), full(l_lo)],
        out_specs=pl.BlockSpec((RWKV_BATCH, RWKV_CHUNK, GW), lambda b, c: (b, c, 0)),
        out_shape=jax.ShapeDtypeStruct((B, S, GW), F32),
        scratch_shapes=[pltpu.VMEM((RWKV_BATCH, N_HEADS // RWKV_GROUP, RWKV_GROUP * RWKV_CHUNK,
                                    RWKV_GROUP * HEAD_DIM), F32),
                        pltpu.VMEM((RWKV_BATCH, 1, A_PAD), F32)],
        compiler_params=pltpu.CompilerParams(dimension_semantics=("parallel", "arbitrary"),
                                             vmem_limit_bytes=VMEM_LIMIT),
        name="rwkv7",
    )(p, mu2, vecs, l_hi, l_lo)


SB_UNROLL = 4


def _sb_kernel(q_ref, k_ref, v_ref, g_ref, o_ref, kbd_ref, vbd_ref):
    i = pl.program_id(1)
    T, GW, H = SB_BLOCK, GROUP_W, N_HEADS
    lane_h = lax.broadcasted_iota(jnp.int32, (T, GW), 1) // HEAD_DIM
    k_new, v_new = k_ref[0], v_ref[0]
    for h in range(H):
        kbd_ref[i, h * T:(h + 1) * T, :] = jnp.where(lane_h == h, k_new, 0.0).astype(BF16)
        vbd_ref[i, h * T:(h + 1) * T, :] = jnp.where(lane_h == h, v_new, 0.0).astype(BF16)

    q = (q_ref[0] * HEAD_DIM ** -0.5).astype(BF16)
    si = lax.broadcasted_iota(jnp.int32, (T, 2 * T), 0)
    sj = lax.broadcasted_iota(jnp.int32, (T, 2 * T), 1)
    later_and_all = jnp.where((si > sj) | (sj >= T), 1.0, 0.0).astype(BF16)
    qrow = lax.broadcasted_iota(jnp.int32, (T, H * T), 0)
    kcol = lax.broadcasted_iota(jnp.int32, (T, H * T), 1) % T
    causal = kcol < qrow

    def key_block_stages(j, box, diagonal):
        z = _dot_nt(q, kbd_ref[j])
        yield
        log1m = -(jnp.maximum(z, 0.0) + jnp.log(1.0 + jnp.exp(-jnp.abs(z))))
        log_sig = z + log1m
        log1m_in = (jnp.where(causal, log1m, 0.0) if diagonal else log1m).astype(BF16)
        sums = _dot(jnp.concatenate([log1m_in[:, h * T:(h + 1) * T] for h in range(H)], axis=0), later_and_all)
        yield
        suffix = jnp.concatenate([sums[h * T:(h + 1) * T, :T] for h in range(H)], axis=1)
        total = jnp.concatenate([sums[h * T:(h + 1) * T, T:] for h in range(H)], axis=1)
        att = jnp.exp(log_sig + suffix + box["carry"])
        if diagonal:
            att = jnp.where(causal, att, 0.0)
        box["carry"] = box["carry"] + total
        pv = _dot(att.astype(BF16), vbd_ref[j])
        yield
        box["acc"] = box["acc"] + pv

    def key_blocks(js, state, diagonal=False):
        box = {"carry": state[0], "acc": state[1]}
        for _ in itertools.zip_longest(*[key_block_stages(j, box, diagonal) for j in js]):
            pass
        return box["carry"], box["acc"]

    state = key_blocks([i], (jnp.zeros((T, H * T), F32), jnp.zeros((T, GW), F32)), True)
    rem = i % SB_UNROLL
    state = lax.fori_loop(0, rem, lambda it, st: key_blocks([i - 1 - it], st), state)
    top = i - 1 - rem
    _, y = lax.fori_loop(0, i // SB_UNROLL,
                         lambda it, st: key_blocks([top - SB_UNROLL * it - n for n in range(SB_UNROLL)], st), state)

    hi_ = lax.broadcasted_iota(jnp.int32, (GW, GW), 0) // HEAD_DIM
    hj_ = lax.broadcasted_iota(jnp.int32, (GW, GW), 1) // HEAD_DIM
    ones_bd = jnp.where(hi_ == hj_, 1.0, 0.0).astype(BF16)
    ms = _dot_f32_by_exact(y * y, ones_bd) * (1.0 / HEAD_DIM)
    o_ref[0] = y * lax.rsqrt(ms + NORM_EPS) * g_ref[...]


def _stick_breaking_norm(p, g):
    B, S, _ = p.shape
    GW = GROUP_W
    assert S % SB_BLOCK == 0
    col = lambda n: pl.BlockSpec((1, SB_BLOCK, GW), lambda b, i: (b, i, n))
    blk = col(0)
    q = k = v = p
    stacked = pltpu.VMEM((S // SB_BLOCK, N_HEADS * SB_BLOCK, GW), BF16)
    return pl.pallas_call(
        _sb_kernel,
        grid=(B, S // SB_BLOCK),
        in_specs=[col(0), col(1), col(2), pl.BlockSpec((1, GW), lambda b, i: (0, 0))],
        out_specs=blk,
        out_shape=jax.ShapeDtypeStruct((B, S, GW), F32),
        scratch_shapes=[stacked, stacked],
        compiler_params=pltpu.CompilerParams(dimension_semantics=("parallel", "arbitrary"),
                                             vmem_limit_bytes=VMEM_LIMIT),
        name="stick_breaking",
    )(q, k, v, g[None])


ML_HALO = 8
ML_GROUP = N_HEADS
ML_BATCH = 4


def _dot_nt_exact_by_f32(a_exact, b):
    return _dot_nt(jnp.concatenate([a_exact] * 3, axis=1), jnp.concatenate(_split3_bf16(b), axis=1))


def _mlstm_kernel(p_ref, cw_ref, cb_ref, gb_ref, g_ref, o_ref, ext_ref, ct_ref, n_ref, m_ref):
    @pl.when(pl.program_id(1) == 0)
    def _():
        ext_ref[...] = jnp.zeros_like(ext_ref)
        ct_ref[...] = jnp.zeros_like(ct_ref)
        n_ref[...] = jnp.zeros_like(n_ref)
        m_ref[...] = jnp.zeros_like(m_ref)

    masks = _mlstm_masks()
    chains = [_mlstm_chunk(masks, p_ref.at[n], cw_ref, cb_ref, gb_ref, g_ref, o_ref.at[n], ext_ref.at[n],
                           ct_ref.at[n], n_ref.at[n], m_ref.at[n]) for n in range(ML_BATCH)]
    for _ in itertools.zip_longest(*chains):
        pass


def _mlstm_masks():
    L, GW, H, SG = ML_CHUNK, GROUP_W, N_HEADS, ML_GROUP * ML_CHUNK
    gi = lax.broadcasted_iota(jnp.int32, (LANE, 2 * GW), 0)
    gj = lax.broadcasted_iota(jnp.int32, (LANE, 2 * GW), 1)
    expand = jnp.where(gi == (gj % GW) // HEAD_DIM + H * (gj // GW), 1.0, 0.0).astype(BF16)
    ti = lax.broadcasted_iota(jnp.int32, (L, L), 0)
    tj = lax.broadcasted_iota(jnp.int32, (L, L), 1)
    tri = jnp.where(tj <= ti, 1.0, 0.0).astype(BF16)
    ri = lax.broadcasted_iota(jnp.int32, (SG, SG), 0)
    ci = lax.broadcasted_iota(jnp.int32, (SG, SG), 1)
    same_head = (ri // L) == (ci // HEAD_DIM)
    first_lane = ci == (ri // L) * HEAD_DIM
    sel_first = jnp.where(first_lane, 1.0, 0.0).astype(BF16)
    causal = lax.broadcasted_iota(jnp.int32, (SG, L), 1) <= lax.broadcasted_iota(jnp.int32, (SG, L), 0) % L
    hi_ = lax.broadcasted_iota(jnp.int32, (GW, GW), 0) // HEAD_DIM
    hj_ = lax.broadcasted_iota(jnp.int32, (GW, GW), 1) // HEAD_DIM
    ones_bd = jnp.where(hi_ == hj_, 1.0, 0.0).astype(BF16)
    return expand, tri, same_head, first_lane, sel_first, causal, ones_bd


def _mlstm_chunk(masks, p_ref, cw_ref, cb_ref, gb_ref, g_ref, o_ref, ext_ref, ct_ref, n_ref, m_ref):
    L, GW, H = ML_CHUNK, GROUP_W, N_HEADS
    x = p_ref[...]
    ext_ref[ML_HALO:, :] = x[:, :2 * GW]
    conv = cb_ref[...]
    for j in range(ML_CONV):
        conv = conv + cw_ref[j:j + 1, :] * ext_ref[pl.ds(ML_HALO - (ML_CONV - 1) + j, L), :]
    ext_ref[:ML_HALO, :] = x[L - ML_HALO:, :2 * GW]
    qk = conv * jax.nn.sigmoid(conv)
    q, k = qk[:, :GW], qk[:, GW:] * HEAD_DIM ** -0.5
    v, o = x[:, 2 * GW:3 * GW], x[:, 3 * GW:4 * GW]

    gates = x[:, 4 * GW:]
    expand, tri, same_head, first_lane, sel_first, causal, ones_bd = masks
    graw = _dot_f32_by_exact(gates, expand) + gb_ref[...]
    yield
    capped = GATE_CAP * jnp.tanh(graw * (1.0 / GATE_CAP))
    log_i = capped[:, :GW]
    cf = capped[:, GW:]
    log_f = jnp.minimum(cf, 0.0) - jnp.log(1.0 + jnp.exp(-jnp.abs(cf)))

    bf = _dot_exact_by_f32(tri, log_f)
    yield
    b_last = bf[L - 1:L, :]
    m_row, n_row = m_ref[...], n_ref[...]
    dec = b_last - bf + log_i
    m_new = jnp.maximum(b_last + m_row, jnp.max(dec, axis=0, keepdims=True))
    kw = k * jnp.exp(dec - m_new)
    s_old = jnp.exp(b_last + m_row - m_new)
    n_ref[...] = s_old * n_row + jnp.sum(kw, axis=0, keepdims=True)
    m_ref[...] = m_new
    g_in = bf + m_row
    li_b = log_i - bf

    SG = ML_GROUP * L
    stack = lambda t: jnp.concatenate([t] * ML_GROUP, axis=0)
    pick = lambda t: jnp.sum(jnp.where(first_lane, t, 0.0), axis=1, keepdims=True)
    hs = []
    for g in range(H // ML_GROUP):
        cols = slice(g * SG, (g + 1) * SG)
        ct = ct_ref[g]
        qs = jnp.where(same_head, stack(q[:, cols]), 0.0)
        qs_b = qs.astype(BF16)
        v_b = v[:, cols].astype(BF16)
        b_col = pick(stack(bf[:, cols]))
        g_col = pick(stack(g_in[:, cols]))
        row_part = _dot_nt_exact_by_f32(sel_first, li_b[:, cols])
        qk = _dot_nt(qs_b, k[:, cols].astype(BF16))
        inter = _dot(qs_b, ct.astype(BF16))
        carried = _dot_tn(kw[:, cols].astype(BF16), v_b)
        yield
        dmat = jnp.where(causal, b_col + row_part, -jnp.inf)
        m_t = jnp.maximum(g_col, jnp.max(dmat, axis=1, keepdims=True))
        s_inter = jnp.exp(g_col - m_t)
        sqk = qk * jnp.exp(dmat - m_t)
        intra = _dot(sqk.astype(BF16), v_b)
        yield
        num = s_inter * inter + jnp.where(same_head, intra, 0.0)
        den = (s_inter * jnp.sum(qs * n_row[:, cols], axis=1, keepdims=True)
               + jnp.sum(sqk, axis=1, keepdims=True))
        hst = num / jnp.maximum(jnp.abs(den), jnp.exp(-m_t))
        hs.append(sum(hst[n * L:(n + 1) * L, :] for n in range(ML_GROUP)))
        ct_ref[g] = s_old[:, cols] * ct + jnp.where(same_head, carried, 0.0)
    h = jnp.concatenate(hs, axis=1)

    ms = _dot_f32_by_exact(h * h, ones_bd) * (1.0 / HEAD_DIM)
    yield
    o_ref[...] = jax.nn.sigmoid(o) * (h * lax.rsqrt(ms + NORM_EPS) * g_ref[...])


def _mlstm_mix(p, conv_w, conv_b, ig_b, fg_b, norm_g):
    B, S, _ = p.shape
    GW = GROUP_W
    assert S % ML_CHUNK == 0 and ML_CONV - 1 <= ML_HALO <= ML_CHUNK and B % ML_BATCH == 0
    gate_b = jnp.concatenate([jnp.repeat(ig_b, HEAD_DIM), jnp.repeat(fg_b, HEAD_DIM)])[None]
    full = lambda a: pl.BlockSpec(a.shape, lambda b, c: (0,) * a.ndim)
    cb2, g2 = conv_b[None], norm_g[None]
    return pl.pallas_call(
        _mlstm_kernel,
        grid=(B // ML_BATCH, S // ML_CHUNK),
        in_specs=[pl.BlockSpec((ML_BATCH, ML_CHUNK, C_PAD), lambda b, c: (b, c, 0)),
                  full(conv_w), full(cb2), full(gate_b), full(g2)],
        out_specs=pl.BlockSpec((ML_BATCH, ML_CHUNK, GW), lambda b, c: (b, c, 0)),
        out_shape=jax.ShapeDtypeStruct((B, S, GW), F32),
        scratch_shapes=[pltpu.VMEM((ML_BATCH, ML_HALO + ML_CHUNK, 2 * GW), F32),
                        pltpu.VMEM((ML_BATCH, N_HEADS // ML_GROUP, ML_GROUP * ML_CHUNK, ML_GROUP * HEAD_DIM), F32),
                        pltpu.VMEM((ML_BATCH, 1, GW), F32), pltpu.VMEM((ML_BATCH, 1, GW), F32)],
        compiler_params=pltpu.CompilerParams(dimension_semantics=("parallel", "arbitrary"),
                                             vmem_limit_bytes=VMEM_LIMIT),
        name="mlstm",
    )(p, conv_w, cb2, gate_b, g2)


def _rms_norm(x, g):
    xf = x.astype(F32)
    y = xf * lax.rsqrt(jnp.mean(xf * xf, -1, keepdims=True) + NORM_EPS)
    return (y * g.astype(F32)).astype(x.dtype)


def _rope(x, pos):
    half = x.shape[-1] // 2
    inv = ROPE_THETA ** (-jnp.arange(half, dtype=F32) / half)
    ang = pos.astype(F32)[:, None] * inv[None, :]
    cos = jnp.cos(ang)[None, :, None, :]
    sin = jnp.sin(ang)[None, :, None, :]
    xf = x.astype(F32)
    x1, x2 = xf[..., :half], xf[..., half:]
    return jnp.concatenate([x1 * cos - x2 * sin, x2 * cos + x1 * sin], -1).astype(x.dtype)


def kernel(x, c, ada_w, ada_b, norm1_g, norm2_g, w_in, rk_mu, rk_w0, rk_w2, rk_a0, rk_a2, rk_g2, rk_kk, rk_ka, rk_rk, rk_ln_g, rk_ln_b, sb_norm_g, ml_conv_w, ml_conv_b, ml_ig_b, ml_fg_b, ml_norm_g, ds_qn_g, ds_kn_g, ds_out_g, w_out, moe_wg, moe_bg, moe_we, moe_be, moe_w1, moe_w3, moe_w2):
    B, S, D = x.shape
    H, d = N_HEADS, HEAD_DIM
    depth = ada_w.shape[0]
    pos = jnp.arange(S)
    c_act = jax.nn.silu(c)
    for l in range(depth):
        mod = (c_act @ ada_w[l] + ada_b[l])[:, None, :]
        sh1, sc1, gt1, sh2, sc2, gt2 = jnp.split(mod, 6, axis=-1)

        pA, pB, pC, pD = _in_proj(x, sc1, sh1, norm1_g[l][None], _pad_w_in(w_in[l]))

        yA = _rwkv7_time_mix(pA, rk_mu[l], rk_w0[l], rk_w2[l], rk_a0[l], rk_a2[l], rk_g2[l],
                             rk_kk[l], rk_ka[l], rk_rk[l], rk_ln_g[l], rk_ln_b[l])

        yB = _stick_breaking_norm(pB, sb_norm_g[l])

        yC = _mlstm_mix(pC, ml_conv_w[l], ml_conv_b[l], ml_ig_b[l], ml_fg_b[l], ml_norm_g[l])

        yD = _dsa_attn_norm(*_dsa_prep(pD, ds_qn_g[l], ds_kn_g[l]), ds_out_g[l])

        router = jnp.pad(jnp.concatenate([moe_wg[l], moe_we[l]], 1),
                         ((0, 0), (0, ROUTER_PAD - N_GROUPS - N_EXPERTS)))
        r_hi, r_lo = _split_bf16(router)
        r_b = jnp.pad(jnp.concatenate([moe_bg[l], moe_be[l]]), (0, ROUTER_PAD - N_GROUPS - N_EXPERTS))[None]
        x1, h2, route = _out_proj((yA, yB, yC, yD), x, gt1, sc2, sh2, norm2_g[l][None],
                                  w_out[l].astype(BF16), r_hi, r_lo, r_b)

        moe = _hier_moe(h2.reshape(B * S, D), route.reshape(B * S, ROUTER_PAD),
                        moe_w1, moe_w3, moe_w2, l)
        x = x1 + gt2 * moe.reshape(B, S, D)
    return x
```

```python
import functools
import itertools

import jax
import jax.numpy as jnp
import numpy as np
from jax import lax
from jax.experimental import pallas as pl
from jax.experimental.pallas import tpu as pltpu

F32 = jnp.float32
BF16 = jnp.bfloat16

D_MODEL = 1024
N_MIXERS = 4
GROUP_W = D_MODEL // N_MIXERS
HEAD_DIM = 64
N_HEADS = GROUP_W // HEAD_DIM
NORM_EPS = 1e-6
RWKV_W_RANK = 32
RWKV_A_RANK = 32
RWKV_G_RANK = 64
RWKV_GN_EPS = 64e-5
SB_BLOCK = 128
ML_CHUNK = 64
ML_CONV = 4
GATE_CAP = 15.0
DSA_BLOCK = 128
IDX_HEADS = 4
IDX_DIM = 32
TOPK_MAX = 256
ROPE_THETA = 10000.0
N_GROUPS = 4
EXP_PER_GROUP = 8
N_EXPERTS = N_GROUPS * EXP_PER_GROUP
EXPERT_FF = D_MODEL // 2
TOP_IN_GROUP = 2

A_SIZES = (GROUP_W, GROUP_W, GROUP_W, RWKV_W_RANK, RWKV_A_RANK, RWKV_G_RANK)
B_SIZES = (GROUP_W, GROUP_W, GROUP_W)
C_SIZES = (GROUP_W, GROUP_W, GROUP_W, GROUP_W, N_HEADS, N_HEADS)
D_SIZES = (GROUP_W, HEAD_DIM, HEAD_DIM, IDX_HEADS * IDX_DIM, IDX_DIM, IDX_HEADS)
A_COLS = sum(A_SIZES)
B_COLS = sum(B_SIZES)
C_COLS = sum(C_SIZES)
D_COLS = sum(D_SIZES)

LANE = 128
A_PAD = 896
B_PAD = 768
C_PAD = 1152
D_PAD = 640
P_PAD = A_PAD + B_PAD + C_PAD + D_PAD
ROUTER_PAD = LANE

IN_ROWS = 512
OUT_ROWS = 512
MOE_ROWS = 512
VMEM_LIMIT = 48 * 1024 * 1024


def _split_cols(t, sizes):
    return jnp.split(t, [int(i) for i in np.cumsum(sizes)[:-1]], axis=-1)


def _in_proj_kernel(x_ref, sc_ref, sh_ref, g_ref, w_ref, oa_ref, ob_ref, oc_ref, od_ref):
    x = x_ref[0]
    y = x * lax.rsqrt(jnp.mean(x * x, -1, keepdims=True) + NORM_EPS) * g_ref[...]
    h = y * (1.0 + sc_ref[0]) + sh_ref[0]
    p = jnp.dot(h.astype(BF16), w_ref[...], preferred_element_type=F32)
    oa_ref[0] = p[:, :A_PAD]
    ob_ref[0] = p[:, A_PAD:A_PAD + B_PAD]
    oc_ref[0] = p[:, A_PAD + B_PAD:A_PAD + B_PAD + C_PAD]
    od_ref[0] = p[:, A_PAD + B_PAD + C_PAD:]


def _in_proj(x, sc, sh, g, w_pad):
    B, S, D = x.shape
    row = lambda w: pl.BlockSpec((1, IN_ROWS, w), lambda b, i: (b, i, 0))
    vec = pl.BlockSpec((1, 1, D), lambda b, i: (b, 0, 0))
    return pl.pallas_call(
        _in_proj_kernel,
        grid=(B, S // IN_ROWS),
        in_specs=[row(D), vec, vec, pl.BlockSpec((1, D), lambda b, i: (0, 0)),
                  pl.BlockSpec((D, P_PAD), lambda b, i: (0, 0))],
        out_specs=[row(A_PAD), row(B_PAD), row(C_PAD), row(D_PAD)],
        out_shape=[jax.ShapeDtypeStruct((B, S, w), F32) for w in (A_PAD, B_PAD, C_PAD, D_PAD)],
        compiler_params=pltpu.CompilerParams(dimension_semantics=("parallel", "parallel"),
                                             vmem_limit_bytes=VMEM_LIMIT),
        name="in_proj",
    )(x, sc, sh, g, w_pad)


def _pad_w_in(w):
    wa, wb, wc, wd = _split_cols(w, (A_COLS, B_COLS, C_COLS, D_COLS))
    padc = lambda t, n: jnp.pad(t, ((0, 0), (0, n - t.shape[1])))
    return jnp.concatenate([padc(wa, A_PAD), padc(wb, B_PAD), padc(wc, C_PAD), padc(wd, D_PAD)], 1).astype(BF16)


def _split_bf16(t):
    hi = t.astype(BF16)
    lo = (t - hi.astype(F32)).astype(BF16)
    return hi, lo


def _out_proj_kernel(ya_ref, yb_ref, yc_ref, yd_ref, x_ref, gt_ref, sc_ref, sh_ref, g_ref, w_ref,
                     rhi_ref, rlo_ref, rb_ref, x1_ref, h2_ref, route_ref):
    acc = jnp.zeros(x_ref.shape[1:], F32)
    for n, y_ref in enumerate((ya_ref, yb_ref, yc_ref, yd_ref)):
        acc += jnp.dot(y_ref[0].astype(BF16), w_ref[n * GROUP_W:(n + 1) * GROUP_W, :],
                       preferred_element_type=F32)
    x1 = x_ref[0] + gt_ref[0] * acc
    x1_ref[0] = x1
    y = x1 * lax.rsqrt(jnp.mean(x1 * x1, -1, keepdims=True) + NORM_EPS) * g_ref[...]
    h = y * (1.0 + sc_ref[0]) + sh_ref[0]
    hi, lo = _split_bf16(h)
    h2_ref[0] = hi
    lg = (jnp.dot(hi, rhi_ref[...], preferred_element_type=F32)
          + jnp.dot(lo, rhi_ref[...], preferred_element_type=F32)
          + jnp.dot(hi, rlo_ref[...], preferred_element_type=F32)) + rb_ref[...]
    route_ref[0] = _route(lg)


def _route(lg):
    lane = lax.broadcasted_iota(jnp.int32, lg.shape, 1)
    neg = -jnp.inf
    first = lambda hit: jnp.min(jnp.where(hit, lane, ROUTER_PAD), axis=1, keepdims=True)
    is_grp = lane < N_GROUPS
    grp = jnp.where(is_grp, lg, neg)
    g_max = jnp.max(grp, axis=1, keepdims=True)
    g_p = 1.0 / jnp.sum(jnp.where(is_grp, jnp.exp(grp - g_max), 0.0), axis=1, keepdims=True)
    g_idx = first(grp == g_max)
    e_lane = lane - N_GROUPS
    in_group = (e_lane >= 0) & (e_lane < N_EXPERTS) & (e_lane // EXP_PER_GROUP == g_idx)
    e_log = jnp.where(in_group, lg, neg)
    e1_max = jnp.max(e_log, axis=1, keepdims=True)
    e1_lane = first(e_log == e1_max)
    e_log2 = jnp.where(lane == e1_lane, neg, e_log)
    e2_max = jnp.max(e_log2, axis=1, keepdims=True)
    e2_lane = first(e_log2 == e2_max)
    ratio = jnp.exp(e2_max - e1_max)
    gate1 = g_p / (1.0 + ratio)
    gate2 = gate1 * ratio
    out = jnp.where(lane == 0, (e1_lane - N_GROUPS).astype(F32), 0.0)
    out = jnp.where(lane == 1, (e2_lane - N_GROUPS).astype(F32), out)
    out = jnp.where(lane == 2, gate1, out)
    return jnp.where(lane == 3, gate2, out)


def _out_proj(ys, x, gt, sc, sh, g, w_out, r_hi, r_lo, r_b):
    B, S, D = x.shape
    row = lambda w: pl.BlockSpec((1, OUT_ROWS, w), lambda b, i: (b, i, 0))
    vec = pl.BlockSpec((1, 1, D), lambda b, i: (b, 0, 0))
    full = lambda a: pl.BlockSpec(a.shape, lambda b, i: (0,) * a.ndim)
    return pl.pallas_call(
        _out_proj_kernel,
        grid=(B, S // OUT_ROWS),
        in_specs=[row(GROUP_W)] * 4 + [row(D), vec, vec, vec, full(g), full(w_out), full(r_hi), full(r_lo),
                                        full(r_b)],
        out_specs=[row(D), row(D), row(ROUTER_PAD)],
        out_shape=[jax.ShapeDtypeStruct((B, S, D), F32), jax.ShapeDtypeStruct((B, S, D), BF16),
                   jax.ShapeDtypeStruct((B, S, ROUTER_PAD), F32)],
        compiler_params=pltpu.CompilerParams(dimension_semantics=("parallel", "parallel"),
                                             vmem_limit_bytes=VMEM_LIMIT),
        name="out_proj",
    )(*ys, x, gt, sc, sh, g, w_out, r_hi, r_lo, r_b)


def _moe_ffn_kernel(blk_e_ref, x_ref, wt_ref, w1_ref, w3_ref, w2_ref, o_ref, w1b_ref, w3b_ref, w2b_ref):
    i = pl.program_id(0)
    changed = jnp.logical_or(i == 0, blk_e_ref[i] != blk_e_ref[jnp.maximum(i - 1, 0)])

    @pl.when(changed)
    def _():
        w1b_ref[...] = w1_ref[0, 0].astype(BF16)
        w3b_ref[...] = w3_ref[0, 0].astype(BF16)
        w2b_ref[...] = w2_ref[0, 0].astype(BF16)

    xb = x_ref[...]
    a = jnp.dot(xb, w1b_ref[...], preferred_element_type=F32)
    b = jnp.dot(xb, w3b_ref[...], preferred_element_type=F32)
    hmid = (a * jax.nn.sigmoid(a) * b).astype(BF16)
    y = jnp.dot(hmid, w2b_ref[...], preferred_element_type=F32)
    o_ref[...] = (y * wt_ref[...]).astype(o_ref.dtype)


def _moe_ffn(blk_e, xs, wt, w1, w3, w2, layer):
    n_slots, D = xs.shape
    n_blocks = n_slots // MOE_ROWS
    FF = w1.shape[-1]
    return pl.pallas_call(
        _moe_ffn_kernel,
        grid_spec=pltpu.PrefetchScalarGridSpec(
            num_scalar_prefetch=1,
            grid=(n_blocks,),
            in_specs=[pl.BlockSpec((MOE_ROWS, D), lambda i, e: (i, 0)),
                      pl.BlockSpec((MOE_ROWS, 1), lambda i, e: (i, 0)),
                      pl.BlockSpec((1, 1, D, FF), lambda i, e: (layer, e[i], 0, 0)),
                      pl.BlockSpec((1, 1, D, FF), lambda i, e: (layer, e[i], 0, 0)),
                      pl.BlockSpec((1, 1, FF, D), lambda i, e: (layer, e[i], 0, 0))],
            out_specs=pl.BlockSpec((MOE_ROWS, D), lambda i, e: (i, 0)),
            scratch_shapes=[pltpu.VMEM((D, FF), BF16), pltpu.VMEM((D, FF), BF16), pltpu.VMEM((FF, D), BF16)],
        ),
        out_shape=jax.ShapeDtypeStruct((n_slots, D), BF16),
        compiler_params=pltpu.CompilerParams(dimension_semantics=("arbitrary",),
                                             vmem_limit_bytes=VMEM_LIMIT),
        name="moe_ffn",
    )(blk_e, xs, wt, w1, w3, w2)


def _hier_moe(h2, route, w1, w3, w2, layer):
    N, D = h2.shape
    expert = route[:, :TOP_IN_GROUP].astype(jnp.int32)
    gate = route[:, TOP_IN_GROUP:2 * TOP_IN_GROUP]
    n_asg = N * TOP_IN_GROUP
    n_blocks = n_asg // MOE_ROWS + N_EXPERTS
    n_slots = n_blocks * MOE_ROWS
    n_fill = n_slots - n_asg
    flat_e = expert.reshape(n_asg // LANE, LANE)
    counts = jnp.sum(flat_e[None] == jnp.arange(N_EXPERTS)[:, None, None], axis=(1, 2)).astype(jnp.int32)
    pad_counts = (counts + MOE_ROWS - 1) // MOE_ROWS * MOE_ROWS
    pad_end = jnp.cumsum(pad_counts)
    blk_start = jnp.arange(n_blocks) * MOE_ROWS
    blk_e = jnp.minimum(jnp.sum(pad_end[None, :] <= blk_start[:, None], 1), N_EXPERTS - 1).astype(jnp.int32)
    fill_end = jnp.cumsum(pad_counts - counts)
    fill_id = jnp.arange(n_fill).reshape(n_fill // LANE, LANE)
    fill_e = jnp.zeros_like(fill_id)
    for e in range(N_EXPERTS):
        fill_e = fill_e + (fill_id >= fill_end[e])
    keys = jnp.concatenate([flat_e.reshape(n_asg) * 2, fill_e.reshape(n_fill) * 2 + 1])
    toks = jnp.concatenate([jnp.arange(n_asg, dtype=jnp.int32) // TOP_IN_GROUP, jnp.zeros((n_fill,), jnp.int32)])
    wts = jnp.concatenate([gate.reshape(n_asg), jnp.zeros((n_fill,), F32)])
    _, slot_tok, slot_w, slot_src = lax.sort((keys, toks, wts, jnp.arange(n_slots, dtype=jnp.int32)), num_keys=1)
    _, entry_slot = lax.sort((slot_src, jnp.arange(n_slots, dtype=jnp.int32)), num_keys=1)
    asg_slot = entry_slot[:n_asg].reshape(N, TOP_IN_GROUP)
    yb = _moe_ffn(blk_e, h2[slot_tok], slot_w[:, None], w1, w3, w2, layer)
    return yb[asg_slot[:, 0]], yb[asg_slot[:, 1]]


INT_MIN = -2 ** 31
DSA_KEY_STEP = 256


def _float_order_key(x):
    bits = pltpu.bitcast(x, jnp.int32)
    bits = jnp.where(x == 0.0, 0, bits)
    return bits ^ ((bits >> 31) & 0x7FFFFFFF)


COL_PART = 64


def _col_reduce(x, reduce):
    part = reduce(x.reshape(x.shape[0] // COL_PART, COL_PART, x.shape[1]), axis=0)
    return reduce(part, axis=0, keepdims=True)


def _col_count(mask):
    return _col_reduce(jnp.where(mask, 1.0, 0.0), jnp.sum)


def _head_block_diag(t, group):
    n_heads = t.shape[0] // group
    row_h = lax.broadcasted_iota(jnp.int32, t.shape, 0) // group
    return jnp.concatenate([jnp.where(row_h == h, t, 0.0) for h in range(n_heads)], axis=1)


def _dsa_block_t(qd_ref, kd4_ref, vdwt_ref, qi_ref, kihi_ref, kilo_ref, wi_ref, g_ref, o_ref, *, kl, n_sel):
    Q = DSA_BLOCK
    q0 = pl.program_id(1) * Q
    w_hi, w_lo = _split_bf16(_head_block_diag(jnp.transpose(qi_ref[0]), IDX_DIM))
    k_hi, k_lo = kihi_ref[0, :kl, :], kilo_ref[0, :kl, :]
    sc = _dot(jnp.concatenate([k_hi, k_lo, k_hi], axis=1),
              jnp.concatenate([w_hi, w_hi, w_lo], axis=0))
    wit = jnp.transpose(wi_ref[0])
    score = sum(wit[h:h + 1, :] * jnp.maximum(sc[:, h * Q:(h + 1) * Q], 0.0) for h in range(IDX_HEADS))
    kidx = lax.broadcasted_iota(jnp.int32, (kl, Q), 0)
    qpos = q0 + lax.broadcasted_iota(jnp.int32, (kl, Q), 1)
    adm = kidx <= qpos
    key = _float_order_key(jnp.where(adm, score, -jnp.inf))

    def value_bit(it, tau):
        cand = tau | jnp.left_shift(jnp.int32(1), 31 - it)
        return jnp.where(_col_count(key >= (cand ^ INT_MIN)) >= n_sel, cand, tau)

    tau = lax.fori_loop(0, 32, value_bit, jnp.zeros((1, Q), jnp.int32)) ^ INT_MIN
    gt = key > tau
    eq = (key == tau) & adm
    need = n_sel - _col_count(gt)
    n_eq = _col_count(eq)

    index_width = kl.bit_length()

    def index_bits():
        def index_bit(it, bound):
            cand = bound | jnp.left_shift(jnp.int32(1), index_width - 1 - it)
            return jnp.where(_col_count(eq & (kidx < cand)) <= need, cand, bound)
        return lax.fori_loop(0, index_width, index_bit, jnp.zeros((1, Q), jnp.int32))

    bound = lax.cond(jnp.max(n_eq - need) > 0.0, index_bits, lambda: jnp.full((1, Q), kl, jnp.int32))
    sel = gt | (eq & (kidx < bound))

    w_att = _head_block_diag(jnp.transpose(qd_ref[0]), HEAD_DIM).astype(BF16)
    lg = _dot(kd4_ref[0, :kl, :], w_att) * HEAD_DIM ** -0.5
    lg = jnp.where(jnp.concatenate([sel] * N_HEADS, axis=1), lg, -jnp.inf)
    p = jnp.exp(lg - _col_reduce(lg, jnp.max))
    out_t = _dot(vdwt_ref[0, :, :kl], p.astype(BF16)) / _col_reduce(p, jnp.sum)
    row_h = lax.broadcasted_iota(jnp.int32, (GROUP_W, Q), 0) // HEAD_DIM
    nat_t = sum(jnp.where(row_h == h, out_t[:, h * Q:(h + 1) * Q], 0.0) for h in range(N_HEADS))
    r = jnp.transpose(nat_t)
    hi_ = lax.broadcasted_iota(jnp.int32, (GROUP_W, GROUP_W), 0) // HEAD_DIM
    hj_ = lax.broadcasted_iota(jnp.int32, (GROUP_W, GROUP_W), 1) // HEAD_DIM
    ms = _dot_f32_by_exact(r * r, jnp.where(hi_ == hj_, 1.0, 0.0).astype(BF16)) * (1.0 / HEAD_DIM)
    o_ref[0] = r * lax.rsqrt(ms + NORM_EPS) * g_ref[...]


def _dsa_kernel(qd_ref, kd4_ref, vdwt_ref, qi_ref, kihi_ref, kilo_ref, wi_ref, g_ref, o_ref, *, kls, n_sel):
    blocks_per_step = DSA_KEY_STEP // DSA_BLOCK
    for j, kl in enumerate(kls):
        @pl.when(pl.program_id(1) // blocks_per_step == j)
        def _():
            _dsa_block_t(qd_ref, kd4_ref, vdwt_ref, qi_ref, kihi_ref, kilo_ref, wi_ref, g_ref, o_ref,
                         kl=kl, n_sel=n_sel)


DSA_PREP_ROWS = 256
D_Q, D_KV, D_QI, D_KW = 0, GROUP_W, GROUP_W + LANE, GROUP_W + 2 * LANE


def _swap_halves(x, half):
    n = x.shape[1]
    lane = lax.broadcasted_iota(jnp.int32, x.shape, 1)
    return jnp.where(lane % (2 * half) < half, pltpu.roll(x, n - half, axis=1), pltpu.roll(x, half, axis=1))


def _dsa_prep_kernel(p_ref, cq_ref, sq_ref, ci_ref, si_ref, gq_ref, gk_ref,
                     qd_ref, kd4_ref, vdwt_ref, qi_ref, kihi_ref, kilo_ref, wi_ref):
    GW = GROUP_W
    x = p_ref[0]
    hi_ = lax.broadcasted_iota(jnp.int32, (GW, GW), 0) // HEAD_DIM
    hj_ = lax.broadcasted_iota(jnp.int32, (GW, GW), 1) // HEAD_DIM
    ones_bd = jnp.where(hi_ == hj_, 1.0, 0.0).astype(BF16)
    q = x[:, D_Q:D_Q + GW]
    q = q * lax.rsqrt(_dot_f32_by_exact(q * q, ones_bd) * (1.0 / HEAD_DIM) + NORM_EPS) * gq_ref[...]
    qd_ref[0] = q * cq_ref[...] + _swap_halves(q, HEAD_DIM // 2) * sq_ref[...]
    kv = x[:, D_KV:D_KV + LANE]
    lane = lax.broadcasted_iota(jnp.int32, kv.shape, 1)
    is_k = lane < HEAD_DIM
    ms = jnp.sum(jnp.where(is_k, kv * kv, 0.0), axis=1, keepdims=True) * (1.0 / HEAD_DIM)
    kn = kv * lax.rsqrt(ms + NORM_EPS) * gk_ref[...]
    kr = kn * cq_ref[:, :LANE] + _swap_halves(kn, HEAD_DIM // 2) * sq_ref[:, :LANE]
    k2 = jnp.where(is_k, kr, pltpu.roll(kr, HEAD_DIM, axis=1))
    kd4_ref[0] = jnp.concatenate([k2] * (GW // LANE), axis=1).astype(BF16)
    v2 = jnp.where(is_k, pltpu.roll(kv, HEAD_DIM, axis=1), kv)
    v2t = jnp.transpose(v2)
    vdwt_ref[0] = jnp.concatenate([v2t] * (GW // LANE), axis=0).astype(BF16)
    qi = x[:, D_QI:D_QI + LANE]
    qi_ref[0] = qi * ci_ref[...] + _swap_halves(qi, IDX_DIM // 2) * si_ref[...]
    kw = x[:, D_KW:D_KW + LANE]
    kir = kw * ci_ref[...] + _swap_halves(kw, IDX_DIM // 2) * si_ref[...]
    ki1 = jnp.where(lane < IDX_DIM, kir, 0.0)
    ki2 = ki1 + pltpu.roll(ki1, IDX_DIM, axis=1)
    ki4 = ki2 + pltpu.roll(ki2, 2 * IDX_DIM, axis=1)
    kihi_ref[0], kilo_ref[0] = _split_bf16(ki4)
    wi_ref[0] = pltpu.roll(kw, LANE - IDX_DIM, axis=1) * (IDX_HEADS ** -0.5 * IDX_DIM ** -0.5)


def _rope_tables(S, dim, width):
    half = dim // 2
    inv = ROPE_THETA ** (-jnp.arange(half, dtype=F32) / half)
    ang = jnp.arange(S, dtype=F32)[:, None] * inv[None, :]
    cos = jnp.tile(jnp.cos(ang), (1, width // half))
    sin = jnp.tile(jnp.concatenate([-jnp.sin(ang), jnp.sin(ang)], axis=1), (1, width // dim))
    return cos, sin


def _dsa_prep(p, qn_g, kn_g):
    B, S, _ = p.shape
    GW, R = GROUP_W, DSA_PREP_ROWS
    cq, sq = _rope_tables(S, HEAD_DIM, GW)
    ci, si = _rope_tables(S, IDX_DIM, LANE)
    gq = jnp.tile(qn_g, N_HEADS)[None]
    gk = jnp.pad(kn_g, (0, LANE - HEAD_DIM))[None]
    rows = lambda w: pl.BlockSpec((1, R, w), lambda b, i: (b, i, 0))
    tab = lambda w: pl.BlockSpec((R, w), lambda b, i: (i, 0))
    cols = lambda r: pl.BlockSpec((1, r, R), lambda b, i: (b, 0, i))
    vec = lambda w: pl.BlockSpec((1, w), lambda b, i: (0, 0))
    return pl.pallas_call(
        _dsa_prep_kernel,
        grid=(B, S // R),
        in_specs=[rows(D_PAD), tab(GW), tab(GW), tab(LANE), tab(LANE), vec(GW), vec(LANE)],
        out_specs=[rows(GW), rows(GW), cols(GW), rows(LANE), rows(LANE), rows(LANE), rows(LANE)],
        out_shape=[jax.ShapeDtypeStruct((B, S, GW), F32), jax.ShapeDtypeStruct((B, S, GW), BF16),
                   jax.ShapeDtypeStruct((B, GW, S), BF16), jax.ShapeDtypeStruct((B, S, LANE), F32),
                   jax.ShapeDtypeStruct((B, S, LANE), BF16), jax.ShapeDtypeStruct((B, S, LANE), BF16),
                   jax.ShapeDtypeStruct((B, S, LANE), F32)],
        compiler_params=pltpu.CompilerParams(dimension_semantics=("parallel", "parallel"),
                                             vmem_limit_bytes=VMEM_LIMIT),
        name="dsa_prep",
    )(p, cq, sq, ci, si, gq, gk)


def _dsa_attn_norm(qd, kd4, vdwt, qi, ki_hi, ki_lo, wi, g):
    B, S, _ = qd.shape
    n_sel = min(TOPK_MAX, S // 4)
    assert S % DSA_KEY_STEP == 0 and n_sel <= DSA_KEY_STEP
    kls = tuple(range(DSA_KEY_STEP, S + 1, DSA_KEY_STEP))
    blk = lambda w: pl.BlockSpec((1, DSA_BLOCK, w), lambda b, i: (b, i, 0))
    per_b = lambda r, c: pl.BlockSpec((1, r, c), lambda b, i: (b, 0, 0))
    return pl.pallas_call(
        functools.partial(_dsa_kernel, kls=kls, n_sel=n_sel),
        grid=(B, S // DSA_BLOCK),
        in_specs=[blk(GROUP_W), per_b(S, GROUP_W), per_b(GROUP_W, S), blk(IDX_HEADS * IDX_DIM),
                  per_b(S, LANE), per_b(S, LANE), blk(LANE), pl.BlockSpec((1, GROUP_W), lambda b, i: (0, 0))],
        out_specs=blk(GROUP_W),
        out_shape=jax.ShapeDtypeStruct((B, S, GROUP_W), F32),
        compiler_params=pltpu.CompilerParams(dimension_semantics=("parallel", "parallel"),
                                             vmem_limit_bytes=VMEM_LIMIT),
        name="dsa_attn",
    )(qd, kd4, vdwt, qi, ki_hi, ki_lo, wi, g[None])


RWKV_CHUNK = 64
RWKV_LOW = RWKV_W_RANK + RWKV_A_RANK + RWKV_G_RANK
RWKV_GROUP = N_HEADS
RWKV_BATCH = 8


def _dot(a, b):
    return jnp.dot(a, b, preferred_element_type=F32)


def _dot_nt(a, b):
    return lax.dot_general(a, b, (((1,), (1,)), ((), ())), preferred_element_type=F32)


def _dot_tn(a, b):
    return lax.dot_general(a, b, (((0,), (0,)), ((), ())), preferred_element_type=F32)


def _split3_bf16(t):
    p1 = t.astype(BF16)
    r1 = t - p1.astype(F32)
    p2 = r1.astype(BF16)
    p3 = (r1 - p2.astype(F32)).astype(BF16)
    return p1, p2, p3


def _dot_f32_by_exact(a, b_exact):
    m = a.shape[0]
    r = _dot(jnp.concatenate(_split3_bf16(a), axis=0), b_exact)
    return r[:m] + r[m:2 * m] + r[2 * m:]


def _dot_exact_by_f32(a_exact, b):
    n = b.shape[1]
    r = _dot(a_exact, jnp.concatenate(_split3_bf16(b), axis=1))
    return r[:, :n] + r[:, n:2 * n] + r[:, 2 * n:]


def _dot3(a, b_hi, b_lo):
    a_hi, a_lo = _split_bf16(a)
    return _dot(jnp.concatenate([a_hi, a_lo, a_hi], axis=1), jnp.concatenate([b_hi, b_hi, b_lo], axis=0))


def _softplus(z):
    return jnp.maximum(z, 0.0) + jnp.log(1.0 + jnp.exp(-jnp.abs(z)))


def _rwkv_kernel(p_ref, mu_ref, vec_ref, lhi_ref, llo_ref, o_ref, state_ref, prev_ref):
    @pl.when(pl.program_id(1) == 0)
    def _():
        state_ref[...] = jnp.zeros_like(state_ref)
        prev_ref[...] = jnp.zeros_like(prev_ref)

    masks = _rwkv_masks()
    chains = [_rwkv_chunk(masks, p_ref.at[n], mu_ref, vec_ref, lhi_ref, llo_ref, o_ref.at[n], state_ref.at[n],
                          prev_ref.at[n]) for n in range(RWKV_BATCH)]
    for _ in itertools.zip_longest(*chains):
        pass


def _rwkv_masks():
    L, GW, SG = RWKV_CHUNK, GROUP_W, RWKV_GROUP * RWKV_CHUNK
    ri = lax.broadcasted_iota(jnp.int32, (GW, GW), 0)
    ci = lax.broadcasted_iota(jnp.int32, (GW, GW), 1)
    ones_bd = jnp.where((ri // HEAD_DIM) == (ci // HEAD_DIM), 1.0, 0.0).astype(BF16)
    ti = lax.broadcasted_iota(jnp.int32, (L, L), 0)
    tj = lax.broadcasted_iota(jnp.int32, (L, L), 1)
    tri = jnp.where(tj <= ti, 1.0, 0.0).astype(BF16)
    gi = lax.broadcasted_iota(jnp.int32, (SG, SG), 0)
    gj = lax.broadcasted_iota(jnp.int32, (SG, SG), 1)
    g_same = (gi // L) == (gj // L)
    strict = g_same & ((gj % L) < (gi % L))
    incl = g_same & ((gj % L) <= (gi % L))
    diag = gi == gj
    return ones_bd, tri, g_same, strict, incl, diag, jnp.where(diag, 1.0, 0.0)


def _rwkv_chunk(masks, p_ref, mu_ref, vec_ref, lhi_ref, llo_ref, o_ref, state_ref, prev_ref):
    L, GW = RWKV_CHUNK, GROUP_W
    p = p_ref[...]
    row = lax.broadcasted_iota(jnp.int32, p.shape, 0)
    prev = jnp.where(row == 0, prev_ref[...], pltpu.roll(p, 1, axis=0))
    prev_ref[...] = p[L - 1:L, :]
    ps = p + (prev - p) * mu_ref[...]
    r, k, v = ps[:, :GW], ps[:, GW:2 * GW], ps[:, 2 * GW:3 * GW]
    low = ps[:, 3 * GW:]
    lane_low = lax.broadcasted_iota(jnp.int32, low.shape, 1)
    low = jnp.where(lane_low < RWKV_W_RANK, jnp.tanh(low),
                    jnp.where(lane_low < RWKV_W_RANK + RWKV_A_RANK, low, jax.nn.sigmoid(low)))
    up = _dot3(low, lhi_ref[...], llo_ref[...])
    w0, a0, k_k, k_a = vec_ref[0:1, :], vec_ref[1:2, :], vec_ref[2:3, :], vec_ref[3:4, :]
    r_k, ln_g, ln_b = vec_ref[4:5, :], vec_ref[5:6, :], vec_ref[6:7, :]
    logw = -jnp.exp(-_softplus(-(w0 + up[:, :GW])) - 0.5)
    rate = jax.nn.sigmoid(a0 + up[:, GW:2 * GW])
    gate = up[:, 2 * GW:]

    ones_bd, tri, g_same, strict, incl, diag, eye = masks
    kk = k * k_k
    k = k * (1.0 + (rate - 1.0) * k_a)
    seg = _dot_f32_by_exact(jnp.concatenate([kk * kk, r * k * r_k], axis=0), ones_bd)
    kk = kk / jnp.maximum(jnp.sqrt(seg[:L]), 1e-12)

    lc = _dot_exact_by_f32(tri, logw)
    lc_last = lc[L - 1:L, :]
    dec_in = jnp.exp(lc)
    dec_out = jnp.exp(-lc)
    a_t = -kk * jnp.exp(lc - logw)
    b_t = kk * rate * dec_out
    k_t = k * dec_out
    r_t = r * dec_in
    to_end = jnp.exp(lc_last)

    SG = RWKV_GROUP * L
    stack = lambda t: jnp.concatenate([t] * RWKV_GROUP, axis=0)
    bd = lambda t: jnp.where(g_same, stack(t), 0.0).astype(BF16)
    n_doublings = RWKV_CHUNK.bit_length() - 2
    ys = []
    for g in range(N_HEADS // RWKV_GROUP):
        cols = slice(g * SG, (g + 1) * SG)
        a_bd, r_bd, v_bd = bd(a_t[:, cols]), bd(r_t[:, cols]), bd(v[:, cols])
        m = _dot_nt(jnp.concatenate([a_bd, r_bd], axis=0),
                    jnp.concatenate([stack(b_t[:, cols]), stack(k_t[:, cols])], axis=0).astype(BF16))
        yield
        m_ab = jnp.where(strict, m[:SG, :SG], 0.0)
        m_ak = jnp.where(strict, m[:SG, SG:], 0.0).astype(BF16)
        m_rb = jnp.where(incl, m[SG:, :SG], 0.0).astype(BF16)
        m_rk = jnp.where(incl, m[SG:, SG:], 0.0).astype(BF16)

        inv = eye + m_ab
        pw_b = m_ab.astype(BF16)
        sq = _dot(pw_b, pw_b)
        yield
        pw_b = sq.astype(BF16)
        for s in range(n_doublings - 1):
            both = _dot(jnp.concatenate([inv.astype(BF16), pw_b], axis=0), pw_b)
            yield
            inv = inv + both[:SG]
            pw_b = both[SG:].astype(BF16)
        last = _dot(inv.astype(BF16), pw_b)
        t0 = state_ref[g]
        t0_b = t0.astype(BF16)
        rhs = _dot(jnp.concatenate([a_bd, m_ak], axis=1), jnp.concatenate([t0_b, v_bd], axis=0))
        yield
        inv_b = (inv + last).astype(BF16)
        u = _dot(inv_b, rhs.astype(BF16))
        yield
        u = u.astype(BF16)
        y_bd = _dot(jnp.concatenate([r_bd, m_rb, m_rk], axis=1),
                    jnp.concatenate([t0_b, u, v_bd], axis=0))
        end_g = to_end[:, cols]
        to_end_col = jnp.sum(jnp.where(diag, jnp.broadcast_to(end_g, (SG, SG)), 0.0), axis=1, keepdims=True)
        carried = _dot_tn(jnp.concatenate([bd(b_t[:, cols] * end_g), bd(k_t[:, cols] * end_g)], axis=0),
                          jnp.concatenate([u, v_bd], axis=0))
        yield
        ys.append(sum(y_bd[h * L:(h + 1) * L, :] for h in range(RWKV_GROUP)))
        state_ref[g] = to_end_col * t0 + carried
    y = jnp.concatenate(ys, axis=1)

    inv_d = 1.0 / HEAD_DIM
    mean = _dot_f32_by_exact(y, ones_bd) * inv_d
    yield
    yc = y - mean
    var = _dot_f32_by_exact(yc * yc, ones_bd) * inv_d
    yield
    yn = yc * lax.rsqrt(var + RWKV_GN_EPS) * ln_g + ln_b
    o_ref[...] = (yn + seg[L:] * v) * gate


def _rwkv7_time_mix(p, mu, w0, w2, a0, a2, g2, k_k, k_a, r_k, ln_g, ln_b):
    B, S, _ = p.shape
    GW = GROUP_W
    assert S % RWKV_CHUNK == 0 and RWKV_CHUNK == HEAD_DIM and B % RWKV_BATCH == 0
    low_w = jnp.zeros((RWKV_LOW, 3 * GW), F32)
    low_w = low_w.at[:RWKV_W_RANK, :GW].set(w2)
    low_w = low_w.at[RWKV_W_RANK:RWKV_W_RANK + RWKV_A_RANK, GW:2 * GW].set(a2)
    low_w = low_w.at[RWKV_W_RANK + RWKV_A_RANK:, 2 * GW:].set(g2)
    l_hi, l_lo = _split_bf16(low_w)
    vecs = jnp.stack([w0, a0, k_k, k_a, r_k, ln_g, ln_b, jnp.zeros_like(w0)], 0)
    full = lambda a: pl.BlockSpec(a.shape, lambda b, c: (0,) * a.ndim)
    mu2 = mu[None]
    return pl.pallas_call(
        _rwkv_kernel,
        grid=(B // RWKV_BATCH, S // RWKV_CHUNK),
        in_specs=[pl.BlockSpec((RWKV_BATCH, RWKV_CHUNK, A_PAD), lambda b, c: (b, c, 0)),
                  full(mu2), full(vecs), full(l_hi), full(l_lo)],
        out_specs=pl.BlockSpec((RWKV_BATCH, RWKV_CHUNK, GW), lambda b, c: (b, c, 0)),
        out_shape=jax.ShapeDtypeStruct((B, S, GW), F32),
        scratch_shapes=[pltpu.VMEM((RWKV_BATCH, N_HEADS // RWKV_GROUP, RWKV_GROUP * RWKV_CHUNK,
                                    RWKV_GROUP * HEAD_DIM), F32),
                        pltpu.VMEM((RWKV_BATCH, 1, A_PAD), F32)],
        compiler_params=pltpu.CompilerParams(dimension_semantics=("parallel", "arbitrary"),
                                             vmem_limit_bytes=VMEM_LIMIT),
        name="rwkv7",
    )(p, mu2, vecs, l_hi, l_lo)


SB_UNROLL = 4


def _sb_kernel(q_ref, k_ref, v_ref, g_ref, o_ref, kbd_ref, vbd_ref):
    i = pl.program_id(1)
    T, GW, H = SB_BLOCK, GROUP_W, N_HEADS
    lane_h = lax.broadcasted_iota(jnp.int32, (T, GW), 1) // HEAD_DIM
    k_new, v_new = k_ref[0], v_ref[0]
    for h in range(H):
        kbd_ref[i, h * T:(h + 1) * T, :] = jnp.where(lane_h == h, k_new, 0.0).astype(BF16)
        vbd_ref[i, h * T:(h + 1) * T, :] = jnp.where(lane_h == h, v_new, 0.0).astype(BF16)

    q = (q_ref[0] * HEAD_DIM ** -0.5).astype(BF16)
    si = lax.broadcasted_iota(jnp.int32, (T, 2 * T), 0)
    sj = lax.broadcasted_iota(jnp.int32, (T, 2 * T), 1)
    later_and_all = jnp.where((si > sj) | (sj >= T), 1.0, 0.0).astype(BF16)
    qrow = lax.broadcasted_iota(jnp.int32, (T, H * T), 0)
    kcol = lax.broadcasted_iota(jnp.int32, (T, H * T), 1) % T
    causal = kcol < qrow

    def key_block_stages(j, box, diagonal):
        z = _dot_nt(q, kbd_ref[j])
        yield
        log1m = -(jnp.maximum(z, 0.0) + jnp.log(1.0 + jnp.exp(-jnp.abs(z))))
        log_sig = z + log1m
        log1m_in = (jnp.where(causal, log1m, 0.0) if diagonal else log1m).astype(BF16)
        sums = _dot(jnp.concatenate([log1m_in[:, h * T:(h + 1) * T] for h in range(H)], axis=0), later_and_all)
        yield
        suffix = jnp.concatenate([sums[h * T:(h + 1) * T, :T] for h in range(H)], axis=1)
        total = jnp.concatenate([sums[h * T:(h + 1) * T, T:] for h in range(H)], axis=1)
        att = jnp.exp(log_sig + suffix + box["carry"])
        if diagonal:
            att = jnp.where(causal, att, 0.0)
        box["carry"] = box["carry"] + total
        pv = _dot(att.astype(BF16), vbd_ref[j])
        yield
        box["acc"] = box["acc"] + pv

    def key_blocks(js, state, diagonal=False):
        box = {"carry": state[0], "acc": state[1]}
        for _ in itertools.zip_longest(*[key_block_stages(j, box, diagonal) for j in js]):
            pass
        return box["carry"], box["acc"]

    state = key_blocks([i], (jnp.zeros((T, H * T), F32), jnp.zeros((T, GW), F32)), True)
    rem = i % SB_UNROLL
    state = lax.fori_loop(0, rem, lambda it, st: key_blocks([i - 1 - it], st), state)
    top = i - 1 - rem
    _, y = lax.fori_loop(0, i // SB_UNROLL,
                         lambda it, st: key_blocks([top - SB_UNROLL * it - n for n in range(SB_UNROLL)], st), state)

    hi_ = lax.broadcasted_iota(jnp.int32, (GW, GW), 0) // HEAD_DIM
    hj_ = lax.broadcasted_iota(jnp.int32, (GW, GW), 1) // HEAD_DIM
    ones_bd = jnp.where(hi_ == hj_, 1.0, 0.0).astype(BF16)
    ms = _dot_f32_by_exact(y * y, ones_bd) * (1.0 / HEAD_DIM)
    o_ref[0] = y * lax.rsqrt(ms + NORM_EPS) * g_ref[...]


def _stick_breaking_norm(p, g):
    B, S, _ = p.shape
    GW = GROUP_W
    assert S % SB_BLOCK == 0 and HEAD_DIM ** -0.5 == 2.0 ** -(HEAD_DIM.bit_length() // 2)
    col = lambda n: pl.BlockSpec((1, SB_BLOCK, GW), lambda b, i: (b, i, n))
    stacked = pltpu.VMEM((S // SB_BLOCK, N_HEADS * SB_BLOCK, GW), BF16)
    return pl.pallas_call(
        _sb_kernel,
        grid=(B, S // SB_BLOCK),
        in_specs=[col(0), col(1), col(2), pl.BlockSpec((1, GW), lambda b, i: (0, 0))],
        out_specs=col(0),
        out_shape=jax.ShapeDtypeStruct((B, S, GW), F32),
        scratch_shapes=[stacked, stacked],
        compiler_params=pltpu.CompilerParams(dimension_semantics=("parallel", "arbitrary"),
                                             vmem_limit_bytes=VMEM_LIMIT),
        name="stick_breaking",
    )(p, p, p, g[None])


ML_HALO = 8
ML_GROUP = N_HEADS
ML_BATCH = 4


def _dot_nt_exact_by_f32(a_exact, b):
    return _dot_nt(jnp.concatenate([a_exact] * 3, axis=1), jnp.concatenate(_split3_bf16(b), axis=1))


def _mlstm_kernel(p_ref, cw_ref, cb_ref, gb_ref, g_ref, o_ref, ext_ref, ct_ref, n_ref, m_ref):
    @pl.when(pl.program_id(1) == 0)
    def _():
        ext_ref[...] = jnp.zeros_like(ext_ref)
        ct_ref[...] = jnp.zeros_like(ct_ref)
        n_ref[...] = jnp.zeros_like(n_ref)
        m_ref[...] = jnp.zeros_like(m_ref)

    masks = _mlstm_masks()
    chains = [_mlstm_chunk(masks, p_ref.at[n], cw_ref, cb_ref, gb_ref, g_ref, o_ref.at[n], ext_ref.at[n],
                           ct_ref.at[n], n_ref.at[n], m_ref.at[n]) for n in range(ML_BATCH)]
    for _ in itertools.zip_longest(*chains):
        pass


def _mlstm_masks():
    L, GW, H, SG = ML_CHUNK, GROUP_W, N_HEADS, ML_GROUP * ML_CHUNK
    gi = lax.broadcasted_iota(jnp.int32, (LANE, 2 * GW), 0)
    gj = lax.broadcasted_iota(jnp.int32, (LANE, 2 * GW), 1)
    expand = jnp.where(gi == (gj % GW) // HEAD_DIM + H * (gj // GW), 1.0, 0.0).astype(BF16)
    ti = lax.broadcasted_iota(jnp.int32, (L, L), 0)
    tj = lax.broadcasted_iota(jnp.int32, (L, L), 1)
    tri = jnp.where(tj <= ti, 1.0, 0.0).astype(BF16)
    ri = lax.broadcasted_iota(jnp.int32, (SG, SG), 0)
    ci = lax.broadcasted_iota(jnp.int32, (SG, SG), 1)
    same_head = (ri // L) == (ci // HEAD_DIM)
    first_lane = ci == (ri // L) * HEAD_DIM
    sel_first = jnp.where(first_lane, 1.0, 0.0).astype(BF16)
    causal = lax.broadcasted_iota(jnp.int32, (SG, L), 1) <= lax.broadcasted_iota(jnp.int32, (SG, L), 0) % L
    hi_ = lax.broadcasted_iota(jnp.int32, (GW, GW), 0) // HEAD_DIM
    hj_ = lax.broadcasted_iota(jnp.int32, (GW, GW), 1) // HEAD_DIM
    ones_bd = jnp.where(hi_ == hj_, 1.0, 0.0).astype(BF16)
    return expand, tri, same_head, first_lane, sel_first, causal, ones_bd


def _mlstm_chunk(masks, p_ref, cw_ref, cb_ref, gb_ref, g_ref, o_ref, ext_ref, ct_ref, n_ref, m_ref):
    L, GW, H = ML_CHUNK, GROUP_W, N_HEADS
    x = p_ref[...]
    ext_ref[ML_HALO:, :] = x[:, :2 * GW]
    conv = cb_ref[...]
    for j in range(ML_CONV):
        conv = conv + cw_ref[j:j + 1, :] * ext_ref[pl.ds(ML_HALO - (ML_CONV - 1) + j, L), :]
    ext_ref[:ML_HALO, :] = x[L - ML_HALO:, :2 * GW]
    qk = conv * jax.nn.sigmoid(conv)
    q, k = qk[:, :GW], qk[:, GW:] * HEAD_DIM ** -0.5
    v, o = x[:, 2 * GW:3 * GW], x[:, 3 * GW:4 * GW]

    gates = x[:, 4 * GW:]
    expand, tri, same_head, first_lane, sel_first, causal, ones_bd = masks
    graw = _dot_f32_by_exact(gates, expand) + gb_ref[...]
    yield
    capped = GATE_CAP * jnp.tanh(graw * (1.0 / GATE_CAP))
    log_i = capped[:, :GW]
    cf = capped[:, GW:]
    log_f = jnp.minimum(cf, 0.0) - jnp.log(1.0 + jnp.exp(-jnp.abs(cf)))

    bf = _dot_exact_by_f32(tri, log_f)
    yield
    b_last = bf[L - 1:L, :]
    m_row, n_row = m_ref[...], n_ref[...]
    dec = b_last - bf + log_i
    m_new = jnp.maximum(b_last + m_row, jnp.max(dec, axis=0, keepdims=True))
    kw = k * jnp.exp(dec - m_new)
    s_old = jnp.exp(b_last + m_row - m_new)
    n_ref[...] = s_old * n_row + jnp.sum(kw, axis=0, keepdims=True)
    m_ref[...] = m_new
    g_in = bf + m_row
    li_b = log_i - bf

    SG = ML_GROUP * L
    stack = lambda t: jnp.concatenate([t] * ML_GROUP, axis=0)
    pick = lambda t: jnp.sum(jnp.where(first_lane, t, 0.0), axis=1, keepdims=True)
    hs = []
    for g in range(H // ML_GROUP):
        cols = slice(g * SG, (g + 1) * SG)
        ct = ct_ref[g]
        qs = jnp.where(same_head, stack(q[:, cols]), 0.0)
        qs_b = qs.astype(BF16)
        v_b = v[:, cols].astype(BF16)
        b_col = pick(stack(bf[:, cols]))
        g_col = pick(stack(g_in[:, cols]))
        row_part = _dot_nt_exact_by_f32(sel_first, li_b[:, cols])
        qk = _dot_nt(qs_b, k[:, cols].astype(BF16))
        inter = _dot(qs_b, ct.astype(BF16))
        carried = _dot_tn(kw[:, cols].astype(BF16), v_b)
        yield
        dmat = jnp.where(causal, b_col + row_part, -jnp.inf)
        m_t = jnp.maximum(g_col, jnp.max(dmat, axis=1, keepdims=True))
        s_inter = jnp.exp(g_col - m_t)
        sqk = qk * jnp.exp(dmat - m_t)
        intra = _dot(sqk.astype(BF16), v_b)
        yield
        num = s_inter * inter + jnp.where(same_head, intra, 0.0)
        den = (s_inter * jnp.sum(qs * n_row[:, cols], axis=1, keepdims=True)
               + jnp.sum(sqk, axis=1, keepdims=True))
        hst = num / jnp.maximum(jnp.abs(den), jnp.exp(-m_t))
        hs.append(sum(hst[n * L:(n + 1) * L, :] for n in range(ML_GROUP)))
        ct_ref[g] = s_old[:, cols] * ct + jnp.where(same_head, carried, 0.0)
    h = jnp.concatenate(hs, axis=1)

    ms = _dot_f32_by_exact(h * h, ones_bd) * (1.0 / HEAD_DIM)
    yield
    o_ref[...] = jax.nn.sigmoid(o) * (h * lax.rsqrt(ms + NORM_EPS) * g_ref[...])


def _mlstm_mix(p, conv_w, conv_b, ig_b, fg_b, norm_g):
    B, S, _ = p.shape
    GW = GROUP_W
    assert S % ML_CHUNK == 0 and ML_CONV - 1 <= ML_HALO <= ML_CHUNK and B % ML_BATCH == 0
    gate_b = jnp.concatenate([jnp.repeat(ig_b, HEAD_DIM), jnp.repeat(fg_b, HEAD_DIM)])[None]
    full = lambda a: pl.BlockSpec(a.shape, lambda b, c: (0,) * a.ndim)
    cb2, g2 = conv_b[None], norm_g[None]
    return pl.pallas_call(
        _mlstm_kernel,
        grid=(B // ML_BATCH, S // ML_CHUNK),
        in_specs=[pl.BlockSpec((ML_BATCH, ML_CHUNK, C_PAD), lambda b, c: (b, c, 0)),
                  full(conv_w), full(cb2), full(gate_b), full(g2)],
        out_specs=pl.BlockSpec((ML_BATCH, ML_CHUNK, GW), lambda b, c: (b, c, 0)),
        out_shape=jax.ShapeDtypeStruct((B, S, GW), F32),
        scratch_shapes=[pltpu.VMEM((ML_BATCH, ML_HALO + ML_CHUNK, 2 * GW), F32),
                        pltpu.VMEM((ML_BATCH, N_HEADS // ML_GROUP, ML_GROUP * ML_CHUNK, ML_GROUP * HEAD_DIM), F32),
                        pltpu.VMEM((ML_BATCH, 1, GW), F32), pltpu.VMEM((ML_BATCH, 1, GW), F32)],
        compiler_params=pltpu.CompilerParams(dimension_semantics=("parallel", "arbitrary"),
                                             vmem_limit_bytes=VMEM_LIMIT),
        name="mlstm",
    )(p, conv_w, cb2, gate_b, g2)


def kernel(x, c, ada_w, ada_b, norm1_g, norm2_g, w_in, rk_mu, rk_w0, rk_w2, rk_a0, rk_a2, rk_g2, rk_kk, rk_ka, rk_rk, rk_ln_g, rk_ln_b, sb_norm_g, ml_conv_w, ml_conv_b, ml_ig_b, ml_fg_b, ml_norm_g, ds_qn_g, ds_kn_g, ds_out_g, w_out, moe_wg, moe_bg, moe_we, moe_be, moe_w1, moe_w3, moe_w2):
    B, S, D = x.shape
    depth = ada_w.shape[0]
    c_act = jax.nn.silu(c)
    for l in range(depth):
        mod = (c_act @ ada_w[l] + ada_b[l])[:, None, :]
        sh1, sc1, gt1, sh2, sc2, gt2 = jnp.split(mod, 6, axis=-1)

        pA, pB, pC, pD = _in_proj(x, sc1, sh1, norm1_g[l][None], _pad_w_in(w_in[l]))

        yA = _rwkv7_time_mix(pA, rk_mu[l], rk_w0[l], rk_w2[l], rk_a0[l], rk_a2[l], rk_g2[l],
                             rk_kk[l], rk_ka[l], rk_rk[l], rk_ln_g[l], rk_ln_b[l])

        yB = _stick_breaking_norm(pB, sb_norm_g[l])

        yC = _mlstm_mix(pC, ml_conv_w[l], ml_conv_b[l], ml_ig_b[l], ml_fg_b[l], ml_norm_g[l])

        yD = _dsa_attn_norm(*_dsa_prep(pD, ds_qn_g[l], ds_kn_g[l]), ds_out_g[l])

        router = jnp.pad(jnp.concatenate([moe_wg[l], moe_we[l]], 1),
                         ((0, 0), (0, ROUTER_PAD - N_GROUPS - N_EXPERTS)))
        r_hi, r_lo = _split_bf16(router)
        r_b = jnp.pad(jnp.concatenate([moe_bg[l], moe_be[l]]), (0, ROUTER_PAD - N_GROUPS - N_EXPERTS))[None]
        x1, h2, route = _out_proj((yA, yB, yC, yD), x, gt1, sc2, sh2, norm2_g[l][None],
                                  w_out[l].astype(BF16), r_hi, r_lo, r_b)

        y1, y2 = _hier_moe(h2.reshape(B * S, D), route.reshape(B * S, ROUTER_PAD), moe_w1, moe_w3, moe_w2, l)
        x = x1 + gt2 * (y1.astype(F32) + y2.astype(F32)).reshape(B, S, D)
    return x
```

```python
import functools
import itertools

import jax
import jax.numpy as jnp
import numpy as np
from jax import lax
from jax.experimental import pallas as pl
from jax.experimental.pallas import tpu as pltpu

F32 = jnp.float32
BF16 = jnp.bfloat16

D_MODEL = 1024
N_MIXERS = 4
GROUP_W = D_MODEL // N_MIXERS
HEAD_DIM = 64
N_HEADS = GROUP_W // HEAD_DIM
NORM_EPS = 1e-6
RWKV_W_RANK = 32
RWKV_A_RANK = 32
RWKV_G_RANK = 64
RWKV_GN_EPS = 64e-5
SB_BLOCK = 128
ML_CHUNK = 64
ML_CONV = 4
GATE_CAP = 15.0
IDX_HEADS = 4
IDX_DIM = 32
TOPK_MAX = 256
ROPE_THETA = 10000.0
N_GROUPS = 4
EXP_PER_GROUP = 8
N_EXPERTS = N_GROUPS * EXP_PER_GROUP
EXPERT_FF = D_MODEL // 2
TOP_IN_GROUP = 2

A_SIZES = (GROUP_W, GROUP_W, GROUP_W, RWKV_W_RANK, RWKV_A_RANK, RWKV_G_RANK)
B_SIZES = (GROUP_W, GROUP_W, GROUP_W)
C_SIZES = (GROUP_W, GROUP_W, GROUP_W, GROUP_W, N_HEADS, N_HEADS)
D_SIZES = (GROUP_W, HEAD_DIM, HEAD_DIM, IDX_HEADS * IDX_DIM, IDX_DIM, IDX_HEADS)
A_COLS = sum(A_SIZES)
B_COLS = sum(B_SIZES)
C_COLS = sum(C_SIZES)
D_COLS = sum(D_SIZES)

LANE = 128
A_PAD = 896
B_PAD = 768
C_PAD = 1152
D_PAD = 640
P_PAD = A_PAD + B_PAD + C_PAD + D_PAD
ROUTER_PAD = LANE

IN_ROWS = 512
OUT_ROWS = 512
MOE_ROWS = 512
VMEM_LIMIT = 48 * 1024 * 1024


def _split_cols(t, sizes):
    return jnp.split(t, [int(i) for i in np.cumsum(sizes)[:-1]], axis=-1)


def _in_proj_kernel(x_ref, sc_ref, sh_ref, g_ref, w_ref, oa_ref, ob_ref, oc_ref, od_ref):
    x = x_ref[0]
    y = x * lax.rsqrt(jnp.mean(x * x, -1, keepdims=True) + NORM_EPS) * g_ref[...]
    h = y * (1.0 + sc_ref[0]) + sh_ref[0]
    p = jnp.dot(h.astype(BF16), w_ref[...], preferred_element_type=F32)
    oa_ref[0] = p[:, :A_PAD]
    ob_ref[0] = p[:, A_PAD:A_PAD + B_PAD]
    oc_ref[0] = p[:, A_PAD + B_PAD:A_PAD + B_PAD + C_PAD]
    od_ref[0] = p[:, A_PAD + B_PAD + C_PAD:]


def _in_proj(x, sc, sh, g, w_pad):
    B, S, D = x.shape
    row = lambda w: pl.BlockSpec((1, IN_ROWS, w), lambda b, i: (b, i, 0))
    vec = pl.BlockSpec((1, 1, D), lambda b, i: (b, 0, 0))
    return pl.pallas_call(
        _in_proj_kernel,
        grid=(B, S // IN_ROWS),
        in_specs=[row(D), vec, vec, pl.BlockSpec((1, D), lambda b, i: (0, 0)),
                  pl.BlockSpec((D, P_PAD), lambda b, i: (0, 0))],
        out_specs=[row(A_PAD), row(B_PAD), row(C_PAD), row(D_PAD)],
        out_shape=[jax.ShapeDtypeStruct((B, S, w), F32) for w in (A_PAD, B_PAD, C_PAD, D_PAD)],
        compiler_params=pltpu.CompilerParams(dimension_semantics=("parallel", "parallel"),
                                             vmem_limit_bytes=VMEM_LIMIT),
        name="in_proj",
    )(x, sc, sh, g, w_pad)


def _pad_w_in(w):
    wa, wb, wc, wd = _split_cols(w, (A_COLS, B_COLS, C_COLS, D_COLS))
    padc = lambda t, n: jnp.pad(t, ((0, 0), (0, n - t.shape[1])))
    return jnp.concatenate([padc(wa, A_PAD), padc(wb, B_PAD), padc(wc, C_PAD), padc(wd, D_PAD)], 1).astype(BF16)


def _split_bf16(t):
    hi = t.astype(BF16)
    lo = (t - hi.astype(F32)).astype(BF16)
    return hi, lo


def _out_proj_kernel(ya_ref, yb_ref, yc_ref, yd_ref, x_ref, gt_ref, sc_ref, sh_ref, g_ref, w_ref,
                     rhi_ref, rlo_ref, rb_ref, x1_ref, h2_ref, route_ref):
    acc = jnp.zeros(x_ref.shape[1:], F32)
    for n, y_ref in enumerate((ya_ref, yb_ref, yc_ref, yd_ref)):
        acc += jnp.dot(y_ref[0].astype(BF16), w_ref[n * GROUP_W:(n + 1) * GROUP_W, :],
                       preferred_element_type=F32)
    x1 = x_ref[0] + gt_ref[0] * acc
    x1_ref[0] = x1
    y = x1 * lax.rsqrt(jnp.mean(x1 * x1, -1, keepdims=True) + NORM_EPS) * g_ref[...]
    h = y * (1.0 + sc_ref[0]) + sh_ref[0]
    hi, lo = _split_bf16(h)
    h2_ref[0] = hi
    lg = (jnp.dot(hi, rhi_ref[...], preferred_element_type=F32)
          + jnp.dot(lo, rhi_ref[...], preferred_element_type=F32)
          + jnp.dot(hi, rlo_ref[...], preferred_element_type=F32)) + rb_ref[...]
    route_ref[0] = _route(lg)


def _route(lg):
    lane = lax.broadcasted_iota(jnp.int32, lg.shape, 1)
    neg = -jnp.inf
    first = lambda hit: jnp.min(jnp.where(hit, lane, ROUTER_PAD), axis=1, keepdims=True)
    is_grp = lane < N_GROUPS
    grp = jnp.where(is_grp, lg, neg)
    g_max = jnp.max(grp, axis=1, keepdims=True)
    g_p = 1.0 / jnp.sum(jnp.where(is_grp, jnp.exp(grp - g_max), 0.0), axis=1, keepdims=True)
    g_idx = first(grp == g_max)
    e_lane = lane - N_GROUPS
    in_group = (e_lane >= 0) & (e_lane < N_EXPERTS) & (e_lane // EXP_PER_GROUP == g_idx)
    e_log = jnp.where(in_group, lg, neg)
    e1_max = jnp.max(e_log, axis=1, keepdims=True)
    e1_lane = first(e_log == e1_max)
    e_log2 = jnp.where(lane == e1_lane, neg, e_log)
    e2_max = jnp.max(e_log2, axis=1, keepdims=True)
    e2_lane = first(e_log2 == e2_max)
    ratio = jnp.exp(e2_max - e1_max)
    gate1 = g_p / (1.0 + ratio)
    gate2 = gate1 * ratio
    out = jnp.where(lane == 0, (e1_lane - N_GROUPS).astype(F32), 0.0)
    out = jnp.where(lane == 1, (e2_lane - N_GROUPS).astype(F32), out)
    out = jnp.where(lane == 2, gate1, out)
    return jnp.where(lane == 3, gate2, out)


def _out_proj(ys, x, gt, sc, sh, g, w_out, r_hi, r_lo, r_b):
    B, S, D = x.shape
    row = lambda w: pl.BlockSpec((1, OUT_ROWS, w), lambda b, i: (b, i, 0))
    vec = pl.BlockSpec((1, 1, D), lambda b, i: (b, 0, 0))
    full = lambda a: pl.BlockSpec(a.shape, lambda b, i: (0,) * a.ndim)
    return pl.pallas_call(
        _out_proj_kernel,
        grid=(B, S // OUT_ROWS),
        in_specs=[row(GROUP_W)] * 4 + [row(D), vec, vec, vec, full(g), full(w_out), full(r_hi), full(r_lo),
                                        full(r_b)],
        out_specs=[row(D), row(D), row(ROUTER_PAD)],
        out_shape=[jax.ShapeDtypeStruct((B, S, D), F32), jax.ShapeDtypeStruct((B, S, D), BF16),
                   jax.ShapeDtypeStruct((B, S, ROUTER_PAD), F32)],
        compiler_params=pltpu.CompilerParams(dimension_semantics=("parallel", "parallel"),
                                             vmem_limit_bytes=VMEM_LIMIT),
        name="out_proj",
    )(*ys, x, gt, sc, sh, g, w_out, r_hi, r_lo, r_b)


def _moe_ffn_kernel(blk_e_ref, x_ref, wt_ref, w1_ref, w3_ref, w2_ref, o_ref, w1b_ref, w3b_ref, w2b_ref):
    i = pl.program_id(0)
    changed = jnp.logical_or(i == 0, blk_e_ref[i] != blk_e_ref[jnp.maximum(i - 1, 0)])

    @pl.when(changed)
    def _():
        w1b_ref[...] = w1_ref[0, 0].astype(BF16)
        w3b_ref[...] = w3_ref[0, 0].astype(BF16)
        w2b_ref[...] = w2_ref[0, 0].astype(BF16)

    xb = x_ref[...]
    a = jnp.dot(xb, w1b_ref[...], preferred_element_type=F32)
    b = jnp.dot(xb, w3b_ref[...], preferred_element_type=F32)
    hmid = (a * jax.nn.sigmoid(a) * b).astype(BF16)
    y = jnp.dot(hmid, w2b_ref[...], preferred_element_type=F32)
    o_ref[...] = (y * wt_ref[...]).astype(o_ref.dtype)


def _moe_ffn(blk_e, xs, wt, w1, w3, w2, layer):
    n_slots, D = xs.shape
    n_blocks = n_slots // MOE_ROWS
    FF = w1.shape[-1]
    return pl.pallas_call(
        _moe_ffn_kernel,
        grid_spec=pltpu.PrefetchScalarGridSpec(
            num_scalar_prefetch=1,
            grid=(n_blocks,),
            in_specs=[pl.BlockSpec((MOE_ROWS, D), lambda i, e: (i, 0)),
                      pl.BlockSpec((MOE_ROWS, 1), lambda i, e: (i, 0)),
                      pl.BlockSpec((1, 1, D, FF), lambda i, e: (layer, e[i], 0, 0)),
                      pl.BlockSpec((1, 1, D, FF), lambda i, e: (layer, e[i], 0, 0)),
                      pl.BlockSpec((1, 1, FF, D), lambda i, e: (layer, e[i], 0, 0))],
            out_specs=pl.BlockSpec((MOE_ROWS, D), lambda i, e: (i, 0)),
            scratch_shapes=[pltpu.VMEM((D, FF), BF16), pltpu.VMEM((D, FF), BF16), pltpu.VMEM((FF, D), BF16)],
        ),
        out_shape=jax.ShapeDtypeStruct((n_slots, D), BF16),
        compiler_params=pltpu.CompilerParams(dimension_semantics=("arbitrary",),
                                             vmem_limit_bytes=VMEM_LIMIT),
        name="moe_ffn",
    )(blk_e, xs, wt, w1, w3, w2)


def _hier_moe(h2, route, w1, w3, w2, layer):
    N, D = h2.shape
    expert = route[:, :TOP_IN_GROUP].astype(jnp.int32)
    gate = route[:, TOP_IN_GROUP:2 * TOP_IN_GROUP]
    n_asg = N * TOP_IN_GROUP
    n_blocks = n_asg // MOE_ROWS + N_EXPERTS
    n_slots = n_blocks * MOE_ROWS
    n_fill = n_slots - n_asg
    flat_e = expert.reshape(n_asg // LANE, LANE)
    counts = jnp.sum(flat_e[None] == jnp.arange(N_EXPERTS)[:, None, None], axis=(1, 2)).astype(jnp.int32)
    pad_counts = (counts + MOE_ROWS - 1) // MOE_ROWS * MOE_ROWS
    pad_end = jnp.cumsum(pad_counts)
    blk_start = jnp.arange(n_blocks) * MOE_ROWS
    blk_e = jnp.minimum(jnp.sum(pad_end[None, :] <= blk_start[:, None], 1), N_EXPERTS - 1).astype(jnp.int32)
    fill_end = jnp.cumsum(pad_counts - counts)
    fill_id = jnp.arange(n_fill).reshape(n_fill // LANE, LANE)
    fill_e = jnp.zeros_like(fill_id)
    for e in range(N_EXPERTS):
        fill_e = fill_e + (fill_id >= fill_end[e])
    keys = jnp.concatenate([flat_e.reshape(n_asg) * 2, fill_e.reshape(n_fill) * 2 + 1])
    toks = jnp.concatenate([jnp.arange(n_asg, dtype=jnp.int32) // TOP_IN_GROUP, jnp.zeros((n_fill,), jnp.int32)])
    wts = jnp.concatenate([gate.reshape(n_asg), jnp.zeros((n_fill,), F32)])
    _, slot_tok, slot_w, slot_src = lax.sort((keys, toks, wts, jnp.arange(n_slots, dtype=jnp.int32)), num_keys=1)
    _, entry_slot = lax.sort((slot_src, jnp.arange(n_slots, dtype=jnp.int32)), num_keys=1)
    asg_slot = entry_slot[:n_asg].reshape(N, TOP_IN_GROUP)
    yb = _moe_ffn(blk_e, h2[slot_tok], slot_w[:, None], w1, w3, w2, layer)
    return yb[asg_slot[:, 0]], yb[asg_slot[:, 1]]


INT_MIN = -2 ** 31
DSA_KEY_STEP = 256
DSA_QUERIES = 256
DSA_VMEM_LIMIT = 60 * 1024 * 1024


def _float_order_key(x):
    bits = pltpu.bitcast(x, jnp.int32)
    bits = jnp.where(x == 0.0, 0, bits)
    return bits ^ ((bits >> 31) & 0x7FFFFFFF)


COL_PART = 64


def _col_reduce(x, reduce):
    part = reduce(x.reshape(x.shape[0] // COL_PART, COL_PART, x.shape[1]), axis=0)
    return reduce(part, axis=0, keepdims=True)


def _col_count(mask):
    return _col_reduce(jnp.where(mask, 1.0, 0.0), jnp.sum)


def _head_block_diag(t, group):
    n_heads = t.shape[0] // group
    row_h = lax.broadcasted_iota(jnp.int32, t.shape, 0) // group
    return jnp.concatenate([jnp.where(row_h == h, t, 0.0) for h in range(n_heads)], axis=1)


def _dsa_block_t(qd_ref, kd4_ref, vdwt_ref, qi_ref, kihi_ref, kilo_ref, wi_ref, g_ref, o_ref, *, kl, n_sel):
    Q = DSA_QUERIES
    q0 = pl.program_id(1) * Q
    w_hi, w_lo = _split_bf16(_head_block_diag(jnp.transpose(qi_ref[0]), IDX_DIM))
    k_hi, k_lo = kihi_ref[0, :kl, :], kilo_ref[0, :kl, :]
    sc = _dot(jnp.concatenate([k_hi, k_lo, k_hi], axis=1),
              jnp.concatenate([w_hi, w_hi, w_lo], axis=0))
    wit = jnp.transpose(wi_ref[0])
    score = sum(wit[h:h + 1, :] * jnp.maximum(sc[:, h * Q:(h + 1) * Q], 0.0) for h in range(IDX_HEADS))
    kidx = lax.broadcasted_iota(jnp.int32, (kl, Q), 0)
    qpos = q0 + lax.broadcasted_iota(jnp.int32, (kl, Q), 1)
    adm = kidx <= qpos
    key = _float_order_key(jnp.where(adm, score, -jnp.inf))

    def value_bit(it, tau):
        cand = tau | jnp.left_shift(jnp.int32(1), 31 - it)
        return jnp.where(_col_count(key >= (cand ^ INT_MIN)) >= n_sel, cand, tau)

    tau = lax.fori_loop(0, 32, value_bit, jnp.zeros((1, Q), jnp.int32)) ^ INT_MIN
    gt = key > tau
    eq = (key == tau) & adm
    need = n_sel - _col_count(gt)
    n_eq = _col_count(eq)

    index_width = kl.bit_length()

    def index_bits():
        def index_bit(it, bound):
            cand = bound | jnp.left_shift(jnp.int32(1), index_width - 1 - it)
            return jnp.where(_col_count(eq & (kidx < cand)) <= need, cand, bound)
        return lax.fori_loop(0, index_width, index_bit, jnp.zeros((1, Q), jnp.int32))

    bound = lax.cond(jnp.max(n_eq - need) > 0.0, index_bits, lambda: jnp.full((1, Q), kl, jnp.int32))
    sel = gt | (eq & (kidx < bound))

    w_att = _head_block_diag(jnp.transpose(qd_ref[0]), HEAD_DIM).astype(BF16)
    lg = _dot(kd4_ref[0, :kl, :], w_att) * HEAD_DIM ** -0.5
    lg = jnp.where(jnp.concatenate([sel] * N_HEADS, axis=1), lg, -jnp.inf)
    p = jnp.exp(lg - _col_reduce(lg, jnp.max))
    out_t = _dot(vdwt_ref[0, :, :kl], p.astype(BF16)) / _col_reduce(p, jnp.sum)
    row_h = lax.broadcasted_iota(jnp.int32, (GROUP_W, Q), 0) // HEAD_DIM
    nat_t = sum(jnp.where(row_h == h, out_t[:, h * Q:(h + 1) * Q], 0.0) for h in range(N_HEADS))
    r = jnp.transpose(nat_t)
    hi_ = lax.broadcasted_iota(jnp.int32, (GROUP_W, GROUP_W), 0) // HEAD_DIM
    hj_ = lax.broadcasted_iota(jnp.int32, (GROUP_W, GROUP_W), 1) // HEAD_DIM
    ms = _dot_f32_by_exact(r * r, jnp.where(hi_ == hj_, 1.0, 0.0).astype(BF16)) * (1.0 / HEAD_DIM)
    o_ref[0] = r * lax.rsqrt(ms + NORM_EPS) * g_ref[...]


def _dsa_kernel(qd_ref, kd4_ref, vdwt_ref, qi_ref, kihi_ref, kilo_ref, wi_ref, g_ref, o_ref, *, kls, n_sel):
    blocks_per_step = DSA_KEY_STEP // DSA_QUERIES
    for j, kl in enumerate(kls):
        @pl.when(pl.program_id(1) // blocks_per_step == j)
        def _():
            _dsa_block_t(qd_ref, kd4_ref, vdwt_ref, qi_ref, kihi_ref, kilo_ref, wi_ref, g_ref, o_ref,
                         kl=kl, n_sel=n_sel)


DSA_PREP_ROWS = 256
D_Q, D_KV, D_QI, D_KW = 0, GROUP_W, GROUP_W + LANE, GROUP_W + 2 * LANE


def _swap_halves(x, half):
    n = x.shape[1]
    lane = lax.broadcasted_iota(jnp.int32, x.shape, 1)
    return jnp.where(lane % (2 * half) < half, pltpu.roll(x, n - half, axis=1), pltpu.roll(x, half, axis=1))


def _dsa_prep_kernel(p_ref, cq_ref, sq_ref, ci_ref, si_ref, gq_ref, gk_ref,
                     qd_ref, kd4_ref, vdwt_ref, qi_ref, kihi_ref, kilo_ref, wi_ref):
    GW = GROUP_W
    x = p_ref[0]
    hi_ = lax.broadcasted_iota(jnp.int32, (GW, GW), 0) // HEAD_DIM
    hj_ = lax.broadcasted_iota(jnp.int32, (GW, GW), 1) // HEAD_DIM
    ones_bd = jnp.where(hi_ == hj_, 1.0, 0.0).astype(BF16)
    q = x[:, D_Q:D_Q + GW]
    q = q * lax.rsqrt(_dot_f32_by_exact(q * q, ones_bd) * (1.0 / HEAD_DIM) + NORM_EPS) * gq_ref[...]
    qd_ref[0] = q * cq_ref[...] + _swap_halves(q, HEAD_DIM // 2) * sq_ref[...]
    kv = x[:, D_KV:D_KV + LANE]
    lane = lax.broadcasted_iota(jnp.int32, kv.shape, 1)
    is_k = lane < HEAD_DIM
    ms = jnp.sum(jnp.where(is_k, kv * kv, 0.0), axis=1, keepdims=True) * (1.0 / HEAD_DIM)
    kn = kv * lax.rsqrt(ms + NORM_EPS) * gk_ref[...]
    kr = kn * cq_ref[:, :LANE] + _swap_halves(kn, HEAD_DIM // 2) * sq_ref[:, :LANE]
    k2 = jnp.where(is_k, kr, pltpu.roll(kr, HEAD_DIM, axis=1))
    kd4_ref[0] = jnp.concatenate([k2] * (GW // LANE), axis=1).astype(BF16)
    v2 = jnp.where(is_k, pltpu.roll(kv, HEAD_DIM, axis=1), kv)
    v2t = jnp.transpose(v2)
    vdwt_ref[0] = jnp.concatenate([v2t] * (GW // LANE), axis=0).astype(BF16)
    qi = x[:, D_QI:D_QI + LANE]
    qi_ref[0] = qi * ci_ref[...] + _swap_halves(qi, IDX_DIM // 2) * si_ref[...]
    kw = x[:, D_KW:D_KW + LANE]
    kir = kw * ci_ref[...] + _swap_halves(kw, IDX_DIM // 2) * si_ref[...]
    ki1 = jnp.where(lane < IDX_DIM, kir, 0.0)
    ki2 = ki1 + pltpu.roll(ki1, IDX_DIM, axis=1)
    ki4 = ki2 + pltpu.roll(ki2, 2 * IDX_DIM, axis=1)
    kihi_ref[0], kilo_ref[0] = _split_bf16(ki4)
    wi_ref[0] = pltpu.roll(kw, LANE - IDX_DIM, axis=1) * (IDX_HEADS ** -0.5 * IDX_DIM ** -0.5)


def _rope_tables(S, dim, width):
    half = dim // 2
    inv = ROPE_THETA ** (-jnp.arange(half, dtype=F32) / half)
    ang = jnp.arange(S, dtype=F32)[:, None] * inv[None, :]
    cos = jnp.tile(jnp.cos(ang), (1, width // half))
    sin = jnp.tile(jnp.concatenate([-jnp.sin(ang), jnp.sin(ang)], axis=1), (1, width // dim))
    return cos, sin


def _dsa_prep(p, qn_g, kn_g):
    B, S, _ = p.shape
    GW, R = GROUP_W, DSA_PREP_ROWS
    cq, sq = _rope_tables(S, HEAD_DIM, GW)
    ci, si = _rope_tables(S, IDX_DIM, LANE)
    gq = jnp.tile(qn_g, N_HEADS)[None]
    gk = jnp.pad(kn_g, (0, LANE - HEAD_DIM))[None]
    rows = lambda w: pl.BlockSpec((1, R, w), lambda b, i: (b, i, 0))
    tab = lambda w: pl.BlockSpec((R, w), lambda b, i: (i, 0))
    cols = lambda r: pl.BlockSpec((1, r, R), lambda b, i: (b, 0, i))
    vec = lambda w: pl.BlockSpec((1, w), lambda b, i: (0, 0))
    return pl.pallas_call(
        _dsa_prep_kernel,
        grid=(B, S // R),
        in_specs=[rows(D_PAD), tab(GW), tab(GW), tab(LANE), tab(LANE), vec(GW), vec(LANE)],
        out_specs=[rows(GW), rows(GW), cols(GW), rows(LANE), rows(LANE), rows(LANE), rows(LANE)],
        out_shape=[jax.ShapeDtypeStruct((B, S, GW), F32), jax.ShapeDtypeStruct((B, S, GW), BF16),
                   jax.ShapeDtypeStruct((B, GW, S), BF16), jax.ShapeDtypeStruct((B, S, LANE), F32),
                   jax.ShapeDtypeStruct((B, S, LANE), BF16), jax.ShapeDtypeStruct((B, S, LANE), BF16),
                   jax.ShapeDtypeStruct((B, S, LANE), F32)],
        compiler_params=pltpu.CompilerParams(dimension_semantics=("parallel", "parallel"),
                                             vmem_limit_bytes=VMEM_LIMIT),
        name="dsa_prep",
    )(p, cq, sq, ci, si, gq, gk)


def _dsa_attn_norm(qd, kd4, vdwt, qi, ki_hi, ki_lo, wi, g):
    B, S, _ = qd.shape
    n_sel = min(TOPK_MAX, S // 4)
    assert S % DSA_KEY_STEP == 0 and n_sel <= DSA_KEY_STEP
    kls = tuple(range(DSA_KEY_STEP, S + 1, DSA_KEY_STEP))
    blk = lambda w: pl.BlockSpec((1, DSA_QUERIES, w), lambda b, i: (b, i, 0))
    per_b = lambda r, c: pl.BlockSpec((1, r, c), lambda b, i: (b, 0, 0))
    return pl.pallas_call(
        functools.partial(_dsa_kernel, kls=kls, n_sel=n_sel),
        grid=(B, S // DSA_QUERIES),
        in_specs=[blk(GROUP_W), per_b(S, GROUP_W), per_b(GROUP_W, S), blk(IDX_HEADS * IDX_DIM),
                  per_b(S, LANE), per_b(S, LANE), blk(LANE), pl.BlockSpec((1, GROUP_W), lambda b, i: (0, 0))],
        out_specs=blk(GROUP_W),
        out_shape=jax.ShapeDtypeStruct((B, S, GROUP_W), F32),
        compiler_params=pltpu.CompilerParams(dimension_semantics=("parallel", "parallel"),
                                             vmem_limit_bytes=DSA_VMEM_LIMIT),
        name="dsa_attn",
    )(qd, kd4, vdwt, qi, ki_hi, ki_lo, wi, g[None])


RWKV_CHUNK = 64
RWKV_LOW = RWKV_W_RANK + RWKV_A_RANK + RWKV_G_RANK
RWKV_GROUP = N_HEADS
RWKV_BATCH = 8


def _dot(a, b):
    return jnp.dot(a, b, preferred_element_type=F32)


def _dot_nt(a, b):
    return lax.dot_general(a, b, (((1,), (1,)), ((), ())), preferred_element_type=F32)


def _dot_tn(a, b):
    return lax.dot_general(a, b, (((0,), (0,)), ((), ())), preferred_element_type=F32)


def _split3_bf16(t):
    p1 = t.astype(BF16)
    r1 = t - p1.astype(F32)
    p2 = r1.astype(BF16)
    p3 = (r1 - p2.astype(F32)).astype(BF16)
    return p1, p2, p3


def _dot_f32_by_exact(a, b_exact):
    m = a.shape[0]
    r = _dot(jnp.concatenate(_split3_bf16(a), axis=0), b_exact)
    return r[:m] + r[m:2 * m] + r[2 * m:]


def _dot_exact_by_f32(a_exact, b):
    n = b.shape[1]
    r = _dot(a_exact, jnp.concatenate(_split3_bf16(b), axis=1))
    return r[:, :n] + r[:, n:2 * n] + r[:, 2 * n:]


def _dot3(a, b_hi, b_lo):
    a_hi, a_lo = _split_bf16(a)
    return _dot(jnp.concatenate([a_hi, a_lo, a_hi], axis=1), jnp.concatenate([b_hi, b_hi, b_lo], axis=0))


def _softplus(z):
    return jnp.maximum(z, 0.0) + jnp.log(1.0 + jnp.exp(-jnp.abs(z)))


def _rwkv_kernel(p_ref, mu_ref, vec_ref, lhi_ref, llo_ref, o_ref, state_ref, prev_ref):
    @pl.when(pl.program_id(1) == 0)
    def _():
        state_ref[...] = jnp.zeros_like(state_ref)
        prev_ref[...] = jnp.zeros_like(prev_ref)

    masks = _rwkv_masks()
    chains = [_rwkv_chunk(masks, p_ref.at[n], mu_ref, vec_ref, lhi_ref, llo_ref, o_ref.at[n], state_ref.at[n],
                          prev_ref.at[n]) for n in range(RWKV_BATCH)]
    for _ in itertools.zip_longest(*chains):
        pass


def _rwkv_masks():
    L, GW, SG = RWKV_CHUNK, GROUP_W, RWKV_GROUP * RWKV_CHUNK
    ri = lax.broadcasted_iota(jnp.int32, (GW, GW), 0)
    ci = lax.broadcasted_iota(jnp.int32, (GW, GW), 1)
    ones_bd = jnp.where((ri // HEAD_DIM) == (ci // HEAD_DIM), 1.0, 0.0).astype(BF16)
    ti = lax.broadcasted_iota(jnp.int32, (L, L), 0)
    tj = lax.broadcasted_iota(jnp.int32, (L, L), 1)
    tri = jnp.where(tj <= ti, 1.0, 0.0).astype(BF16)
    gi = lax.broadcasted_iota(jnp.int32, (SG, SG), 0)
    gj = lax.broadcasted_iota(jnp.int32, (SG, SG), 1)
    g_same = (gi // L) == (gj // L)
    strict = g_same & ((gj % L) < (gi % L))
    incl = g_same & ((gj % L) <= (gi % L))
    diag = gi == gj
    return ones_bd, tri, g_same, strict, incl, diag, jnp.where(diag, 1.0, 0.0)


def _rwkv_chunk(masks, p_ref, mu_ref, vec_ref, lhi_ref, llo_ref, o_ref, state_ref, prev_ref):
    L, GW = RWKV_CHUNK, GROUP_W
    p = p_ref[...]
    row = lax.broadcasted_iota(jnp.int32, p.shape, 0)
    prev = jnp.where(row == 0, prev_ref[...], pltpu.roll(p, 1, axis=0))
    prev_ref[...] = p[L - 1:L, :]
    ps = p + (prev - p) * mu_ref[...]
    r, k, v = ps[:, :GW], ps[:, GW:2 * GW], ps[:, 2 * GW:3 * GW]
    low = ps[:, 3 * GW:]
    lane_low = lax.broadcasted_iota(jnp.int32, low.shape, 1)
    low = jnp.where(lane_low < RWKV_W_RANK, jnp.tanh(low),
                    jnp.where(lane_low < RWKV_W_RANK + RWKV_A_RANK, low, jax.nn.sigmoid(low)))
    up = _dot3(low, lhi_ref[...], llo_ref[...])
    w0, a0, k_k, k_a = vec_ref[0:1, :], vec_ref[1:2, :], vec_ref[2:3, :], vec_ref[3:4, :]
    r_k, ln_g, ln_b = vec_ref[4:5, :], vec_ref[5:6, :], vec_ref[6:7, :]
    logw = -jnp.exp(-_softplus(-(w0 + up[:, :GW])) - 0.5)
    rate = jax.nn.sigmoid(a0 + up[:, GW:2 * GW])
    gate = up[:, 2 * GW:]

    ones_bd, tri, g_same, strict, incl, diag, eye = masks
    kk = k * k_k
    k = k * (1.0 + (rate - 1.0) * k_a)
    seg = _dot_f32_by_exact(jnp.concatenate([kk * kk, r * k * r_k], axis=0), ones_bd)
    kk = kk / jnp.maximum(jnp.sqrt(seg[:L]), 1e-12)

    lc = _dot_exact_by_f32(tri, logw)
    lc_last = lc[L - 1:L, :]
    dec_in = jnp.exp(lc)
    dec_out = jnp.exp(-lc)
    a_t = -kk * jnp.exp(lc - logw)
    b_t = kk * rate * dec_out
    k_t = k * dec_out
    r_t = r * dec_in
    to_end = jnp.exp(lc_last)

    SG = RWKV_GROUP * L
    stack = lambda t: jnp.concatenate([t] * RWKV_GROUP, axis=0)
    bd = lambda t: jnp.where(g_same, stack(t), 0.0).astype(BF16)
    n_doublings = RWKV_CHUNK.bit_length() - 2
    ys = []
    for g in range(N_HEADS // RWKV_GROUP):
        cols = slice(g * SG, (g + 1) * SG)
        a_bd, r_bd, v_bd = bd(a_t[:, cols]), bd(r_t[:, cols]), bd(v[:, cols])
        m = _dot_nt(jnp.concatenate([a_bd, r_bd], axis=0),
                    jnp.concatenate([stack(b_t[:, cols]), stack(k_t[:, cols])], axis=0).astype(BF16))
        yield
        m_ab = jnp.where(strict, m[:SG, :SG], 0.0)
        m_ak = jnp.where(strict, m[:SG, SG:], 0.0).astype(BF16)
        m_rb = jnp.where(incl, m[SG:, :SG], 0.0).astype(BF16)
        m_rk = jnp.where(incl, m[SG:, SG:], 0.0).astype(BF16)

        inv = eye + m_ab
        pw_b = m_ab.astype(BF16)
        sq = _dot(pw_b, pw_b)
        yield
        pw_b = sq.astype(BF16)
        for s in range(n_doublings - 1):
            both = _dot(jnp.concatenate([inv.astype(BF16), pw_b], axis=0), pw_b)
            yield
            inv = inv + both[:SG]
            pw_b = both[SG:].astype(BF16)
        last = _dot(inv.astype(BF16), pw_b)
        t0 = state_ref[g]
        t0_b = t0.astype(BF16)
        rhs = _dot(jnp.concatenate([a_bd, m_ak], axis=1), jnp.concatenate([t0_b, v_bd], axis=0))
        yield
        inv_b = (inv + last).astype(BF16)
        u = _dot(inv_b, rhs.astype(BF16))
        yield
        u = u.astype(BF16)
        y_bd = _dot(jnp.concatenate([r_bd, m_rb, m_rk], axis=1),
                    jnp.concatenate([t0_b, u, v_bd], axis=0))
        end_g = to_end[:, cols]
        to_end_col = jnp.sum(jnp.where(diag, jnp.broadcast_to(end_g, (SG, SG)), 0.0), axis=1, keepdims=True)
        carried = _dot_tn(jnp.concatenate([bd(b_t[:, cols] * end_g), bd(k_t[:, cols] * end_g)], axis=0),
                          jnp.concatenate([u, v_bd], axis=0))
        yield
        ys.append(sum(y_bd[h * L:(h + 1) * L, :] for h in range(RWKV_GROUP)))
        state_ref[g] = to_end_col * t0 + carried
    y = jnp.concatenate(ys, axis=1)

    inv_d = 1.0 / HEAD_DIM
    mean = _dot_f32_by_exact(y, ones_bd) * inv_d
    yield
    yc = y - mean
    var = _dot_f32_by_exact(yc * yc, ones_bd) * inv_d
    yield
    yn = yc * lax.rsqrt(var + RWKV_GN_EPS) * ln_g + ln_b
    o_ref[...] = (yn + seg[L:] * v) * gate


def _rwkv7_time_mix(p, mu, w0, w2, a0, a2, g2, k_k, k_a, r_k, ln_g, ln_b):
    B, S, _ = p.shape
    GW = GROUP_W
    assert S % RWKV_CHUNK == 0 and RWKV_CHUNK == HEAD_DIM and B % RWKV_BATCH == 0
    low_w = jnp.zeros((RWKV_LOW, 3 * GW), F32)
    low_w = low_w.at[:RWKV_W_RANK, :GW].set(w2)
    low_w = low_w.at[RWKV_W_RANK:RWKV_W_RANK + RWKV_A_RANK, GW:2 * GW].set(a2)
    low_w = low_w.at[RWKV_W_RANK + RWKV_A_RANK:, 2 * GW:].set(g2)
    l_hi, l_lo = _split_bf16(low_w)
    vecs = jnp.stack([w0, a0, k_k, k_a, r_k, ln_g, ln_b, jnp.zeros_like(w0)], 0)
    full = lambda a: pl.BlockSpec(a.shape, lambda b, c: (0,) * a.ndim)
    mu2 = mu[None]
    return pl.pallas_call(
        _rwkv_kernel,
        grid=(B // RWKV_BATCH, S // RWKV_CHUNK),
        in_specs=[pl.BlockSpec((RWKV_BATCH, RWKV_CHUNK, A_PAD), lambda b, c: (b, c, 0)),
                  full(mu2), full(vecs), full(l_hi), full(l_lo)],
        out_specs=pl.BlockSpec((RWKV_BATCH, RWKV_CHUNK, GW), lambda b, c: (b, c, 0)),
        out_shape=jax.ShapeDtypeStruct((B, S, GW), F32),
        scratch_shapes=[pltpu.VMEM((RWKV_BATCH, N_HEADS // RWKV_GROUP, RWKV_GROUP * RWKV_CHUNK,
                                    RWKV_GROUP * HEAD_DIM), F32),
                        pltpu.VMEM((RWKV_BATCH, 1, A_PAD), F32)],
        compiler_params=pltpu.CompilerParams(dimension_semantics=("parallel", "arbitrary"),
                                             vmem_limit_bytes=VMEM_LIMIT),
        name="rwkv7",
    )(p, mu2, vecs, l_hi, l_lo)


SB_UNROLL = 4


def _sb_kernel(q_ref, k_ref, v_ref, g_ref, o_ref, kbd_ref, vbd_ref):
    i = pl.program_id(1)
    T, GW, H = SB_BLOCK, GROUP_W, N_HEADS
    lane_h = lax.broadcasted_iota(jnp.int32, (T, GW), 1) // HEAD_DIM
    k_new, v_new = k_ref[0], v_ref[0]
    for h in range(H):
        kbd_ref[i, h * T:(h + 1) * T, :] = jnp.where(lane_h == h, k_new, 0.0).astype(BF16)
        vbd_ref[i, h * T:(h + 1) * T, :] = jnp.where(lane_h == h, v_new, 0.0).astype(BF16)

    q = (q_ref[0] * HEAD_DIM ** -0.5).astype(BF16)
    si = lax.broadcasted_iota(jnp.int32, (T, 2 * T), 0)
    sj = lax.broadcasted_iota(jnp.int32, (T, 2 * T), 1)
    later_and_all = jnp.where((si > sj) | (sj >= T), 1.0, 0.0).astype(BF16)
    qrow = lax.broadcasted_iota(jnp.int32, (T, H * T), 0)
    kcol = lax.broadcasted_iota(jnp.int32, (T, H * T), 1) % T
    causal = kcol < qrow

    def key_block_stages(j, box, diagonal):
        z = _dot_nt(q, kbd_ref[j])
        yield
        log1m = -(jnp.maximum(z, 0.0) + jnp.log(1.0 + jnp.exp(-jnp.abs(z))))
        log_sig = z + log1m
        log1m_in = (jnp.where(causal, log1m, 0.0) if diagonal else log1m).astype(BF16)
        sums = _dot(jnp.concatenate([log1m_in[:, h * T:(h + 1) * T] for h in range(H)], axis=0), later_and_all)
        yield
        suffix = jnp.concatenate([sums[h * T:(h + 1) * T, :T] for h in range(H)], axis=1)
        total = jnp.concatenate([sums[h * T:(h + 1) * T, T:] for h in range(H)], axis=1)
        att = jnp.exp(log_sig + suffix + box["carry"])
        if diagonal:
            att = jnp.where(causal, att, 0.0)
        box["carry"] = box["carry"] + total
        pv = _dot(att.astype(BF16), vbd_ref[j])
        yield
        box["acc"] = box["acc"] + pv

    def key_blocks(js, state, diagonal=False):
        box = {"carry": state[0], "acc": state[1]}
        for _ in itertools.zip_longest(*[key_block_stages(j, box, diagonal) for j in js]):
            pass
        return box["carry"], box["acc"]

    state = key_blocks([i], (jnp.zeros((T, H * T), F32), jnp.zeros((T, GW), F32)), True)
    rem = i % SB_UNROLL
    state = lax.fori_loop(0, rem, lambda it, st: key_blocks([i - 1 - it], st), state)
    top = i - 1 - rem
    _, y = lax.fori_loop(0, i // SB_UNROLL,
                         lambda it, st: key_blocks([top - SB_UNROLL * it - n for n in range(SB_UNROLL)], st), state)

    hi_ = lax.broadcasted_iota(jnp.int32, (GW, GW), 0) // HEAD_DIM
    hj_ = lax.broadcasted_iota(jnp.int32, (GW, GW), 1) // HEAD_DIM
    ones_bd = jnp.where(hi_ == hj_, 1.0, 0.0).astype(BF16)
    ms = _dot_f32_by_exact(y * y, ones_bd) * (1.0 / HEAD_DIM)
    o_ref[0] = y * lax.rsqrt(ms + NORM_EPS) * g_ref[...]


def _stick_breaking_norm(p, g):
    B, S, _ = p.shape
    GW = GROUP_W
    assert S % SB_BLOCK == 0 and HEAD_DIM ** -0.5 == 2.0 ** -(HEAD_DIM.bit_length() // 2)
    col = lambda n: pl.BlockSpec((1, SB_BLOCK, GW), lambda b, i: (b, i, n))
    stacked = pltpu.VMEM((S // SB_BLOCK, N_HEADS * SB_BLOCK, GW), BF16)
    return pl.pallas_call(
        _sb_kernel,
        grid=(B, S // SB_BLOCK),
        in_specs=[col(0), col(1), col(2), pl.BlockSpec((1, GW), lambda b, i: (0, 0))],
        out_specs=col(0),
        out_shape=jax.ShapeDtypeStruct((B, S, GW), F32),
        scratch_shapes=[stacked, stacked],
        compiler_params=pltpu.CompilerParams(dimension_semantics=("parallel", "arbitrary"),
                                             vmem_limit_bytes=VMEM_LIMIT),
        name="stick_breaking",
    )(p, p, p, g[None])


ML_HALO = 8
ML_GROUP = N_HEADS
ML_BATCH = 4


def _dot_nt_exact_by_f32(a_exact, b):
    return _dot_nt(jnp.concatenate([a_exact] * 3, axis=1), jnp.concatenate(_split3_bf16(b), axis=1))


def _mlstm_kernel(p_ref, cw_ref, cb_ref, gb_ref, g_ref, o_ref, ext_ref, ct_ref, n_ref, m_ref):
    @pl.when(pl.program_id(1) == 0)
    def _():
        ext_ref[...] = jnp.zeros_like(ext_ref)
        ct_ref[...] = jnp.zeros_like(ct_ref)
        n_ref[...] = jnp.zeros_like(n_ref)
        m_ref[...] = jnp.zeros_like(m_ref)

    masks = _mlstm_masks()
    chains = [_mlstm_chunk(masks, p_ref.at[n], cw_ref, cb_ref, gb_ref, g_ref, o_ref.at[n], ext_ref.at[n],
                           ct_ref.at[n], n_ref.at[n], m_ref.at[n]) for n in range(ML_BATCH)]
    for _ in itertools.zip_longest(*chains):
        pass


def _mlstm_masks():
    L, GW, H, SG = ML_CHUNK, GROUP_W, N_HEADS, ML_GROUP * ML_CHUNK
    gi = lax.broadcasted_iota(jnp.int32, (LANE, 2 * GW), 0)
    gj = lax.broadcasted_iota(jnp.int32, (LANE, 2 * GW), 1)
    expand = jnp.where(gi == (gj % GW) // HEAD_DIM + H * (gj // GW), 1.0, 0.0).astype(BF16)
    ti = lax.broadcasted_iota(jnp.int32, (L, L), 0)
    tj = lax.broadcasted_iota(jnp.int32, (L, L), 1)
    tri = jnp.where(tj <= ti, 1.0, 0.0).astype(BF16)
    ri = lax.broadcasted_iota(jnp.int32, (SG, SG), 0)
    ci = lax.broadcasted_iota(jnp.int32, (SG, SG), 1)
    same_head = (ri // L) == (ci // HEAD_DIM)
    first_lane = ci == (ri // L) * HEAD_DIM
    sel_first = jnp.where(first_lane, 1.0, 0.0).astype(BF16)
    causal = lax.broadcasted_iota(jnp.int32, (SG, L), 1) <= lax.broadcasted_iota(jnp.int32, (SG, L), 0) % L
    hi_ = lax.broadcasted_iota(jnp.int32, (GW, GW), 0) // HEAD_DIM
    hj_ = lax.broadcasted_iota(jnp.int32, (GW, GW), 1) // HEAD_DIM
    ones_bd = jnp.where(hi_ == hj_, 1.0, 0.0).astype(BF16)
    return expand, tri, same_head, first_lane, sel_first, causal, ones_bd


def _mlstm_chunk(masks, p_ref, cw_ref, cb_ref, gb_ref, g_ref, o_ref, ext_ref, ct_ref, n_ref, m_ref):
    L, GW, H = ML_CHUNK, GROUP_W, N_HEADS
    x = p_ref[...]
    ext_ref[ML_HALO:, :] = x[:, :2 * GW]
    conv = cb_ref[...]
    for j in range(ML_CONV):
        conv = conv + cw_ref[j:j + 1, :] * ext_ref[pl.ds(ML_HALO - (ML_CONV - 1) + j, L), :]
    ext_ref[:ML_HALO, :] = x[L - ML_HALO:, :2 * GW]
    qk = conv * jax.nn.sigmoid(conv)
    q, k = qk[:, :GW], qk[:, GW:] * HEAD_DIM ** -0.5
    v, o = x[:, 2 * GW:3 * GW], x[:, 3 * GW:4 * GW]

    gates = x[:, 4 * GW:]
    expand, tri, same_head, first_lane, sel_first, causal, ones_bd = masks
    graw = _dot_f32_by_exact(gates, expand) + gb_ref[...]
    yield
    capped = GATE_CAP * jnp.tanh(graw * (1.0 / GATE_CAP))
    log_i = capped[:, :GW]
    cf = capped[:, GW:]
    log_f = jnp.minimum(cf, 0.0) - jnp.log(1.0 + jnp.exp(-jnp.abs(cf)))

    bf = _dot_exact_by_f32(tri, log_f)
    yield
    b_last = bf[L - 1:L, :]
    m_row, n_row = m_ref[...], n_ref[...]
    dec = b_last - bf + log_i
    m_new = jnp.maximum(b_last + m_row, jnp.max(dec, axis=0, keepdims=True))
    kw = k * jnp.exp(dec - m_new)
    s_old = jnp.exp(b_last + m_row - m_new)
    n_ref[...] = s_old * n_row + jnp.sum(kw, axis=0, keepdims=True)
    m_ref[...] = m_new
    g_in = bf + m_row
    li_b = log_i - bf

    SG = ML_GROUP * L
    stack = lambda t: jnp.concatenate([t] * ML_GROUP, axis=0)
    pick = lambda t: jnp.sum(jnp.where(first_lane, t, 0.0), axis=1, keepdims=True)
    hs = []
    for g in range(H // ML_GROUP):
        cols = slice(g * SG, (g + 1) * SG)
        ct = ct_ref[g]
        qs = jnp.where(same_head, stack(q[:, cols]), 0.0)
        qs_b = qs.astype(BF16)
        v_b = v[:, cols].astype(BF16)
        b_col = pick(stack(bf[:, cols]))
        g_col = pick(stack(g_in[:, cols]))
        row_part = _dot_nt_exact_by_f32(sel_first, li_b[:, cols])
        qk = _dot_nt(qs_b, k[:, cols].astype(BF16))
        inter = _dot(qs_b, ct.astype(BF16))
        carried = _dot_tn(kw[:, cols].astype(BF16), v_b)
        yield
        dmat = jnp.where(causal, b_col + row_part, -jnp.inf)
        m_t = jnp.maximum(g_col, jnp.max(dmat, axis=1, keepdims=True))
        s_inter = jnp.exp(g_col - m_t)
        sqk = qk * jnp.exp(dmat - m_t)
        intra = _dot(sqk.astype(BF16), v_b)
        yield
        num = s_inter * inter + jnp.where(same_head, intra, 0.0)
        den = (s_inter * jnp.sum(qs * n_row[:, cols], axis=1, keepdims=True)
               + jnp.sum(sqk, axis=1, keepdims=True))
        hst = num / jnp.maximum(jnp.abs(den), jnp.exp(-m_t))
        hs.append(sum(hst[n * L:(n + 1) * L, :] for n in range(ML_GROUP)))
        ct_ref[g] = s_old[:, cols] * ct + jnp.where(same_head, carried, 0.0)
    h = jnp.concatenate(hs, axis=1)

    ms = _dot_f32_by_exact(h * h, ones_bd) * (1.0 / HEAD_DIM)
    yield
    o_ref[...] = jax.nn.sigmoid(o) * (h * lax.rsqrt(ms + NORM_EPS) * g_ref[...])


def _mlstm_mix(p, conv_w, conv_b, ig_b, fg_b, norm_g):
    B, S, _ = p.shape
    GW = GROUP_W
    assert S % ML_CHUNK == 0 and ML_CONV - 1 <= ML_HALO <= ML_CHUNK and B % ML_BATCH == 0
    gate_b = jnp.concatenate([jnp.repeat(ig_b, HEAD_DIM), jnp.repeat(fg_b, HEAD_DIM)])[None]
    full = lambda a: pl.BlockSpec(a.shape, lambda b, c: (0,) * a.ndim)
    cb2, g2 = conv_b[None], norm_g[None]
    return pl.pallas_call(
        _mlstm_kernel,
        grid=(B // ML_BATCH, S // ML_CHUNK),
        in_specs=[pl.BlockSpec((ML_BATCH, ML_CHUNK, C_PAD), lambda b, c: (b, c, 0)),
                  full(conv_w), full(cb2), full(gate_b), full(g2)],
        out_specs=pl.BlockSpec((ML_BATCH, ML_CHUNK, GW), lambda b, c: (b, c, 0)),
        out_shape=jax.ShapeDtypeStruct((B, S, GW), F32),
        scratch_shapes=[pltpu.VMEM((ML_BATCH, ML_HALO + ML_CHUNK, 2 * GW), F32),
                        pltpu.VMEM((ML_BATCH, N_HEADS // ML_GROUP, ML_GROUP * ML_CHUNK, ML_GROUP * HEAD_DIM), F32),
                        pltpu.VMEM((ML_BATCH, 1, GW), F32), pltpu.VMEM((ML_BATCH, 1, GW), F32)],
        compiler_params=pltpu.CompilerParams(dimension_semantics=("parallel", "arbitrary"),
                                             vmem_limit_bytes=VMEM_LIMIT),
        name="mlstm",
    )(p, conv_w, cb2, gate_b, g2)


def kernel(x, c, ada_w, ada_b, norm1_g, norm2_g, w_in, rk_mu, rk_w0, rk_w2, rk_a0, rk_a2, rk_g2, rk_kk, rk_ka, rk_rk, rk_ln_g, rk_ln_b, sb_norm_g, ml_conv_w, ml_conv_b, ml_ig_b, ml_fg_b, ml_norm_g, ds_qn_g, ds_kn_g, ds_out_g, w_out, moe_wg, moe_bg, moe_we, moe_be, moe_w1, moe_w3, moe_w2):
    B, S, D = x.shape
    depth = ada_w.shape[0]
    c_act = jax.nn.silu(c)
    for l in range(depth):
        mod = (c_act @ ada_w[l] + ada_b[l])[:, None, :]
        sh1, sc1, gt1, sh2, sc2, gt2 = jnp.split(mod, 6, axis=-1)

        pA, pB, pC, pD = _in_proj(x, sc1, sh1, norm1_g[l][None], _pad_w_in(w_in[l]))

        yA = _rwkv7_time_mix(pA, rk_mu[l], rk_w0[l], rk_w2[l], rk_a0[l], rk_a2[l], rk_g2[l],
                             rk_kk[l], rk_ka[l], rk_rk[l], rk_ln_g[l], rk_ln_b[l])

        yB = _stick_breaking_norm(pB, sb_norm_g[l])

        yC = _mlstm_mix(pC, ml_conv_w[l], ml_conv_b[l], ml_ig_b[l], ml_fg_b[l], ml_norm_g[l])

        yD = _dsa_attn_norm(*_dsa_prep(pD, ds_qn_g[l], ds_kn_g[l]), ds_out_g[l])

        router = jnp.pad(jnp.concatenate([moe_wg[l], moe_we[l]], 1),
                         ((0, 0), (0, ROUTER_PAD - N_GROUPS - N_EXPERTS)))
        r_hi, r_lo = _split_bf16(router)
        r_b = jnp.pad(jnp.concatenate([moe_bg[l], moe_be[l]]), (0, ROUTER_PAD - N_GROUPS - N_EXPERTS))[None]
        x1, h2, route = _out_proj((yA, yB, yC, yD), x, gt1, sc2, sh2, norm2_g[l][None],
                                  w_out[l].astype(BF16), r_hi, r_lo, r_b)

        y1, y2 = _hier_moe(h2.reshape(B * S, D), route.reshape(B * S, ROUTER_PAD), moe_w1, moe_w3, moe_w2, l)
        x = x1 + gt2 * (y1.astype(F32) + y2.astype(F32)).reshape(B, S, D)
    return x
```

```python
import functools
import itertools

import jax
import jax.numpy as jnp
import numpy as np
from jax import lax
from jax.experimental import pallas as pl
from jax.experimental.pallas import tpu as pltpu

F32 = jnp.float32
BF16 = jnp.bfloat16

D_MODEL = 1024
N_MIXERS = 4
GROUP_W = D_MODEL // N_MIXERS
HEAD_DIM = 64
N_HEADS = GROUP_W // HEAD_DIM
NORM_EPS = 1e-6
RWKV_W_RANK = 32
RWKV_A_RANK = 32
RWKV_G_RANK = 64
RWKV_GN_EPS = 64e-5
SB_BLOCK = 128
ML_CHUNK = 64
ML_CONV = 4
GATE_CAP = 15.0
IDX_HEADS = 4
IDX_DIM = 32
TOPK_MAX = 256
ROPE_THETA = 10000.0
N_GROUPS = 4
EXP_PER_GROUP = 8
N_EXPERTS = N_GROUPS * EXP_PER_GROUP
EXPERT_FF = D_MODEL // 2
TOP_IN_GROUP = 2

A_SIZES = (GROUP_W, GROUP_W, GROUP_W, RWKV_W_RANK, RWKV_A_RANK, RWKV_G_RANK)
B_SIZES = (GROUP_W, GROUP_W, GROUP_W)
C_SIZES = (GROUP_W, GROUP_W, GROUP_W, GROUP_W, N_HEADS, N_HEADS)
D_SIZES = (GROUP_W, HEAD_DIM, HEAD_DIM, IDX_HEADS * IDX_DIM, IDX_DIM, IDX_HEADS)
A_COLS = sum(A_SIZES)
B_COLS = sum(B_SIZES)
C_COLS = sum(C_SIZES)
D_COLS = sum(D_SIZES)

LANE = 128
A_PAD = 896
B_PAD = 768
C_PAD = 1152
D_PAD = 640
P_PAD = A_PAD + B_PAD + C_PAD + D_PAD
ROUTER_PAD = LANE

IN_ROWS = 512
OUT_ROWS = 512
MOE_ROWS = 512
VMEM_LIMIT = 48 * 1024 * 1024


def _split_cols(t, sizes):
    return jnp.split(t, [int(i) for i in np.cumsum(sizes)[:-1]], axis=-1)


def _in_proj_kernel(x_ref, sc_ref, sh_ref, g_ref, w_ref, oa_ref, ob_ref, oc_ref, od_ref):
    x = x_ref[0]
    y = x * lax.rsqrt(jnp.mean(x * x, -1, keepdims=True) + NORM_EPS) * g_ref[...]
    h = y * (1.0 + sc_ref[0]) + sh_ref[0]
    p = jnp.dot(h.astype(BF16), w_ref[...], preferred_element_type=F32)
    oa_ref[0] = p[:, :A_PAD]
    ob_ref[0] = p[:, A_PAD:A_PAD + B_PAD]
    oc_ref[0] = p[:, A_PAD + B_PAD:A_PAD + B_PAD + C_PAD]
    od_ref[0] = p[:, A_PAD + B_PAD + C_PAD:]


def _in_proj(x, sc, sh, g, w_pad):
    B, S, D = x.shape
    row = lambda w: pl.BlockSpec((1, IN_ROWS, w), lambda b, i: (b, i, 0))
    vec = pl.BlockSpec((1, 1, D), lambda b, i: (b, 0, 0))
    return pl.pallas_call(
        _in_proj_kernel,
        grid=(B, S // IN_ROWS),
        in_specs=[row(D), vec, vec, pl.BlockSpec((1, D), lambda b, i: (0, 0)),
                  pl.BlockSpec((D, P_PAD), lambda b, i: (0, 0))],
        out_specs=[row(A_PAD), row(B_PAD), row(C_PAD), row(D_PAD)],
        out_shape=[jax.ShapeDtypeStruct((B, S, w), F32) for w in (A_PAD, B_PAD, C_PAD, D_PAD)],
        compiler_params=pltpu.CompilerParams(dimension_semantics=("parallel", "parallel"),
                                             vmem_limit_bytes=VMEM_LIMIT),
        name="in_proj",
    )(x, sc, sh, g, w_pad)


def _pad_w_in(w):
    wa, wb, wc, wd = _split_cols(w, (A_COLS, B_COLS, C_COLS, D_COLS))
    padc = lambda t, n: jnp.pad(t, ((0, 0), (0, n - t.shape[1])))
    return jnp.concatenate([padc(wa, A_PAD), padc(wb, B_PAD), padc(wc, C_PAD), padc(wd, D_PAD)], 1).astype(BF16)


def _split_bf16(t):
    hi = t.astype(BF16)
    lo = (t - hi.astype(F32)).astype(BF16)
    return hi, lo


def _out_proj_kernel(ya_ref, yb_ref, yc_ref, yd_ref, x_ref, gt_ref, sc_ref, sh_ref, g_ref, w_ref,
                     rhi_ref, rlo_ref, rb_ref, x1_ref, h2_ref, route_ref):
    acc = jnp.zeros(x_ref.shape[1:], F32)
    for n, y_ref in enumerate((ya_ref, yb_ref, yc_ref, yd_ref)):
        acc += jnp.dot(y_ref[0].astype(BF16), w_ref[n * GROUP_W:(n + 1) * GROUP_W, :],
                       preferred_element_type=F32)
    x1 = x_ref[0] + gt_ref[0] * acc
    x1_ref[0] = x1
    y = x1 * lax.rsqrt(jnp.mean(x1 * x1, -1, keepdims=True) + NORM_EPS) * g_ref[...]
    h = y * (1.0 + sc_ref[0]) + sh_ref[0]
    hi, lo = _split_bf16(h)
    h2_ref[0] = hi
    lg = (jnp.dot(hi, rhi_ref[...], preferred_element_type=F32)
          + jnp.dot(lo, rhi_ref[...], preferred_element_type=F32)
          + jnp.dot(hi, rlo_ref[...], preferred_element_type=F32)) + rb_ref[...]
    route_ref[0] = _route(lg)


def _route(lg):
    lane = lax.broadcasted_iota(jnp.int32, lg.shape, 1)
    neg = -jnp.inf
    first = lambda hit: jnp.min(jnp.where(hit, lane, ROUTER_PAD), axis=1, keepdims=True)
    is_grp = lane < N_GROUPS
    grp = jnp.where(is_grp, lg, neg)
    g_max = jnp.max(grp, axis=1, keepdims=True)
    g_p = 1.0 / jnp.sum(jnp.where(is_grp, jnp.exp(grp - g_max), 0.0), axis=1, keepdims=True)
    g_idx = first(grp == g_max)
    e_lane = lane - N_GROUPS
    in_group = (e_lane >= 0) & (e_lane < N_EXPERTS) & (e_lane // EXP_PER_GROUP == g_idx)
    e_log = jnp.where(in_group, lg, neg)
    e1_max = jnp.max(e_log, axis=1, keepdims=True)
    e1_lane = first(e_log == e1_max)
    e_log2 = jnp.where(lane == e1_lane, neg, e_log)
    e2_max = jnp.max(e_log2, axis=1, keepdims=True)
    e2_lane = first(e_log2 == e2_max)
    ratio = jnp.exp(e2_max - e1_max)
    gate1 = g_p / (1.0 + ratio)
    gate2 = gate1 * ratio
    out = jnp.where(lane == 0, (e1_lane - N_GROUPS).astype(F32), 0.0)
    out = jnp.where(lane == 1, (e2_lane - N_GROUPS).astype(F32), out)
    out = jnp.where(lane == 2, gate1, out)
    return jnp.where(lane == 3, gate2, out)


def _out_proj(ys, x, gt, sc, sh, g, w_out, r_hi, r_lo, r_b):
    B, S, D = x.shape
    row = lambda w: pl.BlockSpec((1, OUT_ROWS, w), lambda b, i: (b, i, 0))
    vec = pl.BlockSpec((1, 1, D), lambda b, i: (b, 0, 0))
    full = lambda a: pl.BlockSpec(a.shape, lambda b, i: (0,) * a.ndim)
    return pl.pallas_call(
        _out_proj_kernel,
        grid=(B, S // OUT_ROWS),
        in_specs=[row(GROUP_W)] * 4 + [row(D), vec, vec, vec, full(g), full(w_out), full(r_hi), full(r_lo),
                                        full(r_b)],
        out_specs=[row(D), row(D), row(ROUTER_PAD)],
        out_shape=[jax.ShapeDtypeStruct((B, S, D), F32), jax.ShapeDtypeStruct((B, S, D), BF16),
                   jax.ShapeDtypeStruct((B, S, ROUTER_PAD), F32)],
        compiler_params=pltpu.CompilerParams(dimension_semantics=("parallel", "parallel"),
                                             vmem_limit_bytes=VMEM_LIMIT),
        name="out_proj",
    )(*ys, x, gt, sc, sh, g, w_out, r_hi, r_lo, r_b)


def _moe_ffn_kernel(blk_e_ref, x_ref, wt_ref, w1_ref, w3_ref, w2_ref, o_ref, w1b_ref, w3b_ref, w2b_ref):
    i = pl.program_id(0)
    changed = jnp.logical_or(i == 0, blk_e_ref[i] != blk_e_ref[jnp.maximum(i - 1, 0)])

    @pl.when(changed)
    def _():
        w1b_ref[...] = w1_ref[0, 0].astype(BF16)
        w3b_ref[...] = w3_ref[0, 0].astype(BF16)
        w2b_ref[...] = w2_ref[0, 0].astype(BF16)

    xb = x_ref[...]
    a = jnp.dot(xb, w1b_ref[...], preferred_element_type=F32)
    b = jnp.dot(xb, w3b_ref[...], preferred_element_type=F32)
    hmid = (a * jax.nn.sigmoid(a) * b).astype(BF16)
    y = jnp.dot(hmid, w2b_ref[...], preferred_element_type=F32)
    o_ref[...] = (y * wt_ref[...]).astype(o_ref.dtype)


def _moe_ffn(blk_e, xs, wt, w1, w3, w2, layer):
    n_slots, D = xs.shape
    n_blocks = n_slots // MOE_ROWS
    FF = w1.shape[-1]
    return pl.pallas_call(
        _moe_ffn_kernel,
        grid_spec=pltpu.PrefetchScalarGridSpec(
            num_scalar_prefetch=1,
            grid=(n_blocks,),
            in_specs=[pl.BlockSpec((MOE_ROWS, D), lambda i, e: (i, 0)),
                      pl.BlockSpec((MOE_ROWS, 1), lambda i, e: (i, 0)),
                      pl.BlockSpec((1, 1, D, FF), lambda i, e: (layer, e[i], 0, 0)),
                      pl.BlockSpec((1, 1, D, FF), lambda i, e: (layer, e[i], 0, 0)),
                      pl.BlockSpec((1, 1, FF, D), lambda i, e: (layer, e[i], 0, 0))],
            out_specs=pl.BlockSpec((MOE_ROWS, D), lambda i, e: (i, 0)),
            scratch_shapes=[pltpu.VMEM((D, FF), BF16), pltpu.VMEM((D, FF), BF16), pltpu.VMEM((FF, D), BF16)],
        ),
        out_shape=jax.ShapeDtypeStruct((n_slots, D), BF16),
        compiler_params=pltpu.CompilerParams(dimension_semantics=("arbitrary",),
                                             vmem_limit_bytes=VMEM_LIMIT),
        name="moe_ffn",
    )(blk_e, xs, wt, w1, w3, w2)


def _hier_moe(h2, route, w1, w3, w2, layer):
    N, D = h2.shape
    expert = route[:, :TOP_IN_GROUP].astype(jnp.int32)
    gate = route[:, TOP_IN_GROUP:2 * TOP_IN_GROUP]
    n_asg = N * TOP_IN_GROUP
    n_blocks = n_asg // MOE_ROWS + N_EXPERTS
    n_slots = n_blocks * MOE_ROWS
    n_fill = n_slots - n_asg
    flat_e = expert.reshape(n_asg // LANE, LANE)
    counts = jnp.sum(flat_e[None] == jnp.arange(N_EXPERTS)[:, None, None], axis=(1, 2)).astype(jnp.int32)
    pad_counts = (counts + MOE_ROWS - 1) // MOE_ROWS * MOE_ROWS
    pad_end = jnp.cumsum(pad_counts)
    blk_start = jnp.arange(n_blocks) * MOE_ROWS
    blk_e = jnp.minimum(jnp.sum(pad_end[None, :] <= blk_start[:, None], 1), N_EXPERTS - 1).astype(jnp.int32)
    fill_end = jnp.cumsum(pad_counts - counts)
    fill_id = jnp.arange(n_fill).reshape(n_fill // LANE, LANE)
    fill_e = jnp.zeros_like(fill_id)
    for e in range(N_EXPERTS):
        fill_e = fill_e + (fill_id >= fill_end[e])
    keys = jnp.concatenate([flat_e.reshape(n_asg) * 2, fill_e.reshape(n_fill) * 2 + 1])
    toks = jnp.concatenate([jnp.arange(n_asg, dtype=jnp.int32) // TOP_IN_GROUP, jnp.zeros((n_fill,), jnp.int32)])
    wts = jnp.concatenate([gate.reshape(n_asg), jnp.zeros((n_fill,), F32)])
    _, slot_tok, slot_w, slot_src = lax.sort((keys, toks, wts, jnp.arange(n_slots, dtype=jnp.int32)), num_keys=1)
    _, entry_slot = lax.sort((slot_src, jnp.arange(n_slots, dtype=jnp.int32)), num_keys=1)
    asg_slot = entry_slot[:n_asg].reshape(N, TOP_IN_GROUP)
    yb = _moe_ffn(blk_e, h2[slot_tok], slot_w[:, None], w1, w3, w2, layer)
    return yb[asg_slot[:, 0]], yb[asg_slot[:, 1]]


INT_MIN = -2 ** 31
DSA_KEY_STEP = 256
DSA_QUERIES = 128


def _float_order_key(x):
    bits = pltpu.bitcast(x, jnp.int32)
    bits = jnp.where(x == 0.0, 0, bits)
    return bits ^ ((bits >> 31) & 0x7FFFFFFF)


COL_PART = 64


def _col_reduce(x, reduce):
    part = reduce(x.reshape(x.shape[0] // COL_PART, COL_PART, x.shape[1]), axis=0)
    return reduce(part, axis=0, keepdims=True)


def _col_count(mask):
    return _col_reduce(jnp.where(mask, 1.0, 0.0), jnp.sum)


def _head_block_diag(t, group):
    n_heads = t.shape[0] // group
    row_h = lax.broadcasted_iota(jnp.int32, t.shape, 0) // group
    return jnp.concatenate([jnp.where(row_h == h, t, 0.0) for h in range(n_heads)], axis=1)


def _dsa_block_t(qd_ref, kd4_ref, vdwt_ref, qi_ref, kihi_ref, kilo_ref, wi_ref, g_ref, o_ref, *, kl, n_sel):
    Q = DSA_QUERIES
    q0 = pl.program_id(1) * Q
    w_hi, w_lo = _split_bf16(_head_block_diag(jnp.transpose(qi_ref[0]), IDX_DIM))
    k_hi, k_lo = kihi_ref[0, :kl, :], kilo_ref[0, :kl, :]
    sc = _dot(jnp.concatenate([k_hi, k_lo, k_hi], axis=1),
              jnp.concatenate([w_hi, w_hi, w_lo], axis=0))
    wit = jnp.transpose(wi_ref[0])
    score = sum(wit[h:h + 1, :] * jnp.maximum(sc[:, h * Q:(h + 1) * Q], 0.0) for h in range(IDX_HEADS))
    kidx = lax.broadcasted_iota(jnp.int32, (kl, Q), 0)
    qpos = q0 + lax.broadcasted_iota(jnp.int32, (kl, Q), 1)
    adm = kidx <= qpos
    key = _float_order_key(jnp.where(adm, score, -jnp.inf))

    def value_bit(it, tau):
        cand = tau | jnp.left_shift(jnp.int32(1), 31 - it)
        return jnp.where(_col_count(key >= (cand ^ INT_MIN)) >= n_sel, cand, tau)

    tau = lax.fori_loop(0, 32, value_bit, jnp.zeros((1, Q), jnp.int32)) ^ INT_MIN
    gt = key > tau
    eq = (key == tau) & adm
    need = n_sel - _col_count(gt)
    n_eq = _col_count(eq)

    index_width = kl.bit_length()

    def index_bits():
        def index_bit(it, bound):
            cand = bound | jnp.left_shift(jnp.int32(1), index_width - 1 - it)
            return jnp.where(_col_count(eq & (kidx < cand)) <= need, cand, bound)
        return lax.fori_loop(0, index_width, index_bit, jnp.zeros((1, Q), jnp.int32))

    bound = lax.cond(jnp.max(n_eq - need) > 0.0, index_bits, lambda: jnp.full((1, Q), kl, jnp.int32))
    sel = gt | (eq & (kidx < bound))

    w_att = _head_block_diag(jnp.transpose(qd_ref[0]), HEAD_DIM).astype(BF16)
    lg = _dot(kd4_ref[0, :kl, :], w_att) * HEAD_DIM ** -0.5
    lg = jnp.where(jnp.concatenate([sel] * N_HEADS, axis=1), lg, -jnp.inf)
    p = jnp.exp(lg - _col_reduce(lg, jnp.max))
    out_t = _dot(vdwt_ref[0, :, :kl], p.astype(BF16)) / _col_reduce(p, jnp.sum)
    row_h = lax.broadcasted_iota(jnp.int32, (GROUP_W, Q), 0) // HEAD_DIM
    nat_t = sum(jnp.where(row_h == h, out_t[:, h * Q:(h + 1) * Q], 0.0) for h in range(N_HEADS))
    r = jnp.transpose(nat_t)
    hi_ = lax.broadcasted_iota(jnp.int32, (GROUP_W, GROUP_W), 0) // HEAD_DIM
    hj_ = lax.broadcasted_iota(jnp.int32, (GROUP_W, GROUP_W), 1) // HEAD_DIM
    ms = _dot_f32_by_exact(r * r, jnp.where(hi_ == hj_, 1.0, 0.0).astype(BF16)) * (1.0 / HEAD_DIM)
    o_ref[0] = r * lax.rsqrt(ms + NORM_EPS) * g_ref[...]


def _dsa_kernel(qd_ref, kd4_ref, vdwt_ref, qi_ref, kihi_ref, kilo_ref, wi_ref, g_ref, o_ref, *, kls, n_sel):
    blocks_per_step = DSA_KEY_STEP // DSA_QUERIES
    for j, kl in enumerate(kls):
        @pl.when(pl.program_id(1) // blocks_per_step == j)
        def _():
            _dsa_block_t(qd_ref, kd4_ref, vdwt_ref, qi_ref, kihi_ref, kilo_ref, wi_ref, g_ref, o_ref,
                         kl=kl, n_sel=n_sel)


DSA_PREP_ROWS = 256
D_Q, D_KV, D_QI, D_KW = 0, GROUP_W, GROUP_W + LANE, GROUP_W + 2 * LANE


def _swap_halves(x, half):
    n = x.shape[1]
    lane = lax.broadcasted_iota(jnp.int32, x.shape, 1)
    return jnp.where(lane % (2 * half) < half, pltpu.roll(x, n - half, axis=1), pltpu.roll(x, half, axis=1))


def _dsa_prep_kernel(p_ref, cq_ref, sq_ref, ci_ref, si_ref, gq_ref, gk_ref,
                     qd_ref, kd4_ref, vdwt_ref, qi_ref, kihi_ref, kilo_ref, wi_ref):
    GW = GROUP_W
    x = p_ref[0]
    hi_ = lax.broadcasted_iota(jnp.int32, (GW, GW), 0) // HEAD_DIM
    hj_ = lax.broadcasted_iota(jnp.int32, (GW, GW), 1) // HEAD_DIM
    ones_bd = jnp.where(hi_ == hj_, 1.0, 0.0).astype(BF16)
    q = x[:, D_Q:D_Q + GW]
    q = q * lax.rsqrt(_dot_f32_by_exact(q * q, ones_bd) * (1.0 / HEAD_DIM) + NORM_EPS) * gq_ref[...]
    qd_ref[0] = q * cq_ref[...] + _swap_halves(q, HEAD_DIM // 2) * sq_ref[...]
    kv = x[:, D_KV:D_KV + LANE]
    lane = lax.broadcasted_iota(jnp.int32, kv.shape, 1)
    is_k = lane < HEAD_DIM
    ms = jnp.sum(jnp.where(is_k, kv * kv, 0.0), axis=1, keepdims=True) * (1.0 / HEAD_DIM)
    kn = kv * lax.rsqrt(ms + NORM_EPS) * gk_ref[...]
    kr = kn * cq_ref[:, :LANE] + _swap_halves(kn, HEAD_DIM // 2) * sq_ref[:, :LANE]
    k2 = jnp.where(is_k, kr, pltpu.roll(kr, HEAD_DIM, axis=1))
    kd4_ref[0] = jnp.concatenate([k2] * (GW // LANE), axis=1).astype(BF16)
    v2 = jnp.where(is_k, pltpu.roll(kv, HEAD_DIM, axis=1), kv)
    v2t = jnp.transpose(v2)
    vdwt_ref[0] = jnp.concatenate([v2t] * (GW // LANE), axis=0).astype(BF16)
    qi = x[:, D_QI:D_QI + LANE]
    qi_ref[0] = qi * ci_ref[...] + _swap_halves(qi, IDX_DIM // 2) * si_ref[...]
    kw = x[:, D_KW:D_KW + LANE]
    kir = kw * ci_ref[...] + _swap_halves(kw, IDX_DIM // 2) * si_ref[...]
    ki1 = jnp.where(lane < IDX_DIM, kir, 0.0)
    ki2 = ki1 + pltpu.roll(ki1, IDX_DIM, axis=1)
    ki4 = ki2 + pltpu.roll(ki2, 2 * IDX_DIM, axis=1)
    kihi_ref[0], kilo_ref[0] = _split_bf16(ki4)
    wi_ref[0] = pltpu.roll(kw, LANE - IDX_DIM, axis=1) * (IDX_HEADS ** -0.5 * IDX_DIM ** -0.5)


def _rope_tables(S, dim, width):
    half = dim // 2
    inv = ROPE_THETA ** (-jnp.arange(half, dtype=F32) / half)
    ang = jnp.arange(S, dtype=F32)[:, None] * inv[None, :]
    cos = jnp.tile(jnp.cos(ang), (1, width // half))
    sin = jnp.tile(jnp.concatenate([-jnp.sin(ang), jnp.sin(ang)], axis=1), (1, width // dim))
    return cos, sin


def _dsa_prep(p, qn_g, kn_g):
    B, S, _ = p.shape
    GW, R = GROUP_W, DSA_PREP_ROWS
    cq, sq = _rope_tables(S, HEAD_DIM, GW)
    ci, si = _rope_tables(S, IDX_DIM, LANE)
    gq = jnp.tile(qn_g, N_HEADS)[None]
    gk = jnp.pad(kn_g, (0, LANE - HEAD_DIM))[None]
    rows = lambda w: pl.BlockSpec((1, R, w), lambda b, i: (b, i, 0))
    tab = lambda w: pl.BlockSpec((R, w), lambda b, i: (i, 0))
    cols = lambda r: pl.BlockSpec((1, r, R), lambda b, i: (b, 0, i))
    vec = lambda w: pl.BlockSpec((1, w), lambda b, i: (0, 0))
    return pl.pallas_call(
        _dsa_prep_kernel,
        grid=(B, S // R),
        in_specs=[rows(D_PAD), tab(GW), tab(GW), tab(LANE), tab(LANE), vec(GW), vec(LANE)],
        out_specs=[rows(GW), rows(GW), cols(GW), rows(LANE), rows(LANE), rows(LANE), rows(LANE)],
        out_shape=[jax.ShapeDtypeStruct((B, S, GW), F32), jax.ShapeDtypeStruct((B, S, GW), BF16),
                   jax.ShapeDtypeStruct((B, GW, S), BF16), jax.ShapeDtypeStruct((B, S, LANE), F32),
                   jax.ShapeDtypeStruct((B, S, LANE), BF16), jax.ShapeDtypeStruct((B, S, LANE), BF16),
                   jax.ShapeDtypeStruct((B, S, LANE), F32)],
        compiler_params=pltpu.CompilerParams(dimension_semantics=("parallel", "parallel"),
                                             vmem_limit_bytes=VMEM_LIMIT),
        name="dsa_prep",
    )(p, cq, sq, ci, si, gq, gk)


def _dsa_attn_norm(qd, kd4, vdwt, qi, ki_hi, ki_lo, wi, g):
    B, S, _ = qd.shape
    n_sel = min(TOPK_MAX, S // 4)
    assert S % DSA_KEY_STEP == 0 and n_sel <= DSA_KEY_STEP
    kls = tuple(range(DSA_KEY_STEP, S + 1, DSA_KEY_STEP))
    blk = lambda w: pl.BlockSpec((1, DSA_QUERIES, w), lambda b, i: (b, i, 0))
    per_b = lambda r, c: pl.BlockSpec((1, r, c), lambda b, i: (b, 0, 0))
    return pl.pallas_call(
        functools.partial(_dsa_kernel, kls=kls, n_sel=n_sel),
        grid=(B, S // DSA_QUERIES),
        in_specs=[blk(GROUP_W), per_b(S, GROUP_W), per_b(GROUP_W, S), blk(IDX_HEADS * IDX_DIM),
                  per_b(S, LANE), per_b(S, LANE), blk(LANE), pl.BlockSpec((1, GROUP_W), lambda b, i: (0, 0))],
        out_specs=blk(GROUP_W),
        out_shape=jax.ShapeDtypeStruct((B, S, GROUP_W), F32),
        compiler_params=pltpu.CompilerParams(dimension_semantics=("parallel", "parallel"),
                                             vmem_limit_bytes=VMEM_LIMIT),
        name="dsa_attn",
    )(qd, kd4, vdwt, qi, ki_hi, ki_lo, wi, g[None])


RWKV_CHUNK = 64
RWKV_LOW = RWKV_W_RANK + RWKV_A_RANK + RWKV_G_RANK
RWKV_GROUP = N_HEADS
RWKV_BATCH = 8


def _dot(a, b):
    return jnp.dot(a, b, preferred_element_type=F32)


def _dot_nt(a, b):
    return lax.dot_general(a, b, (((1,), (1,)), ((), ())), preferred_element_type=F32)


def _dot_tn(a, b):
    return lax.dot_general(a, b, (((0,), (0,)), ((), ())), preferred_element_type=F32)


def _split3_bf16(t):
    p1 = t.astype(BF16)
    r1 = t - p1.astype(F32)
    p2 = r1.astype(BF16)
    p3 = (r1 - p2.astype(F32)).astype(BF16)
    return p1, p2, p3


def _dot_f32_by_exact(a, b_exact):
    m = a.shape[0]
    r = _dot(jnp.concatenate(_split3_bf16(a), axis=0), b_exact)
    return r[:m] + r[m:2 * m] + r[2 * m:]


def _dot_exact_by_f32(a_exact, b):
    n = b.shape[1]
    r = _dot(a_exact, jnp.concatenate(_split3_bf16(b), axis=1))
    return r[:, :n] + r[:, n:2 * n] + r[:, 2 * n:]


def _dot3(a, b_hi, b_lo):
    a_hi, a_lo = _split_bf16(a)
    return _dot(jnp.concatenate([a_hi, a_lo, a_hi], axis=1), jnp.concatenate([b_hi, b_hi, b_lo], axis=0))


def _softplus(z):
    return jnp.maximum(z, 0.0) + jnp.log(1.0 + jnp.exp(-jnp.abs(z)))


def _rwkv_kernel(p_ref, mu_ref, vec_ref, lhi_ref, llo_ref, o_ref, state_ref, prev_ref):
    @pl.when(pl.program_id(1) == 0)
    def _():
        state_ref[...] = jnp.zeros_like(state_ref)
        prev_ref[...] = jnp.zeros_like(prev_ref)

    masks = _rwkv_masks()
    chains = [_rwkv_chunk(masks, p_ref.at[n], mu_ref, vec_ref, lhi_ref, llo_ref, o_ref.at[n], state_ref.at[n],
                          prev_ref.at[n]) for n in range(RWKV_BATCH)]
    for _ in itertools.zip_longest(*chains):
        pass


def _rwkv_masks():
    L, GW, SG = RWKV_CHUNK, GROUP_W, RWKV_GROUP * RWKV_CHUNK
    ri = lax.broadcasted_iota(jnp.int32, (GW, GW), 0)
    ci = lax.broadcasted_iota(jnp.int32, (GW, GW), 1)
    ones_bd = jnp.where((ri // HEAD_DIM) == (ci // HEAD_DIM), 1.0, 0.0).astype(BF16)
    ti = lax.broadcasted_iota(jnp.int32, (L, L), 0)
    tj = lax.broadcasted_iota(jnp.int32, (L, L), 1)
    tri = jnp.where(tj <= ti, 1.0, 0.0).astype(BF16)
    gi = lax.broadcasted_iota(jnp.int32, (SG, SG), 0)
    gj = lax.broadcasted_iota(jnp.int32, (SG, SG), 1)
    g_same = (gi // L) == (gj // L)
    strict = g_same & ((gj % L) < (gi % L))
    incl = g_same & ((gj % L) <= (gi % L))
    diag = gi == gj
    return ones_bd, tri, g_same, strict, incl, diag, jnp.where(diag, 1.0, 0.0)


def _rwkv_chunk(masks, p_ref, mu_ref, vec_ref, lhi_ref, llo_ref, o_ref, state_ref, prev_ref):
    L, GW = RWKV_CHUNK, GROUP_W
    p = p_ref[...]
    row = lax.broadcasted_iota(jnp.int32, p.shape, 0)
    prev = jnp.where(row == 0, prev_ref[...], pltpu.roll(p, 1, axis=0))
    prev_ref[...] = p[L - 1:L, :]
    ps = p + (prev - p) * mu_ref[...]
    r, k, v = ps[:, :GW], ps[:, GW:2 * GW], ps[:, 2 * GW:3 * GW]
    low = ps[:, 3 * GW:]
    lane_low = lax.broadcasted_iota(jnp.int32, low.shape, 1)
    low = jnp.where(lane_low < RWKV_W_RANK, jnp.tanh(low),
                    jnp.where(lane_low < RWKV_W_RANK + RWKV_A_RANK, low, jax.nn.sigmoid(low)))
    up = _dot3(low, lhi_ref[...], llo_ref[...])
    w0, a0, k_k, k_a = vec_ref[0:1, :], vec_ref[1:2, :], vec_ref[2:3, :], vec_ref[3:4, :]
    r_k, ln_g, ln_b = vec_ref[4:5, :], vec_ref[5:6, :], vec_ref[6:7, :]
    logw = -jnp.exp(-_softplus(-(w0 + up[:, :GW])) - 0.5)
    rate = jax.nn.sigmoid(a0 + up[:, GW:2 * GW])
    gate = up[:, 2 * GW:]

    ones_bd, tri, g_same, strict, incl, diag, eye = masks
    kk = k * k_k
    k = k * (1.0 + (rate - 1.0) * k_a)
    seg = _dot_f32_by_exact(jnp.concatenate([kk * kk, r * k * r_k], axis=0), ones_bd)
    kk = kk / jnp.maximum(jnp.sqrt(seg[:L]), 1e-12)

    lc = _dot_exact_by_f32(tri, logw)
    lc_last = lc[L - 1:L, :]
    dec_in = jnp.exp(lc)
    dec_out = jnp.exp(-lc)
    a_t = -kk * jnp.exp(lc - logw)
    b_t = kk * rate * dec_out
    k_t = k * dec_out
    r_t = r * dec_in
    to_end = jnp.exp(lc_last)

    SG = RWKV_GROUP * L
    stack = lambda t: jnp.concatenate([t] * RWKV_GROUP, axis=0)
    bd = lambda t: jnp.where(g_same, stack(t), 0.0).astype(BF16)
    n_doublings = RWKV_CHUNK.bit_length() - 2
    ys = []
    for g in range(N_HEADS // RWKV_GROUP):
        cols = slice(g * SG, (g + 1) * SG)
        a_bd, r_bd, v_bd = bd(a_t[:, cols]), bd(r_t[:, cols]), bd(v[:, cols])
        m = _dot_nt(jnp.concatenate([a_bd, r_bd], axis=0),
                    jnp.concatenate([stack(b_t[:, cols]), stack(k_t[:, cols])], axis=0).astype(BF16))
        yield
        m_ab = jnp.where(strict, m[:SG, :SG], 0.0)
        m_ak = jnp.where(strict, m[:SG, SG:], 0.0).astype(BF16)
        m_rb = jnp.where(incl, m[SG:, :SG], 0.0).astype(BF16)
        m_rk = jnp.where(incl, m[SG:, SG:], 0.0).astype(BF16)

        inv = eye + m_ab
        pw_b = m_ab.astype(BF16)
        sq = _dot(pw_b, pw_b)
        yield
        pw_b = sq.astype(BF16)
        for s in range(n_doublings - 1):
            both = _dot(jnp.concatenate([inv.astype(BF16), pw_b], axis=0), pw_b)
            yield
            inv = inv + both[:SG]
            pw_b = both[SG:].astype(BF16)
        last = _dot(inv.astype(BF16), pw_b)
        t0 = state_ref[g]
        t0_b = t0.astype(BF16)
        rhs = _dot(jnp.concatenate([a_bd, m_ak], axis=1), jnp.concatenate([t0_b, v_bd], axis=0))
        yield
        inv_b = (inv + last).astype(BF16)
        u = _dot(inv_b, rhs.astype(BF16))
        yield
        u = u.astype(BF16)
        y_bd = _dot(jnp.concatenate([r_bd, m_rb, m_rk], axis=1),
                    jnp.concatenate([t0_b, u, v_bd], axis=0))
        end_g = to_end[:, cols]
        to_end_col = jnp.sum(jnp.where(diag, jnp.broadcast_to(end_g, (SG, SG)), 0.0), axis=1, keepdims=True)
        carried = _dot_tn(jnp.concatenate([bd(b_t[:, cols] * end_g), bd(k_t[:, cols] * end_g)], axis=0),
                          jnp.concatenate([u, v_bd], axis=0))
        yield
        ys.append(sum(y_bd[h * L:(h + 1) * L, :] for h in range(RWKV_GROUP)))
        state_ref[g] = to_end_col * t0 + carried
    y = jnp.concatenate(ys, axis=1)

    inv_d = 1.0 / HEAD_DIM
    mean = _dot_f32_by_exact(y, ones_bd) * inv_d
    yield
    yc = y - mean
    var = _dot_f32_by_exact(yc * yc, ones_bd) * inv_d
    yield
    yn = yc * lax.rsqrt(var + RWKV_GN_EPS) * ln_g + ln_b
    o_ref[...] = (yn + seg[L:] * v) * gate


def _rwkv7_time_mix(p, mu, w0, w2, a0, a2, g2, k_k, k_a, r_k, ln_g, ln_b):
    B, S, _ = p.shape
    GW = GROUP_W
    assert S % RWKV_CHUNK == 0 and RWKV_CHUNK == HEAD_DIM and B % RWKV_BATCH == 0
    low_w = jnp.zeros((RWKV_LOW, 3 * GW), F32)
    low_w = low_w.at[:RWKV_W_RANK, :GW].set(w2)
    low_w = low_w.at[RWKV_W_RANK:RWKV_W_RANK + RWKV_A_RANK, GW:2 * GW].set(a2)
    low_w = low_w.at[RWKV_W_RANK + RWKV_A_RANK:, 2 * GW:].set(g2)
    l_hi, l_lo = _split_bf16(low_w)
    vecs = jnp.stack([w0, a0, k_k, k_a, r_k, ln_g, ln_b, jnp.zeros_like(w0)], 0)
    full = lambda a: pl.BlockSpec(a.shape, lambda b, c: (0,) * a.ndim)
    mu2 = mu[None]
    return pl.pallas_call(
        _rwkv_kernel,
        grid=(B // RWKV_BATCH, S // RWKV_CHUNK),
        in_specs=[pl.BlockSpec((RWKV_BATCH, RWKV_CHUNK, A_PAD), lambda b, c: (b, c, 0)),
                  full(mu2), full(vecs), full(l_hi), full(l_lo)],
        out_specs=pl.BlockSpec((RWKV_BATCH, RWKV_CHUNK, GW), lambda b, c: (b, c, 0)),
        out_shape=jax.ShapeDtypeStruct((B, S, GW), F32),
        scratch_shapes=[pltpu.VMEM((RWKV_BATCH, N_HEADS // RWKV_GROUP, RWKV_GROUP * RWKV_CHUNK,
                                    RWKV_GROUP * HEAD_DIM), F32),
                        pltpu.VMEM((RWKV_BATCH, 1, A_PAD), F32)],
        compiler_params=pltpu.CompilerParams(dimension_semantics=("parallel", "arbitrary"),
                                             vmem_limit_bytes=VMEM_LIMIT),
        name="rwkv7",
    )(p, mu2, vecs, l_hi, l_lo)


SB_UNROLL = 4


def _sb_kernel(q_ref, k_ref, v_ref, g_ref, o_ref, kbd_ref, vbd_ref):
    i = pl.program_id(1)
    T, GW, H = SB_BLOCK, GROUP_W, N_HEADS
    lane_h = lax.broadcasted_iota(jnp.int32, (T, GW), 1) // HEAD_DIM
    k_new, v_new = k_ref[0], v_ref[0]
    for h in range(H):
        kbd_ref[i, h * T:(h + 1) * T, :] = jnp.where(lane_h == h, k_new, 0.0).astype(BF16)
        vbd_ref[i, h * T:(h + 1) * T, :] = jnp.where(lane_h == h, v_new, 0.0).astype(BF16)

    q = (q_ref[0] * HEAD_DIM ** -0.5).astype(BF16)
    si = lax.broadcasted_iota(jnp.int32, (T, 2 * T), 0)
    sj = lax.broadcasted_iota(jnp.int32, (T, 2 * T), 1)
    later_and_all = jnp.where((si > sj) | (sj >= T), 1.0, 0.0).astype(BF16)
    qrow = lax.broadcasted_iota(jnp.int32, (T, H * T), 0)
    kcol = lax.broadcasted_iota(jnp.int32, (T, H * T), 1) % T
    causal = kcol < qrow

    def key_block_stages(j, box, diagonal):
        z = _dot_nt(q, kbd_ref[j])
        yield
        log1m = -(jnp.maximum(z, 0.0) + jnp.log(1.0 + jnp.exp(-jnp.abs(z))))
        log_sig = z + log1m
        log1m_in = (jnp.where(causal, log1m, 0.0) if diagonal else log1m).astype(BF16)
        sums = _dot(jnp.concatenate([log1m_in[:, h * T:(h + 1) * T] for h in range(H)], axis=0), later_and_all)
        yield
        suffix = jnp.concatenate([sums[h * T:(h + 1) * T, :T] for h in range(H)], axis=1)
        total = jnp.concatenate([sums[h * T:(h + 1) * T, T:] for h in range(H)], axis=1)
        att = jnp.exp(log_sig + suffix + box["carry"])
        if diagonal:
            att = jnp.where(causal, att, 0.0)
        box["carry"] = box["carry"] + total
        pv = _dot(att.astype(BF16), vbd_ref[j])
        yield
        box["acc"] = box["acc"] + pv

    def key_blocks(js, state, diagonal=False):
        box = {"carry": state[0], "acc": state[1]}
        for _ in itertools.zip_longest(*[key_block_stages(j, box, diagonal) for j in js]):
            pass
        return box["carry"], box["acc"]

    state = key_blocks([i], (jnp.zeros((T, H * T), F32), jnp.zeros((T, GW), F32)), True)
    rem = i % SB_UNROLL
    state = lax.fori_loop(0, rem, lambda it, st: key_blocks([i - 1 - it], st), state)
    top = i - 1 - rem
    _, y = lax.fori_loop(0, i // SB_UNROLL,
                         lambda it, st: key_blocks([top - SB_UNROLL * it - n for n in range(SB_UNROLL)], st), state)

    hi_ = lax.broadcasted_iota(jnp.int32, (GW, GW), 0) // HEAD_DIM
    hj_ = lax.broadcasted_iota(jnp.int32, (GW, GW), 1) // HEAD_DIM
    ones_bd = jnp.where(hi_ == hj_, 1.0, 0.0).astype(BF16)
    ms = _dot_f32_by_exact(y * y, ones_bd) * (1.0 / HEAD_DIM)
    o_ref[0] = y * lax.rsqrt(ms + NORM_EPS) * g_ref[...]


def _stick_breaking_norm(p, g):
    B, S, _ = p.shape
    GW = GROUP_W
    assert S % SB_BLOCK == 0 and HEAD_DIM ** -0.5 == 2.0 ** -(HEAD_DIM.bit_length() // 2)
    col = lambda n: pl.BlockSpec((1, SB_BLOCK, GW), lambda b, i: (b, i, n))
    stacked = pltpu.VMEM((S // SB_BLOCK, N_HEADS * SB_BLOCK, GW), BF16)
    return pl.pallas_call(
        _sb_kernel,
        grid=(B, S // SB_BLOCK),
        in_specs=[col(0), col(1), col(2), pl.BlockSpec((1, GW), lambda b, i: (0, 0))],
        out_specs=col(0),
        out_shape=jax.ShapeDtypeStruct((B, S, GW), F32),
        scratch_shapes=[stacked, stacked],
        compiler_params=pltpu.CompilerParams(dimension_semantics=("parallel", "arbitrary"),
                                             vmem_limit_bytes=VMEM_LIMIT),
        name="stick_breaking",
    )(p, p, p, g[None])


ML_HALO = 8
ML_GROUP = N_HEADS
ML_BATCH = 8


def _dot_nt_exact_by_f32(a_exact, b):
    return _dot_nt(jnp.concatenate([a_exact] * 3, axis=1), jnp.concatenate(_split3_bf16(b), axis=1))


def _mlstm_kernel(p_ref, cw_ref, cb_ref, gb_ref, g_ref, o_ref, ext_ref, ct_ref, n_ref, m_ref):
    @pl.when(pl.program_id(1) == 0)
    def _():
        ext_ref[...] = jnp.zeros_like(ext_ref)
        ct_ref[...] = jnp.zeros_like(ct_ref)
        n_ref[...] = jnp.zeros_like(n_ref)
        m_ref[...] = jnp.zeros_like(m_ref)

    masks = _mlstm_masks()
    chains = [_mlstm_chunk(masks, p_ref.at[n], cw_ref, cb_ref, gb_ref, g_ref, o_ref.at[n], ext_ref.at[n],
                           ct_ref.at[n], n_ref.at[n], m_ref.at[n]) for n in range(ML_BATCH)]
    for _ in itertools.zip_longest(*chains):
        pass


def _mlstm_masks():
    L, GW, H, SG = ML_CHUNK, GROUP_W, N_HEADS, ML_GROUP * ML_CHUNK
    gi = lax.broadcasted_iota(jnp.int32, (LANE, 2 * GW), 0)
    gj = lax.broadcasted_iota(jnp.int32, (LANE, 2 * GW), 1)
    expand = jnp.where(gi == (gj % GW) // HEAD_DIM + H * (gj // GW), 1.0, 0.0).astype(BF16)
    ti = lax.broadcasted_iota(jnp.int32, (L, L), 0)
    tj = lax.broadcasted_iota(jnp.int32, (L, L), 1)
    tri = jnp.where(tj <= ti, 1.0, 0.0).astype(BF16)
    ri = lax.broadcasted_iota(jnp.int32, (SG, SG), 0)
    ci = lax.broadcasted_iota(jnp.int32, (SG, SG), 1)
    same_head = (ri // L) == (ci // HEAD_DIM)
    first_lane = ci == (ri // L) * HEAD_DIM
    sel_first = jnp.where(first_lane, 1.0, 0.0).astype(BF16)
    causal = lax.broadcasted_iota(jnp.int32, (SG, L), 1) <= lax.broadcasted_iota(jnp.int32, (SG, L), 0) % L
    hi_ = lax.broadcasted_iota(jnp.int32, (GW, GW), 0) // HEAD_DIM
    hj_ = lax.broadcasted_iota(jnp.int32, (GW, GW), 1) // HEAD_DIM
    ones_bd = jnp.where(hi_ == hj_, 1.0, 0.0).astype(BF16)
    return expand, tri, same_head, first_lane, sel_first, causal, ones_bd


def _mlstm_chunk(masks, p_ref, cw_ref, cb_ref, gb_ref, g_ref, o_ref, ext_ref, ct_ref, n_ref, m_ref):
    L, GW, H = ML_CHUNK, GROUP_W, N_HEADS
    x = p_ref[...]
    ext_ref[ML_HALO:, :] = x[:, :2 * GW]
    conv = cb_ref[...]
    for j in range(ML_CONV):
        conv = conv + cw_ref[j:j + 1, :] * ext_ref[pl.ds(ML_HALO - (ML_CONV - 1) + j, L), :]
    ext_ref[:ML_HALO, :] = x[L - ML_HALO:, :2 * GW]
    qk = conv * jax.nn.sigmoid(conv)
    q, k = qk[:, :GW], qk[:, GW:] * HEAD_DIM ** -0.5
    v, o = x[:, 2 * GW:3 * GW], x[:, 3 * GW:4 * GW]

    gates = x[:, 4 * GW:]
    expand, tri, same_head, first_lane, sel_first, causal, ones_bd = masks
    graw = _dot_f32_by_exact(gates, expand) + gb_ref[...]
    yield
    capped = GATE_CAP * jnp.tanh(graw * (1.0 / GATE_CAP))
    log_i = capped[:, :GW]
    cf = capped[:, GW:]
    log_f = jnp.minimum(cf, 0.0) - jnp.log(1.0 + jnp.exp(-jnp.abs(cf)))

    bf = _dot_exact_by_f32(tri, log_f)
    yield
    b_last = bf[L - 1:L, :]
    m_row, n_row = m_ref[...], n_ref[...]
    dec = b_last - bf + log_i
    m_new = jnp.maximum(b_last + m_row, jnp.max(dec, axis=0, keepdims=True))
    kw = k * jnp.exp(dec - m_new)
    s_old = jnp.exp(b_last + m_row - m_new)
    n_ref[...] = s_old * n_row + jnp.sum(kw, axis=0, keepdims=True)
    m_ref[...] = m_new
    g_in = bf + m_row
    li_b = log_i - bf

    SG = ML_GROUP * L
    stack = lambda t: jnp.concatenate([t] * ML_GROUP, axis=0)
    pick = lambda t: jnp.sum(jnp.where(first_lane, t, 0.0), axis=1, keepdims=True)
    hs = []
    for g in range(H // ML_GROUP):
        cols = slice(g * SG, (g + 1) * SG)
        ct = ct_ref[g]
        qs = jnp.where(same_head, stack(q[:, cols]), 0.0)
        qs_b = qs.astype(BF16)
        v_b = v[:, cols].astype(BF16)
        b_col = pick(stack(bf[:, cols]))
        g_col = pick(stack(g_in[:, cols]))
        row_part = _dot_nt_exact_by_f32(sel_first, li_b[:, cols])
        qk = _dot_nt(qs_b, k[:, cols].astype(BF16))
        inter = _dot(qs_b, ct.astype(BF16))
        carried = _dot_tn(kw[:, cols].astype(BF16), v_b)
        yield
        dmat = jnp.where(causal, b_col + row_part, -jnp.inf)
        m_t = jnp.maximum(g_col, jnp.max(dmat, axis=1, keepdims=True))
        s_inter = jnp.exp(g_col - m_t)
        sqk = qk * jnp.exp(dmat - m_t)
        intra = _dot(sqk.astype(BF16), v_b)
        yield
        num = s_inter * inter + jnp.where(same_head, intra, 0.0)
        den = (s_inter * jnp.sum(qs * n_row[:, cols], axis=1, keepdims=True)
               + jnp.sum(sqk, axis=1, keepdims=True))
        hst = num / jnp.maximum(jnp.abs(den), jnp.exp(-m_t))
        hs.append(sum(hst[n * L:(n + 1) * L, :] for n in range(ML_GROUP)))
        ct_ref[g] = s_old[:, cols] * ct + jnp.where(same_head, carried, 0.0)
    h = jnp.concatenate(hs, axis=1)

    ms = _dot_f32_by_exact(h * h, ones_bd) * (1.0 / HEAD_DIM)
    yield
    o_ref[...] = jax.nn.sigmoid(o) * (h * lax.rsqrt(ms + NORM_EPS) * g_ref[...])


def _mlstm_mix(p, conv_w, conv_b, ig_b, fg_b, norm_g):
    B, S, _ = p.shape
    GW = GROUP_W
    assert S % ML_CHUNK == 0 and ML_CONV - 1 <= ML_HALO <= ML_CHUNK and B % ML_BATCH == 0
    gate_b = jnp.concatenate([jnp.repeat(ig_b, HEAD_DIM), jnp.repeat(fg_b, HEAD_DIM)])[None]
    full = lambda a: pl.BlockSpec(a.shape, lambda b, c: (0,) * a.ndim)
    cb2, g2 = conv_b[None], norm_g[None]
    return pl.pallas_call(
        _mlstm_kernel,
        grid=(B // ML_BATCH, S // ML_CHUNK),
        in_specs=[pl.BlockSpec((ML_BATCH, ML_CHUNK, C_PAD), lambda b, c: (b, c, 0)),
                  full(conv_w), full(cb2), full(gate_b), full(g2)],
        out_specs=pl.BlockSpec((ML_BATCH, ML_CHUNK, GW), lambda b, c: (b, c, 0)),
        out_shape=jax.ShapeDtypeStruct((B, S, GW), F32),
        scratch_shapes=[pltpu.VMEM((ML_BATCH, ML_HALO + ML_CHUNK, 2 * GW), F32),
                        pltpu.VMEM((ML_BATCH, N_HEADS // ML_GROUP, ML_GROUP * ML_CHUNK, ML_GROUP * HEAD_DIM), F32),
                        pltpu.VMEM((ML_BATCH, 1, GW), F32), pltpu.VMEM((ML_BATCH, 1, GW), F32)],
        compiler_params=pltpu.CompilerParams(dimension_semantics=("parallel", "arbitrary"),
                                             vmem_limit_bytes=VMEM_LIMIT),
        name="mlstm",
    )(p, conv_w, cb2, gate_b, g2)


def kernel(x, c, ada_w, ada_b, norm1_g, norm2_g, w_in, rk_mu, rk_w0, rk_w2, rk_a0, rk_a2, rk_g2, rk_kk, rk_ka, rk_rk, rk_ln_g, rk_ln_b, sb_norm_g, ml_conv_w, ml_conv_b, ml_ig_b, ml_fg_b, ml_norm_g, ds_qn_g, ds_kn_g, ds_out_g, w_out, moe_wg, moe_bg, moe_we, moe_be, moe_w1, moe_w3, moe_w2):
    B, S, D = x.shape
    depth = ada_w.shape[0]
    c_act = jax.nn.silu(c)
    for l in range(depth):
        mod = (c_act @ ada_w[l] + ada_b[l])[:, None, :]
        sh1, sc1, gt1, sh2, sc2, gt2 = jnp.split(mod, 6, axis=-1)

        pA, pB, pC, pD = _in_proj(x, sc1, sh1, norm1_g[l][None], _pad_w_in(w_in[l]))

        yA = _rwkv7_time_mix(pA, rk_mu[l], rk_w0[l], rk_w2[l], rk_a0[l], rk_a2[l], rk_g2[l],
                             rk_kk[l], rk_ka[l], rk_rk[l], rk_ln_g[l], rk_ln_b[l])

        yB = _stick_breaking_norm(pB, sb_norm_g[l])

        yC = _mlstm_mix(pC, ml_conv_w[l], ml_conv_b[l], ml_ig_b[l], ml_fg_b[l], ml_norm_g[l])

        yD = _dsa_attn_norm(*_dsa_prep(pD, ds_qn_g[l], ds_kn_g[l]), ds_out_g[l])

        router = jnp.pad(jnp.concatenate([moe_wg[l], moe_we[l]], 1),
                         ((0, 0), (0, ROUTER_PAD - N_GROUPS - N_EXPERTS)))
        r_hi, r_lo = _split_bf16(router)
        r_b = jnp.pad(jnp.concatenate([moe_bg[l], moe_be[l]]), (0, ROUTER_PAD - N_GROUPS - N_EXPERTS))[None]
        x1, h2, route = _out_proj((yA, yB, yC, yD), x, gt1, sc2, sh2, norm2_g[l][None],
                                  w_out[l].astype(BF16), r_hi, r_lo, r_b)

        y1, y2 = _hier_moe(h2.reshape(B * S, D), route.reshape(B * S, ROUTER_PAD), moe_w1, moe_w3, moe_w2, l)
        x = x1 + gt2 * (y1.astype(F32) + y2.astype(F32)).reshape(B, S, D)
    return x
```

```python
import functools
import itertools

import jax
import jax.numpy as jnp
import numpy as np
from jax import lax
from jax.experimental import pallas as pl
from jax.experimental.pallas import tpu as pltpu

F32 = jnp.float32
BF16 = jnp.bfloat16

D_MODEL = 1024
N_MIXERS = 4
GROUP_W = D_MODEL // N_MIXERS
HEAD_DIM = 64
N_HEADS = GROUP_W // HEAD_DIM
NORM_EPS = 1e-6
RWKV_W_RANK = 32
RWKV_A_RANK = 32
RWKV_G_RANK = 64
RWKV_GN_EPS = 64e-5
SB_BLOCK = 128
ML_CHUNK = 64
ML_CONV = 4
GATE_CAP = 15.0
IDX_HEADS = 4
IDX_DIM = 32
TOPK_MAX = 256
ROPE_THETA = 10000.0
N_GROUPS = 4
EXP_PER_GROUP = 8
N_EXPERTS = N_GROUPS * EXP_PER_GROUP
EXPERT_FF = D_MODEL // 2
TOP_IN_GROUP = 2

A_SIZES = (GROUP_W, GROUP_W, GROUP_W, RWKV_W_RANK, RWKV_A_RANK, RWKV_G_RANK)
B_SIZES = (GROUP_W, GROUP_W, GROUP_W)
C_SIZES = (GROUP_W, GROUP_W, GROUP_W, GROUP_W, N_HEADS, N_HEADS)
D_SIZES = (GROUP_W, HEAD_DIM, HEAD_DIM, IDX_HEADS * IDX_DIM, IDX_DIM, IDX_HEADS)
A_COLS = sum(A_SIZES)
B_COLS = sum(B_SIZES)
C_COLS = sum(C_SIZES)
D_COLS = sum(D_SIZES)

LANE = 128
A_PAD = 896
B_PAD = 768
C_PAD = 1152
D_PAD = 640
P_PAD = A_PAD + B_PAD + C_PAD + D_PAD
ROUTER_PAD = LANE

IN_ROWS = 512
OUT_ROWS = 512
MOE_ROWS = 512
VMEM_LIMIT = 48 * 1024 * 1024


def _split_cols(t, sizes):
    return jnp.split(t, [int(i) for i in np.cumsum(sizes)[:-1]], axis=-1)


def _in_proj_kernel(x_ref, sc_ref, sh_ref, g_ref, w_ref, oa_ref, ob_ref, oc_ref, od_ref):
    x = x_ref[0]
    y = x * lax.rsqrt(jnp.mean(x * x, -1, keepdims=True) + NORM_EPS) * g_ref[...]
    h = y * (1.0 + sc_ref[0]) + sh_ref[0]
    p = jnp.dot(h.astype(BF16), w_ref[...], preferred_element_type=F32)
    oa_ref[0] = p[:, :A_PAD]
    ob_ref[0] = p[:, A_PAD:A_PAD + B_PAD]
    oc_ref[0] = p[:, A_PAD + B_PAD:A_PAD + B_PAD + C_PAD]
    od_ref[0] = p[:, A_PAD + B_PAD + C_PAD:]


def _in_proj(x, sc, sh, g, w_pad):
    B, S, D = x.shape
    row = lambda w: pl.BlockSpec((1, IN_ROWS, w), lambda b, i: (b, i, 0))
    vec = pl.BlockSpec((1, 1, D), lambda b, i: (b, 0, 0))
    return pl.pallas_call(
        _in_proj_kernel,
        grid=(B, S // IN_ROWS),
        in_specs=[row(D), vec, vec, pl.BlockSpec((1, D), lambda b, i: (0, 0)),
                  pl.BlockSpec((D, P_PAD), lambda b, i: (0, 0))],
        out_specs=[row(A_PAD), row(B_PAD), row(C_PAD), row(D_PAD)],
        out_shape=[jax.ShapeDtypeStruct((B, S, w), F32) for w in (A_PAD, B_PAD, C_PAD, D_PAD)],
        compiler_params=pltpu.CompilerParams(dimension_semantics=("parallel", "parallel"),
                                             vmem_limit_bytes=VMEM_LIMIT),
        name="in_proj",
    )(x, sc, sh, g, w_pad)


def _pad_w_in(w):
    wa, wb, wc, wd = _split_cols(w, (A_COLS, B_COLS, C_COLS, D_COLS))
    padc = lambda t, n: jnp.pad(t, ((0, 0), (0, n - t.shape[1])))
    return jnp.concatenate([padc(wa, A_PAD), padc(wb, B_PAD), padc(wc, C_PAD), padc(wd, D_PAD)], 1).astype(BF16)


def _split_bf16(t):
    hi = t.astype(BF16)
    lo = (t - hi.astype(F32)).astype(BF16)
    return hi, lo


def _out_proj_kernel(ya_ref, yb_ref, yc_ref, yd_ref, x_ref, gt_ref, sc_ref, sh_ref, g_ref, w_ref,
                     rhi_ref, rlo_ref, rb_ref, x1_ref, h2_ref, route_ref):
    acc = jnp.zeros(x_ref.shape[1:], F32)
    for n, y_ref in enumerate((ya_ref, yb_ref, yc_ref, yd_ref)):
        acc += jnp.dot(y_ref[0].astype(BF16), w_ref[n * GROUP_W:(n + 1) * GROUP_W, :],
                       preferred_element_type=F32)
    x1 = x_ref[0] + gt_ref[0] * acc
    x1_ref[0] = x1
    y = x1 * lax.rsqrt(jnp.mean(x1 * x1, -1, keepdims=True) + NORM_EPS) * g_ref[...]
    h = y * (1.0 + sc_ref[0]) + sh_ref[0]
    hi, lo = _split_bf16(h)
    h2_ref[0] = hi
    lg = (jnp.dot(hi, rhi_ref[...], preferred_element_type=F32)
          + jnp.dot(lo, rhi_ref[...], preferred_element_type=F32)
          + jnp.dot(hi, rlo_ref[...], preferred_element_type=F32)) + rb_ref[...]
    route_ref[0] = _route(lg)


def _route(lg):
    lane = lax.broadcasted_iota(jnp.int32, lg.shape, 1)
    neg = -jnp.inf
    first = lambda hit: jnp.min(jnp.where(hit, lane, ROUTER_PAD), axis=1, keepdims=True)
    is_grp = lane < N_GROUPS
    grp = jnp.where(is_grp, lg, neg)
    g_max = jnp.max(grp, axis=1, keepdims=True)
    g_p = 1.0 / jnp.sum(jnp.where(is_grp, jnp.exp(grp - g_max), 0.0), axis=1, keepdims=True)
    g_idx = first(grp == g_max)
    e_lane = lane - N_GROUPS
    in_group = (e_lane >= 0) & (e_lane < N_EXPERTS) & (e_lane // EXP_PER_GROUP == g_idx)
    e_log = jnp.where(in_group, lg, neg)
    e1_max = jnp.max(e_log, axis=1, keepdims=True)
    e1_lane = first(e_log == e1_max)
    e_log2 = jnp.where(lane == e1_lane, neg, e_log)
    e2_max = jnp.max(e_log2, axis=1, keepdims=True)
    e2_lane = first(e_log2 == e2_max)
    ratio = jnp.exp(e2_max - e1_max)
    gate1 = g_p / (1.0 + ratio)
    gate2 = gate1 * ratio
    out = jnp.where(lane == 0, (e1_lane - N_GROUPS).astype(F32), 0.0)
    out = jnp.where(lane == 1, (e2_lane - N_GROUPS).astype(F32), out)
    out = jnp.where(lane == 2, gate1, out)
    return jnp.where(lane == 3, gate2, out)


def _out_proj(ys, x, gt, sc, sh, g, w_out, r_hi, r_lo, r_b):
    B, S, D = x.shape
    row = lambda w: pl.BlockSpec((1, OUT_ROWS, w), lambda b, i: (b, i, 0))
    vec = pl.BlockSpec((1, 1, D), lambda b, i: (b, 0, 0))
    full = lambda a: pl.BlockSpec(a.shape, lambda b, i: (0,) * a.ndim)
    return pl.pallas_call(
        _out_proj_kernel,
        grid=(B, S // OUT_ROWS),
        in_specs=[row(GROUP_W)] * 4 + [row(D), vec, vec, vec, full(g), full(w_out), full(r_hi), full(r_lo),
                                        full(r_b)],
        out_specs=[row(D), row(D), row(ROUTER_PAD)],
        out_shape=[jax.ShapeDtypeStruct((B, S, D), F32), jax.ShapeDtypeStruct((B, S, D), BF16),
                   jax.ShapeDtypeStruct((B, S, ROUTER_PAD), F32)],
        compiler_params=pltpu.CompilerParams(dimension_semantics=("parallel", "parallel"),
                                             vmem_limit_bytes=VMEM_LIMIT),
        name="out_proj",
    )(*ys, x, gt, sc, sh, g, w_out, r_hi, r_lo, r_b)


def _moe_ffn_kernel(blk_e_ref, x_ref, wt_ref, w1_ref, w3_ref, w2_ref, o_ref, w1b_ref, w3b_ref, w2b_ref):
    i = pl.program_id(0)
    changed = jnp.logical_or(i == 0, blk_e_ref[i] != blk_e_ref[jnp.maximum(i - 1, 0)])

    @pl.when(changed)
    def _():
        w1b_ref[...] = w1_ref[0, 0].astype(BF16)
        w3b_ref[...] = w3_ref[0, 0].astype(BF16)
        w2b_ref[...] = w2_ref[0, 0].astype(BF16)

    xb = x_ref[...]
    a = jnp.dot(xb, w1b_ref[...], preferred_element_type=F32)
    b = jnp.dot(xb, w3b_ref[...], preferred_element_type=F32)
    hmid = (a * jax.nn.sigmoid(a) * b).astype(BF16)
    y = jnp.dot(hmid, w2b_ref[...], preferred_element_type=F32)
    o_ref[...] = (y * wt_ref[...]).astype(o_ref.dtype)


def _moe_ffn(blk_e, xs, wt, w1, w3, w2, layer):
    n_slots, D = xs.shape
    n_blocks = n_slots // MOE_ROWS
    FF = w1.shape[-1]
    return pl.pallas_call(
        _moe_ffn_kernel,
        grid_spec=pltpu.PrefetchScalarGridSpec(
            num_scalar_prefetch=1,
            grid=(n_blocks,),
            in_specs=[pl.BlockSpec((MOE_ROWS, D), lambda i, e: (i, 0)),
                      pl.BlockSpec((MOE_ROWS, 1), lambda i, e: (i, 0)),
                      pl.BlockSpec((1, 1, D, FF), lambda i, e: (layer, e[i], 0, 0)),
                      pl.BlockSpec((1, 1, D, FF), lambda i, e: (layer, e[i], 0, 0)),
                      pl.BlockSpec((1, 1, FF, D), lambda i, e: (layer, e[i], 0, 0))],
            out_specs=pl.BlockSpec((MOE_ROWS, D), lambda i, e: (i, 0)),
            scratch_shapes=[pltpu.VMEM((D, FF), BF16), pltpu.VMEM((D, FF), BF16), pltpu.VMEM((FF, D), BF16)],
        ),
        out_shape=jax.ShapeDtypeStruct((n_slots, D), BF16),
        compiler_params=pltpu.CompilerParams(dimension_semantics=("arbitrary",),
                                             vmem_limit_bytes=VMEM_LIMIT),
        name="moe_ffn",
    )(blk_e, xs, wt, w1, w3, w2)


def _hier_moe(h2, route, w1, w3, w2, layer):
    N, D = h2.shape
    expert = route[:, :TOP_IN_GROUP].astype(jnp.int32)
    gate = route[:, TOP_IN_GROUP:2 * TOP_IN_GROUP]
    n_asg = N * TOP_IN_GROUP
    n_blocks = n_asg // MOE_ROWS + N_EXPERTS
    n_slots = n_blocks * MOE_ROWS
    n_fill = n_slots - n_asg
    flat_e = expert.reshape(n_asg // LANE, LANE)
    counts = jnp.sum(flat_e[None] == jnp.arange(N_EXPERTS)[:, None, None], axis=(1, 2)).astype(jnp.int32)
    pad_counts = (counts + MOE_ROWS - 1) // MOE_ROWS * MOE_ROWS
    pad_end = jnp.cumsum(pad_counts)
    blk_start = jnp.arange(n_blocks) * MOE_ROWS
    blk_e = jnp.minimum(jnp.sum(pad_end[None, :] <= blk_start[:, None], 1), N_EXPERTS - 1).astype(jnp.int32)
    fill_end = jnp.cumsum(pad_counts - counts)
    fill_id = jnp.arange(n_fill).reshape(n_fill // LANE, LANE)
    fill_e = jnp.zeros_like(fill_id)
    for e in range(N_EXPERTS):
        fill_e = fill_e + (fill_id >= fill_end[e])
    keys = jnp.concatenate([flat_e.reshape(n_asg) * 2, fill_e.reshape(n_fill) * 2 + 1])
    toks = jnp.concatenate([jnp.arange(n_asg, dtype=jnp.int32) // TOP_IN_GROUP, jnp.zeros((n_fill,), jnp.int32)])
    wts = jnp.concatenate([gate.reshape(n_asg), jnp.zeros((n_fill,), F32)])
    _, slot_tok, slot_w, slot_src = lax.sort((keys, toks, wts, jnp.arange(n_slots, dtype=jnp.int32)), num_keys=1)
    _, entry_slot = lax.sort((slot_src, jnp.arange(n_slots, dtype=jnp.int32)), num_keys=1)
    asg_slot = entry_slot[:n_asg].reshape(N, TOP_IN_GROUP)
    yb = _moe_ffn(blk_e, h2[slot_tok], slot_w[:, None], w1, w3, w2, layer)
    return yb[asg_slot[:, 0]], yb[asg_slot[:, 1]]


INT_MIN = -2 ** 31
DSA_KEY_STEP = 256
DSA_QUERIES = 128


def _float_order_key(x):
    bits = pltpu.bitcast(x, jnp.int32)
    bits = jnp.where(x == 0.0, 0, bits)
    return bits ^ ((bits >> 31) & 0x7FFFFFFF)


COL_PART = 64


def _col_reduce(x, reduce):
    part = reduce(x.reshape(x.shape[0] // COL_PART, COL_PART, x.shape[1]), axis=0)
    return reduce(part, axis=0, keepdims=True)


def _col_count(mask):
    return _col_reduce(jnp.where(mask, 1.0, 0.0), jnp.sum)


def _head_block_diag(t, group):
    n_heads = t.shape[0] // group
    row_h = lax.broadcasted_iota(jnp.int32, t.shape, 0) // group
    return jnp.concatenate([jnp.where(row_h == h, t, 0.0) for h in range(n_heads)], axis=1)


def _dsa_select(qi_ref, kihi_ref, kilo_ref, wi_ref, q0, *, kl, n_sel):
    Q = DSA_QUERIES
    kidx = lax.broadcasted_iota(jnp.int32, (kl, Q), 0)
    qpos = q0 + lax.broadcasted_iota(jnp.int32, (kl, Q), 1)
    adm = kidx <= qpos
    if kl <= n_sel:
        return adm
    w_hi, w_lo = _split_bf16(_head_block_diag(jnp.transpose(qi_ref[0]), IDX_DIM))
    k_hi, k_lo = kihi_ref[0, :kl, :], kilo_ref[0, :kl, :]
    sc = _dot(jnp.concatenate([k_hi, k_lo, k_hi], axis=1),
              jnp.concatenate([w_hi, w_hi, w_lo], axis=0))
    wit = jnp.transpose(wi_ref[0])
    score = sum(wit[h:h + 1, :] * jnp.maximum(sc[:, h * Q:(h + 1) * Q], 0.0) for h in range(IDX_HEADS))
    key =_float_order_key(jnp.where(adm, score, -jnp.inf))

    def value_bit(it, tau):
        cand = tau | jnp.left_shift(jnp.int32(1), 31 - it)
        return jnp.where(_col_count(key >= (cand ^ INT_MIN)) >= n_sel, cand, tau)

    tau = lax.fori_loop(0, 32, value_bit, jnp.zeros((1, Q), jnp.int32)) ^ INT_MIN
    gt = key > tau
    eq = (key == tau) & adm
    need = n_sel - _col_count(gt)
    n_eq = _col_count(eq)

    index_width = kl.bit_length()

    def index_bits():
        def index_bit(it, bound):
            cand = bound | jnp.left_shift(jnp.int32(1), index_width - 1 - it)
            return jnp.where(_col_count(eq & (kidx < cand)) <= need, cand, bound)
        return lax.fori_loop(0, index_width, index_bit, jnp.zeros((1, Q), jnp.int32))

    bound = lax.cond(jnp.max(n_eq - need) > 0.0, index_bits, lambda: jnp.full((1, Q), kl, jnp.int32))
    return gt | (eq & (kidx < bound))


def _dsa_block_t(qd_ref, kd4_ref, vdwt_ref, qi_ref, kihi_ref, kilo_ref, wi_ref, g_ref, o_ref, *, kl, n_sel):
    Q = DSA_QUERIES
    sel = _dsa_select(qi_ref, kihi_ref, kilo_ref, wi_ref, pl.program_id(1) * Q, kl=kl, n_sel=n_sel)

    w_att = _head_block_diag(jnp.transpose(qd_ref[0]), HEAD_DIM).astype(BF16)
    lg = _dot(kd4_ref[0, :kl, :], w_att) * HEAD_DIM ** -0.5
    lg = jnp.where(jnp.concatenate([sel] * N_HEADS, axis=1), lg, -jnp.inf)
    p = jnp.exp(lg - _col_reduce(lg, jnp.max))
    out_t = _dot(vdwt_ref[0, :, :kl], p.astype(BF16)) / _col_reduce(p, jnp.sum)
    row_h = lax.broadcasted_iota(jnp.int32, (GROUP_W, Q), 0) // HEAD_DIM
    nat_t = sum(jnp.where(row_h == h, out_t[:, h * Q:(h + 1) * Q], 0.0) for h in range(N_HEADS))
    r = jnp.transpose(nat_t)
    hi_ = lax.broadcasted_iota(jnp.int32, (GROUP_W, GROUP_W), 0) // HEAD_DIM
    hj_ = lax.broadcasted_iota(jnp.int32, (GROUP_W, GROUP_W), 1) // HEAD_DIM
    ms = _dot_f32_by_exact(r * r, jnp.where(hi_ == hj_, 1.0, 0.0).astype(BF16)) * (1.0 / HEAD_DIM)
    o_ref[0] = r * lax.rsqrt(ms + NORM_EPS) * g_ref[...]


def _dsa_kernel(qd_ref, kd4_ref, vdwt_ref, qi_ref, kihi_ref, kilo_ref, wi_ref, g_ref, o_ref, *, kls, n_sel):
    blocks_per_step = DSA_KEY_STEP // DSA_QUERIES
    for j, kl in enumerate(kls):
        @pl.when(pl.program_id(1) // blocks_per_step == j)
        def _():
            _dsa_block_t(qd_ref, kd4_ref, vdwt_ref, qi_ref, kihi_ref, kilo_ref, wi_ref, g_ref, o_ref,
                         kl=kl, n_sel=n_sel)


DSA_PREP_ROWS = 256
D_Q, D_KV, D_QI, D_KW = 0, GROUP_W, GROUP_W + LANE, GROUP_W + 2 * LANE


def _swap_halves(x, half):
    n = x.shape[1]
    lane = lax.broadcasted_iota(jnp.int32, x.shape, 1)
    return jnp.where(lane % (2 * half) < half, pltpu.roll(x, n - half, axis=1), pltpu.roll(x, half, axis=1))


def _dsa_prep_kernel(p_ref, cq_ref, sq_ref, ci_ref, si_ref, gq_ref, gk_ref,
                     qd_ref, kd4_ref, vdwt_ref, qi_ref, kihi_ref, kilo_ref, wi_ref):
    GW = GROUP_W
    x = p_ref[0]
    hi_ = lax.broadcasted_iota(jnp.int32, (GW, GW), 0) // HEAD_DIM
    hj_ = lax.broadcasted_iota(jnp.int32, (GW, GW), 1) // HEAD_DIM
    ones_bd = jnp.where(hi_ == hj_, 1.0, 0.0).astype(BF16)
    q = x[:, D_Q:D_Q + GW]
    q = q * lax.rsqrt(_dot_f32_by_exact(q * q, ones_bd) * (1.0 / HEAD_DIM) + NORM_EPS) * gq_ref[...]
    qd_ref[0] = q * cq_ref[...] + _swap_halves(q, HEAD_DIM // 2) * sq_ref[...]
    kv = x[:, D_KV:D_KV + LANE]
    lane = lax.broadcasted_iota(jnp.int32, kv.shape, 1)
    is_k = lane < HEAD_DIM
    ms = jnp.sum(jnp.where(is_k, kv * kv, 0.0), axis=1, keepdims=True) * (1.0 / HEAD_DIM)
    kn = kv * lax.rsqrt(ms + NORM_EPS) * gk_ref[...]
    kr = kn * cq_ref[:, :LANE] + _swap_halves(kn, HEAD_DIM // 2) * sq_ref[:, :LANE]
    k2 = jnp.where(is_k, kr, pltpu.roll(kr, HEAD_DIM, axis=1))
    kd4_ref[0] = jnp.concatenate([k2] * (GW // LANE), axis=1).astype(BF16)
    v2 = jnp.where(is_k, pltpu.roll(kv, HEAD_DIM, axis=1), kv)
    v2t = jnp.transpose(v2)
    vdwt_ref[0] = jnp.concatenate([v2t] * (GW // LANE), axis=0).astype(BF16)
    qi = x[:, D_QI:D_QI + LANE]
    qi_ref[0] = qi * ci_ref[...] + _swap_halves(qi, IDX_DIM // 2) * si_ref[...]
    kw = x[:, D_KW:D_KW + LANE]
    kir = kw * ci_ref[...] + _swap_halves(kw, IDX_DIM // 2) * si_ref[...]
    ki1 = jnp.where(lane < IDX_DIM, kir, 0.0)
    ki2 = ki1 + pltpu.roll(ki1, IDX_DIM, axis=1)
    ki4 = ki2 + pltpu.roll(ki2, 2 * IDX_DIM, axis=1)
    kihi_ref[0], kilo_ref[0] = _split_bf16(ki4)
    wi_ref[0] = pltpu.roll(kw, LANE - IDX_DIM, axis=1) * (IDX_HEADS ** -0.5 * IDX_DIM ** -0.5)


def _rope_tables(S, dim, width):
    half = dim // 2
    inv = ROPE_THETA ** (-jnp.arange(half, dtype=F32) / half)
    ang = jnp.arange(S, dtype=F32)[:, None] * inv[None, :]
    cos = jnp.tile(jnp.cos(ang), (1, width // half))
    sin = jnp.tile(jnp.concatenate([-jnp.sin(ang), jnp.sin(ang)], axis=1), (1, width // dim))
    return cos, sin


def _dsa_prep(p, qn_g, kn_g):
    B, S, _ = p.shape
    GW, R = GROUP_W, DSA_PREP_ROWS
    cq, sq = _rope_tables(S, HEAD_DIM, GW)
    ci, si = _rope_tables(S, IDX_DIM, LANE)
    gq = jnp.tile(qn_g, N_HEADS)[None]
    gk = jnp.pad(kn_g, (0, LANE - HEAD_DIM))[None]
    rows = lambda w: pl.BlockSpec((1, R, w), lambda b, i: (b, i, 0))
    tab = lambda w: pl.BlockSpec((R, w), lambda b, i: (i, 0))
    cols = lambda r: pl.BlockSpec((1, r, R), lambda b, i: (b, 0, i))
    vec = lambda w: pl.BlockSpec((1, w), lambda b, i: (0, 0))
    return pl.pallas_call(
        _dsa_prep_kernel,
        grid=(B, S // R),
        in_specs=[rows(D_PAD), tab(GW), tab(GW), tab(LANE), tab(LANE), vec(GW), vec(LANE)],
        out_specs=[rows(GW), rows(GW), cols(GW), rows(LANE), rows(LANE), rows(LANE), rows(LANE)],
        out_shape=[jax.ShapeDtypeStruct((B, S, GW), F32), jax.ShapeDtypeStruct((B, S, GW), BF16),
                   jax.ShapeDtypeStruct((B, GW, S), BF16), jax.ShapeDtypeStruct((B, S, LANE), F32),
                   jax.ShapeDtypeStruct((B, S, LANE), BF16), jax.ShapeDtypeStruct((B, S, LANE), BF16),
                   jax.ShapeDtypeStruct((B, S, LANE), F32)],
        compiler_params=pltpu.CompilerParams(dimension_semantics=("parallel", "parallel"),
                                             vmem_limit_bytes=VMEM_LIMIT),
        name="dsa_prep",
    )(p, cq, sq, ci, si, gq, gk)


def _dsa_attn_norm(qd, kd4, vdwt, qi, ki_hi, ki_lo, wi, g):
    B, S, _ = qd.shape
    n_sel = min(TOPK_MAX, S // 4)
    assert S % DSA_KEY_STEP == 0 and n_sel <= DSA_KEY_STEP
    kls = tuple(range(DSA_KEY_STEP, S + 1, DSA_KEY_STEP))
    blk = lambda w: pl.BlockSpec((1, DSA_QUERIES, w), lambda b, i: (b, i, 0))
    per_b = lambda r, c: pl.BlockSpec((1, r, c), lambda b, i: (b, 0, 0))
    return pl.pallas_call(
        functools.partial(_dsa_kernel, kls=kls, n_sel=n_sel),
        grid=(B, S // DSA_QUERIES),
        in_specs=[blk(GROUP_W), per_b(S, GROUP_W), per_b(GROUP_W, S), blk(IDX_HEADS * IDX_DIM),
                  per_b(S, LANE), per_b(S, LANE), blk(LANE), pl.BlockSpec((1, GROUP_W), lambda b, i: (0, 0))],
        out_specs=blk(GROUP_W),
        out_shape=jax.ShapeDtypeStruct((B, S, GROUP_W), F32),
        compiler_params=pltpu.CompilerParams(dimension_semantics=("parallel", "parallel"),
                                             vmem_limit_bytes=VMEM_LIMIT),
        name="dsa_attn",
    )(qd, kd4, vdwt, qi, ki_hi, ki_lo, wi, g[None])


RWKV_CHUNK = 64
RWKV_LOW = RWKV_W_RANK + RWKV_A_RANK + RWKV_G_RANK
RWKV_GROUP = N_HEADS
RWKV_BATCH = 8


def _dot(a, b):
    return jnp.dot(a, b, preferred_element_type=F32)


def _dot_nt(a, b):
    return lax.dot_general(a, b, (((1,), (1,)), ((), ())), preferred_element_type=F32)


def _dot_tn(a, b):
    return lax.dot_general(a, b, (((0,), (0,)), ((), ())), preferred_element_type=F32)


def _split3_bf16(t):
    p1 = t.astype(BF16)
    r1 = t - p1.astype(F32)
    p2 = r1.astype(BF16)
    p3 = (r1 - p2.astype(F32)).astype(BF16)
    return p1, p2, p3


def _dot_f32_by_exact(a, b_exact):
    m = a.shape[0]
    r = _dot(jnp.concatenate(_split3_bf16(a), axis=0), b_exact)
    return r[:m] + r[m:2 * m] + r[2 * m:]


def _dot_exact_by_f32(a_exact, b):
    n = b.shape[1]
    r = _dot(a_exact, jnp.concatenate(_split3_bf16(b), axis=1))
    return r[:, :n] + r[:, n:2 * n] + r[:, 2 * n:]


def _dot3(a, b_hi, b_lo):
    a_hi, a_lo = _split_bf16(a)
    return _dot(jnp.concatenate([a_hi, a_lo, a_hi], axis=1), jnp.concatenate([b_hi, b_hi, b_lo], axis=0))


def _softplus(z):
    return jnp.maximum(z, 0.0) + jnp.log(1.0 + jnp.exp(-jnp.abs(z)))


def _rwkv_kernel(p_ref, mu_ref, vec_ref, lhi_ref, llo_ref, o_ref, state_ref, prev_ref):
    @pl.when(pl.program_id(1) == 0)
    def _():
        state_ref[...] = jnp.zeros_like(state_ref)
        prev_ref[...] = jnp.zeros_like(prev_ref)

    masks = _rwkv_masks()
    chains = [_rwkv_chunk(masks, p_ref.at[n], mu_ref, vec_ref, lhi_ref, llo_ref, o_ref.at[n], state_ref.at[n],
                          prev_ref.at[n]) for n in range(RWKV_BATCH)]
    for _ in itertools.zip_longest(*chains):
        pass


def _rwkv_masks():
    L, GW, SG = RWKV_CHUNK, GROUP_W, RWKV_GROUP * RWKV_CHUNK
    ri = lax.broadcasted_iota(jnp.int32, (GW, GW), 0)
    ci = lax.broadcasted_iota(jnp.int32, (GW, GW), 1)
    ones_bd = jnp.where((ri // HEAD_DIM) == (ci // HEAD_DIM), 1.0, 0.0).astype(BF16)
    ti = lax.broadcasted_iota(jnp.int32, (L, L), 0)
    tj = lax.broadcasted_iota(jnp.int32, (L, L), 1)
    tri = jnp.where(tj <= ti, 1.0, 0.0).astype(BF16)
    gi = lax.broadcasted_iota(jnp.int32, (SG, SG), 0)
    gj = lax.broadcasted_iota(jnp.int32, (SG, SG), 1)
    g_same = (gi // L) == (gj // L)
    strict = g_same & ((gj % L) < (gi % L))
    incl = g_same & ((gj % L) <= (gi % L))
    diag = gi == gj
    return ones_bd, tri, g_same, strict, incl, diag, jnp.where(diag, 1.0, 0.0)


def _rwkv_chunk(masks, p_ref, mu_ref, vec_ref, lhi_ref, llo_ref, o_ref, state_ref, prev_ref):
    L, GW = RWKV_CHUNK, GROUP_W
    p = p_ref[...]
    row = lax.broadcasted_iota(jnp.int32, p.shape, 0)
    prev = jnp.where(row == 0, prev_ref[...], pltpu.roll(p, 1, axis=0))
    prev_ref[...] = p[L - 1:L, :]
    ps = p + (prev - p) * mu_ref[...]
    r, k, v = ps[:, :GW], ps[:, GW:2 * GW], ps[:, 2 * GW:3 * GW]
    low = ps[:, 3 * GW:]
    lane_low = lax.broadcasted_iota(jnp.int32, low.shape, 1)
    low = jnp.where(lane_low < RWKV_W_RANK, jnp.tanh(low),
                    jnp.where(lane_low < RWKV_W_RANK + RWKV_A_RANK, low, jax.nn.sigmoid(low)))
    up = _dot3(low, lhi_ref[...], llo_ref[...])
    w0, a0, k_k, k_a = vec_ref[0:1, :], vec_ref[1:2, :], vec_ref[2:3, :], vec_ref[3:4, :]
    r_k, ln_g, ln_b = vec_ref[4:5, :], vec_ref[5:6, :], vec_ref[6:7, :]
    logw = -jnp.exp(-_softplus(-(w0 + up[:, :GW])) - 0.5)
    rate = jax.nn.sigmoid(a0 + up[:, GW:2 * GW])
    gate = up[:, 2 * GW:]

    ones_bd, tri, g_same, strict, incl, diag, eye = masks
    kk = k * k_k
    k = k * (1.0 + (rate - 1.0) * k_a)
    seg = _dot_f32_by_exact(jnp.concatenate([kk * kk, r * k * r_k], axis=0), ones_bd)
    kk = kk / jnp.maximum(jnp.sqrt(seg[:L]), 1e-12)

    lc = _dot_exact_by_f32(tri, logw)
    lc_last = lc[L - 1:L, :]
    dec_in = jnp.exp(lc)
    dec_out = jnp.exp(-lc)
    a_t = -kk * jnp.exp(lc - logw)
    b_t = kk * rate * dec_out
    k_t = k * dec_out
    r_t = r * dec_in
    to_end = jnp.exp(lc_last)

    SG = RWKV_GROUP * L
    stack = lambda t: jnp.concatenate([t] * RWKV_GROUP, axis=0)
    bd = lambda t: jnp.where(g_same, stack(t), 0.0).astype(BF16)
    n_doublings = RWKV_CHUNK.bit_length() - 2
    ys = []
    for g in range(N_HEADS // RWKV_GROUP):
        cols = slice(g * SG, (g + 1) * SG)
        a_bd, r_bd, v_bd = bd(a_t[:, cols]), bd(r_t[:, cols]), bd(v[:, cols])
        m = _dot_nt(jnp.concatenate([a_bd, r_bd], axis=0),
                    jnp.concatenate([stack(b_t[:, cols]), stack(k_t[:, cols])], axis=0).astype(BF16))
        yield
        m_ab = jnp.where(strict, m[:SG, :SG], 0.0)
        m_ak = jnp.where(strict, m[:SG, SG:], 0.0).astype(BF16)
        m_rb = jnp.where(incl, m[SG:, :SG], 0.0).astype(BF16)
        m_rk = jnp.where(incl, m[SG:, SG:], 0.0).astype(BF16)

        inv = eye + m_ab
        pw_b = m_ab.astype(BF16)
        sq = _dot(pw_b, pw_b)
        yield
        pw_b = sq.astype(BF16)
        for s in range(n_doublings - 1):
            both = _dot(jnp.concatenate([inv.astype(BF16), pw_b], axis=0), pw_b)
            yield
            inv = inv + both[:SG]
            pw_b = both[SG:].astype(BF16)
        last = _dot(inv.astype(BF16), pw_b)
        t0 = state_ref[g]
        t0_b = t0.astype(BF16)
        rhs = _dot(jnp.concatenate([a_bd, m_ak], axis=1), jnp.concatenate([t0_b, v_bd], axis=0))
        yield
        inv_b = (inv + last).astype(BF16)
        u = _dot(inv_b, rhs.astype(BF16))
        yield
        u = u.astype(BF16)
        y_bd = _dot(jnp.concatenate([r_bd, m_rb, m_rk], axis=1),
                    jnp.concatenate([t0_b, u, v_bd], axis=0))
        end_g = to_end[:, cols]
        to_end_col = jnp.sum(jnp.where(diag, jnp.broadcast_to(end_g, (SG, SG)), 0.0), axis=1, keepdims=True)
        carried = _dot_tn(jnp.concatenate([bd(b_t[:, cols] * end_g), bd(k_t[:, cols] * end_g)], axis=0),
                          jnp.concatenate([u, v_bd], axis=0))
        yield
        ys.append(sum(y_bd[h * L:(h + 1) * L, :] for h in range(RWKV_GROUP)))
        state_ref[g] = to_end_col * t0 + carried
    y = jnp.concatenate(ys, axis=1)

    inv_d = 1.0 / HEAD_DIM
    mean = _dot_f32_by_exact(y, ones_bd) * inv_d
    yield
    yc = y - mean
    var = _dot_f32_by_exact(yc * yc, ones_bd) * inv_d
    yield
    yn = yc * lax.rsqrt(var + RWKV_GN_EPS) * ln_g + ln_b
    o_ref[...] = (yn + seg[L:] * v) * gate


def _rwkv7_time_mix(p, mu, w0, w2, a0, a2, g2, k_k, k_a, r_k, ln_g, ln_b):
    B, S, _ = p.shape
    GW = GROUP_W
    assert S % RWKV_CHUNK == 0 and RWKV_CHUNK == HEAD_DIM and B % RWKV_BATCH == 0
    low_w = jnp.zeros((RWKV_LOW, 3 * GW), F32)
    low_w = low_w.at[:RWKV_W_RANK, :GW].set(w2)
    low_w = low_w.at[RWKV_W_RANK:RWKV_W_RANK + RWKV_A_RANK, GW:2 * GW].set(a2)
    low_w = low_w.at[RWKV_W_RANK + RWKV_A_RANK:, 2 * GW:].set(g2)
    l_hi, l_lo = _split_bf16(low_w)
    vecs = jnp.stack([w0, a0, k_k, k_a, r_k, ln_g, ln_b, jnp.zeros_like(w0)], 0)
    full = lambda a: pl.BlockSpec(a.shape, lambda b, c: (0,) * a.ndim)
    mu2 = mu[None]
    return pl.pallas_call(
        _rwkv_kernel,
        grid=(B // RWKV_BATCH, S // RWKV_CHUNK),
        in_specs=[pl.BlockSpec((RWKV_BATCH, RWKV_CHUNK, A_PAD), lambda b, c: (b, c, 0)),
                  full(mu2), full(vecs), full(l_hi), full(l_lo)],
        out_specs=pl.BlockSpec((RWKV_BATCH, RWKV_CHUNK, GW), lambda b, c: (b, c, 0)),
        out_shape=jax.ShapeDtypeStruct((B, S, GW), F32),
        scratch_shapes=[pltpu.VMEM((RWKV_BATCH, N_HEADS // RWKV_GROUP, RWKV_GROUP * RWKV_CHUNK,
                                    RWKV_GROUP * HEAD_DIM), F32),
                        pltpu.VMEM((RWKV_BATCH, 1, A_PAD), F32)],
        compiler_params=pltpu.CompilerParams(dimension_semantics=("parallel", "arbitrary"),
                                             vmem_limit_bytes=VMEM_LIMIT),
        name="rwkv7",
    )(p, mu2, vecs, l_hi, l_lo)


SB_UNROLL = 4


def _sb_kernel(q_ref, k_ref, v_ref, g_ref, o_ref, kbd_ref, vbd_ref):
    i = pl.program_id(1)
    T, GW, H = SB_BLOCK, GROUP_W, N_HEADS
    lane_h = lax.broadcasted_iota(jnp.int32, (T, GW), 1) // HEAD_DIM
    k_new, v_new = k_ref[0], v_ref[0]
    for h in range(H):
        kbd_ref[i, h * T:(h + 1) * T, :] = jnp.where(lane_h == h, k_new, 0.0).astype(BF16)
        vbd_ref[i, h * T:(h + 1) * T, :] = jnp.where(lane_h == h, v_new, 0.0).astype(BF16)

    q = (q_ref[0] * HEAD_DIM ** -0.5).astype(BF16)
    si = lax.broadcasted_iota(jnp.int32, (T, 2 * T), 0)
    sj = lax.broadcasted_iota(jnp.int32, (T, 2 * T), 1)
    later_and_all = jnp.where((si > sj) | (sj >= T), 1.0, 0.0).astype(BF16)
    qrow = lax.broadcasted_iota(jnp.int32, (T, H * T), 0)
    kcol = lax.broadcasted_iota(jnp.int32, (T, H * T), 1) % T
    causal = kcol < qrow

    def key_block_stages(j, box, diagonal):
        z = _dot_nt(q, kbd_ref[j])
        yield
        log1m = -(jnp.maximum(z, 0.0) + jnp.log(1.0 + jnp.exp(-jnp.abs(z))))
        log_sig = z + log1m
        log1m_in = (jnp.where(causal, log1m, 0.0) if diagonal else log1m).astype(BF16)
        sums = _dot(jnp.concatenate([log1m_in[:, h * T:(h + 1) * T] for h in range(H)], axis=0), later_and_all)
        yield
        suffix = jnp.concatenate([sums[h * T:(h + 1) * T, :T] for h in range(H)], axis=1)
        total = jnp.concatenate([sums[h * T:(h + 1) * T, T:] for h in range(H)], axis=1)
        att = jnp.exp(log_sig + suffix + box["carry"])
        if diagonal:
            att = jnp.where(causal, att, 0.0)
        box["carry"] = box["carry"] + total
        pv = _dot(att.astype(BF16), vbd_ref[j])
        yield
        box["acc"] = box["acc"] + pv

    def key_blocks(js, state, diagonal=False):
        box = {"carry": state[0], "acc": state[1]}
        for _ in itertools.zip_longest(*[key_block_stages(j, box, diagonal) for j in js]):
            pass
        return box["carry"], box["acc"]

    state = key_blocks([i], (jnp.zeros((T, H * T), F32), jnp.zeros((T, GW), F32)), True)
    rem = i % SB_UNROLL
    state = lax.fori_loop(0, rem, lambda it, st: key_blocks([i - 1 - it], st), state)
    top = i - 1 - rem
    _, y = lax.fori_loop(0, i // SB_UNROLL,
                         lambda it, st: key_blocks([top - SB_UNROLL * it - n for n in range(SB_UNROLL)], st), state)

    hi_ = lax.broadcasted_iota(jnp.int32, (GW, GW), 0) // HEAD_DIM
    hj_ = lax.broadcasted_iota(jnp.int32, (GW, GW), 1) // HEAD_DIM
    ones_bd = jnp.where(hi_ == hj_, 1.0, 0.0).astype(BF16)
    ms = _dot_f32_by_exact(y * y, ones_bd) * (1.0 / HEAD_DIM)
    o_ref[0] = y * lax.rsqrt(ms + NORM_EPS) * g_ref[...]


def _stick_breaking_norm(p, g):
    B, S, _ = p.shape
    GW = GROUP_W
    assert S % SB_BLOCK == 0 and HEAD_DIM ** -0.5 == 2.0 ** -(HEAD_DIM.bit_length() // 2)
    col = lambda n: pl.BlockSpec((1, SB_BLOCK, GW), lambda b, i: (b, i, n))
    stacked = pltpu.VMEM((S // SB_BLOCK, N_HEADS * SB_BLOCK, GW), BF16)
    return pl.pallas_call(
        _sb_kernel,
        grid=(B, S // SB_BLOCK),
        in_specs=[col(0), col(1), col(2), pl.BlockSpec((1, GW), lambda b, i: (0, 0))],
        out_specs=col(0),
        out_shape=jax.ShapeDtypeStruct((B, S, GW), F32),
        scratch_shapes=[stacked, stacked],
        compiler_params=pltpu.CompilerParams(dimension_semantics=("parallel", "arbitrary"),
                                             vmem_limit_bytes=VMEM_LIMIT),
        name="stick_breaking",
    )(p, p, p, g[None])


ML_HALO = 8
ML_GROUP = N_HEADS
ML_BATCH = 8


def _dot_nt_exact_by_f32(a_exact, b):
    return _dot_nt(jnp.concatenate([a_exact] * 3, axis=1), jnp.concatenate(_split3_bf16(b), axis=1))


def _mlstm_kernel(p_ref, cw_ref, cb_ref, gb_ref, g_ref, o_ref, ext_ref, ct_ref, n_ref, m_ref):
    @pl.when(pl.program_id(1) == 0)
    def _():
        ext_ref[...] = jnp.zeros_like(ext_ref)
        ct_ref[...] = jnp.zeros_like(ct_ref)
        n_ref[...] = jnp.zeros_like(n_ref)
        m_ref[...] = jnp.zeros_like(m_ref)

    masks = _mlstm_masks()
    chains = [_mlstm_chunk(masks, p_ref.at[n], cw_ref, cb_ref, gb_ref, g_ref, o_ref.at[n], ext_ref.at[n],
                           ct_ref.at[n], n_ref.at[n], m_ref.at[n]) for n in range(ML_BATCH)]
    for _ in itertools.zip_longest(*chains):
        pass


def _mlstm_masks():
    L, GW, H, SG = ML_CHUNK, GROUP_W, N_HEADS, ML_GROUP * ML_CHUNK
    gi = lax.broadcasted_iota(jnp.int32, (LANE, 2 * GW), 0)
    gj = lax.broadcasted_iota(jnp.int32, (LANE, 2 * GW), 1)
    expand = jnp.where(gi == (gj % GW) // HEAD_DIM + H * (gj // GW), 1.0, 0.0).astype(BF16)
    ti = lax.broadcasted_iota(jnp.int32, (L, L), 0)
    tj = lax.broadcasted_iota(jnp.int32, (L, L), 1)
    tri = jnp.where(tj <= ti, 1.0, 0.0).astype(BF16)
    ri = lax.broadcasted_iota(jnp.int32, (SG, SG), 0)
    ci = lax.broadcasted_iota(jnp.int32, (SG, SG), 1)
    same_head = (ri // L) == (ci // HEAD_DIM)
    first_lane = ci == (ri // L) * HEAD_DIM
    sel_first = jnp.where(first_lane, 1.0, 0.0).astype(BF16)
    causal = lax.broadcasted_iota(jnp.int32, (SG, L), 1) <= lax.broadcasted_iota(jnp.int32, (SG, L), 0) % L
    hi_ = lax.broadcasted_iota(jnp.int32, (GW, GW), 0) // HEAD_DIM
    hj_ = lax.broadcasted_iota(jnp.int32, (GW, GW), 1) // HEAD_DIM
    ones_bd = jnp.where(hi_ == hj_, 1.0, 0.0).astype(BF16)
    return expand, tri, same_head, first_lane, sel_first, causal, ones_bd


def _mlstm_chunk(masks, p_ref, cw_ref, cb_ref, gb_ref, g_ref, o_ref, ext_ref, ct_ref, n_ref, m_ref):
    L, GW, H = ML_CHUNK, GROUP_W, N_HEADS
    x = p_ref[...]
    ext_ref[ML_HALO:, :] = x[:, :2 * GW]
    conv = cb_ref[...]
    for j in range(ML_CONV):
        conv = conv + cw_ref[j:j + 1, :] * ext_ref[pl.ds(ML_HALO - (ML_CONV - 1) + j, L), :]
    ext_ref[:ML_HALO, :] = x[L - ML_HALO:, :2 * GW]
    qk = conv * jax.nn.sigmoid(conv)
    q, k = qk[:, :GW], qk[:, GW:] * HEAD_DIM ** -0.5
    v, o = x[:, 2 * GW:3 * GW], x[:, 3 * GW:4 * GW]

    gates = x[:, 4 * GW:]
    expand, tri, same_head, first_lane, sel_first, causal, ones_bd = masks
    graw = _dot_f32_by_exact(gates, expand) + gb_ref[...]
    yield
    capped = GATE_CAP * jnp.tanh(graw * (1.0 / GATE_CAP))
    log_i = capped[:, :GW]
    cf = capped[:, GW:]
    log_f = jnp.minimum(cf, 0.0) - jnp.log(1.0 + jnp.exp(-jnp.abs(cf)))

    bf = _dot_exact_by_f32(tri, log_f)
    yield
    b_last = bf[L - 1:L, :]
    m_row, n_row = m_ref[...], n_ref[...]
    dec = b_last - bf + log_i
    m_new = jnp.maximum(b_last + m_row, jnp.max(dec, axis=0, keepdims=True))
    kw = k * jnp.exp(dec - m_new)
    s_old = jnp.exp(b_last + m_row - m_new)
    n_ref[...] = s_old * n_row + jnp.sum(kw, axis=0, keepdims=True)
    m_ref[...] = m_new
    g_in = bf + m_row
    li_b = log_i - bf

    SG = ML_GROUP * L
    stack = lambda t: jnp.concatenate([t] * ML_GROUP, axis=0)
    pick = lambda t: jnp.sum(jnp.where(first_lane, t, 0.0), axis=1, keepdims=True)
    hs = []
    for g in range(H // ML_GROUP):
        cols = slice(g * SG, (g + 1) * SG)
        ct = ct_ref[g]
        qs = jnp.where(same_head, stack(q[:, cols]), 0.0)
        qs_b = qs.astype(BF16)
        v_b = v[:, cols].astype(BF16)
        b_col = pick(stack(bf[:, cols]))
        g_col = pick(stack(g_in[:, cols]))
        row_part = _dot_nt_exact_by_f32(sel_first, li_b[:, cols])
        qk = _dot_nt(qs_b, k[:, cols].astype(BF16))
        inter = _dot(qs_b, ct.astype(BF16))
        carried = _dot_tn(kw[:, cols].astype(BF16), v_b)
        yield
        dmat = jnp.where(causal, b_col + row_part, -jnp.inf)
        m_t = jnp.maximum(g_col, jnp.max(dmat, axis=1, keepdims=True))
        s_inter = jnp.exp(g_col - m_t)
        sqk = qk * jnp.exp(dmat - m_t)
        intra = _dot(sqk.astype(BF16), v_b)
        yield
        num = s_inter * inter + jnp.where(same_head, intra, 0.0)
        den = (s_inter * jnp.sum(qs * n_row[:, cols], axis=1, keepdims=True)
               + jnp.sum(sqk, axis=1, keepdims=True))
        hst = num / jnp.maximum(jnp.abs(den), jnp.exp(-m_t))
        hs.append(sum(hst[n * L:(n + 1) * L, :] for n in range(ML_GROUP)))
        ct_ref[g] = s_old[:, cols] * ct + jnp.where(same_head, carried, 0.0)
    h = jnp.concatenate(hs, axis=1)

    ms = _dot_f32_by_exact(h * h, ones_bd) * (1.0 / HEAD_DIM)
    yield
    o_ref[...] = jax.nn.sigmoid(o) * (h * lax.rsqrt(ms + NORM_EPS) * g_ref[...])


def _mlstm_mix(p, conv_w, conv_b, ig_b, fg_b, norm_g):
    B, S, _ = p.shape
    GW = GROUP_W
    assert S % ML_CHUNK == 0 and ML_CONV - 1 <= ML_HALO <= ML_CHUNK and B % ML_BATCH == 0
    gate_b = jnp.concatenate([jnp.repeat(ig_b, HEAD_DIM), jnp.repeat(fg_b, HEAD_DIM)])[None]
    full = lambda a: pl.BlockSpec(a.shape, lambda b, c: (0,) * a.ndim)
    cb2, g2 = conv_b[None], norm_g[None]
    return pl.pallas_call(
        _mlstm_kernel,
        grid=(B // ML_BATCH, S // ML_CHUNK),
        in_specs=[pl.BlockSpec((ML_BATCH, ML_CHUNK, C_PAD), lambda b, c: (b, c, 0)),
                  full(conv_w), full(cb2), full(gate_b), full(g2)],
        out_specs=pl.BlockSpec((ML_BATCH, ML_CHUNK, GW), lambda b, c: (b, c, 0)),
        out_shape=jax.ShapeDtypeStruct((B, S, GW), F32),
        scratch_shapes=[pltpu.VMEM((ML_BATCH, ML_HALO + ML_CHUNK, 2 * GW), F32),
                        pltpu.VMEM((ML_BATCH, N_HEADS // ML_GROUP, ML_GROUP * ML_CHUNK, ML_GROUP * HEAD_DIM), F32),
                        pltpu.VMEM((ML_BATCH, 1, GW), F32), pltpu.VMEM((ML_BATCH, 1, GW), F32)],
        compiler_params=pltpu.CompilerParams(dimension_semantics=("parallel", "arbitrary"),
                                             vmem_limit_bytes=VMEM_LIMIT),
        name="mlstm",
    )(p, conv_w, cb2, gate_b, g2)


def kernel(x, c, ada_w, ada_b, norm1_g, norm2_g, w_in, rk_mu, rk_w0, rk_w2, rk_a0, rk_a2, rk_g2, rk_kk, rk_ka, rk_rk, rk_ln_g, rk_ln_b, sb_norm_g, ml_conv_w, ml_conv_b, ml_ig_b, ml_fg_b, ml_norm_g, ds_qn_g, ds_kn_g, ds_out_g, w_out, moe_wg, moe_bg, moe_we, moe_be, moe_w1, moe_w3, moe_w2):
    B, S, D = x.shape
    depth = ada_w.shape[0]
    c_act = jax.nn.silu(c)
    for l in range(depth):
        mod = (c_act @ ada_w[l] + ada_b[l])[:, None, :]
        sh1, sc1, gt1, sh2, sc2, gt2 = jnp.split(mod, 6, axis=-1)

        pA, pB, pC, pD = _in_proj(x, sc1, sh1, norm1_g[l][None], _pad_w_in(w_in[l]))

        yA = _rwkv7_time_mix(pA, rk_mu[l], rk_w0[l], rk_w2[l], rk_a0[l], rk_a2[l], rk_g2[l],
                             rk_kk[l], rk_ka[l], rk_rk[l], rk_ln_g[l], rk_ln_b[l])

        yB = _stick_breaking_norm(pB, sb_norm_g[l])

        yC = _mlstm_mix(pC, ml_conv_w[l], ml_conv_b[l], ml_ig_b[l], ml_fg_b[l], ml_norm_g[l])

        yD = _dsa_attn_norm(*_dsa_prep(pD, ds_qn_g[l], ds_kn_g[l]), ds_out_g[l])

        router = jnp.pad(jnp.concatenate([moe_wg[l], moe_we[l]], 1),
                         ((0, 0), (0, ROUTER_PAD - N_GROUPS - N_EXPERTS)))
        r_hi, r_lo = _split_bf16(router)
        r_b = jnp.pad(jnp.concatenate([moe_bg[l], moe_be[l]]), (0, ROUTER_PAD - N_GROUPS - N_EXPERTS))[None]
        x1, h2, route = _out_proj((yA, yB, yC, yD), x, gt1, sc2, sh2, norm2_g[l][None],
                                  w_out[l].astype(BF16), r_hi, r_lo, r_b)

        y1, y2 = _hier_moe(h2.reshape(B * S, D), route.reshape(B * S, ROUTER_PAD), moe_w1, moe_w3, moe_w2, l)
        x = x1 + gt2 * (y1.astype(F32) + y2.astype(F32)).reshape(B, S, D)
    return x
```
